```python
import jax, jax.numpy as jnp
from jax import lax
import numpy as np

D_MODEL = 1024
BATCH = 32
SEQ = 256
DEPTH = 1
DEC_BATCH = 8
DEC_SEQ = 2048
PAST_LEN = 512

GRID_W = 64
ATTN_HEADS = 8
KV_HEADS = 2
HEAD_DIM = 64
ATTN_WIDTH = ATTN_HEADS * HEAD_DIM
KV_WIDTH = KV_HEADS * HEAD_DIM
ROPE_THETA = 10000.0
Q_BLOCK = 128
GLA_HEADS = 4
GLA_DK = 64
GLA_DV = 128
GLA_K_WIDTH = GLA_HEADS * GLA_DK
GLA_V_WIDTH = GLA_HEADS * GLA_DV
GLA_GATE_RANK = 16
GLA_TAU = 16.0
GLA_CHUNK = 64
N_BRANCHES = 2
IN_WIDTHS = (ATTN_WIDTH, KV_WIDTH, KV_WIDTH, GLA_K_WIDTH, GLA_K_WIDTH, GLA_V_WIDTH, GLA_V_WIDTH,
             2 * GLA_GATE_RANK, N_BRANCHES * D_MODEL)
IN_TOTAL = (ATTN_WIDTH + 2 * KV_WIDTH + 2 * GLA_K_WIDTH + 2 * GLA_V_WIDTH
            + 2 * GLA_GATE_RANK + N_BRANCHES * D_MODEL)
N_GROUPS = 4
EXPERTS_PER_GROUP = 8
N_EXPERTS = N_GROUPS * EXPERTS_PER_GROUP
D_EXPERT = 256
TOP_K_INNER = 2
DEEPNORM_ALPHA = (2.0 * DEPTH) ** 0.25
DEEPNORM_BETA = (8.0 * DEPTH) ** -0.25
LN_EPS = 1e-6
RMS_EPS = 1e-6

kernel_name = "hybrid_gqa_gla_hmoe_diffusion_step"


def _ln_f32(x):
    xf = x.astype(jnp.float32)
    mu = jnp.mean(xf, axis=-1, keepdims=True)
    var = jnp.mean(jnp.square(xf - mu), axis=-1, keepdims=True)
    return (xf - mu) * lax.rsqrt(var + LN_EPS)


def ln_plain(x):
    return _ln_f32(x).astype(x.dtype)


def ln_affine(x, g, b):
    return (_ln_f32(x) * g.astype(jnp.float32) + b.astype(jnp.float32)).astype(x.dtype)


def rms_norm(x, g):
    xf = x.astype(jnp.float32)
    y = xf * lax.rsqrt(jnp.mean(jnp.square(xf), axis=-1, keepdims=True) + RMS_EPS)
    return (y * g.astype(jnp.float32)).astype(x.dtype)


def rope_1d(x, pos):
    half = x.shape[-1] // 2
    inv = ROPE_THETA ** (-jnp.arange(half, dtype=jnp.float32) / half)
    ang = pos[:, None] * inv[None, :]
    cos = jnp.cos(ang)[None, :, None, :]
    sin = jnp.sin(ang)[None, :, None, :]
    x1, x2 = x[..., :half], x[..., half:]
    return jnp.concatenate([x1 * cos - x2 * sin, x2 * cos + x1 * sin], axis=-1)


def axial_rope(x):
    n = x.shape[1]
    rows = n // GRID_W
    row = jnp.repeat(jnp.arange(rows, dtype=jnp.float32), GRID_W)
    col = jnp.tile(jnp.arange(GRID_W, dtype=jnp.float32), rows)
    half = HEAD_DIM // 2
    xf = x.astype(jnp.float32)
    out = jnp.concatenate([rope_1d(xf[..., :half], row), rope_1d(xf[..., half:], col)], axis=-1)
    return out.astype(x.dtype)


def block_attention(q, k, v):
    b, s, h, hd = q.shape
    kv = k.shape[2]
    g = h // kv
    nb = s // Q_BLOCK
    kf = k.astype(jnp.float32)
    vf = v.astype(jnp.float32)
    qb = q.astype(jnp.float32).reshape(b, nb, Q_BLOCK, kv, g, hd).transpose(1, 0, 2, 3, 4, 5)
    scale = hd ** -0.5

    def one_block(qblk):
        sc = jnp.einsum('bqkgd,btkd->bkgqt', qblk, kf) * scale
        p = jax.nn.softmax(sc, axis=-1)
        return jnp.einsum('bkgqt,btkd->bqkgd', p, vf)

    ob = lax.map(one_block, qb)
    return ob.transpose(1, 0, 2, 3, 4, 5).reshape(b, s, h * hd).astype(q.dtype)


def gla_chunked(q, k, v, logg, s0):
    b, n, h, dk = q.shape
    dv = v.shape[-1]
    nc = n // GLA_CHUNK

    def chunks(t):
        return t.reshape(b, nc, GLA_CHUNK, h, t.shape[-1]).transpose(1, 0, 3, 2, 4)

    mask = jnp.tril(jnp.ones((GLA_CHUNK, GLA_CHUNK), dtype=bool))

    def step(state, inp):
        qc, kc, vc, gc = inp
        cum = jnp.cumsum(gc, axis=2)
        q_t = qc * jnp.exp(cum)
        k_t = kc * jnp.exp(-cum)
        a = jnp.where(mask, jnp.einsum('bhcd,bhsd->bhcs', q_t, k_t), 0.0)
        o = jnp.einsum('bhcd,bhde->bhce', q_t, state) + jnp.einsum('bhcs,bhse->bhce', a, vc)
        last = cum[:, :, -1:, :]
        new_state = (jnp.exp(last[:, :, 0, :])[..., None] * state
                     + jnp.einsum('bhsd,bhse->bhde', kc * jnp.exp(last - cum), vc))
        return new_state, o

    s_fin, o = lax.scan(step, s0, (chunks(q), chunks(k), chunks(v), chunks(logg)))
    o = o.transpose(1, 0, 3, 2, 4).reshape(b, n, h, dv)
    return o, s_fin


def hier_moe(h, lp):
    b, n, d = h.shape
    t = h.reshape(b * n, d)
    lg = (t @ lp['router_group_w'] + lp['router_group_b']).astype(jnp.float32)
    pg = jax.nn.softmax(lg, axis=-1)
    pg_top, g_idx = lax.top_k(pg, 1)
    le = (t @ lp['router_expert_w'] + lp['router_expert_b']).astype(jnp.float32)
    le = le.reshape(b * n, N_GROUPS, EXPERTS_PER_GROUP)
    le_sel = jnp.einsum('tg,tge->te', jax.nn.one_hot(g_idx[:, 0], N_GROUPS, dtype=jnp.float32), le)
    top_v, top_i = lax.top_k(le_sel, TOP_K_INNER)
    w = jax.nn.softmax(top_v, axis=-1) * pg_top
    e_idx = g_idx * EXPERTS_PER_GROUP + top_i
    comb = jnp.einsum('tk,tke->te', w, jax.nn.one_hot(e_idx, N_EXPERTS, dtype=jnp.float32)).astype(t.dtype)
    y = jnp.zeros_like(t)
    for e in range(N_EXPERTS):
        hid = jax.nn.silu(t @ lp['exp_w_gate'][e]) * (t @ lp['exp_w_up'][e])
        y = y + comb[:, e:e + 1] * (hid @ lp['exp_w_down'][e])
    return y.reshape(b, n, d)


def trunk_layer(x, mod, lp, latent, ctx_k, ctx_v, s0_fwd, s0_bwd):
    b, n, _ = x.shape
    f32 = jnp.float32
    sh1, sc1, g1, sh2, sc2, g2 = jnp.split(mod, 6, axis=-1)
    h = ln_plain(x) * (1 + sc1) + sh1
    offs = [int(o) for o in np.cumsum(IN_WIDTHS)[:-1]]
    q_a, k_a, v_a, q_g, k_g, v_g, r_g, lr, mg = jnp.split(h @ lp['w_in'], offs, axis=-1)

    q_a = rms_norm(q_a.reshape(b, n, ATTN_HEADS, HEAD_DIM), lp['q_norm'])
    k_a = rms_norm(k_a.reshape(b, n, KV_HEADS, HEAD_DIM), lp['k_norm'])
    v_a = v_a.reshape(b, n, KV_HEADS, HEAD_DIM)
    if latent:
        keys = jnp.concatenate([axial_rope(k_a), ctx_k.astype(k_a.dtype)], axis=1)
        vals = jnp.concatenate([v_a, ctx_v.astype(v_a.dtype)], axis=1)
        q_a = axial_rope(q_a)
    else:
        keys, vals = k_a, v_a
    attn = block_attention(q_a, keys, vals)

    qg = q_g.reshape(b, n, GLA_HEADS, GLA_DK).astype(f32) * (GLA_DK ** -0.5)
    kg = k_g.reshape(b, n, GLA_HEADS, GLA_DK).astype(f32)
    vg = v_g.reshape(b, n, GLA_HEADS, GLA_DV).astype(f32)
    z = jnp.einsum('bnjr,jrd->bnjd', lr.reshape(b, n, 2, GLA_GATE_RANK), lp['gla_w_gate']) + lp['gla_b_gate']
    logg = (jax.nn.log_sigmoid(z.astype(f32)) / GLA_TAU).reshape(b, n, 2, GLA_HEADS, GLA_DK)
    o_f, s_f = gla_chunked(qg, kg, vg, logg[:, :, 0], s0_fwd.astype(f32))
    o_b, s_b = gla_chunked(jnp.flip(qg, axis=1), jnp.flip(kg, axis=1), jnp.flip(vg, axis=1),
                           jnp.flip(logg[:, :, 1], axis=1), s0_bwd.astype(f32))
    o = rms_norm(o_f + jnp.flip(o_b, axis=1), lp['gla_norm']).astype(h.dtype).reshape(b, n, GLA_V_WIDTH)
    gla = o * jax.nn.silu(r_g)

    gates = jax.nn.sigmoid(mg.reshape(b, n, N_BRANCHES, D_MODEL))
    merged = gates[:, :, 0] * (attn @ lp['w_br_attn']) + gates[:, :, 1] * (gla @ lp['w_br_gla'])
    x = ln_affine(DEEPNORM_ALPHA * x + g1 * (merged @ lp['w_out']), lp['ln1_g'], lp['ln1_b'])

    h2 = ln_plain(x) * (1 + sc2) + sh2
    x = ln_affine(DEEPNORM_ALPHA * x + g2 * hier_moe(h2, lp), lp['ln2_g'], lp['ln2_b'])
    return x, k_a, v_a, s_f.astype(x.dtype), s_b.astype(x.dtype)


def setup_inputs(seed: int = 0) -> dict:
    key = jax.random.key(seed)
    ks = jax.random.split(key, 32)

    def nrm(k, shape, scale):
        return jax.random.normal(k, shape, jnp.float32) * scale

    D = D_MODEL
    return {
        'x_prompt': nrm(ks[0], (BATCH, SEQ, D), 1.0),
        'x_sample': nrm(ks[1], (DEC_BATCH, DEC_SEQ, D), 1.0),
        'cache_k': nrm(ks[2], (DEC_BATCH, DEPTH, PAST_LEN, KV_HEADS, HEAD_DIM), 1.0),
        'cache_v': nrm(ks[3], (DEC_BATCH, DEPTH, PAST_LEN, KV_HEADS, HEAD_DIM), 1.0),
        'state_gla_fwd': nrm(ks[4], (DEC_BATCH, DEPTH, GLA_HEADS, GLA_DK, GLA_DV), 1.0),
        'state_gla_bwd': nrm(ks[5], (DEC_BATCH, DEPTH, GLA_HEADS, GLA_DK, GLA_DV), 1.0),
        'c': nrm(ks[6], (DEC_BATCH, D), 1.0),
        'c_ctx': nrm(ks[7], (D,), 1.0),
        'w_ada': nrm(ks[8], (DEPTH, D, 6 * D), 0.5 * D ** -0.5),
        'b_ada': nrm(ks[9], (DEPTH, 6 * D), 0.02),
        'w_in': nrm(ks[10], (DEPTH, D, IN_TOTAL), D ** -0.5),
        'q_norm': 1.0 + nrm(ks[11], (DEPTH, HEAD_DIM), 0.02),
        'k_norm': 1.0 + nrm(ks[12], (DEPTH, HEAD_DIM), 0.02),
        'gla_w_gate': nrm(ks[13], (DEPTH, 2, GLA_GATE_RANK, GLA_K_WIDTH), GLA_GATE_RANK ** -0.5),
        'gla_b_gate': nrm(ks[14], (DEPTH, 2, GLA_K_WIDTH), 0.1),
        'gla_norm': 1.0 + nrm(ks[15], (DEPTH, GLA_DV), 0.02),
        'w_br_attn': nrm(ks[16], (DEPTH, ATTN_WIDTH, D), ATTN_WIDTH ** -0.5),
        'w_br_gla': nrm(ks[17], (DEPTH, GLA_V_WIDTH, D), GLA_V_WIDTH ** -0.5),
        'w_out': nrm(ks[18], (DEPTH, D, D), DEEPNORM_BETA * D ** -0.5),
        'ln1_g': 1.0 + nrm(ks[19], (DEPTH, D), 0.02),
        'ln1_b': nrm(ks[20], (DEPTH, D), 0.02),
        'router_group_w': nrm(ks[21], (DEPTH, D, N_GROUPS), D ** -0.5),
        'router_group_b': nrm(ks[22], (DEPTH, N_GROUPS), 0.01),
        'router_expert_w': nrm(ks[23], (DEPTH, D, N_EXPERTS), D ** -0.5),
        'router_expert_b': nrm(ks[24], (DEPTH, N_EXPERTS), 0.01),
        'exp_w_gate': nrm(ks[25], (DEPTH, N_EXPERTS, D, D_EXPERT), D ** -0.5),
        'exp_w_up': nrm(ks[26], (DEPTH, N_EXPERTS, D, D_EXPERT), D ** -0.5),
        'exp_w_down': nrm(ks[27], (DEPTH, N_EXPERTS, D_EXPERT, D), DEEPNORM_BETA * D_EXPERT ** -0.5),
        'ln2_g': 1.0 + nrm(ks[28], (DEPTH, D), 0.02),
        'ln2_b': nrm(ks[29], (DEPTH, D), 0.02),
    }


def reference(x_prompt, x_sample, cache_k, cache_v, state_gla_fwd, state_gla_bwd, c, c_ctx,
              w_ada, b_ada, w_in, q_norm, k_norm, gla_w_gate, gla_b_gate, gla_norm,
              w_br_attn, w_br_gla, w_out, ln1_g, ln1_b, router_group_w, router_group_b,
              router_expert_w, router_expert_b, exp_w_gate, exp_w_up, exp_w_down, ln2_g, ln2_b):
    b_ctx = x_prompt.shape[0]
    zero_state = jnp.zeros((b_ctx, GLA_HEADS, GLA_DK, GLA_DV), jnp.float32)
    x_ctx = x_prompt
    x_lat = x_sample
    ks_new, vs_new, sf_new, sb_new = [], [], [], []
    for l in range(DEPTH):
        lp = {
            'w_in': w_in[l], 'q_norm': q_norm[l], 'k_norm': k_norm[l],
            'gla_w_gate': gla_w_gate[l], 'gla_b_gate': gla_b_gate[l], 'gla_norm': gla_norm[l],
            'w_br_attn': w_br_attn[l], 'w_br_gla': w_br_gla[l], 'w_out': w_out[l],
            'ln1_g': ln1_g[l], 'ln1_b': ln1_b[l],
            'router_group_w': router_group_w[l], 'router_group_b': router_group_b[l],
            'router_expert_w': router_expert_w[l], 'router_expert_b': router_expert_b[l],
            'exp_w_gate': exp_w_gate[l], 'exp_w_up': exp_w_up[l], 'exp_w_down': exp_w_down[l],
            'ln2_g': ln2_g[l], 'ln2_b': ln2_b[l],
        }
        mod_ctx = (jax.nn.silu(c_ctx) @ w_ada[l] + b_ada[l])[None, None, :]
        x_ctx, k_l, v_l, sf_l, sb_l = trunk_layer(x_ctx, mod_ctx, lp, False, None, None, zero_state, zero_state)
        ks_new.append(k_l)
        vs_new.append(v_l)
        sf_new.append(sf_l)
        sb_new.append(sb_l)
        mod_lat = (jax.nn.silu(c) @ w_ada[l] + b_ada[l])[:, None, :]
        x_lat, _, _, _, _ = trunk_layer(x_lat, mod_lat, lp, True, cache_k[:, l], cache_v[:, l],
                                        state_gla_fwd[:, l], state_gla_bwd[:, l])
    new_cache_k = jnp.stack(ks_new, axis=1)
    new_cache_v = jnp.stack(vs_new, axis=1)
    new_state_gla_fwd = jnp.stack(sf_new, axis=1)
    new_state_gla_bwd = jnp.stack(sb_new, axis=1)
    return (x_ctx, x_lat, new_cache_k, new_cache_v, new_state_gla_fwd, new_state_gla_bwd)
```

```python
import functools

import numpy as np
import jax
import jax.numpy as jnp
from jax import lax
from jax.experimental import pallas as pl
from jax.experimental.pallas import tpu as pltpu

F32 = jnp.float32
BF16 = jnp.bfloat16
U32 = jnp.uint32
HIGHEST = lax.Precision.HIGHEST

D = 1024
GRID_W = 64
HD = 64
N_Q_HEADS = 8
N_KV_HEADS = 2
AW = N_Q_HEADS * HD
KVW = N_KV_HEADS * HD
ROPE_THETA = 10000.0
GLA_H = 4
GLA_DK = 64
GLA_DV = 128
GKW = GLA_H * GLA_DK
GVW = GLA_H * GLA_DV
GATE_RANK = 16
GLA_TAU = 16.0
CHUNK = 64
N_GROUPS = 4
EPG = 8
N_EXP = N_GROUPS * EPG
D_EXP = 256
DEPTH = 1
ALPHA = (2.0 * DEPTH) ** 0.25
LN_EPS = 1e-6
RMS_EPS = 1e-6

LANES = 128
A_WIDTH = AW + 2 * KVW + 2 * GKW + 2 * GVW + 2 * GATE_RANK
MG_WIDTH = 2 * D
TM_TOK = 512
TQ_LAT = 128
TM_EXP = 256
TS_ROWS = 256
VMEM_LIMIT = 56 * 1024 * 1024


def _cparams(sem):
    return pltpu.CompilerParams(dimension_semantics=sem, vmem_limit_bytes=VMEM_LIMIT)


def _dot(a, b):
    return jnp.dot(a, b, preferred_element_type=F32)


def _dot_nt(a, b):
    return lax.dot_general(a, b, (((1,), (1,)), ((), ())), preferred_element_type=F32)


def _dot_tn(a, b):
    return lax.dot_general(a, b, (((0,), (0,)), ((), ())), preferred_element_type=F32)


def _ln(x):
    mu = jnp.mean(x, axis=-1, keepdims=True)
    xc = x - mu
    var = jnp.mean(xc * xc, axis=-1, keepdims=True)
    return xc * lax.rsqrt(var + LN_EPS)


def _silu(x):
    return x * jax.nn.sigmoid(x)


def _split_bf16(x):
    hi = x.astype(BF16)
    lo = (x - hi.astype(F32)).astype(BF16)
    return hi, lo


def _ada_kernel(c_ref, w_ref, b_ref, o_ref):
    s = _silu(c_ref[...])
    o_ref[...] = jnp.dot(s, w_ref[...], preferred_element_type=F32, precision=HIGHEST) + b_ref[...]


def _ada(c_rows, w_ada, b_ada):
    rows = c_rows.shape[0]
    n = w_ada.shape[1]
    bn = 1024
    return pl.pallas_call(
        _ada_kernel,
        grid=(n // bn,),
        in_specs=[pl.BlockSpec((rows, D), lambda j: (0, 0)),
                  pl.BlockSpec((D, bn), lambda j: (0, j)),
                  pl.BlockSpec((1, bn), lambda j: (0, j))],
        out_specs=pl.BlockSpec((rows, bn), lambda j: (0, j)),
        out_shape=jax.ShapeDtypeStruct((rows, n), F32),
        compiler_params=_cparams(("arbitrary",)),
        name="ada",
    )(c_rows, w_ada, b_ada)


def _inproj_kernel(*refs, latent):
    if latent:
        (x_ref, mod_ref, w_ref, gain_ref, ind_ref, cos_ref, sin_ref,
         q_ref, k_ref, v_ref, qg_ref, kg_ref, vg_ref, rs_ref, lr_ref) = refs
    else:
        (x_ref, mod_ref, w_ref, gain_ref, ind_ref,
         q_ref, k_ref, v_ref, qg_ref, kg_ref, vg_ref, rs_ref, lr_ref, kf_ref, vf_ref) = refs
    tm = x_ref.shape[0]
    sh1 = mod_ref[0, :, 0:D]
    sc1 = mod_ref[0, :, D:2 * D]
    h = _ln(x_ref[...]) * (1.0 + sc1) + sh1
    res = _dot(h.astype(BF16), w_ref[...])

    qk = res[:, 0:AW + KVW]
    hi, lo = _split_bf16(qk * qk)
    ms = _dot(hi, ind_ref[...]) + _dot(lo, ind_ref[...])
    r = lax.rsqrt(ms + RMS_EPS)
    lane = lax.broadcasted_iota(jnp.int32, (tm, LANES), 1)
    low_half = lane < HD
    if latent:
        cos = cos_ref[...]
        sin = sin_ref[...]
        first = (lane % 32) < 16
    for s in range(5):
        rb = jnp.where(low_half, r[:, 2 * s:2 * s + 1], r[:, 2 * s + 1:2 * s + 2])
        y = qk[:, LANES * s:LANES * (s + 1)] * rb * gain_ref[:, LANES * s:LANES * (s + 1)]
        if s == 4 and not latent:
            kf_ref[...] = y
        if latent:
            partner = jnp.where(first, pltpu.roll(y, LANES - 16, 1), pltpu.roll(y, 16, 1))
            y = y * cos + partner * sin
        if s < 4:
            q_ref[:, LANES * s:LANES * (s + 1)] = (y * (HD ** -0.5)).astype(BF16)
        else:
            k_ref[...] = y.astype(BF16)
    o = AW + KVW
    v = res[:, o:o + KVW]
    v_ref[...] = v.astype(BF16)
    if not latent:
        vf_ref[...] = v
    o += KVW
    qg_ref[...] = res[:, o:o + GKW] * (GLA_DK ** -0.5)
    o += GKW
    kg_ref[...] = res[:, o:o + GKW]
    o += GKW
    vg_ref[...] = res[:, o:o + GVW].astype(BF16)
    o += GVW
    rs_ref[...] = _silu(res[:, o:o + GVW]).astype(BF16)
    o += GVW
    lr_ref[...] = res[:, o:o + 2 * GATE_RANK]


def _inproj(x2, mod, w_a, gain, ind, rope, seq, latent):
    t = x2.shape[0]
    tm = min(TM_TOK, seq)
    per_seq = seq // tm
    row = lambda i: (i, 0)
    const = lambda i: (0, 0)
    in_specs = [pl.BlockSpec((tm, D), row),
                pl.BlockSpec((1, 1, 6 * D), (lambda i: (i // per_seq, 0, 0)) if latent else (lambda i: (0, 0, 0))),
                pl.BlockSpec((D, A_WIDTH), const),
                pl.BlockSpec((1, AW + KVW), const),
                pl.BlockSpec((AW + KVW, LANES), const)]
    args = [x2, mod, w_a, gain, ind]
    if latent:
        in_specs += [pl.BlockSpec((tm, LANES), lambda i: (i % per_seq, 0))] * 2
        args += list(rope)
    widths = [(AW, BF16), (KVW, BF16), (KVW, BF16), (GKW, F32), (GKW, F32), (GVW, BF16), (GVW, BF16),
              (2 * GATE_RANK, F32)]
    if not latent:
        widths += [(KVW, F32), (KVW, F32)]
    return pl.pallas_call(
        functools.partial(_inproj_kernel, latent=latent),
        grid=(t // tm,),
        in_specs=in_specs,
        out_specs=[pl.BlockSpec((tm, w), row) for w, _ in widths],
        out_shape=[jax.ShapeDtypeStruct((t, w), dt) for w, dt in widths],
        compiler_params=_cparams(("parallel",)),
        name="inproj_lat" if latent else "inproj_ctx",
    )(*args)


def _attn_kernel(*refs, has_cache):
    if has_cache:
        q_ref, k_ref, v_ref, kc_ref, vc_ref, o_ref = refs
    else:
        q_ref, k_ref, v_ref, o_ref = refs
    tq = q_ref.shape[0]
    lane = lax.broadcasted_iota(jnp.int32, (tq, LANES), 1)
    low_half = lane < HD
    k = k_ref[...]
    v = v_ref[...]
    outs = []
    for j in range(N_KV_HEADS):
        keep = low_half if j == 0 else jnp.logical_not(low_half)
        zero = jnp.zeros((tq, LANES), BF16)
        qs = jnp.concatenate(
            [jnp.where(keep, q_ref[:, LANES * s:LANES * (s + 1)], zero) for s in range(4)], axis=0)
        s1 = _dot_nt(qs, k)
        m = jnp.max(s1, axis=-1, keepdims=True)
        if has_cache:
            s2 = _dot_nt(qs, kc_ref[...])
            m = jnp.maximum(m, jnp.max(s2, axis=-1, keepdims=True))
        p1 = jnp.exp(s1 - m)
        den = jnp.sum(p1, axis=-1, keepdims=True)
        acc = _dot(p1.astype(BF16), v)
        if has_cache:
            p2 = jnp.exp(s2 - m)
            den = den + jnp.sum(p2, axis=-1, keepdims=True)
            acc = acc + _dot(p2.astype(BF16), vc_ref[...])
        outs.append(acc / den)
    for s in range(4):
        o_ref[:, LANES * s:LANES * (s + 1)] = jnp.where(
            low_half, outs[0][s * tq:(s + 1) * tq], outs[1][s * tq:(s + 1) * tq]).astype(BF16)


def _attention(q, k, v, cache, seq):
    t = q.shape[0]
    if cache is None:
        tq = seq
        grid = (t // seq,)
        qmap = lambda b: (b, 0)
        in_specs = [pl.BlockSpec((tq, AW), qmap), pl.BlockSpec((seq, KVW), qmap),
                    pl.BlockSpec((seq, KVW), qmap)]
        args = [q, k, v]
        sem = ("parallel",)
        name = "attn_ctx"
    else:
        tq = TQ_LAT
        nq = seq // tq
        kc, vc = cache
        past = kc.shape[1]
        grid = (t // seq, nq)
        qmap = lambda b, i: (b * nq + i, 0)
        kmap = lambda b, i: (b, 0)
        cmap = lambda b, i: (b, 0, 0)
        in_specs = [pl.BlockSpec((tq, AW), qmap), pl.BlockSpec((seq, KVW), kmap),
                    pl.BlockSpec((seq, KVW), kmap),
                    pl.BlockSpec((None, past, KVW), cmap), pl.BlockSpec((None, past, KVW), cmap)]
        args = [q, k, v, kc, vc]
        sem = ("parallel", "arbitrary")
        name = "attn_lat"
    return pl.pallas_call(
        functools.partial(_attn_kernel, has_cache=cache is not None),
        grid=grid,
        in_specs=in_specs,
        out_specs=pl.BlockSpec((tq, AW), qmap),
        out_shape=jax.ShapeDtypeStruct((t, AW), BF16),
        compiler_params=_cparams(sem),
        name=name,
    )(*args)


def _gla_kernel(qg_ref, kg_ref, vg_ref, lr_ref, rs_ref, s0_ref, wg_ref, bg_ref, gn_ref,
                o_ref, sfin_ref, cum_s, kv_s, dec_s, acc_s):
    n = qg_ref.shape[0]
    nc = n // CHUNK
    lane = lax.broadcasted_iota(jnp.int32, (CHUNK, LANES), 1)
    low_half = lane < GLA_DK
    lane_sq = lax.broadcasted_iota(jnp.int32, (LANES, LANES), 1)
    low_half_sq = lane_sq < GLA_DK
    ri = lax.broadcasted_iota(jnp.int32, (CHUNK, CHUNK), 0)
    ci = lax.broadcasted_iota(jnp.int32, (CHUNK, CHUNK), 1)
    keep = (ri >= ci, ci >= ri)
    tri = tuple(m.astype(F32) for m in keep)

    lr = lr_ref[...]
    for d in range(2):
        z = jnp.dot(lr, wg_ref[d], preferred_element_type=F32, precision=HIGHEST) + bg_ref[d]
        cum_s[d] = (jnp.minimum(z, 0.0) - jnp.log1p(jnp.exp(-jnp.abs(z)))) * (1.0 / GLA_TAU)

    def pass1(c, carry):
        rows = pl.ds(pl.multiple_of(c * CHUNK, CHUNK), CHUNK)
        kc = kg_ref[rows, :]
        for d in range(2):
            cum = jnp.dot(tri[d], cum_s[d, rows, :], preferred_element_type=F32, precision=HIGHEST)
            cum_s[d, rows, :] = cum
            last = cum[CHUNK - 1:CHUNK, :] if d == 0 else cum[0:1, :]
            dec_s[d, c] = jnp.exp(last)
            kdec = (kc * jnp.exp(last - cum)).astype(BF16)
            for p in range(2):
                slab = kdec[:, LANES * p:LANES * (p + 1)]
                va = vg_ref[rows, GLA_DV * (2 * p):GLA_DV * (2 * p + 1)]
                vb = vg_ref[rows, GLA_DV * (2 * p + 1):GLA_DV * (2 * p + 2)]
                kv_s[d, c, p] = jnp.where(low_half_sq, _dot_tn(va, slab), _dot_tn(vb, slab))
        return carry

    lax.fori_loop(0, nc, pass1, 0)

    for d in range(2):
        def scan(i, st):
            c = i if d == 0 else nc - 1 - i
            dec = dec_s[d, c]
            new = []
            for p in range(2):
                kv = kv_s[d, c, p]
                kv_s[d, c, p] = st[p]
                new.append(st[p] * dec[:, LANES * p:LANES * (p + 1)] + kv)
            return tuple(new)

        fin = lax.fori_loop(0, nc, scan, (s0_ref[d, 0], s0_ref[d, 1]))
        sfin_ref[d, 0] = fin[0]
        sfin_ref[d, 1] = fin[1]

    for d in range(2):
        def pass3(c, carry):
            rows = pl.ds(pl.multiple_of(c * CHUNK, CHUNK), CHUNK)
            cum = cum_s[d, rows, :]
            qt = qg_ref[rows, :] * jnp.exp(cum)
            kt = (kg_ref[rows, :] * jnp.exp(-cum)).astype(BF16)
            for h in range(GLA_H):
                p = h // 2
                sel = low_half if h % 2 == 0 else jnp.logical_not(low_half)
                qh = jnp.where(sel, qt[:, LANES * p:LANES * (p + 1)], 0.0).astype(BF16)
                a = jnp.where(keep[d], _dot_nt(qh, kt[:, LANES * p:LANES * (p + 1)]), 0.0)
                vh = vg_ref[rows, GLA_DV * h:GLA_DV * (h + 1)]
                oh = _dot(a.astype(BF16), vh) + _dot_nt(qh, kv_s[d, c, p].astype(BF16))
                cols = slice(GLA_DV * h, GLA_DV * (h + 1))
                if d == 0:
                    acc_s[rows, cols] = oh
                else:
                    tot = acc_s[rows, cols] + oh
                    y = tot * lax.rsqrt(jnp.mean(tot * tot, axis=-1, keepdims=True) + RMS_EPS)
                    y = (y * gn_ref[...]).astype(BF16)
                    o_ref[rows, cols] = (y * rs_ref[rows, cols]).astype(BF16)
            return carry

        lax.fori_loop(0, nc, pass3, 0)


def _gla(qg, kg, vg, lr, rs, s0, wg, bg, gn, seq):
    t = qg.shape[0]
    nb = t // seq
    nc = seq // CHUNK
    row = lambda b: (b, 0)
    c3 = lambda b: (0, 0, 0)
    st = lambda b: (b, 0, 0, 0, 0)
    if s0.shape[0] == 1:
        s0map = lambda b: (0, 0, 0, 0, 0)
    else:
        s0map = st
    return pl.pallas_call(
        _gla_kernel,
        grid=(nb,),
        in_specs=[pl.BlockSpec((seq, GKW), row), pl.BlockSpec((seq, GKW), row),
                  pl.BlockSpec((seq, GVW), row), pl.BlockSpec((seq, 2 * GATE_RANK), row),
                  pl.BlockSpec((seq, GVW), row),
                  pl.BlockSpec((None, 2, 2, LANES, LANES), s0map),
                  pl.BlockSpec((2, 2 * GATE_RANK, GKW), c3), pl.BlockSpec((2, 1, GKW), c3),
                  pl.BlockSpec((1, GLA_DV), lambda b: (0, 0))],
        out_specs=[pl.BlockSpec((seq, GVW), row),
                   pl.BlockSpec((None, 2, 2, LANES, LANES), st)],
        out_shape=[jax.ShapeDtypeStruct((t, GVW), BF16),
                   jax.ShapeDtypeStruct((nb, 2, 2, LANES, LANES), F32)],
        scratch_shapes=[pltpu.VMEM((2, seq, GKW), F32),
                        pltpu.VMEM((2, nc, 2, LANES, LANES), F32),
                        pltpu.VMEM((2, nc, 1, GKW), F32),
                        pltpu.VMEM((seq, GVW), F32)],
        compiler_params=_cparams(("parallel",)),
        name="gla_lat" if seq > 256 else "gla_ctx",
    )(qg, kg, vg, lr, rs, s0, wg, bg, gn)


def _pack_pairs(x):
    half = x.shape[1] // 2
    lo = pltpu.bitcast(x[:, :half].astype(BF16).astype(F32), U32)
    hi = pltpu.bitcast(x[:, half:].astype(BF16).astype(F32), U32)
    return (hi & jnp.uint32(0xFFFF0000)) | (lo >> 16)


def _unpack_pairs(w):
    lo = pltpu.bitcast(w << 16, F32)
    hi = pltpu.bitcast(w & jnp.uint32(0xFFFF0000), F32)
    return lo, hi


def _post_kernel(xc_ref, xl_ref, ac_ref, al_ref, gc_ref, gl_ref, mod_ref,
                 wmg_ref, wba_ref, wbg_ref, wo_ref, l1g_ref, l1b_ref, wr_ref, br_ref,
                 x1_ref, h2_ref, rt_ref, cnt_ref, run_s, *, n_ctx_tiles):
    i = pl.program_id(0)
    tm = xc_ref.shape[0]
    is_ctx = i < n_ctx_tiles
    x = jnp.where(is_ctx, xc_ref[...], xl_ref[...])
    attn = jnp.where(is_ctx, ac_ref[...], al_ref[...])
    gla = jnp.where(is_ctx, gc_ref[...], gl_ref[...])
    sh1 = mod_ref[0, :, 0:D]
    sc1 = mod_ref[0, :, D:2 * D]
    g1 = mod_ref[0, :, 2 * D:3 * D]
    sh2 = mod_ref[0, :, 3 * D:4 * D]
    sc2 = mod_ref[0, :, 4 * D:5 * D]
    h = (_ln(x) * (1.0 + sc1) + sh1).astype(BF16)
    gates = jax.nn.sigmoid(_dot(h, wmg_ref[...]))
    merged = gates[:, :D] * _dot(attn, wba_ref[...]) + gates[:, D:] * _dot(gla, wbg_ref[...])
    mix = _dot(merged.astype(BF16), wo_ref[...])
    x1 = _ln(ALPHA * x + g1 * mix) * l1g_ref[...] + l1b_ref[...]
    x1_ref[...] = x1
    h2 = _ln(x1) * (1.0 + sc2) + sh2
    h2_ref[...] = _pack_pairs(h2)

    logit = _dot(h2.astype(BF16), wr_ref[...]) + br_ref[...]
    lane_i = lax.broadcasted_iota(jnp.int32, (tm, LANES), 1)
    lane = lane_i.astype(F32)
    lane_grp = ((lane_i - N_GROUPS) >> 3).astype(F32)
    neg = jnp.float32(-jnp.inf)
    far = jnp.float32(LANES)
    is_g = lane_i < N_GROUPS
    lg = jnp.where(is_g, logit, neg)
    mg = jnp.max(lg, axis=-1, keepdims=True)
    pg_top = 1.0 / jnp.sum(jnp.where(is_g, jnp.exp(logit - mg), 0.0), axis=-1, keepdims=True)
    g_idx = jnp.min(jnp.where(lg == mg, lane, far), axis=-1, keepdims=True)
    in_grp = (lane_i >= N_GROUPS) & (lane_i < N_GROUPS + N_EXP) & (lane_grp == g_idx)
    le = jnp.where(in_grp, logit, neg)
    v1 = jnp.max(le, axis=-1, keepdims=True)
    i1 = jnp.min(jnp.where(le == v1, lane, far), axis=-1, keepdims=True)
    le2 = jnp.where(lane == i1, neg, le)
    v2 = jnp.max(le2, axis=-1, keepdims=True)
    i2 = jnp.min(jnp.where(le2 == v2, lane, far), axis=-1, keepdims=True)
    e1 = i1 - N_GROUPS
    e2 = i2 - N_GROUPS
    tt = jnp.exp(v2 - v1)
    w1 = pg_top / (1.0 + tt)
    w2 = pg_top * tt / (1.0 + tt)

    @pl.when(i == 0)
    def _():
        run_s[...] = jnp.zeros_like(run_s)

    hot = ((lane == e1) | (lane == e2)).astype(F32)
    ri = lax.broadcasted_iota(jnp.int32, (tm, tm), 0)
    ci = lax.broadcasted_iota(jnp.int32, (tm, tm), 1)
    before = _dot((ri > ci).astype(BF16), hot.astype(BF16)) + run_s[0:1, :]
    r1 = jnp.sum(jnp.where(lane == e1, before, 0.0), axis=-1, keepdims=True)
    r2 = jnp.sum(jnp.where(lane == e2, before, 0.0), axis=-1, keepdims=True)
    run_s[0:1, :] = run_s[0:1, :] + jnp.sum(hot, axis=0, keepdims=True)
    cnt_ref[...] = jnp.broadcast_to(run_s[0:1, :], cnt_ref.shape)

    rt = jnp.where(lane_i == 0, e1, 0.0)
    rt = jnp.where(lane_i == 1, e2, rt)
    rt = jnp.where(lane_i == 2, w1, rt)
    rt = jnp.where(lane_i == 3, w2, rt)
    rt = jnp.where(lane_i == 4, r1, rt)
    rt = jnp.where(lane_i == 5, r2, rt)
    rt_ref[...] = rt


def _post(x_ctx, x_lat, a_ctx, a_lat, g_ctx, g_lat, mod_all, seq_lat,
          w_mg, w_ba, w_bg, w_o, l1g, l1b, w_r, b_r):
    t_ctx, t_lat = x_ctx.shape[0], x_lat.shape[0]
    tm = TM_TOK
    nct, nlt = t_ctx // tm, t_lat // tm
    per_seq = seq_lat // tm
    nb_lat = t_lat // seq_lat
    t = t_ctx + t_lat
    cmap = lambda i: (jnp.minimum(i, nct - 1), 0)
    lmap = lambda i: (jnp.maximum(i - nct, 0), 0)
    mmap = lambda i: (jnp.where(i < nct, nb_lat, jnp.maximum(i - nct, 0) // per_seq), 0, 0)
    row = lambda i: (i, 0)
    const = lambda i: (0, 0)
    return pl.pallas_call(
        functools.partial(_post_kernel, n_ctx_tiles=nct),
        grid=(nct + nlt,),
        in_specs=[pl.BlockSpec((tm, D), cmap), pl.BlockSpec((tm, D), lmap),
                  pl.BlockSpec((tm, AW), cmap), pl.BlockSpec((tm, AW), lmap),
                  pl.BlockSpec((tm, GVW), cmap), pl.BlockSpec((tm, GVW), lmap),
                  pl.BlockSpec((1, 1, 6 * D), mmap),
                  pl.BlockSpec((D, MG_WIDTH), const), pl.BlockSpec((AW, D), const),
                  pl.BlockSpec((GVW, D), const), pl.BlockSpec((D, D), const),
                  pl.BlockSpec((1, D), const), pl.BlockSpec((1, D), const),
                  pl.BlockSpec((D, LANES), const), pl.BlockSpec((1, LANES), const)],
        out_specs=[pl.BlockSpec((tm, D), row), pl.BlockSpec((tm, D // 2), row),
                   pl.BlockSpec((tm, LANES), row), pl.BlockSpec((8, LANES), const)],
        out_shape=[jax.ShapeDtypeStruct((t, D), F32), jax.ShapeDtypeStruct((t, D // 2), U32),
                   jax.ShapeDtypeStruct((t, LANES), F32), jax.ShapeDtypeStruct((8, LANES), F32)],
        scratch_shapes=[pltpu.VMEM((8, LANES), F32)],
        compiler_params=_cparams(("arbitrary",)),
        name="post",
    )(x_ctx, x_lat, a_ctx, a_lat, g_ctx, g_lat, mod_all, w_mg, w_ba, w_bg, w_o, l1g, l1b, w_r, b_r)


def _row_copy_out(src_ref, dst_ref, r, p, sem):
    return pltpu.make_async_copy(src_ref.at[pl.ds(r, 1), :], dst_ref.at[pl.ds(p, 1), :], sem)


def _scatter_kernel(pos_ref, h_ref, xs_ref, sem):
    ts = h_ref.shape[0]

    def issue(r, carry):
        _row_copy_out(h_ref, xs_ref, r, pos_ref[0, 2 * r], sem).start()
        _row_copy_out(h_ref, xs_ref, r, pos_ref[0, 2 * r + 1], sem).start()
        return carry

    lax.fori_loop(0, ts, issue, 0)

    def drain(r, carry):
        _row_copy_out(h_ref, xs_ref, 0, 0, sem).wait()
        _row_copy_out(h_ref, xs_ref, 0, 0, sem).wait()
        return carry

    lax.fori_loop(0, ts, drain, 0)


def _scatter_rows(h2p, pos):
    t, w = h2p.shape
    ts = TS_ROWS
    pos2 = pos.reshape(t // ts, 1, 2 * ts)
    return pl.pallas_call(
        _scatter_kernel,
        grid=(t // ts,),
        in_specs=[pl.BlockSpec((None, 1, 2 * ts), lambda i: (i, 0, 0), memory_space=pltpu.SMEM),
                  pl.BlockSpec((ts, w), lambda i: (i, 0))],
        out_specs=pl.BlockSpec(memory_space=pl.ANY),
        out_shape=jax.ShapeDtypeStruct((2 * t, w), U32),
        scratch_shapes=[pltpu.SemaphoreType.DMA(())],
        compiler_params=_cparams(("arbitrary",)),
        name="scatter",
    )(pos2, h2p)


def _expert_kernel(vt_ref, ve_ref, lo_ref, hi_ref, first_ref, newe_ref,
                   xs_ref, wg_ref, wu_ref, wd_ref, y_ref, wgu_s, wd_s):
    v = pl.program_id(0)
    tm = xs_ref.shape[0]

    @pl.when(newe_ref[v] == 1)
    def _():
        wgu_s[:, 0:D_EXP] = wg_ref[...].astype(BF16)
        wgu_s[:, D_EXP:2 * D_EXP] = wu_ref[...].astype(BF16)
        wd_s[...] = wd_ref[...].astype(BF16)

    @pl.when(first_ref[v] == 1)
    def _():
        y_ref[...] = jnp.zeros_like(y_ref)

    lo = lo_ref[v]
    hi = hi_ref[v]

    @pl.when(hi > lo)
    def _():
        xl, xh = _unpack_pairs(xs_ref[...])
        gu = _dot(xl.astype(BF16), wgu_s[0:D // 2, :]) + _dot(xh.astype(BF16), wgu_s[D // 2:D, :])
        hid = _silu(gu[:, 0:D_EXP]) * gu[:, D_EXP:2 * D_EXP]
        y = _pack_pairs(_dot(hid.astype(BF16), wd_s[...]))
        rid = lax.broadcasted_iota(jnp.int32, (tm, D // 2), 0)
        y_ref[...] = jnp.where((rid >= lo) & (rid < hi), y, y_ref[...])


def _experts(xs, meta, w_gate, w_up, w_down):
    p = xs.shape[0]
    nv = meta[0].shape[0]
    tm = TM_EXP
    xmap = lambda v, vt, ve, lo, hi, fi, ne: (vt[v], 0)
    wmap = lambda v, vt, ve, lo, hi, fi, ne: (ve[v], 0, 0)
    return pl.pallas_call(
        _expert_kernel,
        grid_spec=pltpu.PrefetchScalarGridSpec(
            num_scalar_prefetch=6,
            grid=(nv,),
            in_specs=[pl.BlockSpec((tm, D // 2), xmap),
                      pl.BlockSpec((None, D, D_EXP), wmap), pl.BlockSpec((None, D, D_EXP), wmap),
                      pl.BlockSpec((None, D_EXP, D), wmap)],
            out_specs=pl.BlockSpec((tm, D // 2), xmap),
            scratch_shapes=[pltpu.VMEM((D, 2 * D_EXP), BF16), pltpu.VMEM((D_EXP, D), BF16)]),
        out_shape=jax.ShapeDtypeStruct((p, D // 2), U32),
        compiler_params=_cparams(("arbitrary",)),
        name="experts",
    )(*meta, xs, w_gate, w_up, w_down)


def _row_copy_in(src_ref, dst_ref, p, slot, r, sem):
    return pltpu.make_async_copy(src_ref.at[pl.ds(p, 1), :], dst_ref.at[slot, pl.ds(r, 1), :], sem)


def _final_kernel(pos_ref, x1_ref, rt_ref, mod_ref, l2g_ref, l2b_ref, ys_ref,
                  oc_ref, ol_ref, buf, sem, *, n_ctx_tiles):
    i = pl.program_id(0)
    tm = x1_ref.shape[0]

    def issue(r, carry):
        _row_copy_in(ys_ref, buf, pos_ref[0, 2 * r], 0, r, sem).start()
        _row_copy_in(ys_ref, buf, pos_ref[0, 2 * r + 1], 1, r, sem).start()
        return carry

    lax.fori_loop(0, tm, issue, 0)

    def drain(r, carry):
        _row_copy_in(ys_ref, buf, 0, 0, 0, sem).wait()
        _row_copy_in(ys_ref, buf, 0, 0, 0, sem).wait()
        return carry

    lax.fori_loop(0, tm, drain, 0)

    g2 = mod_ref[0, :, 5 * D:6 * D]
    w1 = rt_ref[:, 2:3]
    w2 = rt_ref[:, 3:4]
    al, ah = _unpack_pairs(buf[0])
    bl, bh = _unpack_pairs(buf[1])
    moe = jnp.concatenate([w1 * al + w2 * bl, w1 * ah + w2 * bh], axis=1)
    out = _ln(ALPHA * x1_ref[...] + g2 * moe) * l2g_ref[...] + l2b_ref[...]

    @pl.when(i < n_ctx_tiles)
    def _():
        oc_ref[...] = out

    @pl.when(i >= n_ctx_tiles)
    def _():
        ol_ref[...] = out


def _final(x1, rt, pos, mod_all, l2g, l2b, ys, t_ctx, seq_lat):
    t = x1.shape[0]
    tm = TM_TOK
    nct = t_ctx // tm
    t_lat = t - t_ctx
    per_seq = seq_lat // tm
    nb_lat = t_lat // seq_lat
    pos2 = pos.reshape(t // tm, 1, 2 * tm)
    row = lambda i: (i, 0)
    const = lambda i: (0, 0)
    mmap = lambda i: (jnp.where(i < nct, nb_lat, jnp.maximum(i - nct, 0) // per_seq), 0, 0)
    return pl.pallas_call(
        functools.partial(_final_kernel, n_ctx_tiles=nct),
        grid=(t // tm,),
        in_specs=[pl.BlockSpec((None, 1, 2 * tm), lambda i: (i, 0, 0), memory_space=pltpu.SMEM),
                  pl.BlockSpec((tm, D), row), pl.BlockSpec((tm, LANES), row),
                  pl.BlockSpec((1, 1, 6 * D), mmap),
                  pl.BlockSpec((1, D), const), pl.BlockSpec((1, D), const),
                  pl.BlockSpec(memory_space=pl.ANY)],
        out_specs=[pl.BlockSpec((tm, D), lambda i: (jnp.minimum(i, nct - 1), 0)),
                   pl.BlockSpec((tm, D), lambda i: (jnp.maximum(i - nct, 0), 0))],
        out_shape=[jax.ShapeDtypeStruct((t_ctx, D), F32), jax.ShapeDtypeStruct((t_lat, D), F32)],
        scratch_shapes=[pltpu.VMEM((2, tm, D // 2), U32), pltpu.SemaphoreType.DMA(())],
        compiler_params=_cparams(("arbitrary",)),
        name="final",
    )(pos2, x1, rt, mod_all, l2g, l2b, ys)


def _q_perm():
    idx = np.arange(AW)
    s, j, d = idx // LANES, (idx % LANES) // HD, idx % HD
    return 64 * (4 * j + s) + d


def _rope_tables(seq):
    t = jnp.arange(seq, dtype=jnp.int32)
    row = (t // GRID_W).astype(F32)
    col = (t % GRID_W).astype(F32)
    half = HD // 4
    inv = ROPE_THETA ** (-jnp.arange(half, dtype=F32) / half)
    lane = np.arange(LANES)
    d64 = lane % HD
    use_row = jnp.asarray(d64 < HD // 2)
    freq = inv[jnp.asarray(d64 % half)]
    pos = jnp.where(use_row[None, :], row[:, None], col[:, None])
    ang = pos * freq[None, :]
    sign = jnp.asarray(np.where((d64 % 32) < 16, -1.0, 1.0), F32)
    return jnp.cos(ang), jnp.sin(ang) * sign[None, :]


def _pair_states(s):
    b = s.shape[0]
    s = s.reshape(b, 2, 2, GLA_DK, GLA_DV)
    return s.transpose(0, 1, 4, 2, 3).reshape(b, 2, GLA_DV, 2 * GLA_DK)


def _unpair_states(s):
    b = s.shape[0]
    s = s.reshape(b, 2, GLA_DV, 2, GLA_DK)
    return s.transpose(0, 1, 3, 4, 2).reshape(b, GLA_H, GLA_DK, GLA_DV)


def _route_tables(rt, counts, t):
    e = rt[:, 0:2].astype(jnp.int32)
    rank = rt[:, 4:6].astype(jnp.int32)
    cnt = counts[0, :N_EXP].astype(jnp.int32)
    ends = jnp.cumsum(cnt)
    starts = ends - cnt
    pos = (starts[e] + rank).reshape(-1)
    p = 2 * t
    nt = p // TM_EXP
    bounds = jnp.sort(jnp.concatenate([jnp.arange(nt, dtype=jnp.int32) * TM_EXP, starts]))
    nxt = jnp.concatenate([bounds[1:], jnp.array([p], jnp.int32)])
    tile = jnp.minimum(bounds // TM_EXP, nt - 1)
    exp_id = jnp.minimum(jnp.searchsorted(ends, bounds, side="right"), N_EXP - 1).astype(jnp.int32)
    lo = bounds - tile * TM_EXP
    hi = jnp.minimum(nxt - tile * TM_EXP, TM_EXP)
    hi = jnp.where(nxt > bounds, hi, lo)
    prev_tile = jnp.concatenate([jnp.array([-1], jnp.int32), tile[:-1]])
    prev_exp = jnp.concatenate([jnp.array([-1], jnp.int32), exp_id[:-1]])
    first = (tile != prev_tile).astype(jnp.int32)
    newe = (exp_id != prev_exp).astype(jnp.int32)
    return pos, (tile.astype(jnp.int32), exp_id, lo.astype(jnp.int32), hi.astype(jnp.int32), first, newe)


def kernel(x_prompt, x_sample, cache_k, cache_v, state_gla_fwd, state_gla_bwd, c, c_ctx, w_ada, b_ada, w_in, q_norm, k_norm, gla_w_gate, gla_b_gate, gla_norm, w_br_attn, w_br_gla, w_out, ln1_g, ln1_b, router_group_w, router_group_b, router_expert_w, router_expert_b, exp_w_gate, exp_w_up, exp_w_down, ln2_g, ln2_b):
    b_ctx, seq_ctx, _ = x_prompt.shape
    b_lat, seq_lat, _ = x_sample.shape
    t_ctx, t_lat = b_ctx * seq_ctx, b_lat * seq_lat
    t = t_ctx + t_lat
    l = 0

    rows = -(-(b_lat + 1) // 8) * 8
    c_rows = jnp.zeros((rows, D), F32).at[:b_lat].set(c).at[b_lat].set(c_ctx)
    mod = _ada(c_rows, w_ada[l], b_ada[l][None, :])
    mod_all = mod[:b_lat + 1, None, :]
    mod_lat = mod_all[:b_lat]
    mod_ctx = mod_all[b_lat:]

    perm = _q_perm()
    w_full = w_in[l]
    w_a = jnp.concatenate([w_full[:, :AW][:, perm], w_full[:, AW:A_WIDTH]], axis=1).astype(BF16)
    w_mg = w_full[:, A_WIDTH:].astype(BF16)
    gain = jnp.concatenate([jnp.tile(q_norm[l], N_Q_HEADS), jnp.tile(k_norm[l], N_KV_HEADS)])[None, :]
    head_of = np.arange(AW + KVW) // HD
    ind = jnp.asarray((head_of[:, None] == np.arange(LANES)[None, :]) / HD, BF16)
    w_ba = w_br_attn[l][perm, :].astype(BF16)
    w_bg = w_br_gla[l].astype(BF16)
    w_o = w_out[l].astype(BF16)
    w_r = jnp.zeros((D, LANES), F32).at[:, :N_GROUPS].set(router_group_w[l])
    w_r = w_r.at[:, N_GROUPS:N_GROUPS + N_EXP].set(router_expert_w[l]).astype(BF16)
    b_r = jnp.zeros((1, LANES), F32).at[0, :N_GROUPS].set(router_group_b[l])
    b_r = b_r.at[0, N_GROUPS:N_GROUPS + N_EXP].set(router_expert_b[l])
    wg = jnp.zeros((2, 2 * GATE_RANK, GKW), F32)
    wg = wg.at[0, :GATE_RANK].set(gla_w_gate[l, 0]).at[1, GATE_RANK:].set(gla_w_gate[l, 1])
    bg = gla_b_gate[l][:, None, :]
    gn = gla_norm[l][None, :]

    xc = x_prompt.reshape(t_ctx, D)
    xl = x_sample.reshape(t_lat, D)

    q_c, k_c, v_c, qg_c, kg_c, vg_c, rs_c, lr_c, kf_c, vf_c = _inproj(
        xc, mod_ctx, w_a, gain, ind, None, seq_ctx, latent=False)
    attn_c = _attention(q_c, k_c, v_c, None, seq_ctx)
    zero_state = jnp.zeros((1, 2, 2, LANES, LANES), F32)
    gla_c, sfin_c = _gla(qg_c, kg_c, vg_c, lr_c, rs_c, zero_state, wg, bg, gn, seq_ctx)

    q_l, k_l, v_l, qg_l, kg_l, vg_l, rs_l, lr_l = _inproj(
        xl, mod_lat, w_a, gain, ind, _rope_tables(seq_lat), seq_lat, latent=True)
    past = cache_k.shape[2]
    kc = cache_k[:, l].reshape(b_lat, past, KVW).astype(BF16)
    vc = cache_v[:, l].reshape(b_lat, past, KVW).astype(BF16)
    attn_l = _attention(q_l, k_l, v_l, (kc, vc), seq_lat)
    s0 = jnp.stack([_pair_states(state_gla_fwd[:, l]), _pair_states(state_gla_bwd[:, l])], axis=1)
    gla_l, _ = _gla(qg_l, kg_l, vg_l, lr_l, rs_l, s0, wg, bg, gn, seq_lat)

    x1, h2p, rt, counts = _post(xc, xl, attn_c, attn_l, gla_c, gla_l, mod_all, seq_lat,
                                w_mg, w_ba, w_bg, w_o, ln1_g[l][None, :], ln1_b[l][None, :], w_r, b_r)

    pos, meta = _route_tables(rt, counts, t)
    xs = _scatter_rows(h2p, pos)
    ys = _experts(xs, meta, exp_w_gate[l], exp_w_up[l], exp_w_down[l])
    y_ctx, y_lat = _final(x1, rt, pos, mod_all, ln2_g[l][None, :], ln2_b[l][None, :], ys, t_ctx, seq_lat)

    new_k = kf_c.reshape(b_ctx, 1, seq_ctx, N_KV_HEADS, HD)
    new_v = vf_c.reshape(b_ctx, 1, seq_ctx, N_KV_HEADS, HD)
    new_sf = _unpair_states(sfin_c[:, 0])[:, None]
    new_sb = _unpair_states(sfin_c[:, 1])[:, None]
    return (y_ctx.reshape(b_ctx, seq_ctx, D), y_lat.reshape(b_lat, seq_lat, D),
            new_k, new_v, new_sf, new_sb)
```

```python
import functools

import numpy as np
import jax
import jax.numpy as jnp
from jax import lax
from jax.experimental import pallas as pl
from jax.experimental.pallas import tpu as pltpu

F32 = jnp.float32
BF16 = jnp.bfloat16
HIGHEST = lax.Precision.HIGHEST

D = 1024
GRID_W = 64
HD = 64
N_Q_HEADS = 8
N_KV_HEADS = 2
AW = N_Q_HEADS * HD
KVW = N_KV_HEADS * HD
ROPE_THETA = 10000.0
GLA_H = 4
GLA_DK = 64
GLA_DV = 128
GKW = GLA_H * GLA_DK
GVW = GLA_H * GLA_DV
GATE_RANK = 16
GLA_TAU = 16.0
CHUNK = 64
N_GROUPS = 4
EPG = 8
N_EXP = N_GROUPS * EPG
D_EXP = 256
DEPTH = 1
ALPHA = (2.0 * DEPTH) ** 0.25
LN_EPS = 1e-6
RMS_EPS = 1e-6

LANES = 128
A_WIDTH = AW + 2 * KVW + 2 * GKW + 2 * GVW + 2 * GATE_RANK
MG_WIDTH = 2 * D
TM_TOK = 512
TQ_LAT = 128
TM_EXP = 256
TS_ROWS = 256
VMEM_LIMIT = 56 * 1024 * 1024


def _cparams(sem):
    return pltpu.CompilerParams(dimension_semantics=sem, vmem_limit_bytes=VMEM_LIMIT)


def _dot(a, b):
    return jnp.dot(a, b, preferred_element_type=F32)


def _dot_nt(a, b):
    return lax.dot_general(a, b, (((1,), (1,)), ((), ())), preferred_element_type=F32)


def _dot_tn(a, b):
    return lax.dot_general(a, b, (((0,), (0,)), ((), ())), preferred_element_type=F32)


def _ln(x):
    mu = jnp.mean(x, axis=-1, keepdims=True)
    xc = x - mu
    var = jnp.mean(xc * xc, axis=-1, keepdims=True)
    return xc * lax.rsqrt(var + LN_EPS)


def _silu(x):
    return x * jax.nn.sigmoid(x)


def _split_bf16(x):
    hi = x.astype(BF16)
    lo = (x - hi.astype(F32)).astype(BF16)
    return hi, lo


def _ada_kernel(c_ref, w_ref, b_ref, o_ref):
    s = _silu(c_ref[...])
    o_ref[...] = jnp.dot(s, w_ref[...], preferred_element_type=F32, precision=HIGHEST) + b_ref[...]


def _ada(c_rows, w_ada, b_ada):
    rows = c_rows.shape[0]
    n = w_ada.shape[1]
    bn = 1024
    return pl.pallas_call(
        _ada_kernel,
        grid=(n // bn,),
        in_specs=[pl.BlockSpec((rows, D), lambda j: (0, 0)),
                  pl.BlockSpec((D, bn), lambda j: (0, j)),
                  pl.BlockSpec((1, bn), lambda j: (0, j))],
        out_specs=pl.BlockSpec((rows, bn), lambda j: (0, j)),
        out_shape=jax.ShapeDtypeStruct((rows, n), F32),
        compiler_params=_cparams(("arbitrary",)),
        name="ada",
    )(c_rows, w_ada, b_ada)


def _inproj_kernel(*refs, latent):
    if latent:
        (x_ref, mod_ref, w_ref, gain_ref, ind_ref, cos_ref, sin_ref,
         q_ref, k_ref, v_ref, qg_ref, kg_ref, vg_ref, rs_ref, lr_ref) = refs
    else:
        (x_ref, mod_ref, w_ref, gain_ref, ind_ref,
         q_ref, k_ref, v_ref, qg_ref, kg_ref, vg_ref, rs_ref, lr_ref, kf_ref, vf_ref) = refs
    tm = x_ref.shape[0]
    sh1 = mod_ref[0, :, 0:D]
    sc1 = mod_ref[0, :, D:2 * D]
    h = _ln(x_ref[...]) * (1.0 + sc1) + sh1
    res = _dot(h.astype(BF16), w_ref[...])

    qk = res[:, 0:AW + KVW]
    hi, lo = _split_bf16(qk * qk)
    ms = _dot(hi, ind_ref[...]) + _dot(lo, ind_ref[...])
    r = lax.rsqrt(ms + RMS_EPS)
    lane = lax.broadcasted_iota(jnp.int32, (tm, LANES), 1)
    low_half = lane < HD
    if latent:
        cos = cos_ref[...]
        sin = sin_ref[...]
        first = (lane % 32) < 16
    for s in range(5):
        rb = jnp.where(low_half, r[:, 2 * s:2 * s + 1], r[:, 2 * s + 1:2 * s + 2])
        y = qk[:, LANES * s:LANES * (s + 1)] * rb * gain_ref[:, LANES * s:LANES * (s + 1)]
        if s == 4 and not latent:
            for j in range(N_KV_HEADS):
                kf_ref[:, j, :] = y[:, HD * j:HD * (j + 1)]
        if latent:
            partner = jnp.where(first, pltpu.roll(y, LANES - 16, 1), pltpu.roll(y, 16, 1))
            y = y * cos + partner * sin
        if s < 4:
            q_ref[:, LANES * s:LANES * (s + 1)] = (y * (HD ** -0.5)).astype(BF16)
        else:
            k_ref[...] = y.astype(BF16)
    o = AW + KVW
    v = res[:, o:o + KVW]
    v_ref[...] = v.astype(BF16)
    if not latent:
        for j in range(N_KV_HEADS):
            vf_ref[:, j, :] = v[:, HD * j:HD * (j + 1)]
    o += KVW
    qg_ref[...] = res[:, o:o + GKW] * (GLA_DK ** -0.5)
    o += GKW
    kg_ref[...] = res[:, o:o + GKW]
    o += GKW
    vg_ref[...] = res[:, o:o + GVW].astype(BF16)
    o += GVW
    rs_ref[...] = _silu(res[:, o:o + GVW]).astype(BF16)
    o += GVW
    lr_ref[...] = res[:, o:o + 2 * GATE_RANK]


def _inproj(x2, mod, w_a, gain, ind, rope, seq, latent):
    t = x2.shape[0]
    tm = min(TM_TOK, seq)
    per_seq = seq // tm
    row = lambda i: (i, 0)
    const = lambda i: (0, 0)
    in_specs = [pl.BlockSpec((tm, D), row),
                pl.BlockSpec((1, 1, 6 * D), (lambda i: (i // per_seq, 0, 0)) if latent else (lambda i: (0, 0, 0))),
                pl.BlockSpec((D, A_WIDTH), const),
                pl.BlockSpec((1, AW + KVW), const),
                pl.BlockSpec((AW + KVW, LANES), const)]
    args = [x2, mod, w_a, gain, ind]
    if latent:
        in_specs += [pl.BlockSpec((tm, LANES), lambda i: (i % per_seq, 0))] * 2
        args += list(rope)
    widths = [(AW, BF16), (KVW, BF16), (KVW, BF16), (GKW, F32), (GKW, F32), (GVW, BF16), (GVW, BF16),
              (2 * GATE_RANK, F32)]
    out_specs = [pl.BlockSpec((tm, w), row) for w, _ in widths]
    out_shape = [jax.ShapeDtypeStruct((t, w), dt) for w, dt in widths]
    if not latent:
        cache_spec = pl.BlockSpec((None, None, tm, N_KV_HEADS, HD),
                                  lambda i: (i // per_seq, 0, i % per_seq, 0, 0))
        out_specs += [cache_spec] * 2
        out_shape += [jax.ShapeDtypeStruct((t // seq, 1, seq, N_KV_HEADS, HD), F32)] * 2
    return pl.pallas_call(
        functools.partial(_inproj_kernel, latent=latent),
        grid=(t // tm,),
        in_specs=in_specs,
        out_specs=out_specs,
        out_shape=out_shape,
        compiler_params=_cparams(("parallel",)),
        name="inproj_lat" if latent else "inproj_ctx",
    )(*args)


def _attn_kernel(*refs, has_cache):
    if has_cache:
        q_ref, k_ref, v_ref, kc_ref, vc_ref, o_ref = refs
    else:
        q_ref, k_ref, v_ref, o_ref = refs
    tq = q_ref.shape[0]
    lane = lax.broadcasted_iota(jnp.int32, (tq, LANES), 1)
    low_half = lane < HD
    k = k_ref[...]
    v = v_ref[...]
    outs = []
    for j in range(N_KV_HEADS):
        keep = low_half if j == 0 else jnp.logical_not(low_half)
        zero = jnp.zeros((tq, LANES), BF16)
        qs = jnp.concatenate(
            [jnp.where(keep, q_ref[:, LANES * s:LANES * (s + 1)], zero) for s in range(4)], axis=0)
        s1 = _dot_nt(qs, k)
        m = jnp.max(s1, axis=-1, keepdims=True)
        if has_cache:
            s2 = _dot_nt(qs, kc_ref[...])
            m = jnp.maximum(m, jnp.max(s2, axis=-1, keepdims=True))
        p1 = jnp.exp(s1 - m)
        den = jnp.sum(p1, axis=-1, keepdims=True)
        acc = _dot(p1.astype(BF16), v)
        if has_cache:
            p2 = jnp.exp(s2 - m)
            den = den + jnp.sum(p2, axis=-1, keepdims=True)
            acc = acc + _dot(p2.astype(BF16), vc_ref[...])
        outs.append(acc / den)
    for s in range(4):
        o_ref[:, LANES * s:LANES * (s + 1)] = jnp.where(
            low_half, outs[0][s * tq:(s + 1) * tq], outs[1][s * tq:(s + 1) * tq]).astype(BF16)


def _attention(q, k, v, cache, seq):
    t = q.shape[0]
    if cache is None:
        tq = seq
        grid = (t // seq,)
        qmap = lambda b: (b, 0)
        in_specs = [pl.BlockSpec((tq, AW), qmap), pl.BlockSpec((seq, KVW), qmap),
                    pl.BlockSpec((seq, KVW), qmap)]
        args = [q, k, v]
        sem = ("parallel",)
        name = "attn_ctx"
    else:
        tq = TQ_LAT
        nq = seq // tq
        kc, vc = cache
        past = kc.shape[1]
        grid = (t // seq, nq)
        qmap = lambda b, i: (b * nq + i, 0)
        kmap = lambda b, i: (b, 0)
        cmap = lambda b, i: (b, 0, 0)
        in_specs = [pl.BlockSpec((tq, AW), qmap), pl.BlockSpec((seq, KVW), kmap),
                    pl.BlockSpec((seq, KVW), kmap),
                    pl.BlockSpec((None, past, KVW), cmap), pl.BlockSpec((None, past, KVW), cmap)]
        args = [q, k, v, kc, vc]
        sem = ("parallel", "arbitrary")
        name = "attn_lat"
    return pl.pallas_call(
        functools.partial(_attn_kernel, has_cache=cache is not None),
        grid=grid,
        in_specs=in_specs,
        out_specs=pl.BlockSpec((tq, AW), qmap),
        out_shape=jax.ShapeDtypeStruct((t, AW), BF16),
        compiler_params=_cparams(sem),
        name=name,
    )(*args)


def _gla_kernel(qg_ref, kg_ref, vg_ref, lr_ref, rs_ref, s0_ref, wg_ref, bg_ref, gn_ref,
                o_ref, sfin_ref, cum_s, kv_s, dec_s, acc_s):
    n = qg_ref.shape[0]
    nc = n // CHUNK
    lane = lax.broadcasted_iota(jnp.int32, (CHUNK, LANES), 1)
    low_half = lane < GLA_DK
    lane_sq = lax.broadcasted_iota(jnp.int32, (LANES, LANES), 1)
    low_half_sq = lane_sq < GLA_DK
    ri = lax.broadcasted_iota(jnp.int32, (CHUNK, CHUNK), 0)
    ci = lax.broadcasted_iota(jnp.int32, (CHUNK, CHUNK), 1)
    keep = (ri >= ci, ci >= ri)
    tri = tuple(m.astype(F32) for m in keep)

    lr = lr_ref[...]
    for d in range(2):
        z = jnp.dot(lr, wg_ref[d], preferred_element_type=F32, precision=HIGHEST) + bg_ref[d]
        cum_s[d] = (jnp.minimum(z, 0.0) - jnp.log1p(jnp.exp(-jnp.abs(z)))) * (1.0 / GLA_TAU)

    def pass1(c, carry):
        rows = pl.ds(pl.multiple_of(c * CHUNK, CHUNK), CHUNK)
        kc = kg_ref[rows, :]
        for d in range(2):
            cum = jnp.dot(tri[d], cum_s[d, rows, :], preferred_element_type=F32, precision=HIGHEST)
            cum_s[d, rows, :] = cum
            last = cum[CHUNK - 1:CHUNK, :] if d == 0 else cum[0:1, :]
            dec_s[d, c] = jnp.exp(last)
            kdec = (kc * jnp.exp(last - cum)).astype(BF16)
            for p in range(2):
                slab = kdec[:, LANES * p:LANES * (p + 1)]
                va = vg_ref[rows, GLA_DV * (2 * p):GLA_DV * (2 * p + 1)]
                vb = vg_ref[rows, GLA_DV * (2 * p + 1):GLA_DV * (2 * p + 2)]
                kv_s[d, c, p] = jnp.where(low_half_sq, _dot_tn(va, slab), _dot_tn(vb, slab))
        return carry

    lax.fori_loop(0, nc, pass1, 0)

    for d in range(2):
        def scan(i, st):
            c = i if d == 0 else nc - 1 - i
            dec = dec_s[d, c]
            new = []
            for p in range(2):
                kv = kv_s[d, c, p]
                kv_s[d, c, p] = st[p]
                new.append(st[p] * dec[:, LANES * p:LANES * (p + 1)] + kv)
            return tuple(new)

        fin = lax.fori_loop(0, nc, scan, (s0_ref[d, 0], s0_ref[d, 1]))
        sfin_ref[d, 0] = fin[0]
        sfin_ref[d, 1] = fin[1]

    for d in range(2):
        def pass3(c, carry):
            rows = pl.ds(pl.multiple_of(c * CHUNK, CHUNK), CHUNK)
            cum = cum_s[d, rows, :]
            qt = qg_ref[rows, :] * jnp.exp(cum)
            kt = (kg_ref[rows, :] * jnp.exp(-cum)).astype(BF16)
            for h in range(GLA_H):
                p = h // 2
                sel = low_half if h % 2 == 0 else jnp.logical_not(low_half)
                qh = jnp.where(sel, qt[:, LANES * p:LANES * (p + 1)], 0.0).astype(BF16)
                a = jnp.where(keep[d], _dot_nt(qh, kt[:, LANES * p:LANES * (p + 1)]), 0.0)
                vh = vg_ref[rows, GLA_DV * h:GLA_DV * (h + 1)]
                oh = _dot(a.astype(BF16), vh) + _dot_nt(qh, kv_s[d, c, p].astype(BF16))
                cols = slice(GLA_DV * h, GLA_DV * (h + 1))
                if d == 0:
                    acc_s[rows, cols] = oh
                else:
                    tot = acc_s[rows, cols] + oh
                    y = tot * lax.rsqrt(jnp.mean(tot * tot, axis=-1, keepdims=True) + RMS_EPS)
                    y = (y * gn_ref[...]).astype(BF16)
                    o_ref[rows, cols] = (y * rs_ref[rows, cols]).astype(BF16)
            return carry

        lax.fori_loop(0, nc, pass3, 0)


def _gla(qg, kg, vg, lr, rs, s0, wg, bg, gn, seq):
    t = qg.shape[0]
    nb = t // seq
    nc = seq // CHUNK
    row = lambda b: (b, 0)
    c3 = lambda b: (0, 0, 0)
    st = lambda b: (b, 0, 0, 0, 0)
    if s0.shape[0] == 1:
        s0map = lambda b: (0, 0, 0, 0, 0)
    else:
        s0map = st
    return pl.pallas_call(
        _gla_kernel,
        grid=(nb,),
        in_specs=[pl.BlockSpec((seq, GKW), row), pl.BlockSpec((seq, GKW), row),
                  pl.BlockSpec((seq, GVW), row), pl.BlockSpec((seq, 2 * GATE_RANK), row),
                  pl.BlockSpec((seq, GVW), row),
                  pl.BlockSpec((None, 2, 2, LANES, LANES), s0map),
                  pl.BlockSpec((2, 2 * GATE_RANK, GKW), c3), pl.BlockSpec((2, 1, GKW), c3),
                  pl.BlockSpec((1, GLA_DV), lambda b: (0, 0))],
        out_specs=[pl.BlockSpec((seq, GVW), row),
                   pl.BlockSpec((None, 2, 2, LANES, LANES), st)],
        out_shape=[jax.ShapeDtypeStruct((t, GVW), BF16),
                   jax.ShapeDtypeStruct((nb, 2, 2, LANES, LANES), F32)],
        scratch_shapes=[pltpu.VMEM((2, seq, GKW), F32),
                        pltpu.VMEM((2, nc, 2, LANES, LANES), F32),
                        pltpu.VMEM((2, nc, 1, GKW), F32),
                        pltpu.VMEM((seq, GVW), F32)],
        compiler_params=_cparams(("parallel",)),
        name="gla_lat" if seq > 256 else "gla_ctx",
    )(qg, kg, vg, lr, rs, s0, wg, bg, gn)


ROW_SUB = D // LANES


def _store_row_slabs(ref, x):
    for c in range(ROW_SUB):
        ref[:, c, :] = x[:, LANES * c:LANES * (c + 1)]


def _load_row_slabs(ref):
    return jnp.concatenate([ref[:, c, :] for c in range(ROW_SUB)], axis=1)


def _post_kernel(xc_ref, xl_ref, ac_ref, al_ref, gc_ref, gl_ref, mod_ref,
                 wmg_ref, wba_ref, wbg_ref, wo_ref, l1g_ref, l1b_ref, wr_ref, br_ref,
                 x1_ref, h2_ref, rt_ref, ert_ref, cnt_ref, run_s, *, n_ctx_tiles):
    i = pl.program_id(0)
    tm = xc_ref.shape[0]
    is_ctx = i < n_ctx_tiles
    x = jnp.where(is_ctx, xc_ref[...], xl_ref[...])
    attn = jnp.where(is_ctx, ac_ref[...], al_ref[...])
    gla = jnp.where(is_ctx, gc_ref[...], gl_ref[...])
    sh1 = mod_ref[0, :, 0:D]
    sc1 = mod_ref[0, :, D:2 * D]
    g1 = mod_ref[0, :, 2 * D:3 * D]
    sh2 = mod_ref[0, :, 3 * D:4 * D]
    sc2 = mod_ref[0, :, 4 * D:5 * D]
    h = (_ln(x) * (1.0 + sc1) + sh1).astype(BF16)
    gates = jax.nn.sigmoid(_dot(h, wmg_ref[...]))
    merged = gates[:, :D] * _dot(attn, wba_ref[...]) + gates[:, D:] * _dot(gla, wbg_ref[...])
    mix = _dot(merged.astype(BF16), wo_ref[...])
    x1 = _ln(ALPHA * x + g1 * mix) * l1g_ref[...] + l1b_ref[...]
    x1_ref[...] = x1
    h2 = _ln(x1) * (1.0 + sc2) + sh2
    _store_row_slabs(h2_ref, h2)

    logit = _dot(h2.astype(BF16), wr_ref[...]) + br_ref[...]
    lane_i = lax.broadcasted_iota(jnp.int32, (tm, LANES), 1)
    lane = lane_i.astype(F32)
    lane_grp = ((lane_i - N_GROUPS) >> 3).astype(F32)
    neg = jnp.float32(-jnp.inf)
    far = jnp.float32(LANES)
    is_g = lane_i < N_GROUPS
    lg = jnp.where(is_g, logit, neg)
    mg = jnp.max(lg, axis=-1, keepdims=True)
    pg_top = 1.0 / jnp.sum(jnp.where(is_g, jnp.exp(logit - mg), 0.0), axis=-1, keepdims=True)
    g_idx = jnp.min(jnp.where(lg == mg, lane, far), axis=-1, keepdims=True)
    in_grp = (lane_i >= N_GROUPS) & (lane_i < N_GROUPS + N_EXP) & (lane_grp == g_idx)
    le = jnp.where(in_grp, logit, neg)
    v1 = jnp.max(le, axis=-1, keepdims=True)
    i1 = jnp.min(jnp.where(le == v1, lane, far), axis=-1, keepdims=True)
    le2 = jnp.where(lane == i1, neg, le)
    v2 = jnp.max(le2, axis=-1, keepdims=True)
    i2 = jnp.min(jnp.where(le2 == v2, lane, far), axis=-1, keepdims=True)
    e1 = i1 - N_GROUPS
    e2 = i2 - N_GROUPS
    tt = jnp.exp(v2 - v1)
    w1 = pg_top / (1.0 + tt)
    w2 = pg_top * tt / (1.0 + tt)

    @pl.when(i == 0)
    def _():
        run_s[...] = jnp.zeros_like(run_s)

    hot = ((lane == e1) | (lane == e2)).astype(F32)
    ri = lax.broadcasted_iota(jnp.int32, (tm, tm), 0)
    ci = lax.broadcasted_iota(jnp.int32, (tm, tm), 1)
    before = _dot((ri > ci).astype(BF16), hot.astype(BF16)) + run_s[0:1, :]
    r1 = jnp.sum(jnp.where(lane == e1, before, 0.0), axis=-1, keepdims=True)
    r2 = jnp.sum(jnp.where(lane == e2, before, 0.0), axis=-1, keepdims=True)
    run_s[0:1, :] = run_s[0:1, :] + jnp.sum(hot, axis=0, keepdims=True)
    cnt_ref[...] = jnp.broadcast_to(run_s[0:1, :], cnt_ref.shape)

    rt = jnp.where(lane_i == 0, e1, 0.0)
    rt = jnp.where(lane_i == 1, e2, rt)
    rt = jnp.where(lane_i == 2, w1, rt)
    rt = jnp.where(lane_i == 3, w2, rt)
    rt = jnp.where(lane_i == 4, r1, rt)
    rt = jnp.where(lane_i == 5, r2, rt)
    rt_ref[...] = rt
    ert_ref[...] = rt.T[0:8, :]


def _post(x_ctx, x_lat, a_ctx, a_lat, g_ctx, g_lat, mod_all, seq_lat,
          w_mg, w_ba, w_bg, w_o, l1g, l1b, w_r, b_r):
    t_ctx, t_lat = x_ctx.shape[0], x_lat.shape[0]
    tm = TM_TOK
    nct, nlt = t_ctx // tm, t_lat // tm
    per_seq = seq_lat // tm
    nb_lat = t_lat // seq_lat
    t = t_ctx + t_lat
    cmap = lambda i: (jnp.minimum(i, nct - 1), 0)
    lmap = lambda i: (jnp.maximum(i - nct, 0), 0)
    mmap = lambda i: (jnp.where(i < nct, nb_lat, jnp.maximum(i - nct, 0) // per_seq), 0, 0)
    row = lambda i: (i, 0)
    const = lambda i: (0, 0)
    return pl.pallas_call(
        functools.partial(_post_kernel, n_ctx_tiles=nct),
        grid=(nct + nlt,),
        in_specs=[pl.BlockSpec((tm, D), cmap), pl.BlockSpec((tm, D), lmap),
                  pl.BlockSpec((tm, AW), cmap), pl.BlockSpec((tm, AW), lmap),
                  pl.BlockSpec((tm, GVW), cmap), pl.BlockSpec((tm, GVW), lmap),
                  pl.BlockSpec((1, 1, 6 * D), mmap),
                  pl.BlockSpec((D, MG_WIDTH), const), pl.BlockSpec((AW, D), const),
                  pl.BlockSpec((GVW, D), const), pl.BlockSpec((D, D), const),
                  pl.BlockSpec((1, D), const), pl.BlockSpec((1, D), const),
                  pl.BlockSpec((D, LANES), const), pl.BlockSpec((1, LANES), const)],
        out_specs=[pl.BlockSpec((tm, D), row), pl.BlockSpec((tm, ROW_SUB, LANES), lambda i: (i, 0, 0)),
                   pl.BlockSpec((tm, LANES), row), pl.BlockSpec((8, tm), lambda i: (0, i)),
                   pl.BlockSpec((8, LANES), const)],
        out_shape=[jax.ShapeDtypeStruct((t, D), F32), jax.ShapeDtypeStruct((t, ROW_SUB, LANES), F32),
                   jax.ShapeDtypeStruct((t, LANES), F32), jax.ShapeDtypeStruct((8, t), F32),
                   jax.ShapeDtypeStruct((8, LANES), F32)],
        scratch_shapes=[pltpu.VMEM((8, LANES), F32)],
        compiler_params=_cparams(("arbitrary",)),
        name="post",
    )(x_ctx, x_lat, a_ctx, a_lat, g_ctx, g_lat, mod_all, w_mg, w_ba, w_bg, w_o, l1g, l1b, w_r, b_r)


ROW_UNROLL = 8


def _row_copy(src_ref, dst_ref, sem):
    return pltpu.make_async_copy(src_ref, dst_ref, sem)


def _scatter_kernel(pos0_ref, pos1_ref, h_ref, xs_ref, sem):
    ts = h_ref.shape[0]

    def issue(g, carry):
        r0 = pl.multiple_of(g * ROW_UNROLL, ROW_UNROLL)
        for k in range(ROW_UNROLL):
            src = h_ref.at[pl.ds(r0 + k, 1)]
            _row_copy(src, xs_ref.at[pl.ds(pos0_ref[0, r0 + k], 1)], sem).start(priority=0)
            _row_copy(src, xs_ref.at[pl.ds(pos1_ref[0, r0 + k], 1)], sem).start(priority=1)
        return carry

    lax.fori_loop(0, ts // ROW_UNROLL, issue, 0)
    for _ in range(2):
        _row_copy(h_ref, xs_ref.at[pl.ds(0, ts)], sem).wait()


def _scatter_rows(h2p, pos0, pos1):
    t = h2p.shape[0]
    ts = TS_ROWS
    smem = lambda: pl.BlockSpec((None, 1, ts), lambda i: (i, 0, 0), memory_space=pltpu.SMEM)
    return pl.pallas_call(
        _scatter_kernel,
        grid=(t // ts,),
        in_specs=[smem(), smem(), pl.BlockSpec((ts, ROW_SUB, LANES), lambda i: (i, 0, 0))],
        out_specs=pl.BlockSpec(memory_space=pl.ANY),
        out_shape=jax.ShapeDtypeStruct((2 * t, ROW_SUB, LANES), F32),
        scratch_shapes=[pltpu.SemaphoreType.DMA(())],
        compiler_params=_cparams(("arbitrary",)),
        name="scatter",
    )(pos0.reshape(t // ts, 1, ts), pos1.reshape(t // ts, 1, ts), h2p)


def _expert_kernel(vt_ref, ve_ref, lo_ref, hi_ref, first_ref, newe_ref,
                   xs_ref, wg_ref, wu_ref, wd_ref, y_ref, wgu_s, wd_s):
    v = pl.program_id(0)
    tm = xs_ref.shape[0]

    @pl.when(newe_ref[v] == 1)
    def _():
        wgu_s[:, 0:D_EXP] = wg_ref[...].astype(BF16)
        wgu_s[:, D_EXP:2 * D_EXP] = wu_ref[...].astype(BF16)
        wd_s[...] = wd_ref[...].astype(BF16)

    @pl.when(first_ref[v] == 1)
    def _():
        y_ref[...] = jnp.zeros_like(y_ref)

    lo = lo_ref[v]
    hi = hi_ref[v]

    @pl.when(hi > lo)
    def _():
        gu = _dot(_load_row_slabs(xs_ref).astype(BF16), wgu_s[...])
        hid = _silu(gu[:, 0:D_EXP]) * gu[:, D_EXP:2 * D_EXP]
        y = _dot(hid.astype(BF16), wd_s[...])
        rid = lax.broadcasted_iota(jnp.int32, (tm, D), 0)
        _store_row_slabs(y_ref, jnp.where((rid >= lo) & (rid < hi), y, _load_row_slabs(y_ref)))


def _experts(xs, meta, w_gate, w_up, w_down):
    p = xs.shape[0]
    nv = meta[0].shape[0]
    tm = TM_EXP
    xmap = lambda v, vt, ve, lo, hi, fi, ne: (vt[v], 0, 0)
    wmap = lambda v, vt, ve, lo, hi, fi, ne: (ve[v], 0, 0)
    return pl.pallas_call(
        _expert_kernel,
        grid_spec=pltpu.PrefetchScalarGridSpec(
            num_scalar_prefetch=6,
            grid=(nv,),
            in_specs=[pl.BlockSpec((tm, ROW_SUB, LANES), xmap),
                      pl.BlockSpec((None, D, D_EXP), wmap), pl.BlockSpec((None, D, D_EXP), wmap),
                      pl.BlockSpec((None, D_EXP, D), wmap)],
            out_specs=pl.BlockSpec((tm, ROW_SUB, LANES), xmap),
            scratch_shapes=[pltpu.VMEM((D, 2 * D_EXP), BF16), pltpu.VMEM((D_EXP, D), BF16)]),
        out_shape=jax.ShapeDtypeStruct((p, ROW_SUB, LANES), F32),
        compiler_params=_cparams(("arbitrary",)),
        name="experts",
    )(*meta, xs, w_gate, w_up, w_down)


def _final_kernel(p0c_ref, p1c_ref, p0n_ref, p1n_ref, x1_ref, rt_ref, mod_ref, l2g_ref, l2b_ref, ys_ref,
                  oc_ref, ol_ref, buf, sem, *, n_ctx_tiles):
    i = pl.program_id(0)
    n = pl.num_programs(0)
    tm = x1_ref.shape[0]

    def gather(p0_ref, p1_ref, slot):
        def issue(g, carry):
            r0 = pl.multiple_of(g * ROW_UNROLL, ROW_UNROLL)
            for k in range(ROW_UNROLL):
                _row_copy(ys_ref.at[pl.ds(p0_ref[0, r0 + k], 1)],
                          buf.at[slot, 0, pl.ds(r0 + k, 1)], sem.at[slot]).start(priority=0)
                _row_copy(ys_ref.at[pl.ds(p1_ref[0, r0 + k], 1)],
                          buf.at[slot, 1, pl.ds(r0 + k, 1)], sem.at[slot]).start(priority=1)
            return carry

        lax.fori_loop(0, tm // ROW_UNROLL, issue, 0)

    cur = i % 2

    @pl.when(i == 0)
    def _():
        gather(p0c_ref, p1c_ref, 0)

    @pl.when(i + 1 < n)
    def _():
        gather(p0n_ref, p1n_ref, 1 - cur)

    for k in range(2):
        _row_copy(ys_ref.at[pl.ds(0, tm)], buf.at[cur, k], sem.at[cur]).wait()

    g2 = mod_ref[0, :, 5 * D:6 * D]
    w1 = rt_ref[:, 2:3]
    w2 = rt_ref[:, 3:4]
    moe = w1 * _load_row_slabs(buf.at[cur, 0]) + w2 * _load_row_slabs(buf.at[cur, 1])
    out = _ln(ALPHA * x1_ref[...] + g2 * moe) * l2g_ref[...] + l2b_ref[...]

    @pl.when(i < n_ctx_tiles)
    def _():
        oc_ref[...] = out

    @pl.when(i >= n_ctx_tiles)
    def _():
        ol_ref[...] = out


def _final(x1, rt, pos0, pos1, mod_all, l2g, l2b, ys, t_ctx, seq_lat):
    t = x1.shape[0]
    tm = TM_TOK
    nt = t // tm
    nct = t_ctx // tm
    t_lat = t - t_ctx
    per_seq = seq_lat // tm
    nb_lat = t_lat // seq_lat
    p0 = pos0.reshape(nt, 1, tm)
    p1 = pos1.reshape(nt, 1, tm)
    row = lambda i: (i, 0)
    const = lambda i: (0, 0)
    mmap = lambda i: (jnp.where(i < nct, nb_lat, jnp.maximum(i - nct, 0) // per_seq), 0, 0)
    smem_cur = lambda: pl.BlockSpec((None, 1, tm), lambda i: (i, 0, 0), memory_space=pltpu.SMEM)
    smem_nxt = lambda: pl.BlockSpec((None, 1, tm), lambda i: (jnp.minimum(i + 1, nt - 1), 0, 0),
                                    memory_space=pltpu.SMEM)
    return pl.pallas_call(
        functools.partial(_final_kernel, n_ctx_tiles=nct),
        grid=(nt,),
        in_specs=[smem_cur(), smem_cur(), smem_nxt(), smem_nxt(),
                  pl.BlockSpec((tm, D), row), pl.BlockSpec((tm, LANES), row),
                  pl.BlockSpec((1, 1, 6 * D), mmap),
                  pl.BlockSpec((1, D), const), pl.BlockSpec((1, D), const),
                  pl.BlockSpec(memory_space=pl.ANY)],
        out_specs=[pl.BlockSpec((tm, D), lambda i: (jnp.minimum(i, nct - 1), 0)),
                   pl.BlockSpec((tm, D), lambda i: (jnp.maximum(i - nct, 0), 0))],
        out_shape=[jax.ShapeDtypeStruct((t_ctx, D), F32), jax.ShapeDtypeStruct((t_lat, D), F32)],
        scratch_shapes=[pltpu.VMEM((2, 2, tm, ROW_SUB, LANES), F32), pltpu.SemaphoreType.DMA((2,))],
        compiler_params=_cparams(("arbitrary",)),
        name="final",
    )(p0, p1, p0, p1, x1, rt, mod_all, l2g, l2b, ys)


def _reorder_q_heads(w, axis):
    shape = w.shape
    split = shape[:axis] + (N_KV_HEADS, N_Q_HEADS // N_KV_HEADS, HD) + shape[axis + 1:]
    return jnp.swapaxes(w.reshape(split), axis, axis + 1).reshape(shape)


def _rope_tables(seq):
    t = jnp.arange(seq, dtype=jnp.int32)
    row = (t // GRID_W).astype(F32)
    col = (t % GRID_W).astype(F32)
    half = HD // 4
    inv = ROPE_THETA ** (-jnp.arange(half, dtype=F32) / half)
    lane = np.arange(LANES)
    d64 = lane % HD
    use_row = jnp.asarray(d64 < HD // 2)
    freq = inv[jnp.asarray(d64 % half)]
    pos = jnp.where(use_row[None, :], row[:, None], col[:, None])
    ang = pos * freq[None, :]
    sign = jnp.asarray(np.where((d64 % 32) < 16, -1.0, 1.0), F32)
    return jnp.cos(ang), jnp.sin(ang) * sign[None, :]


def _pair_states(s):
    b = s.shape[0]
    s = s.reshape(b, 2, 2, GLA_DK, GLA_DV)
    return s.transpose(0, 1, 4, 2, 3).reshape(b, 2, GLA_DV, 2 * GLA_DK)


def _unpair_states(s):
    b = s.shape[0]
    s = s.reshape(b, 2, GLA_DV, 2, GLA_DK)
    return s.transpose(0, 1, 3, 4, 2).reshape(b, GLA_H, GLA_DK, GLA_DV)


def _route_tables(ert, counts, t):
    i32 = jnp.int32
    cnt = counts[0, :N_EXP].astype(i32)
    ends = jnp.cumsum(cnt)
    starts = ends - cnt
    table = lambda e: jnp.sum(jnp.where(e[None, :] == jnp.arange(N_EXP, dtype=i32)[:, None],
                                        starts[:, None], 0), axis=0)
    pos0 = table(ert[0].astype(i32)) + ert[4].astype(i32)
    pos1 = table(ert[1].astype(i32)) + ert[5].astype(i32)
    p = 2 * t
    nt = p // TM_EXP
    nv = nt + N_EXP
    tile_starts = jnp.arange(nt, dtype=i32) * TM_EXP
    idx_t = jnp.arange(nt, dtype=i32) + jnp.sum(starts[None, :] < tile_starts[:, None], axis=1, dtype=i32)
    idx_e = jnp.arange(N_EXP, dtype=i32) + jnp.minimum(starts // TM_EXP + 1, nt)
    k = jnp.arange(nv, dtype=i32)[:, None]
    bounds = (jnp.sum(jnp.where(idx_t[None, :] == k, tile_starts[None, :], 0), axis=1)
              + jnp.sum(jnp.where(idx_e[None, :] == k, starts[None, :], 0), axis=1))
    nxt = jnp.concatenate([bounds[1:], jnp.array([p], i32)])
    tile = jnp.minimum(bounds // TM_EXP, nt - 1)
    exp_id = jnp.minimum(jnp.sum(ends[None, :] <= bounds[:, None], axis=1, dtype=i32), N_EXP - 1)
    lo = bounds - tile * TM_EXP
    hi = jnp.where(nxt > bounds, jnp.minimum(nxt - tile * TM_EXP, TM_EXP), lo)
    prev_tile = jnp.concatenate([jnp.array([-1], i32), tile[:-1]])
    prev_exp = jnp.concatenate([jnp.array([-1], i32), exp_id[:-1]])
    first = (tile != prev_tile).astype(i32)
    newe = (exp_id != prev_exp).astype(i32)
    return pos0, pos1, (tile, exp_id, lo, hi, first, newe)


def kernel(x_prompt, x_sample, cache_k, cache_v, state_gla_fwd, state_gla_bwd, c, c_ctx, w_ada, b_ada, w_in, q_norm, k_norm, gla_w_gate, gla_b_gate, gla_norm, w_br_attn, w_br_gla, w_out, ln1_g, ln1_b, router_group_w, router_group_b, router_expert_w, router_expert_b, exp_w_gate, exp_w_up, exp_w_down, ln2_g, ln2_b):
    b_ctx, seq_ctx, _ = x_prompt.shape
    b_lat, seq_lat, _ = x_sample.shape
    t_ctx, t_lat = b_ctx * seq_ctx, b_lat * seq_lat
    t = t_ctx + t_lat
    l = 0

    rows = -(-(b_lat + 1) // 8) * 8
    c_rows = jnp.zeros((rows, D), F32).at[:b_lat].set(c).at[b_lat].set(c_ctx)
    mod = _ada(c_rows, w_ada[l], b_ada[l][None, :])
    mod_all = mod[:b_lat + 1, None, :]
    mod_lat = mod_all[:b_lat]
    mod_ctx = mod_all[b_lat:]

    w_full = w_in[l]
    w_a = jnp.concatenate([_reorder_q_heads(w_full[:, :AW], 1), w_full[:, AW:A_WIDTH]], axis=1).astype(BF16)
    w_mg = w_full[:, A_WIDTH:].astype(BF16)
    gain = jnp.concatenate([jnp.tile(q_norm[l], N_Q_HEADS), jnp.tile(k_norm[l], N_KV_HEADS)])[None, :]
    head_of = np.arange(AW + KVW) // HD
    ind = jnp.asarray((head_of[:, None] == np.arange(LANES)[None, :]) / HD, BF16)
    w_ba = _reorder_q_heads(w_br_attn[l], 0).astype(BF16)
    w_bg = w_br_gla[l].astype(BF16)
    w_o = w_out[l].astype(BF16)
    w_r = jnp.zeros((D, LANES), F32).at[:, :N_GROUPS].set(router_group_w[l])
    w_r = w_r.at[:, N_GROUPS:N_GROUPS + N_EXP].set(router_expert_w[l]).astype(BF16)
    b_r = jnp.zeros((1, LANES), F32).at[0, :N_GROUPS].set(router_group_b[l])
    b_r = b_r.at[0, N_GROUPS:N_GROUPS + N_EXP].set(router_expert_b[l])
    wg = jnp.zeros((2, 2 * GATE_RANK, GKW), F32)
    wg = wg.at[0, :GATE_RANK].set(gla_w_gate[l, 0]).at[1, GATE_RANK:].set(gla_w_gate[l, 1])
    bg = gla_b_gate[l][:, None, :]
    gn = gla_norm[l][None, :]

    xc = x_prompt.reshape(t_ctx, D)
    xl = x_sample.reshape(t_lat, D)

    q_c, k_c, v_c, qg_c, kg_c, vg_c, rs_c, lr_c, kf_c, vf_c = _inproj(
        xc, mod_ctx, w_a, gain, ind, None, seq_ctx, latent=False)
    attn_c = _attention(q_c, k_c, v_c, None, seq_ctx)
    zero_state = jnp.zeros((1, 2, 2, LANES, LANES), F32)
    gla_c, sfin_c = _gla(qg_c, kg_c, vg_c, lr_c, rs_c, zero_state, wg, bg, gn, seq_ctx)

    q_l, k_l, v_l, qg_l, kg_l, vg_l, rs_l, lr_l = _inproj(
        xl, mod_lat, w_a, gain, ind, _rope_tables(seq_lat), seq_lat, latent=True)
    past = cache_k.shape[2]
    kc = cache_k[:, l].reshape(b_lat, past, KVW).astype(BF16)
    vc = cache_v[:, l].reshape(b_lat, past, KVW).astype(BF16)
    attn_l = _attention(q_l, k_l, v_l, (kc, vc), seq_lat)
    s0 = jnp.stack([_pair_states(state_gla_fwd[:, l]), _pair_states(state_gla_bwd[:, l])], axis=1)
    gla_l, _ = _gla(qg_l, kg_l, vg_l, lr_l, rs_l, s0, wg, bg, gn, seq_lat)

    x1, h2p, rt, ert, counts = _post(xc, xl, attn_c, attn_l, gla_c, gla_l, mod_all, seq_lat,
                                w_mg, w_ba, w_bg, w_o, ln1_g[l][None, :], ln1_b[l][None, :], w_r, b_r)

    pos0, pos1, meta = _route_tables(ert, counts, t)
    xs = _scatter_rows(h2p, pos0, pos1)
    ys = _experts(xs, meta, exp_w_gate[l], exp_w_up[l], exp_w_down[l])
    y_ctx, y_lat = _final(x1, rt, pos0, pos1, mod_all, ln2_g[l][None, :], ln2_b[l][None, :], ys,
                          t_ctx, seq_lat)

    new_k, new_v = kf_c, vf_c
    new_sf = _unpair_states(sfin_c[:, 0])[:, None]
    new_sb = _unpair_states(sfin_c[:, 1])[:, None]
    return (y_ctx.reshape(b_ctx, seq_ctx, D), y_lat.reshape(b_lat, seq_lat, D),
            new_k, new_v, new_sf, new_sb)
```

```python
import functools

import numpy as np
import jax
import jax.numpy as jnp
from jax import lax
from jax.experimental import pallas as pl
from jax.experimental.pallas import tpu as pltpu

F32 = jnp.float32
BF16 = jnp.bfloat16
HIGHEST = lax.Precision.HIGHEST

D = 1024
GRID_W = 64
HD = 64
N_Q_HEADS = 8
N_KV_HEADS = 2
AW = N_Q_HEADS * HD
KVW = N_KV_HEADS * HD
ROPE_THETA = 10000.0
GLA_H = 4
GLA_DK = 64
GLA_DV = 128
GKW = GLA_H * GLA_DK
GVW = GLA_H * GLA_DV
GATE_RANK = 16
GLA_TAU = 16.0
CHUNK = 64
N_GROUPS = 4
EPG = 8
N_EXP = N_GROUPS * EPG
D_EXP = 256
DEPTH = 1
ALPHA = (2.0 * DEPTH) ** 0.25
LN_EPS = 1e-6
RMS_EPS = 1e-6

LANES = 128
A_WIDTH = AW + 2 * KVW + 2 * GKW + 2 * GVW + 2 * GATE_RANK
MG_WIDTH = 2 * D
TM_TOK = 512
TQ_LAT = 128
TM_EXP = 256
TS_ROWS = 256
VMEM_LIMIT = 56 * 1024 * 1024


def _cparams(sem):
    return pltpu.CompilerParams(dimension_semantics=sem, vmem_limit_bytes=VMEM_LIMIT)


def _dot(a, b):
    return jnp.dot(a, b, preferred_element_type=F32)


def _dot_nt(a, b):
    return lax.dot_general(a, b, (((1,), (1,)), ((), ())), preferred_element_type=F32)


def _dot_tn(a, b):
    return lax.dot_general(a, b, (((0,), (0,)), ((), ())), preferred_element_type=F32)


def _ln(x):
    mu = jnp.mean(x, axis=-1, keepdims=True)
    xc = x - mu
    var = jnp.mean(xc * xc, axis=-1, keepdims=True)
    return xc * lax.rsqrt(var + LN_EPS)


def _silu(x):
    return x * jax.nn.sigmoid(x)


def _split_bf16(x):
    hi = x.astype(BF16)
    lo = (x - hi.astype(F32)).astype(BF16)
    return hi, lo


def _ada_kernel(c_ref, w_ref, b_ref, o_ref):
    s = _silu(c_ref[...])
    o_ref[...] = jnp.dot(s, w_ref[...], preferred_element_type=F32, precision=HIGHEST) + b_ref[...]


def _ada(c_rows, w_ada, b_ada):
    rows = c_rows.shape[0]
    n = w_ada.shape[1]
    bn = 1024
    return pl.pallas_call(
        _ada_kernel,
        grid=(n // bn,),
        in_specs=[pl.BlockSpec((rows, D), lambda j: (0, 0)),
                  pl.BlockSpec((D, bn), lambda j: (0, j)),
                  pl.BlockSpec((1, bn), lambda j: (0, j))],
        out_specs=pl.BlockSpec((rows, bn), lambda j: (0, j)),
        out_shape=jax.ShapeDtypeStruct((rows, n), F32),
        compiler_params=_cparams(("arbitrary",)),
        name="ada",
    )(c_rows, w_ada, b_ada)


def _inproj_kernel(*refs, latent):
    if latent:
        (x_ref, mod_ref, w_ref, gain_ref, ind_ref, cos_ref, sin_ref,
         q_ref, k_ref, v_ref, qg_ref, kg_ref, vg_ref, rs_ref, lr_ref) = refs
    else:
        (x_ref, mod_ref, w_ref, gain_ref, ind_ref,
         q_ref, k_ref, v_ref, qg_ref, kg_ref, vg_ref, rs_ref, lr_ref, kf_ref, vf_ref) = refs
    tm = x_ref.shape[0]
    sh1 = mod_ref[0, :, 0:D]
    sc1 = mod_ref[0, :, D:2 * D]
    h = _ln(x_ref[...]) * (1.0 + sc1) + sh1
    res = _dot(h.astype(BF16), w_ref[...])

    qk = res[:, 0:AW + KVW]
    hi, lo = _split_bf16(qk * qk)
    ms = _dot(hi, ind_ref[...]) + _dot(lo, ind_ref[...])
    r = lax.rsqrt(ms + RMS_EPS)
    lane = lax.broadcasted_iota(jnp.int32, (tm, LANES), 1)
    low_half = lane < HD
    if latent:
        cos = cos_ref[...]
        sin = sin_ref[...]
        first = (lane % 32) < 16
    for s in range(5):
        rb = jnp.where(low_half, r[:, 2 * s:2 * s + 1], r[:, 2 * s + 1:2 * s + 2])
        y = qk[:, LANES * s:LANES * (s + 1)] * rb * gain_ref[:, LANES * s:LANES * (s + 1)]
        if s == 4 and not latent:
            for j in range(N_KV_HEADS):
                kf_ref[:, j, :] = y[:, HD * j:HD * (j + 1)]
        if latent:
            partner = jnp.where(first, pltpu.roll(y, LANES - 16, 1), pltpu.roll(y, 16, 1))
            y = y * cos + partner * sin
        if s < 4:
            q_ref[:, LANES * s:LANES * (s + 1)] = (y * (HD ** -0.5)).astype(BF16)
        else:
            k_ref[...] = y.astype(BF16)
    o = AW + KVW
    v = res[:, o:o + KVW]
    v_ref[...] = v.astype(BF16)
    if not latent:
        for j in range(N_KV_HEADS):
            vf_ref[:, j, :] = v[:, HD * j:HD * (j + 1)]
    o += KVW
    qg_ref[...] = res[:, o:o + GKW] * (GLA_DK ** -0.5)
    o += GKW
    kg_ref[...] = res[:, o:o + GKW]
    o += GKW
    vg_ref[...] = res[:, o:o + GVW].astype(BF16)
    o += GVW
    rs_ref[...] = _silu(res[:, o:o + GVW]).astype(BF16)
    o += GVW
    lr_ref[...] = res[:, o:o + 2 * GATE_RANK]


def _inproj(x2, mod, w_a, gain, ind, rope, seq, latent):
    t = x2.shape[0]
    tm = min(TM_TOK, seq)
    per_seq = seq // tm
    row = lambda i: (i, 0)
    const = lambda i: (0, 0)
    in_specs = [pl.BlockSpec((tm, D), row),
                pl.BlockSpec((1, 1, 6 * D), (lambda i: (i // per_seq, 0, 0)) if latent else (lambda i: (0, 0, 0))),
                pl.BlockSpec((D, A_WIDTH), const),
                pl.BlockSpec((1, AW + KVW), const),
                pl.BlockSpec((AW + KVW, LANES), const)]
    args = [x2, mod, w_a, gain, ind]
    if latent:
        in_specs += [pl.BlockSpec((tm, LANES), lambda i: (i % per_seq, 0))] * 2
        args += list(rope)
    widths = [(AW, BF16), (KVW, BF16), (KVW, BF16), (GKW, F32), (GKW, F32), (GVW, BF16), (GVW, BF16),
              (2 * GATE_RANK, F32)]
    out_specs = [pl.BlockSpec((tm, w), row) for w, _ in widths]
    out_shape = [jax.ShapeDtypeStruct((t, w), dt) for w, dt in widths]
    if not latent:
        cache_spec = pl.BlockSpec((None, None, tm, N_KV_HEADS, HD),
                                  lambda i: (i // per_seq, 0, i % per_seq, 0, 0))
        out_specs += [cache_spec] * 2
        out_shape += [jax.ShapeDtypeStruct((t // seq, 1, seq, N_KV_HEADS, HD), F32)] * 2
    return pl.pallas_call(
        functools.partial(_inproj_kernel, latent=latent),
        grid=(t // tm,),
        in_specs=in_specs,
        out_specs=out_specs,
        out_shape=out_shape,
        compiler_params=_cparams(("parallel",)),
        name="inproj_lat" if latent else "inproj_ctx",
    )(*args)


def _attn_kernel(*refs, has_cache):
    if has_cache:
        q_ref, k_ref, v_ref, kc_ref, vc_ref, o_ref = refs
    else:
        q_ref, k_ref, v_ref, o_ref = refs
    tq = q_ref.shape[0]
    lane = lax.broadcasted_iota(jnp.int32, (tq, LANES), 1)
    low_half = lane < HD
    k = k_ref[...]
    v = v_ref[...]
    outs = []
    for j in range(N_KV_HEADS):
        keep = low_half if j == 0 else jnp.logical_not(low_half)
        zero = jnp.zeros((tq, LANES), BF16)
        qs = jnp.concatenate(
            [jnp.where(keep, q_ref[:, LANES * s:LANES * (s + 1)], zero) for s in range(4)], axis=0)
        s1 = _dot_nt(qs, k)
        m = jnp.max(s1, axis=-1, keepdims=True)
        if has_cache:
            s2 = _dot_nt(qs, kc_ref[...])
            m = jnp.maximum(m, jnp.max(s2, axis=-1, keepdims=True))
        p1 = jnp.exp(s1 - m)
        den = jnp.sum(p1, axis=-1, keepdims=True)
        acc = _dot(p1.astype(BF16), v)
        if has_cache:
            p2 = jnp.exp(s2 - m)
            den = den + jnp.sum(p2, axis=-1, keepdims=True)
            acc = acc + _dot(p2.astype(BF16), vc_ref[...])
        outs.append(acc / den)
    for s in range(4):
        o_ref[:, LANES * s:LANES * (s + 1)] = jnp.where(
            low_half, outs[0][s * tq:(s + 1) * tq], outs[1][s * tq:(s + 1) * tq]).astype(BF16)


def _attention(q, k, v, cache, seq):
    t = q.shape[0]
    if cache is None:
        tq = seq
        grid = (t // seq,)
        qmap = lambda b: (b, 0)
        in_specs = [pl.BlockSpec((tq, AW), qmap), pl.BlockSpec((seq, KVW), qmap),
                    pl.BlockSpec((seq, KVW), qmap)]
        args = [q, k, v]
        sem = ("parallel",)
        name = "attn_ctx"
    else:
        tq = TQ_LAT
        nq = seq // tq
        kc, vc = cache
        past = kc.shape[1]
        grid = (t // seq, nq)
        qmap = lambda b, i: (b * nq + i, 0)
        kmap = lambda b, i: (b, 0)
        cmap = lambda b, i: (b, 0, 0)
        in_specs = [pl.BlockSpec((tq, AW), qmap), pl.BlockSpec((seq, KVW), kmap),
                    pl.BlockSpec((seq, KVW), kmap),
                    pl.BlockSpec((None, past, KVW), cmap), pl.BlockSpec((None, past, KVW), cmap)]
        args = [q, k, v, kc, vc]
        sem = ("parallel", "arbitrary")
        name = "attn_lat"
    return pl.pallas_call(
        functools.partial(_attn_kernel, has_cache=cache is not None),
        grid=grid,
        in_specs=in_specs,
        out_specs=pl.BlockSpec((tq, AW), qmap),
        out_shape=jax.ShapeDtypeStruct((t, AW), BF16),
        compiler_params=_cparams(sem),
        name=name,
    )(*args)


def _gla_kernel(qg_ref, kg_ref, vg_ref, lr_ref, rs_ref, s0_ref, wg_ref, bg_ref, gn_ref,
                o_ref, sfin_ref, cum_s, kv_s, dec_s, acc_s):
    n = qg_ref.shape[0]
    nc = n // CHUNK
    lane = lax.broadcasted_iota(jnp.int32, (CHUNK, LANES), 1)
    low_half = lane < GLA_DK
    lane_sq = lax.broadcasted_iota(jnp.int32, (LANES, LANES), 1)
    low_half_sq = lane_sq < GLA_DK
    ri = lax.broadcasted_iota(jnp.int32, (CHUNK, CHUNK), 0)
    ci = lax.broadcasted_iota(jnp.int32, (CHUNK, CHUNK), 1)
    keep = (ri >= ci, ci >= ri)
    tri = tuple(m.astype(F32) for m in keep)

    lr = lr_ref[...]
    for d in range(2):
        z = jnp.dot(lr, wg_ref[d], preferred_element_type=F32, precision=HIGHEST) + bg_ref[d]
        cum_s[d] = (jnp.minimum(z, 0.0) - jnp.log1p(jnp.exp(-jnp.abs(z)))) * (1.0 / GLA_TAU)

    def pass1(c, carry):
        rows = pl.ds(pl.multiple_of(c * CHUNK, CHUNK), CHUNK)
        kc = kg_ref[rows, :]
        for d in range(2):
            cum = jnp.dot(tri[d], cum_s[d, rows, :], preferred_element_type=F32, precision=HIGHEST)
            cum_s[d, rows, :] = cum
            last = cum[CHUNK - 1:CHUNK, :] if d == 0 else cum[0:1, :]
            dec_s[d, c] = jnp.exp(last)
            kdec = (kc * jnp.exp(last - cum)).astype(BF16)
            for p in range(2):
                slab = kdec[:, LANES * p:LANES * (p + 1)]
                va = vg_ref[rows, GLA_DV * (2 * p):GLA_DV * (2 * p + 1)]
                vb = vg_ref[rows, GLA_DV * (2 * p + 1):GLA_DV * (2 * p + 2)]
                kv_s[d, c, p] = jnp.where(low_half_sq, _dot_tn(va, slab), _dot_tn(vb, slab))
        return carry

    lax.fori_loop(0, nc, pass1, 0)

    for d in range(2):
        def scan(i, st):
            c = i if d == 0 else nc - 1 - i
            dec = dec_s[d, c]
            new = []
            for p in range(2):
                kv = kv_s[d, c, p]
                kv_s[d, c, p] = st[p]
                new.append(st[p] * dec[:, LANES * p:LANES * (p + 1)] + kv)
            return tuple(new)

        fin = lax.fori_loop(0, nc, scan, (s0_ref[d, 0], s0_ref[d, 1]))
        sfin_ref[d, 0] = fin[0]
        sfin_ref[d, 1] = fin[1]

    for d in range(2):
        def pass3(c, carry):
            rows = pl.ds(pl.multiple_of(c * CHUNK, CHUNK), CHUNK)
            cum = cum_s[d, rows, :]
            qt = qg_ref[rows, :] * jnp.exp(cum)
            kt = (kg_ref[rows, :] * jnp.exp(-cum)).astype(BF16)
            for h in range(GLA_H):
                p = h // 2
                sel = low_half if h % 2 == 0 else jnp.logical_not(low_half)
                qh = jnp.where(sel, qt[:, LANES * p:LANES * (p + 1)], 0.0).astype(BF16)
                a = jnp.where(keep[d], _dot_nt(qh, kt[:, LANES * p:LANES * (p + 1)]), 0.0)
                vh = vg_ref[rows, GLA_DV * h:GLA_DV * (h + 1)]
                oh = _dot(a.astype(BF16), vh) + _dot_nt(qh, kv_s[d, c, p].astype(BF16))
                cols = slice(GLA_DV * h, GLA_DV * (h + 1))
                if d == 0:
                    acc_s[rows, cols] = oh
                else:
                    tot = acc_s[rows, cols] + oh
                    y = tot * lax.rsqrt(jnp.mean(tot * tot, axis=-1, keepdims=True) + RMS_EPS)
                    y = (y * gn_ref[...]).astype(BF16)
                    o_ref[rows, cols] = (y * rs_ref[rows, cols]).astype(BF16)
            return carry

        lax.fori_loop(0, nc, pass3, 0)


def _gla(qg, kg, vg, lr, rs, s0, wg, bg, gn, seq):
    t = qg.shape[0]
    nb = t // seq
    nc = seq // CHUNK
    row = lambda b: (b, 0)
    c3 = lambda b: (0, 0, 0)
    st = lambda b: (b, 0, 0, 0, 0)
    if s0.shape[0] == 1:
        s0map = lambda b: (0, 0, 0, 0, 0)
    else:
        s0map = st
    return pl.pallas_call(
        _gla_kernel,
        grid=(nb,),
        in_specs=[pl.BlockSpec((seq, GKW), row), pl.BlockSpec((seq, GKW), row),
                  pl.BlockSpec((seq, GVW), row), pl.BlockSpec((seq, 2 * GATE_RANK), row),
                  pl.BlockSpec((seq, GVW), row),
                  pl.BlockSpec((None, 2, 2, LANES, LANES), s0map),
                  pl.BlockSpec((2, 2 * GATE_RANK, GKW), c3), pl.BlockSpec((2, 1, GKW), c3),
                  pl.BlockSpec((1, GLA_DV), lambda b: (0, 0))],
        out_specs=[pl.BlockSpec((seq, GVW), row),
                   pl.BlockSpec((None, 2, 2, LANES, LANES), st)],
        out_shape=[jax.ShapeDtypeStruct((t, GVW), BF16),
                   jax.ShapeDtypeStruct((nb, 2, 2, LANES, LANES), F32)],
        scratch_shapes=[pltpu.VMEM((2, seq, GKW), F32),
                        pltpu.VMEM((2, nc, 2, LANES, LANES), F32),
                        pltpu.VMEM((2, nc, 1, GKW), F32),
                        pltpu.VMEM((seq, GVW), F32)],
        compiler_params=_cparams(("parallel",)),
        name="gla_lat" if seq > 256 else "gla_ctx",
    )(qg, kg, vg, lr, rs, s0, wg, bg, gn)


ROW_SUB = D // LANES


def _store_row_slabs(ref, x):
    m = x.shape[0]
    for c in range(ROW_SUB):
        ref[pl.ds(c, m, stride=ROW_SUB), :] = x[:, LANES * c:LANES * (c + 1)]


def _load_row_slabs(ref):
    m = ref.shape[0] // ROW_SUB
    return jnp.concatenate([ref[pl.ds(c, m, stride=ROW_SUB), :] for c in range(ROW_SUB)], axis=1)


def _row_slab(ref, row):
    return ref.at[pl.ds(pl.multiple_of(row * ROW_SUB, ROW_SUB), ROW_SUB), :]


def _post_kernel(xc_ref, xl_ref, ac_ref, al_ref, gc_ref, gl_ref, mod_ref,
                 wmg_ref, wba_ref, wbg_ref, wo_ref, l1g_ref, l1b_ref, wr_ref, br_ref,
                 x1_ref, h2_ref, rt_ref, ert_ref, cnt_ref, run_s, *, n_ctx_tiles):
    i = pl.program_id(0)
    tm = xc_ref.shape[0]
    is_ctx = i < n_ctx_tiles
    x = jnp.where(is_ctx, xc_ref[...], xl_ref[...])
    attn = jnp.where(is_ctx, ac_ref[...], al_ref[...])
    gla = jnp.where(is_ctx, gc_ref[...], gl_ref[...])
    sh1 = mod_ref[0, :, 0:D]
    sc1 = mod_ref[0, :, D:2 * D]
    g1 = mod_ref[0, :, 2 * D:3 * D]
    sh2 = mod_ref[0, :, 3 * D:4 * D]
    sc2 = mod_ref[0, :, 4 * D:5 * D]
    h = (_ln(x) * (1.0 + sc1) + sh1).astype(BF16)
    gates = jax.nn.sigmoid(_dot(h, wmg_ref[...]))
    merged = gates[:, :D] * _dot(attn, wba_ref[...]) + gates[:, D:] * _dot(gla, wbg_ref[...])
    mix = _dot(merged.astype(BF16), wo_ref[...])
    x1 = _ln(ALPHA * x + g1 * mix) * l1g_ref[...] + l1b_ref[...]
    x1_ref[...] = x1
    h2 = _ln(x1) * (1.0 + sc2) + sh2
    _store_row_slabs(h2_ref, h2)

    logit = _dot(h2.astype(BF16), wr_ref[...]) + br_ref[...]
    lane_i = lax.broadcasted_iota(jnp.int32, (tm, LANES), 1)
    lane = lane_i.astype(F32)
    lane_grp = ((lane_i - N_GROUPS) >> 3).astype(F32)
    neg = jnp.float32(-jnp.inf)
    far = jnp.float32(LANES)
    is_g = lane_i < N_GROUPS
    lg = jnp.where(is_g, logit, neg)
    mg = jnp.max(lg, axis=-1, keepdims=True)
    pg_top = 1.0 / jnp.sum(jnp.where(is_g, jnp.exp(logit - mg), 0.0), axis=-1, keepdims=True)
    g_idx = jnp.min(jnp.where(lg == mg, lane, far), axis=-1, keepdims=True)
    in_grp = (lane_i >= N_GROUPS) & (lane_i < N_GROUPS + N_EXP) & (lane_grp == g_idx)
    le = jnp.where(in_grp, logit, neg)
    v1 = jnp.max(le, axis=-1, keepdims=True)
    i1 = jnp.min(jnp.where(le == v1, lane, far), axis=-1, keepdims=True)
    le2 = jnp.where(lane == i1, neg, le)
    v2 = jnp.max(le2, axis=-1, keepdims=True)
    i2 = jnp.min(jnp.where(le2 == v2, lane, far), axis=-1, keepdims=True)
    e1 = i1 - N_GROUPS
    e2 = i2 - N_GROUPS
    tt = jnp.exp(v2 - v1)
    w1 = pg_top / (1.0 + tt)
    w2 = pg_top * tt / (1.0 + tt)

    @pl.when(i == 0)
    def _():
        run_s[...] = jnp.zeros_like(run_s)

    hot = ((lane == e1) | (lane == e2)).astype(F32)
    ri = lax.broadcasted_iota(jnp.int32, (tm, tm), 0)
    ci = lax.broadcasted_iota(jnp.int32, (tm, tm), 1)
    before = _dot((ri > ci).astype(BF16), hot.astype(BF16)) + run_s[0:1, :]
    r1 = jnp.sum(jnp.where(lane == e1, before, 0.0), axis=-1, keepdims=True)
    r2 = jnp.sum(jnp.where(lane == e2, before, 0.0), axis=-1, keepdims=True)
    run_s[0:1, :] = run_s[0:1, :] + jnp.sum(hot, axis=0, keepdims=True)
    cnt_ref[...] = jnp.broadcast_to(run_s[0:1, :], cnt_ref.shape)

    rt = jnp.where(lane_i == 0, e1, 0.0)
    rt = jnp.where(lane_i == 1, e2, rt)
    rt = jnp.where(lane_i == 2, w1, rt)
    rt = jnp.where(lane_i == 3, w2, rt)
    rt = jnp.where(lane_i == 4, r1, rt)
    rt = jnp.where(lane_i == 5, r2, rt)
    rt_ref[...] = rt
    ert_ref[...] = rt.T[0:8, :]


def _post(x_ctx, x_lat, a_ctx, a_lat, g_ctx, g_lat, mod_all, seq_lat,
          w_mg, w_ba, w_bg, w_o, l1g, l1b, w_r, b_r):
    t_ctx, t_lat = x_ctx.shape[0], x_lat.shape[0]
    tm = TM_TOK
    nct, nlt = t_ctx // tm, t_lat // tm
    per_seq = seq_lat // tm
    nb_lat = t_lat // seq_lat
    t = t_ctx + t_lat
    cmap = lambda i: (jnp.minimum(i, nct - 1), 0)
    lmap = lambda i: (jnp.maximum(i - nct, 0), 0)
    mmap = lambda i: (jnp.where(i < nct, nb_lat, jnp.maximum(i - nct, 0) // per_seq), 0, 0)
    row = lambda i: (i, 0)
    const = lambda i: (0, 0)
    return pl.pallas_call(
        functools.partial(_post_kernel, n_ctx_tiles=nct),
        grid=(nct + nlt,),
        in_specs=[pl.BlockSpec((tm, D), cmap), pl.BlockSpec((tm, D), lmap),
                  pl.BlockSpec((tm, AW), cmap), pl.BlockSpec((tm, AW), lmap),
                  pl.BlockSpec((tm, GVW), cmap), pl.BlockSpec((tm, GVW), lmap),
                  pl.BlockSpec((1, 1, 6 * D), mmap),
                  pl.BlockSpec((D, MG_WIDTH), const), pl.BlockSpec((AW, D), const),
                  pl.BlockSpec((GVW, D), const), pl.BlockSpec((D, D), const),
                  pl.BlockSpec((1, D), const), pl.BlockSpec((1, D), const),
                  pl.BlockSpec((D, LANES), const), pl.BlockSpec((1, LANES), const)],
        out_specs=[pl.BlockSpec((tm, D), row), pl.BlockSpec((tm * ROW_SUB, LANES), row),
                   pl.BlockSpec((tm, LANES), row), pl.BlockSpec((8, tm), lambda i: (0, i)),
                   pl.BlockSpec((8, LANES), const)],
        out_shape=[jax.ShapeDtypeStruct((t, D), F32), jax.ShapeDtypeStruct((t * ROW_SUB, LANES), F32),
                   jax.ShapeDtypeStruct((t, LANES), F32), jax.ShapeDtypeStruct((8, t), F32),
                   jax.ShapeDtypeStruct((8, LANES), F32)],
        scratch_shapes=[pltpu.VMEM((8, LANES), F32)],
        compiler_params=_cparams(("arbitrary",)),
        name="post",
    )(x_ctx, x_lat, a_ctx, a_lat, g_ctx, g_lat, mod_all, w_mg, w_ba, w_bg, w_o, l1g, l1b, w_r, b_r)


ROW_UNROLL = 8


def _row_copy(src_ref, dst_ref, sem):
    return pltpu.make_async_copy(src_ref, dst_ref, sem)


def _scatter_kernel(pos0_ref, pos1_ref, h_ref, xs_ref, sem):
    ts = h_ref.shape[0] // ROW_SUB

    def issue(g, carry):
        r0 = pl.multiple_of(g * ROW_UNROLL, ROW_UNROLL)
        for k in range(ROW_UNROLL):
            src = _row_slab(h_ref, r0 + k)
            _row_copy(src, _row_slab(xs_ref, pos0_ref[0, r0 + k]), sem).start(priority=0)
            _row_copy(src, _row_slab(xs_ref, pos1_ref[0, r0 + k]), sem).start(priority=1)
        return carry

    lax.fori_loop(0, ts // ROW_UNROLL, issue, 0)
    for _ in range(2):
        _row_copy(h_ref, xs_ref.at[pl.ds(0, ts * ROW_SUB), :], sem).wait()


def _scatter_rows(h2p, pos0, pos1):
    t = h2p.shape[0] // ROW_SUB
    ts = TS_ROWS
    smem = lambda: pl.BlockSpec((None, 1, ts), lambda i: (i, 0, 0), memory_space=pltpu.SMEM)
    return pl.pallas_call(
        _scatter_kernel,
        grid=(t // ts,),
        in_specs=[smem(), smem(), pl.BlockSpec((ts * ROW_SUB, LANES), lambda i: (i, 0))],
        out_specs=pl.BlockSpec(memory_space=pl.ANY),
        out_shape=jax.ShapeDtypeStruct((2 * t * ROW_SUB, LANES), F32),
        scratch_shapes=[pltpu.SemaphoreType.DMA(())],
        compiler_params=_cparams(("arbitrary",)),
        name="scatter",
    )(pos0.reshape(t // ts, 1, ts), pos1.reshape(t // ts, 1, ts), h2p)


def _expert_kernel(vt_ref, ve_ref, lo_ref, hi_ref, first_ref, last_ref, newe_ref,
                   xs_ref, wg_ref, wu_ref, wd_ref, y_ref, wgu_s, wd_s, acc_s):
    v = pl.program_id(0)
    tm = acc_s.shape[0]

    @pl.when(newe_ref[v] == 1)
    def _():
        wgu_s[:, 0:D_EXP] = wg_ref[...].astype(BF16)
        wgu_s[:, D_EXP:2 * D_EXP] = wu_ref[...].astype(BF16)
        wd_s[...] = wd_ref[...].astype(BF16)

    @pl.when(first_ref[v] == 1)
    def _():
        acc_s[...] = jnp.zeros_like(acc_s)

    lo = lo_ref[v]
    hi = hi_ref[v]

    @pl.when(hi > lo)
    def _():
        gu = _dot(_load_row_slabs(xs_ref).astype(BF16), wgu_s[...])
        hid = _silu(gu[:, 0:D_EXP]) * gu[:, D_EXP:2 * D_EXP]
        y = _dot(hid.astype(BF16), wd_s[...])
        rid = lax.broadcasted_iota(jnp.int32, (tm, D), 0)
        acc_s[...] = jnp.where((rid >= lo) & (rid < hi), y, acc_s[...])

    @pl.when(last_ref[v] == 1)
    def _():
        _store_row_slabs(y_ref, acc_s[...])


def _experts(xs, meta, w_gate, w_up, w_down):
    p = xs.shape[0] // ROW_SUB
    nv = meta[0].shape[0]
    tm = TM_EXP
    xmap = lambda v, vt, ve, lo, hi, fi, la, ne: (vt[v], 0)
    wmap = lambda v, vt, ve, lo, hi, fi, la, ne: (ve[v], 0, 0)
    return pl.pallas_call(
        _expert_kernel,
        grid_spec=pltpu.PrefetchScalarGridSpec(
            num_scalar_prefetch=7,
            grid=(nv,),
            in_specs=[pl.BlockSpec((tm * ROW_SUB, LANES), xmap),
                      pl.BlockSpec((None, D, D_EXP), wmap), pl.BlockSpec((None, D, D_EXP), wmap),
                      pl.BlockSpec((None, D_EXP, D), wmap)],
            out_specs=pl.BlockSpec((tm * ROW_SUB, LANES), xmap),
            scratch_shapes=[pltpu.VMEM((D, 2 * D_EXP), BF16), pltpu.VMEM((D_EXP, D), BF16),
                            pltpu.VMEM((tm, D), F32)]),
        out_shape=jax.ShapeDtypeStruct((p * ROW_SUB, LANES), F32),
        compiler_params=_cparams(("arbitrary",)),
        name="experts",
    )(*meta, xs, w_gate, w_up, w_down)


def _final_kernel(p0c_ref, p1c_ref, p0n_ref, p1n_ref, x1_ref, rt_ref, mod_ref, l2g_ref, l2b_ref, ys_ref,
                  oc_ref, ol_ref, buf, sem, *, n_ctx_tiles):
    i = pl.program_id(0)
    n = pl.num_programs(0)
    tm = x1_ref.shape[0]

    def gather(p0_ref, p1_ref, slot):
        def issue(g, carry):
            r0 = pl.multiple_of(g * ROW_UNROLL, ROW_UNROLL)
            for k in range(ROW_UNROLL):
                _row_copy(_row_slab(ys_ref, p0_ref[0, r0 + k]),
                          _row_slab(buf.at[slot, 0], r0 + k), sem.at[slot]).start(priority=0)
                _row_copy(_row_slab(ys_ref, p1_ref[0, r0 + k]),
                          _row_slab(buf.at[slot, 1], r0 + k), sem.at[slot]).start(priority=1)
            return carry

        lax.fori_loop(0, tm // ROW_UNROLL, issue, 0)

    cur = i % 2

    @pl.when(i == 0)
    def _():
        gather(p0c_ref, p1c_ref, 0)

    @pl.when(i + 1 < n)
    def _():
        gather(p0n_ref, p1n_ref, 1 - cur)

    for k in range(2):
        _row_copy(ys_ref.at[pl.ds(0, tm * ROW_SUB), :], buf.at[cur, k], sem.at[cur]).wait()

    g2 = mod_ref[0, :, 5 * D:6 * D]
    w1 = rt_ref[:, 2:3]
    w2 = rt_ref[:, 3:4]
    moe = w1 * _load_row_slabs(buf.at[cur, 0]) + w2 * _load_row_slabs(buf.at[cur, 1])
    out = _ln(ALPHA * x1_ref[...] + g2 * moe) * l2g_ref[...] + l2b_ref[...]

    @pl.when(i < n_ctx_tiles)
    def _():
        oc_ref[...] = out

    @pl.when(i >= n_ctx_tiles)
    def _():
        ol_ref[...] = out


def _final(x1, rt, pos0, pos1, mod_all, l2g, l2b, ys, t_ctx, seq_lat):
    t = x1.shape[0]
    tm = TM_TOK
    nt = t // tm
    nct = t_ctx // tm
    t_lat = t - t_ctx
    per_seq = seq_lat // tm
    nb_lat = t_lat // seq_lat
    p0 = pos0.reshape(nt, 1, tm)
    p1 = pos1.reshape(nt, 1, tm)
    row = lambda i: (i, 0)
    const = lambda i: (0, 0)
    mmap = lambda i: (jnp.where(i < nct, nb_lat, jnp.maximum(i - nct, 0) // per_seq), 0, 0)
    smem_cur = lambda: pl.BlockSpec((None, 1, tm), lambda i: (i, 0, 0), memory_space=pltpu.SMEM)
    smem_nxt = lambda: pl.BlockSpec((None, 1, tm), lambda i: (jnp.minimum(i + 1, nt - 1), 0, 0),
                                    memory_space=pltpu.SMEM)
    return pl.pallas_call(
        functools.partial(_final_kernel, n_ctx_tiles=nct),
        grid=(nt,),
        in_specs=[smem_cur(), smem_cur(), smem_nxt(), smem_nxt(),
                  pl.BlockSpec((tm, D), row), pl.BlockSpec((tm, LANES), row),
                  pl.BlockSpec((1, 1, 6 * D), mmap),
                  pl.BlockSpec((1, D), const), pl.BlockSpec((1, D), const),
                  pl.BlockSpec(memory_space=pl.ANY)],
        out_specs=[pl.BlockSpec((tm, D), lambda i: (jnp.minimum(i, nct - 1), 0)),
                   pl.BlockSpec((tm, D), lambda i: (jnp.maximum(i - nct, 0), 0))],
        out_shape=[jax.ShapeDtypeStruct((t_ctx, D), F32), jax.ShapeDtypeStruct((t_lat, D), F32)],
        scratch_shapes=[pltpu.VMEM((2, 2, tm * ROW_SUB, LANES), F32), pltpu.SemaphoreType.DMA((2,))],
        compiler_params=_cparams(("arbitrary",)),
        name="final",
    )(p0, p1, p0, p1, x1, rt, mod_all, l2g, l2b, ys)


def _reorder_q_heads(w, axis):
    shape = w.shape
    split = shape[:axis] + (N_KV_HEADS, N_Q_HEADS // N_KV_HEADS, HD) + shape[axis + 1:]
    return jnp.swapaxes(w.reshape(split), axis, axis + 1).reshape(shape)


def _rope_tables(seq):
    t = jnp.arange(seq, dtype=jnp.int32)
    row = (t // GRID_W).astype(F32)
    col = (t % GRID_W).astype(F32)
    half = HD // 4
    inv = ROPE_THETA ** (-jnp.arange(half, dtype=F32) / half)
    lane = np.arange(LANES)
    d64 = lane % HD
    use_row = jnp.asarray(d64 < HD // 2)
    freq = inv[jnp.asarray(d64 % half)]
    pos = jnp.where(use_row[None, :], row[:, None], col[:, None])
    ang = pos * freq[None, :]
    sign = jnp.asarray(np.where((d64 % 32) < 16, -1.0, 1.0), F32)
    return jnp.cos(ang), jnp.sin(ang) * sign[None, :]


def _pair_states(s):
    b = s.shape[0]
    s = s.reshape(b, 2, 2, GLA_DK, GLA_DV)
    return s.transpose(0, 1, 4, 2, 3).reshape(b, 2, GLA_DV, 2 * GLA_DK)


def _unpair_states(s):
    b = s.shape[0]
    s = s.reshape(b, 2, GLA_DV, 2, GLA_DK)
    return s.transpose(0, 1, 3, 4, 2).reshape(b, GLA_H, GLA_DK, GLA_DV)


def _route_tables(ert, counts, t):
    i32 = jnp.int32
    cnt = counts[0, :N_EXP].astype(i32)
    ends = jnp.cumsum(cnt)
    starts = ends - cnt
    table = lambda e: jnp.sum(jnp.where(e[None, :] == jnp.arange(N_EXP, dtype=i32)[:, None],
                                        starts[:, None], 0), axis=0)
    pos0 = table(ert[0].astype(i32)) + ert[4].astype(i32)
    pos1 = table(ert[1].astype(i32)) + ert[5].astype(i32)
    p = 2 * t
    nt = p // TM_EXP
    nv = nt + N_EXP
    tile_starts = jnp.arange(nt, dtype=i32) * TM_EXP
    idx_t = jnp.arange(nt, dtype=i32) + jnp.sum(starts[None, :] < tile_starts[:, None], axis=1, dtype=i32)
    idx_e = jnp.arange(N_EXP, dtype=i32) + jnp.minimum(starts // TM_EXP + 1, nt)
    k = jnp.arange(nv, dtype=i32)[:, None]
    bounds = (jnp.sum(jnp.where(idx_t[None, :] == k, tile_starts[None, :], 0), axis=1)
              + jnp.sum(jnp.where(idx_e[None, :] == k, starts[None, :], 0), axis=1))
    nxt = jnp.concatenate([bounds[1:], jnp.array([p], i32)])
    tile = jnp.minimum(bounds // TM_EXP, nt - 1)
    exp_id = jnp.minimum(jnp.sum(ends[None, :] <= bounds[:, None], axis=1, dtype=i32), N_EXP - 1)
    lo = bounds - tile * TM_EXP
    hi = jnp.where(nxt > bounds, jnp.minimum(nxt - tile * TM_EXP, TM_EXP), lo)
    prev_tile = jnp.concatenate([jnp.array([-1], i32), tile[:-1]])
    prev_exp = jnp.concatenate([jnp.array([-1], i32), exp_id[:-1]])
    next_tile = jnp.concatenate([tile[1:], jnp.array([-1], i32)])
    first = (tile != prev_tile).astype(i32)
    last = (tile != next_tile).astype(i32)
    newe = (exp_id != prev_exp).astype(i32)
    return pos0, pos1, (tile, exp_id, lo, hi, first, last, newe)


def kernel(x_prompt, x_sample, cache_k, cache_v, state_gla_fwd, state_gla_bwd, c, c_ctx, w_ada, b_ada, w_in, q_norm, k_norm, gla_w_gate, gla_b_gate, gla_norm, w_br_attn, w_br_gla, w_out, ln1_g, ln1_b, router_group_w, router_group_b, router_expert_w, router_expert_b, exp_w_gate, exp_w_up, exp_w_down, ln2_g, ln2_b):
    b_ctx, seq_ctx, _ = x_prompt.shape
    b_lat, seq_lat, _ = x_sample.shape
    t_ctx, t_lat = b_ctx * seq_ctx, b_lat * seq_lat
    t = t_ctx + t_lat
    l = 0

    rows = -(-(b_lat + 1) // 8) * 8
    c_rows = jnp.zeros((rows, D), F32).at[:b_lat].set(c).at[b_lat].set(c_ctx)
    mod = _ada(c_rows, w_ada[l], b_ada[l][None, :])
    mod_all = mod[:b_lat + 1, None, :]
    mod_lat = mod_all[:b_lat]
    mod_ctx = mod_all[b_lat:]

    w_full = w_in[l]
    w_a = jnp.concatenate([_reorder_q_heads(w_full[:, :AW], 1), w_full[:, AW:A_WIDTH]], axis=1).astype(BF16)
    w_mg = w_full[:, A_WIDTH:].astype(BF16)
    gain = jnp.concatenate([jnp.tile(q_norm[l], N_Q_HEADS), jnp.tile(k_norm[l], N_KV_HEADS)])[None, :]
    head_of = np.arange(AW + KVW) // HD
    ind = jnp.asarray((head_of[:, None] == np.arange(LANES)[None, :]) / HD, BF16)
    w_ba = _reorder_q_heads(w_br_attn[l], 0).astype(BF16)
    w_bg = w_br_gla[l].astype(BF16)
    w_o = w_out[l].astype(BF16)
    w_r = jnp.zeros((D, LANES), F32).at[:, :N_GROUPS].set(router_group_w[l])
    w_r = w_r.at[:, N_GROUPS:N_GROUPS + N_EXP].set(router_expert_w[l]).astype(BF16)
    b_r = jnp.zeros((1, LANES), F32).at[0, :N_GROUPS].set(router_group_b[l])
    b_r = b_r.at[0, N_GROUPS:N_GROUPS + N_EXP].set(router_expert_b[l])
    wg = jnp.zeros((2, 2 * GATE_RANK, GKW), F32)
    wg = wg.at[0, :GATE_RANK].set(gla_w_gate[l, 0]).at[1, GATE_RANK:].set(gla_w_gate[l, 1])
    bg = gla_b_gate[l][:, None, :]
    gn = gla_norm[l][None, :]

    xc = x_prompt.reshape(t_ctx, D)
    xl = x_sample.reshape(t_lat, D)

    q_c, k_c, v_c, qg_c, kg_c, vg_c, rs_c, lr_c, kf_c, vf_c = _inproj(
        xc, mod_ctx, w_a, gain, ind, None, seq_ctx, latent=False)
    attn_c = _attention(q_c, k_c, v_c, None, seq_ctx)
    zero_state = jnp.zeros((1, 2, 2, LANES, LANES), F32)
    gla_c, sfin_c = _gla(qg_c, kg_c, vg_c, lr_c, rs_c, zero_state, wg, bg, gn, seq_ctx)

    q_l, k_l, v_l, qg_l, kg_l, vg_l, rs_l, lr_l = _inproj(
        xl, mod_lat, w_a, gain, ind, _rope_tables(seq_lat), seq_lat, latent=True)
    past = cache_k.shape[2]
    kc = cache_k[:, l].reshape(b_lat, past, KVW).astype(BF16)
    vc = cache_v[:, l].reshape(b_lat, past, KVW).astype(BF16)
    attn_l = _attention(q_l, k_l, v_l, (kc, vc), seq_lat)
    s0 = jnp.stack([_pair_states(state_gla_fwd[:, l]), _pair_states(state_gla_bwd[:, l])], axis=1)
    gla_l, _ = _gla(qg_l, kg_l, vg_l, lr_l, rs_l, s0, wg, bg, gn, seq_lat)

    x1, h2p, rt, ert, counts = _post(xc, xl, attn_c, attn_l, gla_c, gla_l, mod_all, seq_lat,
                                w_mg, w_ba, w_bg, w_o, ln1_g[l][None, :], ln1_b[l][None, :], w_r, b_r)

    pos0, pos1, meta = _route_tables(ert, counts, t)
    xs = _scatter_rows(h2p, pos0, pos1)
    ys = _experts(xs, meta, exp_w_gate[l], exp_w_up[l], exp_w_down[l])
    y_ctx, y_lat = _final(x1, rt, pos0, pos1, mod_all, ln2_g[l][None, :], ln2_b[l][None, :], ys,
                          t_ctx, seq_lat)

    new_k, new_v = kf_c, vf_c
    new_sf = _unpair_states(sfin_c[:, 0])[:, None]
    new_sb = _unpair_states(sfin_c[:, 1])[:, None]
    return (y_ctx.reshape(b_ctx, seq_ctx, D), y_lat.reshape(b_lat, seq_lat, D),
            new_k, new_v, new_sf, new_sb)
```

```python
import functools

import numpy as np
import jax
import jax.numpy as jnp
from jax import lax
from jax.experimental import pallas as pl
from jax.experimental.pallas import tpu as pltpu

F32 = jnp.float32
BF16 = jnp.bfloat16
HIGHEST = lax.Precision.HIGHEST

D = 1024
GRID_W = 64
HD = 64
N_Q_HEADS = 8
N_KV_HEADS = 2
AW = N_Q_HEADS * HD
KVW = N_KV_HEADS * HD
ROPE_THETA = 10000.0
GLA_H = 4
GLA_DK = 64
GLA_DV = 128
GKW = GLA_H * GLA_DK
GVW = GLA_H * GLA_DV
GATE_RANK = 16
GLA_TAU = 16.0
CHUNK = 64
N_GROUPS = 4
EPG = 8
N_EXP = N_GROUPS * EPG
D_EXP = 256
DEPTH = 1
ALPHA = (2.0 * DEPTH) ** 0.25
LN_EPS = 1e-6
RMS_EPS = 1e-6

LANES = 128
A_WIDTH = AW + 2 * KVW + 2 * GKW + 2 * GVW + 2 * GATE_RANK
MG_WIDTH = 2 * D
TM_TOK = 512
TQ_LAT = 128
TM_EXP = 256
TS_ROWS = 256
VMEM_LIMIT = 56 * 1024 * 1024


def _cparams(sem):
    return pltpu.CompilerParams(dimension_semantics=sem, vmem_limit_bytes=VMEM_LIMIT)


def _dot(a, b):
    return jnp.dot(a, b, preferred_element_type=F32)


def _dot_nt(a, b):
    return lax.dot_general(a, b, (((1,), (1,)), ((), ())), preferred_element_type=F32)


def _dot_tn(a, b):
    return lax.dot_general(a, b, (((0,), (0,)), ((), ())), preferred_element_type=F32)


def _ln(x):
    mu = jnp.mean(x, axis=-1, keepdims=True)
    xc = x - mu
    var = jnp.mean(xc * xc, axis=-1, keepdims=True)
    return xc * lax.rsqrt(var + LN_EPS)


def _silu(x):
    return x * jax.nn.sigmoid(x)


def _split_bf16(x):
    hi = x.astype(BF16)
    lo = (x - hi.astype(F32)).astype(BF16)
    return hi, lo


def _ada_kernel(c_ref, w_ref, b_ref, o_ref):
    s = _silu(c_ref[...])
    o_ref[...] = jnp.dot(s, w_ref[...], preferred_element_type=F32, precision=HIGHEST) + b_ref[...]


def _ada(c_rows, w_ada, b_ada):
    rows = c_rows.shape[0]
    n = w_ada.shape[1]
    bn = 1024
    return pl.pallas_call(
        _ada_kernel,
        grid=(n // bn,),
        in_specs=[pl.BlockSpec((rows, D), lambda j: (0, 0)),
                  pl.BlockSpec((D, bn), lambda j: (0, j)),
                  pl.BlockSpec((1, bn), lambda j: (0, j))],
        out_specs=pl.BlockSpec((rows, bn), lambda j: (0, j)),
        out_shape=jax.ShapeDtypeStruct((rows, n), F32),
        compiler_params=_cparams(("arbitrary",)),
        name="ada",
    )(c_rows, w_ada, b_ada)


def _inproj_kernel(*refs, latent):
    if latent:
        (x_ref, mod_ref, w_ref, gain_ref, ind_ref, cos_ref, sin_ref,
         q_ref, k_ref, v_ref, qg_ref, kg_ref, vg_ref, rs_ref, lr_ref) = refs
    else:
        (x_ref, mod_ref, w_ref, gain_ref, ind_ref,
         q_ref, k_ref, v_ref, qg_ref, kg_ref, vg_ref, rs_ref, lr_ref, kf_ref, vf_ref) = refs
    tm = x_ref.shape[0]
    sh1 = mod_ref[0, :, 0:D]
    sc1 = mod_ref[0, :, D:2 * D]
    h = _ln(x_ref[...]) * (1.0 + sc1) + sh1
    res = _dot(h.astype(BF16), w_ref[...])

    qk = res[:, 0:AW + KVW]
    hi, lo = _split_bf16(qk * qk)
    ms = _dot(hi, ind_ref[...]) + _dot(lo, ind_ref[...])
    r = lax.rsqrt(ms + RMS_EPS)
    lane = lax.broadcasted_iota(jnp.int32, (tm, LANES), 1)
    low_half = lane < HD
    if latent:
        cos = cos_ref[...]
        sin = sin_ref[...]
        first = (lane % 32) < 16
    for s in range(5):
        rb = jnp.where(low_half, r[:, 2 * s:2 * s + 1], r[:, 2 * s + 1:2 * s + 2])
        y = qk[:, LANES * s:LANES * (s + 1)] * rb * gain_ref[:, LANES * s:LANES * (s + 1)]
        if s == 4 and not latent:
            for j in range(N_KV_HEADS):
                kf_ref[:, j, :] = y[:, HD * j:HD * (j + 1)]
        if latent:
            partner = jnp.where(first, pltpu.roll(y, LANES - 16, 1), pltpu.roll(y, 16, 1))
            y = y * cos + partner * sin
        if s < 4:
            q_ref[:, LANES * s:LANES * (s + 1)] = (y * (HD ** -0.5)).astype(BF16)
        else:
            k_ref[...] = y.astype(BF16)
    o = AW + KVW
    v = res[:, o:o + KVW]
    v_ref[...] = v.astype(BF16)
    if not latent:
        for j in range(N_KV_HEADS):
            vf_ref[:, j, :] = v[:, HD * j:HD * (j + 1)]
    o += KVW
    qg_ref[...] = res[:, o:o + GKW] * (GLA_DK ** -0.5)
    o += GKW
    kg_ref[...] = res[:, o:o + GKW]
    o += GKW
    vg_ref[...] = res[:, o:o + GVW].astype(BF16)
    o += GVW
    rs_ref[...] = _silu(res[:, o:o + GVW]).astype(BF16)
    o += GVW
    lr_ref[...] = res[:, o:o + 2 * GATE_RANK]


def _inproj(x2, mod, w_a, gain, ind, rope, seq, latent):
    t = x2.shape[0]
    tm = min(TM_TOK, seq)
    per_seq = seq // tm
    row = lambda i: (i, 0)
    const = lambda i: (0, 0)
    in_specs = [pl.BlockSpec((tm, D), row),
                pl.BlockSpec((1, 1, 6 * D), (lambda i: (i // per_seq, 0, 0)) if latent else (lambda i: (0, 0, 0))),
                pl.BlockSpec((D, A_WIDTH), const),
                pl.BlockSpec((1, AW + KVW), const),
                pl.BlockSpec((AW + KVW, LANES), const)]
    args = [x2, mod, w_a, gain, ind]
    if latent:
        in_specs += [pl.BlockSpec((tm, LANES), lambda i: (i % per_seq, 0))] * 2
        args += list(rope)
    widths = [(AW, BF16), (KVW, BF16), (KVW, BF16), (GKW, F32), (GKW, F32), (GVW, BF16), (GVW, BF16),
              (2 * GATE_RANK, F32)]
    out_specs = [pl.BlockSpec((tm, w), row) for w, _ in widths]
    out_shape = [jax.ShapeDtypeStruct((t, w), dt) for w, dt in widths]
    if not latent:
        cache_spec = pl.BlockSpec((None, None, tm, N_KV_HEADS, HD),
                                  lambda i: (i // per_seq, 0, i % per_seq, 0, 0))
        out_specs += [cache_spec] * 2
        out_shape += [jax.ShapeDtypeStruct((t // seq, 1, seq, N_KV_HEADS, HD), F32)] * 2
    return pl.pallas_call(
        functools.partial(_inproj_kernel, latent=latent),
        grid=(t // tm,),
        in_specs=in_specs,
        out_specs=out_specs,
        out_shape=out_shape,
        compiler_params=_cparams(("parallel",)),
        name="inproj_lat" if latent else "inproj_ctx",
    )(*args)


def _attn_kernel(*refs, has_cache):
    if has_cache:
        q_ref, k_ref, v_ref, kc_ref, vc_ref, o_ref, vt_s, vct_s = refs

        @pl.when(pl.program_id(1) == 0)
        def _():
            vt_s[...] = v_ref[...].astype(F32).T.astype(BF16)
            vct_s[...] = vc_ref[...].astype(F32).T.astype(BF16)
    else:
        q_ref, k_ref, v_ref, o_ref, vt_s = refs
        vt_s[...] = v_ref[...].astype(F32).T.astype(BF16)
    tq = q_ref.shape[0]

    lane = lax.broadcasted_iota(jnp.int32, (tq, LANES), 1)
    low_half = lane < HD
    k = k_ref[...]
    outs = []
    for j in range(N_KV_HEADS):
        keep = low_half if j == 0 else jnp.logical_not(low_half)
        zero = jnp.zeros((tq, LANES), BF16)
        qs = jnp.concatenate(
            [jnp.where(keep, q_ref[:, LANES * s:LANES * (s + 1)], zero) for s in range(4)], axis=0)
        s1 = _dot_nt(k, qs)
        m = jnp.max(s1, axis=0, keepdims=True)
        if has_cache:
            s2 = _dot_nt(kc_ref[...], qs)
            m = jnp.maximum(m, jnp.max(s2, axis=0, keepdims=True))
        p1 = jnp.exp(s1 - m)
        den = jnp.sum(p1, axis=0, keepdims=True)
        acc = _dot(vt_s[...], p1.astype(BF16))
        if has_cache:
            p2 = jnp.exp(s2 - m)
            den = den + jnp.sum(p2, axis=0, keepdims=True)
            acc = acc + _dot(vct_s[...], p2.astype(BF16))
        outs.append(acc / den)
    row = lax.broadcasted_iota(jnp.int32, (LANES, 4 * tq), 0)
    out = jnp.where(row < HD, outs[0], outs[1]).T
    for s in range(4):
        o_ref[:, LANES * s:LANES * (s + 1)] = out[s * tq:(s + 1) * tq].astype(BF16)


def _attention(q, k, v, cache, seq):
    t = q.shape[0]
    if cache is None:
        tq = seq
        grid = (t // seq,)
        qmap = lambda b: (b, 0)
        in_specs = [pl.BlockSpec((tq, AW), qmap), pl.BlockSpec((seq, KVW), qmap),
                    pl.BlockSpec((seq, KVW), qmap)]
        args = [q, k, v]
        scratch = [pltpu.VMEM((KVW, seq), BF16)]
        sem = ("parallel",)
        name = "attn_ctx"
    else:
        tq = TQ_LAT
        nq = seq // tq
        kc, vc = cache
        past = kc.shape[1]
        grid = (t // seq, nq)
        qmap = lambda b, i: (b * nq + i, 0)
        kmap = lambda b, i: (b, 0)
        cmap = lambda b, i: (b, 0, 0)
        in_specs = [pl.BlockSpec((tq, AW), qmap), pl.BlockSpec((seq, KVW), kmap),
                    pl.BlockSpec((seq, KVW), kmap),
                    pl.BlockSpec((None, past, KVW), cmap), pl.BlockSpec((None, past, KVW), cmap)]
        args = [q, k, v, kc, vc]
        scratch = [pltpu.VMEM((KVW, seq), BF16), pltpu.VMEM((KVW, past), BF16)]
        sem = ("parallel", "arbitrary")
        name = "attn_lat"
    return pl.pallas_call(
        functools.partial(_attn_kernel, has_cache=cache is not None),
        grid=grid,
        in_specs=in_specs,
        out_specs=pl.BlockSpec((tq, AW), qmap),
        out_shape=jax.ShapeDtypeStruct((t, AW), BF16),
        scratch_shapes=scratch,
        compiler_params=_cparams(sem),
        name=name,
    )(*args)


GLA_BLK = 256
GLA_UNROLL = 2


def _split3_bf16(x):
    hi = x.astype(BF16)
    r1 = x - hi.astype(F32)
    mid = r1.astype(BF16)
    lo = (r1 - mid.astype(F32)).astype(BF16)
    return hi, mid, lo


def _gla_kernel(qg_ref, kg_ref, vg_ref, lr_ref, rs_ref, s0_ref, wg_ref, bg_ref, gn_ref,
                o_ref, sfin_ref, cum_s, kv_s, dec_s):
    n = qg_ref.shape[0]
    nc = n // CHUNK
    lane = lax.broadcasted_iota(jnp.int32, (CHUNK, LANES), 1)
    low_half = lane < GLA_DK
    lane_sq = lax.broadcasted_iota(jnp.int32, (LANES, LANES), 1)
    low_half_sq = lane_sq < GLA_DK
    ri = lax.broadcasted_iota(jnp.int32, (2 * CHUNK, 2 * CHUNK), 0)
    ci = lax.broadcasted_iota(jnp.int32, (2 * CHUNK, 2 * CHUNK), 1)
    diag = (ri >> 6) == (ci >> 6)
    keep2 = (diag & (ri >= ci), diag & (ci >= ri))

    rb = lax.broadcasted_iota(jnp.int32, (GLA_BLK, GLA_BLK), 0)
    cb = lax.broadcasted_iota(jnp.int32, (GLA_BLK, GLA_BLK), 1)
    same = (rb >> 6) == (cb >> 6)
    tri = ((same & (rb >= cb)).astype(BF16), (same & (cb >= rb)).astype(BF16))
    for d in range(2):
        w_hi, w_lo = _split_bf16(wg_ref[d])
        for blk in range(n // GLA_BLK):
            rows = slice(blk * GLA_BLK, (blk + 1) * GLA_BLK)
            l_hi, l_lo = _split_bf16(lr_ref[rows, :])
            z = _dot(l_hi, w_hi) + _dot(l_lo, w_hi) + _dot(l_hi, w_lo) + bg_ref[d]
            logg = (jnp.minimum(z, 0.0) - jnp.log(1.0 + jnp.exp(-jnp.abs(z)))) * (1.0 / GLA_TAU)
            pieces = _split3_bf16(logg)
            cum_s[d, rows, :] = _dot(tri[d], pieces[0]) + _dot(tri[d], pieces[1]) + _dot(tri[d], pieces[2])

    def pass1(i, carry):
        cs = [i * GLA_UNROLL + u for u in range(GLA_UNROLL)]
        rws = [pl.ds(pl.multiple_of(c * CHUNK, CHUNK), CHUNK) for c in cs]
        prods = []
        for c, rows in zip(cs, rws):
            kc = kg_ref[rows, :]
            kdec = []
            for d in range(2):
                cum = cum_s[d, rows, :]
                last = cum[CHUNK - 1:CHUNK, :] if d == 0 else cum[0:1, :]
                dec_s[d, c] = jnp.exp(last)
                kdec.append((kc * jnp.exp(last - cum)).astype(BF16))
            for p in range(2):
                vpair = vg_ref[rows, GLA_DV * 2 * p:GLA_DV * 2 * (p + 1)]
                kpair = jnp.concatenate([kdec[0][:, LANES * p:LANES * (p + 1)],
                                         kdec[1][:, LANES * p:LANES * (p + 1)]], axis=1)
                prods.append(_dot_tn(vpair, kpair))
        for j, c in enumerate(cs):
            for p in range(2):
                res = prods[2 * j + p]
                for d in range(2):
                    cols = slice(LANES * d, LANES * (d + 1))
                    kv_s[d, c, p] = jnp.where(low_half_sq, res[0:GLA_DV, cols], res[GLA_DV:2 * GLA_DV, cols])
        return carry

    lax.fori_loop(0, nc // GLA_UNROLL, pass1, 0)

    for d in range(2):
        def scan(i, st):
            c = i if d == 0 else nc - 1 - i
            dec = dec_s[d, c]
            new = []
            for p in range(2):
                kv = kv_s[d, c, p]
                kv_s[d, c, p] = st[p]
                new.append(st[p] * dec[:, LANES * p:LANES * (p + 1)] + kv)
            return tuple(new)

        fin = lax.fori_loop(0, nc, scan, (s0_ref[d, 0], s0_ref[d, 1]))
        sfin_ref[d, 0] = fin[0]
        sfin_ref[d, 1] = fin[1]

    def pass3(i, carry):
        cs = [i * GLA_UNROLL + u for u in range(GLA_UNROLL)]
        rws = [pl.ds(pl.multiple_of(c * CHUNK, CHUNK), CHUNK) for c in cs]
        first = []
        for c, rows in zip(cs, rws):
            q = qg_ref[rows, :]
            k = kg_ref[rows, :]
            for d in range(2):
                cum = cum_s[d, rows, :]
                qt = q * jnp.exp(cum)
                kt = (k * jnp.exp(-cum)).astype(BF16)
                for p in range(2):
                    qs = qt[:, LANES * p:LANES * (p + 1)]
                    lhs = jnp.concatenate([jnp.where(low_half, qs, 0.0), jnp.where(low_half, 0.0, qs)],
                                          axis=0).astype(BF16)
                    kts = kt[:, LANES * p:LANES * (p + 1)]
                    rhs = jnp.concatenate([kts, kts, kv_s[d, c, p].astype(BF16)], axis=0)
                    first.append(_dot_nt(lhs, rhs))
        second = []
        for j, rows in enumerate(rws):
            for d in range(2):
                for p in range(2):
                    res = first[4 * j + 2 * d + p]
                    vp = jnp.concatenate([vg_ref[rows, GLA_DV * (2 * p):GLA_DV * (2 * p + 1)],
                                          vg_ref[rows, GLA_DV * (2 * p + 1):GLA_DV * (2 * p + 2)]], axis=0)
                    a = jnp.where(keep2[d], res[:, 0:2 * CHUNK], 0.0).astype(BF16)
                    second.append(_dot(a, vp) + res[:, 2 * CHUNK:])
        for j, rows in enumerate(rws):
            for p in range(2):
                tot = second[4 * j + p] + second[4 * j + 2 + p]
                y = tot * lax.rsqrt(jnp.mean(tot * tot, axis=-1, keepdims=True) + RMS_EPS) * gn_ref[...]
                for hh in range(2):
                    cols = slice(GLA_DV * (2 * p + hh), GLA_DV * (2 * p + hh + 1))
                    o_ref[rows, cols] = (y[CHUNK * hh:CHUNK * (hh + 1)]
                                         * rs_ref[rows, cols].astype(F32)).astype(BF16)
        return carry

    lax.fori_loop(0, nc // GLA_UNROLL, pass3, 0)


def _gla(qg, kg, vg, lr, rs, s0, wg, bg, gn, seq):
    t = qg.shape[0]
    nb = t // seq
    nc = seq // CHUNK
    row = lambda b: (b, 0)
    c3 = lambda b: (0, 0, 0)
    st = lambda b: (b, 0, 0, 0, 0)
    if s0.shape[0] == 1:
        s0map = lambda b: (0, 0, 0, 0, 0)
    else:
        s0map = st
    return pl.pallas_call(
        _gla_kernel,
        grid=(nb,),
        in_specs=[pl.BlockSpec((seq, GKW), row), pl.BlockSpec((seq, GKW), row),
                  pl.BlockSpec((seq, GVW), row), pl.BlockSpec((seq, 2 * GATE_RANK), row),
                  pl.BlockSpec((seq, GVW), row),
                  pl.BlockSpec((None, 2, 2, LANES, LANES), s0map),
                  pl.BlockSpec((2, 2 * GATE_RANK, GKW), c3), pl.BlockSpec((2, 1, GKW), c3),
                  pl.BlockSpec((1, GLA_DV), lambda b: (0, 0))],
        out_specs=[pl.BlockSpec((seq, GVW), row),
                   pl.BlockSpec((None, 2, 2, LANES, LANES), st)],
        out_shape=[jax.ShapeDtypeStruct((t, GVW), BF16),
                   jax.ShapeDtypeStruct((nb, 2, 2, LANES, LANES), F32)],
        scratch_shapes=[pltpu.VMEM((2, seq, GKW), F32),
                        pltpu.VMEM((2, nc, 2, LANES, LANES), F32),
                        pltpu.VMEM((2, nc, 1, GKW), F32)],
        compiler_params=_cparams(("parallel",)),
        name="gla_lat" if seq > 256 else "gla_ctx",
    )(qg, kg, vg, lr, rs, s0, wg, bg, gn)


ROW_SUB = D // LANES


def _store_row_slabs(ref, x):
    m = x.shape[0]
    for c in range(ROW_SUB):
        ref[pl.ds(c, m, stride=ROW_SUB), :] = x[:, LANES * c:LANES * (c + 1)]


def _load_row_slabs(ref):
    m = ref.shape[0] // ROW_SUB
    return jnp.concatenate([ref[pl.ds(c, m, stride=ROW_SUB), :] for c in range(ROW_SUB)], axis=1)


def _row_slab(ref, row):
    return ref.at[pl.ds(pl.multiple_of(row * ROW_SUB, ROW_SUB), ROW_SUB), :]


def _post_kernel(xc_ref, xl_ref, ac_ref, al_ref, gc_ref, gl_ref, mod_ref,
                 wmg_ref, wba_ref, wbg_ref, wo_ref, l1g_ref, l1b_ref, wr_ref, br_ref,
                 x1_ref, h2_ref, rt_ref, ert_ref, cnt_ref, run_s, *, n_ctx_tiles):
    i = pl.program_id(0)
    tm = xc_ref.shape[0]
    is_ctx = i < n_ctx_tiles
    x = jnp.where(is_ctx, xc_ref[...], xl_ref[...])
    attn = jnp.where(is_ctx, ac_ref[...], al_ref[...])
    gla = jnp.where(is_ctx, gc_ref[...], gl_ref[...])
    sh1 = mod_ref[0, :, 0:D]
    sc1 = mod_ref[0, :, D:2 * D]
    g1 = mod_ref[0, :, 2 * D:3 * D]
    sh2 = mod_ref[0, :, 3 * D:4 * D]
    sc2 = mod_ref[0, :, 4 * D:5 * D]
    h = (_ln(x) * (1.0 + sc1) + sh1).astype(BF16)
    gates = jax.nn.sigmoid(_dot(h, wmg_ref[...]))
    merged = gates[:, :D] * _dot(attn, wba_ref[...]) + gates[:, D:] * _dot(gla, wbg_ref[...])
    mix = _dot(merged.astype(BF16), wo_ref[...])
    x1 = _ln(ALPHA * x + g1 * mix) * l1g_ref[...] + l1b_ref[...]
    x1_ref[...] = x1
    h2 = _ln(x1) * (1.0 + sc2) + sh2
    _store_row_slabs(h2_ref, h2)

    logit = _dot(h2.astype(BF16), wr_ref[...]) + br_ref[...]
    lane_i = lax.broadcasted_iota(jnp.int32, (tm, LANES), 1)
    lane = lane_i.astype(F32)
    lane_grp = ((lane_i - N_GROUPS) >> 3).astype(F32)
    neg = jnp.float32(-jnp.inf)
    far = jnp.float32(LANES)
    is_g = lane_i < N_GROUPS
    lg = jnp.where(is_g, logit, neg)
    mg = jnp.max(lg, axis=-1, keepdims=True)
    pg_top = 1.0 / jnp.sum(jnp.where(is_g, jnp.exp(logit - mg), 0.0), axis=-1, keepdims=True)
    g_idx = jnp.min(jnp.where(lg == mg, lane, far), axis=-1, keepdims=True)
    in_grp = (lane_i >= N_GROUPS) & (lane_i < N_GROUPS + N_EXP) & (lane_grp == g_idx)
    le = jnp.where(in_grp, logit, neg)
    v1 = jnp.max(le, axis=-1, keepdims=True)
    i1 = jnp.min(jnp.where(le == v1, lane, far), axis=-1, keepdims=True)
    le2 = jnp.where(lane == i1, neg, le)
    v2 = jnp.max(le2, axis=-1, keepdims=True)
    i2 = jnp.min(jnp.where(le2 == v2, lane, far), axis=-1, keepdims=True)
    e1 = i1 - N_GROUPS
    e2 = i2 - N_GROUPS
    tt = jnp.exp(v2 - v1)
    w1 = pg_top / (1.0 + tt)
    w2 = pg_top * tt / (1.0 + tt)

    @pl.when(i == 0)
    def _():
        run_s[...] = jnp.zeros_like(run_s)

    hot = ((lane == e1) | (lane == e2)).astype(F32)
    ri = lax.broadcasted_iota(jnp.int32, (tm, tm), 0)
    ci = lax.broadcasted_iota(jnp.int32, (tm, tm), 1)
    before = _dot((ri > ci).astype(BF16), hot.astype(BF16)) + run_s[0:1, :]
    r1 = jnp.sum(jnp.where(lane == e1, before, 0.0), axis=-1, keepdims=True)
    r2 = jnp.sum(jnp.where(lane == e2, before, 0.0), axis=-1, keepdims=True)
    run_s[0:1, :] = run_s[0:1, :] + jnp.sum(hot, axis=0, keepdims=True)
    cnt_ref[...] = jnp.broadcast_to(run_s[0:1, :], cnt_ref.shape)

    rt = jnp.where(lane_i == 0, e1, 0.0)
    rt = jnp.where(lane_i == 1, e2, rt)
    rt = jnp.where(lane_i == 2, w1, rt)
    rt = jnp.where(lane_i == 3, w2, rt)
    rt = jnp.where(lane_i == 4, r1, rt)
    rt = jnp.where(lane_i == 5, r2, rt)
    rt_ref[...] = rt
    ert_ref[...] = rt.T[0:8, :]


def _post(x_ctx, x_lat, a_ctx, a_lat, g_ctx, g_lat, mod_all, seq_lat,
          w_mg, w_ba, w_bg, w_o, l1g, l1b, w_r, b_r):
    t_ctx, t_lat = x_ctx.shape[0], x_lat.shape[0]
    tm = TM_TOK
    nct, nlt = t_ctx // tm, t_lat // tm
    per_seq = seq_lat // tm
    nb_lat = t_lat // seq_lat
    t = t_ctx + t_lat
    cmap = lambda i: (jnp.minimum(i, nct - 1), 0)
    lmap = lambda i: (jnp.maximum(i - nct, 0), 0)
    mmap = lambda i: (jnp.where(i < nct, nb_lat, jnp.maximum(i - nct, 0) // per_seq), 0, 0)
    row = lambda i: (i, 0)
    const = lambda i: (0, 0)
    return pl.pallas_call(
        functools.partial(_post_kernel, n_ctx_tiles=nct),
        grid=(nct + nlt,),
        in_specs=[pl.BlockSpec((tm, D), cmap), pl.BlockSpec((tm, D), lmap),
                  pl.BlockSpec((tm, AW), cmap), pl.BlockSpec((tm, AW), lmap),
                  pl.BlockSpec((tm, GVW), cmap), pl.BlockSpec((tm, GVW), lmap),
                  pl.BlockSpec((1, 1, 6 * D), mmap),
                  pl.BlockSpec((D, MG_WIDTH), const), pl.BlockSpec((AW, D), const),
                  pl.BlockSpec((GVW, D), const), pl.BlockSpec((D, D), const),
                  pl.BlockSpec((1, D), const), pl.BlockSpec((1, D), const),
                  pl.BlockSpec((D, LANES), const), pl.BlockSpec((1, LANES), const)],
        out_specs=[pl.BlockSpec((tm, D), row), pl.BlockSpec((tm * ROW_SUB, LANES), row),
                   pl.BlockSpec((tm, LANES), row), pl.BlockSpec((8, tm), lambda i: (0, i)),
                   pl.BlockSpec((8, LANES), const)],
        out_shape=[jax.ShapeDtypeStruct((t, D), F32), jax.ShapeDtypeStruct((t * ROW_SUB, LANES), F32),
                   jax.ShapeDtypeStruct((t, LANES), F32), jax.ShapeDtypeStruct((8, t), F32),
                   jax.ShapeDtypeStruct((8, LANES), F32)],
        scratch_shapes=[pltpu.VMEM((8, LANES), F32)],
        compiler_params=_cparams(("arbitrary",)),
        name="post",
    )(x_ctx, x_lat, a_ctx, a_lat, g_ctx, g_lat, mod_all, w_mg, w_ba, w_bg, w_o, l1g, l1b, w_r, b_r)


ROW_UNROLL = 8


def _row_copy(src_ref, dst_ref, sem):
    return pltpu.make_async_copy(src_ref, dst_ref, sem)


def _scatter_kernel(pos0_ref, pos1_ref, h_ref, xs_ref, sem):
    ts = h_ref.shape[0] // ROW_SUB

    def issue(g, carry):
        r0 = pl.multiple_of(g * ROW_UNROLL, ROW_UNROLL)
        for k in range(ROW_UNROLL):
            src = _row_slab(h_ref, r0 + k)
            _row_copy(src, _row_slab(xs_ref, pos0_ref[0, r0 + k]), sem).start(priority=0)
            _row_copy(src, _row_slab(xs_ref, pos1_ref[0, r0 + k]), sem).start(priority=1)
        return carry

    lax.fori_loop(0, ts // ROW_UNROLL, issue, 0)
    for _ in range(2):
        _row_copy(h_ref, xs_ref.at[pl.ds(0, ts * ROW_SUB), :], sem).wait()


def _scatter_rows(h2p, pos0, pos1):
    t = h2p.shape[0] // ROW_SUB
    ts = TS_ROWS
    smem = lambda: pl.BlockSpec((None, 1, ts), lambda i: (i, 0, 0), memory_space=pltpu.SMEM)
    return pl.pallas_call(
        _scatter_kernel,
        grid=(t // ts,),
        in_specs=[smem(), smem(), pl.BlockSpec((ts * ROW_SUB, LANES), lambda i: (i, 0))],
        out_specs=pl.BlockSpec(memory_space=pl.ANY),
        out_shape=jax.ShapeDtypeStruct((2 * t * ROW_SUB, LANES), F32),
        scratch_shapes=[pltpu.SemaphoreType.DMA(())],
        compiler_params=_cparams(("arbitrary",)),
        name="scatter",
    )(pos0.reshape(t // ts, 1, ts), pos1.reshape(t // ts, 1, ts), h2p)


def _expert_kernel(vt_ref, ve_ref, lo_ref, hi_ref, first_ref, last_ref, newe_ref,
                   xs_ref, wg_ref, wu_ref, wd_ref, y_ref, wgu_s, wd_s, acc_s):
    v = pl.program_id(0)
    tm = acc_s.shape[0]

    @pl.when(newe_ref[v] == 1)
    def _():
        wgu_s[:, 0:D_EXP] = wg_ref[...].astype(BF16)
        wgu_s[:, D_EXP:2 * D_EXP] = wu_ref[...].astype(BF16)
        wd_s[...] = wd_ref[...].astype(BF16)

    @pl.when(first_ref[v] == 1)
    def _():
        acc_s[...] = jnp.zeros_like(acc_s)

    lo = lo_ref[v]
    hi = hi_ref[v]

    @pl.when(hi > lo)
    def _():
        gu = _dot(_load_row_slabs(xs_ref).astype(BF16), wgu_s[...])
        hid = _silu(gu[:, 0:D_EXP]) * gu[:, D_EXP:2 * D_EXP]
        y = _dot(hid.astype(BF16), wd_s[...])
        rid = lax.broadcasted_iota(jnp.int32, (tm, D), 0)
        acc_s[...] = jnp.where((rid >= lo) & (rid < hi), y, acc_s[...])

    @pl.when(last_ref[v] == 1)
    def _():
        _store_row_slabs(y_ref, acc_s[...])


def _experts(xs, meta, w_gate, w_up, w_down):
    p = xs.shape[0] // ROW_SUB
    nv = meta[0].shape[0]
    tm = TM_EXP
    xmap = lambda v, vt, ve, lo, hi, fi, la, ne: (vt[v], 0)
    wmap = lambda v, vt, ve, lo, hi, fi, la, ne: (ve[v], 0, 0)
    return pl.pallas_call(
        _expert_kernel,
        grid_spec=pltpu.PrefetchScalarGridSpec(
            num_scalar_prefetch=7,
            grid=(nv,),
            in_specs=[pl.BlockSpec((tm * ROW_SUB, LANES), xmap),
                      pl.BlockSpec((None, D, D_EXP), wmap), pl.BlockSpec((None, D, D_EXP), wmap),
                      pl.BlockSpec((None, D_EXP, D), wmap)],
            out_specs=pl.BlockSpec((tm * ROW_SUB, LANES), xmap),
            scratch_shapes=[pltpu.VMEM((D, 2 * D_EXP), BF16), pltpu.VMEM((D_EXP, D), BF16),
                            pltpu.VMEM((tm, D), F32)]),
        out_shape=jax.ShapeDtypeStruct((p * ROW_SUB, LANES), F32),
        compiler_params=_cparams(("arbitrary",)),
        name="experts",
    )(*meta, xs, w_gate, w_up, w_down)


def _final_kernel(p0c_ref, p1c_ref, p0n_ref, p1n_ref, x1_ref, rt_ref, mod_ref, l2g_ref, l2b_ref, ys_ref,
                  oc_ref, ol_ref, buf, sem, *, n_ctx_tiles):
    i = pl.program_id(0)
    n = pl.num_programs(0)
    tm = x1_ref.shape[0]

    def gather(p0_ref, p1_ref, slot):
        def issue(g, carry):
            r0 = pl.multiple_of(g * ROW_UNROLL, ROW_UNROLL)
            for k in range(ROW_UNROLL):
                _row_copy(_row_slab(ys_ref, p0_ref[0, r0 + k]),
                          _row_slab(buf.at[slot, 0], r0 + k), sem.at[slot]).start(priority=0)
                _row_copy(_row_slab(ys_ref, p1_ref[0, r0 + k]),
                          _row_slab(buf.at[slot, 1], r0 + k), sem.at[slot]).start(priority=1)
            return carry

        lax.fori_loop(0, tm // ROW_UNROLL, issue, 0)

    cur = i % 2

    @pl.when(i == 0)
    def _():
        gather(p0c_ref, p1c_ref, 0)

    @pl.when(i + 1 < n)
    def _():
        gather(p0n_ref, p1n_ref, 1 - cur)

    for k in range(2):
        _row_copy(ys_ref.at[pl.ds(0, tm * ROW_SUB), :], buf.at[cur, k], sem.at[cur]).wait()

    g2 = mod_ref[0, :, 5 * D:6 * D]
    w1 = rt_ref[:, 2:3]
    w2 = rt_ref[:, 3:4]
    moe = w1 * _load_row_slabs(buf.at[cur, 0]) + w2 * _load_row_slabs(buf.at[cur, 1])
    out = _ln(ALPHA * x1_ref[...] + g2 * moe) * l2g_ref[...] + l2b_ref[...]

    @pl.when(i < n_ctx_tiles)
    def _():
        oc_ref[...] = out

    @pl.when(i >= n_ctx_tiles)
    def _():
        ol_ref[...] = out


def _final(x1, rt, pos0, pos1, mod_all, l2g, l2b, ys, t_ctx, seq_lat):
    t = x1.shape[0]
    tm = TM_TOK
    nt = t // tm
    nct = t_ctx // tm
    t_lat = t - t_ctx
    per_seq = seq_lat // tm
    nb_lat = t_lat // seq_lat
    p0 = pos0.reshape(nt, 1, tm)
    p1 = pos1.reshape(nt, 1, tm)
    row = lambda i: (i, 0)
    const = lambda i: (0, 0)
    mmap = lambda i: (jnp.where(i < nct, nb_lat, jnp.maximum(i - nct, 0) // per_seq), 0, 0)
    smem_cur = lambda: pl.BlockSpec((None, 1, tm), lambda i: (i, 0, 0), memory_space=pltpu.SMEM)
    smem_nxt = lambda: pl.BlockSpec((None, 1, tm), lambda i: (jnp.minimum(i + 1, nt - 1), 0, 0),
                                    memory_space=pltpu.SMEM)
    return pl.pallas_call(
        functools.partial(_final_kernel, n_ctx_tiles=nct),
        grid=(nt,),
        in_specs=[smem_cur(), smem_cur(), smem_nxt(), smem_nxt(),
                  pl.BlockSpec((tm, D), row), pl.BlockSpec((tm, LANES), row),
                  pl.BlockSpec((1, 1, 6 * D), mmap),
                  pl.BlockSpec((1, D), const), pl.BlockSpec((1, D), const),
                  pl.BlockSpec(memory_space=pl.ANY)],
        out_specs=[pl.BlockSpec((tm, D), lambda i: (jnp.minimum(i, nct - 1), 0)),
                   pl.BlockSpec((tm, D), lambda i: (jnp.maximum(i - nct, 0), 0))],
        out_shape=[jax.ShapeDtypeStruct((t_ctx, D), F32), jax.ShapeDtypeStruct((t_lat, D), F32)],
        scratch_shapes=[pltpu.VMEM((2, 2, tm * ROW_SUB, LANES), F32), pltpu.SemaphoreType.DMA((2,))],
        compiler_params=_cparams(("arbitrary",)),
        name="final",
    )(p0, p1, p0, p1, x1, rt, mod_all, l2g, l2b, ys)


def _reorder_q_heads(w, axis):
    shape = w.shape
    split = shape[:axis] + (N_KV_HEADS, N_Q_HEADS // N_KV_HEADS, HD) + shape[axis + 1:]
    return jnp.swapaxes(w.reshape(split), axis, axis + 1).reshape(shape)


def _rope_tables(seq):
    t = jnp.arange(seq, dtype=jnp.int32)
    row = (t // GRID_W).astype(F32)
    col = (t % GRID_W).astype(F32)
    half = HD // 4
    inv = ROPE_THETA ** (-jnp.arange(half, dtype=F32) / half)
    lane = np.arange(LANES)
    d64 = lane % HD
    use_row = jnp.asarray(d64 < HD // 2)
    freq = inv[jnp.asarray(d64 % half)]
    pos = jnp.where(use_row[None, :], row[:, None], col[:, None])
    ang = pos * freq[None, :]
    sign = jnp.asarray(np.where((d64 % 32) < 16, -1.0, 1.0), F32)
    return jnp.cos(ang), jnp.sin(ang) * sign[None, :]


def _pair_states(s):
    b = s.shape[0]
    s = s.reshape(b, 2, 2, GLA_DK, GLA_DV)
    return s.transpose(0, 1, 4, 2, 3).reshape(b, 2, GLA_DV, 2 * GLA_DK)


def _unpair_states(s):
    b = s.shape[0]
    s = s.reshape(b, 2, GLA_DV, 2, GLA_DK)
    return s.transpose(0, 1, 3, 4, 2).reshape(b, GLA_H, GLA_DK, GLA_DV)


def _route_tables(ert, counts, t):
    i32 = jnp.int32
    cnt = counts[0, :N_EXP].astype(i32)
    ends = jnp.cumsum(cnt)
    starts = ends - cnt
    table = lambda e: jnp.sum(jnp.where(e[None, :] == jnp.arange(N_EXP, dtype=i32)[:, None],
                                        starts[:, None], 0), axis=0)
    pos0 = table(ert[0].astype(i32)) + ert[4].astype(i32)
    pos1 = table(ert[1].astype(i32)) + ert[5].astype(i32)
    p = 2 * t
    nt = p // TM_EXP
    nv = nt + N_EXP
    tile_starts = jnp.arange(nt, dtype=i32) * TM_EXP
    idx_t = jnp.arange(nt, dtype=i32) + jnp.sum(starts[None, :] < tile_starts[:, None], axis=1, dtype=i32)
    idx_e = jnp.arange(N_EXP, dtype=i32) + jnp.minimum(starts // TM_EXP + 1, nt)
    k = jnp.arange(nv, dtype=i32)[:, None]
    bounds = (jnp.sum(jnp.where(idx_t[None, :] == k, tile_starts[None, :], 0), axis=1)
              + jnp.sum(jnp.where(idx_e[None, :] == k, starts[None, :], 0), axis=1))
    nxt = jnp.concatenate([bounds[1:], jnp.array([p], i32)])
    tile = jnp.minimum(bounds // TM_EXP, nt - 1)
    exp_id = jnp.minimum(jnp.sum(ends[None, :] <= bounds[:, None], axis=1, dtype=i32), N_EXP - 1)
    lo = bounds - tile * TM_EXP
    hi = jnp.where(nxt > bounds, jnp.minimum(nxt - tile * TM_EXP, TM_EXP), lo)
    prev_tile = jnp.concatenate([jnp.array([-1], i32), tile[:-1]])
    prev_exp = jnp.concatenate([jnp.array([-1], i32), exp_id[:-1]])
    next_tile = jnp.concatenate([tile[1:], jnp.array([-1], i32)])
    first = (tile != prev_tile).astype(i32)
    last = (tile != next_tile).astype(i32)
    newe = (exp_id != prev_exp).astype(i32)
    return pos0, pos1, (tile, exp_id, lo, hi, first, last, newe)


def kernel(x_prompt, x_sample, cache_k, cache_v, state_gla_fwd, state_gla_bwd, c, c_ctx, w_ada, b_ada, w_in, q_norm, k_norm, gla_w_gate, gla_b_gate, gla_norm, w_br_attn, w_br_gla, w_out, ln1_g, ln1_b, router_group_w, router_group_b, router_expert_w, router_expert_b, exp_w_gate, exp_w_up, exp_w_down, ln2_g, ln2_b):
    b_ctx, seq_ctx, _ = x_prompt.shape
    b_lat, seq_lat, _ = x_sample.shape
    t_ctx, t_lat = b_ctx * seq_ctx, b_lat * seq_lat
    t = t_ctx + t_lat
    l = 0

    rows = -(-(b_lat + 1) // 8) * 8
    c_rows = jnp.zeros((rows, D), F32).at[:b_lat].set(c).at[b_lat].set(c_ctx)
    mod = _ada(c_rows, w_ada[l], b_ada[l][None, :])
    mod_all = mod[:b_lat + 1, None, :]
    mod_lat = mod_all[:b_lat]
    mod_ctx = mod_all[b_lat:]

    w_full = w_in[l]
    w_a = jnp.concatenate([_reorder_q_heads(w_full[:, :AW], 1), w_full[:, AW:A_WIDTH]], axis=1).astype(BF16)
    w_mg = w_full[:, A_WIDTH:].astype(BF16)
    gain = jnp.concatenate([jnp.tile(q_norm[l], N_Q_HEADS), jnp.tile(k_norm[l], N_KV_HEADS)])[None, :]
    head_of = np.arange(AW + KVW) // HD
    ind = jnp.asarray((head_of[:, None] == np.arange(LANES)[None, :]) / HD, BF16)
    w_ba = _reorder_q_heads(w_br_attn[l], 0).astype(BF16)
    w_bg = w_br_gla[l].astype(BF16)
    w_o = w_out[l].astype(BF16)
    w_r = jnp.zeros((D, LANES), F32).at[:, :N_GROUPS].set(router_group_w[l])
    w_r = w_r.at[:, N_GROUPS:N_GROUPS + N_EXP].set(router_expert_w[l]).astype(BF16)
    b_r = jnp.zeros((1, LANES), F32).at[0, :N_GROUPS].set(router_group_b[l])
    b_r = b_r.at[0, N_GROUPS:N_GROUPS + N_EXP].set(router_expert_b[l])
    wg = jnp.zeros((2, 2 * GATE_RANK, GKW), F32)
    wg = wg.at[0, :GATE_RANK].set(gla_w_gate[l, 0]).at[1, GATE_RANK:].set(gla_w_gate[l, 1])
    bg = gla_b_gate[l][:, None, :]
    gn = gla_norm[l][None, :]

    xc = x_prompt.reshape(t_ctx, D)
    xl = x_sample.reshape(t_lat, D)

    q_c, k_c, v_c, qg_c, kg_c, vg_c, rs_c, lr_c, kf_c, vf_c = _inproj(
        xc, mod_ctx, w_a, gain, ind, None, seq_ctx, latent=False)
    attn_c = _attention(q_c, k_c, v_c, None, seq_ctx)
    zero_state = jnp.zeros((1, 2, 2, LANES, LANES), F32)
    gla_c, sfin_c = _gla(qg_c, kg_c, vg_c, lr_c, rs_c, zero_state, wg, bg, gn, seq_ctx)

    q_l, k_l, v_l, qg_l, kg_l, vg_l, rs_l, lr_l = _inproj(
        xl, mod_lat, w_a, gain, ind, _rope_tables(seq_lat), seq_lat, latent=True)
    past = cache_k.shape[2]
    kc = cache_k[:, l].reshape(b_lat, past, KVW).astype(BF16)
    vc = cache_v[:, l].reshape(b_lat, past, KVW).astype(BF16)
    attn_l = _attention(q_l, k_l, v_l, (kc, vc), seq_lat)
    s0 = jnp.stack([_pair_states(state_gla_fwd[:, l]), _pair_states(state_gla_bwd[:, l])], axis=1)
    gla_l, _ = _gla(qg_l, kg_l, vg_l, lr_l, rs_l, s0, wg, bg, gn, seq_lat)

    x1, h2p, rt, ert, counts = _post(xc, xl, attn_c, attn_l, gla_c, gla_l, mod_all, seq_lat,
                                w_mg, w_ba, w_bg, w_o, ln1_g[l][None, :], ln1_b[l][None, :], w_r, b_r)

    pos0, pos1, meta = _route_tables(ert, counts, t)
    xs = _scatter_rows(h2p, pos0, pos1)
    ys = _experts(xs, meta, exp_w_gate[l], exp_w_up[l], exp_w_down[l])
    y_ctx, y_lat = _final(x1, rt, pos0, pos1, mod_all, ln2_g[l][None, :], ln2_b[l][None, :], ys,
                          t_ctx, seq_lat)

    new_k, new_v = kf_c, vf_c
    new_sf = _unpair_states(sfin_c[:, 0])[:, None]
    new_sb = _unpair_states(sfin_c[:, 1])[:, None]
    return (y_ctx.reshape(b_ctx, seq_ctx, D), y_lat.reshape(b_lat, seq_lat, D),
            new_k, new_v, new_sf, new_sb)
```

```python
import functools

import numpy as np
import jax
import jax.numpy as jnp
from jax import lax
from jax.experimental import pallas as pl
from jax.experimental.pallas import tpu as pltpu

F32 = jnp.float32
BF16 = jnp.bfloat16
HIGHEST = lax.Precision.HIGHEST

D = 1024
GRID_W = 64
HD = 64
N_Q_HEADS = 8
N_KV_HEADS = 2
AW = N_Q_HEADS * HD
KVW = N_KV_HEADS * HD
ROPE_THETA = 10000.0
GLA_H = 4
GLA_DK = 64
GLA_DV = 128
GKW = GLA_H * GLA_DK
GVW = GLA_H * GLA_DV
GATE_RANK = 16
GLA_TAU = 16.0
CHUNK = 64
N_GROUPS = 4
EPG = 8
N_EXP = N_GROUPS * EPG
D_EXP = 256
DEPTH = 1
ALPHA = (2.0 * DEPTH) ** 0.25
LN_EPS = 1e-6
RMS_EPS = 1e-6

LANES = 128
A_WIDTH = AW + 2 * KVW + 2 * GKW + 2 * GVW + 2 * GATE_RANK
MG_WIDTH = 2 * D
TM_TOK = 512
ROW_GROUP = 128
LOG2E = 1.4426950408889634
ONES_ROWS = 16
TQ_LAT = 128
TM_EXP = 256
TS_ROWS = 256
VMEM_LIMIT = 56 * 1024 * 1024


def _cparams(sem):
    return pltpu.CompilerParams(dimension_semantics=sem, vmem_limit_bytes=VMEM_LIMIT)


def _dot(a, b):
    return jnp.dot(a, b, preferred_element_type=F32)


def _dot_nt(a, b):
    return lax.dot_general(a, b, (((1,), (1,)), ((), ())), preferred_element_type=F32)


def _dot_tn(a, b):
    return lax.dot_general(a, b, (((0,), (0,)), ((), ())), preferred_element_type=F32)


def _ln(x):
    mu = jnp.mean(x, axis=-1, keepdims=True)
    xc = x - mu
    var = jnp.mean(xc * xc, axis=-1, keepdims=True)
    return xc * lax.rsqrt(var + LN_EPS)


def _silu(x):
    return x * jax.nn.sigmoid(x)


def _split_bf16(x):
    hi = x.astype(BF16)
    lo = (x - hi.astype(F32)).astype(BF16)
    return hi, lo


def _ada_kernel(c_ref, w_ref, b_ref, o_ref):
    s = _silu(c_ref[...])
    o_ref[...] = jnp.dot(s, w_ref[...], preferred_element_type=F32, precision=HIGHEST) + b_ref[...]


def _ada(c_rows, w_ada, b_ada):
    rows = c_rows.shape[0]
    n = w_ada.shape[1]
    bn = 1024
    return pl.pallas_call(
        _ada_kernel,
        grid=(n // bn,),
        in_specs=[pl.BlockSpec((rows, D), lambda j: (0, 0)),
                  pl.BlockSpec((D, bn), lambda j: (0, j)),
                  pl.BlockSpec((1, bn), lambda j: (0, j))],
        out_specs=pl.BlockSpec((rows, bn), lambda j: (0, j)),
        out_shape=jax.ShapeDtypeStruct((rows, n), F32),
        compiler_params=_cparams(("arbitrary",)),
        name="ada",
    )(c_rows, w_ada, b_ada)


def _inproj_kernel(*refs, latent):
    if latent:
        (x_ref, mod_ref, w_ref, gain_ref, ind_ref, cos_ref, sin_ref,
         q_ref, k_ref, v_ref, qg_ref, kg_ref, vg_ref, rs_ref, lr_ref) = refs
    else:
        (x_ref, mod_ref, w_ref, gain_ref, ind_ref,
         q_ref, k_ref, v_ref, qg_ref, kg_ref, vg_ref, rs_ref, lr_ref, kf_ref, vf_ref) = refs
    tm = x_ref.shape[0]
    sub = ROW_GROUP
    n_groups = tm // sub
    sh1 = mod_ref[0, :, 0:D]
    sc1 = mod_ref[0, :, D:2 * D]
    lane = lax.broadcasted_iota(jnp.int32, (sub, LANES), 1)
    low_half = lane < HD
    first = (lane % 32) < 16

    def project(g):
        rows = slice(g * sub, (g + 1) * sub)
        h = (_ln(x_ref[rows, :]) * (1.0 + sc1) + sh1).astype(BF16)
        return _dot(h, w_ref[...])

    def finish(g, res):
        rows = slice(g * sub, (g + 1) * sub)
        qk = res[:, 0:AW + KVW]
        hi, lo = _split_bf16(qk * qk)
        ms = _dot(hi, ind_ref[...]) + _dot(lo, ind_ref[...])
        r = lax.rsqrt(ms + RMS_EPS)
        for s in range(5):
            rb = jnp.where(low_half, r[:, 2 * s:2 * s + 1], r[:, 2 * s + 1:2 * s + 2])
            y = res[:, LANES * s:LANES * (s + 1)] * rb * gain_ref[:, LANES * s:LANES * (s + 1)]
            if s == 4 and not latent:
                for j in range(N_KV_HEADS):
                    kf_ref[rows, j, :] = y[:, HD * j:HD * (j + 1)]
            if latent:
                partner = jnp.where(first, pltpu.roll(y, LANES - 16, 1), pltpu.roll(y, 16, 1))
                y = y * cos_ref[rows, :] + partner * sin_ref[rows, :]
            if s < 4:
                q_ref[rows, LANES * s:LANES * (s + 1)] = (y * (HD ** -0.5 * LOG2E)).astype(BF16)
            else:
                k_ref[rows, :] = y.astype(BF16)
        o = AW + KVW
        v = res[:, o:o + KVW]
        v_ref[rows, :] = v.astype(BF16)
        if not latent:
            for j in range(N_KV_HEADS):
                vf_ref[rows, j, :] = v[:, HD * j:HD * (j + 1)]
        o += KVW
        qg_ref[rows, :] = res[:, o:o + GKW] * (GLA_DK ** -0.5)
        o += GKW
        kg_ref[rows, :] = res[:, o:o + GKW]
        o += GKW
        vg_ref[rows, :] = res[:, o:o + GVW].astype(BF16)
        o += GVW
        rs_ref[rows, :] = _silu(res[:, o:o + GVW]).astype(BF16)
        o += GVW
        lr_ref[rows, :] = res[:, o:o + 2 * GATE_RANK]

    pending = {0: project(0)}
    for g in range(n_groups):
        if g + 1 < n_groups:
            pending[g + 1] = project(g + 1)
        finish(g, pending.pop(g))


def _inproj(x2, mod, w_a, gain, ind, rope, seq, latent):
    t = x2.shape[0]
    tm = min(TM_TOK, seq)
    per_seq = seq // tm
    row = lambda i: (i, 0)
    const = lambda i: (0, 0)
    in_specs = [pl.BlockSpec((tm, D), row),
                pl.BlockSpec((1, 1, 6 * D), (lambda i: (i // per_seq, 0, 0)) if latent else (lambda i: (0, 0, 0))),
                pl.BlockSpec((D, A_WIDTH), const),
                pl.BlockSpec((1, AW + KVW), const),
                pl.BlockSpec((AW + KVW, LANES), const)]
    args = [x2, mod, w_a, gain, ind]
    if latent:
        in_specs += [pl.BlockSpec((tm, LANES), lambda i: (i % per_seq, 0))] * 2
        args += list(rope)
    widths = [(AW, BF16), (KVW, BF16), (KVW, BF16), (GKW, F32), (GKW, F32), (GVW, BF16), (GVW, BF16),
              (2 * GATE_RANK, F32)]
    out_specs = [pl.BlockSpec((tm, w), row) for w, _ in widths]
    out_shape = [jax.ShapeDtypeStruct((t, w), dt) for w, dt in widths]
    if not latent:
        cache_spec = pl.BlockSpec((None, None, tm, N_KV_HEADS, HD),
                                  lambda i: (i // per_seq, 0, i % per_seq, 0, 0))
        out_specs += [cache_spec] * 2
        out_shape += [jax.ShapeDtypeStruct((t // seq, 1, seq, N_KV_HEADS, HD), F32)] * 2
    return pl.pallas_call(
        functools.partial(_inproj_kernel, latent=latent),
        grid=(t // tm,),
        in_specs=in_specs,
        out_specs=out_specs,
        out_shape=out_shape,
        compiler_params=_cparams(("parallel",)),
        name="inproj_lat" if latent else "inproj_ctx",
    )(*args)


def _attn_kernel(*refs, has_cache):
    def transposed_with_ones(dst, src):
        dst[0:KVW, :] = src[...].astype(F32).T.astype(BF16)
        dst[KVW:, :] = jnp.ones((ONES_ROWS, dst.shape[1]), BF16)

    if has_cache:
        q_ref, k_ref, v_ref, kc_ref, vc_ref, o_ref, vt_s, vct_s = refs

        @pl.when(pl.program_id(1) == 0)
        def _():
            transposed_with_ones(vt_s, v_ref)
            transposed_with_ones(vct_s, vc_ref)
    else:
        q_ref, k_ref, v_ref, o_ref, vt_s = refs
        transposed_with_ones(vt_s, v_ref)
    tq = q_ref.shape[0]

    lane = lax.broadcasted_iota(jnp.int32, (tq, LANES), 1)
    low_half = lane < HD
    k = k_ref[...]
    scores = []
    for j in range(N_KV_HEADS):
        keep = low_half if j == 0 else jnp.logical_not(low_half)
        zero = jnp.zeros((tq, LANES), BF16)
        for pair in range(2):
            qs = jnp.concatenate([jnp.where(keep, q_ref[:, LANES * s:LANES * (s + 1)], zero)
                                  for s in (2 * pair, 2 * pair + 1)], axis=0)
            s1 = _dot_nt(k, qs)
            s2 = _dot_nt(kc_ref[...], qs) if has_cache else None
            scores.append((s1, s2))
    outs = []
    for s1, s2 in scores:
        m = jnp.max(s1, axis=0, keepdims=True)
        if has_cache:
            m = jnp.maximum(m, jnp.max(s2, axis=0, keepdims=True))
        acc = _dot(vt_s[...], jnp.exp2(s1 - m).astype(BF16))
        if has_cache:
            acc = acc + _dot(vct_s[...], jnp.exp2(s2 - m).astype(BF16))
        outs.append(acc[0:KVW] / acc[KVW:KVW + 1])
    head0 = jnp.concatenate(outs[0:2], axis=1)
    head1 = jnp.concatenate(outs[2:4], axis=1)
    row = lax.broadcasted_iota(jnp.int32, (LANES, 4 * tq), 0)
    out = jnp.where(row < HD, head0, head1).T
    for s in range(4):
        o_ref[:, LANES * s:LANES * (s + 1)] = out[s * tq:(s + 1) * tq].astype(BF16)


def _attention(q, k, v, cache, seq):
    t = q.shape[0]
    if cache is None:
        tq = seq
        grid = (t // seq,)
        qmap = lambda b: (b, 0)
        in_specs = [pl.BlockSpec((tq, AW), qmap), pl.BlockSpec((seq, KVW), qmap),
                    pl.BlockSpec((seq, KVW), qmap)]
        args = [q, k, v]
        scratch = [pltpu.VMEM((KVW + ONES_ROWS, seq), BF16)]
        sem = ("parallel",)
        name = "attn_ctx"
    else:
        tq = TQ_LAT
        nq = seq // tq
        kc, vc = cache
        past = kc.shape[1]
        grid = (t // seq, nq)
        qmap = lambda b, i: (b * nq + i, 0)
        kmap = lambda b, i: (b, 0)
        cmap = lambda b, i: (b, 0, 0)
        in_specs = [pl.BlockSpec((tq, AW), qmap), pl.BlockSpec((seq, KVW), kmap),
                    pl.BlockSpec((seq, KVW), kmap),
                    pl.BlockSpec((None, past, KVW), cmap), pl.BlockSpec((None, past, KVW), cmap)]
        args = [q, k, v, kc, vc]
        scratch = [pltpu.VMEM((KVW + ONES_ROWS, seq), BF16), pltpu.VMEM((KVW + ONES_ROWS, past), BF16)]
        sem = ("parallel", "arbitrary")
        name = "attn_lat"
    return pl.pallas_call(
        functools.partial(_attn_kernel, has_cache=cache is not None),
        grid=grid,
        in_specs=in_specs,
        out_specs=pl.BlockSpec((tq, AW), qmap),
        out_shape=jax.ShapeDtypeStruct((t, AW), BF16),
        scratch_shapes=scratch,
        compiler_params=_cparams(sem),
        name=name,
    )(*args)


GLA_BLK = 256
GLA_UNROLL = 2


def _split3_bf16(x):
    hi = x.astype(BF16)
    r1 = x - hi.astype(F32)
    mid = r1.astype(BF16)
    lo = (r1 - mid.astype(F32)).astype(BF16)
    return hi, mid, lo


def _gla_kernel(qg_ref, kg_ref, vg_ref, lr_ref, rs_ref, s0_ref, wg_ref, bg_ref, gn_ref,
                o_ref, sfin_ref, cum_s, kv_s, dec_s):
    n = qg_ref.shape[0]
    nc = n // CHUNK
    lane = lax.broadcasted_iota(jnp.int32, (CHUNK, LANES), 1)
    low_half = lane < GLA_DK
    lane_sq = lax.broadcasted_iota(jnp.int32, (LANES, LANES), 1)
    low_half_sq = lane_sq < GLA_DK
    ri = lax.broadcasted_iota(jnp.int32, (2 * CHUNK, 2 * CHUNK), 0)
    ci = lax.broadcasted_iota(jnp.int32, (2 * CHUNK, 2 * CHUNK), 1)
    diag = (ri >> 6) == (ci >> 6)
    keep2 = (diag & (ri >= ci), diag & (ci >= ri))

    rb = lax.broadcasted_iota(jnp.int32, (GLA_BLK, GLA_BLK), 0)
    cb = lax.broadcasted_iota(jnp.int32, (GLA_BLK, GLA_BLK), 1)
    same = (rb >> 6) == (cb >> 6)
    tri = ((same & (rb >= cb)).astype(BF16), (same & (cb >= rb)).astype(BF16))
    for d in range(2):
        w_hi, w_lo = _split_bf16(wg_ref[d])
        for blk in range(n // GLA_BLK):
            rows = slice(blk * GLA_BLK, (blk + 1) * GLA_BLK)
            l_hi, l_lo = _split_bf16(lr_ref[rows, :])
            z = _dot(l_hi, w_hi) + _dot(l_lo, w_hi) + _dot(l_hi, w_lo) + bg_ref[d]
            logg = (jnp.minimum(z, 0.0) - jnp.log(1.0 + jnp.exp(-jnp.abs(z)))) * (1.0 / GLA_TAU)
            pieces = _split3_bf16(logg)
            cum_s[d, rows, :] = _dot(tri[d], pieces[0]) + _dot(tri[d], pieces[1]) + _dot(tri[d], pieces[2])

    def pass1(i, carry):
        cs = [i * GLA_UNROLL + u for u in range(GLA_UNROLL)]
        rws = [pl.ds(pl.multiple_of(c * CHUNK, CHUNK), CHUNK) for c in cs]
        prods = []
        for c, rows in zip(cs, rws):
            kc = kg_ref[rows, :]
            kdec = []
            for d in range(2):
                cum = cum_s[d, rows, :]
                last = cum[CHUNK - 1:CHUNK, :] if d == 0 else cum[0:1, :]
                dec_s[d, c] = jnp.exp(last)
                kdec.append((kc * jnp.exp(last - cum)).astype(BF16))
            for p in range(2):
                vpair = vg_ref[rows, GLA_DV * 2 * p:GLA_DV * 2 * (p + 1)]
                kpair = jnp.concatenate([kdec[0][:, LANES * p:LANES * (p + 1)],
                                         kdec[1][:, LANES * p:LANES * (p + 1)]], axis=1)
                prods.append(_dot_tn(vpair, kpair))
        for j, c in enumerate(cs):
            for p in range(2):
                res = prods[2 * j + p]
                for d in range(2):
                    cols = slice(LANES * d, LANES * (d + 1))
                    kv_s[d, c, p] = jnp.where(low_half_sq, res[0:GLA_DV, cols], res[GLA_DV:2 * GLA_DV, cols])
        return carry

    lax.fori_loop(0, nc // GLA_UNROLL, pass1, 0)

    for d in range(2):
        def scan(i, st):
            c = i if d == 0 else nc - 1 - i
            dec = dec_s[d, c]
            new = []
            for p in range(2):
                kv = kv_s[d, c, p]
                kv_s[d, c, p] = st[p]
                new.append(st[p] * dec[:, LANES * p:LANES * (p + 1)] + kv)
            return tuple(new)

        fin = lax.fori_loop(0, nc, scan, (s0_ref[d, 0], s0_ref[d, 1]))
        sfin_ref[d, 0] = fin[0]
        sfin_ref[d, 1] = fin[1]

    def pass3(i, carry):
        cs = [i * GLA_UNROLL + u for u in range(GLA_UNROLL)]
        rws = [pl.ds(pl.multiple_of(c * CHUNK, CHUNK), CHUNK) for c in cs]
        first = []
        for c, rows in zip(cs, rws):
            q = qg_ref[rows, :]
            k = kg_ref[rows, :]
            for d in range(2):
                cum = cum_s[d, rows, :]
                qt = q * jnp.exp(cum)
                kt = (k * jnp.exp(-cum)).astype(BF16)
                for p in range(2):
                    qs = qt[:, LANES * p:LANES * (p + 1)]
                    lhs = jnp.concatenate([jnp.where(low_half, qs, 0.0), jnp.where(low_half, 0.0, qs)],
                                          axis=0).astype(BF16)
                    kts = kt[:, LANES * p:LANES * (p + 1)]
                    rhs = jnp.concatenate([kts, kts, kv_s[d, c, p].astype(BF16)], axis=0)
                    first.append(_dot_nt(lhs, rhs))
        second = []
        for j, rows in enumerate(rws):
            for d in range(2):
                for p in range(2):
                    res = first[4 * j + 2 * d + p]
                    vp = jnp.concatenate([vg_ref[rows, GLA_DV * (2 * p):GLA_DV * (2 * p + 1)],
                                          vg_ref[rows, GLA_DV * (2 * p + 1):GLA_DV * (2 * p + 2)]], axis=0)
                    a = jnp.where(keep2[d], res[:, 0:2 * CHUNK], 0.0).astype(BF16)
                    second.append(_dot(a, vp) + res[:, 2 * CHUNK:])
        for j, rows in enumerate(rws):
            for p in range(2):
                tot = second[4 * j + p] + second[4 * j + 2 + p]
                y = tot * lax.rsqrt(jnp.mean(tot * tot, axis=-1, keepdims=True) + RMS_EPS) * gn_ref[...]
                for hh in range(2):
                    cols = slice(GLA_DV * (2 * p + hh), GLA_DV * (2 * p + hh + 1))
                    o_ref[rows, cols] = (y[CHUNK * hh:CHUNK * (hh + 1)]
                                         * rs_ref[rows, cols].astype(F32)).astype(BF16)
        return carry

    lax.fori_loop(0, nc // GLA_UNROLL, pass3, 0)


def _gla(qg, kg, vg, lr, rs, s0, wg, bg, gn, seq):
    t = qg.shape[0]
    nb = t // seq
    nc = seq // CHUNK
    row = lambda b: (b, 0)
    c3 = lambda b: (0, 0, 0)
    st = lambda b: (b, 0, 0, 0, 0)
    if s0.shape[0] == 1:
        s0map = lambda b: (0, 0, 0, 0, 0)
    else:
        s0map = st
    return pl.pallas_call(
        _gla_kernel,
        grid=(nb,),
        in_specs=[pl.BlockSpec((seq, GKW), row), pl.BlockSpec((seq, GKW), row),
                  pl.BlockSpec((seq, GVW), row), pl.BlockSpec((seq, 2 * GATE_RANK), row),
                  pl.BlockSpec((seq, GVW), row),
                  pl.BlockSpec((None, 2, 2, LANES, LANES), s0map),
                  pl.BlockSpec((2, 2 * GATE_RANK, GKW), c3), pl.BlockSpec((2, 1, GKW), c3),
                  pl.BlockSpec((1, GLA_DV), lambda b: (0, 0))],
        out_specs=[pl.BlockSpec((seq, GVW), row),
                   pl.BlockSpec((None, 2, 2, LANES, LANES), st)],
        out_shape=[jax.ShapeDtypeStruct((t, GVW), BF16),
                   jax.ShapeDtypeStruct((nb, 2, 2, LANES, LANES), F32)],
        scratch_shapes=[pltpu.VMEM((2, seq, GKW), F32),
                        pltpu.VMEM((2, nc, 2, LANES, LANES), F32),
                        pltpu.VMEM((2, nc, 1, GKW), F32)],
        compiler_params=_cparams(("parallel",)),
        name="gla_lat" if seq > 256 else "gla_ctx",
    )(qg, kg, vg, lr, rs, s0, wg, bg, gn)


ROW_SUB = D // LANES


def _store_row_slabs(ref, x):
    m = x.shape[0]
    for c in range(ROW_SUB):
        ref[pl.ds(c, m, stride=ROW_SUB), :] = x[:, LANES * c:LANES * (c + 1)]


def _load_row_slabs(ref):
    m = ref.shape[0] // ROW_SUB
    return jnp.concatenate([ref[pl.ds(c, m, stride=ROW_SUB), :] for c in range(ROW_SUB)], axis=1)


def _row_slab(ref, row):
    return ref.at[pl.ds(pl.multiple_of(row * ROW_SUB, ROW_SUB), ROW_SUB), :]


def _post_kernel(xc_ref, xl_ref, ac_ref, al_ref, gc_ref, gl_ref, mod_ref,
                 wmg_ref, wba_ref, wbg_ref, wo_ref, l1g_ref, l1b_ref, wr_ref, br_ref,
                 x1_ref, h2_ref, rt_ref, ert_ref, cnt_ref, run_s, *, n_ctx_tiles):
    i = pl.program_id(0)
    tm = xc_ref.shape[0]
    sub = ROW_GROUP
    n_groups = tm // sub
    is_ctx = i < n_ctx_tiles
    sh1 = mod_ref[0, :, 0:D]
    sc1 = mod_ref[0, :, D:2 * D]
    g1 = mod_ref[0, :, 2 * D:3 * D]
    sh2 = mod_ref[0, :, 3 * D:4 * D]
    sc2 = mod_ref[0, :, 4 * D:5 * D]

    parts = [slice(g * sub, (g + 1) * sub) for g in range(n_groups)]
    xs = [jnp.where(is_ctx, xc_ref[r, :], xl_ref[r, :]) for r in parts]
    hs = [(_ln(x) * (1.0 + sc1) + sh1).astype(BF16) for x in xs]
    gates = [jax.nn.sigmoid(_dot(h, wmg_ref[...])) for h in hs]
    ba = [_dot(jnp.where(is_ctx, ac_ref[r, :], al_ref[r, :]), wba_ref[...]) for r in parts]
    bg = [_dot(jnp.where(is_ctx, gc_ref[r, :], gl_ref[r, :]), wbg_ref[...]) for r in parts]
    merged = [(g[:, :D] * a + g[:, D:] * b).astype(BF16) for g, a, b in zip(gates, ba, bg)]
    mix = [_dot(m, wo_ref[...]) for m in merged]
    x1s = [_ln(ALPHA * x + g1 * m) * l1g_ref[...] + l1b_ref[...] for x, m in zip(xs, mix)]
    h2s = [_ln(x1) * (1.0 + sc2) + sh2 for x1 in x1s]
    logits = [_dot(h2.astype(BF16), wr_ref[...]) + br_ref[...] for h2 in h2s]
    for g, r in enumerate(parts):
        x1_ref[r, :] = x1s[g]
        _store_row_slabs(h2_ref.at[pl.ds(g * sub * ROW_SUB, sub * ROW_SUB), :], h2s[g])

    @pl.when(i == 0)
    def _():
        run_s[...] = jnp.zeros_like(run_s)

    ri = lax.broadcasted_iota(jnp.int32, (sub, sub), 0)
    ci = lax.broadcasted_iota(jnp.int32, (sub, sub), 1)
    earlier = (ri > ci).astype(BF16)
    run = run_s[0:1, :]
    for g, r in enumerate(parts):
        run = _route(logits[g], earlier, run, rt_ref.at[r, :], ert_ref.at[:, r])
    run_s[0:1, :] = run
    cnt_ref[...] = jnp.broadcast_to(run, cnt_ref.shape)


def _route(logit, earlier, run, rt_ref, ert_ref):
    tm = logit.shape[0]
    lane_i = lax.broadcasted_iota(jnp.int32, (tm, LANES), 1)
    lane = lane_i.astype(F32)
    lane_grp = ((lane_i - N_GROUPS) >> 3).astype(F32)
    neg = jnp.float32(-jnp.inf)
    far = jnp.float32(LANES)
    is_g = lane_i < N_GROUPS
    lg = jnp.where(is_g, logit, neg)
    mg = jnp.max(lg, axis=-1, keepdims=True)
    pg_top = 1.0 / jnp.sum(jnp.where(is_g, jnp.exp(logit - mg), 0.0), axis=-1, keepdims=True)
    g_idx = jnp.min(jnp.where(lg == mg, lane, far), axis=-1, keepdims=True)
    in_grp = (lane_i >= N_GROUPS) & (lane_i < N_GROUPS + N_EXP) & (lane_grp == g_idx)
    le = jnp.where(in_grp, logit, neg)
    v1 = jnp.max(le, axis=-1, keepdims=True)
    i1 = jnp.min(jnp.where(le == v1, lane, far), axis=-1, keepdims=True)
    le2 = jnp.where(lane == i1, neg, le)
    v2 = jnp.max(le2, axis=-1, keepdims=True)
    i2 = jnp.min(jnp.where(le2 == v2, lane, far), axis=-1, keepdims=True)
    e1 = i1 - N_GROUPS
    e2 = i2 - N_GROUPS
    tt = jnp.exp(v2 - v1)
    w1 = pg_top / (1.0 + tt)
    w2 = pg_top * tt / (1.0 + tt)

    hot = ((lane == e1) | (lane == e2)).astype(F32)
    before = _dot(earlier, hot.astype(BF16)) + run
    r1 = jnp.sum(jnp.where(lane == e1, before, 0.0), axis=-1, keepdims=True)
    r2 = jnp.sum(jnp.where(lane == e2, before, 0.0), axis=-1, keepdims=True)

    rt = jnp.where(lane_i == 0, e1, 0.0)
    rt = jnp.where(lane_i == 1, e2, rt)
    rt = jnp.where(lane_i == 2, w1, rt)
    rt = jnp.where(lane_i == 3, w2, rt)
    rt = jnp.where(lane_i == 4, r1, rt)
    rt = jnp.where(lane_i == 5, r2, rt)
    rt_ref[...] = rt
    ert_ref[...] = rt.T[0:8, :]
    return run + jnp.sum(hot, axis=0, keepdims=True)


def _post(x_ctx, x_lat, a_ctx, a_lat, g_ctx, g_lat, mod_all, seq_lat,
          w_mg, w_ba, w_bg, w_o, l1g, l1b, w_r, b_r):
    t_ctx, t_lat = x_ctx.shape[0], x_lat.shape[0]
    tm = TM_TOK
    nct, nlt = t_ctx // tm, t_lat // tm
    per_seq = seq_lat // tm
    nb_lat = t_lat // seq_lat
    t = t_ctx + t_lat
    cmap = lambda i: (jnp.minimum(i, nct - 1), 0)
    lmap = lambda i: (jnp.maximum(i - nct, 0), 0)
    mmap = lambda i: (jnp.where(i < nct, nb_lat, jnp.maximum(i - nct, 0) // per_seq), 0, 0)
    row = lambda i: (i, 0)
    const = lambda i: (0, 0)
    return pl.pallas_call(
        functools.partial(_post_kernel, n_ctx_tiles=nct),
        grid=(nct + nlt,),
        in_specs=[pl.BlockSpec((tm, D), cmap), pl.BlockSpec((tm, D), lmap),
                  pl.BlockSpec((tm, AW), cmap), pl.BlockSpec((tm, AW), lmap),
                  pl.BlockSpec((tm, GVW), cmap), pl.BlockSpec((tm, GVW), lmap),
                  pl.BlockSpec((1, 1, 6 * D), mmap),
                  pl.BlockSpec((D, MG_WIDTH), const), pl.BlockSpec((AW, D), const),
                  pl.BlockSpec((GVW, D), const), pl.BlockSpec((D, D), const),
                  pl.BlockSpec((1, D), const), pl.BlockSpec((1, D), const),
                  pl.BlockSpec((D, LANES), const), pl.BlockSpec((1, LANES), const)],
        out_specs=[pl.BlockSpec((tm, D), row), pl.BlockSpec((tm * ROW_SUB, LANES), row),
                   pl.BlockSpec((tm, LANES), row), pl.BlockSpec((8, tm), lambda i: (0, i)),
                   pl.BlockSpec((8, LANES), const)],
        out_shape=[jax.ShapeDtypeStruct((t, D), F32), jax.ShapeDtypeStruct((t * ROW_SUB, LANES), F32),
                   jax.ShapeDtypeStruct((t, LANES), F32), jax.ShapeDtypeStruct((8, t), F32),
                   jax.ShapeDtypeStruct((8, LANES), F32)],
        scratch_shapes=[pltpu.VMEM((8, LANES), F32)],
        compiler_params=_cparams(("arbitrary",)),
        name="post",
    )(x_ctx, x_lat, a_ctx, a_lat, g_ctx, g_lat, mod_all, w_mg, w_ba, w_bg, w_o, l1g, l1b, w_r, b_r)


ROW_UNROLL = 8


def _row_copy(src_ref, dst_ref, sem):
    return pltpu.make_async_copy(src_ref, dst_ref, sem)


def _scatter_kernel(pos0_ref, pos1_ref, h_ref, xs_ref, sem):
    ts = h_ref.shape[0] // ROW_SUB

    def issue(g, carry):
        r0 = pl.multiple_of(g * ROW_UNROLL, ROW_UNROLL)
        for k in range(ROW_UNROLL):
            src = _row_slab(h_ref, r0 + k)
            _row_copy(src, _row_slab(xs_ref, pos0_ref[0, r0 + k]), sem).start(priority=0)
            _row_copy(src, _row_slab(xs_ref, pos1_ref[0, r0 + k]), sem).start(priority=1)
        return carry

    lax.fori_loop(0, ts // ROW_UNROLL, issue, 0)
    for _ in range(2):
        _row_copy(h_ref, xs_ref.at[pl.ds(0, ts * ROW_SUB), :], sem).wait()


def _scatter_rows(h2p, pos0, pos1):
    t = h2p.shape[0] // ROW_SUB
    ts = TS_ROWS
    smem = lambda: pl.BlockSpec((None, 1, ts), lambda i: (i, 0, 0), memory_space=pltpu.SMEM)
    return pl.pallas_call(
        _scatter_kernel,
        grid=(t // ts,),
        in_specs=[smem(), smem(), pl.BlockSpec((ts * ROW_SUB, LANES), lambda i: (i, 0))],
        out_specs=pl.BlockSpec(memory_space=pl.ANY),
        out_shape=jax.ShapeDtypeStruct((2 * t * ROW_SUB, LANES), F32),
        scratch_shapes=[pltpu.SemaphoreType.DMA(())],
        compiler_params=_cparams(("arbitrary",)),
        name="scatter",
    )(pos0.reshape(t // ts, 1, ts), pos1.reshape(t // ts, 1, ts), h2p)


def _expert_kernel(vt_ref, ve_ref, lo_ref, hi_ref, first_ref, last_ref, newe_ref,
                   xs_ref, wg_ref, wu_ref, wd_ref, y_ref, wgu_s, wd_s, acc_s):
    v = pl.program_id(0)
    tm = acc_s.shape[0]

    @pl.when(newe_ref[v] == 1)
    def _():
        wgu_s[:, 0:D_EXP] = wg_ref[...].astype(BF16)
        wgu_s[:, D_EXP:2 * D_EXP] = wu_ref[...].astype(BF16)
        wd_s[...] = wd_ref[...].astype(BF16)

    @pl.when(first_ref[v] == 1)
    def _():
        acc_s[...] = jnp.zeros_like(acc_s)

    lo = lo_ref[v]
    hi = hi_ref[v]

    @pl.when(hi > lo)
    def _():
        gu = _dot(_load_row_slabs(xs_ref).astype(BF16), wgu_s[...])
        hid = _silu(gu[:, 0:D_EXP]) * gu[:, D_EXP:2 * D_EXP]
        y = _dot(hid.astype(BF16), wd_s[...])
        rid = lax.broadcasted_iota(jnp.int32, (tm, D), 0)
        acc_s[...] = jnp.where((rid >= lo) & (rid < hi), y, acc_s[...])

    @pl.when(last_ref[v] == 1)
    def _():
        _store_row_slabs(y_ref, acc_s[...])


def _experts(xs, meta, w_gate, w_up, w_down):
    p = xs.shape[0] // ROW_SUB
    nv = meta[0].shape[0]
    tm = TM_EXP
    xmap = lambda v, vt, ve, lo, hi, fi, la, ne: (vt[v], 0)
    wmap = lambda v, vt, ve, lo, hi, fi, la, ne: (ve[v], 0, 0)
    return pl.pallas_call(
        _expert_kernel,
        grid_spec=pltpu.PrefetchScalarGridSpec(
            num_scalar_prefetch=7,
            grid=(nv,),
            in_specs=[pl.BlockSpec((tm * ROW_SUB, LANES), xmap),
                      pl.BlockSpec((None, D, D_EXP), wmap), pl.BlockSpec((None, D, D_EXP), wmap),
                      pl.BlockSpec((None, D_EXP, D), wmap)],
            out_specs=pl.BlockSpec((tm * ROW_SUB, LANES), xmap),
            scratch_shapes=[pltpu.VMEM((D, 2 * D_EXP), BF16), pltpu.VMEM((D_EXP, D), BF16),
                            pltpu.VMEM((tm, D), F32)]),
        out_shape=jax.ShapeDtypeStruct((p * ROW_SUB, LANES), F32),
        compiler_params=_cparams(("arbitrary",)),
        name="experts",
    )(*meta, xs, w_gate, w_up, w_down)


def _final_kernel(p0c_ref, p1c_ref, p0n_ref, p1n_ref, x1_ref, rt_ref, mod_ref, l2g_ref, l2b_ref, ys_ref,
                  oc_ref, ol_ref, buf, sem, *, n_ctx_tiles):
    i = pl.program_id(0)
    n = pl.num_programs(0)
    tm = x1_ref.shape[0]

    def gather(p0_ref, p1_ref, slot):
        def issue(g, carry):
            r0 = pl.multiple_of(g * ROW_UNROLL, ROW_UNROLL)
            for k in range(ROW_UNROLL):
                _row_copy(_row_slab(ys_ref, p0_ref[0, r0 + k]),
                          _row_slab(buf.at[slot, 0], r0 + k), sem.at[slot]).start(priority=0)
                _row_copy(_row_slab(ys_ref, p1_ref[0, r0 + k]),
                          _row_slab(buf.at[slot, 1], r0 + k), sem.at[slot]).start(priority=1)
            return carry

        lax.fori_loop(0, tm // ROW_UNROLL, issue, 0)

    cur = i % 2

    @pl.when(i == 0)
    def _():
        gather(p0c_ref, p1c_ref, 0)

    @pl.when(i + 1 < n)
    def _():
        gather(p0n_ref, p1n_ref, 1 - cur)

    for k in range(2):
        _row_copy(ys_ref.at[pl.ds(0, tm * ROW_SUB), :], buf.at[cur, k], sem.at[cur]).wait()

    g2 = mod_ref[0, :, 5 * D:6 * D]
    w1 = rt_ref[:, 2:3]
    w2 = rt_ref[:, 3:4]
    moe = w1 * _load_row_slabs(buf.at[cur, 0]) + w2 * _load_row_slabs(buf.at[cur, 1])
    out = _ln(ALPHA * x1_ref[...] + g2 * moe) * l2g_ref[...] + l2b_ref[...]

    @pl.when(i < n_ctx_tiles)
    def _():
        oc_ref[...] = out

    @pl.when(i >= n_ctx_tiles)
    def _():
        ol_ref[...] = out


def _final(x1, rt, pos0, pos1, mod_all, l2g, l2b, ys, t_ctx, seq_lat):
    t = x1.shape[0]
    tm = TM_TOK
    nt = t // tm
    nct = t_ctx // tm
    t_lat = t - t_ctx
    per_seq = seq_lat // tm
    nb_lat = t_lat // seq_lat
    p0 = pos0.reshape(nt, 1, tm)
    p1 = pos1.reshape(nt, 1, tm)
    row = lambda i: (i, 0)
    const = lambda i: (0, 0)
    mmap = lambda i: (jnp.where(i < nct, nb_lat, jnp.maximum(i - nct, 0) // per_seq), 0, 0)
    smem_cur = lambda: pl.BlockSpec((None, 1, tm), lambda i: (i, 0, 0), memory_space=pltpu.SMEM)
    smem_nxt = lambda: pl.BlockSpec((None, 1, tm), lambda i: (jnp.minimum(i + 1, nt - 1), 0, 0),
                                    memory_space=pltpu.SMEM)
    return pl.pallas_call(
        functools.partial(_final_kernel, n_ctx_tiles=nct),
        grid=(nt,),
        in_specs=[smem_cur(), smem_cur(), smem_nxt(), smem_nxt(),
                  pl.BlockSpec((tm, D), row), pl.BlockSpec((tm, LANES), row),
                  pl.BlockSpec((1, 1, 6 * D), mmap),
                  pl.BlockSpec((1, D), const), pl.BlockSpec((1, D), const),
                  pl.BlockSpec(memory_space=pl.ANY)],
        out_specs=[pl.BlockSpec((tm, D), lambda i: (jnp.minimum(i, nct - 1), 0)),
                   pl.BlockSpec((tm, D), lambda i: (jnp.maximum(i - nct, 0), 0))],
        out_shape=[jax.ShapeDtypeStruct((t_ctx, D), F32), jax.ShapeDtypeStruct((t_lat, D), F32)],
        scratch_shapes=[pltpu.VMEM((2, 2, tm * ROW_SUB, LANES), F32), pltpu.SemaphoreType.DMA((2,))],
        compiler_params=_cparams(("arbitrary",)),
        name="final",
    )(p0, p1, p0, p1, x1, rt, mod_all, l2g, l2b, ys)


def _reorder_q_heads(w, axis):
    shape = w.shape
    split = shape[:axis] + (N_KV_HEADS, N_Q_HEADS // N_KV_HEADS, HD) + shape[axis + 1:]
    return jnp.swapaxes(w.reshape(split), axis, axis + 1).reshape(shape)


def _rope_tables(seq):
    t = jnp.arange(seq, dtype=jnp.int32)
    row = (t // GRID_W).astype(F32)
    col = (t % GRID_W).astype(F32)
    half = HD // 4
    inv = ROPE_THETA ** (-jnp.arange(half, dtype=F32) / half)
    lane = np.arange(LANES)
    d64 = lane % HD
    use_row = jnp.asarray(d64 < HD // 2)
    freq = inv[jnp.asarray(d64 % half)]
    pos = jnp.where(use_row[None, :], row[:, None], col[:, None])
    ang = pos * freq[None, :]
    sign = jnp.asarray(np.where((d64 % 32) < 16, -1.0, 1.0), F32)
    return jnp.cos(ang), jnp.sin(ang) * sign[None, :]


def _pair_states(s):
    b = s.shape[0]
    s = s.reshape(b, 2, 2, GLA_DK, GLA_DV)
    return s.transpose(0, 1, 4, 2, 3).reshape(b, 2, GLA_DV, 2 * GLA_DK)


def _unpair_states(s):
    b = s.shape[0]
    s = s.reshape(b, 2, GLA_DV, 2, GLA_DK)
    return s.transpose(0, 1, 3, 4, 2).reshape(b, GLA_H, GLA_DK, GLA_DV)


def _route_tables(ert, counts, t):
    i32 = jnp.int32
    cnt = counts[0, :N_EXP].astype(i32)
    ends = jnp.cumsum(cnt)
    starts = ends - cnt
    table = lambda e: jnp.sum(jnp.where(e[None, :] == jnp.arange(N_EXP, dtype=i32)[:, None],
                                        starts[:, None], 0), axis=0)
    pos0 = table(ert[0].astype(i32)) + ert[4].astype(i32)
    pos1 = table(ert[1].astype(i32)) + ert[5].astype(i32)
    p = 2 * t
    nt = p // TM_EXP
    nv = nt + N_EXP
    tile_starts = jnp.arange(nt, dtype=i32) * TM_EXP
    idx_t = jnp.arange(nt, dtype=i32) + jnp.sum(starts[None, :] < tile_starts[:, None], axis=1, dtype=i32)
    idx_e = jnp.arange(N_EXP, dtype=i32) + jnp.minimum(starts // TM_EXP + 1, nt)
    k = jnp.arange(nv, dtype=i32)[:, None]
    bounds = (jnp.sum(jnp.where(idx_t[None, :] == k, tile_starts[None, :], 0), axis=1)
              + jnp.sum(jnp.where(idx_e[None, :] == k, starts[None, :], 0), axis=1))
    nxt = jnp.concatenate([bounds[1:], jnp.array([p], i32)])
    tile = jnp.minimum(bounds // TM_EXP, nt - 1)
    exp_id = jnp.minimum(jnp.sum(ends[None, :] <= bounds[:, None], axis=1, dtype=i32), N_EXP - 1)
    lo = bounds - tile * TM_EXP
    hi = jnp.where(nxt > bounds, jnp.minimum(nxt - tile * TM_EXP, TM_EXP), lo)
    prev_tile = jnp.concatenate([jnp.array([-1], i32), tile[:-1]])
    prev_exp = jnp.concatenate([jnp.array([-1], i32), exp_id[:-1]])
    next_tile = jnp.concatenate([tile[1:], jnp.array([-1], i32)])
    first = (tile != prev_tile).astype(i32)
    last = (tile != next_tile).astype(i32)
    newe = (exp_id != prev_exp).astype(i32)
    return pos0, pos1, (tile, exp_id, lo, hi, first, last, newe)


def kernel(x_prompt, x_sample, cache_k, cache_v, state_gla_fwd, state_gla_bwd, c, c_ctx, w_ada, b_ada, w_in, q_norm, k_norm, gla_w_gate, gla_b_gate, gla_norm, w_br_attn, w_br_gla, w_out, ln1_g, ln1_b, router_group_w, router_group_b, router_expert_w, router_expert_b, exp_w_gate, exp_w_up, exp_w_down, ln2_g, ln2_b):
    b_ctx, seq_ctx, _ = x_prompt.shape
    b_lat, seq_lat, _ = x_sample.shape
    t_ctx, t_lat = b_ctx * seq_ctx, b_lat * seq_lat
    t = t_ctx + t_lat
    l = 0

    rows = -(-(b_lat + 1) // 8) * 8
    c_rows = jnp.zeros((rows, D), F32).at[:b_lat].set(c).at[b_lat].set(c_ctx)
    mod = _ada(c_rows, w_ada[l], b_ada[l][None, :])
    mod_all = mod[:b_lat + 1, None, :]
    mod_lat = mod_all[:b_lat]
    mod_ctx = mod_all[b_lat:]

    w_full = w_in[l]
    w_a = jnp.concatenate([_reorder_q_heads(w_full[:, :AW], 1), w_full[:, AW:A_WIDTH]], axis=1).astype(BF16)
    w_mg = w_full[:, A_WIDTH:].astype(BF16)
    gain = jnp.concatenate([jnp.tile(q_norm[l], N_Q_HEADS), jnp.tile(k_norm[l], N_KV_HEADS)])[None, :]
    head_of = np.arange(AW + KVW) // HD
    ind = jnp.asarray((head_of[:, None] == np.arange(LANES)[None, :]) / HD, BF16)
    w_ba = _reorder_q_heads(w_br_attn[l], 0).astype(BF16)
    w_bg = w_br_gla[l].astype(BF16)
    w_o = w_out[l].astype(BF16)
    w_r = jnp.zeros((D, LANES), F32).at[:, :N_GROUPS].set(router_group_w[l])
    w_r = w_r.at[:, N_GROUPS:N_GROUPS + N_EXP].set(router_expert_w[l]).astype(BF16)
    b_r = jnp.zeros((1, LANES), F32).at[0, :N_GROUPS].set(router_group_b[l])
    b_r = b_r.at[0, N_GROUPS:N_GROUPS + N_EXP].set(router_expert_b[l])
    wg = jnp.zeros((2, 2 * GATE_RANK, GKW), F32)
    wg = wg.at[0, :GATE_RANK].set(gla_w_gate[l, 0]).at[1, GATE_RANK:].set(gla_w_gate[l, 1])
    bg = gla_b_gate[l][:, None, :]
    gn = gla_norm[l][None, :]

    xc = x_prompt.reshape(t_ctx, D)
    xl = x_sample.reshape(t_lat, D)

    q_c, k_c, v_c, qg_c, kg_c, vg_c, rs_c, lr_c, kf_c, vf_c = _inproj(
        xc, mod_ctx, w_a, gain, ind, None, seq_ctx, latent=False)
    attn_c = _attention(q_c, k_c, v_c, None, seq_ctx)
    zero_state = jnp.zeros((1, 2, 2, LANES, LANES), F32)
    gla_c, sfin_c = _gla(qg_c, kg_c, vg_c, lr_c, rs_c, zero_state, wg, bg, gn, seq_ctx)

    q_l, k_l, v_l, qg_l, kg_l, vg_l, rs_l, lr_l = _inproj(
        xl, mod_lat, w_a, gain, ind, _rope_tables(seq_lat), seq_lat, latent=True)
    past = cache_k.shape[2]
    kc = cache_k[:, l].reshape(b_lat, past, KVW).astype(BF16)
    vc = cache_v[:, l].reshape(b_lat, past, KVW).astype(BF16)
    attn_l = _attention(q_l, k_l, v_l, (kc, vc), seq_lat)
    s0 = jnp.stack([_pair_states(state_gla_fwd[:, l]), _pair_states(state_gla_bwd[:, l])], axis=1)
    gla_l, _ = _gla(qg_l, kg_l, vg_l, lr_l, rs_l, s0, wg, bg, gn, seq_lat)

    x1, h2p, rt, ert, counts = _post(xc, xl, attn_c, attn_l, gla_c, gla_l, mod_all, seq_lat,
                                w_mg, w_ba, w_bg, w_o, ln1_g[l][None, :], ln1_b[l][None, :], w_r, b_r)

    pos0, pos1, meta = _route_tables(ert, counts, t)
    xs = _scatter_rows(h2p, pos0, pos1)
    ys = _experts(xs, meta, exp_w_gate[l], exp_w_up[l], exp_w_down[l])
    y_ctx, y_lat = _final(x1, rt, pos0, pos1, mod_all, ln2_g[l][None, :], ln2_b[l][None, :], ys,
                          t_ctx, seq_lat)

    new_k, new_v = kf_c, vf_c
    new_sf = _unpair_states(sfin_c[:, 0])[:, None]
    new_sb = _unpair_states(sfin_c[:, 1])[:, None]
    return (y_ctx.reshape(b_ctx, seq_ctx, D), y_lat.reshape(b_lat, seq_lat, D),
            new_k, new_v, new_sf, new_sb)
```

```python
import functools

import numpy as np
import jax
import jax.numpy as jnp
from jax import lax
from jax.experimental import pallas as pl
from jax.experimental.pallas import tpu as pltpu

F32 = jnp.float32
BF16 = jnp.bfloat16
HIGHEST = lax.Precision.HIGHEST

D = 1024
GRID_W = 64
HD = 64
N_Q_HEADS = 8
N_KV_HEADS = 2
AW = N_Q_HEADS * HD
KVW = N_KV_HEADS * HD
ROPE_THETA = 10000.0
GLA_H = 4
GLA_DK = 64
GLA_DV = 128
GKW = GLA_H * GLA_DK
GVW = GLA_H * GLA_DV
GATE_RANK = 16
GLA_TAU = 16.0
CHUNK = 64
N_GROUPS = 4
EPG = 8
N_EXP = N_GROUPS * EPG
D_EXP = 256
DEPTH = 1
ALPHA = (2.0 * DEPTH) ** 0.25
LN_EPS = 1e-6
RMS_EPS = 1e-6

LANES = 128
A_WIDTH = AW + 2 * KVW + 2 * GKW + 2 * GVW + 2 * GATE_RANK
MG_WIDTH = 2 * D
TM_TOK = 512
ROW_GROUP = 128
LOG2E = 1.4426950408889634
ONES_ROWS = 16
TQ_LAT = 128
TM_EXP = 256
TS_ROWS = 256
VMEM_LIMIT = 56 * 1024 * 1024


def _cparams(sem):
    return pltpu.CompilerParams(dimension_semantics=sem, vmem_limit_bytes=VMEM_LIMIT)


def _dot(a, b):
    return jnp.dot(a, b, preferred_element_type=F32)


def _dot_nt(a, b):
    return lax.dot_general(a, b, (((1,), (1,)), ((), ())), preferred_element_type=F32)


def _dot_tn(a, b):
    return lax.dot_general(a, b, (((0,), (0,)), ((), ())), preferred_element_type=F32)


def _ln(x):
    mu = jnp.mean(x, axis=-1, keepdims=True)
    xc = x - mu
    var = jnp.mean(xc * xc, axis=-1, keepdims=True)
    return xc * lax.rsqrt(var + LN_EPS)


def _silu(x):
    return x * jax.nn.sigmoid(x)


def _split_bf16(x):
    hi = x.astype(BF16)
    lo = (x - hi.astype(F32)).astype(BF16)
    return hi, lo


def _ada_kernel(c_ref, w_ref, b_ref, o_ref):
    s = _silu(c_ref[...])
    o_ref[...] = jnp.dot(s, w_ref[...], preferred_element_type=F32, precision=HIGHEST) + b_ref[...]


def _ada(c_rows, w_ada, b_ada):
    rows = c_rows.shape[0]
    n = w_ada.shape[1]
    bn = 1024
    return pl.pallas_call(
        _ada_kernel,
        grid=(n // bn,),
        in_specs=[pl.BlockSpec((rows, D), lambda j: (0, 0)),
                  pl.BlockSpec((D, bn), lambda j: (0, j)),
                  pl.BlockSpec((1, bn), lambda j: (0, j))],
        out_specs=pl.BlockSpec((rows, bn), lambda j: (0, j)),
        out_shape=jax.ShapeDtypeStruct((rows, n), F32),
        compiler_params=_cparams(("arbitrary",)),
        name="ada",
    )(c_rows, w_ada, b_ada)


def _inproj_kernel(*refs, latent):
    if latent:
        (x_ref, mod_ref, w_ref, gain_ref, ind_ref, cos_ref, sin_ref,
         q_ref, k_ref, v_ref, qg_ref, kg_ref, vg_ref, rs_ref, lr_ref) = refs
    else:
        (x_ref, mod_ref, w_ref, gain_ref, ind_ref,
         q_ref, k_ref, v_ref, qg_ref, kg_ref, vg_ref, rs_ref, lr_ref, kf_ref, vf_ref) = refs
    tm = x_ref.shape[0]
    sub = ROW_GROUP
    n_groups = tm // sub
    sh1 = mod_ref[0, :, 0:D]
    sc1 = mod_ref[0, :, D:2 * D]
    lane = lax.broadcasted_iota(jnp.int32, (sub, LANES), 1)
    low_half = lane < HD
    first = (lane % 32) < 16

    def project(g):
        rows = slice(g * sub, (g + 1) * sub)
        h = (_ln(x_ref[rows, :]) * (1.0 + sc1) + sh1).astype(BF16)
        return _dot(h, w_ref[...])

    def finish(g, res):
        rows = slice(g * sub, (g + 1) * sub)
        qk = res[:, 0:AW + KVW]
        hi, lo = _split_bf16(qk * qk)
        ms = _dot(hi, ind_ref[...]) + _dot(lo, ind_ref[...])
        r = lax.rsqrt(ms + RMS_EPS)
        for s in range(5):
            rb = jnp.where(low_half, r[:, 2 * s:2 * s + 1], r[:, 2 * s + 1:2 * s + 2])
            y = res[:, LANES * s:LANES * (s + 1)] * rb * gain_ref[:, LANES * s:LANES * (s + 1)]
            if s == 4 and not latent:
                for j in range(N_KV_HEADS):
                    kf_ref[rows, j, :] = y[:, HD * j:HD * (j + 1)]
            if latent:
                partner = jnp.where(first, pltpu.roll(y, LANES - 16, 1), pltpu.roll(y, 16, 1))
                y = y * cos_ref[rows, :] + partner * sin_ref[rows, :]
            if s < 4:
                q_ref[rows, LANES * s:LANES * (s + 1)] = (y * (HD ** -0.5 * LOG2E)).astype(BF16)
            else:
                k_ref[rows, :] = y.astype(BF16)
        o = AW + KVW
        v = res[:, o:o + KVW]
        v_ref[rows, :] = v.astype(BF16)
        if not latent:
            for j in range(N_KV_HEADS):
                vf_ref[rows, j, :] = v[:, HD * j:HD * (j + 1)]
        o += KVW
        qg_ref[rows, :] = res[:, o:o + GKW] * (GLA_DK ** -0.5)
        o += GKW
        kg_ref[rows, :] = res[:, o:o + GKW]
        o += GKW
        vg_ref[rows, :] = res[:, o:o + GVW].astype(BF16)
        o += GVW
        rs_ref[rows, :] = _silu(res[:, o:o + GVW]).astype(BF16)
        o += GVW
        lr_ref[rows, :] = res[:, o:o + 2 * GATE_RANK]

    pending = {0: project(0)}
    for g in range(n_groups):
        if g + 1 < n_groups:
            pending[g + 1] = project(g + 1)
        finish(g, pending.pop(g))


def _inproj(x2, mod, w_a, gain, ind, rope, seq, latent):
    t = x2.shape[0]
    tm = min(TM_TOK, seq)
    per_seq = seq // tm
    row = lambda i: (i, 0)
    const = lambda i: (0, 0)
    in_specs = [pl.BlockSpec((tm, D), row),
                pl.BlockSpec((1, 1, 6 * D), (lambda i: (i // per_seq, 0, 0)) if latent else (lambda i: (0, 0, 0))),
                pl.BlockSpec((D, A_WIDTH), const),
                pl.BlockSpec((1, AW + KVW), const),
                pl.BlockSpec((AW + KVW, LANES), const)]
    args = [x2, mod, w_a, gain, ind]
    if latent:
        in_specs += [pl.BlockSpec((tm, LANES), lambda i: (i % per_seq, 0))] * 2
        args += list(rope)
    widths = [(AW, BF16), (KVW, BF16), (KVW, BF16), (GKW, F32), (GKW, F32), (GVW, BF16), (GVW, BF16),
              (2 * GATE_RANK, F32)]
    out_specs = [pl.BlockSpec((tm, w), row) for w, _ in widths]
    out_shape = [jax.ShapeDtypeStruct((t, w), dt) for w, dt in widths]
    if not latent:
        cache_spec = pl.BlockSpec((None, None, tm, N_KV_HEADS, HD),
                                  lambda i: (i // per_seq, 0, i % per_seq, 0, 0))
        out_specs += [cache_spec] * 2
        out_shape += [jax.ShapeDtypeStruct((t // seq, 1, seq, N_KV_HEADS, HD), F32)] * 2
    return pl.pallas_call(
        functools.partial(_inproj_kernel, latent=latent),
        grid=(t // tm,),
        in_specs=in_specs,
        out_specs=out_specs,
        out_shape=out_shape,
        compiler_params=_cparams(("parallel",)),
        name="inproj_lat" if latent else "inproj_ctx",
    )(*args)


def _attn_kernel(*refs, has_cache):
    def transposed_with_ones(dst, src):
        dst[0:KVW, :] = src[...].astype(F32).T.astype(BF16)
        dst[KVW:, :] = jnp.ones((ONES_ROWS, dst.shape[1]), BF16)

    if has_cache:
        q_ref, k_ref, v_ref, kc_ref, vc_ref, o_ref, vt_s, vct_s = refs

        @pl.when(pl.program_id(1) == 0)
        def _():
            transposed_with_ones(vt_s, v_ref)
            transposed_with_ones(vct_s, vc_ref)
    else:
        q_ref, k_ref, v_ref, o_ref, vt_s = refs
        transposed_with_ones(vt_s, v_ref)
    tq = q_ref.shape[0]

    lane = lax.broadcasted_iota(jnp.int32, (tq, LANES), 1)
    low_half = lane < HD
    k = k_ref[...]
    scores = []
    for j in range(N_KV_HEADS):
        keep = low_half if j == 0 else jnp.logical_not(low_half)
        zero = jnp.zeros((tq, LANES), BF16)
        for pair in range(2):
            qs = jnp.concatenate([jnp.where(keep, q_ref[:, LANES * s:LANES * (s + 1)], zero)
                                  for s in (2 * pair, 2 * pair + 1)], axis=0)
            s1 = _dot_nt(k, qs)
            s2 = _dot_nt(kc_ref[...], qs) if has_cache else None
            scores.append((s1, s2))
    outs = []
    for s1, s2 in scores:
        m = jnp.max(s1, axis=0, keepdims=True)
        if has_cache:
            m = jnp.maximum(m, jnp.max(s2, axis=0, keepdims=True))
        acc = _dot(vt_s[...], jnp.exp2(s1 - m).astype(BF16))
        if has_cache:
            acc = acc + _dot(vct_s[...], jnp.exp2(s2 - m).astype(BF16))
        outs.append(acc[0:KVW] / acc[KVW:KVW + 1])
    head0 = jnp.concatenate(outs[0:2], axis=1)
    head1 = jnp.concatenate(outs[2:4], axis=1)
    row = lax.broadcasted_iota(jnp.int32, (LANES, 4 * tq), 0)
    out = jnp.where(row < HD, head0, head1).T
    for s in range(4):
        o_ref[:, LANES * s:LANES * (s + 1)] = out[s * tq:(s + 1) * tq].astype(BF16)


def _attention(q, k, v, cache, seq):
    t = q.shape[0]
    if cache is None:
        tq = seq
        grid = (t // seq,)
        qmap = lambda b: (b, 0)
        in_specs = [pl.BlockSpec((tq, AW), qmap), pl.BlockSpec((seq, KVW), qmap),
                    pl.BlockSpec((seq, KVW), qmap)]
        args = [q, k, v]
        scratch = [pltpu.VMEM((KVW + ONES_ROWS, seq), BF16)]
        sem = ("parallel",)
        name = "attn_ctx"
    else:
        tq = TQ_LAT
        nq = seq // tq
        kc, vc = cache
        past = kc.shape[1]
        grid = (t // seq, nq)
        qmap = lambda b, i: (b * nq + i, 0)
        kmap = lambda b, i: (b, 0)
        cmap = lambda b, i: (b, 0, 0)
        in_specs = [pl.BlockSpec((tq, AW), qmap), pl.BlockSpec((seq, KVW), kmap),
                    pl.BlockSpec((seq, KVW), kmap),
                    pl.BlockSpec((None, past, KVW), cmap), pl.BlockSpec((None, past, KVW), cmap)]
        args = [q, k, v, kc, vc]
        scratch = [pltpu.VMEM((KVW + ONES_ROWS, seq), BF16), pltpu.VMEM((KVW + ONES_ROWS, past), BF16)]
        sem = ("parallel", "arbitrary")
        name = "attn_lat"
    return pl.pallas_call(
        functools.partial(_attn_kernel, has_cache=cache is not None),
        grid=grid,
        in_specs=in_specs,
        out_specs=pl.BlockSpec((tq, AW), qmap),
        out_shape=jax.ShapeDtypeStruct((t, AW), BF16),
        scratch_shapes=scratch,
        compiler_params=_cparams(sem),
        name=name,
    )(*args)


GLA_BLK = 256
GLA_UNROLL = 2


def _split3_bf16(x):
    hi = x.astype(BF16)
    r1 = x - hi.astype(F32)
    mid = r1.astype(BF16)
    lo = (r1 - mid.astype(F32)).astype(BF16)
    return hi, mid, lo


def _gla_kernel(qg_ref, kg_ref, vg_ref, lr_ref, rs_ref, s0_ref, wg_ref, bg_ref, gn_ref,
                o_ref, sfin_ref, cum_s, kv_s, dec_s):
    n = qg_ref.shape[0]
    nc = n // CHUNK
    lane = lax.broadcasted_iota(jnp.int32, (CHUNK, LANES), 1)
    low_half = lane < GLA_DK
    lane_sq = lax.broadcasted_iota(jnp.int32, (LANES, LANES), 1)
    low_half_sq = lane_sq < GLA_DK
    ri = lax.broadcasted_iota(jnp.int32, (2 * CHUNK, 2 * CHUNK), 0)
    ci = lax.broadcasted_iota(jnp.int32, (2 * CHUNK, 2 * CHUNK), 1)
    diag = (ri >> 6) == (ci >> 6)
    keep2 = (diag & (ri >= ci), diag & (ci >= ri))

    rb = lax.broadcasted_iota(jnp.int32, (GLA_BLK, GLA_BLK), 0)
    cb = lax.broadcasted_iota(jnp.int32, (GLA_BLK, GLA_BLK), 1)
    same = (rb >> 6) == (cb >> 6)
    tri = ((same & (rb >= cb)).astype(BF16), (same & (cb >= rb)).astype(BF16))
    for d in range(2):
        w_hi, w_lo = _split_bf16(wg_ref[d])
        for blk in range(n // GLA_BLK):
            rows = slice(blk * GLA_BLK, (blk + 1) * GLA_BLK)
            l_hi, l_lo = _split_bf16(lr_ref[rows, :])
            z = _dot(l_hi, w_hi) + _dot(l_lo, w_hi) + _dot(l_hi, w_lo) + bg_ref[d]
            logg = (jnp.minimum(z, 0.0) - jnp.log(1.0 + jnp.exp(-jnp.abs(z)))) * (1.0 / GLA_TAU)
            pieces = _split3_bf16(logg)
            cum_s[d, rows, :] = _dot(tri[d], pieces[0]) + _dot(tri[d], pieces[1]) + _dot(tri[d], pieces[2])

    def pass1(i, carry):
        cs = [i * GLA_UNROLL + u for u in range(GLA_UNROLL)]
        rws = [pl.ds(pl.multiple_of(c * CHUNK, CHUNK), CHUNK) for c in cs]
        prods = []
        for c, rows in zip(cs, rws):
            kc = kg_ref[rows, :]
            kdec = []
            for d in range(2):
                cum = cum_s[d, rows, :]
                last = cum[CHUNK - 1:CHUNK, :] if d == 0 else cum[0:1, :]
                dec_s[d, c] = jnp.exp(last)
                kdec.append((kc * jnp.exp(last - cum)).astype(BF16))
            for p in range(2):
                vpair = vg_ref[rows, GLA_DV * 2 * p:GLA_DV * 2 * (p + 1)]
                kpair = jnp.concatenate([kdec[0][:, LANES * p:LANES * (p + 1)],
                                         kdec[1][:, LANES * p:LANES * (p + 1)]], axis=1)
                prods.append(_dot_tn(vpair, kpair))
        for j, c in enumerate(cs):
            for p in range(2):
                res = prods[2 * j + p]
                for d in range(2):
                    cols = slice(LANES * d, LANES * (d + 1))
                    kv_s[d, c, p] = jnp.where(low_half_sq, res[0:GLA_DV, cols], res[GLA_DV:2 * GLA_DV, cols])
        return carry

    lax.fori_loop(0, nc // GLA_UNROLL, pass1, 0)

    for d in range(2):
        def scan(i, st):
            c = i if d == 0 else nc - 1 - i
            dec = dec_s[d, c]
            new = []
            for p in range(2):
                kv = kv_s[d, c, p]
                kv_s[d, c, p] = st[p]
                new.append(st[p] * dec[:, LANES * p:LANES * (p + 1)] + kv)
            return tuple(new)

        fin = lax.fori_loop(0, nc, scan, (s0_ref[d, 0], s0_ref[d, 1]))
        sfin_ref[d, 0] = fin[0]
        sfin_ref[d, 1] = fin[1]

    def pass3(i, carry):
        cs = [i * GLA_UNROLL + u for u in range(GLA_UNROLL)]
        rws = [pl.ds(pl.multiple_of(c * CHUNK, CHUNK), CHUNK) for c in cs]
        first = []
        for c, rows in zip(cs, rws):
            q = qg_ref[rows, :]
            k = kg_ref[rows, :]
            for d in range(2):
                cum = cum_s[d, rows, :]
                qt = q * jnp.exp(cum)
                kt = (k * jnp.exp(-cum)).astype(BF16)
                for p in range(2):
                    qs = qt[:, LANES * p:LANES * (p + 1)]
                    lhs = jnp.concatenate([jnp.where(low_half, qs, 0.0), jnp.where(low_half, 0.0, qs)],
                                          axis=0).astype(BF16)
                    kts = kt[:, LANES * p:LANES * (p + 1)]
                    rhs = jnp.concatenate([kts, kts, kv_s[d, c, p].astype(BF16)], axis=0)
                    first.append(_dot_nt(lhs, rhs))
        second = []
        for j, rows in enumerate(rws):
            for d in range(2):
                for p in range(2):
                    res = first[4 * j + 2 * d + p]
                    vp = jnp.concatenate([vg_ref[rows, GLA_DV * (2 * p):GLA_DV * (2 * p + 1)],
                                          vg_ref[rows, GLA_DV * (2 * p + 1):GLA_DV * (2 * p + 2)]], axis=0)
                    a = jnp.where(keep2[d], res[:, 0:2 * CHUNK], 0.0).astype(BF16)
                    second.append(_dot(a, vp) + res[:, 2 * CHUNK:])
        for j, rows in enumerate(rws):
            for p in range(2):
                tot = second[4 * j + p] + second[4 * j + 2 + p]
                y = tot * lax.rsqrt(jnp.mean(tot * tot, axis=-1, keepdims=True) + RMS_EPS) * gn_ref[...]
                for hh in range(2):
                    cols = slice(GLA_DV * (2 * p + hh), GLA_DV * (2 * p + hh + 1))
                    o_ref[rows, cols] = (y[CHUNK * hh:CHUNK * (hh + 1)]
                                         * rs_ref[rows, cols].astype(F32)).astype(BF16)
        return carry

    lax.fori_loop(0, nc // GLA_UNROLL, pass3, 0)


def _gla(qg, kg, vg, lr, rs, s0, wg, bg, gn, seq):
    t = qg.shape[0]
    nb = t // seq
    nc = seq // CHUNK
    row = lambda b: (b, 0)
    c3 = lambda b: (0, 0, 0)
    st = lambda b: (b, 0, 0, 0, 0)
    if s0.shape[0] == 1:
        s0map = lambda b: (0, 0, 0, 0, 0)
    else:
        s0map = st
    return pl.pallas_call(
        _gla_kernel,
        grid=(nb,),
        in_specs=[pl.BlockSpec((seq, GKW), row), pl.BlockSpec((seq, GKW), row),
                  pl.BlockSpec((seq, GVW), row), pl.BlockSpec((seq, 2 * GATE_RANK), row),
                  pl.BlockSpec((seq, GVW), row),
                  pl.BlockSpec((None, 2, 2, LANES, LANES), s0map),
                  pl.BlockSpec((2, 2 * GATE_RANK, GKW), c3), pl.BlockSpec((2, 1, GKW), c3),
                  pl.BlockSpec((1, GLA_DV), lambda b: (0, 0))],
        out_specs=[pl.BlockSpec((seq, GVW), row),
                   pl.BlockSpec((None, 2, 2, LANES, LANES), st)],
        out_shape=[jax.ShapeDtypeStruct((t, GVW), BF16),
                   jax.ShapeDtypeStruct((nb, 2, 2, LANES, LANES), F32)],
        scratch_shapes=[pltpu.VMEM((2, seq, GKW), F32),
                        pltpu.VMEM((2, nc, 2, LANES, LANES), F32),
                        pltpu.VMEM((2, nc, 1, GKW), F32)],
        compiler_params=_cparams(("parallel",)),
        name="gla_lat" if seq > 256 else "gla_ctx",
    )(qg, kg, vg, lr, rs, s0, wg, bg, gn)


ROW_SUB = D // LANES


def _store_row_slabs(ref, x, tmp):
    m = x.shape[0]
    for c in range(ROW_SUB):
        tmp[pl.ds(c, m, stride=ROW_SUB), :] = x[:, LANES * c:LANES * (c + 1)]
    ref[...] = tmp[...].reshape(m, ROW_SUB, LANES).astype(BF16)


def _load_row_slabs(ref, tmp):
    m = ref.shape[0]
    tmp[...] = ref[...].astype(F32).reshape(m * ROW_SUB, LANES)
    return jnp.concatenate([tmp[pl.ds(c, m, stride=ROW_SUB), :] for c in range(ROW_SUB)], axis=1)


def _row_slab(ref, row):
    return ref.at[pl.ds(row, 1)]


def _post_kernel(xc_ref, xl_ref, ac_ref, al_ref, gc_ref, gl_ref, mod_ref,
                 wmg_ref, wba_ref, wbg_ref, wo_ref, l1g_ref, l1b_ref, wr_ref, br_ref,
                 x1_ref, h2_ref, rt_ref, ert_ref, cnt_ref, run_s, slab_s, *, n_ctx_tiles):
    i = pl.program_id(0)
    tm = xc_ref.shape[0]
    sub = ROW_GROUP
    n_groups = tm // sub
    is_ctx = i < n_ctx_tiles
    sh1 = mod_ref[0, :, 0:D]
    sc1 = mod_ref[0, :, D:2 * D]
    g1 = mod_ref[0, :, 2 * D:3 * D]
    sh2 = mod_ref[0, :, 3 * D:4 * D]
    sc2 = mod_ref[0, :, 4 * D:5 * D]

    parts = [slice(g * sub, (g + 1) * sub) for g in range(n_groups)]
    xs = [jnp.where(is_ctx, xc_ref[r, :], xl_ref[r, :]) for r in parts]
    hs = [(_ln(x) * (1.0 + sc1) + sh1).astype(BF16) for x in xs]
    gates = [jax.nn.sigmoid(_dot(h, wmg_ref[...])) for h in hs]
    ba = [_dot(jnp.where(is_ctx, ac_ref[r, :], al_ref[r, :]), wba_ref[...]) for r in parts]
    bg = [_dot(jnp.where(is_ctx, gc_ref[r, :], gl_ref[r, :]), wbg_ref[...]) for r in parts]
    merged = [(g[:, :D] * a + g[:, D:] * b).astype(BF16) for g, a, b in zip(gates, ba, bg)]
    mix = [_dot(m, wo_ref[...]) for m in merged]
    x1s = [_ln(ALPHA * x + g1 * m) * l1g_ref[...] + l1b_ref[...] for x, m in zip(xs, mix)]
    h2s = [_ln(x1) * (1.0 + sc2) + sh2 for x1 in x1s]
    logits = [_dot(h2.astype(BF16), wr_ref[...]) + br_ref[...] for h2 in h2s]
    for g, r in enumerate(parts):
        x1_ref[r, :] = x1s[g]
        _store_row_slabs(h2_ref.at[pl.ds(g * sub, sub)], h2s[g], slab_s)

    @pl.when(i == 0)
    def _():
        run_s[...] = jnp.zeros_like(run_s)

    ri = lax.broadcasted_iota(jnp.int32, (sub, sub), 0)
    ci = lax.broadcasted_iota(jnp.int32, (sub, sub), 1)
    earlier = (ri > ci).astype(BF16)
    run = run_s[0:1, :]
    for g, r in enumerate(parts):
        run = _route(logits[g], earlier, run, rt_ref.at[r, :], ert_ref.at[:, r])
    run_s[0:1, :] = run
    cnt_ref[...] = jnp.broadcast_to(run, cnt_ref.shape)


def _route(logit, earlier, run, rt_ref, ert_ref):
    tm = logit.shape[0]
    lane_i = lax.broadcasted_iota(jnp.int32, (tm, LANES), 1)
    lane = lane_i.astype(F32)
    lane_grp = ((lane_i - N_GROUPS) >> 3).astype(F32)
    neg = jnp.float32(-jnp.inf)
    far = jnp.float32(LANES)
    is_g = lane_i < N_GROUPS
    lg = jnp.where(is_g, logit, neg)
    mg = jnp.max(lg, axis=-1, keepdims=True)
    pg_top = 1.0 / jnp.sum(jnp.where(is_g, jnp.exp(logit - mg), 0.0), axis=-1, keepdims=True)
    g_idx = jnp.min(jnp.where(lg == mg, lane, far), axis=-1, keepdims=True)
    in_grp = (lane_i >= N_GROUPS) & (lane_i < N_GROUPS + N_EXP) & (lane_grp == g_idx)
    le = jnp.where(in_grp, logit, neg)
    v1 = jnp.max(le, axis=-1, keepdims=True)
    i1 = jnp.min(jnp.where(le == v1, lane, far), axis=-1, keepdims=True)
    le2 = jnp.where(lane == i1, neg, le)
    v2 = jnp.max(le2, axis=-1, keepdims=True)
    i2 = jnp.min(jnp.where(le2 == v2, lane, far), axis=-1, keepdims=True)
    e1 = i1 - N_GROUPS
    e2 = i2 - N_GROUPS
    tt = jnp.exp(v2 - v1)
    w1 = pg_top / (1.0 + tt)
    w2 = pg_top * tt / (1.0 + tt)

    hot = ((lane == e1) | (lane == e2)).astype(F32)
    before = _dot(earlier, hot.astype(BF16)) + run
    r1 = jnp.sum(jnp.where(lane == e1, before, 0.0), axis=-1, keepdims=True)
    r2 = jnp.sum(jnp.where(lane == e2, before, 0.0), axis=-1, keepdims=True)

    rt = jnp.where(lane_i == 0, e1, 0.0)
    rt = jnp.where(lane_i == 1, e2, rt)
    rt = jnp.where(lane_i == 2, w1, rt)
    rt = jnp.where(lane_i == 3, w2, rt)
    rt = jnp.where(lane_i == 4, r1, rt)
    rt = jnp.where(lane_i == 5, r2, rt)
    rt_ref[...] = rt
    ert_ref[...] = rt.T[0:8, :]
    return run + jnp.sum(hot, axis=0, keepdims=True)


def _post(x_ctx, x_lat, a_ctx, a_lat, g_ctx, g_lat, mod_all, seq_lat,
          w_mg, w_ba, w_bg, w_o, l1g, l1b, w_r, b_r):
    t_ctx, t_lat = x_ctx.shape[0], x_lat.shape[0]
    tm = TM_TOK
    nct, nlt = t_ctx // tm, t_lat // tm
    per_seq = seq_lat // tm
    nb_lat = t_lat // seq_lat
    t = t_ctx + t_lat
    cmap = lambda i: (jnp.minimum(i, nct - 1), 0)
    lmap = lambda i: (jnp.maximum(i - nct, 0), 0)
    mmap = lambda i: (jnp.where(i < nct, nb_lat, jnp.maximum(i - nct, 0) // per_seq), 0, 0)
    row = lambda i: (i, 0)
    const = lambda i: (0, 0)
    return pl.pallas_call(
        functools.partial(_post_kernel, n_ctx_tiles=nct),
        grid=(nct + nlt,),
        in_specs=[pl.BlockSpec((tm, D), cmap), pl.BlockSpec((tm, D), lmap),
                  pl.BlockSpec((tm, AW), cmap), pl.BlockSpec((tm, AW), lmap),
                  pl.BlockSpec((tm, GVW), cmap), pl.BlockSpec((tm, GVW), lmap),
                  pl.BlockSpec((1, 1, 6 * D), mmap),
                  pl.BlockSpec((D, MG_WIDTH), const), pl.BlockSpec((AW, D), const),
                  pl.BlockSpec((GVW, D), const), pl.BlockSpec((D, D), const),
                  pl.BlockSpec((1, D), const), pl.BlockSpec((1, D), const),
                  pl.BlockSpec((D, LANES), const), pl.BlockSpec((1, LANES), const)],
        out_specs=[pl.BlockSpec((tm, D), row), pl.BlockSpec((tm, ROW_SUB, LANES), lambda i: (i, 0, 0)),
                   pl.BlockSpec((tm, LANES), row), pl.BlockSpec((8, tm), lambda i: (0, i)),
                   pl.BlockSpec((8, LANES), const)],
        out_shape=[jax.ShapeDtypeStruct((t, D), F32), jax.ShapeDtypeStruct((t, ROW_SUB, LANES), BF16),
                   jax.ShapeDtypeStruct((t, LANES), F32), jax.ShapeDtypeStruct((8, t), F32),
                   jax.ShapeDtypeStruct((8, LANES), F32)],
        scratch_shapes=[pltpu.VMEM((8, LANES), F32), pltpu.VMEM((ROW_GROUP * ROW_SUB, LANES), F32)],
        compiler_params=_cparams(("arbitrary",)),
        name="post",
    )(x_ctx, x_lat, a_ctx, a_lat, g_ctx, g_lat, mod_all, w_mg, w_ba, w_bg, w_o, l1g, l1b, w_r, b_r)


ROW_UNROLL = 8


def _row_copy(src_ref, dst_ref, sem):
    return pltpu.make_async_copy(src_ref, dst_ref, sem)


def _scatter_kernel(pos0_ref, pos1_ref, h_ref, xs_ref, sem):
    ts = h_ref.shape[0]

    def issue(g, carry):
        r0 = pl.multiple_of(g * ROW_UNROLL, ROW_UNROLL)
        for k in range(ROW_UNROLL):
            src = _row_slab(h_ref, r0 + k)
            _row_copy(src, _row_slab(xs_ref, pos0_ref[0, r0 + k]), sem).start(priority=0)
            _row_copy(src, _row_slab(xs_ref, pos1_ref[0, r0 + k]), sem).start(priority=1)
        return carry

    lax.fori_loop(0, ts // ROW_UNROLL, issue, 0)
    for _ in range(2):
        _row_copy(h_ref, xs_ref.at[pl.ds(0, ts)], sem).wait()


def _scatter_rows(h2p, pos0, pos1):
    t = h2p.shape[0]
    ts = TS_ROWS
    smem = lambda: pl.BlockSpec((None, 1, ts), lambda i: (i, 0, 0), memory_space=pltpu.SMEM)
    return pl.pallas_call(
        _scatter_kernel,
        grid=(t // ts,),
        in_specs=[smem(), smem(), pl.BlockSpec((ts, ROW_SUB, LANES), lambda i: (i, 0, 0))],
        out_specs=pl.BlockSpec(memory_space=pl.ANY),
        out_shape=jax.ShapeDtypeStruct((2 * t, ROW_SUB, LANES), BF16),
        scratch_shapes=[pltpu.SemaphoreType.DMA(())],
        compiler_params=_cparams(("arbitrary",)),
        name="scatter",
    )(pos0.reshape(t // ts, 1, ts), pos1.reshape(t // ts, 1, ts), h2p)


def _expert_kernel(vt_ref, ve_ref, lo_ref, hi_ref, first_ref, last_ref, newe_ref,
                   xs_ref, wg_ref, wu_ref, wd_ref, y_ref, wgu_s, wd_s, acc_s, slab_s):
    v = pl.program_id(0)
    tm = acc_s.shape[0]

    @pl.when(newe_ref[v] == 1)
    def _():
        wgu_s[:, 0:D_EXP] = wg_ref[...].astype(BF16)
        wgu_s[:, D_EXP:2 * D_EXP] = wu_ref[...].astype(BF16)
        wd_s[...] = wd_ref[...].astype(BF16)

    @pl.when(first_ref[v] == 1)
    def _():
        acc_s[...] = jnp.zeros_like(acc_s)

    lo = lo_ref[v]
    hi = hi_ref[v]

    @pl.when(hi > lo)
    def _():
        gu = _dot(_load_row_slabs(xs_ref, slab_s).astype(BF16), wgu_s[...])
        hid = _silu(gu[:, 0:D_EXP]) * gu[:, D_EXP:2 * D_EXP]
        y = _dot(hid.astype(BF16), wd_s[...])
        rid = lax.broadcasted_iota(jnp.int32, (tm, D), 0)
        acc_s[...] = jnp.where((rid >= lo) & (rid < hi), y, acc_s[...])

    @pl.when(last_ref[v] == 1)
    def _():
        _store_row_slabs(y_ref, acc_s[...], slab_s)


def _experts(xs, meta, w_gate, w_up, w_down):
    p = xs.shape[0]
    nv = meta[0].shape[0]
    tm = TM_EXP
    xmap = lambda v, vt, ve, lo, hi, fi, la, ne: (vt[v], 0, 0)
    wmap = lambda v, vt, ve, lo, hi, fi, la, ne: (ve[v], 0, 0)
    return pl.pallas_call(
        _expert_kernel,
        grid_spec=pltpu.PrefetchScalarGridSpec(
            num_scalar_prefetch=7,
            grid=(nv,),
            in_specs=[pl.BlockSpec((tm, ROW_SUB, LANES), xmap),
                      pl.BlockSpec((None, D, D_EXP), wmap), pl.BlockSpec((None, D, D_EXP), wmap),
                      pl.BlockSpec((None, D_EXP, D), wmap)],
            out_specs=pl.BlockSpec((tm, ROW_SUB, LANES), xmap),
            scratch_shapes=[pltpu.VMEM((D, 2 * D_EXP), BF16), pltpu.VMEM((D_EXP, D), BF16),
                            pltpu.VMEM((tm, D), F32), pltpu.VMEM((tm * ROW_SUB, LANES), F32)]),
        out_shape=jax.ShapeDtypeStruct((p, ROW_SUB, LANES), BF16),
        compiler_params=_cparams(("arbitrary",)),
        name="experts",
    )(*meta, xs, w_gate, w_up, w_down)


def _final_kernel(p0c_ref, p1c_ref, p0n_ref, p1n_ref, x1_ref, rt_ref, mod_ref, l2g_ref, l2b_ref, ys_ref,
                  oc_ref, ol_ref, buf, sem, slab_s, *, n_ctx_tiles):
    i = pl.program_id(0)
    n = pl.num_programs(0)
    tm = x1_ref.shape[0]

    def gather(p0_ref, p1_ref, slot):
        def issue(g, carry):
            r0 = pl.multiple_of(g * ROW_UNROLL, ROW_UNROLL)
            for k in range(ROW_UNROLL):
                _row_copy(_row_slab(ys_ref, p0_ref[0, r0 + k]),
                          _row_slab(buf.at[slot, 0], r0 + k), sem.at[slot]).start(priority=0)
                _row_copy(_row_slab(ys_ref, p1_ref[0, r0 + k]),
                          _row_slab(buf.at[slot, 1], r0 + k), sem.at[slot]).start(priority=1)
            return carry

        lax.fori_loop(0, tm // ROW_UNROLL, issue, 0)

    cur = i % 2

    @pl.when(i == 0)
    def _():
        gather(p0c_ref, p1c_ref, 0)

    @pl.when(i + 1 < n)
    def _():
        gather(p0n_ref, p1n_ref, 1 - cur)

    for k in range(2):
        _row_copy(ys_ref.at[pl.ds(0, tm)], buf.at[cur, k], sem.at[cur]).wait()

    g2 = mod_ref[0, :, 5 * D:6 * D]
    w1 = rt_ref[:, 2:3]
    w2 = rt_ref[:, 3:4]
    moe = (w1 * _load_row_slabs(buf.at[cur, 0], slab_s.at[0])
           + w2 * _load_row_slabs(buf.at[cur, 1], slab_s.at[1]))
    out = _ln(ALPHA * x1_ref[...] + g2 * moe) * l2g_ref[...] + l2b_ref[...]

    @pl.when(i < n_ctx_tiles)
    def _():
        oc_ref[...] = out

    @pl.when(i >= n_ctx_tiles)
    def _():
        ol_ref[...] = out


def _final(x1, rt, pos0, pos1, mod_all, l2g, l2b, ys, t_ctx, seq_lat):
    t = x1.shape[0]
    tm = TM_TOK
    nt = t // tm
    nct = t_ctx // tm
    t_lat = t - t_ctx
    per_seq = seq_lat // tm
    nb_lat = t_lat // seq_lat
    p0 = pos0.reshape(nt, 1, tm)
    p1 = pos1.reshape(nt, 1, tm)
    row = lambda i: (i, 0)
    const = lambda i: (0, 0)
    mmap = lambda i: (jnp.where(i < nct, nb_lat, jnp.maximum(i - nct, 0) // per_seq), 0, 0)
    smem_cur = lambda: pl.BlockSpec((None, 1, tm), lambda i: (i, 0, 0), memory_space=pltpu.SMEM)
    smem_nxt = lambda: pl.BlockSpec((None, 1, tm), lambda i: (jnp.minimum(i + 1, nt - 1), 0, 0),
                                    memory_space=pltpu.SMEM)
    return pl.pallas_call(
        functools.partial(_final_kernel, n_ctx_tiles=nct),
        grid=(nt,),
        in_specs=[smem_cur(), smem_cur(), smem_nxt(), smem_nxt(),
                  pl.BlockSpec((tm, D), row), pl.BlockSpec((tm, LANES), row),
                  pl.BlockSpec((1, 1, 6 * D), mmap),
                  pl.BlockSpec((1, D), const), pl.BlockSpec((1, D), const),
                  pl.BlockSpec(memory_space=pl.ANY)],
        out_specs=[pl.BlockSpec((tm, D), lambda i: (jnp.minimum(i, nct - 1), 0)),
                   pl.BlockSpec((tm, D), lambda i: (jnp.maximum(i - nct, 0), 0))],
        out_shape=[jax.ShapeDtypeStruct((t_ctx, D), F32), jax.ShapeDtypeStruct((t_lat, D), F32)],
        scratch_shapes=[pltpu.VMEM((2, 2, tm, ROW_SUB, LANES), BF16), pltpu.SemaphoreType.DMA((2,)),
                        pltpu.VMEM((2, tm * ROW_SUB, LANES), F32)],
        compiler_params=_cparams(("arbitrary",)),
        name="final",
    )(p0, p1, p0, p1, x1, rt, mod_all, l2g, l2b, ys)


def _reorder_q_heads(w, axis):
    shape = w.shape
    split = shape[:axis] + (N_KV_HEADS, N_Q_HEADS // N_KV_HEADS, HD) + shape[axis + 1:]
    return jnp.swapaxes(w.reshape(split), axis, axis + 1).reshape(shape)


def _rope_tables(seq):
    t = jnp.arange(seq, dtype=jnp.int32)
    row = (t // GRID_W).astype(F32)
    col = (t % GRID_W).astype(F32)
    half = HD // 4
    inv = ROPE_THETA ** (-jnp.arange(half, dtype=F32) / half)
    lane = np.arange(LANES)
    d64 = lane % HD
    use_row = jnp.asarray(d64 < HD // 2)
    freq = inv[jnp.asarray(d64 % half)]
    pos = jnp.where(use_row[None, :], row[:, None], col[:, None])
    ang = pos * freq[None, :]
    sign = jnp.asarray(np.where((d64 % 32) < 16, -1.0, 1.0), F32)
    return jnp.cos(ang), jnp.sin(ang) * sign[None, :]


def _pair_states(s):
    b = s.shape[0]
    s = s.reshape(b, 2, 2, GLA_DK, GLA_DV)
    return s.transpose(0, 1, 4, 2, 3).reshape(b, 2, GLA_DV, 2 * GLA_DK)


def _unpair_states(s):
    b = s.shape[0]
    s = s.reshape(b, 2, GLA_DV, 2, GLA_DK)
    return s.transpose(0, 1, 3, 4, 2).reshape(b, GLA_H, GLA_DK, GLA_DV)


def _route_tables(ert, counts, t):
    i32 = jnp.int32
    cnt = counts[0, :N_EXP].astype(i32)
    ends = jnp.cumsum(cnt)
    starts = ends - cnt
    table = lambda e: jnp.sum(jnp.where(e[None, :] == jnp.arange(N_EXP, dtype=i32)[:, None],
                                        starts[:, None], 0), axis=0)
    pos0 = table(ert[0].astype(i32)) + ert[4].astype(i32)
    pos1 = table(ert[1].astype(i32)) + ert[5].astype(i32)
    p = 2 * t
    nt = p // TM_EXP
    nv = nt + N_EXP
    tile_starts = jnp.arange(nt, dtype=i32) * TM_EXP
    idx_t = jnp.arange(nt, dtype=i32) + jnp.sum(starts[None, :] < tile_starts[:, None], axis=1, dtype=i32)
    idx_e = jnp.arange(N_EXP, dtype=i32) + jnp.minimum(starts // TM_EXP + 1, nt)
    k = jnp.arange(nv, dtype=i32)[:, None]
    bounds = (jnp.sum(jnp.where(idx_t[None, :] == k, tile_starts[None, :], 0), axis=1)
              + jnp.sum(jnp.where(idx_e[None, :] == k, starts[None, :], 0), axis=1))
    nxt = jnp.concatenate([bounds[1:], jnp.array([p], i32)])
    tile = jnp.minimum(bounds // TM_EXP, nt - 1)
    exp_id = jnp.minimum(jnp.sum(ends[None, :] <= bounds[:, None], axis=1, dtype=i32), N_EXP - 1)
    lo = bounds - tile * TM_EXP
    hi = jnp.where(nxt > bounds, jnp.minimum(nxt - tile * TM_EXP, TM_EXP), lo)
    prev_tile = jnp.concatenate([jnp.array([-1], i32), tile[:-1]])
    prev_exp = jnp.concatenate([jnp.array([-1], i32), exp_id[:-1]])
    next_tile = jnp.concatenate([tile[1:], jnp.array([-1], i32)])
    first = (tile != prev_tile).astype(i32)
    last = (tile != next_tile).astype(i32)
    newe = (exp_id != prev_exp).astype(i32)
    return pos0, pos1, (tile, exp_id, lo, hi, first, last, newe)


def kernel(x_prompt, x_sample, cache_k, cache_v, state_gla_fwd, state_gla_bwd, c, c_ctx, w_ada, b_ada, w_in, q_norm, k_norm, gla_w_gate, gla_b_gate, gla_norm, w_br_attn, w_br_gla, w_out, ln1_g, ln1_b, router_group_w, router_group_b, router_expert_w, router_expert_b, exp_w_gate, exp_w_up, exp_w_down, ln2_g, ln2_b):
    b_ctx, seq_ctx, _ = x_prompt.shape
    b_lat, seq_lat, _ = x_sample.shape
    t_ctx, t_lat = b_ctx * seq_ctx, b_lat * seq_lat
    t = t_ctx + t_lat
    l = 0

    rows = -(-(b_lat + 1) // 8) * 8
    c_rows = jnp.zeros((rows, D), F32).at[:b_lat].set(c).at[b_lat].set(c_ctx)
    mod = _ada(c_rows, w_ada[l], b_ada[l][None, :])
    mod_all = mod[:b_lat + 1, None, :]
    mod_lat = mod_all[:b_lat]
    mod_ctx = mod_all[b_lat:]

    w_full = w_in[l]
    w_a = jnp.concatenate([_reorder_q_heads(w_full[:, :AW], 1), w_full[:, AW:A_WIDTH]], axis=1).astype(BF16)
    w_mg = w_full[:, A_WIDTH:].astype(BF16)
    gain = jnp.concatenate([jnp.tile(q_norm[l], N_Q_HEADS), jnp.tile(k_norm[l], N_KV_HEADS)])[None, :]
    head_of = np.arange(AW + KVW) // HD
    ind = jnp.asarray((head_of[:, None] == np.arange(LANES)[None, :]) / HD, BF16)
    w_ba = _reorder_q_heads(w_br_attn[l], 0).astype(BF16)
    w_bg = w_br_gla[l].astype(BF16)
    w_o = w_out[l].astype(BF16)
    w_r = jnp.zeros((D, LANES), F32).at[:, :N_GROUPS].set(router_group_w[l])
    w_r = w_r.at[:, N_GROUPS:N_GROUPS + N_EXP].set(router_expert_w[l]).astype(BF16)
    b_r = jnp.zeros((1, LANES), F32).at[0, :N_GROUPS].set(router_group_b[l])
    b_r = b_r.at[0, N_GROUPS:N_GROUPS + N_EXP].set(router_expert_b[l])
    wg = jnp.zeros((2, 2 * GATE_RANK, GKW), F32)
    wg = wg.at[0, :GATE_RANK].set(gla_w_gate[l, 0]).at[1, GATE_RANK:].set(gla_w_gate[l, 1])
    bg = gla_b_gate[l][:, None, :]
    gn = gla_norm[l][None, :]

    xc = x_prompt.reshape(t_ctx, D)
    xl = x_sample.reshape(t_lat, D)

    q_c, k_c, v_c, qg_c, kg_c, vg_c, rs_c, lr_c, kf_c, vf_c = _inproj(
        xc, mod_ctx, w_a, gain, ind, None, seq_ctx, latent=False)
    attn_c = _attention(q_c, k_c, v_c, None, seq_ctx)
    zero_state = jnp.zeros((1, 2, 2, LANES, LANES), F32)
    gla_c, sfin_c = _gla(qg_c, kg_c, vg_c, lr_c, rs_c, zero_state, wg, bg, gn, seq_ctx)

    q_l, k_l, v_l, qg_l, kg_l, vg_l, rs_l, lr_l = _inproj(
        xl, mod_lat, w_a, gain, ind, _rope_tables(seq_lat), seq_lat, latent=True)
    past = cache_k.shape[2]
    kc = cache_k[:, l].reshape(b_lat, past, KVW).astype(BF16)
    vc = cache_v[:, l].reshape(b_lat, past, KVW).astype(BF16)
    attn_l = _attention(q_l, k_l, v_l, (kc, vc), seq_lat)
    s0 = jnp.stack([_pair_states(state_gla_fwd[:, l]), _pair_states(state_gla_bwd[:, l])], axis=1)
    gla_l, _ = _gla(qg_l, kg_l, vg_l, lr_l, rs_l, s0, wg, bg, gn, seq_lat)

    x1, h2p, rt, ert, counts = _post(xc, xl, attn_c, attn_l, gla_c, gla_l, mod_all, seq_lat,
                                w_mg, w_ba, w_bg, w_o, ln1_g[l][None, :], ln1_b[l][None, :], w_r, b_r)

    pos0, pos1, meta = _route_tables(ert, counts, t)
    xs = _scatter_rows(h2p, pos0, pos1)
    ys = _experts(xs, meta, exp_w_gate[l], exp_w_up[l], exp_w_down[l])
    y_ctx, y_lat = _final(x1, rt, pos0, pos1, mod_all, ln2_g[l][None, :], ln2_b[l][None, :], ys,
                          t_ctx, seq_lat)

    new_k, new_v = kf_c, vf_c
    new_sf = _unpair_states(sfin_c[:, 0])[:, None]
    new_sb = _unpair_states(sfin_c[:, 1])[:, None]
    return (y_ctx.reshape(b_ctx, seq_ctx, D), y_lat.reshape(b_lat, seq_lat, D),
            new_k, new_v, new_sf, new_sb)
```

```python
import functools

import numpy as np
import jax
import jax.numpy as jnp
from jax import lax
from jax.experimental import pallas as pl
from jax.experimental.pallas import tpu as pltpu

F32 = jnp.float32
BF16 = jnp.bfloat16
HIGHEST = lax.Precision.HIGHEST

D = 1024
GRID_W = 64
HD = 64
N_Q_HEADS = 8
N_KV_HEADS = 2
AW = N_Q_HEADS * HD
KVW = N_KV_HEADS * HD
ROPE_THETA = 10000.0
GLA_H = 4
GLA_DK = 64
GLA_DV = 128
GKW = GLA_H * GLA_DK
GVW = GLA_H * GLA_DV
GATE_RANK = 16
GLA_TAU = 16.0
CHUNK = 64
N_GROUPS = 4
EPG = 8
N_EXP = N_GROUPS * EPG
D_EXP = 256
DEPTH = 1
ALPHA = (2.0 * DEPTH) ** 0.25
LN_EPS = 1e-6
RMS_EPS = 1e-6

LANES = 128
A_WIDTH = AW + 2 * KVW + 2 * GKW + 2 * GVW + 2 * GATE_RANK
MG_WIDTH = 2 * D
TM_TOK = 512
ROW_GROUP = 128
LOG2E = 1.4426950408889634
ONES_ROWS = 16
TQ_LAT = 256
TM_EXP = 256
TS_ROWS = 256
VMEM_LIMIT = 56 * 1024 * 1024


def _cparams(sem):
    return pltpu.CompilerParams(dimension_semantics=sem, vmem_limit_bytes=VMEM_LIMIT)


def _dot(a, b):
    return jnp.dot(a, b, preferred_element_type=F32)


def _dot_nt(a, b):
    return lax.dot_general(a, b, (((1,), (1,)), ((), ())), preferred_element_type=F32)


def _dot_tn(a, b):
    return lax.dot_general(a, b, (((0,), (0,)), ((), ())), preferred_element_type=F32)


def _ln(x):
    mu = jnp.mean(x, axis=-1, keepdims=True)
    xc = x - mu
    var = jnp.mean(xc * xc, axis=-1, keepdims=True)
    return xc * lax.rsqrt(var + LN_EPS)


def _silu(x):
    return x * jax.nn.sigmoid(x)


def _split_bf16(x):
    hi = x.astype(BF16)
    lo = (x - hi.astype(F32)).astype(BF16)
    return hi, lo


def _ada_kernel(c_ref, w_ref, b_ref, o_ref):
    s = _silu(c_ref[...])
    o_ref[...] = jnp.dot(s, w_ref[...], preferred_element_type=F32, precision=HIGHEST) + b_ref[...]


def _ada(c_rows, w_ada, b_ada):
    rows = c_rows.shape[0]
    n = w_ada.shape[1]
    bn = 1024
    return pl.pallas_call(
        _ada_kernel,
        grid=(n // bn,),
        in_specs=[pl.BlockSpec((rows, D), lambda j: (0, 0)),
                  pl.BlockSpec((D, bn), lambda j: (0, j)),
                  pl.BlockSpec((1, bn), lambda j: (0, j))],
        out_specs=pl.BlockSpec((rows, bn), lambda j: (0, j)),
        out_shape=jax.ShapeDtypeStruct((rows, n), F32),
        compiler_params=_cparams(("arbitrary",)),
        name="ada",
    )(c_rows, w_ada, b_ada)


def _inproj_kernel(*refs, latent, seq):
    if latent:
        (x_ref, mod_ref, w_ref, gain_ref, ind_ref, cos_ref, sin_ref,
         q_ref, k_ref, v_ref, qg_ref, kg_ref, vg_ref, rs_ref, lr_ref) = refs
    else:
        (x_ref, mod_ref, w_ref, gain_ref, ind_ref,
         q_ref, k_ref, v_ref, qg_ref, kg_ref, vg_ref, rs_ref, lr_ref, kf_ref, vf_ref) = refs
    tm = x_ref.shape[0]
    sub = ROW_GROUP
    n_groups = tm // sub
    sh1 = mod_ref[0, :, 0:D]
    sc1 = mod_ref[0, :, D:2 * D]
    lane = lax.broadcasted_iota(jnp.int32, (sub, LANES), 1)
    low_half = lane < HD
    first = (lane % 32) < 16

    def project(g):
        rows = slice(g * sub, (g + 1) * sub)
        h = (_ln(x_ref[rows, :]) * (1.0 + sc1) + sh1).astype(BF16)
        return _dot(h, w_ref[...])

    def finish(g, res):
        rows = slice(g * sub, (g + 1) * sub)
        qk = res[:, 0:AW + KVW]
        hi, lo = _split_bf16(qk * qk)
        ms = _dot(hi, ind_ref[...]) + _dot(lo, ind_ref[...])
        r = lax.rsqrt(ms + RMS_EPS)
        for s in range(5):
            rb = jnp.where(low_half, r[:, 2 * s:2 * s + 1], r[:, 2 * s + 1:2 * s + 2])
            y = res[:, LANES * s:LANES * (s + 1)] * rb * gain_ref[:, LANES * s:LANES * (s + 1)]
            if s == 4 and not latent:
                kf_ref[(g * sub) // seq, :, (g * sub) % seq:(g * sub) % seq + sub] = y.T
            if latent:
                partner = jnp.where(first, pltpu.roll(y, LANES - 16, 1), pltpu.roll(y, 16, 1))
                y = y * cos_ref[rows, :] + partner * sin_ref[rows, :]
            if s < 4:
                q_ref[rows, LANES * s:LANES * (s + 1)] = (y * (HD ** -0.5 * LOG2E)).astype(BF16)
            else:
                k_ref[rows, :] = y.astype(BF16)
        o = AW + KVW
        v = res[:, o:o + KVW]
        v_ref[rows, :] = v.astype(BF16)
        if not latent:
            vf_ref[(g * sub) // seq, :, (g * sub) % seq:(g * sub) % seq + sub] = v.T
        o += KVW
        qg_ref[rows, :] = res[:, o:o + GKW] * (GLA_DK ** -0.5)
        o += GKW
        kg_ref[rows, :] = res[:, o:o + GKW]
        o += GKW
        vg_ref[rows, :] = res[:, o:o + GVW].astype(BF16)
        o += GVW
        rs_ref[rows, :] = _silu(res[:, o:o + GVW]).astype(BF16)
        o += GVW
        lr_ref[rows, :] = res[:, o:o + 2 * GATE_RANK]

    pending = {0: project(0)}
    for g in range(n_groups):
        if g + 1 < n_groups:
            pending[g + 1] = project(g + 1)
        finish(g, pending.pop(g))


def _inproj(x2, mod, w_a, gain, ind, rope, seq, latent):
    t = x2.shape[0]
    tm = TM_TOK
    per_seq = max(seq // tm, 1)
    per_tile = max(tm // seq, 1)
    row = lambda i: (i, 0)
    const = lambda i: (0, 0)
    in_specs = [pl.BlockSpec((tm, D), row),
                pl.BlockSpec((1, 1, 6 * D), (lambda i: (i // per_seq, 0, 0)) if latent else (lambda i: (0, 0, 0))),
                pl.BlockSpec((D, A_WIDTH), const),
                pl.BlockSpec((1, AW + KVW), const),
                pl.BlockSpec((AW + KVW, LANES), const)]
    args = [x2, mod, w_a, gain, ind]
    if latent:
        in_specs += [pl.BlockSpec((tm, LANES), lambda i: (i % per_seq, 0))] * 2
        args += list(rope)
    widths = [(AW, BF16), (KVW, BF16), (KVW, BF16), (GKW, F32), (GKW, F32), (GVW, BF16), (GVW, BF16),
              (2 * GATE_RANK, F32)]
    out_specs = [pl.BlockSpec((tm, w), row) for w, _ in widths]
    out_shape = [jax.ShapeDtypeStruct((t, w), dt) for w, dt in widths]
    if not latent:
        cache_spec = pl.BlockSpec((per_tile, KVW, seq), lambda i: (i, 0, 0))
        out_specs += [cache_spec] * 2
        out_shape += [jax.ShapeDtypeStruct((t // seq, KVW, seq), F32)] * 2
    return pl.pallas_call(
        functools.partial(_inproj_kernel, latent=latent, seq=seq),
        grid=(t // tm,),
        in_specs=in_specs,
        out_specs=out_specs,
        out_shape=out_shape,
        compiler_params=_cparams(("parallel",)),
        name="inproj_lat" if latent else "inproj_ctx",
    )(*args)


def _attn_kernel(*refs, has_cache):
    def transposed_with_ones(dst, src):
        dst[0:KVW, :] = src[...].astype(F32).T.astype(BF16)
        dst[KVW:, :] = jnp.ones((ONES_ROWS, dst.shape[1]), BF16)

    if has_cache:
        q_ref, k_ref, v_ref, kc_ref, vc_ref, o_ref, vt_s, vct_s = refs

        @pl.when(pl.program_id(1) == 0)
        def _():
            transposed_with_ones(vt_s, v_ref)
            transposed_with_ones(vct_s, vc_ref)
    else:
        q_ref, k_ref, v_ref, o_ref, vt_s = refs
        transposed_with_ones(vt_s, v_ref)
    tq = q_ref.shape[0]

    lane = lax.broadcasted_iota(jnp.int32, (tq, LANES), 1)
    low_half = lane < HD
    k = k_ref[...]
    scores = []
    for j in range(N_KV_HEADS):
        keep = low_half if j == 0 else jnp.logical_not(low_half)
        zero = jnp.zeros((tq, LANES), BF16)
        for pair in range(2):
            qs = jnp.concatenate([jnp.where(keep, q_ref[:, LANES * s:LANES * (s + 1)], zero)
                                  for s in (2 * pair, 2 * pair + 1)], axis=0)
            s1 = _dot_nt(k, qs)
            s2 = _dot_nt(kc_ref[...], qs) if has_cache else None
            scores.append((s1, s2))
    outs = []
    for s1, s2 in scores:
        m = jnp.max(s1, axis=0, keepdims=True)
        if has_cache:
            m = jnp.maximum(m, jnp.max(s2, axis=0, keepdims=True))
        acc = _dot(vt_s[...], jnp.exp2(s1 - m).astype(BF16))
        if has_cache:
            acc = acc + _dot(vct_s[...], jnp.exp2(s2 - m).astype(BF16))
        outs.append(acc[0:KVW] / acc[KVW:KVW + 1])
    head0 = jnp.concatenate(outs[0:2], axis=1)
    head1 = jnp.concatenate(outs[2:4], axis=1)
    row = lax.broadcasted_iota(jnp.int32, (LANES, 4 * tq), 0)
    out = jnp.where(row < HD, head0, head1).T
    for s in range(4):
        o_ref[:, LANES * s:LANES * (s + 1)] = out[s * tq:(s + 1) * tq].astype(BF16)


def _attention(q, k, v, cache, seq):
    t = q.shape[0]
    if cache is None:
        tq = seq
        grid = (t // seq,)
        qmap = lambda b: (b, 0)
        in_specs = [pl.BlockSpec((tq, AW), qmap), pl.BlockSpec((seq, KVW), qmap),
                    pl.BlockSpec((seq, KVW), qmap)]
        args = [q, k, v]
        scratch = [pltpu.VMEM((KVW + ONES_ROWS, seq), BF16)]
        sem = ("parallel",)
        name = "attn_ctx"
    else:
        tq = TQ_LAT
        nq = seq // tq
        kc, vc = cache
        past = kc.shape[1]
        grid = (t // seq, nq)
        qmap = lambda b, i: (b * nq + i, 0)
        kmap = lambda b, i: (b, 0)
        cmap = lambda b, i: (b, 0, 0)
        in_specs = [pl.BlockSpec((tq, AW), qmap), pl.BlockSpec((seq, KVW), kmap),
                    pl.BlockSpec((seq, KVW), kmap),
                    pl.BlockSpec((None, past, KVW), cmap), pl.BlockSpec((None, past, KVW), cmap)]
        args = [q, k, v, kc, vc]
        scratch = [pltpu.VMEM((KVW + ONES_ROWS, seq), BF16), pltpu.VMEM((KVW + ONES_ROWS, past), BF16)]
        sem = ("parallel", "arbitrary")
        name = "attn_lat"
    return pl.pallas_call(
        functools.partial(_attn_kernel, has_cache=cache is not None),
        grid=grid,
        in_specs=in_specs,
        out_specs=pl.BlockSpec((tq, AW), qmap),
        out_shape=jax.ShapeDtypeStruct((t, AW), BF16),
        scratch_shapes=scratch,
        compiler_params=_cparams(sem),
        name=name,
    )(*args)


GLA_BLK = 256
GLA_UNROLL = 2


def _split3_bf16(x):
    hi = x.astype(BF16)
    r1 = x - hi.astype(F32)
    mid = r1.astype(BF16)
    lo = (r1 - mid.astype(F32)).astype(BF16)
    return hi, mid, lo


def _gla_kernel(qg_ref, kg_ref, vg_ref, lr_ref, rs_ref, s0_ref, wg_ref, bg_ref, gn_ref,
                o_ref, sfin_ref, cum_s, kv_s, dec_s):
    n = qg_ref.shape[0]
    nc = n // CHUNK
    lane = lax.broadcasted_iota(jnp.int32, (CHUNK, LANES), 1)
    low_half = lane < GLA_DK
    lane_sq = lax.broadcasted_iota(jnp.int32, (LANES, LANES), 1)
    low_half_sq = lane_sq < GLA_DK
    ri = lax.broadcasted_iota(jnp.int32, (2 * CHUNK, 2 * CHUNK), 0)
    ci = lax.broadcasted_iota(jnp.int32, (2 * CHUNK, 2 * CHUNK), 1)
    diag = (ri >> 6) == (ci >> 6)
    keep2 = (diag & (ri >= ci), diag & (ci >= ri))

    rb = lax.broadcasted_iota(jnp.int32, (GLA_BLK, GLA_BLK), 0)
    cb = lax.broadcasted_iota(jnp.int32, (GLA_BLK, GLA_BLK), 1)
    same = (rb >> 6) == (cb >> 6)
    tri = ((same & (rb >= cb)).astype(BF16), (same & (cb >= rb)).astype(BF16))
    for d in range(2):
        w_hi, w_lo = _split_bf16(wg_ref[d])
        for blk in range(n // GLA_BLK):
            rows = slice(blk * GLA_BLK, (blk + 1) * GLA_BLK)
            l_hi, l_lo = _split_bf16(lr_ref[rows, :])
            z = _dot(l_hi, w_hi) + _dot(l_lo, w_hi) + _dot(l_hi, w_lo) + bg_ref[d]
            logg = (jnp.minimum(z, 0.0) - jnp.log(1.0 + jnp.exp(-jnp.abs(z)))) * (1.0 / GLA_TAU)
            pieces = _split3_bf16(logg)
            cum_s[d, rows, :] = _dot(tri[d], pieces[0]) + _dot(tri[d], pieces[1]) + _dot(tri[d], pieces[2])

    def pass1(i, carry):
        cs = [i * GLA_UNROLL + u for u in range(GLA_UNROLL)]
        rws = [pl.ds(pl.multiple_of(c * CHUNK, CHUNK), CHUNK) for c in cs]
        prods = []
        for c, rows in zip(cs, rws):
            kc = kg_ref[rows, :]
            kdec = []
            for d in range(2):
                cum = cum_s[d, rows, :]
                last = cum[CHUNK - 1:CHUNK, :] if d == 0 else cum[0:1, :]
                dec_s[d, c] = jnp.exp(last)
                kdec.append((kc * jnp.exp(last - cum)).astype(BF16))
            for p in range(2):
                vpair = vg_ref[rows, GLA_DV * 2 * p:GLA_DV * 2 * (p + 1)]
                kpair = jnp.concatenate([kdec[0][:, LANES * p:LANES * (p + 1)],
                                         kdec[1][:, LANES * p:LANES * (p + 1)]], axis=1)
                prods.append(_dot_tn(vpair, kpair))
        for j, c in enumerate(cs):
            for p in range(2):
                res = prods[2 * j + p]
                for d in range(2):
                    cols = slice(LANES * d, LANES * (d + 1))
                    kv_s[d, c, p] = jnp.where(low_half_sq, res[0:GLA_DV, cols], res[GLA_DV:2 * GLA_DV, cols])
        return carry

    lax.fori_loop(0, nc // GLA_UNROLL, pass1, 0)

    for d in range(2):
        def scan(i, st):
            c = i if d == 0 else nc - 1 - i
            dec = dec_s[d, c]
            new = []
            for p in range(2):
                kv = kv_s[d, c, p]
                kv_s[d, c, p] = st[p]
                new.append(st[p] * dec[:, LANES * p:LANES * (p + 1)] + kv)
            return tuple(new)

        fin = lax.fori_loop(0, nc, scan, (s0_ref[d, 0], s0_ref[d, 1]))
        sfin_ref[d, 0] = fin[0]
        sfin_ref[d, 1] = fin[1]

    def pass3(i, carry):
        cs = [i * GLA_UNROLL + u for u in range(GLA_UNROLL)]
        rws = [pl.ds(pl.multiple_of(c * CHUNK, CHUNK), CHUNK) for c in cs]
        first = []
        for c, rows in zip(cs, rws):
            q = qg_ref[rows, :]
            k = kg_ref[rows, :]
            for d in range(2):
                cum = cum_s[d, rows, :]
                qt = q * jnp.exp(cum)
                kt = (k * jnp.exp(-cum)).astype(BF16)
                for p in range(2):
                    qs = qt[:, LANES * p:LANES * (p + 1)]
                    lhs = jnp.concatenate([jnp.where(low_half, qs, 0.0), jnp.where(low_half, 0.0, qs)],
                                          axis=0).astype(BF16)
                    kts = kt[:, LANES * p:LANES * (p + 1)]
                    rhs = jnp.concatenate([kts, kts, kv_s[d, c, p].astype(BF16)], axis=0)
                    first.append(_dot_nt(lhs, rhs))
        second = []
        for j, rows in enumerate(rws):
            for d in range(2):
                for p in range(2):
                    res = first[4 * j + 2 * d + p]
                    vp = jnp.concatenate([vg_ref[rows, GLA_DV * (2 * p):GLA_DV * (2 * p + 1)],
                                          vg_ref[rows, GLA_DV * (2 * p + 1):GLA_DV * (2 * p + 2)]], axis=0)
                    a = jnp.where(keep2[d], res[:, 0:2 * CHUNK], 0.0).astype(BF16)
                    second.append(_dot(a, vp) + res[:, 2 * CHUNK:])
        for j, rows in enumerate(rws):
            for p in range(2):
                tot = second[4 * j + p] + second[4 * j + 2 + p]
                y = tot * lax.rsqrt(jnp.mean(tot * tot, axis=-1, keepdims=True) + RMS_EPS) * gn_ref[...]
                for hh in range(2):
                    cols = slice(GLA_DV * (2 * p + hh), GLA_DV * (2 * p + hh + 1))
                    o_ref[rows, cols] = (y[CHUNK * hh:CHUNK * (hh + 1)]
                                         * rs_ref[rows, cols].astype(F32)).astype(BF16)
        return carry

    lax.fori_loop(0, nc // GLA_UNROLL, pass3, 0)


def _gla(qg, kg, vg, lr, rs, s0, wg, bg, gn, seq):
    t = qg.shape[0]
    nb = t // seq
    nc = seq // CHUNK
    row = lambda b: (b, 0)
    c3 = lambda b: (0, 0, 0)
    st = lambda b: (b, 0, 0, 0, 0)
    if s0.shape[0] == 1:
        s0map = lambda b: (0, 0, 0, 0, 0)
    else:
        s0map = st
    return pl.pallas_call(
        _gla_kernel,
        grid=(nb,),
        in_specs=[pl.BlockSpec((seq, GKW), row), pl.BlockSpec((seq, GKW), row),
                  pl.BlockSpec((seq, GVW), row), pl.BlockSpec((seq, 2 * GATE_RANK), row),
                  pl.BlockSpec((seq, GVW), row),
                  pl.BlockSpec((None, 2, 2, LANES, LANES), s0map),
                  pl.BlockSpec((2, 2 * GATE_RANK, GKW), c3), pl.BlockSpec((2, 1, GKW), c3),
                  pl.BlockSpec((1, GLA_DV), lambda b: (0, 0))],
        out_specs=[pl.BlockSpec((seq, GVW), row),
                   pl.BlockSpec((None, 2, 2, LANES, LANES), st)],
        out_shape=[jax.ShapeDtypeStruct((t, GVW), BF16),
                   jax.ShapeDtypeStruct((nb, 2, 2, LANES, LANES), F32)],
        scratch_shapes=[pltpu.VMEM((2, seq, GKW), F32),
                        pltpu.VMEM((2, nc, 2, LANES, LANES), F32),
                        pltpu.VMEM((2, nc, 1, GKW), F32)],
        compiler_params=_cparams(("parallel",)),
        name="gla_lat" if seq > 256 else "gla_ctx",
    )(qg, kg, vg, lr, rs, s0, wg, bg, gn)


ROW_SUB = D // LANES


def _store_row_slabs(ref, x, tmp):
    m = x.shape[0]
    for c in range(ROW_SUB):
        tmp[pl.ds(c, m, stride=ROW_SUB), :] = x[:, LANES * c:LANES * (c + 1)]
    ref[...] = tmp[...].reshape(m, ROW_SUB, LANES).astype(BF16)


def _load_row_slabs(ref, tmp):
    m = ref.shape[0]
    tmp[...] = ref[...].astype(F32).reshape(m * ROW_SUB, LANES)
    return jnp.concatenate([tmp[pl.ds(c, m, stride=ROW_SUB), :] for c in range(ROW_SUB)], axis=1)


def _row_slab(ref, row):
    return ref.at[pl.ds(row, 1)]


def _post_kernel(xc_ref, xl_ref, ac_ref, al_ref, gc_ref, gl_ref, mod_ref,
                 wmg_ref, wba_ref, wbg_ref, wo_ref, l1g_ref, l1b_ref, wr_ref, br_ref,
                 x1_ref, h2_ref, rt_ref, ert_ref, cnt_ref, run_s, slab_s, *, n_ctx_tiles):
    i = pl.program_id(0)
    tm = xc_ref.shape[0]
    sub = ROW_GROUP
    n_groups = tm // sub
    is_ctx = i < n_ctx_tiles
    sh1 = mod_ref[0, :, 0:D]
    sc1 = mod_ref[0, :, D:2 * D]
    g1 = mod_ref[0, :, 2 * D:3 * D]
    sh2 = mod_ref[0, :, 3 * D:4 * D]
    sc2 = mod_ref[0, :, 4 * D:5 * D]

    parts = [slice(g * sub, (g + 1) * sub) for g in range(n_groups)]
    xs = [jnp.where(is_ctx, xc_ref[r, :], xl_ref[r, :]) for r in parts]
    hs = [(_ln(x) * (1.0 + sc1) + sh1).astype(BF16) for x in xs]
    gates = [jax.nn.sigmoid(_dot(h, wmg_ref[...])) for h in hs]
    ba = [_dot(jnp.where(is_ctx, ac_ref[r, :], al_ref[r, :]), wba_ref[...]) for r in parts]
    bg = [_dot(jnp.where(is_ctx, gc_ref[r, :], gl_ref[r, :]), wbg_ref[...]) for r in parts]
    merged = [(g[:, :D] * a + g[:, D:] * b).astype(BF16) for g, a, b in zip(gates, ba, bg)]
    mix = [_dot(m, wo_ref[...]) for m in merged]
    x1s = [_ln(ALPHA * x + g1 * m) * l1g_ref[...] + l1b_ref[...] for x, m in zip(xs, mix)]
    h2s = [_ln(x1) * (1.0 + sc2) + sh2 for x1 in x1s]
    logits = [_dot(h2.astype(BF16), wr_ref[...]) + br_ref[...] for h2 in h2s]
    for g, r in enumerate(parts):
        x1_ref[r, :] = x1s[g]
        _store_row_slabs(h2_ref.at[pl.ds(g * sub, sub)], h2s[g], slab_s)

    @pl.when(i == 0)
    def _():
        run_s[...] = jnp.zeros_like(run_s)

    ri = lax.broadcasted_iota(jnp.int32, (sub, sub), 0)
    ci = lax.broadcasted_iota(jnp.int32, (sub, sub), 1)
    earlier = (ri > ci).astype(BF16)
    run = run_s[0:1, :]
    for g, r in enumerate(parts):
        run = _route(logits[g], earlier, run, rt_ref.at[r, :], ert_ref.at[:, r])
    run_s[0:1, :] = run
    cnt_ref[...] = jnp.broadcast_to(run, cnt_ref.shape)


def _route(logit, earlier, run, rt_ref, ert_ref):
    tm = logit.shape[0]
    lane_i = lax.broadcasted_iota(jnp.int32, (tm, LANES), 1)
    lane = lane_i.astype(F32)
    lane_grp = ((lane_i - N_GROUPS) >> 3).astype(F32)
    neg = jnp.float32(-jnp.inf)
    far = jnp.float32(LANES)
    is_g = lane_i < N_GROUPS
    lg = jnp.where(is_g, logit, neg)
    mg = jnp.max(lg, axis=-1, keepdims=True)
    pg_top = 1.0 / jnp.sum(jnp.where(is_g, jnp.exp(logit - mg), 0.0), axis=-1, keepdims=True)
    g_idx = jnp.min(jnp.where(lg == mg, lane, far), axis=-1, keepdims=True)
    in_grp = (lane_i >= N_GROUPS) & (lane_i < N_GROUPS + N_EXP) & (lane_grp == g_idx)
    le = jnp.where(in_grp, logit, neg)
    v1 = jnp.max(le, axis=-1, keepdims=True)
    i1 = jnp.min(jnp.where(le == v1, lane, far), axis=-1, keepdims=True)
    le2 = jnp.where(lane == i1, neg, le)
    v2 = jnp.max(le2, axis=-1, keepdims=True)
    i2 = jnp.min(jnp.where(le2 == v2, lane, far), axis=-1, keepdims=True)
    e1 = i1 - N_GROUPS
    e2 = i2 - N_GROUPS
    tt = jnp.exp(v2 - v1)
    w1 = pg_top / (1.0 + tt)
    w2 = pg_top * tt / (1.0 + tt)

    hot = ((lane == e1) | (lane == e2)).astype(F32)
    before = _dot(earlier, hot.astype(BF16)) + run
    r1 = jnp.sum(jnp.where(lane == e1, before, 0.0), axis=-1, keepdims=True)
    r2 = jnp.sum(jnp.where(lane == e2, before, 0.0), axis=-1, keepdims=True)

    rt = jnp.where(lane_i == 0, e1, 0.0)
    rt = jnp.where(lane_i == 1, e2, rt)
    rt = jnp.where(lane_i == 2, w1, rt)
    rt = jnp.where(lane_i == 3, w2, rt)
    rt = jnp.where(lane_i == 4, r1, rt)
    rt = jnp.where(lane_i == 5, r2, rt)
    rt_ref[...] = rt
    ert_ref[...] = rt.T[0:8, :]
    return run + jnp.sum(hot, axis=0, keepdims=True)


def _post(x_ctx, x_lat, a_ctx, a_lat, g_ctx, g_lat, mod_all, seq_lat,
          w_mg, w_ba, w_bg, w_o, l1g, l1b, w_r, b_r):
    t_ctx, t_lat = x_ctx.shape[0], x_lat.shape[0]
    tm = TM_TOK
    nct, nlt = t_ctx // tm, t_lat // tm
    per_seq = seq_lat // tm
    nb_lat = t_lat // seq_lat
    t = t_ctx + t_lat
    cmap = lambda i: (jnp.minimum(i, nct - 1), 0)
    lmap = lambda i: (jnp.maximum(i - nct, 0), 0)
    mmap = lambda i: (jnp.where(i < nct, nb_lat, jnp.maximum(i - nct, 0) // per_seq), 0, 0)
    row = lambda i: (i, 0)
    const = lambda i: (0, 0)
    return pl.pallas_call(
        functools.partial(_post_kernel, n_ctx_tiles=nct),
        grid=(nct + nlt,),
        in_specs=[pl.BlockSpec((tm, D), cmap), pl.BlockSpec((tm, D), lmap),
                  pl.BlockSpec((tm, AW), cmap), pl.BlockSpec((tm, AW), lmap),
                  pl.BlockSpec((tm, GVW), cmap), pl.BlockSpec((tm, GVW), lmap),
                  pl.BlockSpec((1, 1, 6 * D), mmap),
                  pl.BlockSpec((D, MG_WIDTH), const), pl.BlockSpec((AW, D), const),
                  pl.BlockSpec((GVW, D), const), pl.BlockSpec((D, D), const),
                  pl.BlockSpec((1, D), const), pl.BlockSpec((1, D), const),
                  pl.BlockSpec((D, LANES), const), pl.BlockSpec((1, LANES), const)],
        out_specs=[pl.BlockSpec((tm, D), row), pl.BlockSpec((tm, ROW_SUB, LANES), lambda i: (i, 0, 0)),
                   pl.BlockSpec((tm, LANES), row), pl.BlockSpec((8, tm), lambda i: (0, i)),
                   pl.BlockSpec((8, LANES), const)],
        out_shape=[jax.ShapeDtypeStruct((t, D), F32), jax.ShapeDtypeStruct((t, ROW_SUB, LANES), BF16),
                   jax.ShapeDtypeStruct((t, LANES), F32), jax.ShapeDtypeStruct((8, t), F32),
                   jax.ShapeDtypeStruct((8, LANES), F32)],
        scratch_shapes=[pltpu.VMEM((8, LANES), F32), pltpu.VMEM((ROW_GROUP * ROW_SUB, LANES), F32)],
        compiler_params=_cparams(("arbitrary",)),
        name="post",
    )(x_ctx, x_lat, a_ctx, a_lat, g_ctx, g_lat, mod_all, w_mg, w_ba, w_bg, w_o, l1g, l1b, w_r, b_r)


ROW_UNROLL = 8


def _row_copy(src_ref, dst_ref, sem):
    return pltpu.make_async_copy(src_ref, dst_ref, sem)


def _scatter_kernel(pos0_ref, pos1_ref, h_ref, xs_ref, sem):
    ts = h_ref.shape[0]

    def issue(g, carry):
        r0 = pl.multiple_of(g * ROW_UNROLL, ROW_UNROLL)
        for k in range(ROW_UNROLL):
            src = _row_slab(h_ref, r0 + k)
            _row_copy(src, _row_slab(xs_ref, pos0_ref[0, r0 + k]), sem).start(priority=0)
            _row_copy(src, _row_slab(xs_ref, pos1_ref[0, r0 + k]), sem).start(priority=1)
        return carry

    lax.fori_loop(0, ts // ROW_UNROLL, issue, 0)
    for _ in range(2):
        _row_copy(h_ref, xs_ref.at[pl.ds(0, ts)], sem).wait()


def _scatter_rows(h2p, pos0, pos1):
    t = h2p.shape[0]
    ts = TS_ROWS
    smem = lambda: pl.BlockSpec((None, 1, ts), lambda i: (i, 0, 0), memory_space=pltpu.SMEM)
    return pl.pallas_call(
        _scatter_kernel,
        grid=(t // ts,),
        in_specs=[smem(), smem(), pl.BlockSpec((ts, ROW_SUB, LANES), lambda i: (i, 0, 0))],
        out_specs=pl.BlockSpec(memory_space=pl.ANY),
        out_shape=jax.ShapeDtypeStruct((2 * t, ROW_SUB, LANES), BF16),
        scratch_shapes=[pltpu.SemaphoreType.DMA(())],
        compiler_params=_cparams(("arbitrary",)),
        name="scatter",
    )(pos0.reshape(t // ts, 1, ts), pos1.reshape(t // ts, 1, ts), h2p)


def _expert_kernel(vt_ref, ve_ref, lo_ref, hi_ref, first_ref, last_ref, newe_ref,
                   xs_ref, wg_ref, wu_ref, wd_ref, y_ref, wgu_s, wd_s, acc_s, slab_s):
    v = pl.program_id(0)
    tm = acc_s.shape[0]

    @pl.when(newe_ref[v] == 1)
    def _():
        wgu_s[:, 0:D_EXP] = wg_ref[...].astype(BF16)
        wgu_s[:, D_EXP:2 * D_EXP] = wu_ref[...].astype(BF16)
        wd_s[...] = wd_ref[...].astype(BF16)

    @pl.when(first_ref[v] == 1)
    def _():
        acc_s[...] = jnp.zeros_like(acc_s)

    lo = lo_ref[v]
    hi = hi_ref[v]

    @pl.when(hi > lo)
    def _():
        gu = _dot(_load_row_slabs(xs_ref, slab_s).astype(BF16), wgu_s[...])
        hid = _silu(gu[:, 0:D_EXP]) * gu[:, D_EXP:2 * D_EXP]
        y = _dot(hid.astype(BF16), wd_s[...])
        rid = lax.broadcasted_iota(jnp.int32, (tm, D), 0)
        acc_s[...] = jnp.where((rid >= lo) & (rid < hi), y, acc_s[...])

    @pl.when(last_ref[v] == 1)
    def _():
        _store_row_slabs(y_ref, acc_s[...], slab_s)


def _experts(xs, meta, w_gate, w_up, w_down):
    p = xs.shape[0]
    nv = meta[0].shape[0]
    tm = TM_EXP
    xmap = lambda v, vt, ve, lo, hi, fi, la, ne: (vt[v], 0, 0)
    wmap = lambda v, vt, ve, lo, hi, fi, la, ne: (ve[v], 0, 0)
    return pl.pallas_call(
        _expert_kernel,
        grid_spec=pltpu.PrefetchScalarGridSpec(
            num_scalar_prefetch=7,
            grid=(nv,),
            in_specs=[pl.BlockSpec((tm, ROW_SUB, LANES), xmap),
                      pl.BlockSpec((None, D, D_EXP), wmap), pl.BlockSpec((None, D, D_EXP), wmap),
                      pl.BlockSpec((None, D_EXP, D), wmap)],
            out_specs=pl.BlockSpec((tm, ROW_SUB, LANES), xmap),
            scratch_shapes=[pltpu.VMEM((D, 2 * D_EXP), BF16), pltpu.VMEM((D_EXP, D), BF16),
                            pltpu.VMEM((tm, D), F32), pltpu.VMEM((tm * ROW_SUB, LANES), F32)]),
        out_shape=jax.ShapeDtypeStruct((p, ROW_SUB, LANES), BF16),
        compiler_params=_cparams(("arbitrary",)),
        name="experts",
    )(*meta, xs, w_gate, w_up, w_down)


def _final_kernel(p0c_ref, p1c_ref, p0n_ref, p1n_ref, x1_ref, rt_ref, mod_ref, l2g_ref, l2b_ref, ys_ref,
                  oc_ref, ol_ref, buf, sem, slab_s, *, n_ctx_tiles):
    i = pl.program_id(0)
    n = pl.num_programs(0)
    tm = x1_ref.shape[0]

    def gather(p0_ref, p1_ref, slot):
        def issue(g, carry):
            r0 = pl.multiple_of(g * ROW_UNROLL, ROW_UNROLL)
            for k in range(ROW_UNROLL):
                _row_copy(_row_slab(ys_ref, p0_ref[0, r0 + k]),
                          _row_slab(buf.at[slot, 0], r0 + k), sem.at[slot]).start(priority=0)
                _row_copy(_row_slab(ys_ref, p1_ref[0, r0 + k]),
                          _row_slab(buf.at[slot, 1], r0 + k), sem.at[slot]).start(priority=1)
            return carry

        lax.fori_loop(0, tm // ROW_UNROLL, issue, 0)

    cur = i % 2

    @pl.when(i == 0)
    def _():
        gather(p0c_ref, p1c_ref, 0)

    @pl.when(i + 1 < n)
    def _():
        gather(p0n_ref, p1n_ref, 1 - cur)

    for k in range(2):
        _row_copy(ys_ref.at[pl.ds(0, tm)], buf.at[cur, k], sem.at[cur]).wait()

    g2 = mod_ref[0, :, 5 * D:6 * D]
    w1 = rt_ref[:, 2:3]
    w2 = rt_ref[:, 3:4]
    moe = (w1 * _load_row_slabs(buf.at[cur, 0], slab_s.at[0])
           + w2 * _load_row_slabs(buf.at[cur, 1], slab_s.at[1]))
    out = _ln(ALPHA * x1_ref[...] + g2 * moe) * l2g_ref[...] + l2b_ref[...]

    @pl.when(i < n_ctx_tiles)
    def _():
        oc_ref[...] = out

    @pl.when(i >= n_ctx_tiles)
    def _():
        ol_ref[...] = out


def _final(x1, rt, pos0, pos1, mod_all, l2g, l2b, ys, t_ctx, seq_lat):
    t = x1.shape[0]
    tm = TM_TOK
    nt = t // tm
    nct = t_ctx // tm
    t_lat = t - t_ctx
    per_seq = seq_lat // tm
    nb_lat = t_lat // seq_lat
    p0 = pos0.reshape(nt, 1, tm)
    p1 = pos1.reshape(nt, 1, tm)
    row = lambda i: (i, 0)
    const = lambda i: (0, 0)
    mmap = lambda i: (jnp.where(i < nct, nb_lat, jnp.maximum(i - nct, 0) // per_seq), 0, 0)
    smem_cur = lambda: pl.BlockSpec((None, 1, tm), lambda i: (i, 0, 0), memory_space=pltpu.SMEM)
    smem_nxt = lambda: pl.BlockSpec((None, 1, tm), lambda i: (jnp.minimum(i + 1, nt - 1), 0, 0),
                                    memory_space=pltpu.SMEM)
    return pl.pallas_call(
        functools.partial(_final_kernel, n_ctx_tiles=nct),
        grid=(nt,),
        in_specs=[smem_cur(), smem_cur(), smem_nxt(), smem_nxt(),
                  pl.BlockSpec((tm, D), row), pl.BlockSpec((tm, LANES), row),
                  pl.BlockSpec((1, 1, 6 * D), mmap),
                  pl.BlockSpec((1, D), const), pl.BlockSpec((1, D), const),
                  pl.BlockSpec(memory_space=pl.ANY)],
        out_specs=[pl.BlockSpec((tm, D), lambda i: (jnp.minimum(i, nct - 1), 0)),
                   pl.BlockSpec((tm, D), lambda i: (jnp.maximum(i - nct, 0), 0))],
        out_shape=[jax.ShapeDtypeStruct((t_ctx, D), F32), jax.ShapeDtypeStruct((t_lat, D), F32)],
        scratch_shapes=[pltpu.VMEM((2, 2, tm, ROW_SUB, LANES), BF16), pltpu.SemaphoreType.DMA((2,)),
                        pltpu.VMEM((2, tm * ROW_SUB, LANES), F32)],
        compiler_params=_cparams(("arbitrary",)),
        name="final",
    )(p0, p1, p0, p1, x1, rt, mod_all, l2g, l2b, ys)


def _reorder_q_heads(w, axis):
    shape = w.shape
    split = shape[:axis] + (N_KV_HEADS, N_Q_HEADS // N_KV_HEADS, HD) + shape[axis + 1:]
    return jnp.swapaxes(w.reshape(split), axis, axis + 1).reshape(shape)


def _rope_tables(seq):
    t = np.arange(seq)
    half = HD // 4
    inv = (ROPE_THETA ** (-np.arange(half, dtype=np.float64) / half)).astype(np.float32)
    d64 = np.arange(LANES) % HD
    pos = np.where((d64 < HD // 2)[None, :], (t // GRID_W)[:, None], (t % GRID_W)[:, None])
    ang = (pos.astype(np.float32) * inv[d64 % half][None, :]).astype(np.float64)
    sign = np.where((d64 % 32) < 16, -1.0, 1.0)
    return (jnp.asarray(np.cos(ang), F32), jnp.asarray(np.sin(ang) * sign[None, :], F32))


def _pair_states(s):
    b = s.shape[0]
    s = s.reshape(b, 2, 2, GLA_DK, GLA_DV)
    return s.transpose(0, 1, 4, 2, 3).reshape(b, 2, GLA_DV, 2 * GLA_DK)


def _unpair_states(s):
    b = s.shape[0]
    s = s.reshape(b, 2, GLA_DV, 2, GLA_DK)
    return s.transpose(0, 1, 3, 4, 2).reshape(b, GLA_H, GLA_DK, GLA_DV)


def _route_tables(ert, counts, t):
    i32 = jnp.int32
    cnt = counts[0, :N_EXP].astype(i32)
    ends = jnp.cumsum(cnt)
    starts = ends - cnt
    table = lambda e: jnp.sum(jnp.where(e[None, :] == jnp.arange(N_EXP, dtype=i32)[:, None],
                                        starts[:, None], 0), axis=0)
    pos0 = table(ert[0].astype(i32)) + ert[4].astype(i32)
    pos1 = table(ert[1].astype(i32)) + ert[5].astype(i32)
    p = 2 * t
    nt = p // TM_EXP
    nv = nt + N_EXP
    tile_starts = jnp.arange(nt, dtype=i32) * TM_EXP
    idx_t = jnp.arange(nt, dtype=i32) + jnp.sum(starts[None, :] < tile_starts[:, None], axis=1, dtype=i32)
    idx_e = jnp.arange(N_EXP, dtype=i32) + jnp.minimum(starts // TM_EXP + 1, nt)
    k = jnp.arange(nv, dtype=i32)[:, None]
    bounds = (jnp.sum(jnp.where(idx_t[None, :] == k, tile_starts[None, :], 0), axis=1)
              + jnp.sum(jnp.where(idx_e[None, :] == k, starts[None, :], 0), axis=1))
    nxt = jnp.concatenate([bounds[1:], jnp.array([p], i32)])
    tile = jnp.minimum(bounds // TM_EXP, nt - 1)
    exp_id = jnp.minimum(jnp.sum(ends[None, :] <= bounds[:, None], axis=1, dtype=i32), N_EXP - 1)
    lo = bounds - tile * TM_EXP
    hi = jnp.where(nxt > bounds, jnp.minimum(nxt - tile * TM_EXP, TM_EXP), lo)
    prev_tile = jnp.concatenate([jnp.array([-1], i32), tile[:-1]])
    prev_exp = jnp.concatenate([jnp.array([-1], i32), exp_id[:-1]])
    next_tile = jnp.concatenate([tile[1:], jnp.array([-1], i32)])
    first = (tile != prev_tile).astype(i32)
    last = (tile != next_tile).astype(i32)
    newe = (exp_id != prev_exp).astype(i32)
    return pos0, pos1, (tile, exp_id, lo, hi, first, last, newe)


def kernel(x_prompt, x_sample, cache_k, cache_v, state_gla_fwd, state_gla_bwd, c, c_ctx, w_ada, b_ada, w_in, q_norm, k_norm, gla_w_gate, gla_b_gate, gla_norm, w_br_attn, w_br_gla, w_out, ln1_g, ln1_b, router_group_w, router_group_b, router_expert_w, router_expert_b, exp_w_gate, exp_w_up, exp_w_down, ln2_g, ln2_b):
    b_ctx, seq_ctx, _ = x_prompt.shape
    b_lat, seq_lat, _ = x_sample.shape
    t_ctx, t_lat = b_ctx * seq_ctx, b_lat * seq_lat
    t = t_ctx + t_lat
    l = 0

    rows = -(-(b_lat + 1) // 8) * 8
    c_rows = jnp.zeros((rows, D), F32).at[:b_lat].set(c).at[b_lat].set(c_ctx)
    mod = _ada(c_rows, w_ada[l], b_ada[l][None, :])
    mod_all = mod[:b_lat + 1, None, :]
    mod_lat = mod_all[:b_lat]
    mod_ctx = mod_all[b_lat:]

    w_full = w_in[l]
    w_a = jnp.concatenate([_reorder_q_heads(w_full[:, :AW], 1), w_full[:, AW:A_WIDTH]], axis=1).astype(BF16)
    w_mg = w_full[:, A_WIDTH:].astype(BF16)
    gain = jnp.concatenate([jnp.tile(q_norm[l], N_Q_HEADS), jnp.tile(k_norm[l], N_KV_HEADS)])[None, :]
    head_of = np.arange(AW + KVW) // HD
    ind = jnp.asarray((head_of[:, None] == np.arange(LANES)[None, :]) / HD, BF16)
    w_ba = _reorder_q_heads(w_br_attn[l], 0).astype(BF16)
    w_bg = w_br_gla[l].astype(BF16)
    w_o = w_out[l].astype(BF16)
    w_r = jnp.zeros((D, LANES), F32).at[:, :N_GROUPS].set(router_group_w[l])
    w_r = w_r.at[:, N_GROUPS:N_GROUPS + N_EXP].set(router_expert_w[l]).astype(BF16)
    b_r = jnp.zeros((1, LANES), F32).at[0, :N_GROUPS].set(router_group_b[l])
    b_r = b_r.at[0, N_GROUPS:N_GROUPS + N_EXP].set(router_expert_b[l])
    wg = jnp.zeros((2, 2 * GATE_RANK, GKW), F32)
    wg = wg.at[0, :GATE_RANK].set(gla_w_gate[l, 0]).at[1, GATE_RANK:].set(gla_w_gate[l, 1])
    bg = gla_b_gate[l][:, None, :]
    gn = gla_norm[l][None, :]

    xc = x_prompt.reshape(t_ctx, D)
    xl = x_sample.reshape(t_lat, D)

    q_c, k_c, v_c, qg_c, kg_c, vg_c, rs_c, lr_c, kf_c, vf_c = _inproj(
        xc, mod_ctx, w_a, gain, ind, None, seq_ctx, latent=False)
    attn_c = _attention(q_c, k_c, v_c, None, seq_ctx)
    zero_state = jnp.zeros((1, 2, 2, LANES, LANES), F32)
    gla_c, sfin_c = _gla(qg_c, kg_c, vg_c, lr_c, rs_c, zero_state, wg, bg, gn, seq_ctx)

    q_l, k_l, v_l, qg_l, kg_l, vg_l, rs_l, lr_l = _inproj(
        xl, mod_lat, w_a, gain, ind, _rope_tables(seq_lat), seq_lat, latent=True)
    past = cache_k.shape[2]
    kc = cache_k[:, l].reshape(b_lat, past, KVW).astype(BF16)
    vc = cache_v[:, l].reshape(b_lat, past, KVW).astype(BF16)
    attn_l = _attention(q_l, k_l, v_l, (kc, vc), seq_lat)
    s0 = jnp.stack([_pair_states(state_gla_fwd[:, l]), _pair_states(state_gla_bwd[:, l])], axis=1)
    gla_l, _ = _gla(qg_l, kg_l, vg_l, lr_l, rs_l, s0, wg, bg, gn, seq_lat)

    x1, h2p, rt, ert, counts = _post(xc, xl, attn_c, attn_l, gla_c, gla_l, mod_all, seq_lat,
                                w_mg, w_ba, w_bg, w_o, ln1_g[l][None, :], ln1_b[l][None, :], w_r, b_r)

    pos0, pos1, meta = _route_tables(ert, counts, t)
    xs = _scatter_rows(h2p, pos0, pos1)
    ys = _experts(xs, meta, exp_w_gate[l], exp_w_up[l], exp_w_down[l])
    y_ctx, y_lat = _final(x1, rt, pos0, pos1, mod_all, ln2_g[l][None, :], ln2_b[l][None, :], ys,
                          t_ctx, seq_lat)

    untranspose = lambda a: a.reshape(b_ctx, 1, N_KV_HEADS, HD, seq_ctx).transpose(0, 1, 4, 2, 3)
    new_k, new_v = untranspose(kf_c), untranspose(vf_c)
    new_sf = _unpair_states(sfin_c[:, 0])[:, None]
    new_sb = _unpair_states(sfin_c[:, 1])[:, None]
    return (y_ctx.reshape(b_ctx, seq_ctx, D), y_lat.reshape(b_lat, seq_lat, D),
            new_k, new_v, new_sf, new_sb)
```

```python
import functools

import numpy as np
import jax
import jax.numpy as jnp
from jax import lax
from jax.experimental import pallas as pl
from jax.experimental.pallas import tpu as pltpu

F32 = jnp.float32
BF16 = jnp.bfloat16
HIGHEST = lax.Precision.HIGHEST

D = 1024
GRID_W = 64
HD = 64
N_Q_HEADS = 8
N_KV_HEADS = 2
AW = N_Q_HEADS * HD
KVW = N_KV_HEADS * HD
ROPE_THETA = 10000.0
GLA_H = 4
GLA_DK = 64
GLA_DV = 128
GKW = GLA_H * GLA_DK
GVW = GLA_H * GLA_DV
GATE_RANK = 16
GLA_TAU = 16.0
CHUNK = 64
N_GROUPS = 4
EPG = 8
N_EXP = N_GROUPS * EPG
D_EXP = 256
DEPTH = 1
ALPHA = (2.0 * DEPTH) ** 0.25
LN_EPS = 1e-6
RMS_EPS = 1e-6

LANES = 128
A_WIDTH = AW + 2 * KVW + 2 * GKW + 2 * GVW + 2 * GATE_RANK
MG_WIDTH = 2 * D
TM_TOK = 512
ROW_GROUP = 128
LOG2E = 1.4426950408889634
ONES_ROWS = 16
TQ_LAT = 256
TM_EXP = 256
TS_ROWS = 256
VMEM_LIMIT = 56 * 1024 * 1024


def _cparams(sem):
    return pltpu.CompilerParams(dimension_semantics=sem, vmem_limit_bytes=VMEM_LIMIT)


def _dot(a, b):
    return jnp.dot(a, b, preferred_element_type=F32)


def _dot_nt(a, b):
    return lax.dot_general(a, b, (((1,), (1,)), ((), ())), preferred_element_type=F32)


def _dot_tn(a, b):
    return lax.dot_general(a, b, (((0,), (0,)), ((), ())), preferred_element_type=F32)


def _ln(x):
    mu = jnp.mean(x, axis=-1, keepdims=True)
    xc = x - mu
    var = jnp.mean(xc * xc, axis=-1, keepdims=True)
    return xc * lax.rsqrt(var + LN_EPS)


def _silu(x):
    return x * jax.nn.sigmoid(x)


def _split_bf16(x):
    hi = x.astype(BF16)
    lo = (x - hi.astype(F32)).astype(BF16)
    return hi, lo


def _ada_kernel(c_ref, w_ref, b_ref, o_ref):
    s = _silu(c_ref[...])
    o_ref[...] = jnp.dot(s, w_ref[...], preferred_element_type=F32, precision=HIGHEST) + b_ref[...]


def _ada(c_rows, w_ada, b_ada):
    rows = c_rows.shape[0]
    n = w_ada.shape[1]
    bn = 1024
    return pl.pallas_call(
        _ada_kernel,
        grid=(n // bn,),
        in_specs=[pl.BlockSpec((rows, D), lambda j: (0, 0)),
                  pl.BlockSpec((D, bn), lambda j: (0, j)),
                  pl.BlockSpec((1, bn), lambda j: (0, j))],
        out_specs=pl.BlockSpec((rows, bn), lambda j: (0, j)),
        out_shape=jax.ShapeDtypeStruct((rows, n), F32),
        compiler_params=_cparams(("arbitrary",)),
        name="ada",
    )(c_rows, w_ada, b_ada)


def _inproj_kernel(*refs, latent, seq):
    if latent:
        (x_ref, mod_ref, w_ref, gain_ref, ind_ref, cos_ref, sin_ref,
         q_ref, k_ref, v_ref, qg_ref, kg_ref, vg_ref, rs_ref, lr_ref) = refs
    else:
        (x_ref, mod_ref, w_ref, gain_ref, ind_ref,
         q_ref, k_ref, v_ref, qg_ref, kg_ref, vg_ref, rs_ref, lr_ref, kf_ref, vf_ref) = refs
    tm = x_ref.shape[0]
    sub = ROW_GROUP
    n_groups = tm // sub
    sh1 = mod_ref[0, :, 0:D]
    sc1 = mod_ref[0, :, D:2 * D]
    lane = lax.broadcasted_iota(jnp.int32, (sub, LANES), 1)
    low_half = lane < HD
    first = (lane % 32) < 16

    def project(g):
        rows = slice(g * sub, (g + 1) * sub)
        h = (_ln(x_ref[rows, :]) * (1.0 + sc1) + sh1).astype(BF16)
        return _dot(h, w_ref[...])

    def finish(g, res):
        rows = slice(g * sub, (g + 1) * sub)
        qk = res[:, 0:AW + KVW]
        hi, lo = _split_bf16(qk * qk)
        ms = _dot(hi, ind_ref[...]) + _dot(lo, ind_ref[...])
        r = lax.rsqrt(ms + RMS_EPS)
        for s in range(5):
            rb = jnp.where(low_half, r[:, 2 * s:2 * s + 1], r[:, 2 * s + 1:2 * s + 2])
            y = res[:, LANES * s:LANES * (s + 1)] * rb * gain_ref[:, LANES * s:LANES * (s + 1)]
            if s == 4 and not latent:
                kf_ref[(g * sub) // seq, :, (g * sub) % seq:(g * sub) % seq + sub] = y.T
            if latent:
                partner = jnp.where(first, pltpu.roll(y, LANES - 16, 1), pltpu.roll(y, 16, 1))
                y = y * cos_ref[rows, :] + partner * sin_ref[rows, :]
            if s < 4:
                q_ref[rows, LANES * s:LANES * (s + 1)] = (y * (HD ** -0.5 * LOG2E)).astype(BF16)
            else:
                k_ref[rows, :] = y.astype(BF16)
        o = AW + KVW
        v = res[:, o:o + KVW]
        v_ref[rows, :] = v.astype(BF16)
        if not latent:
            vf_ref[(g * sub) // seq, :, (g * sub) % seq:(g * sub) % seq + sub] = v.T
        o += KVW
        qg_ref[rows, :] = res[:, o:o + GKW] * (GLA_DK ** -0.5)
        o += GKW
        kg_ref[rows, :] = res[:, o:o + GKW]
        o += GKW
        vg_ref[rows, :] = res[:, o:o + GVW].astype(BF16)
        o += GVW
        rs_ref[rows, :] = _silu(res[:, o:o + GVW]).astype(BF16)
        o += GVW
        lr_ref[rows, :] = res[:, o:o + 2 * GATE_RANK]

    pending = {0: project(0)}
    for g in range(n_groups):
        if g + 1 < n_groups:
            pending[g + 1] = project(g + 1)
        finish(g, pending.pop(g))


def _inproj(x2, mod, w_a, gain, ind, rope, seq, latent):
    t = x2.shape[0]
    tm = TM_TOK
    per_seq = max(seq // tm, 1)
    per_tile = max(tm // seq, 1)
    row = lambda i: (i, 0)
    const = lambda i: (0, 0)
    in_specs = [pl.BlockSpec((tm, D), row),
                pl.BlockSpec((1, 1, 6 * D), (lambda i: (i // per_seq, 0, 0)) if latent else (lambda i: (0, 0, 0))),
                pl.BlockSpec((D, A_WIDTH), const),
                pl.BlockSpec((1, AW + KVW), const),
                pl.BlockSpec((AW + KVW, LANES), const)]
    args = [x2, mod, w_a, gain, ind]
    if latent:
        in_specs += [pl.BlockSpec((tm, LANES), lambda i: (i % per_seq, 0))] * 2
        args += list(rope)
    widths = [(AW, BF16), (KVW, BF16), (KVW, BF16), (GKW, F32), (GKW, F32), (GVW, BF16), (GVW, BF16),
              (2 * GATE_RANK, F32)]
    out_specs = [pl.BlockSpec((tm, w), row) for w, _ in widths]
    out_shape = [jax.ShapeDtypeStruct((t, w), dt) for w, dt in widths]
    if not latent:
        cache_spec = pl.BlockSpec((per_tile, KVW, seq), lambda i: (i, 0, 0))
        out_specs += [cache_spec] * 2
        out_shape += [jax.ShapeDtypeStruct((t // seq, KVW, seq), F32)] * 2
    return pl.pallas_call(
        functools.partial(_inproj_kernel, latent=latent, seq=seq),
        grid=(t // tm,),
        in_specs=in_specs,
        out_specs=out_specs,
        out_shape=out_shape,
        compiler_params=_cparams(("parallel",)),
        name="inproj_lat" if latent else "inproj_ctx",
    )(*args)


def _attn_kernel(*refs, has_cache):
    def transposed_with_ones(dst, src):
        dst[0:KVW, :] = src[...].astype(F32).T.astype(BF16)
        dst[KVW:, :] = jnp.ones((ONES_ROWS, dst.shape[1]), BF16)

    if has_cache:
        q_ref, k_ref, v_ref, kc_ref, vc_ref, o_ref, vt_s, vct_s = refs

        @pl.when(pl.program_id(1) == 0)
        def _():
            transposed_with_ones(vt_s, v_ref)
            transposed_with_ones(vct_s, vc_ref)
    else:
        q_ref, k_ref, v_ref, o_ref, vt_s = refs
        transposed_with_ones(vt_s, v_ref)
    tq = q_ref.shape[0]

    lane = lax.broadcasted_iota(jnp.int32, (tq, LANES), 1)
    low_half = lane < HD
    k = k_ref[...]
    scores = []
    for j in range(N_KV_HEADS):
        keep = low_half if j == 0 else jnp.logical_not(low_half)
        zero = jnp.zeros((tq, LANES), BF16)
        for pair in range(2):
            qs = jnp.concatenate([jnp.where(keep, q_ref[:, LANES * s:LANES * (s + 1)], zero)
                                  for s in (2 * pair, 2 * pair + 1)], axis=0)
            s1 = _dot_nt(k, qs)
            s2 = _dot_nt(kc_ref[...], qs) if has_cache else None
            scores.append((s1, s2))
    outs = []
    for s1, s2 in scores:
        m = jnp.max(s1, axis=0, keepdims=True)
        if has_cache:
            m = jnp.maximum(m, jnp.max(s2, axis=0, keepdims=True))
        acc = _dot(vt_s[...], jnp.exp2(s1 - m).astype(BF16))
        if has_cache:
            acc = acc + _dot(vct_s[...], jnp.exp2(s2 - m).astype(BF16))
        outs.append(acc[0:KVW] / acc[KVW:KVW + 1])
    head0 = jnp.concatenate(outs[0:2], axis=1)
    head1 = jnp.concatenate(outs[2:4], axis=1)
    row = lax.broadcasted_iota(jnp.int32, (LANES, 4 * tq), 0)
    out = jnp.where(row < HD, head0, head1).T
    for s in range(4):
        o_ref[:, LANES * s:LANES * (s + 1)] = out[s * tq:(s + 1) * tq].astype(BF16)


def _attention(q, k, v, cache, seq):
    t = q.shape[0]
    if cache is None:
        tq = seq
        grid = (t // seq,)
        qmap = lambda b: (b, 0)
        in_specs = [pl.BlockSpec((tq, AW), qmap), pl.BlockSpec((seq, KVW), qmap),
                    pl.BlockSpec((seq, KVW), qmap)]
        args = [q, k, v]
        scratch = [pltpu.VMEM((KVW + ONES_ROWS, seq), BF16)]
        sem = ("parallel",)
        name = "attn_ctx"
    else:
        tq = TQ_LAT
        nq = seq // tq
        kc, vc = cache
        past = kc.shape[1]
        grid = (t // seq, nq)
        qmap = lambda b, i: (b * nq + i, 0)
        kmap = lambda b, i: (b, 0)
        cmap = lambda b, i: (b, 0, 0)
        in_specs = [pl.BlockSpec((tq, AW), qmap), pl.BlockSpec((seq, KVW), kmap),
                    pl.BlockSpec((seq, KVW), kmap),
                    pl.BlockSpec((None, past, KVW), cmap), pl.BlockSpec((None, past, KVW), cmap)]
        args = [q, k, v, kc, vc]
        scratch = [pltpu.VMEM((KVW + ONES_ROWS, seq), BF16), pltpu.VMEM((KVW + ONES_ROWS, past), BF16)]
        sem = ("parallel", "arbitrary")
        name = "attn_lat"
    return pl.pallas_call(
        functools.partial(_attn_kernel, has_cache=cache is not None),
        grid=grid,
        in_specs=in_specs,
        out_specs=pl.BlockSpec((tq, AW), qmap),
        out_shape=jax.ShapeDtypeStruct((t, AW), BF16),
        scratch_shapes=scratch,
        compiler_params=_cparams(sem),
        name=name,
    )(*args)


GLA_BLK = 256
GLA_UNROLL = 2


def _split3_bf16(x):
    hi = x.astype(BF16)
    r1 = x - hi.astype(F32)
    mid = r1.astype(BF16)
    lo = (r1 - mid.astype(F32)).astype(BF16)
    return hi, mid, lo


def _gla_kernel(qg_ref, kg_ref, vg_ref, lr_ref, rs_ref, s0_ref, wg_ref, bg_ref, gn_ref,
                o_ref, sfin_ref, cum_s, kv_s, dec_s):
    n = qg_ref.shape[0]
    nc = n // CHUNK
    lane = lax.broadcasted_iota(jnp.int32, (CHUNK, LANES), 1)
    low_half = lane < GLA_DK
    lane_sq = lax.broadcasted_iota(jnp.int32, (LANES, LANES), 1)
    low_half_sq = lane_sq < GLA_DK
    ri = lax.broadcasted_iota(jnp.int32, (2 * CHUNK, 2 * CHUNK), 0)
    ci = lax.broadcasted_iota(jnp.int32, (2 * CHUNK, 2 * CHUNK), 1)
    diag = (ri >> 6) == (ci >> 6)
    keep2 = (diag & (ri >= ci), diag & (ci >= ri))

    rb = lax.broadcasted_iota(jnp.int32, (GLA_BLK, GLA_BLK), 0)
    cb = lax.broadcasted_iota(jnp.int32, (GLA_BLK, GLA_BLK), 1)
    same = (rb >> 6) == (cb >> 6)
    tri = ((same & (rb >= cb)).astype(BF16), (same & (cb >= rb)).astype(BF16))
    for d in range(2):
        w_hi, w_lo = _split_bf16(wg_ref[d])
        for blk in range(n // GLA_BLK):
            rows = slice(blk * GLA_BLK, (blk + 1) * GLA_BLK)
            l_hi, l_lo = _split_bf16(lr_ref[rows, :])
            z = _dot(l_hi, w_hi) + _dot(l_lo, w_hi) + _dot(l_hi, w_lo) + bg_ref[d]
            logg = (jnp.minimum(z, 0.0) - jnp.log(1.0 + jnp.exp(-jnp.abs(z)))) * (1.0 / GLA_TAU)
            pieces = _split3_bf16(logg)
            cum_s[d, rows, :] = _dot(tri[d], pieces[0]) + _dot(tri[d], pieces[1]) + _dot(tri[d], pieces[2])

    def pass1(i, carry):
        cs = [i * GLA_UNROLL + u for u in range(GLA_UNROLL)]
        rws = [pl.ds(pl.multiple_of(c * CHUNK, CHUNK), CHUNK) for c in cs]
        prods = []
        for c, rows in zip(cs, rws):
            kc = kg_ref[rows, :]
            kdec = []
            for d in range(2):
                cum = cum_s[d, rows, :]
                last = cum[CHUNK - 1:CHUNK, :] if d == 0 else cum[0:1, :]
                dec_s[d, c] = jnp.exp(last)
                kdec.append((kc * jnp.exp(last - cum)).astype(BF16))
            for p in range(2):
                vpair = vg_ref[rows, GLA_DV * 2 * p:GLA_DV * 2 * (p + 1)]
                kpair = jnp.concatenate([kdec[0][:, LANES * p:LANES * (p + 1)],
                                         kdec[1][:, LANES * p:LANES * (p + 1)]], axis=1)
                prods.append(_dot_tn(vpair, kpair))
        for j, c in enumerate(cs):
            for p in range(2):
                res = prods[2 * j + p]
                for d in range(2):
                    cols = slice(LANES * d, LANES * (d + 1))
                    kv_s[d, c, p] = jnp.where(low_half_sq, res[0:GLA_DV, cols], res[GLA_DV:2 * GLA_DV, cols])
        return carry

    lax.fori_loop(0, nc // GLA_UNROLL, pass1, 0)

    for d in range(2):
        def scan(i, st):
            c = i if d == 0 else nc - 1 - i
            dec = dec_s[d, c]
            new = []
            for p in range(2):
                kv = kv_s[d, c, p]
                kv_s[d, c, p] = st[p]
                new.append(st[p] * dec[:, LANES * p:LANES * (p + 1)] + kv)
            return tuple(new)

        fin = lax.fori_loop(0, nc, scan, (s0_ref[d, 0], s0_ref[d, 1]))
        sfin_ref[d, 0] = fin[0]
        sfin_ref[d, 1] = fin[1]

    def pass3(i, carry):
        cs = [i * GLA_UNROLL + u for u in range(GLA_UNROLL)]
        rws = [pl.ds(pl.multiple_of(c * CHUNK, CHUNK), CHUNK) for c in cs]
        first = []
        for c, rows in zip(cs, rws):
            q = qg_ref[rows, :]
            k = kg_ref[rows, :]
            for d in range(2):
                cum = cum_s[d, rows, :]
                qt = q * jnp.exp(cum)
                kt = (k * jnp.exp(-cum)).astype(BF16)
                for p in range(2):
                    qs = qt[:, LANES * p:LANES * (p + 1)]
                    lhs = jnp.concatenate([jnp.where(low_half, qs, 0.0), jnp.where(low_half, 0.0, qs)],
                                          axis=0).astype(BF16)
                    kts = kt[:, LANES * p:LANES * (p + 1)]
                    rhs = jnp.concatenate([kts, kts, kv_s[d, c, p].astype(BF16)], axis=0)
                    first.append(_dot_nt(lhs, rhs))
        second = []
        for j, rows in enumerate(rws):
            for d in range(2):
                for p in range(2):
                    res = first[4 * j + 2 * d + p]
                    vp = jnp.concatenate([vg_ref[rows, GLA_DV * (2 * p):GLA_DV * (2 * p + 1)],
                                          vg_ref[rows, GLA_DV * (2 * p + 1):GLA_DV * (2 * p + 2)]], axis=0)
                    a = jnp.where(keep2[d], res[:, 0:2 * CHUNK], 0.0).astype(BF16)
                    second.append(_dot(a, vp) + res[:, 2 * CHUNK:])
        for j, rows in enumerate(rws):
            for p in range(2):
                tot = second[4 * j + p] + second[4 * j + 2 + p]
                y = tot * lax.rsqrt(jnp.mean(tot * tot, axis=-1, keepdims=True) + RMS_EPS) * gn_ref[...]
                for hh in range(2):
                    cols = slice(GLA_DV * (2 * p + hh), GLA_DV * (2 * p + hh + 1))
                    o_ref[rows, cols] = (y[CHUNK * hh:CHUNK * (hh + 1)]
                                         * rs_ref[rows, cols].astype(F32)).astype(BF16)
        return carry

    lax.fori_loop(0, nc // GLA_UNROLL, pass3, 0)


def _gla(qg, kg, vg, lr, rs, s0, wg, bg, gn, seq):
    t = qg.shape[0]
    nb = t // seq
    nc = seq // CHUNK
    row = lambda b: (b, 0)
    c3 = lambda b: (0, 0, 0)
    st = lambda b: (b, 0, 0, 0, 0)
    if s0.shape[0] == 1:
        s0map = lambda b: (0, 0, 0, 0, 0)
    else:
        s0map = st
    return pl.pallas_call(
        _gla_kernel,
        grid=(nb,),
        in_specs=[pl.BlockSpec((seq, GKW), row), pl.BlockSpec((seq, GKW), row),
                  pl.BlockSpec((seq, GVW), row), pl.BlockSpec((seq, 2 * GATE_RANK), row),
                  pl.BlockSpec((seq, GVW), row),
                  pl.BlockSpec((None, 2, 2, LANES, LANES), s0map),
                  pl.BlockSpec((2, 2 * GATE_RANK, GKW), c3), pl.BlockSpec((2, 1, GKW), c3),
                  pl.BlockSpec((1, GLA_DV), lambda b: (0, 0))],
        out_specs=[pl.BlockSpec((seq, GVW), row),
                   pl.BlockSpec((None, 2, 2, LANES, LANES), st)],
        out_shape=[jax.ShapeDtypeStruct((t, GVW), BF16),
                   jax.ShapeDtypeStruct((nb, 2, 2, LANES, LANES), F32)],
        scratch_shapes=[pltpu.VMEM((2, seq, GKW), F32),
                        pltpu.VMEM((2, nc, 2, LANES, LANES), F32),
                        pltpu.VMEM((2, nc, 1, GKW), F32)],
        compiler_params=_cparams(("parallel",)),
        name="gla_lat" if seq > 256 else "gla_ctx",
    )(qg, kg, vg, lr, rs, s0, wg, bg, gn)


ROW_SUB = D // LANES


def _store_row_slabs(ref, x, tmp):
    m = x.shape[0]
    for c in range(ROW_SUB):
        tmp[pl.ds(c, m, stride=ROW_SUB), :] = x[:, LANES * c:LANES * (c + 1)]
    ref[...] = tmp[...].reshape(m, ROW_SUB, LANES).astype(BF16)


def _load_row_slabs(ref, tmp):
    m = ref.shape[0]
    tmp[...] = ref[...].astype(F32).reshape(m * ROW_SUB, LANES)
    return jnp.concatenate([tmp[pl.ds(c, m, stride=ROW_SUB), :] for c in range(ROW_SUB)], axis=1)


def _row_slab(ref, row):
    return ref.at[pl.ds(row, 1)]


def _post_kernel(xc_ref, xl_ref, ac_ref, al_ref, gc_ref, gl_ref, mod_ref,
                 wmg_ref, wba_ref, wbg_ref, wo_ref, l1g_ref, l1b_ref, wr_ref, br_ref,
                 x1_ref, h2_ref, rt_ref, ert_ref, cnt_ref, run_s, slab_s, *, n_ctx_tiles):
    i = pl.program_id(0)
    tm = xc_ref.shape[0]
    sub = ROW_GROUP
    n_groups = tm // sub
    is_ctx = i < n_ctx_tiles
    sh1 = mod_ref[0, :, 0:D]
    sc1 = mod_ref[0, :, D:2 * D]
    g1 = mod_ref[0, :, 2 * D:3 * D]
    sh2 = mod_ref[0, :, 3 * D:4 * D]
    sc2 = mod_ref[0, :, 4 * D:5 * D]

    parts = [slice(g * sub, (g + 1) * sub) for g in range(n_groups)]
    xs = [jnp.where(is_ctx, xc_ref[r, :], xl_ref[r, :]) for r in parts]
    hs = [(_ln(x) * (1.0 + sc1) + sh1).astype(BF16) for x in xs]
    gates = [jax.nn.sigmoid(_dot(h, wmg_ref[...])) for h in hs]
    ba = [_dot(jnp.where(is_ctx, ac_ref[r, :], al_ref[r, :]), wba_ref[...]) for r in parts]
    bg = [_dot(jnp.where(is_ctx, gc_ref[r, :], gl_ref[r, :]), wbg_ref[...]) for r in parts]
    merged = [(g[:, :D] * a + g[:, D:] * b).astype(BF16) for g, a, b in zip(gates, ba, bg)]
    mix = [_dot(m, wo_ref[...]) for m in merged]
    x1s = [_ln(ALPHA * x + g1 * m) * l1g_ref[...] + l1b_ref[...] for x, m in zip(xs, mix)]
    h2s = [_ln(x1) * (1.0 + sc2) + sh2 for x1 in x1s]
    logits = [_dot(h2.astype(BF16), wr_ref[...]) + br_ref[...] for h2 in h2s]
    for g, r in enumerate(parts):
        x1_ref[r, :] = x1s[g]
        _store_row_slabs(h2_ref.at[pl.ds(g * sub, sub)], h2s[g], slab_s)

    @pl.when(i == 0)
    def _():
        run_s[...] = jnp.zeros_like(run_s)

    ri = lax.broadcasted_iota(jnp.int32, (sub, sub), 0)
    ci = lax.broadcasted_iota(jnp.int32, (sub, sub), 1)
    earlier = (ri > ci).astype(BF16)
    run = run_s[0:1, :]
    for g, r in enumerate(parts):
        run = _route(logits[g], earlier, run, rt_ref.at[r, :], ert_ref.at[:, r])
    run_s[0:1, :] = run
    cnt_ref[...] = jnp.broadcast_to(run, cnt_ref.shape)


def _route(logit, earlier, run, rt_ref, ert_ref):
    tm = logit.shape[0]
    lane_i = lax.broadcasted_iota(jnp.int32, (tm, LANES), 1)
    lane = lane_i.astype(F32)
    lane_grp = ((lane_i - N_GROUPS) >> 3).astype(F32)
    neg = jnp.float32(-jnp.inf)
    far = jnp.float32(LANES)
    is_g = lane_i < N_GROUPS
    lg = jnp.where(is_g, logit, neg)
    mg = jnp.max(lg, axis=-1, keepdims=True)
    pg_top = 1.0 / jnp.sum(jnp.where(is_g, jnp.exp(logit - mg), 0.0), axis=-1, keepdims=True)
    g_idx = jnp.min(jnp.where(lg == mg, lane, far), axis=-1, keepdims=True)
    in_grp = (lane_i >= N_GROUPS) & (lane_i < N_GROUPS + N_EXP) & (lane_grp == g_idx)
    le = jnp.where(in_grp, logit, neg)
    v1 = jnp.max(le, axis=-1, keepdims=True)
    i1 = jnp.min(jnp.where(le == v1, lane, far), axis=-1, keepdims=True)
    le2 = jnp.where(lane == i1, neg, le)
    v2 = jnp.max(le2, axis=-1, keepdims=True)
    i2 = jnp.min(jnp.where(le2 == v2, lane, far), axis=-1, keepdims=True)
    e1 = i1 - N_GROUPS
    e2 = i2 - N_GROUPS
    tt = jnp.exp(v2 - v1)
    w1 = pg_top / (1.0 + tt)
    w2 = pg_top * tt / (1.0 + tt)

    hot = ((lane == e1) | (lane == e2)).astype(F32)
    before = _dot(earlier, hot.astype(BF16)) + run
    r1 = jnp.sum(jnp.where(lane == e1, before, 0.0), axis=-1, keepdims=True)
    r2 = jnp.sum(jnp.where(lane == e2, before, 0.0), axis=-1, keepdims=True)

    rt = jnp.where(lane_i == 0, e1, 0.0)
    rt = jnp.where(lane_i == 1, e2, rt)
    rt = jnp.where(lane_i == 2, w1, rt)
    rt = jnp.where(lane_i == 3, w2, rt)
    rt = jnp.where(lane_i == 4, r1, rt)
    rt = jnp.where(lane_i == 5, r2, rt)
    rt_ref[...] = rt
    ert_ref[...] = rt.T[0:8, :]
    return run + jnp.sum(hot, axis=0, keepdims=True)


def _post(x_ctx, x_lat, a_ctx, a_lat, g_ctx, g_lat, mod_all, seq_lat,
          w_mg, w_ba, w_bg, w_o, l1g, l1b, w_r, b_r):
    t_ctx, t_lat = x_ctx.shape[0], x_lat.shape[0]
    tm = TM_TOK
    nct, nlt = t_ctx // tm, t_lat // tm
    per_seq = seq_lat // tm
    nb_lat = t_lat // seq_lat
    t = t_ctx + t_lat
    cmap = lambda i: (jnp.minimum(i, nct - 1), 0)
    lmap = lambda i: (jnp.maximum(i - nct, 0), 0)
    mmap = lambda i: (jnp.where(i < nct, nb_lat, jnp.maximum(i - nct, 0) // per_seq), 0, 0)
    row = lambda i: (i, 0)
    const = lambda i: (0, 0)
    return pl.pallas_call(
        functools.partial(_post_kernel, n_ctx_tiles=nct),
        grid=(nct + nlt,),
        in_specs=[pl.BlockSpec((tm, D), cmap), pl.BlockSpec((tm, D), lmap),
                  pl.BlockSpec((tm, AW), cmap), pl.BlockSpec((tm, AW), lmap),
                  pl.BlockSpec((tm, GVW), cmap), pl.BlockSpec((tm, GVW), lmap),
                  pl.BlockSpec((1, 1, 6 * D), mmap),
                  pl.BlockSpec((D, MG_WIDTH), const), pl.BlockSpec((AW, D), const),
                  pl.BlockSpec((GVW, D), const), pl.BlockSpec((D, D), const),
                  pl.BlockSpec((1, D), const), pl.BlockSpec((1, D), const),
                  pl.BlockSpec((D, LANES), const), pl.BlockSpec((1, LANES), const)],
        out_specs=[pl.BlockSpec((tm, D), row), pl.BlockSpec((tm, ROW_SUB, LANES), lambda i: (i, 0, 0)),
                   pl.BlockSpec((tm, LANES), row), pl.BlockSpec((8, tm), lambda i: (0, i)),
                   pl.BlockSpec((8, LANES), const)],
        out_shape=[jax.ShapeDtypeStruct((t, D), F32), jax.ShapeDtypeStruct((t, ROW_SUB, LANES), BF16),
                   jax.ShapeDtypeStruct((t, LANES), F32), jax.ShapeDtypeStruct((8, t), F32),
                   jax.ShapeDtypeStruct((8, LANES), F32)],
        scratch_shapes=[pltpu.VMEM((8, LANES), F32), pltpu.VMEM((ROW_GROUP * ROW_SUB, LANES), F32)],
        compiler_params=_cparams(("arbitrary",)),
        name="post",
    )(x_ctx, x_lat, a_ctx, a_lat, g_ctx, g_lat, mod_all, w_mg, w_ba, w_bg, w_o, l1g, l1b, w_r, b_r)


ROW_UNROLL = 8


def _row_copy(src_ref, dst_ref, sem):
    return pltpu.make_async_copy(src_ref, dst_ref, sem)


def _scatter_kernel(pos0_ref, pos1_ref, h_ref, xs_ref, sem, zero_s, zero_sem):
    ts = h_ref.shape[0]
    n_rows = xs_ref.shape[0] - TM_EXP

    @pl.when(pl.program_id(0) == 0)
    def _():
        zero_s[...] = jnp.zeros_like(zero_s)
        pad = _row_copy(zero_s, xs_ref.at[pl.ds(n_rows, TM_EXP)], zero_sem)
        pad.start()
        pad.wait()

    def issue(g, carry):
        r0 = pl.multiple_of(g * ROW_UNROLL, ROW_UNROLL)
        for k in range(ROW_UNROLL):
            src = _row_slab(h_ref, r0 + k)
            _row_copy(src, _row_slab(xs_ref, pos0_ref[0, r0 + k]), sem).start(priority=0)
            _row_copy(src, _row_slab(xs_ref, pos1_ref[0, r0 + k]), sem).start(priority=1)
        return carry

    lax.fori_loop(0, ts // ROW_UNROLL, issue, 0)
    for _ in range(2):
        _row_copy(h_ref, xs_ref.at[pl.ds(0, ts)], sem).wait()


def _scatter_rows(h2p, pos0, pos1):
    t = h2p.shape[0]
    ts = TS_ROWS
    smem = lambda: pl.BlockSpec((None, 1, ts), lambda i: (i, 0, 0), memory_space=pltpu.SMEM)
    return pl.pallas_call(
        _scatter_kernel,
        grid=(t // ts,),
        in_specs=[smem(), smem(), pl.BlockSpec((ts, ROW_SUB, LANES), lambda i: (i, 0, 0))],
        out_specs=pl.BlockSpec(memory_space=pl.ANY),
        out_shape=jax.ShapeDtypeStruct((2 * t + TM_EXP, ROW_SUB, LANES), BF16),
        scratch_shapes=[pltpu.SemaphoreType.DMA(()), pltpu.VMEM((TM_EXP, ROW_SUB, LANES), BF16),
                        pltpu.SemaphoreType.DMA(())],
        compiler_params=_cparams(("arbitrary",)),
        name="scatter",
    )(pos0.reshape(t // ts, 1, ts), pos1.reshape(t // ts, 1, ts), h2p)


def _expert_kernel(start_ref, nwin_ref, xs_ref, wg_ref, wu_ref, wd_ref, ys_ref,
                   wgu_s, wd_s, in_buf, out_buf, slab_s, in_sem, out_sem):
    e = pl.program_id(0)
    tm = TM_EXP
    wgu_s[:, 0:D_EXP] = wg_ref[...].astype(BF16)
    wgu_s[:, D_EXP:2 * D_EXP] = wu_ref[...].astype(BF16)
    wd_s[...] = wd_ref[...].astype(BF16)
    base = start_ref[e]
    n = nwin_ref[e]

    @pl.when(e == 0)
    def _():
        out_buf[0] = jnp.zeros(out_buf.shape[1:], BF16)
        pad = pltpu.make_async_copy(out_buf.at[0], ys_ref.at[pl.ds(ys_ref.shape[0] - tm, tm)], out_sem.at[0])
        pad.start()
        pad.wait()

    def read(w, slot):
        return pltpu.make_async_copy(xs_ref.at[pl.ds(base + w * tm, tm)], in_buf.at[slot], in_sem.at[slot])

    def write(w, slot):
        return pltpu.make_async_copy(out_buf.at[slot], ys_ref.at[pl.ds(base + w * tm, tm)], out_sem.at[slot])

    @pl.when(n > 0)
    def _():
        read(0, 0).start()

    def body(w, carry):
        slot = w % 2

        @pl.when(w + 1 < n)
        def _():
            read(w + 1, 1 - slot).start()

        read(w, slot).wait()

        @pl.when(w >= 2)
        def _():
            write(w - 2, slot).wait()

        gu = _dot(_load_row_slabs(in_buf.at[slot], slab_s).astype(BF16), wgu_s[...])
        hid = _silu(gu[:, 0:D_EXP]) * gu[:, D_EXP:2 * D_EXP]
        _store_row_slabs(out_buf.at[slot], _dot(hid.astype(BF16), wd_s[...]), slab_s)
        write(w, slot).start()
        return carry

    lax.fori_loop(0, n, body, 0)

    @pl.when(n >= 1)
    def _():
        write(n - 1, (n - 1) % 2).wait()

    @pl.when(n >= 2)
    def _():
        write(n - 2, n % 2).wait()


def _experts(xs, starts, n_win, w_gate, w_up, w_down):
    tm = TM_EXP
    wmap = lambda e, st, nw: (e, 0, 0)
    slab = (tm, ROW_SUB, LANES)
    return pl.pallas_call(
        _expert_kernel,
        grid_spec=pltpu.PrefetchScalarGridSpec(
            num_scalar_prefetch=2,
            grid=(N_EXP,),
            in_specs=[pl.BlockSpec(memory_space=pl.ANY),
                      pl.BlockSpec((None, D, D_EXP), wmap), pl.BlockSpec((None, D, D_EXP), wmap),
                      pl.BlockSpec((None, D_EXP, D), wmap)],
            out_specs=pl.BlockSpec(memory_space=pl.ANY),
            scratch_shapes=[pltpu.VMEM((D, 2 * D_EXP), BF16), pltpu.VMEM((D_EXP, D), BF16),
                            pltpu.VMEM((2,) + slab, BF16), pltpu.VMEM((2,) + slab, BF16),
                            pltpu.VMEM((tm * ROW_SUB, LANES), F32),
                            pltpu.SemaphoreType.DMA((2,)), pltpu.SemaphoreType.DMA((2,))]),
        out_shape=jax.ShapeDtypeStruct(xs.shape, BF16),
        compiler_params=_cparams(("arbitrary",)),
        name="experts",
    )(starts, n_win, xs, w_gate, w_up, w_down)


def _final_kernel(p0c_ref, p1c_ref, p0n_ref, p1n_ref, x1_ref, rt_ref, mod_ref, l2g_ref, l2b_ref, ys_ref,
                  oc_ref, ol_ref, buf, sem, slab_s, *, n_ctx_tiles):
    i = pl.program_id(0)
    n = pl.num_programs(0)
    tm = x1_ref.shape[0]

    def gather(p0_ref, p1_ref, slot):
        def issue(g, carry):
            r0 = pl.multiple_of(g * ROW_UNROLL, ROW_UNROLL)
            for k in range(ROW_UNROLL):
                _row_copy(_row_slab(ys_ref, p0_ref[0, r0 + k]),
                          _row_slab(buf.at[slot, 0], r0 + k), sem.at[slot]).start(priority=0)
                _row_copy(_row_slab(ys_ref, p1_ref[0, r0 + k]),
                          _row_slab(buf.at[slot, 1], r0 + k), sem.at[slot]).start(priority=1)
            return carry

        lax.fori_loop(0, tm // ROW_UNROLL, issue, 0)

    cur = i % 2

    @pl.when(i == 0)
    def _():
        gather(p0c_ref, p1c_ref, 0)

    @pl.when(i + 1 < n)
    def _():
        gather(p0n_ref, p1n_ref, 1 - cur)

    for k in range(2):
        _row_copy(ys_ref.at[pl.ds(0, tm)], buf.at[cur, k], sem.at[cur]).wait()

    g2 = mod_ref[0, :, 5 * D:6 * D]
    w1 = rt_ref[:, 2:3]
    w2 = rt_ref[:, 3:4]
    moe = (w1 * _load_row_slabs(buf.at[cur, 0], slab_s.at[0])
           + w2 * _load_row_slabs(buf.at[cur, 1], slab_s.at[1]))
    out = _ln(ALPHA * x1_ref[...] + g2 * moe) * l2g_ref[...] + l2b_ref[...]

    @pl.when(i < n_ctx_tiles)
    def _():
        oc_ref[...] = out

    @pl.when(i >= n_ctx_tiles)
    def _():
        ol_ref[...] = out


def _final(x1, rt, pos0, pos1, mod_all, l2g, l2b, ys, t_ctx, seq_lat):
    t = x1.shape[0]
    tm = TM_TOK
    nt = t // tm
    nct = t_ctx // tm
    t_lat = t - t_ctx
    per_seq = seq_lat // tm
    nb_lat = t_lat // seq_lat
    p0 = pos0.reshape(nt, 1, tm)
    p1 = pos1.reshape(nt, 1, tm)
    row = lambda i: (i, 0)
    const = lambda i: (0, 0)
    mmap = lambda i: (jnp.where(i < nct, nb_lat, jnp.maximum(i - nct, 0) // per_seq), 0, 0)
    smem_cur = lambda: pl.BlockSpec((None, 1, tm), lambda i: (i, 0, 0), memory_space=pltpu.SMEM)
    smem_nxt = lambda: pl.BlockSpec((None, 1, tm), lambda i: (jnp.minimum(i + 1, nt - 1), 0, 0),
                                    memory_space=pltpu.SMEM)
    return pl.pallas_call(
        functools.partial(_final_kernel, n_ctx_tiles=nct),
        grid=(nt,),
        in_specs=[smem_cur(), smem_cur(), smem_nxt(), smem_nxt(),
                  pl.BlockSpec((tm, D), row), pl.BlockSpec((tm, LANES), row),
                  pl.BlockSpec((1, 1, 6 * D), mmap),
                  pl.BlockSpec((1, D), const), pl.BlockSpec((1, D), const),
                  pl.BlockSpec(memory_space=pl.ANY)],
        out_specs=[pl.BlockSpec((tm, D), lambda i: (jnp.minimum(i, nct - 1), 0)),
                   pl.BlockSpec((tm, D), lambda i: (jnp.maximum(i - nct, 0), 0))],
        out_shape=[jax.ShapeDtypeStruct((t_ctx, D), F32), jax.ShapeDtypeStruct((t_lat, D), F32)],
        scratch_shapes=[pltpu.VMEM((2, 2, tm, ROW_SUB, LANES), BF16), pltpu.SemaphoreType.DMA((2,)),
                        pltpu.VMEM((2, tm * ROW_SUB, LANES), F32)],
        compiler_params=_cparams(("arbitrary",)),
        name="final",
    )(p0, p1, p0, p1, x1, rt, mod_all, l2g, l2b, ys)


def _reorder_q_heads(w, axis):
    shape = w.shape
    split = shape[:axis] + (N_KV_HEADS, N_Q_HEADS // N_KV_HEADS, HD) + shape[axis + 1:]
    return jnp.swapaxes(w.reshape(split), axis, axis + 1).reshape(shape)


def _rope_tables(seq):
    t = np.arange(seq)
    half = HD // 4
    inv = (ROPE_THETA ** (-np.arange(half, dtype=np.float64) / half)).astype(np.float32)
    d64 = np.arange(LANES) % HD
    pos = np.where((d64 < HD // 2)[None, :], (t // GRID_W)[:, None], (t % GRID_W)[:, None])
    ang = (pos.astype(np.float32) * inv[d64 % half][None, :]).astype(np.float64)
    sign = np.where((d64 % 32) < 16, -1.0, 1.0)
    return (jnp.asarray(np.cos(ang), F32), jnp.asarray(np.sin(ang) * sign[None, :], F32))


def _pair_states(s):
    b = s.shape[0]
    s = s.reshape(b, 2, 2, GLA_DK, GLA_DV)
    return s.transpose(0, 1, 4, 2, 3).reshape(b, 2, GLA_DV, 2 * GLA_DK)


def _unpair_states(s):
    b = s.shape[0]
    s = s.reshape(b, 2, GLA_DV, 2, GLA_DK)
    return s.transpose(0, 1, 3, 4, 2).reshape(b, GLA_H, GLA_DK, GLA_DV)


def _route_tables(ert, counts):
    i32 = jnp.int32
    cnt = counts[0, :N_EXP].astype(i32)
    starts = jnp.cumsum(cnt) - cnt
    table = lambda e: jnp.sum(jnp.where(e[None, :] == jnp.arange(N_EXP, dtype=i32)[:, None],
                                        starts[:, None], 0), axis=0)
    pos0 = table(ert[0].astype(i32)) + ert[4].astype(i32)
    pos1 = table(ert[1].astype(i32)) + ert[5].astype(i32)
    return pos0, pos1, starts, (cnt + (TM_EXP - 1)) // TM_EXP


def kernel(x_prompt, x_sample, cache_k, cache_v, state_gla_fwd, state_gla_bwd, c, c_ctx, w_ada, b_ada, w_in, q_norm, k_norm, gla_w_gate, gla_b_gate, gla_norm, w_br_attn, w_br_gla, w_out, ln1_g, ln1_b, router_group_w, router_group_b, router_expert_w, router_expert_b, exp_w_gate, exp_w_up, exp_w_down, ln2_g, ln2_b):
    b_ctx, seq_ctx, _ = x_prompt.shape
    b_lat, seq_lat, _ = x_sample.shape
    t_ctx, t_lat = b_ctx * seq_ctx, b_lat * seq_lat
    t = t_ctx + t_lat
    l = 0

    rows = -(-(b_lat + 1) // 8) * 8
    c_rows = jnp.zeros((rows, D), F32).at[:b_lat].set(c).at[b_lat].set(c_ctx)
    mod = _ada(c_rows, w_ada[l], b_ada[l][None, :])
    mod_all = mod[:b_lat + 1, None, :]
    mod_lat = mod_all[:b_lat]
    mod_ctx = mod_all[b_lat:]

    w_full = w_in[l]
    w_a = jnp.concatenate([_reorder_q_heads(w_full[:, :AW], 1), w_full[:, AW:A_WIDTH]], axis=1).astype(BF16)
    w_mg = w_full[:, A_WIDTH:].astype(BF16)
    gain = jnp.concatenate([jnp.tile(q_norm[l], N_Q_HEADS), jnp.tile(k_norm[l], N_KV_HEADS)])[None, :]
    head_of = np.arange(AW + KVW) // HD
    ind = jnp.asarray((head_of[:, None] == np.arange(LANES)[None, :]) / HD, BF16)
    w_ba = _reorder_q_heads(w_br_attn[l], 0).astype(BF16)
    w_bg = w_br_gla[l].astype(BF16)
    w_o = w_out[l].astype(BF16)
    w_r = jnp.zeros((D, LANES), F32).at[:, :N_GROUPS].set(router_group_w[l])
    w_r = w_r.at[:, N_GROUPS:N_GROUPS + N_EXP].set(router_expert_w[l]).astype(BF16)
    b_r = jnp.zeros((1, LANES), F32).at[0, :N_GROUPS].set(router_group_b[l])
    b_r = b_r.at[0, N_GROUPS:N_GROUPS + N_EXP].set(router_expert_b[l])
    wg = jnp.zeros((2, 2 * GATE_RANK, GKW), F32)
    wg = wg.at[0, :GATE_RANK].set(gla_w_gate[l, 0]).at[1, GATE_RANK:].set(gla_w_gate[l, 1])
    bg = gla_b_gate[l][:, None, :]
    gn = gla_norm[l][None, :]

    xc = x_prompt.reshape(t_ctx, D)
    xl = x_sample.reshape(t_lat, D)

    q_c, k_c, v_c, qg_c, kg_c, vg_c, rs_c, lr_c, kf_c, vf_c = _inproj(
        xc, mod_ctx, w_a, gain, ind, None, seq_ctx, latent=False)
    attn_c = _attention(q_c, k_c, v_c, None, seq_ctx)
    zero_state = jnp.zeros((1, 2, 2, LANES, LANES), F32)
    gla_c, sfin_c = _gla(qg_c, kg_c, vg_c, lr_c, rs_c, zero_state, wg, bg, gn, seq_ctx)

    q_l, k_l, v_l, qg_l, kg_l, vg_l, rs_l, lr_l = _inproj(
        xl, mod_lat, w_a, gain, ind, _rope_tables(seq_lat), seq_lat, latent=True)
    past = cache_k.shape[2]
    kc = cache_k[:, l].reshape(b_lat, past, KVW).astype(BF16)
    vc = cache_v[:, l].reshape(b_lat, past, KVW).astype(BF16)
    attn_l = _attention(q_l, k_l, v_l, (kc, vc), seq_lat)
    s0 = jnp.stack([_pair_states(state_gla_fwd[:, l]), _pair_states(state_gla_bwd[:, l])], axis=1)
    gla_l, _ = _gla(qg_l, kg_l, vg_l, lr_l, rs_l, s0, wg, bg, gn, seq_lat)

    x1, h2p, rt, ert, counts = _post(xc, xl, attn_c, attn_l, gla_c, gla_l, mod_all, seq_lat,
                                w_mg, w_ba, w_bg, w_o, ln1_g[l][None, :], ln1_b[l][None, :], w_r, b_r)

    pos0, pos1, starts, n_win = _route_tables(ert, counts)
    xs = _scatter_rows(h2p, pos0, pos1)
    ys = _experts(xs, starts, n_win, exp_w_gate[l], exp_w_up[l], exp_w_down[l])
    y_ctx, y_lat = _final(x1, rt, pos0, pos1, mod_all, ln2_g[l][None, :], ln2_b[l][None, :], ys,
                          t_ctx, seq_lat)

    untranspose = lambda a: a.reshape(b_ctx, 1, N_KV_HEADS, HD, seq_ctx).transpose(0, 1, 4, 2, 3)
    new_k, new_v = untranspose(kf_c), untranspose(vf_c)
    new_sf = _unpair_states(sfin_c[:, 0])[:, None]
    new_sb = _unpair_states(sfin_c[:, 1])[:, None]
    return (y_ctx.reshape(b_ctx, seq_ctx, D), y_lat.reshape(b_lat, seq_lat, D),
            new_k, new_v, new_sf, new_sb)
```

```python
import functools

import numpy as np
import jax
import jax.numpy as jnp
from jax import lax
from jax.experimental import pallas as pl
from jax.experimental.pallas import tpu as pltpu

F32 = jnp.float32
BF16 = jnp.bfloat16
HIGHEST = lax.Precision.HIGHEST

D = 1024
GRID_W = 64
HD = 64
N_Q_HEADS = 8
N_KV_HEADS = 2
AW = N_Q_HEADS * HD
KVW = N_KV_HEADS * HD
ROPE_THETA = 10000.0
GLA_H = 4
GLA_DK = 64
GLA_DV = 128
GKW = GLA_H * GLA_DK
GVW = GLA_H * GLA_DV
GATE_RANK = 16
GLA_TAU = 16.0
CHUNK = 64
N_GROUPS = 4
EPG = 8
N_EXP = N_GROUPS * EPG
D_EXP = 256
DEPTH = 1
ALPHA = (2.0 * DEPTH) ** 0.25
LN_EPS = 1e-6
RMS_EPS = 1e-6

LANES = 128
A_WIDTH = AW + 2 * KVW + 2 * GKW + 2 * GVW + 2 * GATE_RANK
MG_WIDTH = 2 * D
TM_TOK = 512
ROW_GROUP = 128
LOG2E = 1.4426950408889634
ONES_ROWS = 16
TQ_LAT = 256
TM_EXP = 256
TS_ROWS = 256
VMEM_LIMIT = 56 * 1024 * 1024


def _cparams(sem):
    return pltpu.CompilerParams(dimension_semantics=sem, vmem_limit_bytes=VMEM_LIMIT)


def _dot(a, b):
    return jnp.dot(a, b, preferred_element_type=F32)


def _dot_nt(a, b):
    return lax.dot_general(a, b, (((1,), (1,)), ((), ())), preferred_element_type=F32)


def _dot_tn(a, b):
    return lax.dot_general(a, b, (((0,), (0,)), ((), ())), preferred_element_type=F32)


def _ln(x):
    mu = jnp.mean(x, axis=-1, keepdims=True)
    xc = x - mu
    var = jnp.mean(xc * xc, axis=-1, keepdims=True)
    return xc * lax.rsqrt(var + LN_EPS)


def _silu(x):
    return x * jax.nn.sigmoid(x)


def _split_bf16(x):
    hi = x.astype(BF16)
    lo = (x - hi.astype(F32)).astype(BF16)
    return hi, lo


def _ada_kernel(c_ref, w_ref, b_ref, o_ref):
    s = _silu(c_ref[...])
    o_ref[...] = jnp.dot(s, w_ref[...], preferred_element_type=F32, precision=HIGHEST) + b_ref[...]


def _ada(c_rows, w_ada, b_ada):
    rows = c_rows.shape[0]
    n = w_ada.shape[1]
    bn = 1024
    return pl.pallas_call(
        _ada_kernel,
        grid=(n // bn,),
        in_specs=[pl.BlockSpec((rows, D), lambda j: (0, 0)),
                  pl.BlockSpec((D, bn), lambda j: (0, j)),
                  pl.BlockSpec((1, bn), lambda j: (0, j))],
        out_specs=pl.BlockSpec((rows, bn), lambda j: (0, j)),
        out_shape=jax.ShapeDtypeStruct((rows, n), F32),
        compiler_params=_cparams(("arbitrary",)),
        name="ada",
    )(c_rows, w_ada, b_ada)


def _inproj_kernel(*refs, latent, seq):
    if latent:
        (x_ref, mod_ref, w_ref, gain_ref, ind_ref, cos_ref, sin_ref,
         q_ref, k_ref, v_ref, qg_ref, kg_ref, vg_ref, rs_ref, lr_ref) = refs
    else:
        (x_ref, mod_ref, w_ref, gain_ref, ind_ref,
         q_ref, k_ref, v_ref, qg_ref, kg_ref, vg_ref, rs_ref, lr_ref, kf_ref, vf_ref) = refs
    tm = x_ref.shape[0]
    sub = ROW_GROUP
    n_groups = tm // sub
    sh1 = mod_ref[0, :, 0:D]
    sc1 = mod_ref[0, :, D:2 * D]
    lane = lax.broadcasted_iota(jnp.int32, (sub, LANES), 1)
    low_half = lane < HD
    first = (lane % 32) < 16

    def project(g):
        rows = slice(g * sub, (g + 1) * sub)
        h = (_ln(x_ref[rows, :]) * (1.0 + sc1) + sh1).astype(BF16)
        return _dot(h, w_ref[...])

    def finish(g, res):
        rows = slice(g * sub, (g + 1) * sub)
        qk = res[:, 0:AW + KVW]
        hi, lo = _split_bf16(qk * qk)
        ms = _dot(hi, ind_ref[...]) + _dot(lo, ind_ref[...])
        r = lax.rsqrt(ms + RMS_EPS)
        for s in range(5):
            rb = jnp.where(low_half, r[:, 2 * s:2 * s + 1], r[:, 2 * s + 1:2 * s + 2])
            y = res[:, LANES * s:LANES * (s + 1)] * rb * gain_ref[:, LANES * s:LANES * (s + 1)]
            if s == 4 and not latent:
                kf_ref[(g * sub) // seq, :, (g * sub) % seq:(g * sub) % seq + sub] = y.T
            if latent:
                partner = jnp.where(first, pltpu.roll(y, LANES - 16, 1), pltpu.roll(y, 16, 1))
                y = y * cos_ref[rows, :] + partner * sin_ref[rows, :]
            if s < 4:
                q_ref[rows, LANES * s:LANES * (s + 1)] = (y * (HD ** -0.5 * LOG2E)).astype(BF16)
            else:
                k_ref[rows, :] = y.astype(BF16)
        o = AW + KVW
        v = res[:, o:o + KVW]
        v_ref[rows, :] = v.astype(BF16)
        if not latent:
            vf_ref[(g * sub) // seq, :, (g * sub) % seq:(g * sub) % seq + sub] = v.T
        o += KVW
        qg_ref[rows, :] = res[:, o:o + GKW] * (GLA_DK ** -0.5)
        o += GKW
        kg_ref[rows, :] = res[:, o:o + GKW]
        o += GKW
        vg_ref[rows, :] = res[:, o:o + GVW].astype(BF16)
        o += GVW
        rs_ref[rows, :] = _silu(res[:, o:o + GVW]).astype(BF16)
        o += GVW
        lr_ref[rows, :] = res[:, o:o + 2 * GATE_RANK]

    pending = {0: project(0)}
    for g in range(n_groups):
        if g + 1 < n_groups:
            pending[g + 1] = project(g + 1)
        finish(g, pending.pop(g))


def _inproj(x2, mod, w_a, gain, ind, rope, seq, latent):
    t = x2.shape[0]
    tm = TM_TOK
    per_seq = max(seq // tm, 1)
    per_tile = max(tm // seq, 1)
    row = lambda i: (i, 0)
    const = lambda i: (0, 0)
    in_specs = [pl.BlockSpec((tm, D), row),
                pl.BlockSpec((1, 1, 6 * D), (lambda i: (i // per_seq, 0, 0)) if latent else (lambda i: (0, 0, 0))),
                pl.BlockSpec((D, A_WIDTH), const),
                pl.BlockSpec((1, AW + KVW), const),
                pl.BlockSpec((AW + KVW, LANES), const)]
    args = [x2, mod, w_a, gain, ind]
    if latent:
        in_specs += [pl.BlockSpec((tm, LANES), lambda i: (i % per_seq, 0))] * 2
        args += list(rope)
    widths = [(AW, BF16), (KVW, BF16), (KVW, BF16), (GKW, F32), (GKW, F32), (GVW, BF16), (GVW, BF16),
              (2 * GATE_RANK, F32)]
    out_specs = [pl.BlockSpec((tm, w), row) for w, _ in widths]
    out_shape = [jax.ShapeDtypeStruct((t, w), dt) for w, dt in widths]
    if not latent:
        cache_spec = pl.BlockSpec((per_tile, KVW, seq), lambda i: (i, 0, 0))
        out_specs += [cache_spec] * 2
        out_shape += [jax.ShapeDtypeStruct((t // seq, KVW, seq), F32)] * 2
    return pl.pallas_call(
        functools.partial(_inproj_kernel, latent=latent, seq=seq),
        grid=(t // tm,),
        in_specs=in_specs,
        out_specs=out_specs,
        out_shape=out_shape,
        compiler_params=_cparams(("parallel",)),
        name="inproj_lat" if latent else "inproj_ctx",
    )(*args)


def _attn_kernel(*refs, has_cache):
    def transposed_with_ones(dst, src):
        dst[0:KVW, :] = src[...].astype(F32).T.astype(BF16)
        dst[KVW:, :] = jnp.ones((ONES_ROWS, dst.shape[1]), BF16)

    if has_cache:
        q_ref, k_ref, v_ref, kc_ref, vc_ref, o_ref, vt_s, vct_s = refs

        @pl.when(pl.program_id(1) == 0)
        def _():
            transposed_with_ones(vt_s, v_ref)
            transposed_with_ones(vct_s, vc_ref)
    else:
        q_ref, k_ref, v_ref, o_ref, vt_s = refs
        transposed_with_ones(vt_s, v_ref)
    tq = q_ref.shape[0]

    lane = lax.broadcasted_iota(jnp.int32, (tq, LANES), 1)
    low_half = lane < HD
    k = k_ref[...]
    scores = []
    for j in range(N_KV_HEADS):
        keep = low_half if j == 0 else jnp.logical_not(low_half)
        zero = jnp.zeros((tq, LANES), BF16)
        for pair in range(2):
            qs = jnp.concatenate([jnp.where(keep, q_ref[:, LANES * s:LANES * (s + 1)], zero)
                                  for s in (2 * pair, 2 * pair + 1)], axis=0)
            s1 = _dot_nt(k, qs)
            s2 = _dot_nt(kc_ref[...], qs) if has_cache else None
            scores.append((s1, s2))
    outs = []
    for s1, s2 in scores:
        m = jnp.max(s1, axis=0, keepdims=True)
        if has_cache:
            m = jnp.maximum(m, jnp.max(s2, axis=0, keepdims=True))
        acc = _dot(vt_s[...], jnp.exp2(s1 - m).astype(BF16))
        if has_cache:
            acc = acc + _dot(vct_s[...], jnp.exp2(s2 - m).astype(BF16))
        outs.append(acc[0:KVW] / acc[KVW:KVW + 1])
    head0 = jnp.concatenate(outs[0:2], axis=1)
    head1 = jnp.concatenate(outs[2:4], axis=1)
    row = lax.broadcasted_iota(jnp.int32, (LANES, 4 * tq), 0)
    out = jnp.where(row < HD, head0, head1).T
    for s in range(4):
        o_ref[:, LANES * s:LANES * (s + 1)] = out[s * tq:(s + 1) * tq].astype(BF16)


def _attention(q, k, v, cache, seq):
    t = q.shape[0]
    if cache is None:
        tq = seq
        grid = (t // seq,)
        qmap = lambda b: (b, 0)
        in_specs = [pl.BlockSpec((tq, AW), qmap), pl.BlockSpec((seq, KVW), qmap),
                    pl.BlockSpec((seq, KVW), qmap)]
        args = [q, k, v]
        scratch = [pltpu.VMEM((KVW + ONES_ROWS, seq), BF16)]
        sem = ("parallel",)
        name = "attn_ctx"
    else:
        tq = TQ_LAT
        nq = seq // tq
        kc, vc = cache
        past = kc.shape[1]
        grid = (t // seq, nq)
        qmap = lambda b, i: (b * nq + i, 0)
        kmap = lambda b, i: (b, 0)
        cmap = lambda b, i: (b, 0, 0)
        in_specs = [pl.BlockSpec((tq, AW), qmap), pl.BlockSpec((seq, KVW), kmap),
                    pl.BlockSpec((seq, KVW), kmap),
                    pl.BlockSpec((None, past, KVW), cmap), pl.BlockSpec((None, past, KVW), cmap)]
        args = [q, k, v, kc, vc]
        scratch = [pltpu.VMEM((KVW + ONES_ROWS, seq), BF16), pltpu.VMEM((KVW + ONES_ROWS, past), BF16)]
        sem = ("parallel", "arbitrary")
        name = "attn_lat"
    return pl.pallas_call(
        functools.partial(_attn_kernel, has_cache=cache is not None),
        grid=grid,
        in_specs=in_specs,
        out_specs=pl.BlockSpec((tq, AW), qmap),
        out_shape=jax.ShapeDtypeStruct((t, AW), BF16),
        scratch_shapes=scratch,
        compiler_params=_cparams(sem),
        name=name,
    )(*args)


GLA_BLK = 256
GLA_UNROLL = 2


def _split3_bf16(x):
    hi = x.astype(BF16)
    r1 = x - hi.astype(F32)
    mid = r1.astype(BF16)
    lo = (r1 - mid.astype(F32)).astype(BF16)
    return hi, mid, lo


def _gla_kernel(qg_ref, kg_ref, vg_ref, lr_ref, rs_ref, s0_ref, wg_ref, bg_ref, gn_ref,
                o_ref, sfin_ref, cum_s, kv_s, dec_s):
    n = qg_ref.shape[0]
    nc = n // CHUNK
    lane = lax.broadcasted_iota(jnp.int32, (CHUNK, LANES), 1)
    low_half = lane < GLA_DK
    lane_sq = lax.broadcasted_iota(jnp.int32, (LANES, LANES), 1)
    low_half_sq = lane_sq < GLA_DK
    ri = lax.broadcasted_iota(jnp.int32, (2 * CHUNK, 2 * CHUNK), 0)
    ci = lax.broadcasted_iota(jnp.int32, (2 * CHUNK, 2 * CHUNK), 1)
    diag = (ri >> 6) == (ci >> 6)
    keep2 = (diag & (ri >= ci), diag & (ci >= ri))

    rb = lax.broadcasted_iota(jnp.int32, (GLA_BLK, GLA_BLK), 0)
    cb = lax.broadcasted_iota(jnp.int32, (GLA_BLK, GLA_BLK), 1)
    same = (rb >> 6) == (cb >> 6)
    tri = ((same & (rb >= cb)).astype(BF16), (same & (cb >= rb)).astype(BF16))
    for d in range(2):
        w_hi, w_lo = _split_bf16(wg_ref[d])
        for blk in range(n // GLA_BLK):
            rows = slice(blk * GLA_BLK, (blk + 1) * GLA_BLK)
            l_hi, l_lo = _split_bf16(lr_ref[rows, :])
            z = _dot(l_hi, w_hi) + _dot(l_lo, w_hi) + _dot(l_hi, w_lo) + bg_ref[d]
            logg = (jnp.minimum(z, 0.0) - jnp.log(1.0 + jnp.exp(-jnp.abs(z)))) * (1.0 / GLA_TAU)
            pieces = _split3_bf16(logg)
            cum_s[d, rows, :] = _dot(tri[d], pieces[0]) + _dot(tri[d], pieces[1]) + _dot(tri[d], pieces[2])

    def pass1(i, carry):
        cs = [i * GLA_UNROLL + u for u in range(GLA_UNROLL)]
        rws = [pl.ds(pl.multiple_of(c * CHUNK, CHUNK), CHUNK) for c in cs]
        prods = []
        for c, rows in zip(cs, rws):
            kc = kg_ref[rows, :]
            kdec = []
            for d in range(2):
                cum = cum_s[d, rows, :]
                last = cum[CHUNK - 1:CHUNK, :] if d == 0 else cum[0:1, :]
                dec_s[d, c] = jnp.exp(last)
                kdec.append((kc * jnp.exp(last - cum)).astype(BF16))
            for p in range(2):
                vpair = vg_ref[rows, GLA_DV * 2 * p:GLA_DV * 2 * (p + 1)]
                kpair = jnp.concatenate([kdec[0][:, LANES * p:LANES * (p + 1)],
                                         kdec[1][:, LANES * p:LANES * (p + 1)]], axis=1)
                prods.append(_dot_tn(vpair, kpair))
        for j, c in enumerate(cs):
            for p in range(2):
                res = prods[2 * j + p]
                for d in range(2):
                    cols = slice(LANES * d, LANES * (d + 1))
                    kv_s[d, c, p] = jnp.where(low_half_sq, res[0:GLA_DV, cols], res[GLA_DV:2 * GLA_DV, cols])
        return carry

    lax.fori_loop(0, nc // GLA_UNROLL, pass1, 0)

    for d in range(2):
        def scan(i, st):
            c = i if d == 0 else nc - 1 - i
            dec = dec_s[d, c]
            new = []
            for p in range(2):
                kv = kv_s[d, c, p]
                kv_s[d, c, p] = st[p]
                new.append(st[p] * dec[:, LANES * p:LANES * (p + 1)] + kv)
            return tuple(new)

        fin = lax.fori_loop(0, nc, scan, (s0_ref[d, 0], s0_ref[d, 1]))
        sfin_ref[d, 0] = fin[0]
        sfin_ref[d, 1] = fin[1]

    def pass3(i, carry):
        cs = [i * GLA_UNROLL + u for u in range(GLA_UNROLL)]
        rws = [pl.ds(pl.multiple_of(c * CHUNK, CHUNK), CHUNK) for c in cs]
        first = []
        for c, rows in zip(cs, rws):
            q = qg_ref[rows, :]
            k = kg_ref[rows, :]
            for d in range(2):
                cum = cum_s[d, rows, :]
                qt = q * jnp.exp(cum)
                kt = (k * jnp.exp(-cum)).astype(BF16)
                for p in range(2):
                    qs = qt[:, LANES * p:LANES * (p + 1)]
                    lhs = jnp.concatenate([jnp.where(low_half, qs, 0.0), jnp.where(low_half, 0.0, qs)],
                                          axis=0).astype(BF16)
                    kts = kt[:, LANES * p:LANES * (p + 1)]
                    rhs = jnp.concatenate([kts, kts, kv_s[d, c, p].astype(BF16)], axis=0)
                    first.append(_dot_nt(lhs, rhs))
        second = []
        for j, rows in enumerate(rws):
            for d in range(2):
                for p in range(2):
                    res = first[4 * j + 2 * d + p]
                    vp = jnp.concatenate([vg_ref[rows, GLA_DV * (2 * p):GLA_DV * (2 * p + 1)],
                                          vg_ref[rows, GLA_DV * (2 * p + 1):GLA_DV * (2 * p + 2)]], axis=0)
                    a = jnp.where(keep2[d], res[:, 0:2 * CHUNK], 0.0).astype(BF16)
                    second.append(_dot(a, vp) + res[:, 2 * CHUNK:])
        for j, rows in enumerate(rws):
            for p in range(2):
                tot = second[4 * j + p] + second[4 * j + 2 + p]
                y = tot * lax.rsqrt(jnp.mean(tot * tot, axis=-1, keepdims=True) + RMS_EPS) * gn_ref[...]
                for hh in range(2):
                    cols = slice(GLA_DV * (2 * p + hh), GLA_DV * (2 * p + hh + 1))
                    o_ref[rows, cols] = (y[CHUNK * hh:CHUNK * (hh + 1)]
                                         * rs_ref[rows, cols].astype(F32)).astype(BF16)
        return carry

    lax.fori_loop(0, nc // GLA_UNROLL, pass3, 0)


def _gla(qg, kg, vg, lr, rs, s0, wg, bg, gn, seq):
    t = qg.shape[0]
    nb = t // seq
    nc = seq // CHUNK
    row = lambda b: (b, 0)
    c3 = lambda b: (0, 0, 0)
    st = lambda b: (b, 0, 0, 0, 0)
    if s0.shape[0] == 1:
        s0map = lambda b: (0, 0, 0, 0, 0)
    else:
        s0map = st
    return pl.pallas_call(
        _gla_kernel,
        grid=(nb,),
        in_specs=[pl.BlockSpec((seq, GKW), row), pl.BlockSpec((seq, GKW), row),
                  pl.BlockSpec((seq, GVW), row), pl.BlockSpec((seq, 2 * GATE_RANK), row),
                  pl.BlockSpec((seq, GVW), row),
                  pl.BlockSpec((None, 2, 2, LANES, LANES), s0map),
                  pl.BlockSpec((2, 2 * GATE_RANK, GKW), c3), pl.BlockSpec((2, 1, GKW), c3),
                  pl.BlockSpec((1, GLA_DV), lambda b: (0, 0))],
        out_specs=[pl.BlockSpec((seq, GVW), row),
                   pl.BlockSpec((None, 2, 2, LANES, LANES), st)],
        out_shape=[jax.ShapeDtypeStruct((t, GVW), BF16),
                   jax.ShapeDtypeStruct((nb, 2, 2, LANES, LANES), F32)],
        scratch_shapes=[pltpu.VMEM((2, seq, GKW), F32),
                        pltpu.VMEM((2, nc, 2, LANES, LANES), F32),
                        pltpu.VMEM((2, nc, 1, GKW), F32)],
        compiler_params=_cparams(("parallel",)),
        name="gla_lat" if seq > 256 else "gla_ctx",
    )(qg, kg, vg, lr, rs, s0, wg, bg, gn)


ROW_SUB = D // LANES


def _store_row_slabs(ref, x, tmp):
    m = x.shape[0]
    for c in range(ROW_SUB):
        tmp[pl.ds(c, m, stride=ROW_SUB), :] = x[:, LANES * c:LANES * (c + 1)]
    ref[...] = tmp[...].astype(BF16)


def _load_row_slabs(ref, tmp):
    m = ref.shape[0] // ROW_SUB
    tmp[...] = ref[...].astype(F32)
    return jnp.concatenate([tmp[pl.ds(c, m, stride=ROW_SUB), :] for c in range(ROW_SUB)], axis=1)


def _row_slab(ref, row):
    return ref.at[pl.ds(pl.multiple_of(row * ROW_SUB, ROW_SUB), ROW_SUB), :]


def _post_kernel(xc_ref, xl_ref, ac_ref, al_ref, gc_ref, gl_ref, mod_ref,
                 wmg_ref, wba_ref, wbg_ref, wo_ref, l1g_ref, l1b_ref, wr_ref, br_ref,
                 x1_ref, h2_ref, rt_ref, ert_ref, cnt_ref, run_s, slab_s, *, n_ctx_tiles):
    i = pl.program_id(0)
    tm = xc_ref.shape[0]
    sub = ROW_GROUP
    n_groups = tm // sub
    is_ctx = i < n_ctx_tiles
    sh1 = mod_ref[0, :, 0:D]
    sc1 = mod_ref[0, :, D:2 * D]
    g1 = mod_ref[0, :, 2 * D:3 * D]
    sh2 = mod_ref[0, :, 3 * D:4 * D]
    sc2 = mod_ref[0, :, 4 * D:5 * D]

    parts = [slice(g * sub, (g + 1) * sub) for g in range(n_groups)]
    xs = [jnp.where(is_ctx, xc_ref[r, :], xl_ref[r, :]) for r in parts]
    hs = [(_ln(x) * (1.0 + sc1) + sh1).astype(BF16) for x in xs]
    gates = [jax.nn.sigmoid(_dot(h, wmg_ref[...])) for h in hs]
    ba = [_dot(jnp.where(is_ctx, ac_ref[r, :], al_ref[r, :]), wba_ref[...]) for r in parts]
    bg = [_dot(jnp.where(is_ctx, gc_ref[r, :], gl_ref[r, :]), wbg_ref[...]) for r in parts]
    merged = [(g[:, :D] * a + g[:, D:] * b).astype(BF16) for g, a, b in zip(gates, ba, bg)]
    mix = [_dot(m, wo_ref[...]) for m in merged]
    x1s = [_ln(ALPHA * x + g1 * m) * l1g_ref[...] + l1b_ref[...] for x, m in zip(xs, mix)]
    h2s = [_ln(x1) * (1.0 + sc2) + sh2 for x1 in x1s]
    logits = [_dot(h2.astype(BF16), wr_ref[...]) + br_ref[...] for h2 in h2s]
    for g, r in enumerate(parts):
        x1_ref[r, :] = x1s[g]
        _store_row_slabs(h2_ref.at[pl.ds(g * sub * ROW_SUB, sub * ROW_SUB), :], h2s[g], slab_s)

    @pl.when(i == 0)
    def _():
        run_s[...] = jnp.zeros_like(run_s)

    ri = lax.broadcasted_iota(jnp.int32, (sub, sub), 0)
    ci = lax.broadcasted_iota(jnp.int32, (sub, sub), 1)
    earlier = (ri > ci).astype(BF16)
    run = run_s[0:1, :]
    for g, r in enumerate(parts):
        run = _route(logits[g], earlier, run, rt_ref.at[r, :], ert_ref.at[:, r])
    run_s[0:1, :] = run
    cnt_ref[...] = jnp.broadcast_to(run, cnt_ref.shape)


def _route(logit, earlier, run, rt_ref, ert_ref):
    tm = logit.shape[0]
    lane_i = lax.broadcasted_iota(jnp.int32, (tm, LANES), 1)
    lane = lane_i.astype(F32)
    lane_grp = ((lane_i - N_GROUPS) >> 3).astype(F32)
    neg = jnp.float32(-jnp.inf)
    far = jnp.float32(LANES)
    is_g = lane_i < N_GROUPS
    lg = jnp.where(is_g, logit, neg)
    mg = jnp.max(lg, axis=-1, keepdims=True)
    pg_top = 1.0 / jnp.sum(jnp.where(is_g, jnp.exp(logit - mg), 0.0), axis=-1, keepdims=True)
    g_idx = jnp.min(jnp.where(lg == mg, lane, far), axis=-1, keepdims=True)
    in_grp = (lane_i >= N_GROUPS) & (lane_i < N_GROUPS + N_EXP) & (lane_grp == g_idx)
    le = jnp.where(in_grp, logit, neg)
    v1 = jnp.max(le, axis=-1, keepdims=True)
    i1 = jnp.min(jnp.where(le == v1, lane, far), axis=-1, keepdims=True)
    le2 = jnp.where(lane == i1, neg, le)
    v2 = jnp.max(le2, axis=-1, keepdims=True)
    i2 = jnp.min(jnp.where(le2 == v2, lane, far), axis=-1, keepdims=True)
    e1 = i1 - N_GROUPS
    e2 = i2 - N_GROUPS
    tt = jnp.exp(v2 - v1)
    w1 = pg_top / (1.0 + tt)
    w2 = pg_top * tt / (1.0 + tt)

    hot = ((lane == e1) | (lane == e2)).astype(F32)
    before = _dot(earlier, hot.astype(BF16)) + run
    r1 = jnp.sum(jnp.where(lane == e1, before, 0.0), axis=-1, keepdims=True)
    r2 = jnp.sum(jnp.where(lane == e2, before, 0.0), axis=-1, keepdims=True)

    rt = jnp.where(lane_i == 0, e1, 0.0)
    rt = jnp.where(lane_i == 1, e2, rt)
    rt = jnp.where(lane_i == 2, w1, rt)
    rt = jnp.where(lane_i == 3, w2, rt)
    rt = jnp.where(lane_i == 4, r1, rt)
    rt = jnp.where(lane_i == 5, r2, rt)
    rt_ref[...] = rt
    ert_ref[...] = rt.T[0:8, :]
    return run + jnp.sum(hot, axis=0, keepdims=True)


def _post(x_ctx, x_lat, a_ctx, a_lat, g_ctx, g_lat, mod_all, seq_lat,
          w_mg, w_ba, w_bg, w_o, l1g, l1b, w_r, b_r):
    t_ctx, t_lat = x_ctx.shape[0], x_lat.shape[0]
    tm = TM_TOK
    nct, nlt = t_ctx // tm, t_lat // tm
    per_seq = seq_lat // tm
    nb_lat = t_lat // seq_lat
    t = t_ctx + t_lat
    cmap = lambda i: (jnp.minimum(i, nct - 1), 0)
    lmap = lambda i: (jnp.maximum(i - nct, 0), 0)
    mmap = lambda i: (jnp.where(i < nct, nb_lat, jnp.maximum(i - nct, 0) // per_seq), 0, 0)
    row = lambda i: (i, 0)
    const = lambda i: (0, 0)
    return pl.pallas_call(
        functools.partial(_post_kernel, n_ctx_tiles=nct),
        grid=(nct + nlt,),
        in_specs=[pl.BlockSpec((tm, D), cmap), pl.BlockSpec((tm, D), lmap),
                  pl.BlockSpec((tm, AW), cmap), pl.BlockSpec((tm, AW), lmap),
                  pl.BlockSpec((tm, GVW), cmap), pl.BlockSpec((tm, GVW), lmap),
                  pl.BlockSpec((1, 1, 6 * D), mmap),
                  pl.BlockSpec((D, MG_WIDTH), const), pl.BlockSpec((AW, D), const),
                  pl.BlockSpec((GVW, D), const), pl.BlockSpec((D, D), const),
                  pl.BlockSpec((1, D), const), pl.BlockSpec((1, D), const),
                  pl.BlockSpec((D, LANES), const), pl.BlockSpec((1, LANES), const)],
        out_specs=[pl.BlockSpec((tm, D), row), pl.BlockSpec((tm * ROW_SUB, LANES), row),
                   pl.BlockSpec((tm, LANES), row), pl.BlockSpec((8, tm), lambda i: (0, i)),
                   pl.BlockSpec((8, LANES), const)],
        out_shape=[jax.ShapeDtypeStruct((t, D), F32), jax.ShapeDtypeStruct((t * ROW_SUB, LANES), BF16),
                   jax.ShapeDtypeStruct((t, LANES), F32), jax.ShapeDtypeStruct((8, t), F32),
                   jax.ShapeDtypeStruct((8, LANES), F32)],
        scratch_shapes=[pltpu.VMEM((8, LANES), F32), pltpu.VMEM((ROW_GROUP * ROW_SUB, LANES), F32)],
        compiler_params=_cparams(("arbitrary",)),
        name="post",
    )(x_ctx, x_lat, a_ctx, a_lat, g_ctx, g_lat, mod_all, w_mg, w_ba, w_bg, w_o, l1g, l1b, w_r, b_r)


ROW_UNROLL = 8


def _row_copy(src_ref, dst_ref, sem):
    return pltpu.make_async_copy(src_ref, dst_ref, sem)


def _scatter_kernel(pos0_ref, pos1_ref, h_ref, xs_ref, sem, zero_s, zero_sem):
    ts = h_ref.shape[0] // ROW_SUB
    pad_rows = TM_EXP * ROW_SUB

    @pl.when(pl.program_id(0) == 0)
    def _():
        zero_s[...] = jnp.zeros_like(zero_s)
        pad = _row_copy(zero_s, xs_ref.at[pl.ds(xs_ref.shape[0] - pad_rows, pad_rows), :], zero_sem)
        pad.start()
        pad.wait()

    def issue(g, carry):
        r0 = pl.multiple_of(g * ROW_UNROLL, ROW_UNROLL)
        for k in range(ROW_UNROLL):
            src = _row_slab(h_ref, r0 + k)
            _row_copy(src, _row_slab(xs_ref, pos0_ref[0, r0 + k]), sem).start(priority=0)
            _row_copy(src, _row_slab(xs_ref, pos1_ref[0, r0 + k]), sem).start(priority=1)
        return carry

    lax.fori_loop(0, ts // ROW_UNROLL, issue, 0)
    for _ in range(2):
        _row_copy(h_ref, xs_ref.at[pl.ds(0, ts * ROW_SUB), :], sem).wait()


def _scatter_rows(h2p, pos0, pos1):
    t = h2p.shape[0] // ROW_SUB
    ts = TS_ROWS
    smem = lambda: pl.BlockSpec((None, 1, ts), lambda i: (i, 0, 0), memory_space=pltpu.SMEM)
    return pl.pallas_call(
        _scatter_kernel,
        grid=(t // ts,),
        in_specs=[smem(), smem(), pl.BlockSpec((ts * ROW_SUB, LANES), lambda i: (i, 0))],
        out_specs=pl.BlockSpec(memory_space=pl.ANY),
        out_shape=jax.ShapeDtypeStruct(((2 * t + TM_EXP) * ROW_SUB, LANES), BF16),
        scratch_shapes=[pltpu.SemaphoreType.DMA(()), pltpu.VMEM((TM_EXP * ROW_SUB, LANES), BF16),
                        pltpu.SemaphoreType.DMA(())],
        compiler_params=_cparams(("arbitrary",)),
        name="scatter",
    )(pos0.reshape(t // ts, 1, ts), pos1.reshape(t // ts, 1, ts), h2p)


def _expert_kernel(start_ref, nwin_ref, xs_ref, wg_ref, wu_ref, wd_ref, ys_ref,
                   wgu_s, wd_s, in_buf, out_buf, slab_s, in_sem, out_sem):
    e = pl.program_id(0)
    tm = TM_EXP
    wgu_s[:, 0:D_EXP] = wg_ref[...].astype(BF16)
    wgu_s[:, D_EXP:2 * D_EXP] = wu_ref[...].astype(BF16)
    wd_s[...] = wd_ref[...].astype(BF16)
    base = start_ref[e]
    n = nwin_ref[e]

    @pl.when(e == 0)
    def _():
        out_buf[0] = jnp.zeros(out_buf.shape[1:], BF16)
        pad = pltpu.make_async_copy(out_buf.at[0], ys_ref.at[pl.ds(ys_ref.shape[0] - tm * ROW_SUB, tm * ROW_SUB), :],
                                    out_sem.at[0])
        pad.start()
        pad.wait()

    def rows(w):
        return pl.ds(pl.multiple_of((base + w * tm) * ROW_SUB, ROW_SUB), tm * ROW_SUB)

    def read(w, slot):
        return pltpu.make_async_copy(xs_ref.at[rows(w), :], in_buf.at[slot], in_sem.at[slot])

    def write(w, slot):
        return pltpu.make_async_copy(out_buf.at[slot], ys_ref.at[rows(w), :], out_sem.at[slot])

    @pl.when(n > 0)
    def _():
        read(0, 0).start()

    def body(w, carry):
        slot = w % 2

        @pl.when(w + 1 < n)
        def _():
            read(w + 1, 1 - slot).start()

        read(w, slot).wait()

        @pl.when(w >= 2)
        def _():
            write(w - 2, slot).wait()

        gu = _dot(_load_row_slabs(in_buf.at[slot], slab_s).astype(BF16), wgu_s[...])
        hid = _silu(gu[:, 0:D_EXP]) * gu[:, D_EXP:2 * D_EXP]
        _store_row_slabs(out_buf.at[slot], _dot(hid.astype(BF16), wd_s[...]), slab_s)
        write(w, slot).start()
        return carry

    lax.fori_loop(0, n, body, 0)

    @pl.when(n >= 1)
    def _():
        write(n - 1, (n - 1) % 2).wait()

    @pl.when(n >= 2)
    def _():
        write(n - 2, n % 2).wait()


def _experts(xs, starts, n_win, w_gate, w_up, w_down):
    tm = TM_EXP
    wmap = lambda e, st, nw: (e, 0, 0)
    slab = (tm * ROW_SUB, LANES)
    return pl.pallas_call(
        _expert_kernel,
        grid_spec=pltpu.PrefetchScalarGridSpec(
            num_scalar_prefetch=2,
            grid=(N_EXP,),
            in_specs=[pl.BlockSpec(memory_space=pl.ANY),
                      pl.BlockSpec((None, D, D_EXP), wmap), pl.BlockSpec((None, D, D_EXP), wmap),
                      pl.BlockSpec((None, D_EXP, D), wmap)],
            out_specs=pl.BlockSpec(memory_space=pl.ANY),
            scratch_shapes=[pltpu.VMEM((D, 2 * D_EXP), BF16), pltpu.VMEM((D_EXP, D), BF16),
                            pltpu.VMEM((2,) + slab, BF16), pltpu.VMEM((2,) + slab, BF16),
                            pltpu.VMEM((tm * ROW_SUB, LANES), F32),
                            pltpu.SemaphoreType.DMA((2,)), pltpu.SemaphoreType.DMA((2,))]),
        out_shape=jax.ShapeDtypeStruct(xs.shape, BF16),
        compiler_params=_cparams(("arbitrary",)),
        name="experts",
    )(starts, n_win, xs, w_gate, w_up, w_down)


def _final_kernel(p0c_ref, p1c_ref, p0n_ref, p1n_ref, x1_ref, rt_ref, mod_ref, l2g_ref, l2b_ref, ys_ref,
                  oc_ref, ol_ref, buf, sem, slab_s, *, n_ctx_tiles):
    i = pl.program_id(0)
    n = pl.num_programs(0)
    tm = x1_ref.shape[0]

    def gather(p0_ref, p1_ref, slot):
        def issue(g, carry):
            r0 = pl.multiple_of(g * ROW_UNROLL, ROW_UNROLL)
            for k in range(ROW_UNROLL):
                _row_copy(_row_slab(ys_ref, p0_ref[0, r0 + k]),
                          _row_slab(buf.at[slot, 0], r0 + k), sem.at[slot]).start(priority=0)
                _row_copy(_row_slab(ys_ref, p1_ref[0, r0 + k]),
                          _row_slab(buf.at[slot, 1], r0 + k), sem.at[slot]).start(priority=1)
            return carry

        lax.fori_loop(0, tm // ROW_UNROLL, issue, 0)

    cur = i % 2

    @pl.when(i == 0)
    def _():
        gather(p0c_ref, p1c_ref, 0)

    @pl.when(i + 1 < n)
    def _():
        gather(p0n_ref, p1n_ref, 1 - cur)

    for k in range(2):
        _row_copy(ys_ref.at[pl.ds(0, tm * ROW_SUB), :], buf.at[cur, k], sem.at[cur]).wait()

    g2 = mod_ref[0, :, 5 * D:6 * D]
    w1 = rt_ref[:, 2:3]
    w2 = rt_ref[:, 3:4]
    moe = (w1 * _load_row_slabs(buf.at[cur, 0], slab_s.at[0])
           + w2 * _load_row_slabs(buf.at[cur, 1], slab_s.at[1]))
    out = _ln(ALPHA * x1_ref[...] + g2 * moe) * l2g_ref[...] + l2b_ref[...]

    @pl.when(i < n_ctx_tiles)
    def _():
        oc_ref[...] = out

    @pl.when(i >= n_ctx_tiles)
    def _():
        ol_ref[...] = out


def _final(x1, rt, pos0, pos1, mod_all, l2g, l2b, ys, t_ctx, seq_lat):
    t = x1.shape[0]
    tm = TM_TOK
    nt = t // tm
    nct = t_ctx // tm
    t_lat = t - t_ctx
    per_seq = seq_lat // tm
    nb_lat = t_lat // seq_lat
    p0 = pos0.reshape(nt, 1, tm)
    p1 = pos1.reshape(nt, 1, tm)
    row = lambda i: (i, 0)
    const = lambda i: (0, 0)
    mmap = lambda i: (jnp.where(i < nct, nb_lat, jnp.maximum(i - nct, 0) // per_seq), 0, 0)
    smem_cur = lambda: pl.BlockSpec((None, 1, tm), lambda i: (i, 0, 0), memory_space=pltpu.SMEM)
    smem_nxt = lambda: pl.BlockSpec((None, 1, tm), lambda i: (jnp.minimum(i + 1, nt - 1), 0, 0),
                                    memory_space=pltpu.SMEM)
    return pl.pallas_call(
        functools.partial(_final_kernel, n_ctx_tiles=nct),
        grid=(nt,),
        in_specs=[smem_cur(), smem_cur(), smem_nxt(), smem_nxt(),
                  pl.BlockSpec((tm, D), row), pl.BlockSpec((tm, LANES), row),
                  pl.BlockSpec((1, 1, 6 * D), mmap),
                  pl.BlockSpec((1, D), const), pl.BlockSpec((1, D), const),
                  pl.BlockSpec(memory_space=pl.ANY)],
        out_specs=[pl.BlockSpec((tm, D), lambda i: (jnp.minimum(i, nct - 1), 0)),
                   pl.BlockSpec((tm, D), lambda i: (jnp.maximum(i - nct, 0), 0))],
        out_shape=[jax.ShapeDtypeStruct((t_ctx, D), F32), jax.ShapeDtypeStruct((t_lat, D), F32)],
        scratch_shapes=[pltpu.VMEM((2, 2, tm * ROW_SUB, LANES), BF16), pltpu.SemaphoreType.DMA((2,)),
                        pltpu.VMEM((2, tm * ROW_SUB, LANES), F32)],
        compiler_params=_cparams(("arbitrary",)),
        name="final",
    )(p0, p1, p0, p1, x1, rt, mod_all, l2g, l2b, ys)


def _reorder_q_heads(w, axis):
    shape = w.shape
    split = shape[:axis] + (N_KV_HEADS, N_Q_HEADS // N_KV_HEADS, HD) + shape[axis + 1:]
    return jnp.swapaxes(w.reshape(split), axis, axis + 1).reshape(shape)


def _rope_tables(seq):
    t = np.arange(seq)
    half = HD // 4
    inv = (ROPE_THETA ** (-np.arange(half, dtype=np.float64) / half)).astype(np.float32)
    d64 = np.arange(LANES) % HD
    pos = np.where((d64 < HD // 2)[None, :], (t // GRID_W)[:, None], (t % GRID_W)[:, None])
    ang = (pos.astype(np.float32) * inv[d64 % half][None, :]).astype(np.float64)
    sign = np.where((d64 % 32) < 16, -1.0, 1.0)
    return (jnp.asarray(np.cos(ang), F32), jnp.asarray(np.sin(ang) * sign[None, :], F32))


def _pair_states(s):
    b = s.shape[0]
    s = s.reshape(b, 2, 2, GLA_DK, GLA_DV)
    return s.transpose(0, 1, 4, 2, 3).reshape(b, 2, GLA_DV, 2 * GLA_DK)


def _unpair_states(s):
    b = s.shape[0]
    s = s.reshape(b, 2, GLA_DV, 2, GLA_DK)
    return s.transpose(0, 1, 3, 4, 2).reshape(b, GLA_H, GLA_DK, GLA_DV)


def _route_tables(ert, counts):
    i32 = jnp.int32
    cnt = counts[0, :N_EXP].astype(i32)
    starts = jnp.cumsum(cnt) - cnt
    table = lambda e: jnp.sum(jnp.where(e[None, :] == jnp.arange(N_EXP, dtype=i32)[:, None],
                                        starts[:, None], 0), axis=0)
    pos0 = table(ert[0].astype(i32)) + ert[4].astype(i32)
    pos1 = table(ert[1].astype(i32)) + ert[5].astype(i32)
    return pos0, pos1, starts, (cnt + (TM_EXP - 1)) // TM_EXP


def kernel(x_prompt, x_sample, cache_k, cache_v, state_gla_fwd, state_gla_bwd, c, c_ctx, w_ada, b_ada, w_in, q_norm, k_norm, gla_w_gate, gla_b_gate, gla_norm, w_br_attn, w_br_gla, w_out, ln1_g, ln1_b, router_group_w, router_group_b, router_expert_w, router_expert_b, exp_w_gate, exp_w_up, exp_w_down, ln2_g, ln2_b):
    b_ctx, seq_ctx, _ = x_prompt.shape
    b_lat, seq_lat, _ = x_sample.shape
    t_ctx, t_lat = b_ctx * seq_ctx, b_lat * seq_lat
    t = t_ctx + t_lat
    l = 0

    rows = -(-(b_lat + 1) // 8) * 8
    c_rows = jnp.zeros((rows, D), F32).at[:b_lat].set(c).at[b_lat].set(c_ctx)
    mod = _ada(c_rows, w_ada[l], b_ada[l][None, :])
    mod_all = mod[:b_lat + 1, None, :]
    mod_lat = mod_all[:b_lat]
    mod_ctx = mod_all[b_lat:]

    w_full = w_in[l]
    w_a = jnp.concatenate([_reorder_q_heads(w_full[:, :AW], 1), w_full[:, AW:A_WIDTH]], axis=1).astype(BF16)
    w_mg = w_full[:, A_WIDTH:].astype(BF16)
    gain = jnp.concatenate([jnp.tile(q_norm[l], N_Q_HEADS), jnp.tile(k_norm[l], N_KV_HEADS)])[None, :]
    head_of = np.arange(AW + KVW) // HD
    ind = jnp.asarray((head_of[:, None] == np.arange(LANES)[None, :]) / HD, BF16)
    w_ba = _reorder_q_heads(w_br_attn[l], 0).astype(BF16)
    w_bg = w_br_gla[l].astype(BF16)
    w_o = w_out[l].astype(BF16)
    w_r = jnp.zeros((D, LANES), F32).at[:, :N_GROUPS].set(router_group_w[l])
    w_r = w_r.at[:, N_GROUPS:N_GROUPS + N_EXP].set(router_expert_w[l]).astype(BF16)
    b_r = jnp.zeros((1, LANES), F32).at[0, :N_GROUPS].set(router_group_b[l])
    b_r = b_r.at[0, N_GROUPS:N_GROUPS + N_EXP].set(router_expert_b[l])
    wg = jnp.zeros((2, 2 * GATE_RANK, GKW), F32)
    wg = wg.at[0, :GATE_RANK].set(gla_w_gate[l, 0]).at[1, GATE_RANK:].set(gla_w_gate[l, 1])
    bg = gla_b_gate[l][:, None, :]
    gn = gla_norm[l][None, :]

    xc = x_prompt.reshape(t_ctx, D)
    xl = x_sample.reshape(t_lat, D)

    q_c, k_c, v_c, qg_c, kg_c, vg_c, rs_c, lr_c, kf_c, vf_c = _inproj(
        xc, mod_ctx, w_a, gain, ind, None, seq_ctx, latent=False)
    attn_c = _attention(q_c, k_c, v_c, None, seq_ctx)
    zero_state = jnp.zeros((1, 2, 2, LANES, LANES), F32)
    gla_c, sfin_c = _gla(qg_c, kg_c, vg_c, lr_c, rs_c, zero_state, wg, bg, gn, seq_ctx)

    q_l, k_l, v_l, qg_l, kg_l, vg_l, rs_l, lr_l = _inproj(
        xl, mod_lat, w_a, gain, ind, _rope_tables(seq_lat), seq_lat, latent=True)
    past = cache_k.shape[2]
    kc = cache_k[:, l].reshape(b_lat, past, KVW).astype(BF16)
    vc = cache_v[:, l].reshape(b_lat, past, KVW).astype(BF16)
    attn_l = _attention(q_l, k_l, v_l, (kc, vc), seq_lat)
    s0 = jnp.stack([_pair_states(state_gla_fwd[:, l]), _pair_states(state_gla_bwd[:, l])], axis=1)
    gla_l, _ = _gla(qg_l, kg_l, vg_l, lr_l, rs_l, s0, wg, bg, gn, seq_lat)

    x1, h2p, rt, ert, counts = _post(xc, xl, attn_c, attn_l, gla_c, gla_l, mod_all, seq_lat,
                                w_mg, w_ba, w_bg, w_o, ln1_g[l][None, :], ln1_b[l][None, :], w_r, b_r)

    pos0, pos1, starts, n_win = _route_tables(ert, counts)
    xs = _scatter_rows(h2p, pos0, pos1)
    ys = _experts(xs, starts, n_win, exp_w_gate[l], exp_w_up[l], exp_w_down[l])
    y_ctx, y_lat = _final(x1, rt, pos0, pos1, mod_all, ln2_g[l][None, :], ln2_b[l][None, :], ys,
                          t_ctx, seq_lat)

    untranspose = lambda a: a.reshape(b_ctx, 1, N_KV_HEADS, HD, seq_ctx).transpose(0, 1, 4, 2, 3)
    new_k, new_v = untranspose(kf_c), untranspose(vf_c)
    new_sf = _unpair_states(sfin_c[:, 0])[:, None]
    new_sb = _unpair_states(sfin_c[:, 1])[:, None]
    return (y_ctx.reshape(b_ctx, seq_ctx, D), y_lat.reshape(b_lat, seq_lat, D),
            new_k, new_v, new_sf, new_sb)
```

```python
import functools

import numpy as np
import jax
import jax.numpy as jnp
from jax import lax
from jax.experimental import pallas as pl
from jax.experimental.pallas import tpu as pltpu

F32 = jnp.float32
BF16 = jnp.bfloat16
HIGHEST = lax.Precision.HIGHEST

D = 1024
GRID_W = 64
HD = 64
N_Q_HEADS = 8
N_KV_HEADS = 2
AW = N_Q_HEADS * HD
KVW = N_KV_HEADS * HD
ROPE_THETA = 10000.0
GLA_H = 4
GLA_DK = 64
GLA_DV = 128
GKW = GLA_H * GLA_DK
GVW = GLA_H * GLA_DV
GATE_RANK = 16
GLA_TAU = 16.0
CHUNK = 64
N_GROUPS = 4
EPG = 8
N_EXP = N_GROUPS * EPG
D_EXP = 256
DEPTH = 1
ALPHA = (2.0 * DEPTH) ** 0.25
LN_EPS = 1e-6
RMS_EPS = 1e-6

LANES = 128
A_WIDTH = AW + 2 * KVW + 2 * GKW + 2 * GVW + 2 * GATE_RANK
MG_WIDTH = 2 * D
TM_TOK = 512
ROW_GROUP = 128
LOG2E = 1.4426950408889634
ONES_ROWS = 16
TQ_LAT = 256
TM_EXP = 256
READ_AHEAD = 2
TS_ROWS = 1024
VMEM_LIMIT = 56 * 1024 * 1024


def _cparams(sem):
    return pltpu.CompilerParams(dimension_semantics=sem, vmem_limit_bytes=VMEM_LIMIT)


def _dot(a, b):
    return jnp.dot(a, b, preferred_element_type=F32)


def _dot_nt(a, b):
    return lax.dot_general(a, b, (((1,), (1,)), ((), ())), preferred_element_type=F32)


def _dot_tn(a, b):
    return lax.dot_general(a, b, (((0,), (0,)), ((), ())), preferred_element_type=F32)


def _ln(x):
    mu = jnp.mean(x, axis=-1, keepdims=True)
    xc = x - mu
    var = jnp.mean(xc * xc, axis=-1, keepdims=True)
    return xc * lax.rsqrt(var + LN_EPS)


def _silu(x):
    return x * jax.nn.sigmoid(x)


def _split_bf16(x):
    hi = x.astype(BF16)
    lo = (x - hi.astype(F32)).astype(BF16)
    return hi, lo


def _ada_kernel(c_ref, w_ref, b_ref, o_ref):
    s = _silu(c_ref[...])
    o_ref[...] = jnp.dot(s, w_ref[...], preferred_element_type=F32, precision=HIGHEST) + b_ref[...]


def _ada(c_rows, w_ada, b_ada):
    rows = c_rows.shape[0]
    n = w_ada.shape[1]
    bn = 1024
    return pl.pallas_call(
        _ada_kernel,
        grid=(n // bn,),
        in_specs=[pl.BlockSpec((rows, D), lambda j: (0, 0)),
                  pl.BlockSpec((D, bn), lambda j: (0, j)),
                  pl.BlockSpec((1, bn), lambda j: (0, j))],
        out_specs=pl.BlockSpec((rows, bn), lambda j: (0, j)),
        out_shape=jax.ShapeDtypeStruct((rows, n), F32),
        compiler_params=_cparams(("arbitrary",)),
        name="ada",
    )(c_rows, w_ada, b_ada)


def _inproj_kernel(*refs, latent, seq):
    if latent:
        (x_ref, mod_ref, w_ref, gain_ref, ind_ref, cos_ref, sin_ref,
         q_ref, k_ref, v_ref, qg_ref, kg_ref, vg_ref, rs_ref, lr_ref) = refs
    else:
        (x_ref, mod_ref, w_ref, gain_ref, ind_ref,
         q_ref, k_ref, v_ref, qg_ref, kg_ref, vg_ref, rs_ref, lr_ref, kf_ref, vf_ref) = refs
    tm = x_ref.shape[0]
    sub = ROW_GROUP
    n_groups = tm // sub
    sh1 = mod_ref[0, :, 0:D]
    sc1 = mod_ref[0, :, D:2 * D]
    lane = lax.broadcasted_iota(jnp.int32, (sub, LANES), 1)
    low_half = lane < HD
    first = (lane % 32) < 16

    def project(g):
        rows = slice(g * sub, (g + 1) * sub)
        h = (_ln(x_ref[rows, :]) * (1.0 + sc1) + sh1).astype(BF16)
        return _dot(h, w_ref[...])

    def finish(g, res):
        rows = slice(g * sub, (g + 1) * sub)
        qk = res[:, 0:AW + KVW]
        hi, lo = _split_bf16(qk * qk)
        ms = _dot(hi, ind_ref[...]) + _dot(lo, ind_ref[...])
        r = lax.rsqrt(ms + RMS_EPS)
        for s in range(5):
            rb = jnp.where(low_half, r[:, 2 * s:2 * s + 1], r[:, 2 * s + 1:2 * s + 2])
            y = res[:, LANES * s:LANES * (s + 1)] * rb * gain_ref[:, LANES * s:LANES * (s + 1)]
            if s == 4 and not latent:
                kf_ref[(g * sub) // seq, :, (g * sub) % seq:(g * sub) % seq + sub] = y.T
            if latent:
                partner = jnp.where(first, pltpu.roll(y, LANES - 16, 1), pltpu.roll(y, 16, 1))
                y = y * cos_ref[rows, :] + partner * sin_ref[rows, :]
            if s < 4:
                q_ref[rows, LANES * s:LANES * (s + 1)] = (y * (HD ** -0.5 * LOG2E)).astype(BF16)
            else:
                k_ref[rows, :] = y.astype(BF16)
        o = AW + KVW
        v = res[:, o:o + KVW]
        v_ref[rows, :] = v.astype(BF16)
        if not latent:
            vf_ref[(g * sub) // seq, :, (g * sub) % seq:(g * sub) % seq + sub] = v.T
        o += KVW
        qg_ref[rows, :] = res[:, o:o + GKW] * (GLA_DK ** -0.5)
        o += GKW
        kg_ref[rows, :] = res[:, o:o + GKW]
        o += GKW
        vg_ref[rows, :] = res[:, o:o + GVW].astype(BF16)
        o += GVW
        rs_ref[rows, :] = _silu(res[:, o:o + GVW]).astype(BF16)
        o += GVW
        lr_ref[rows, :] = res[:, o:o + 2 * GATE_RANK]

    pending = {0: project(0)}
    for g in range(n_groups):
        if g + 1 < n_groups:
            pending[g + 1] = project(g + 1)
        finish(g, pending.pop(g))


def _inproj(x2, mod, w_a, gain, ind, rope, seq, latent):
    t = x2.shape[0]
    tm = TM_TOK
    per_seq = max(seq // tm, 1)
    per_tile = max(tm // seq, 1)
    row = lambda i: (i, 0)
    const = lambda i: (0, 0)
    in_specs = [pl.BlockSpec((tm, D), row),
                pl.BlockSpec((1, 1, 6 * D), (lambda i: (i // per_seq, 0, 0)) if latent else (lambda i: (0, 0, 0))),
                pl.BlockSpec((D, A_WIDTH), const),
                pl.BlockSpec((1, AW + KVW), const),
                pl.BlockSpec((AW + KVW, LANES), const)]
    args = [x2, mod, w_a, gain, ind]
    if latent:
        in_specs += [pl.BlockSpec((tm, LANES), lambda i: (i % per_seq, 0))] * 2
        args += list(rope)
    widths = [(AW, BF16), (KVW, BF16), (KVW, BF16), (GKW, F32), (GKW, F32), (GVW, BF16), (GVW, BF16),
              (2 * GATE_RANK, F32)]
    out_specs = [pl.BlockSpec((tm, w), row) for w, _ in widths]
    out_shape = [jax.ShapeDtypeStruct((t, w), dt) for w, dt in widths]
    if not latent:
        cache_spec = pl.BlockSpec((per_tile, KVW, seq), lambda i: (i, 0, 0))
        out_specs += [cache_spec] * 2
        out_shape += [jax.ShapeDtypeStruct((t // seq, KVW, seq), F32)] * 2
    return pl.pallas_call(
        functools.partial(_inproj_kernel, latent=latent, seq=seq),
        grid=(t // tm,),
        in_specs=in_specs,
        out_specs=out_specs,
        out_shape=out_shape,
        compiler_params=_cparams(("parallel",)),
        name="inproj_lat" if latent else "inproj_ctx",
    )(*args)


def _attn_kernel(*refs, has_cache):
    def transposed_with_ones(dst, src):
        dst[0:KVW, :] = src[...].astype(F32).T.astype(BF16)
        dst[KVW:, :] = jnp.ones((ONES_ROWS, dst.shape[1]), BF16)

    if has_cache:
        q_ref, k_ref, v_ref, kc_ref, vc_ref, o_ref, vt_s, vct_s = refs

        @pl.when(pl.program_id(1) == 0)
        def _():
            transposed_with_ones(vt_s, v_ref)
            transposed_with_ones(vct_s, vc_ref)
    else:
        q_ref, k_ref, v_ref, o_ref, vt_s = refs
        transposed_with_ones(vt_s, v_ref)
    tq = q_ref.shape[0]

    lane = lax.broadcasted_iota(jnp.int32, (tq, LANES), 1)
    low_half = lane < HD
    k = k_ref[...]
    scores = []
    for j in range(N_KV_HEADS):
        keep = low_half if j == 0 else jnp.logical_not(low_half)
        zero = jnp.zeros((tq, LANES), BF16)
        for pair in range(2):
            qs = jnp.concatenate([jnp.where(keep, q_ref[:, LANES * s:LANES * (s + 1)], zero)
                                  for s in (2 * pair, 2 * pair + 1)], axis=0)
            s1 = _dot_nt(k, qs)
            s2 = _dot_nt(kc_ref[...], qs) if has_cache else None
            scores.append((s1, s2))
    outs = []
    for s1, s2 in scores:
        m = jnp.max(s1, axis=0, keepdims=True)
        if has_cache:
            m = jnp.maximum(m, jnp.max(s2, axis=0, keepdims=True))
        acc = _dot(vt_s[...], jnp.exp2(s1 - m).astype(BF16))
        if has_cache:
            acc = acc + _dot(vct_s[...], jnp.exp2(s2 - m).astype(BF16))
        outs.append(acc[0:KVW] / acc[KVW:KVW + 1])
    head0 = jnp.concatenate(outs[0:2], axis=1)
    head1 = jnp.concatenate(outs[2:4], axis=1)
    row = lax.broadcasted_iota(jnp.int32, (LANES, 4 * tq), 0)
    out = jnp.where(row < HD, head0, head1).T
    for s in range(4):
        o_ref[:, LANES * s:LANES * (s + 1)] = out[s * tq:(s + 1) * tq].astype(BF16)


def _attention(q, k, v, cache, seq):
    t = q.shape[0]
    if cache is None:
        tq = seq
        grid = (t // seq,)
        qmap = lambda b: (b, 0)
        in_specs = [pl.BlockSpec((tq, AW), qmap), pl.BlockSpec((seq, KVW), qmap),
                    pl.BlockSpec((seq, KVW), qmap)]
        args = [q, k, v]
        scratch = [pltpu.VMEM((KVW + ONES_ROWS, seq), BF16)]
        sem = ("parallel",)
        name = "attn_ctx"
    else:
        tq = TQ_LAT
        nq = seq // tq
        kc, vc = cache
        past = kc.shape[1]
        grid = (t // seq, nq)
        qmap = lambda b, i: (b * nq + i, 0)
        kmap = lambda b, i: (b, 0)
        cmap = lambda b, i: (b, 0, 0)
        in_specs = [pl.BlockSpec((tq, AW), qmap), pl.BlockSpec((seq, KVW), kmap),
                    pl.BlockSpec((seq, KVW), kmap),
                    pl.BlockSpec((None, past, KVW), cmap), pl.BlockSpec((None, past, KVW), cmap)]
        args = [q, k, v, kc, vc]
        scratch = [pltpu.VMEM((KVW + ONES_ROWS, seq), BF16), pltpu.VMEM((KVW + ONES_ROWS, past), BF16)]
        sem = ("parallel", "arbitrary")
        name = "attn_lat"
    return pl.pallas_call(
        functools.partial(_attn_kernel, has_cache=cache is not None),
        grid=grid,
        in_specs=in_specs,
        out_specs=pl.BlockSpec((tq, AW), qmap),
        out_shape=jax.ShapeDtypeStruct((t, AW), BF16),
        scratch_shapes=scratch,
        compiler_params=_cparams(sem),
        name=name,
    )(*args)


GLA_BLK = 256
GLA_UNROLL = 2


def _split3_bf16(x):
    hi = x.astype(BF16)
    r1 = x - hi.astype(F32)
    mid = r1.astype(BF16)
    lo = (r1 - mid.astype(F32)).astype(BF16)
    return hi, mid, lo


def _gla_kernel(qg_ref, kg_ref, vg_ref, lr_ref, rs_ref, s0_ref, wg_ref, bg_ref, gn_ref,
                o_ref, sfin_ref, cum_s, kv_s, dec_s):
    n = qg_ref.shape[0]
    nc = n // CHUNK
    lane = lax.broadcasted_iota(jnp.int32, (CHUNK, LANES), 1)
    low_half = lane < GLA_DK
    lane_sq = lax.broadcasted_iota(jnp.int32, (LANES, LANES), 1)
    low_half_sq = lane_sq < GLA_DK
    ri = lax.broadcasted_iota(jnp.int32, (2 * CHUNK, 2 * CHUNK), 0)
    ci = lax.broadcasted_iota(jnp.int32, (2 * CHUNK, 2 * CHUNK), 1)
    diag = (ri >> 6) == (ci >> 6)
    keep2 = (diag & (ri >= ci), diag & (ci >= ri))

    rb = lax.broadcasted_iota(jnp.int32, (GLA_BLK, GLA_BLK), 0)
    cb = lax.broadcasted_iota(jnp.int32, (GLA_BLK, GLA_BLK), 1)
    same = (rb >> 6) == (cb >> 6)
    tri = ((same & (rb >= cb)).astype(BF16), (same & (cb >= rb)).astype(BF16))
    for d in range(2):
        w_hi, w_lo = _split_bf16(wg_ref[d])
        for blk in range(n // GLA_BLK):
            rows = slice(blk * GLA_BLK, (blk + 1) * GLA_BLK)
            l_hi, l_lo = _split_bf16(lr_ref[rows, :])
            z = _dot(l_hi, w_hi) + _dot(l_lo, w_hi) + _dot(l_hi, w_lo) + bg_ref[d]
            logg = (jnp.minimum(z, 0.0) - jnp.log(1.0 + jnp.exp(-jnp.abs(z)))) * (1.0 / GLA_TAU)
            pieces = _split3_bf16(logg)
            cum_s[d, rows, :] = _dot(tri[d], pieces[0]) + _dot(tri[d], pieces[1]) + _dot(tri[d], pieces[2])

    def pass1(i, carry):
        cs = [i * GLA_UNROLL + u for u in range(GLA_UNROLL)]
        rws = [pl.ds(pl.multiple_of(c * CHUNK, CHUNK), CHUNK) for c in cs]
        prods = []
        for c, rows in zip(cs, rws):
            kc = kg_ref[rows, :]
            kdec = []
            for d in range(2):
                cum = cum_s[d, rows, :]
                last = cum[CHUNK - 1:CHUNK, :] if d == 0 else cum[0:1, :]
                dec_s[d, c] = jnp.exp(last)
                kdec.append((kc * jnp.exp(last - cum)).astype(BF16))
            for p in range(2):
                vpair = vg_ref[rows, GLA_DV * 2 * p:GLA_DV * 2 * (p + 1)]
                kpair = jnp.concatenate([kdec[0][:, LANES * p:LANES * (p + 1)],
                                         kdec[1][:, LANES * p:LANES * (p + 1)]], axis=1)
                prods.append(_dot_tn(vpair, kpair))
        for j, c in enumerate(cs):
            for p in range(2):
                res = prods[2 * j + p]
                for d in range(2):
                    cols = slice(LANES * d, LANES * (d + 1))
                    kv_s[d, c, p] = jnp.where(low_half_sq, res[0:GLA_DV, cols], res[GLA_DV:2 * GLA_DV, cols])
        return carry

    lax.fori_loop(0, nc // GLA_UNROLL, pass1, 0)

    for d in range(2):
        def scan(i, st):
            c = i if d == 0 else nc - 1 - i
            dec = dec_s[d, c]
            new = []
            for p in range(2):
                kv = kv_s[d, c, p]
                kv_s[d, c, p] = st[p]
                new.append(st[p] * dec[:, LANES * p:LANES * (p + 1)] + kv)
            return tuple(new)

        fin = lax.fori_loop(0, nc, scan, (s0_ref[d, 0], s0_ref[d, 1]))
        sfin_ref[d, 0] = fin[0]
        sfin_ref[d, 1] = fin[1]

    def pass3(i, carry):
        cs = [i * GLA_UNROLL + u for u in range(GLA_UNROLL)]
        rws = [pl.ds(pl.multiple_of(c * CHUNK, CHUNK), CHUNK) for c in cs]
        first = []
        for c, rows in zip(cs, rws):
            q = qg_ref[rows, :]
            k = kg_ref[rows, :]
            for d in range(2):
                cum = cum_s[d, rows, :]
                qt = q * jnp.exp(cum)
                kt = (k * jnp.exp(-cum)).astype(BF16)
                for p in range(2):
                    qs = qt[:, LANES * p:LANES * (p + 1)]
                    lhs = jnp.concatenate([jnp.where(low_half, qs, 0.0), jnp.where(low_half, 0.0, qs)],
                                          axis=0).astype(BF16)
                    kts = kt[:, LANES * p:LANES * (p + 1)]
                    rhs = jnp.concatenate([kts, kts, kv_s[d, c, p].astype(BF16)], axis=0)
                    first.append(_dot_nt(lhs, rhs))
        second = []
        for j, rows in enumerate(rws):
            for d in range(2):
                for p in range(2):
                    res = first[4 * j + 2 * d + p]
                    vp = jnp.concatenate([vg_ref[rows, GLA_DV * (2 * p):GLA_DV * (2 * p + 1)],
                                          vg_ref[rows, GLA_DV * (2 * p + 1):GLA_DV * (2 * p + 2)]], axis=0)
                    a = jnp.where(keep2[d], res[:, 0:2 * CHUNK], 0.0).astype(BF16)
                    second.append(_dot(a, vp) + res[:, 2 * CHUNK:])
        for j, rows in enumerate(rws):
            for p in range(2):
                tot = second[4 * j + p] + second[4 * j + 2 + p]
                y = tot * lax.rsqrt(jnp.mean(tot * tot, axis=-1, keepdims=True) + RMS_EPS) * gn_ref[...]
                for hh in range(2):
                    cols = slice(GLA_DV * (2 * p + hh), GLA_DV * (2 * p + hh + 1))
                    o_ref[rows, cols] = (y[CHUNK * hh:CHUNK * (hh + 1)]
                                         * rs_ref[rows, cols].astype(F32)).astype(BF16)
        return carry

    lax.fori_loop(0, nc // GLA_UNROLL, pass3, 0)


def _gla(qg, kg, vg, lr, rs, s0, wg, bg, gn, seq):
    t = qg.shape[0]
    nb = t // seq
    nc = seq // CHUNK
    row = lambda b: (b, 0)
    c3 = lambda b: (0, 0, 0)
    st = lambda b: (b, 0, 0, 0, 0)
    if s0.shape[0] == 1:
        s0map = lambda b: (0, 0, 0, 0, 0)
    else:
        s0map = st
    return pl.pallas_call(
        _gla_kernel,
        grid=(nb,),
        in_specs=[pl.BlockSpec((seq, GKW), row), pl.BlockSpec((seq, GKW), row),
                  pl.BlockSpec((seq, GVW), row), pl.BlockSpec((seq, 2 * GATE_RANK), row),
                  pl.BlockSpec((seq, GVW), row),
                  pl.BlockSpec((None, 2, 2, LANES, LANES), s0map),
                  pl.BlockSpec((2, 2 * GATE_RANK, GKW), c3), pl.BlockSpec((2, 1, GKW), c3),
                  pl.BlockSpec((1, GLA_DV), lambda b: (0, 0))],
        out_specs=[pl.BlockSpec((seq, GVW), row),
                   pl.BlockSpec((None, 2, 2, LANES, LANES), st)],
        out_shape=[jax.ShapeDtypeStruct((t, GVW), BF16),
                   jax.ShapeDtypeStruct((nb, 2, 2, LANES, LANES), F32)],
        scratch_shapes=[pltpu.VMEM((2, seq, GKW), F32),
                        pltpu.VMEM((2, nc, 2, LANES, LANES), F32),
                        pltpu.VMEM((2, nc, 1, GKW), F32)],
        compiler_params=_cparams(("parallel",)),
        name="gla_lat" if seq > 256 else "gla_ctx",
    )(qg, kg, vg, lr, rs, s0, wg, bg, gn)


ROW_SUB = D // LANES


def _store_row_slabs(ref, x, tmp):
    m = x.shape[0]
    for c in range(ROW_SUB):
        tmp[pl.ds(c, m, stride=ROW_SUB), :] = x[:, LANES * c:LANES * (c + 1)]
    ref[...] = tmp[...].reshape(m, ROW_SUB, LANES).astype(BF16)


def _load_row_slabs(ref, tmp):
    m = ref.shape[0]
    tmp[...] = ref[...].astype(F32).reshape(m * ROW_SUB, LANES)
    return jnp.concatenate([tmp[pl.ds(c, m, stride=ROW_SUB), :] for c in range(ROW_SUB)], axis=1)


def _row_slab(ref, row):
    return ref.at[pl.ds(row, 1)]


def _post_kernel(xc_ref, xl_ref, ac_ref, al_ref, gc_ref, gl_ref, mod_ref,
                 wmg_ref, wba_ref, wbg_ref, wo_ref, l1g_ref, l1b_ref, wr_ref, br_ref,
                 x1_ref, h2_ref, rt_ref, ert_ref, cnt_ref, run_s, slab_s, *, n_ctx_tiles):
    i = pl.program_id(0)
    tm = xc_ref.shape[0]
    sub = ROW_GROUP
    n_groups = tm // sub
    is_ctx = i < n_ctx_tiles
    sh1 = mod_ref[0, :, 0:D]
    sc1 = mod_ref[0, :, D:2 * D]
    g1 = mod_ref[0, :, 2 * D:3 * D]
    sh2 = mod_ref[0, :, 3 * D:4 * D]
    sc2 = mod_ref[0, :, 4 * D:5 * D]

    parts = [slice(g * sub, (g + 1) * sub) for g in range(n_groups)]
    xs = [jnp.where(is_ctx, xc_ref[r, :], xl_ref[r, :]) for r in parts]
    hs = [(_ln(x) * (1.0 + sc1) + sh1).astype(BF16) for x in xs]
    gates = [jax.nn.sigmoid(_dot(h, wmg_ref[...])) for h in hs]
    ba = [_dot(jnp.where(is_ctx, ac_ref[r, :], al_ref[r, :]), wba_ref[...]) for r in parts]
    bg = [_dot(jnp.where(is_ctx, gc_ref[r, :], gl_ref[r, :]), wbg_ref[...]) for r in parts]
    merged = [(g[:, :D] * a + g[:, D:] * b).astype(BF16) for g, a, b in zip(gates, ba, bg)]
    mix = [_dot(m, wo_ref[...]) for m in merged]
    x1s = [_ln(ALPHA * x + g1 * m) * l1g_ref[...] + l1b_ref[...] for x, m in zip(xs, mix)]
    h2s = [_ln(x1) * (1.0 + sc2) + sh2 for x1 in x1s]
    logits = [_dot(h2.astype(BF16), wr_ref[...]) + br_ref[...] for h2 in h2s]
    for g, r in enumerate(parts):
        x1_ref[r, :] = x1s[g]
        _store_row_slabs(h2_ref.at[pl.ds(g * sub, sub)], h2s[g], slab_s)

    @pl.when(i == 0)
    def _():
        run_s[...] = jnp.zeros_like(run_s)

    ri = lax.broadcasted_iota(jnp.int32, (sub, sub), 0)
    ci = lax.broadcasted_iota(jnp.int32, (sub, sub), 1)
    earlier = (ri > ci).astype(BF16)
    run = run_s[0:1, :]
    for g, r in enumerate(parts):
        run = _route(logits[g], earlier, run, rt_ref.at[r, :], ert_ref.at[:, r])
    run_s[0:1, :] = run
    cnt_ref[...] = jnp.broadcast_to(run, cnt_ref.shape)


def _route(logit, earlier, run, rt_ref, ert_ref):
    tm = logit.shape[0]
    lane_i = lax.broadcasted_iota(jnp.int32, (tm, LANES), 1)
    lane = lane_i.astype(F32)
    lane_grp = ((lane_i - N_GROUPS) >> 3).astype(F32)
    neg = jnp.float32(-jnp.inf)
    far = jnp.float32(LANES)
    is_g = lane_i < N_GROUPS
    lg = jnp.where(is_g, logit, neg)
    mg = jnp.max(lg, axis=-1, keepdims=True)
    pg_top = 1.0 / jnp.sum(jnp.where(is_g, jnp.exp(logit - mg), 0.0), axis=-1, keepdims=True)
    g_idx = jnp.min(jnp.where(lg == mg, lane, far), axis=-1, keepdims=True)
    in_grp = (lane_i >= N_GROUPS) & (lane_i < N_GROUPS + N_EXP) & (lane_grp == g_idx)
    le = jnp.where(in_grp, logit, neg)
    v1 = jnp.max(le, axis=-1, keepdims=True)
    i1 = jnp.min(jnp.where(le == v1, lane, far), axis=-1, keepdims=True)
    le2 = jnp.where(lane == i1, neg, le)
    v2 = jnp.max(le2, axis=-1, keepdims=True)
    i2 = jnp.min(jnp.where(le2 == v2, lane, far), axis=-1, keepdims=True)
    e1 = i1 - N_GROUPS
    e2 = i2 - N_GROUPS
    tt = jnp.exp(v2 - v1)
    w1 = pg_top / (1.0 + tt)
    w2 = pg_top * tt / (1.0 + tt)

    hot = ((lane == e1) | (lane == e2)).astype(F32)
    before = _dot(earlier, hot.astype(BF16)) + run
    r1 = jnp.sum(jnp.where(lane == e1, before, 0.0), axis=-1, keepdims=True)
    r2 = jnp.sum(jnp.where(lane == e2, before, 0.0), axis=-1, keepdims=True)

    rt = jnp.where(lane_i == 0, e1, 0.0)
    rt = jnp.where(lane_i == 1, e2, rt)
    rt = jnp.where(lane_i == 2, w1, rt)
    rt = jnp.where(lane_i == 3, w2, rt)
    rt = jnp.where(lane_i == 4, r1, rt)
    rt = jnp.where(lane_i == 5, r2, rt)
    rt_ref[...] = rt
    ert_ref[...] = rt.T[0:8, :]
    return run + jnp.sum(hot, axis=0, keepdims=True)


def _post(x_ctx, x_lat, a_ctx, a_lat, g_ctx, g_lat, mod_all, seq_lat,
          w_mg, w_ba, w_bg, w_o, l1g, l1b, w_r, b_r):
    t_ctx, t_lat = x_ctx.shape[0], x_lat.shape[0]
    tm = TM_TOK
    nct, nlt = t_ctx // tm, t_lat // tm
    per_seq = seq_lat // tm
    nb_lat = t_lat // seq_lat
    t = t_ctx + t_lat
    cmap = lambda i: (jnp.minimum(i, nct - 1), 0)
    lmap = lambda i: (jnp.maximum(i - nct, 0), 0)
    mmap = lambda i: (jnp.where(i < nct, nb_lat, jnp.maximum(i - nct, 0) // per_seq), 0, 0)
    row = lambda i: (i, 0)
    const = lambda i: (0, 0)
    return pl.pallas_call(
        functools.partial(_post_kernel, n_ctx_tiles=nct),
        grid=(nct + nlt,),
        in_specs=[pl.BlockSpec((tm, D), cmap), pl.BlockSpec((tm, D), lmap),
                  pl.BlockSpec((tm, AW), cmap), pl.BlockSpec((tm, AW), lmap),
                  pl.BlockSpec((tm, GVW), cmap), pl.BlockSpec((tm, GVW), lmap),
                  pl.BlockSpec((1, 1, 6 * D), mmap),
                  pl.BlockSpec((D, MG_WIDTH), const), pl.BlockSpec((AW, D), const),
                  pl.BlockSpec((GVW, D), const), pl.BlockSpec((D, D), const),
                  pl.BlockSpec((1, D), const), pl.BlockSpec((1, D), const),
                  pl.BlockSpec((D, LANES), const), pl.BlockSpec((1, LANES), const)],
        out_specs=[pl.BlockSpec((tm, D), row), pl.BlockSpec((tm, ROW_SUB, LANES), lambda i: (i, 0, 0)),
                   pl.BlockSpec((tm, LANES), row), pl.BlockSpec((8, tm), lambda i: (0, i)),
                   pl.BlockSpec((8, LANES), const)],
        out_shape=[jax.ShapeDtypeStruct((t, D), F32), jax.ShapeDtypeStruct((t, ROW_SUB, LANES), BF16),
                   jax.ShapeDtypeStruct((t, LANES), F32), jax.ShapeDtypeStruct((8, t), F32),
                   jax.ShapeDtypeStruct((8, LANES), F32)],
        scratch_shapes=[pltpu.VMEM((8, LANES), F32), pltpu.VMEM((ROW_GROUP * ROW_SUB, LANES), F32)],
        compiler_params=_cparams(("arbitrary",)),
        name="post",
    )(x_ctx, x_lat, a_ctx, a_lat, g_ctx, g_lat, mod_all, w_mg, w_ba, w_bg, w_o, l1g, l1b, w_r, b_r)


ROW_UNROLL = 8


def _row_copy(src_ref, dst_ref, sem):
    return pltpu.make_async_copy(src_ref, dst_ref, sem)


def _scatter_kernel(pos0_ref, pos1_ref, h_ref, xs_ref, sem, zero_s, zero_sem):
    ts = h_ref.shape[0]
    n_rows = xs_ref.shape[0] - TM_EXP

    @pl.when(pl.program_id(0) == 0)
    def _():
        zero_s[...] = jnp.zeros_like(zero_s)
        pad = _row_copy(zero_s, xs_ref.at[pl.ds(n_rows, TM_EXP)], zero_sem)
        pad.start()
        pad.wait()

    def issue(g, carry):
        r0 = pl.multiple_of(g * ROW_UNROLL, ROW_UNROLL)
        for k in range(ROW_UNROLL):
            src = _row_slab(h_ref, r0 + k)
            _row_copy(src, _row_slab(xs_ref, pos0_ref[0, r0 + k]), sem).start(priority=0)
            _row_copy(src, _row_slab(xs_ref, pos1_ref[0, r0 + k]), sem).start(priority=1)
        return carry

    lax.fori_loop(0, ts // ROW_UNROLL, issue, 0)
    for _ in range(2):
        _row_copy(h_ref, xs_ref.at[pl.ds(0, ts)], sem).wait()


def _scatter_rows(h2p, pos0, pos1):
    t = h2p.shape[0]
    ts = TS_ROWS
    smem = lambda: pl.BlockSpec((None, 1, ts), lambda i: (i, 0, 0), memory_space=pltpu.SMEM)
    return pl.pallas_call(
        _scatter_kernel,
        grid=(t // ts,),
        in_specs=[smem(), smem(), pl.BlockSpec((ts, ROW_SUB, LANES), lambda i: (i, 0, 0))],
        out_specs=pl.BlockSpec(memory_space=pl.ANY),
        out_shape=jax.ShapeDtypeStruct((2 * t + TM_EXP, ROW_SUB, LANES), BF16),
        scratch_shapes=[pltpu.SemaphoreType.DMA(()), pltpu.VMEM((TM_EXP, ROW_SUB, LANES), BF16),
                        pltpu.SemaphoreType.DMA(())],
        compiler_params=_cparams(("arbitrary",)),
        name="scatter",
    )(pos0.reshape(t // ts, 1, ts), pos1.reshape(t // ts, 1, ts), h2p)


def _expert_kernel(start_ref, nwin_ref, xs_ref, wg_ref, wu_ref, wd_ref, ys_ref,
                   wgu_s, wd_s, in_buf, out_buf, slab_s, in_sem, out_sem):
    e = pl.program_id(0)
    tm = TM_EXP
    wgu_s[:, 0:D_EXP] = wg_ref[...].astype(BF16)
    wgu_s[:, D_EXP:2 * D_EXP] = wu_ref[...].astype(BF16)
    wd_s[...] = wd_ref[...].astype(BF16)
    base = start_ref[e]
    n = nwin_ref[e]

    @pl.when(e == 0)
    def _():
        out_buf[0] = jnp.zeros(out_buf.shape[1:], BF16)
        pad = pltpu.make_async_copy(out_buf.at[0], ys_ref.at[pl.ds(ys_ref.shape[0] - tm, tm)], out_sem.at[0])
        pad.start()
        pad.wait()

    def read(w, slot):
        return pltpu.make_async_copy(xs_ref.at[pl.ds(base + w * tm, tm)], in_buf.at[slot], in_sem.at[slot])

    def write(w, slot):
        return pltpu.make_async_copy(out_buf.at[slot], ys_ref.at[pl.ds(base + w * tm, tm)], out_sem.at[slot])

    for a in range(READ_AHEAD):
        @pl.when(n > a)
        def _(a=a):
            read(a, a).start()

    def body(w, carry):
        slot = w % 2
        rslot = w % (READ_AHEAD + 1)

        @pl.when(w + READ_AHEAD < n)
        def _():
            read(w + READ_AHEAD, (w + READ_AHEAD) % (READ_AHEAD + 1)).start()

        read(w, rslot).wait()

        @pl.when(w >= 2)
        def _():
            write(w - 2, slot).wait()

        gu = _dot(_load_row_slabs(in_buf.at[rslot], slab_s).astype(BF16), wgu_s[...])
        hid = _silu(gu[:, 0:D_EXP]) * gu[:, D_EXP:2 * D_EXP]
        _store_row_slabs(out_buf.at[slot], _dot(hid.astype(BF16), wd_s[...]), slab_s)
        write(w, slot).start()
        return carry

    lax.fori_loop(0, n, body, 0)

    @pl.when(n >= 1)
    def _():
        write(n - 1, (n - 1) % 2).wait()

    @pl.when(n >= 2)
    def _():
        write(n - 2, n % 2).wait()


def _experts(xs, starts, n_win, w_gate, w_up, w_down):
    tm = TM_EXP
    wmap = lambda e, st, nw: (e, 0, 0)
    slab = (tm, ROW_SUB, LANES)
    return pl.pallas_call(
        _expert_kernel,
        grid_spec=pltpu.PrefetchScalarGridSpec(
            num_scalar_prefetch=2,
            grid=(N_EXP,),
            in_specs=[pl.BlockSpec(memory_space=pl.ANY),
                      pl.BlockSpec((None, D, D_EXP), wmap), pl.BlockSpec((None, D, D_EXP), wmap),
                      pl.BlockSpec((None, D_EXP, D), wmap)],
            out_specs=pl.BlockSpec(memory_space=pl.ANY),
            scratch_shapes=[pltpu.VMEM((D, 2 * D_EXP), BF16), pltpu.VMEM((D_EXP, D), BF16),
                            pltpu.VMEM((READ_AHEAD + 1,) + slab, BF16), pltpu.VMEM((2,) + slab, BF16),
                            pltpu.VMEM((tm * ROW_SUB, LANES), F32),
                            pltpu.SemaphoreType.DMA((READ_AHEAD + 1,)), pltpu.SemaphoreType.DMA((2,))]),
        out_shape=jax.ShapeDtypeStruct(xs.shape, BF16),
        compiler_params=_cparams(("arbitrary",)),
        name="experts",
    )(starts, n_win, xs, w_gate, w_up, w_down)


def _final_kernel(p0c_ref, p1c_ref, p0n_ref, p1n_ref, x1_ref, rt_ref, mod_ref, l2g_ref, l2b_ref, ys_ref,
                  oc_ref, ol_ref, buf, sem, slab_s, *, n_ctx_tiles):
    i = pl.program_id(0)
    n = pl.num_programs(0)
    tm = x1_ref.shape[0]

    def gather(p0_ref, p1_ref, slot):
        def issue(g, carry):
            r0 = pl.multiple_of(g * ROW_UNROLL, ROW_UNROLL)
            for k in range(ROW_UNROLL):
                _row_copy(_row_slab(ys_ref, p0_ref[0, r0 + k]),
                          _row_slab(buf.at[slot, 0], r0 + k), sem.at[slot]).start(priority=0)
                _row_copy(_row_slab(ys_ref, p1_ref[0, r0 + k]),
                          _row_slab(buf.at[slot, 1], r0 + k), sem.at[slot]).start(priority=1)
            return carry

        lax.fori_loop(0, tm // ROW_UNROLL, issue, 0)

    cur = i % 2

    @pl.when(i == 0)
    def _():
        gather(p0c_ref, p1c_ref, 0)

    @pl.when(i + 1 < n)
    def _():
        gather(p0n_ref, p1n_ref, 1 - cur)

    for k in range(2):
        _row_copy(ys_ref.at[pl.ds(0, tm)], buf.at[cur, k], sem.at[cur]).wait()

    g2 = mod_ref[0, :, 5 * D:6 * D]
    w1 = rt_ref[:, 2:3]
    w2 = rt_ref[:, 3:4]
    moe = (w1 * _load_row_slabs(buf.at[cur, 0], slab_s.at[0])
           + w2 * _load_row_slabs(buf.at[cur, 1], slab_s.at[1]))
    out = _ln(ALPHA * x1_ref[...] + g2 * moe) * l2g_ref[...] + l2b_ref[...]

    @pl.when(i < n_ctx_tiles)
    def _():
        oc_ref[...] = out

    @pl.when(i >= n_ctx_tiles)
    def _():
        ol_ref[...] = out


def _final(x1, rt, pos0, pos1, mod_all, l2g, l2b, ys, t_ctx, seq_lat):
    t = x1.shape[0]
    tm = TM_TOK
    nt = t // tm
    nct = t_ctx // tm
    t_lat = t - t_ctx
    per_seq = seq_lat // tm
    nb_lat = t_lat // seq_lat
    p0 = pos0.reshape(nt, 1, tm)
    p1 = pos1.reshape(nt, 1, tm)
    row = lambda i: (i, 0)
    const = lambda i: (0, 0)
    mmap = lambda i: (jnp.where(i < nct, nb_lat, jnp.maximum(i - nct, 0) // per_seq), 0, 0)
    smem_cur = lambda: pl.BlockSpec((None, 1, tm), lambda i: (i, 0, 0), memory_space=pltpu.SMEM)
    smem_nxt = lambda: pl.BlockSpec((None, 1, tm), lambda i: (jnp.minimum(i + 1, nt - 1), 0, 0),
                                    memory_space=pltpu.SMEM)
    return pl.pallas_call(
        functools.partial(_final_kernel, n_ctx_tiles=nct),
        grid=(nt,),
        in_specs=[smem_cur(), smem_cur(), smem_nxt(), smem_nxt(),
                  pl.BlockSpec((tm, D), row), pl.BlockSpec((tm, LANES), row),
                  pl.BlockSpec((1, 1, 6 * D), mmap),
                  pl.BlockSpec((1, D), const), pl.BlockSpec((1, D), const),
                  pl.BlockSpec(memory_space=pl.ANY)],
        out_specs=[pl.BlockSpec((tm, D), lambda i: (jnp.minimum(i, nct - 1), 0)),
                   pl.BlockSpec((tm, D), lambda i: (jnp.maximum(i - nct, 0), 0))],
        out_shape=[jax.ShapeDtypeStruct((t_ctx, D), F32), jax.ShapeDtypeStruct((t_lat, D), F32)],
        scratch_shapes=[pltpu.VMEM((2, 2, tm, ROW_SUB, LANES), BF16), pltpu.SemaphoreType.DMA((2,)),
                        pltpu.VMEM((2, tm * ROW_SUB, LANES), F32)],
        compiler_params=_cparams(("arbitrary",)),
        name="final",
    )(p0, p1, p0, p1, x1, rt, mod_all, l2g, l2b, ys)


def _reorder_q_heads(w, axis):
    shape = w.shape
    split = shape[:axis] + (N_KV_HEADS, N_Q_HEADS // N_KV_HEADS, HD) + shape[axis + 1:]
    return jnp.swapaxes(w.reshape(split), axis, axis + 1).reshape(shape)


def _rope_tables(seq):
    t = np.arange(seq)
    half = HD // 4
    inv = (ROPE_THETA ** (-np.arange(half, dtype=np.float64) / half)).astype(np.float32)
    d64 = np.arange(LANES) % HD
    pos = np.where((d64 < HD // 2)[None, :], (t // GRID_W)[:, None], (t % GRID_W)[:, None])
    ang = (pos.astype(np.float32) * inv[d64 % half][None, :]).astype(np.float64)
    sign = np.where((d64 % 32) < 16, -1.0, 1.0)
    return (jnp.asarray(np.cos(ang), F32), jnp.asarray(np.sin(ang) * sign[None, :], F32))


def _pair_states(s):
    b = s.shape[0]
    s = s.reshape(b, 2, 2, GLA_DK, GLA_DV)
    return s.transpose(0, 1, 4, 2, 3).reshape(b, 2, GLA_DV, 2 * GLA_DK)


def _unpair_states(s):
    b = s.shape[0]
    s = s.reshape(b, 2, GLA_DV, 2, GLA_DK)
    return s.transpose(0, 1, 3, 4, 2).reshape(b, GLA_H, GLA_DK, GLA_DV)


def _route_tables(ert, counts):
    i32 = jnp.int32
    cnt = counts[0, :N_EXP].astype(i32)
    starts = jnp.cumsum(cnt) - cnt
    table = lambda e: jnp.sum(jnp.where(e[None, :] == jnp.arange(N_EXP, dtype=i32)[:, None],
                                        starts[:, None], 0), axis=0)
    pos0 = table(ert[0].astype(i32)) + ert[4].astype(i32)
    pos1 = table(ert[1].astype(i32)) + ert[5].astype(i32)
    return pos0, pos1, starts, (cnt + (TM_EXP - 1)) // TM_EXP


def kernel(x_prompt, x_sample, cache_k, cache_v, state_gla_fwd, state_gla_bwd, c, c_ctx, w_ada, b_ada, w_in, q_norm, k_norm, gla_w_gate, gla_b_gate, gla_norm, w_br_attn, w_br_gla, w_out, ln1_g, ln1_b, router_group_w, router_group_b, router_expert_w, router_expert_b, exp_w_gate, exp_w_up, exp_w_down, ln2_g, ln2_b):
    b_ctx, seq_ctx, _ = x_prompt.shape
    b_lat, seq_lat, _ = x_sample.shape
    t_ctx, t_lat = b_ctx * seq_ctx, b_lat * seq_lat
    t = t_ctx + t_lat
    l = 0

    rows = -(-(b_lat + 1) // 8) * 8
    c_rows = jnp.zeros((rows, D), F32).at[:b_lat].set(c).at[b_lat].set(c_ctx)
    mod = _ada(c_rows, w_ada[l], b_ada[l][None, :])
    mod_all = mod[:b_lat + 1, None, :]
    mod_lat = mod_all[:b_lat]
    mod_ctx = mod_all[b_lat:]

    w_full = w_in[l]
    w_a = jnp.concatenate([_reorder_q_heads(w_full[:, :AW], 1), w_full[:, AW:A_WIDTH]], axis=1).astype(BF16)
    w_mg = w_full[:, A_WIDTH:].astype(BF16)
    gain = jnp.concatenate([jnp.tile(q_norm[l], N_Q_HEADS), jnp.tile(k_norm[l], N_KV_HEADS)])[None, :]
    head_of = np.arange(AW + KVW) // HD
    ind = jnp.asarray((head_of[:, None] == np.arange(LANES)[None, :]) / HD, BF16)
    w_ba = _reorder_q_heads(w_br_attn[l], 0).astype(BF16)
    w_bg = w_br_gla[l].astype(BF16)
    w_o = w_out[l].astype(BF16)
    w_r = jnp.zeros((D, LANES), F32).at[:, :N_GROUPS].set(router_group_w[l])
    w_r = w_r.at[:, N_GROUPS:N_GROUPS + N_EXP].set(router_expert_w[l]).astype(BF16)
    b_r = jnp.zeros((1, LANES), F32).at[0, :N_GROUPS].set(router_group_b[l])
    b_r = b_r.at[0, N_GROUPS:N_GROUPS + N_EXP].set(router_expert_b[l])
    wg = jnp.zeros((2, 2 * GATE_RANK, GKW), F32)
    wg = wg.at[0, :GATE_RANK].set(gla_w_gate[l, 0]).at[1, GATE_RANK:].set(gla_w_gate[l, 1])
    bg = gla_b_gate[l][:, None, :]
    gn = gla_norm[l][None, :]

    xc = x_prompt.reshape(t_ctx, D)
    xl = x_sample.reshape(t_lat, D)

    q_c, k_c, v_c, qg_c, kg_c, vg_c, rs_c, lr_c, kf_c, vf_c = _inproj(
        xc, mod_ctx, w_a, gain, ind, None, seq_ctx, latent=False)
    attn_c = _attention(q_c, k_c, v_c, None, seq_ctx)
    zero_state = jnp.zeros((1, 2, 2, LANES, LANES), F32)
    gla_c, sfin_c = _gla(qg_c, kg_c, vg_c, lr_c, rs_c, zero_state, wg, bg, gn, seq_ctx)

    q_l, k_l, v_l, qg_l, kg_l, vg_l, rs_l, lr_l = _inproj(
        xl, mod_lat, w_a, gain, ind, _rope_tables(seq_lat), seq_lat, latent=True)
    past = cache_k.shape[2]
    kc = cache_k[:, l].reshape(b_lat, past, KVW).astype(BF16)
    vc = cache_v[:, l].reshape(b_lat, past, KVW).astype(BF16)
    attn_l = _attention(q_l, k_l, v_l, (kc, vc), seq_lat)
    s0 = jnp.stack([_pair_states(state_gla_fwd[:, l]), _pair_states(state_gla_bwd[:, l])], axis=1)
    gla_l, _ = _gla(qg_l, kg_l, vg_l, lr_l, rs_l, s0, wg, bg, gn, seq_lat)

    x1, h2p, rt, ert, counts = _post(xc, xl, attn_c, attn_l, gla_c, gla_l, mod_all, seq_lat,
                                w_mg, w_ba, w_bg, w_o, ln1_g[l][None, :], ln1_b[l][None, :], w_r, b_r)

    pos0, pos1, starts, n_win = _route_tables(ert, counts)
    xs = _scatter_rows(h2p, pos0, pos1)
    ys = _experts(xs, starts, n_win, exp_w_gate[l], exp_w_up[l], exp_w_down[l])
    y_ctx, y_lat = _final(x1, rt, pos0, pos1, mod_all, ln2_g[l][None, :], ln2_b[l][None, :], ys,
                          t_ctx, seq_lat)

    untranspose = lambda a: a.reshape(b_ctx, 1, N_KV_HEADS, HD, seq_ctx).transpose(0, 1, 4, 2, 3)
    new_k, new_v = untranspose(kf_c), untranspose(vf_c)
    new_sf = _unpair_states(sfin_c[:, 0])[:, None]
    new_sb = _unpair_states(sfin_c[:, 1])[:, None]
    return (y_ctx.reshape(b_ctx, seq_ctx, D), y_lat.reshape(b_lat, seq_lat, D),
            new_k, new_v, new_sf, new_sb)
```

```python
import functools

import numpy as np
import jax
import jax.numpy as jnp
from jax import lax
from jax.experimental import pallas as pl
from jax.experimental.pallas import tpu as pltpu

F32 = jnp.float32
BF16 = jnp.bfloat16
HIGHEST = lax.Precision.HIGHEST

D = 1024
GRID_W = 64
HD = 64
N_Q_HEADS = 8
N_KV_HEADS = 2
AW = N_Q_HEADS * HD
KVW = N_KV_HEADS * HD
ROPE_THETA = 10000.0
GLA_H = 4
GLA_DK = 64
GLA_DV = 128
GKW = GLA_H * GLA_DK
GVW = GLA_H * GLA_DV
GATE_RANK = 16
GLA_TAU = 16.0
CHUNK = 64
N_GROUPS = 4
EPG = 8
N_EXP = N_GROUPS * EPG
D_EXP = 256
DEPTH = 1
ALPHA = (2.0 * DEPTH) ** 0.25
LN_EPS = 1e-6
RMS_EPS = 1e-6

LANES = 128
A_WIDTH = AW + 2 * KVW + 2 * GKW + 2 * GVW + 2 * GATE_RANK
MG_WIDTH = 2 * D
TM_TOK = 512
ROW_GROUP = 128
LOG2E = 1.4426950408889634
ONES_ROWS = 16
TQ_LAT = 256
TM_EXP = 256
READ_AHEAD = 2
TS_ROWS = 2048
VMEM_LIMIT = 56 * 1024 * 1024


def _cparams(sem):
    return pltpu.CompilerParams(dimension_semantics=sem, vmem_limit_bytes=VMEM_LIMIT)


def _dot(a, b):
    return jnp.dot(a, b, preferred_element_type=F32)


def _dot_nt(a, b):
    return lax.dot_general(a, b, (((1,), (1,)), ((), ())), preferred_element_type=F32)


def _dot_tn(a, b):
    return lax.dot_general(a, b, (((0,), (0,)), ((), ())), preferred_element_type=F32)


def _ln(x):
    mu = jnp.mean(x, axis=-1, keepdims=True)
    xc = x - mu
    var = jnp.mean(xc * xc, axis=-1, keepdims=True)
    return xc * lax.rsqrt(var + LN_EPS)


def _silu(x):
    return x * jax.nn.sigmoid(x)


def _split_bf16(x):
    hi = x.astype(BF16)
    lo = (x - hi.astype(F32)).astype(BF16)
    return hi, lo


def _ada_kernel(c_ref, w_ref, b_ref, o_ref):
    s = _silu(c_ref[...])
    o_ref[...] = jnp.dot(s, w_ref[...], preferred_element_type=F32, precision=HIGHEST) + b_ref[...]


def _ada(c_rows, w_ada, b_ada):
    rows = c_rows.shape[0]
    n = w_ada.shape[1]
    bn = 1024
    return pl.pallas_call(
        _ada_kernel,
        grid=(n // bn,),
        in_specs=[pl.BlockSpec((rows, D), lambda j: (0, 0)),
                  pl.BlockSpec((D, bn), lambda j: (0, j)),
                  pl.BlockSpec((1, bn), lambda j: (0, j))],
        out_specs=pl.BlockSpec((rows, bn), lambda j: (0, j)),
        out_shape=jax.ShapeDtypeStruct((rows, n), F32),
        compiler_params=_cparams(("arbitrary",)),
        name="ada",
    )(c_rows, w_ada, b_ada)


def _inproj_kernel(*refs, latent, seq):
    if latent:
        (x_ref, mod_ref, w_ref, gain_ref, ind_ref, cos_ref, sin_ref,
         q_ref, k_ref, v_ref, qg_ref, kg_ref, vg_ref, rs_ref, lr_ref) = refs
    else:
        (x_ref, mod_ref, w_ref, gain_ref, ind_ref,
         q_ref, k_ref, v_ref, qg_ref, kg_ref, vg_ref, rs_ref, lr_ref, kf_ref, vf_ref) = refs
    tm = x_ref.shape[0]
    sub = ROW_GROUP
    n_groups = tm // sub
    sh1 = mod_ref[0, :, 0:D]
    sc1 = mod_ref[0, :, D:2 * D]
    lane = lax.broadcasted_iota(jnp.int32, (sub, LANES), 1)
    low_half = lane < HD
    first = (lane % 32) < 16

    def project(g):
        rows = slice(g * sub, (g + 1) * sub)
        h = (_ln(x_ref[rows, :]) * (1.0 + sc1) + sh1).astype(BF16)
        return _dot(h, w_ref[...])

    def finish(g, res):
        rows = slice(g * sub, (g + 1) * sub)
        qk = res[:, 0:AW + KVW]
        hi, lo = _split_bf16(qk * qk)
        ms = _dot(hi, ind_ref[...]) + _dot(lo, ind_ref[...])
        r = lax.rsqrt(ms + RMS_EPS)
        for s in range(5):
            rb = jnp.where(low_half, r[:, 2 * s:2 * s + 1], r[:, 2 * s + 1:2 * s + 2])
            y = res[:, LANES * s:LANES * (s + 1)] * rb * gain_ref[:, LANES * s:LANES * (s + 1)]
            if s == 4 and not latent:
                kf_ref[(g * sub) // seq, :, (g * sub) % seq:(g * sub) % seq + sub] = y.T
            if latent:
                partner = jnp.where(first, pltpu.roll(y, LANES - 16, 1), pltpu.roll(y, 16, 1))
                y = y * cos_ref[rows, :] + partner * sin_ref[rows, :]
            if s < 4:
                q_ref[rows, LANES * s:LANES * (s + 1)] = (y * (HD ** -0.5 * LOG2E)).astype(BF16)
            else:
                k_ref[rows, :] = y.astype(BF16)
        o = AW + KVW
        v = res[:, o:o + KVW]
        v_ref[rows, :] = v.astype(BF16)
        if not latent:
            vf_ref[(g * sub) // seq, :, (g * sub) % seq:(g * sub) % seq + sub] = v.T
        o += KVW
        qg_ref[rows, :] = res[:, o:o + GKW] * (GLA_DK ** -0.5)
        o += GKW
        kg_ref[rows, :] = res[:, o:o + GKW]
        o += GKW
        vg_ref[rows, :] = res[:, o:o + GVW].astype(BF16)
        o += GVW
        rs_ref[rows, :] = _silu(res[:, o:o + GVW]).astype(BF16)
        o += GVW
        lr_ref[rows, :] = res[:, o:o + 2 * GATE_RANK]

    pending = {0: project(0)}
    for g in range(n_groups):
        if g + 1 < n_groups:
            pending[g + 1] = project(g + 1)
        finish(g, pending.pop(g))


def _inproj(x2, mod, w_a, gain, ind, rope, seq, latent):
    t = x2.shape[0]
    tm = TM_TOK
    per_seq = max(seq // tm, 1)
    per_tile = max(tm // seq, 1)
    row = lambda i: (i, 0)
    const = lambda i: (0, 0)
    in_specs = [pl.BlockSpec((tm, D), row),
                pl.BlockSpec((1, 1, 6 * D), (lambda i: (i // per_seq, 0, 0)) if latent else (lambda i: (0, 0, 0))),
                pl.BlockSpec((D, A_WIDTH), const),
                pl.BlockSpec((1, AW + KVW), const),
                pl.BlockSpec((AW + KVW, LANES), const)]
    args = [x2, mod, w_a, gain, ind]
    if latent:
        in_specs += [pl.BlockSpec((tm, LANES), lambda i: (i % per_seq, 0))] * 2
        args += list(rope)
    widths = [(AW, BF16), (KVW, BF16), (KVW, BF16), (GKW, F32), (GKW, F32), (GVW, BF16), (GVW, BF16),
              (2 * GATE_RANK, F32)]
    out_specs = [pl.BlockSpec((tm, w), row) for w, _ in widths]
    out_shape = [jax.ShapeDtypeStruct((t, w), dt) for w, dt in widths]
    if not latent:
        cache_spec = pl.BlockSpec((per_tile, KVW, seq), lambda i: (i, 0, 0))
        out_specs += [cache_spec] * 2
        out_shape += [jax.ShapeDtypeStruct((t // seq, KVW, seq), F32)] * 2
    return pl.pallas_call(
        functools.partial(_inproj_kernel, latent=latent, seq=seq),
        grid=(t // tm,),
        in_specs=in_specs,
        out_specs=out_specs,
        out_shape=out_shape,
        compiler_params=_cparams(("parallel",)),
        name="inproj_lat" if latent else "inproj_ctx",
    )(*args)


def _attn_kernel(*refs, has_cache):
    def transposed_with_ones(dst, src):
        dst[0:KVW, :] = src[...].astype(F32).T.astype(BF16)
        dst[KVW:, :] = jnp.ones((ONES_ROWS, dst.shape[1]), BF16)

    if has_cache:
        q_ref, k_ref, v_ref, kc_ref, vc_ref, o_ref, vt_s, vct_s = refs

        @pl.when(pl.program_id(1) == 0)
        def _():
            transposed_with_ones(vt_s, v_ref)
            transposed_with_ones(vct_s, vc_ref)
    else:
        q_ref, k_ref, v_ref, o_ref, vt_s = refs
        transposed_with_ones(vt_s, v_ref)
    tq = q_ref.shape[0]

    lane = lax.broadcasted_iota(jnp.int32, (tq, LANES), 1)
    low_half = lane < HD
    k = k_ref[...]
    scores = []
    for j in range(N_KV_HEADS):
        keep = low_half if j == 0 else jnp.logical_not(low_half)
        zero = jnp.zeros((tq, LANES), BF16)
        for pair in range(2):
            qs = jnp.concatenate([jnp.where(keep, q_ref[:, LANES * s:LANES * (s + 1)], zero)
                                  for s in (2 * pair, 2 * pair + 1)], axis=0)
            s1 = _dot_nt(k, qs)
            s2 = _dot_nt(kc_ref[...], qs) if has_cache else None
            scores.append((s1, s2))
    outs = []
    for s1, s2 in scores:
        m = jnp.max(s1, axis=0, keepdims=True)
        if has_cache:
            m = jnp.maximum(m, jnp.max(s2, axis=0, keepdims=True))
        acc = _dot(vt_s[...], jnp.exp2(s1 - m).astype(BF16))
        if has_cache:
            acc = acc + _dot(vct_s[...], jnp.exp2(s2 - m).astype(BF16))
        outs.append(acc[0:KVW] / acc[KVW:KVW + 1])
    head0 = jnp.concatenate(outs[0:2], axis=1)
    head1 = jnp.concatenate(outs[2:4], axis=1)
    row = lax.broadcasted_iota(jnp.int32, (LANES, 4 * tq), 0)
    out = jnp.where(row < HD, head0, head1).T
    for s in range(4):
        o_ref[:, LANES * s:LANES * (s + 1)] = out[s * tq:(s + 1) * tq].astype(BF16)


def _attention(q, k, v, cache, seq):
    t = q.shape[0]
    if cache is None:
        tq = seq
        grid = (t // seq,)
        qmap = lambda b: (b, 0)
        in_specs = [pl.BlockSpec((tq, AW), qmap), pl.BlockSpec((seq, KVW), qmap),
                    pl.BlockSpec((seq, KVW), qmap)]
        args = [q, k, v]
        scratch = [pltpu.VMEM((KVW + ONES_ROWS, seq), BF16)]
        sem = ("parallel",)
        name = "attn_ctx"
    else:
        tq = TQ_LAT
        nq = seq // tq
        kc, vc = cache
        past = kc.shape[1]
        grid = (t // seq, nq)
        qmap = lambda b, i: (b * nq + i, 0)
        kmap = lambda b, i: (b, 0)
        cmap = lambda b, i: (b, 0, 0)
        in_specs = [pl.BlockSpec((tq, AW), qmap), pl.BlockSpec((seq, KVW), kmap),
                    pl.BlockSpec((seq, KVW), kmap),
                    pl.BlockSpec((None, past, KVW), cmap), pl.BlockSpec((None, past, KVW), cmap)]
        args = [q, k, v, kc, vc]
        scratch = [pltpu.VMEM((KVW + ONES_ROWS, seq), BF16), pltpu.VMEM((KVW + ONES_ROWS, past), BF16)]
        sem = ("parallel", "arbitrary")
        name = "attn_lat"
    return pl.pallas_call(
        functools.partial(_attn_kernel, has_cache=cache is not None),
        grid=grid,
        in_specs=in_specs,
        out_specs=pl.BlockSpec((tq, AW), qmap),
        out_shape=jax.ShapeDtypeStruct((t, AW), BF16),
        scratch_shapes=scratch,
        compiler_params=_cparams(sem),
        name=name,
    )(*args)


GLA_BLK = 256
GLA_UNROLL = 4


def _split3_bf16(x):
    hi = x.astype(BF16)
    r1 = x - hi.astype(F32)
    mid = r1.astype(BF16)
    lo = (r1 - mid.astype(F32)).astype(BF16)
    return hi, mid, lo


def _gla_kernel(qg_ref, kg_ref, vg_ref, lr_ref, rs_ref, s0_ref, wg_ref, bg_ref, gn_ref,
                o_ref, sfin_ref, cum_s, kv_s, dec_s):
    n = qg_ref.shape[0]
    nc = n // CHUNK
    lane = lax.broadcasted_iota(jnp.int32, (CHUNK, LANES), 1)
    low_half = lane < GLA_DK
    lane_sq = lax.broadcasted_iota(jnp.int32, (LANES, LANES), 1)
    low_half_sq = lane_sq < GLA_DK
    ri = lax.broadcasted_iota(jnp.int32, (2 * CHUNK, 2 * CHUNK), 0)
    ci = lax.broadcasted_iota(jnp.int32, (2 * CHUNK, 2 * CHUNK), 1)
    diag = (ri >> 6) == (ci >> 6)
    keep2 = (diag & (ri >= ci), diag & (ci >= ri))

    rb = lax.broadcasted_iota(jnp.int32, (GLA_BLK, GLA_BLK), 0)
    cb = lax.broadcasted_iota(jnp.int32, (GLA_BLK, GLA_BLK), 1)
    same = (rb >> 6) == (cb >> 6)
    tri = ((same & (rb >= cb)).astype(BF16), (same & (cb >= rb)).astype(BF16))
    for d in range(2):
        w_hi, w_lo = _split_bf16(wg_ref[d])
        for blk in range(n // GLA_BLK):
            rows = slice(blk * GLA_BLK, (blk + 1) * GLA_BLK)
            l_hi, l_lo = _split_bf16(lr_ref[rows, :])
            z = _dot(l_hi, w_hi) + _dot(l_lo, w_hi) + _dot(l_hi, w_lo) + bg_ref[d]
            logg = (jnp.minimum(z, 0.0) - jnp.log(1.0 + jnp.exp(-jnp.abs(z)))) * (1.0 / GLA_TAU)
            pieces = _split3_bf16(logg)
            cum_s[d, rows, :] = _dot(tri[d], pieces[0]) + _dot(tri[d], pieces[1]) + _dot(tri[d], pieces[2])

    def pass1(i, carry):
        cs = [i * GLA_UNROLL + u for u in range(GLA_UNROLL)]
        rws = [pl.ds(pl.multiple_of(c * CHUNK, CHUNK), CHUNK) for c in cs]
        prods = []
        for c, rows in zip(cs, rws):
            kc = kg_ref[rows, :]
            kdec = []
            for d in range(2):
                cum = cum_s[d, rows, :]
                last = cum[CHUNK - 1:CHUNK, :] if d == 0 else cum[0:1, :]
                dec_s[d, c] = jnp.exp(last)
                kdec.append((kc * jnp.exp(last - cum)).astype(BF16))
            for p in range(2):
                vpair = vg_ref[rows, GLA_DV * 2 * p:GLA_DV * 2 * (p + 1)]
                kpair = jnp.concatenate([kdec[0][:, LANES * p:LANES * (p + 1)],
                                         kdec[1][:, LANES * p:LANES * (p + 1)]], axis=1)
                prods.append(_dot_tn(vpair, kpair))
        for j, c in enumerate(cs):
            for p in range(2):
                res = prods[2 * j + p]
                for d in range(2):
                    cols = slice(LANES * d, LANES * (d + 1))
                    kv_s[d, c, p] = jnp.where(low_half_sq, res[0:GLA_DV, cols], res[GLA_DV:2 * GLA_DV, cols])
        return carry

    lax.fori_loop(0, nc // GLA_UNROLL, pass1, 0)

    for d in range(2):
        def scan(i, st):
            c = i if d == 0 else nc - 1 - i
            dec = dec_s[d, c]
            new = []
            for p in range(2):
                kv = kv_s[d, c, p]
                kv_s[d, c, p] = st[p]
                new.append(st[p] * dec[:, LANES * p:LANES * (p + 1)] + kv)
            return tuple(new)

        fin = lax.fori_loop(0, nc, scan, (s0_ref[d, 0], s0_ref[d, 1]))
        sfin_ref[d, 0] = fin[0]
        sfin_ref[d, 1] = fin[1]

    def pass3(i, carry):
        cs = [i * GLA_UNROLL + u for u in range(GLA_UNROLL)]
        rws = [pl.ds(pl.multiple_of(c * CHUNK, CHUNK), CHUNK) for c in cs]
        first = []
        for c, rows in zip(cs, rws):
            q = qg_ref[rows, :]
            k = kg_ref[rows, :]
            for d in range(2):
                cum = cum_s[d, rows, :]
                qt = q * jnp.exp(cum)
                kt = (k * jnp.exp(-cum)).astype(BF16)
                for p in range(2):
                    qs = qt[:, LANES * p:LANES * (p + 1)]
                    lhs = jnp.concatenate([jnp.where(low_half, qs, 0.0), jnp.where(low_half, 0.0, qs)],
                                          axis=0).astype(BF16)
                    kts = kt[:, LANES * p:LANES * (p + 1)]
                    rhs = jnp.concatenate([kts, kts, kv_s[d, c, p].astype(BF16)], axis=0)
                    first.append(_dot_nt(lhs, rhs))
        second = []
        for j, rows in enumerate(rws):
            for d in range(2):
                for p in range(2):
                    res = first[4 * j + 2 * d + p]
                    vp = jnp.concatenate([vg_ref[rows, GLA_DV * (2 * p):GLA_DV * (2 * p + 1)],
                                          vg_ref[rows, GLA_DV * (2 * p + 1):GLA_DV * (2 * p + 2)]], axis=0)
                    a = jnp.where(keep2[d], res[:, 0:2 * CHUNK], 0.0).astype(BF16)
                    second.append(_dot(a, vp) + res[:, 2 * CHUNK:])
        for j, rows in enumerate(rws):
            for p in range(2):
                tot = second[4 * j + p] + second[4 * j + 2 + p]
                y = tot * lax.rsqrt(jnp.mean(tot * tot, axis=-1, keepdims=True) + RMS_EPS) * gn_ref[...]
                for hh in range(2):
                    cols = slice(GLA_DV * (2 * p + hh), GLA_DV * (2 * p + hh + 1))
                    o_ref[rows, cols] = (y[CHUNK * hh:CHUNK * (hh + 1)]
                                         * rs_ref[rows, cols].astype(F32)).astype(BF16)
        return carry

    lax.fori_loop(0, nc // GLA_UNROLL, pass3, 0)


def _gla(qg, kg, vg, lr, rs, s0, wg, bg, gn, seq):
    t = qg.shape[0]
    nb = t // seq
    nc = seq // CHUNK
    row = lambda b: (b, 0)
    c3 = lambda b: (0, 0, 0)
    st = lambda b: (b, 0, 0, 0, 0)
    if s0.shape[0] == 1:
        s0map = lambda b: (0, 0, 0, 0, 0)
    else:
        s0map = st
    return pl.pallas_call(
        _gla_kernel,
        grid=(nb,),
        in_specs=[pl.BlockSpec((seq, GKW), row), pl.BlockSpec((seq, GKW), row),
                  pl.BlockSpec((seq, GVW), row), pl.BlockSpec((seq, 2 * GATE_RANK), row),
                  pl.BlockSpec((seq, GVW), row),
                  pl.BlockSpec((None, 2, 2, LANES, LANES), s0map),
                  pl.BlockSpec((2, 2 * GATE_RANK, GKW), c3), pl.BlockSpec((2, 1, GKW), c3),
                  pl.BlockSpec((1, GLA_DV), lambda b: (0, 0))],
        out_specs=[pl.BlockSpec((seq, GVW), row),
                   pl.BlockSpec((None, 2, 2, LANES, LANES), st)],
        out_shape=[jax.ShapeDtypeStruct((t, GVW), BF16),
                   jax.ShapeDtypeStruct((nb, 2, 2, LANES, LANES), F32)],
        scratch_shapes=[pltpu.VMEM((2, seq, GKW), F32),
                        pltpu.VMEM((2, nc, 2, LANES, LANES), F32),
                        pltpu.VMEM((2, nc, 1, GKW), F32)],
        compiler_params=_cparams(("parallel",)),
        name="gla_lat" if seq > 256 else "gla_ctx",
    )(qg, kg, vg, lr, rs, s0, wg, bg, gn)


ROW_SUB = D // LANES


def _store_row_slabs(ref, x, tmp):
    m = x.shape[0]
    for c in range(ROW_SUB):
        tmp[pl.ds(c, m, stride=ROW_SUB), :] = x[:, LANES * c:LANES * (c + 1)]
    ref[...] = tmp[...].reshape(m, ROW_SUB, LANES).astype(BF16)


def _load_row_slabs(ref, tmp):
    m = ref.shape[0]
    tmp[...] = ref[...].astype(F32).reshape(m * ROW_SUB, LANES)
    return jnp.concatenate([tmp[pl.ds(c, m, stride=ROW_SUB), :] for c in range(ROW_SUB)], axis=1)


def _row_slab(ref, row):
    return ref.at[pl.ds(row, 1)]


def _post_kernel(xc_ref, xl_ref, ac_ref, al_ref, gc_ref, gl_ref, mod_ref,
                 wmg_ref, wba_ref, wbg_ref, wo_ref, l1g_ref, l1b_ref, wr_ref, br_ref,
                 x1_ref, h2_ref, rt_ref, ert_ref, cnt_ref, run_s, slab_s, *, n_ctx_tiles):
    i = pl.program_id(0)
    tm = xc_ref.shape[0]
    sub = ROW_GROUP
    n_groups = tm // sub
    is_ctx = i < n_ctx_tiles
    sh1 = mod_ref[0, :, 0:D]
    sc1 = mod_ref[0, :, D:2 * D]
    g1 = mod_ref[0, :, 2 * D:3 * D]
    sh2 = mod_ref[0, :, 3 * D:4 * D]
    sc2 = mod_ref[0, :, 4 * D:5 * D]

    parts = [slice(g * sub, (g + 1) * sub) for g in range(n_groups)]
    xs = [jnp.where(is_ctx, xc_ref[r, :], xl_ref[r, :]) for r in parts]
    hs = [(_ln(x) * (1.0 + sc1) + sh1).astype(BF16) for x in xs]
    gates = [jax.nn.sigmoid(_dot(h, wmg_ref[...])) for h in hs]
    ba = [_dot(jnp.where(is_ctx, ac_ref[r, :], al_ref[r, :]), wba_ref[...]) for r in parts]
    bg = [_dot(jnp.where(is_ctx, gc_ref[r, :], gl_ref[r, :]), wbg_ref[...]) for r in parts]
    merged = [(g[:, :D] * a + g[:, D:] * b).astype(BF16) for g, a, b in zip(gates, ba, bg)]
    mix = [_dot(m, wo_ref[...]) for m in merged]
    x1s = [_ln(ALPHA * x + g1 * m) * l1g_ref[...] + l1b_ref[...] for x, m in zip(xs, mix)]
    h2s = [_ln(x1) * (1.0 + sc2) + sh2 for x1 in x1s]
    logits = [_dot(h2.astype(BF16), wr_ref[...]) + br_ref[...] for h2 in h2s]
    for g, r in enumerate(parts):
        x1_ref[r, :] = x1s[g]
        _store_row_slabs(h2_ref.at[pl.ds(g * sub, sub)], h2s[g], slab_s)

    @pl.when(i == 0)
    def _():
        run_s[...] = jnp.zeros_like(run_s)

    ri = lax.broadcasted_iota(jnp.int32, (sub, sub), 0)
    ci = lax.broadcasted_iota(jnp.int32, (sub, sub), 1)
    earlier = (ri > ci).astype(BF16)
    run = run_s[0:1, :]
    for g, r in enumerate(parts):
        run = _route(logits[g], earlier, run, rt_ref.at[r, :], ert_ref.at[:, r])
    run_s[0:1, :] = run
    cnt_ref[...] = jnp.broadcast_to(run, cnt_ref.shape)


def _route(logit, earlier, run, rt_ref, ert_ref):
    tm = logit.shape[0]
    lane_i = lax.broadcasted_iota(jnp.int32, (tm, LANES), 1)
    lane = lane_i.astype(F32)
    lane_grp = ((lane_i - N_GROUPS) >> 3).astype(F32)
    neg = jnp.float32(-jnp.inf)
    far = jnp.float32(LANES)
    is_g = lane_i < N_GROUPS
    lg = jnp.where(is_g, logit, neg)
    mg = jnp.max(lg, axis=-1, keepdims=True)
    pg_top = 1.0 / jnp.sum(jnp.where(is_g, jnp.exp(logit - mg), 0.0), axis=-1, keepdims=True)
    g_idx = jnp.min(jnp.where(lg == mg, lane, far), axis=-1, keepdims=True)
    in_grp = (lane_i >= N_GROUPS) & (lane_i < N_GROUPS + N_EXP) & (lane_grp == g_idx)
    le = jnp.where(in_grp, logit, neg)
    v1 = jnp.max(le, axis=-1, keepdims=True)
    i1 = jnp.min(jnp.where(le == v1, lane, far), axis=-1, keepdims=True)
    le2 = jnp.where(lane == i1, neg, le)
    v2 = jnp.max(le2, axis=-1, keepdims=True)
    i2 = jnp.min(jnp.where(le2 == v2, lane, far), axis=-1, keepdims=True)
    e1 = i1 - N_GROUPS
    e2 = i2 - N_GROUPS
    tt = jnp.exp(v2 - v1)
    w1 = pg_top / (1.0 + tt)
    w2 = pg_top * tt / (1.0 + tt)

    hot = ((lane == e1) | (lane == e2)).astype(F32)
    before = _dot(earlier, hot.astype(BF16)) + run
    r1 = jnp.sum(jnp.where(lane == e1, before, 0.0), axis=-1, keepdims=True)
    r2 = jnp.sum(jnp.where(lane == e2, before, 0.0), axis=-1, keepdims=True)

    rt = jnp.where(lane_i == 0, e1, 0.0)
    rt = jnp.where(lane_i == 1, e2, rt)
    rt = jnp.where(lane_i == 2, w1, rt)
    rt = jnp.where(lane_i == 3, w2, rt)
    rt = jnp.where(lane_i == 4, r1, rt)
    rt = jnp.where(lane_i == 5, r2, rt)
    rt_ref[...] = rt
    ert_ref[...] = rt.T[0:8, :]
    return run + jnp.sum(hot, axis=0, keepdims=True)


def _post(x_ctx, x_lat, a_ctx, a_lat, g_ctx, g_lat, mod_all, seq_lat,
          w_mg, w_ba, w_bg, w_o, l1g, l1b, w_r, b_r):
    t_ctx, t_lat = x_ctx.shape[0], x_lat.shape[0]
    tm = TM_TOK
    nct, nlt = t_ctx // tm, t_lat // tm
    per_seq = seq_lat // tm
    nb_lat = t_lat // seq_lat
    t = t_ctx + t_lat
    cmap = lambda i: (jnp.minimum(i, nct - 1), 0)
    lmap = lambda i: (jnp.maximum(i - nct, 0), 0)
    mmap = lambda i: (jnp.where(i < nct, nb_lat, jnp.maximum(i - nct, 0) // per_seq), 0, 0)
    row = lambda i: (i, 0)
    const = lambda i: (0, 0)
    return pl.pallas_call(
        functools.partial(_post_kernel, n_ctx_tiles=nct),
        grid=(nct + nlt,),
        in_specs=[pl.BlockSpec((tm, D), cmap), pl.BlockSpec((tm, D), lmap),
                  pl.BlockSpec((tm, AW), cmap), pl.BlockSpec((tm, AW), lmap),
                  pl.BlockSpec((tm, GVW), cmap), pl.BlockSpec((tm, GVW), lmap),
                  pl.BlockSpec((1, 1, 6 * D), mmap),
                  pl.BlockSpec((D, MG_WIDTH), const), pl.BlockSpec((AW, D), const),
                  pl.BlockSpec((GVW, D), const), pl.BlockSpec((D, D), const),
                  pl.BlockSpec((1, D), const), pl.BlockSpec((1, D), const),
                  pl.BlockSpec((D, LANES), const), pl.BlockSpec((1, LANES), const)],
        out_specs=[pl.BlockSpec((tm, D), row), pl.BlockSpec((tm, ROW_SUB, LANES), lambda i: (i, 0, 0)),
                   pl.BlockSpec((tm, LANES), row), pl.BlockSpec((8, tm), lambda i: (0, i)),
                   pl.BlockSpec((8, LANES), const)],
        out_shape=[jax.ShapeDtypeStruct((t, D), F32), jax.ShapeDtypeStruct((t, ROW_SUB, LANES), BF16),
                   jax.ShapeDtypeStruct((t, LANES), F32), jax.ShapeDtypeStruct((8, t), F32),
                   jax.ShapeDtypeStruct((8, LANES), F32)],
        scratch_shapes=[pltpu.VMEM((8, LANES), F32), pltpu.VMEM((ROW_GROUP * ROW_SUB, LANES), F32)],
        compiler_params=_cparams(("arbitrary",)),
        name="post",
    )(x_ctx, x_lat, a_ctx, a_lat, g_ctx, g_lat, mod_all, w_mg, w_ba, w_bg, w_o, l1g, l1b, w_r, b_r)


ROW_UNROLL = 8


def _row_copy(src_ref, dst_ref, sem):
    return pltpu.make_async_copy(src_ref, dst_ref, sem)


def _scatter_kernel(pos0_ref, pos1_ref, h_ref, xs_ref, sem, zero_s, zero_sem):
    ts = h_ref.shape[0]
    n_rows = xs_ref.shape[0] - TM_EXP

    @pl.when(pl.program_id(0) == 0)
    def _():
        zero_s[...] = jnp.zeros_like(zero_s)
        pad = _row_copy(zero_s, xs_ref.at[pl.ds(n_rows, TM_EXP)], zero_sem)
        pad.start()
        pad.wait()

    def issue(g, carry):
        r0 = pl.multiple_of(g * ROW_UNROLL, ROW_UNROLL)
        for k in range(ROW_UNROLL):
            src = _row_slab(h_ref, r0 + k)
            _row_copy(src, _row_slab(xs_ref, pos0_ref[0, r0 + k]), sem).start(priority=0)
            _row_copy(src, _row_slab(xs_ref, pos1_ref[0, r0 + k]), sem).start(priority=1)
        return carry

    lax.fori_loop(0, ts // ROW_UNROLL, issue, 0)
    for _ in range(2):
        _row_copy(h_ref, xs_ref.at[pl.ds(0, ts)], sem).wait()


def _scatter_rows(h2p, pos0, pos1):
    t = h2p.shape[0]
    ts = TS_ROWS
    smem = lambda: pl.BlockSpec((None, 1, ts), lambda i: (i, 0, 0), memory_space=pltpu.SMEM)
    return pl.pallas_call(
        _scatter_kernel,
        grid=(t // ts,),
        in_specs=[smem(), smem(), pl.BlockSpec((ts, ROW_SUB, LANES), lambda i: (i, 0, 0))],
        out_specs=pl.BlockSpec(memory_space=pl.ANY),
        out_shape=jax.ShapeDtypeStruct((2 * t + TM_EXP, ROW_SUB, LANES), BF16),
        scratch_shapes=[pltpu.SemaphoreType.DMA(()), pltpu.VMEM((TM_EXP, ROW_SUB, LANES), BF16),
                        pltpu.SemaphoreType.DMA(())],
        compiler_params=_cparams(("arbitrary",)),
        name="scatter",
    )(pos0.reshape(t // ts, 1, ts), pos1.reshape(t // ts, 1, ts), h2p)


def _expert_kernel(start_ref, nwin_ref, xs_ref, wg_ref, wu_ref, wd_ref, ys_ref,
                   wgu_s, wd_s, in_buf, out_buf, slab_s, in_sem, out_sem):
    e = pl.program_id(0)
    tm = TM_EXP
    wgu_s[:, 0:D_EXP] = wg_ref[...].astype(BF16)
    wgu_s[:, D_EXP:2 * D_EXP] = wu_ref[...].astype(BF16)
    wd_s[...] = wd_ref[...].astype(BF16)
    base = start_ref[e]
    n = nwin_ref[e]

    @pl.when(e == 0)
    def _():
        out_buf[0] = jnp.zeros(out_buf.shape[1:], BF16)
        pad = pltpu.make_async_copy(out_buf.at[0], ys_ref.at[pl.ds(ys_ref.shape[0] - tm, tm)], out_sem.at[0])
        pad.start()
        pad.wait()

    def read(w, slot, first_row=None):
        first_row = base if first_row is None else first_row
        return pltpu.make_async_copy(xs_ref.at[pl.ds(first_row + w * tm, tm)], in_buf.at[slot], in_sem.at[slot])

    def write(w, slot):
        return pltpu.make_async_copy(out_buf.at[slot], ys_ref.at[pl.ds(base + w * tm, tm)], out_sem.at[slot])

    def start_first_reads(expert):
        for a in range(READ_AHEAD):
            @pl.when(nwin_ref[expert] > a)
            def _(a=a):
                read(a, a, start_ref[expert]).start()

    @pl.when(e == 0)
    def _():
        start_first_reads(0)

    def body(w, carry):
        slot = w % 2
        rslot = w % (READ_AHEAD + 1)

        @pl.when(w + READ_AHEAD < n)
        def _():
            read(w + READ_AHEAD, (w + READ_AHEAD) % (READ_AHEAD + 1)).start()

        read(w, rslot).wait()

        @pl.when(w >= 2)
        def _():
            write(w - 2, slot).wait()

        gu = _dot(_load_row_slabs(in_buf.at[rslot], slab_s).astype(BF16), wgu_s[...])
        hid = _silu(gu[:, 0:D_EXP]) * gu[:, D_EXP:2 * D_EXP]
        _store_row_slabs(out_buf.at[slot], _dot(hid.astype(BF16), wd_s[...]), slab_s)
        write(w, slot).start()
        return carry

    lax.fori_loop(0, n, body, 0)

    @pl.when(e + 1 < pl.num_programs(0))
    def _():
        start_first_reads(e + 1)

    @pl.when(n >= 1)
    def _():
        write(n - 1, (n - 1) % 2).wait()

    @pl.when(n >= 2)
    def _():
        write(n - 2, n % 2).wait()


def _experts(xs, starts, n_win, w_gate, w_up, w_down):
    tm = TM_EXP
    wmap = lambda e, st, nw: (e, 0, 0)
    slab = (tm, ROW_SUB, LANES)
    return pl.pallas_call(
        _expert_kernel,
        grid_spec=pltpu.PrefetchScalarGridSpec(
            num_scalar_prefetch=2,
            grid=(N_EXP,),
            in_specs=[pl.BlockSpec(memory_space=pl.ANY),
                      pl.BlockSpec((None, D, D_EXP), wmap), pl.BlockSpec((None, D, D_EXP), wmap),
                      pl.BlockSpec((None, D_EXP, D), wmap)],
            out_specs=pl.BlockSpec(memory_space=pl.ANY),
            scratch_shapes=[pltpu.VMEM((D, 2 * D_EXP), BF16), pltpu.VMEM((D_EXP, D), BF16),
                            pltpu.VMEM((READ_AHEAD + 1,) + slab, BF16), pltpu.VMEM((2,) + slab, BF16),
                            pltpu.VMEM((tm * ROW_SUB, LANES), F32),
                            pltpu.SemaphoreType.DMA((READ_AHEAD + 1,)), pltpu.SemaphoreType.DMA((2,))]),
        out_shape=jax.ShapeDtypeStruct(xs.shape, BF16),
        compiler_params=_cparams(("arbitrary",)),
        name="experts",
    )(starts, n_win, xs, w_gate, w_up, w_down)


def _final_kernel(p0c_ref, p1c_ref, p0n_ref, p1n_ref, x1_ref, rt_ref, mod_ref, l2g_ref, l2b_ref, ys_ref,
                  oc_ref, ol_ref, buf, sem, slab_s, *, n_ctx_tiles):
    i = pl.program_id(0)
    n = pl.num_programs(0)
    tm = x1_ref.shape[0]

    def gather(p0_ref, p1_ref, slot):
        def issue(g, carry):
            r0 = pl.multiple_of(g * ROW_UNROLL, ROW_UNROLL)
            for k in range(ROW_UNROLL):
                _row_copy(_row_slab(ys_ref, p0_ref[0, r0 + k]),
                          _row_slab(buf.at[slot, 0], r0 + k), sem.at[slot]).start(priority=0)
                _row_copy(_row_slab(ys_ref, p1_ref[0, r0 + k]),
                          _row_slab(buf.at[slot, 1], r0 + k), sem.at[slot]).start(priority=1)
            return carry

        lax.fori_loop(0, tm // ROW_UNROLL, issue, 0)

    cur = i % 2

    @pl.when(i == 0)
    def _():
        gather(p0c_ref, p1c_ref, 0)

    @pl.when(i + 1 < n)
    def _():
        gather(p0n_ref, p1n_ref, 1 - cur)

    for k in range(2):
        _row_copy(ys_ref.at[pl.ds(0, tm)], buf.at[cur, k], sem.at[cur]).wait()

    g2 = mod_ref[0, :, 5 * D:6 * D]
    w1 = rt_ref[:, 2:3]
    w2 = rt_ref[:, 3:4]
    moe = (w1 * _load_row_slabs(buf.at[cur, 0], slab_s.at[0])
           + w2 * _load_row_slabs(buf.at[cur, 1], slab_s.at[1]))
    out = _ln(ALPHA * x1_ref[...] + g2 * moe) * l2g_ref[...] + l2b_ref[...]

    @pl.when(i < n_ctx_tiles)
    def _():
        oc_ref[...] = out

    @pl.when(i >= n_ctx_tiles)
    def _():
        ol_ref[...] = out


def _final(x1, rt, pos0, pos1, mod_all, l2g, l2b, ys, t_ctx, seq_lat):
    t = x1.shape[0]
    tm = TM_TOK
    nt = t // tm
    nct = t_ctx // tm
    t_lat = t - t_ctx
    per_seq = seq_lat // tm
    nb_lat = t_lat // seq_lat
    p0 = pos0.reshape(nt, 1, tm)
    p1 = pos1.reshape(nt, 1, tm)
    row = lambda i: (i, 0)
    const = lambda i: (0, 0)
    mmap = lambda i: (jnp.where(i < nct, nb_lat, jnp.maximum(i - nct, 0) // per_seq), 0, 0)
    smem_cur = lambda: pl.BlockSpec((None, 1, tm), lambda i: (i, 0, 0), memory_space=pltpu.SMEM)
    smem_nxt = lambda: pl.BlockSpec((None, 1, tm), lambda i: (jnp.minimum(i + 1, nt - 1), 0, 0),
                                    memory_space=pltpu.SMEM)
    return pl.pallas_call(
        functools.partial(_final_kernel, n_ctx_tiles=nct),
        grid=(nt,),
        in_specs=[smem_cur(), smem_cur(), smem_nxt(), smem_nxt(),
                  pl.BlockSpec((tm, D), row), pl.BlockSpec((tm, LANES), row),
                  pl.BlockSpec((1, 1, 6 * D), mmap),
                  pl.BlockSpec((1, D), const), pl.BlockSpec((1, D), const),
                  pl.BlockSpec(memory_space=pl.ANY)],
        out_specs=[pl.BlockSpec((tm, D), lambda i: (jnp.minimum(i, nct - 1), 0)),
                   pl.BlockSpec((tm, D), lambda i: (jnp.maximum(i - nct, 0), 0))],
        out_shape=[jax.ShapeDtypeStruct((t_ctx, D), F32), jax.ShapeDtypeStruct((t_lat, D), F32)],
        scratch_shapes=[pltpu.VMEM((2, 2, tm, ROW_SUB, LANES), BF16), pltpu.SemaphoreType.DMA((2,)),
                        pltpu.VMEM((2, tm * ROW_SUB, LANES), F32)],
        compiler_params=_cparams(("arbitrary",)),
        name="final",
    )(p0, p1, p0, p1, x1, rt, mod_all, l2g, l2b, ys)


def _reorder_q_heads(w, axis):
    shape = w.shape
    split = shape[:axis] + (N_KV_HEADS, N_Q_HEADS // N_KV_HEADS, HD) + shape[axis + 1:]
    return jnp.swapaxes(w.reshape(split), axis, axis + 1).reshape(shape)


def _rope_tables(seq):
    t = np.arange(seq)
    half = HD // 4
    inv = (ROPE_THETA ** (-np.arange(half, dtype=np.float64) / half)).astype(np.float32)
    d64 = np.arange(LANES) % HD
    pos = np.where((d64 < HD // 2)[None, :], (t // GRID_W)[:, None], (t % GRID_W)[:, None])
    ang = (pos.astype(np.float32) * inv[d64 % half][None, :]).astype(np.float64)
    sign = np.where((d64 % 32) < 16, -1.0, 1.0)
    return (jnp.asarray(np.cos(ang), F32), jnp.asarray(np.sin(ang) * sign[None, :], F32))


def _pair_states(s):
    b = s.shape[0]
    s = s.reshape(b, 2, 2, GLA_DK, GLA_DV)
    return s.transpose(0, 1, 4, 2, 3).reshape(b, 2, GLA_DV, 2 * GLA_DK)


def _unpair_states(s):
    b = s.shape[0]
    s = s.reshape(b, 2, GLA_DV, 2, GLA_DK)
    return s.transpose(0, 1, 3, 4, 2).reshape(b, GLA_H, GLA_DK, GLA_DV)


def _route_tables(ert, counts):
    i32 = jnp.int32
    cnt = counts[0, :N_EXP].astype(i32)
    starts = jnp.cumsum(cnt) - cnt
    table = lambda e: jnp.sum(jnp.where(e[None, :] == jnp.arange(N_EXP, dtype=i32)[:, None],
                                        starts[:, None], 0), axis=0)
    pos0 = table(ert[0].astype(i32)) + ert[4].astype(i32)
    pos1 = table(ert[1].astype(i32)) + ert[5].astype(i32)
    return pos0, pos1, starts, (cnt + (TM_EXP - 1)) // TM_EXP


def kernel(x_prompt, x_sample, cache_k, cache_v, state_gla_fwd, state_gla_bwd, c, c_ctx, w_ada, b_ada, w_in, q_norm, k_norm, gla_w_gate, gla_b_gate, gla_norm, w_br_attn, w_br_gla, w_out, ln1_g, ln1_b, router_group_w, router_group_b, router_expert_w, router_expert_b, exp_w_gate, exp_w_up, exp_w_down, ln2_g, ln2_b):
    b_ctx, seq_ctx, _ = x_prompt.shape
    b_lat, seq_lat, _ = x_sample.shape
    t_ctx, t_lat = b_ctx * seq_ctx, b_lat * seq_lat
    t = t_ctx + t_lat
    l = 0

    rows = -(-(b_lat + 1) // 8) * 8
    c_rows = jnp.zeros((rows, D), F32).at[:b_lat].set(c).at[b_lat].set(c_ctx)
    mod = _ada(c_rows, w_ada[l], b_ada[l][None, :])
    mod_all = mod[:b_lat + 1, None, :]
    mod_lat = mod_all[:b_lat]
    mod_ctx = mod_all[b_lat:]

    w_full = w_in[l]
    w_a = jnp.concatenate([_reorder_q_heads(w_full[:, :AW], 1), w_full[:, AW:A_WIDTH]], axis=1).astype(BF16)
    w_mg = w_full[:, A_WIDTH:].astype(BF16)
    gain = jnp.concatenate([jnp.tile(q_norm[l], N_Q_HEADS), jnp.tile(k_norm[l], N_KV_HEADS)])[None, :]
    head_of = np.arange(AW + KVW) // HD
    ind = jnp.asarray((head_of[:, None] == np.arange(LANES)[None, :]) / HD, BF16)
    w_ba = _reorder_q_heads(w_br_attn[l], 0).astype(BF16)
    w_bg = w_br_gla[l].astype(BF16)
    w_o = w_out[l].astype(BF16)
    w_r = jnp.zeros((D, LANES), F32).at[:, :N_GROUPS].set(router_group_w[l])
    w_r = w_r.at[:, N_GROUPS:N_GROUPS + N_EXP].set(router_expert_w[l]).astype(BF16)
    b_r = jnp.zeros((1, LANES), F32).at[0, :N_GROUPS].set(router_group_b[l])
    b_r = b_r.at[0, N_GROUPS:N_GROUPS + N_EXP].set(router_expert_b[l])
    wg = jnp.zeros((2, 2 * GATE_RANK, GKW), F32)
    wg = wg.at[0, :GATE_RANK].set(gla_w_gate[l, 0]).at[1, GATE_RANK:].set(gla_w_gate[l, 1])
    bg = gla_b_gate[l][:, None, :]
    gn = gla_norm[l][None, :]

    xc = x_prompt.reshape(t_ctx, D)
    xl = x_sample.reshape(t_lat, D)

    q_c, k_c, v_c, qg_c, kg_c, vg_c, rs_c, lr_c, kf_c, vf_c = _inproj(
        xc, mod_ctx, w_a, gain, ind, None, seq_ctx, latent=False)
    attn_c = _attention(q_c, k_c, v_c, None, seq_ctx)
    zero_state = jnp.zeros((1, 2, 2, LANES, LANES), F32)
    gla_c, sfin_c = _gla(qg_c, kg_c, vg_c, lr_c, rs_c, zero_state, wg, bg, gn, seq_ctx)

    q_l, k_l, v_l, qg_l, kg_l, vg_l, rs_l, lr_l = _inproj(
        xl, mod_lat, w_a, gain, ind, _rope_tables(seq_lat), seq_lat, latent=True)
    past = cache_k.shape[2]
    kc = cache_k[:, l].reshape(b_lat, past, KVW).astype(BF16)
    vc = cache_v[:, l].reshape(b_lat, past, KVW).astype(BF16)
    attn_l = _attention(q_l, k_l, v_l, (kc, vc), seq_lat)
    s0 = jnp.stack([_pair_states(state_gla_fwd[:, l]), _pair_states(state_gla_bwd[:, l])], axis=1)
    gla_l, _ = _gla(qg_l, kg_l, vg_l, lr_l, rs_l, s0, wg, bg, gn, seq_lat)

    x1, h2p, rt, ert, counts = _post(xc, xl, attn_c, attn_l, gla_c, gla_l, mod_all, seq_lat,
                                w_mg, w_ba, w_bg, w_o, ln1_g[l][None, :], ln1_b[l][None, :], w_r, b_r)

    pos0, pos1, starts, n_win = _route_tables(ert, counts)
    xs = _scatter_rows(h2p, pos0, pos1)
    ys = _experts(xs, starts, n_win, exp_w_gate[l], exp_w_up[l], exp_w_down[l])
    y_ctx, y_lat = _final(x1, rt, pos0, pos1, mod_all, ln2_g[l][None, :], ln2_b[l][None, :], ys,
                          t_ctx, seq_lat)

    untranspose = lambda a: a.reshape(b_ctx, 1, N_KV_HEADS, HD, seq_ctx).transpose(0, 1, 4, 2, 3)
    new_k, new_v = untranspose(kf_c), untranspose(vf_c)
    new_sf = _unpair_states(sfin_c[:, 0])[:, None]
    new_sb = _unpair_states(sfin_c[:, 1])[:, None]
    return (y_ctx.reshape(b_ctx, seq_ctx, D), y_lat.reshape(b_lat, seq_lat, D),
            new_k, new_v, new_sf, new_sb)
```

```python
import functools

import numpy as np
import jax
import jax.numpy as jnp
from jax import lax
from jax.experimental import pallas as pl
from jax.experimental.pallas import tpu as pltpu

F32 = jnp.float32
BF16 = jnp.bfloat16
HIGHEST = lax.Precision.HIGHEST

D = 1024
GRID_W = 64
HD = 64
N_Q_HEADS = 8
N_KV_HEADS = 2
AW = N_Q_HEADS * HD
KVW = N_KV_HEADS * HD
ROPE_THETA = 10000.0
GLA_H = 4
GLA_DK = 64
GLA_DV = 128
GKW = GLA_H * GLA_DK
GVW = GLA_H * GLA_DV
GATE_RANK = 16
GLA_TAU = 16.0
CHUNK = 64
N_GROUPS = 4
EPG = 8
N_EXP = N_GROUPS * EPG
D_EXP = 256
DEPTH = 1
ALPHA = (2.0 * DEPTH) ** 0.25
LN_EPS = 1e-6
RMS_EPS = 1e-6

LANES = 128
A_WIDTH = AW + 2 * KVW + 2 * GKW + 2 * GVW + 2 * GATE_RANK
MG_WIDTH = 2 * D
TM_TOK = 512
ROW_GROUP = 128
LOG2E = 1.4426950408889634
ONES_ROWS = 16
TQ_LAT = 256
TM_EXP = 256
READ_AHEAD = 2
TS_ROWS = 2048
VMEM_LIMIT = 56 * 1024 * 1024


def _cparams(sem):
    return pltpu.CompilerParams(dimension_semantics=sem, vmem_limit_bytes=VMEM_LIMIT)


def _dot(a, b):
    return jnp.dot(a, b, preferred_element_type=F32)


def _dot_nt(a, b):
    return lax.dot_general(a, b, (((1,), (1,)), ((), ())), preferred_element_type=F32)


def _dot_tn(a, b):
    return lax.dot_general(a, b, (((0,), (0,)), ((), ())), preferred_element_type=F32)


def _ln(x):
    mu = jnp.mean(x, axis=-1, keepdims=True)
    xc = x - mu
    var = jnp.mean(xc * xc, axis=-1, keepdims=True)
    return xc * lax.rsqrt(var + LN_EPS)


def _silu(x):
    return x * jax.nn.sigmoid(x)


def _split_bf16(x):
    hi = x.astype(BF16)
    lo = (x - hi.astype(F32)).astype(BF16)
    return hi, lo


def _ada_kernel(c_ref, w_ref, b_ref, o_ref):
    s = _silu(c_ref[...])
    o_ref[...] = jnp.dot(s, w_ref[...], preferred_element_type=F32, precision=HIGHEST) + b_ref[...]


def _ada(c_rows, w_ada, b_ada):
    rows = c_rows.shape[0]
    n = w_ada.shape[1]
    bn = 1024
    return pl.pallas_call(
        _ada_kernel,
        grid=(n // bn,),
        in_specs=[pl.BlockSpec((rows, D), lambda j: (0, 0)),
                  pl.BlockSpec((D, bn), lambda j: (0, j)),
                  pl.BlockSpec((1, bn), lambda j: (0, j))],
        out_specs=pl.BlockSpec((rows, bn), lambda j: (0, j)),
        out_shape=jax.ShapeDtypeStruct((rows, n), F32),
        compiler_params=_cparams(("arbitrary",)),
        name="ada",
    )(c_rows, w_ada, b_ada)


def _inproj_kernel(*refs, latent, seq):
    if latent:
        (x_ref, mod_ref, w_ref, gain_ref, ind_ref, cos_ref, sin_ref,
         q_ref, k_ref, v_ref, qg_ref, kg_ref, vg_ref, rs_ref, lr_ref) = refs
    else:
        (x_ref, mod_ref, w_ref, gain_ref, ind_ref,
         q_ref, k_ref, v_ref, qg_ref, kg_ref, vg_ref, rs_ref, lr_ref, kf_ref, vf_ref) = refs
    tm = x_ref.shape[0]
    sub = ROW_GROUP
    n_groups = tm // sub
    sh1 = mod_ref[0, :, 0:D]
    sc1 = mod_ref[0, :, D:2 * D]
    lane = lax.broadcasted_iota(jnp.int32, (sub, LANES), 1)
    low_half = lane < HD
    first = (lane % 32) < 16

    def project(g):
        rows = slice(g * sub, (g + 1) * sub)
        h = (_ln(x_ref[rows, :]) * (1.0 + sc1) + sh1).astype(BF16)
        return _dot(h, w_ref[...])

    def finish(g, res):
        rows = slice(g * sub, (g + 1) * sub)
        qk = res[:, 0:AW + KVW]
        hi, lo = _split_bf16(qk * qk)
        ms = _dot(hi, ind_ref[...]) + _dot(lo, ind_ref[...])
        r = lax.rsqrt(ms + RMS_EPS)
        for s in range(5):
            rb = jnp.where(low_half, r[:, 2 * s:2 * s + 1], r[:, 2 * s + 1:2 * s + 2])
            y = res[:, LANES * s:LANES * (s + 1)] * rb * gain_ref[:, LANES * s:LANES * (s + 1)]
            if s == 4 and not latent:
                kf_ref[(g * sub) // seq, :, (g * sub) % seq:(g * sub) % seq + sub] = y.T
            if latent:
                partner = jnp.where(first, pltpu.roll(y, LANES - 16, 1), pltpu.roll(y, 16, 1))
                y = y * cos_ref[rows, :] + partner * sin_ref[rows, :]
            if s < 4:
                q_ref[rows, LANES * s:LANES * (s + 1)] = (y * (HD ** -0.5 * LOG2E)).astype(BF16)
            else:
                k_ref[rows, :] = y.astype(BF16)
        o = AW + KVW
        v = res[:, o:o + KVW]
        v_ref[rows, :] = v.astype(BF16)
        if not latent:
            vf_ref[(g * sub) // seq, :, (g * sub) % seq:(g * sub) % seq + sub] = v.T
        o += KVW
        qg_ref[rows, :] = res[:, o:o + GKW] * (GLA_DK ** -0.5)
        o += GKW
        kg_ref[rows, :] = res[:, o:o + GKW]
        o += GKW
        vg_ref[rows, :] = res[:, o:o + GVW].astype(BF16)
        o += GVW
        rs_ref[rows, :] = _silu(res[:, o:o + GVW]).astype(BF16)
        o += GVW
        lr_ref[rows, :] = res[:, o:o + 2 * GATE_RANK]

    pending = {0: project(0)}
    for g in range(n_groups):
        if g + 1 < n_groups:
            pending[g + 1] = project(g + 1)
        finish(g, pending.pop(g))


def _inproj(x2, mod, w_a, gain, ind, rope, seq, latent):
    t = x2.shape[0]
    tm = TM_TOK
    per_seq = max(seq // tm, 1)
    per_tile = max(tm // seq, 1)
    row = lambda i: (i, 0)
    const = lambda i: (0, 0)
    in_specs = [pl.BlockSpec((tm, D), row),
                pl.BlockSpec((1, 1, 6 * D), (lambda i: (i // per_seq, 0, 0)) if latent else (lambda i: (0, 0, 0))),
                pl.BlockSpec((D, A_WIDTH), const),
                pl.BlockSpec((1, AW + KVW), const),
                pl.BlockSpec((AW + KVW, LANES), const)]
    args = [x2, mod, w_a, gain, ind]
    if latent:
        in_specs += [pl.BlockSpec((tm, LANES), lambda i: (i % per_seq, 0))] * 2
        args += list(rope)
    widths = [(AW, BF16), (KVW, BF16), (KVW, BF16), (GKW, F32), (GKW, F32), (GVW, BF16), (GVW, BF16),
              (2 * GATE_RANK, F32)]
    out_specs = [pl.BlockSpec((tm, w), row) for w, _ in widths]
    out_shape = [jax.ShapeDtypeStruct((t, w), dt) for w, dt in widths]
    if not latent:
        cache_spec = pl.BlockSpec((per_tile, KVW, seq), lambda i: (i, 0, 0))
        out_specs += [cache_spec] * 2
        out_shape += [jax.ShapeDtypeStruct((t // seq, KVW, seq), F32)] * 2
    return pl.pallas_call(
        functools.partial(_inproj_kernel, latent=latent, seq=seq),
        grid=(t // tm,),
        in_specs=in_specs,
        out_specs=out_specs,
        out_shape=out_shape,
        compiler_params=_cparams(("parallel",)),
        name="inproj_lat" if latent else "inproj_ctx",
    )(*args)


def _attn_kernel(*refs, has_cache):
    def transposed_with_ones(dst, src):
        dst[0:KVW, :] = src[...].astype(F32).T.astype(BF16)
        dst[KVW:, :] = jnp.ones((ONES_ROWS, dst.shape[1]), BF16)

    if has_cache:
        q_ref, k_ref, v_ref, kc_ref, vc_ref, o_ref, vt_s, vct_s = refs

        @pl.when(pl.program_id(1) == 0)
        def _():
            transposed_with_ones(vt_s, v_ref)
            transposed_with_ones(vct_s, vc_ref)
    else:
        q_ref, k_ref, v_ref, o_ref, vt_s = refs
        transposed_with_ones(vt_s, v_ref)
    tq = q_ref.shape[0]

    lane = lax.broadcasted_iota(jnp.int32, (tq, LANES), 1)
    low_half = lane < HD
    k = k_ref[...]
    scores = []
    for j in range(N_KV_HEADS):
        keep = low_half if j == 0 else jnp.logical_not(low_half)
        zero = jnp.zeros((tq, LANES), BF16)
        for pair in range(2):
            qs = jnp.concatenate([jnp.where(keep, q_ref[:, LANES * s:LANES * (s + 1)], zero)
                                  for s in (2 * pair, 2 * pair + 1)], axis=0)
            s1 = _dot_nt(k, qs)
            s2 = _dot_nt(kc_ref[...], qs) if has_cache else None
            scores.append((s1, s2))
    outs = []
    for s1, s2 in scores:
        m = jnp.max(s1, axis=0, keepdims=True)
        if has_cache:
            m = jnp.maximum(m, jnp.max(s2, axis=0, keepdims=True))
        acc = _dot(vt_s[...], jnp.exp2(s1 - m).astype(BF16))
        if has_cache:
            acc = acc + _dot(vct_s[...], jnp.exp2(s2 - m).astype(BF16))
        outs.append(acc[0:KVW] / acc[KVW:KVW + 1])
    head0 = jnp.concatenate(outs[0:2], axis=1)
    head1 = jnp.concatenate(outs[2:4], axis=1)
    row = lax.broadcasted_iota(jnp.int32, (LANES, 4 * tq), 0)
    out = jnp.where(row < HD, head0, head1).T
    for s in range(4):
        o_ref[:, LANES * s:LANES * (s + 1)] = out[s * tq:(s + 1) * tq].astype(BF16)


def _attention(q, k, v, cache, seq):
    t = q.shape[0]
    if cache is None:
        tq = seq
        grid = (t // seq,)
        qmap = lambda b: (b, 0)
        in_specs = [pl.BlockSpec((tq, AW), qmap), pl.BlockSpec((seq, KVW), qmap),
                    pl.BlockSpec((seq, KVW), qmap)]
        args = [q, k, v]
        scratch = [pltpu.VMEM((KVW + ONES_ROWS, seq), BF16)]
        sem = ("parallel",)
        name = "attn_ctx"
    else:
        tq = TQ_LAT
        nq = seq // tq
        kc, vc = cache
        past = kc.shape[1]
        grid = (t // seq, nq)
        qmap = lambda b, i: (b * nq + i, 0)
        kmap = lambda b, i: (b, 0)
        cmap = lambda b, i: (b, 0, 0)
        in_specs = [pl.BlockSpec((tq, AW), qmap), pl.BlockSpec((seq, KVW), kmap),
                    pl.BlockSpec((seq, KVW), kmap),
                    pl.BlockSpec((None, past, KVW), cmap), pl.BlockSpec((None, past, KVW), cmap)]
        args = [q, k, v, kc, vc]
        scratch = [pltpu.VMEM((KVW + ONES_ROWS, seq), BF16), pltpu.VMEM((KVW + ONES_ROWS, past), BF16)]
        sem = ("parallel", "arbitrary")
        name = "attn_lat"
    return pl.pallas_call(
        functools.partial(_attn_kernel, has_cache=cache is not None),
        grid=grid,
        in_specs=in_specs,
        out_specs=pl.BlockSpec((tq, AW), qmap),
        out_shape=jax.ShapeDtypeStruct((t, AW), BF16),
        scratch_shapes=scratch,
        compiler_params=_cparams(sem),
        name=name,
    )(*args)


GLA_BLK = 256
GLA_UNROLL = 4


def _split3_bf16(x):
    hi = x.astype(BF16)
    r1 = x - hi.astype(F32)
    mid = r1.astype(BF16)
    lo = (r1 - mid.astype(F32)).astype(BF16)
    return hi, mid, lo


def _gla_kernel(qg_ref, kg_ref, vg_ref, lr_ref, rs_ref, s0_ref, wg_ref, bg_ref, gn_ref,
                o_ref, sfin_ref, cum_s, kv_s, dec_s):
    n = qg_ref.shape[0]
    nc = n // CHUNK
    lane = lax.broadcasted_iota(jnp.int32, (CHUNK, LANES), 1)
    low_half = lane < GLA_DK
    lane_sq = lax.broadcasted_iota(jnp.int32, (LANES, LANES), 1)
    low_half_sq = lane_sq < GLA_DK
    ri = lax.broadcasted_iota(jnp.int32, (2 * CHUNK, 2 * CHUNK), 0)
    ci = lax.broadcasted_iota(jnp.int32, (2 * CHUNK, 2 * CHUNK), 1)
    diag = (ri >> 6) == (ci >> 6)
    keep2 = (diag & (ri >= ci), diag & (ci >= ri))

    rb = lax.broadcasted_iota(jnp.int32, (GLA_BLK, GLA_BLK), 0)
    cb = lax.broadcasted_iota(jnp.int32, (GLA_BLK, GLA_BLK), 1)
    same = (rb >> 6) == (cb >> 6)
    tri = ((same & (rb >= cb)).astype(BF16), (same & (cb >= rb)).astype(BF16))
    for d in range(2):
        w_hi, w_lo = _split_bf16(wg_ref[d])
        for blk in range(n // GLA_BLK):
            rows = slice(blk * GLA_BLK, (blk + 1) * GLA_BLK)
            l_hi, l_lo = _split_bf16(lr_ref[rows, :])
            z = _dot(l_hi, w_hi) + _dot(l_lo, w_hi) + _dot(l_hi, w_lo) + bg_ref[d]
            logg = (jnp.minimum(z, 0.0) - jnp.log(1.0 + jnp.exp(-jnp.abs(z)))) * (1.0 / GLA_TAU)
            pieces = _split3_bf16(logg)
            cum_s[d, rows, :] = _dot(tri[d], pieces[0]) + _dot(tri[d], pieces[1]) + _dot(tri[d], pieces[2])

    def pass1(i, carry):
        cs = [i * GLA_UNROLL + u for u in range(GLA_UNROLL)]
        rws = [pl.ds(pl.multiple_of(c * CHUNK, CHUNK), CHUNK) for c in cs]
        prods = []
        for c, rows in zip(cs, rws):
            kc = kg_ref[rows, :]
            kdec = []
            for d in range(2):
                cum = cum_s[d, rows, :]
                last = cum[CHUNK - 1:CHUNK, :] if d == 0 else cum[0:1, :]
                dec_s[d, c] = jnp.exp(last)
                kdec.append((kc * jnp.exp(last - cum)).astype(BF16))
            for p in range(2):
                vpair = vg_ref[rows, GLA_DV * 2 * p:GLA_DV * 2 * (p + 1)]
                kpair = jnp.concatenate([kdec[0][:, LANES * p:LANES * (p + 1)],
                                         kdec[1][:, LANES * p:LANES * (p + 1)]], axis=1)
                prods.append(_dot_tn(vpair, kpair))
        for j, c in enumerate(cs):
            for p in range(2):
                res = prods[2 * j + p]
                for d in range(2):
                    cols = slice(LANES * d, LANES * (d + 1))
                    kv_s[d, c, p] = jnp.where(low_half_sq, res[0:GLA_DV, cols], res[GLA_DV:2 * GLA_DV, cols])
        return carry

    lax.fori_loop(0, nc // GLA_UNROLL, pass1, 0)

    for d in range(2):
        def scan(i, st):
            c = i if d == 0 else nc - 1 - i
            dec = dec_s[d, c]
            new = []
            for p in range(2):
                kv = kv_s[d, c, p]
                kv_s[d, c, p] = st[p]
                new.append(st[p] * dec[:, LANES * p:LANES * (p + 1)] + kv)
            return tuple(new)

        fin = lax.fori_loop(0, nc, scan, (s0_ref[d, 0], s0_ref[d, 1]))
        sfin_ref[d, 0] = fin[0]
        sfin_ref[d, 1] = fin[1]

    def pass3(i, carry):
        cs = [i * GLA_UNROLL + u for u in range(GLA_UNROLL)]
        rws = [pl.ds(pl.multiple_of(c * CHUNK, CHUNK), CHUNK) for c in cs]
        first = []
        for c, rows in zip(cs, rws):
            q = qg_ref[rows, :]
            k = kg_ref[rows, :]
            for d in range(2):
                cum = cum_s[d, rows, :]
                qt = q * jnp.exp(cum)
                kt = (k * jnp.exp(-cum)).astype(BF16)
                for p in range(2):
                    qs = qt[:, LANES * p:LANES * (p + 1)]
                    lhs = jnp.concatenate([jnp.where(low_half, qs, 0.0), jnp.where(low_half, 0.0, qs)],
                                          axis=0).astype(BF16)
                    kts = kt[:, LANES * p:LANES * (p + 1)]
                    rhs = jnp.concatenate([kts, kts, kv_s[d, c, p].astype(BF16)], axis=0)
                    first.append(_dot_nt(lhs, rhs))
        second = []
        for j, rows in enumerate(rws):
            for d in range(2):
                for p in range(2):
                    res = first[4 * j + 2 * d + p]
                    vp = jnp.concatenate([vg_ref[rows, GLA_DV * (2 * p):GLA_DV * (2 * p + 1)],
                                          vg_ref[rows, GLA_DV * (2 * p + 1):GLA_DV * (2 * p + 2)]], axis=0)
                    a = jnp.where(keep2[d], res[:, 0:2 * CHUNK], 0.0).astype(BF16)
                    second.append(_dot(a, vp) + res[:, 2 * CHUNK:])
        for j, rows in enumerate(rws):
            for p in range(2):
                tot = second[4 * j + p] + second[4 * j + 2 + p]
                y = tot * lax.rsqrt(jnp.mean(tot * tot, axis=-1, keepdims=True) + RMS_EPS) * gn_ref[...]
                for hh in range(2):
                    cols = slice(GLA_DV * (2 * p + hh), GLA_DV * (2 * p + hh + 1))
                    o_ref[rows, cols] = (y[CHUNK * hh:CHUNK * (hh + 1)]
                                         * rs_ref[rows, cols].astype(F32)).astype(BF16)
        return carry

    lax.fori_loop(0, nc // GLA_UNROLL, pass3, 0)


def _gla(qg, kg, vg, lr, rs, s0, wg, bg, gn, seq):
    t = qg.shape[0]
    nb = t // seq
    nc = seq // CHUNK
    row = lambda b: (b, 0)
    c3 = lambda b: (0, 0, 0)
    st = lambda b: (b, 0, 0, 0, 0)
    if s0.shape[0] == 1:
        s0map = lambda b: (0, 0, 0, 0, 0)
    else:
        s0map = st
    return pl.pallas_call(
        _gla_kernel,
        grid=(nb,),
        in_specs=[pl.BlockSpec((seq, GKW), row), pl.BlockSpec((seq, GKW), row),
                  pl.BlockSpec((seq, GVW), row), pl.BlockSpec((seq, 2 * GATE_RANK), row),
                  pl.BlockSpec((seq, GVW), row),
                  pl.BlockSpec((None, 2, 2, LANES, LANES), s0map),
                  pl.BlockSpec((2, 2 * GATE_RANK, GKW), c3), pl.BlockSpec((2, 1, GKW), c3),
                  pl.BlockSpec((1, GLA_DV), lambda b: (0, 0))],
        out_specs=[pl.BlockSpec((seq, GVW), row),
                   pl.BlockSpec((None, 2, 2, LANES, LANES), st)],
        out_shape=[jax.ShapeDtypeStruct((t, GVW), BF16),
                   jax.ShapeDtypeStruct((nb, 2, 2, LANES, LANES), F32)],
        scratch_shapes=[pltpu.VMEM((2, seq, GKW), F32),
                        pltpu.VMEM((2, nc, 2, LANES, LANES), F32),
                        pltpu.VMEM((2, nc, 1, GKW), F32)],
        compiler_params=_cparams(("parallel",)),
        name="gla_lat" if seq > 256 else "gla_ctx",
    )(qg, kg, vg, lr, rs, s0, wg, bg, gn)


ROW_SUB = D // LANES


def _store_row_slabs(ref, x, tmp):
    m = x.shape[0]
    for c in range(ROW_SUB):
        tmp[pl.ds(c, m, stride=ROW_SUB), :] = x[:, LANES * c:LANES * (c + 1)]
    ref[...] = tmp[...].reshape(m, ROW_SUB, LANES).astype(BF16)


def _load_row_slabs(ref, tmp):
    m = ref.shape[0]
    tmp[...] = ref[...].astype(F32).reshape(m * ROW_SUB, LANES)
    return jnp.concatenate([tmp[pl.ds(c, m, stride=ROW_SUB), :] for c in range(ROW_SUB)], axis=1)


def _row_slab(ref, row):
    return ref.at[pl.ds(row, 1)]


def _post_kernel(xc_ref, xl_ref, ac_ref, al_ref, gc_ref, gl_ref, mod_ref,
                 wmg_ref, wba_ref, wbg_ref, wo_ref, l1g_ref, l1b_ref, wr_ref, br_ref,
                 x1_ref, h2_ref, rt_ref, ert_ref, cnt_ref, run_s, slab_s, logit_s, *, n_ctx_tiles):
    i = pl.program_id(0)
    tm = xc_ref.shape[0]
    sub = ROW_GROUP
    n_groups = tm // sub
    is_ctx = i < n_ctx_tiles
    parts = [slice(g * sub, (g + 1) * sub) for g in range(n_groups)]

    @pl.when(i == 0)
    def _():
        run_s[...] = jnp.zeros_like(run_s)
        logit_s[...] = jnp.zeros_like(logit_s)

    sh1 = mod_ref[0, :, 0:D]
    sc1 = mod_ref[0, :, D:2 * D]
    g1 = mod_ref[0, :, 2 * D:3 * D]
    sh2 = mod_ref[0, :, 3 * D:4 * D]
    sc2 = mod_ref[0, :, 4 * D:5 * D]

    xs = [jnp.where(is_ctx, xc_ref[r, :], xl_ref[r, :]) for r in parts]
    hs = [(_ln(x) * (1.0 + sc1) + sh1).astype(BF16) for x in xs]
    gates = [jax.nn.sigmoid(_dot(h, wmg_ref[...])) for h in hs]
    ba = [_dot(jnp.where(is_ctx, ac_ref[r, :], al_ref[r, :]), wba_ref[...]) for r in parts]
    bg = [_dot(jnp.where(is_ctx, gc_ref[r, :], gl_ref[r, :]), wbg_ref[...]) for r in parts]

    ri = lax.broadcasted_iota(jnp.int32, (sub, sub), 0)
    ci = lax.broadcasted_iota(jnp.int32, (sub, sub), 1)
    earlier = (ri > ci).astype(BF16)
    counted = (i > 0).astype(F32)
    run = run_s[0:1, :]
    for r in parts:
        run = _route(logit_s[r, :], earlier, run, counted, rt_ref.at[r, :], ert_ref.at[:, r])
    run_s[0:1, :] = run
    cnt_ref[...] = jnp.broadcast_to(run, cnt_ref.shape)

    merged = [(g[:, :D] * a + g[:, D:] * b).astype(BF16) for g, a, b in zip(gates, ba, bg)]
    mix = [_dot(m, wo_ref[...]) for m in merged]
    x1s = [_ln(ALPHA * x + g1 * m) * l1g_ref[...] + l1b_ref[...] for x, m in zip(xs, mix)]
    h2s = [_ln(x1) * (1.0 + sc2) + sh2 for x1 in x1s]
    logits = [_dot(h2.astype(BF16), wr_ref[...]) + br_ref[...] for h2 in h2s]
    for g, r in enumerate(parts):
        x1_ref[r, :] = x1s[g]
        _store_row_slabs(h2_ref.at[pl.ds(g * sub, sub)], h2s[g], slab_s)
        logit_s[r, :] = logits[g]


def _route(logit, earlier, run, counted, rt_ref, ert_ref):
    tm = logit.shape[0]
    lane_i = lax.broadcasted_iota(jnp.int32, (tm, LANES), 1)
    lane = lane_i.astype(F32)
    lane_grp = ((lane_i - N_GROUPS) >> 3).astype(F32)
    neg = jnp.float32(-jnp.inf)
    far = jnp.float32(LANES)
    is_g = lane_i < N_GROUPS
    lg = jnp.where(is_g, logit, neg)
    mg = jnp.max(lg, axis=-1, keepdims=True)
    pg_top = 1.0 / jnp.sum(jnp.where(is_g, jnp.exp(logit - mg), 0.0), axis=-1, keepdims=True)
    g_idx = jnp.min(jnp.where(lg == mg, lane, far), axis=-1, keepdims=True)
    in_grp = (lane_i >= N_GROUPS) & (lane_i < N_GROUPS + N_EXP) & (lane_grp == g_idx)
    le = jnp.where(in_grp, logit, neg)
    v1 = jnp.max(le, axis=-1, keepdims=True)
    i1 = jnp.min(jnp.where(le == v1, lane, far), axis=-1, keepdims=True)
    le2 = jnp.where(lane == i1, neg, le)
    v2 = jnp.max(le2, axis=-1, keepdims=True)
    i2 = jnp.min(jnp.where(le2 == v2, lane, far), axis=-1, keepdims=True)
    e1 = i1 - N_GROUPS
    e2 = i2 - N_GROUPS
    tt = jnp.exp(v2 - v1)
    w1 = pg_top / (1.0 + tt)
    w2 = pg_top * tt / (1.0 + tt)

    hot = ((lane == e1) | (lane == e2)).astype(F32)
    before = _dot(earlier, hot.astype(BF16)) + run
    r1 = jnp.sum(jnp.where(lane == e1, before, 0.0), axis=-1, keepdims=True)
    r2 = jnp.sum(jnp.where(lane == e2, before, 0.0), axis=-1, keepdims=True)

    rt = jnp.where(lane_i == 0, e1, 0.0)
    rt = jnp.where(lane_i == 1, e2, rt)
    rt = jnp.where(lane_i == 2, w1, rt)
    rt = jnp.where(lane_i == 3, w2, rt)
    rt = jnp.where(lane_i == 4, r1, rt)
    rt = jnp.where(lane_i == 5, r2, rt)
    rt_ref[...] = rt
    ert_ref[...] = rt.T[0:8, :]
    return run + counted * jnp.sum(hot, axis=0, keepdims=True)


def _post(x_ctx, x_lat, a_ctx, a_lat, g_ctx, g_lat, mod_all, seq_lat,
          w_mg, w_ba, w_bg, w_o, l1g, l1b, w_r, b_r):
    t_ctx, t_lat = x_ctx.shape[0], x_lat.shape[0]
    tm = TM_TOK
    nct, nlt = t_ctx // tm, t_lat // tm
    per_seq = seq_lat // tm
    nb_lat = t_lat // seq_lat
    t = t_ctx + t_lat
    n = nct + nlt
    lat = lambda i: jnp.clip(i - nct, 0, nlt - 1)
    cmap = lambda i: (jnp.minimum(i, nct - 1), 0)
    lmap = lambda i: (lat(i), 0)
    mmap = lambda i: (jnp.where(i < nct, nb_lat, lat(i) // per_seq), 0, 0)
    row = lambda i: (jnp.minimum(i, n - 1), 0)
    prev = lambda i: jnp.maximum(i - 1, 0)
    const = lambda i: (0, 0)
    return pl.pallas_call(
        functools.partial(_post_kernel, n_ctx_tiles=nct),
        grid=(n + 1,),
        in_specs=[pl.BlockSpec((tm, D), cmap), pl.BlockSpec((tm, D), lmap),
                  pl.BlockSpec((tm, AW), cmap), pl.BlockSpec((tm, AW), lmap),
                  pl.BlockSpec((tm, GVW), cmap), pl.BlockSpec((tm, GVW), lmap),
                  pl.BlockSpec((1, 1, 6 * D), mmap),
                  pl.BlockSpec((D, MG_WIDTH), const), pl.BlockSpec((AW, D), const),
                  pl.BlockSpec((GVW, D), const), pl.BlockSpec((D, D), const),
                  pl.BlockSpec((1, D), const), pl.BlockSpec((1, D), const),
                  pl.BlockSpec((D, LANES), const), pl.BlockSpec((1, LANES), const)],
        out_specs=[pl.BlockSpec((tm, D), row),
                   pl.BlockSpec((tm, ROW_SUB, LANES), lambda i: (jnp.minimum(i, n - 1), 0, 0)),
                   pl.BlockSpec((tm, LANES), lambda i: (prev(i), 0)),
                   pl.BlockSpec((8, tm), lambda i: (0, prev(i))),
                   pl.BlockSpec((8, LANES), const)],
        out_shape=[jax.ShapeDtypeStruct((t, D), F32), jax.ShapeDtypeStruct((t, ROW_SUB, LANES), BF16),
                   jax.ShapeDtypeStruct((t, LANES), F32), jax.ShapeDtypeStruct((8, t), F32),
                   jax.ShapeDtypeStruct((8, LANES), F32)],
        scratch_shapes=[pltpu.VMEM((8, LANES), F32), pltpu.VMEM((ROW_GROUP * ROW_SUB, LANES), F32),
                        pltpu.VMEM((tm, LANES), F32)],
        compiler_params=_cparams(("arbitrary",)),
        name="post",
    )(x_ctx, x_lat, a_ctx, a_lat, g_ctx, g_lat, mod_all, w_mg, w_ba, w_bg, w_o, l1g, l1b, w_r, b_r)


ROW_UNROLL = 8


def _row_copy(src_ref, dst_ref, sem):
    return pltpu.make_async_copy(src_ref, dst_ref, sem)


def _scatter_kernel(pos0_ref, pos1_ref, h_ref, xs_ref, sem, zero_s, zero_sem):
    ts = h_ref.shape[0]
    n_rows = xs_ref.shape[0] - TM_EXP

    @pl.when(pl.program_id(0) == 0)
    def _():
        zero_s[...] = jnp.zeros_like(zero_s)
        pad = _row_copy(zero_s, xs_ref.at[pl.ds(n_rows, TM_EXP)], zero_sem)
        pad.start()
        pad.wait()

    def issue(g, carry):
        r0 = pl.multiple_of(g * ROW_UNROLL, ROW_UNROLL)
        for k in range(ROW_UNROLL):
            src = _row_slab(h_ref, r0 + k)
            _row_copy(src, _row_slab(xs_ref, pos0_ref[0, r0 + k]), sem).start(priority=0)
            _row_copy(src, _row_slab(xs_ref, pos1_ref[0, r0 + k]), sem).start(priority=1)
        return carry

    lax.fori_loop(0, ts // ROW_UNROLL, issue, 0)
    for _ in range(2):
        _row_copy(h_ref, xs_ref.at[pl.ds(0, ts)], sem).wait()


def _scatter_rows(h2p, pos0, pos1):
    t = h2p.shape[0]
    ts = TS_ROWS
    smem = lambda: pl.BlockSpec((None, 1, ts), lambda i: (i, 0, 0), memory_space=pltpu.SMEM)
    return pl.pallas_call(
        _scatter_kernel,
        grid=(t // ts,),
        in_specs=[smem(), smem(), pl.BlockSpec((ts, ROW_SUB, LANES), lambda i: (i, 0, 0))],
        out_specs=pl.BlockSpec(memory_space=pl.ANY),
        out_shape=jax.ShapeDtypeStruct((2 * t + TM_EXP, ROW_SUB, LANES), BF16),
        scratch_shapes=[pltpu.SemaphoreType.DMA(()), pltpu.VMEM((TM_EXP, ROW_SUB, LANES), BF16),
                        pltpu.SemaphoreType.DMA(())],
        compiler_params=_cparams(("arbitrary",)),
        name="scatter",
    )(pos0.reshape(t // ts, 1, ts), pos1.reshape(t // ts, 1, ts), h2p)


def _expert_kernel(start_ref, nwin_ref, xs_ref, wg_ref, wu_ref, wd_ref, ys_ref,
                   wgu_s, wd_s, in_buf, out_buf, slab_s, in_sem, out_sem):
    e = pl.program_id(0)
    tm = TM_EXP
    wgu_s[:, 0:D_EXP] = wg_ref[...].astype(BF16)
    wgu_s[:, D_EXP:2 * D_EXP] = wu_ref[...].astype(BF16)
    wd_s[...] = wd_ref[...].astype(BF16)
    base = start_ref[e]
    n = nwin_ref[e]

    @pl.when(e == 0)
    def _():
        out_buf[0] = jnp.zeros(out_buf.shape[1:], BF16)
        pad = pltpu.make_async_copy(out_buf.at[0], ys_ref.at[pl.ds(ys_ref.shape[0] - tm, tm)], out_sem.at[0])
        pad.start()
        pad.wait()

    def read(w, slot, first_row=None):
        first_row = base if first_row is None else first_row
        return pltpu.make_async_copy(xs_ref.at[pl.ds(first_row + w * tm, tm)], in_buf.at[slot], in_sem.at[slot])

    def write(w, slot):
        return pltpu.make_async_copy(out_buf.at[slot], ys_ref.at[pl.ds(base + w * tm, tm)], out_sem.at[slot])

    def start_first_reads(expert):
        for a in range(READ_AHEAD):
            @pl.when(nwin_ref[expert] > a)
            def _(a=a):
                read(a, a, start_ref[expert]).start()

    @pl.when(e == 0)
    def _():
        start_first_reads(0)

    def body(w, carry):
        slot = w % 2
        rslot = w % (READ_AHEAD + 1)

        @pl.when(w + READ_AHEAD < n)
        def _():
            read(w + READ_AHEAD, (w + READ_AHEAD) % (READ_AHEAD + 1)).start()

        read(w, rslot).wait()

        @pl.when(w >= 2)
        def _():
            write(w - 2, slot).wait()

        gu = _dot(_load_row_slabs(in_buf.at[rslot], slab_s).astype(BF16), wgu_s[...])
        hid = _silu(gu[:, 0:D_EXP]) * gu[:, D_EXP:2 * D_EXP]
        _store_row_slabs(out_buf.at[slot], _dot(hid.astype(BF16), wd_s[...]), slab_s)
        write(w, slot).start()
        return carry

    lax.fori_loop(0, n, body, 0)

    @pl.when(e + 1 < pl.num_programs(0))
    def _():
        start_first_reads(e + 1)

    @pl.when(n >= 1)
    def _():
        write(n - 1, (n - 1) % 2).wait()

    @pl.when(n >= 2)
    def _():
        write(n - 2, n % 2).wait()


def _experts(xs, starts, n_win, w_gate, w_up, w_down):
    tm = TM_EXP
    wmap = lambda e, st, nw: (e, 0, 0)
    slab = (tm, ROW_SUB, LANES)
    return pl.pallas_call(
        _expert_kernel,
        grid_spec=pltpu.PrefetchScalarGridSpec(
            num_scalar_prefetch=2,
            grid=(N_EXP,),
            in_specs=[pl.BlockSpec(memory_space=pl.ANY),
                      pl.BlockSpec((None, D, D_EXP), wmap), pl.BlockSpec((None, D, D_EXP), wmap),
                      pl.BlockSpec((None, D_EXP, D), wmap)],
            out_specs=pl.BlockSpec(memory_space=pl.ANY),
            scratch_shapes=[pltpu.VMEM((D, 2 * D_EXP), BF16), pltpu.VMEM((D_EXP, D), BF16),
                            pltpu.VMEM((READ_AHEAD + 1,) + slab, BF16), pltpu.VMEM((2,) + slab, BF16),
                            pltpu.VMEM((tm * ROW_SUB, LANES), F32),
                            pltpu.SemaphoreType.DMA((READ_AHEAD + 1,)), pltpu.SemaphoreType.DMA((2,))]),
        out_shape=jax.ShapeDtypeStruct(xs.shape, BF16),
        compiler_params=_cparams(("arbitrary",)),
        name="experts",
    )(starts, n_win, xs, w_gate, w_up, w_down)


def _final_kernel(p0c_ref, p1c_ref, p0n_ref, p1n_ref, x1_ref, rt_ref, mod_ref, l2g_ref, l2b_ref, ys_ref,
                  oc_ref, ol_ref, buf, sem, slab_s, *, n_ctx_tiles):
    i = pl.program_id(0)
    n = pl.num_programs(0)
    tm = x1_ref.shape[0]

    def gather(p0_ref, p1_ref, slot):
        def issue(g, carry):
            r0 = pl.multiple_of(g * ROW_UNROLL, ROW_UNROLL)
            for k in range(ROW_UNROLL):
                _row_copy(_row_slab(ys_ref, p0_ref[0, r0 + k]),
                          _row_slab(buf.at[slot, 0], r0 + k), sem.at[slot]).start(priority=0)
                _row_copy(_row_slab(ys_ref, p1_ref[0, r0 + k]),
                          _row_slab(buf.at[slot, 1], r0 + k), sem.at[slot]).start(priority=1)
            return carry

        lax.fori_loop(0, tm // ROW_UNROLL, issue, 0)

    cur = i % 2

    @pl.when(i == 0)
    def _():
        gather(p0c_ref, p1c_ref, 0)

    @pl.when(i + 1 < n)
    def _():
        gather(p0n_ref, p1n_ref, 1 - cur)

    for k in range(2):
        _row_copy(ys_ref.at[pl.ds(0, tm)], buf.at[cur, k], sem.at[cur]).wait()

    g2 = mod_ref[0, :, 5 * D:6 * D]
    w1 = rt_ref[:, 2:3]
    w2 = rt_ref[:, 3:4]
    moe = (w1 * _load_row_slabs(buf.at[cur, 0], slab_s.at[0])
           + w2 * _load_row_slabs(buf.at[cur, 1], slab_s.at[1]))
    out = _ln(ALPHA * x1_ref[...] + g2 * moe) * l2g_ref[...] + l2b_ref[...]

    @pl.when(i < n_ctx_tiles)
    def _():
        oc_ref[...] = out

    @pl.when(i >= n_ctx_tiles)
    def _():
        ol_ref[...] = out


def _final(x1, rt, pos0, pos1, mod_all, l2g, l2b, ys, t_ctx, seq_lat):
    t = x1.shape[0]
    tm = TM_TOK
    nt = t // tm
    nct = t_ctx // tm
    t_lat = t - t_ctx
    per_seq = seq_lat // tm
    nb_lat = t_lat // seq_lat
    p0 = pos0.reshape(nt, 1, tm)
    p1 = pos1.reshape(nt, 1, tm)
    row = lambda i: (i, 0)
    const = lambda i: (0, 0)
    mmap = lambda i: (jnp.where(i < nct, nb_lat, jnp.maximum(i - nct, 0) // per_seq), 0, 0)
    smem_cur = lambda: pl.BlockSpec((None, 1, tm), lambda i: (i, 0, 0), memory_space=pltpu.SMEM)
    smem_nxt = lambda: pl.BlockSpec((None, 1, tm), lambda i: (jnp.minimum(i + 1, nt - 1), 0, 0),
                                    memory_space=pltpu.SMEM)
    return pl.pallas_call(
        functools.partial(_final_kernel, n_ctx_tiles=nct),
        grid=(nt,),
        in_specs=[smem_cur(), smem_cur(), smem_nxt(), smem_nxt(),
                  pl.BlockSpec((tm, D), row), pl.BlockSpec((tm, LANES), row),
                  pl.BlockSpec((1, 1, 6 * D), mmap),
                  pl.BlockSpec((1, D), const), pl.BlockSpec((1, D), const),
                  pl.BlockSpec(memory_space=pl.ANY)],
        out_specs=[pl.BlockSpec((tm, D), lambda i: (jnp.minimum(i, nct - 1), 0)),
                   pl.BlockSpec((tm, D), lambda i: (jnp.maximum(i - nct, 0), 0))],
        out_shape=[jax.ShapeDtypeStruct((t_ctx, D), F32), jax.ShapeDtypeStruct((t_lat, D), F32)],
        scratch_shapes=[pltpu.VMEM((2, 2, tm, ROW_SUB, LANES), BF16), pltpu.SemaphoreType.DMA((2,)),
                        pltpu.VMEM((2, tm * ROW_SUB, LANES), F32)],
        compiler_params=_cparams(("arbitrary",)),
        name="final",
    )(p0, p1, p0, p1, x1, rt, mod_all, l2g, l2b, ys)


def _reorder_q_heads(w, axis):
    shape = w.shape
    split = shape[:axis] + (N_KV_HEADS, N_Q_HEADS // N_KV_HEADS, HD) + shape[axis + 1:]
    return jnp.swapaxes(w.reshape(split), axis, axis + 1).reshape(shape)


def _rope_tables(seq):
    t = np.arange(seq)
    half = HD // 4
    inv = (ROPE_THETA ** (-np.arange(half, dtype=np.float64) / half)).astype(np.float32)
    d64 = np.arange(LANES) % HD
    pos = np.where((d64 < HD // 2)[None, :], (t // GRID_W)[:, None], (t % GRID_W)[:, None])
    ang = (pos.astype(np.float32) * inv[d64 % half][None, :]).astype(np.float64)
    sign = np.where((d64 % 32) < 16, -1.0, 1.0)
    return (jnp.asarray(np.cos(ang), F32), jnp.asarray(np.sin(ang) * sign[None, :], F32))


def _pair_states(s):
    b = s.shape[0]
    s = s.reshape(b, 2, 2, GLA_DK, GLA_DV)
    return s.transpose(0, 1, 4, 2, 3).reshape(b, 2, GLA_DV, 2 * GLA_DK)


def _unpair_states(s):
    b = s.shape[0]
    s = s.reshape(b, 2, GLA_DV, 2, GLA_DK)
    return s.transpose(0, 1, 3, 4, 2).reshape(b, GLA_H, GLA_DK, GLA_DV)


def _route_tables(ert, counts):
    i32 = jnp.int32
    cnt = counts[0, :N_EXP].astype(i32)
    starts = jnp.cumsum(cnt) - cnt
    table = lambda e: jnp.sum(jnp.where(e[None, :] == jnp.arange(N_EXP, dtype=i32)[:, None],
                                        starts[:, None], 0), axis=0)
    pos0 = table(ert[0].astype(i32)) + ert[4].astype(i32)
    pos1 = table(ert[1].astype(i32)) + ert[5].astype(i32)
    return pos0, pos1, starts, (cnt + (TM_EXP - 1)) // TM_EXP


def kernel(x_prompt, x_sample, cache_k, cache_v, state_gla_fwd, state_gla_bwd, c, c_ctx, w_ada, b_ada, w_in, q_norm, k_norm, gla_w_gate, gla_b_gate, gla_norm, w_br_attn, w_br_gla, w_out, ln1_g, ln1_b, router_group_w, router_group_b, router_expert_w, router_expert_b, exp_w_gate, exp_w_up, exp_w_down, ln2_g, ln2_b):
    b_ctx, seq_ctx, _ = x_prompt.shape
    b_lat, seq_lat, _ = x_sample.shape
    t_ctx, t_lat = b_ctx * seq_ctx, b_lat * seq_lat
    t = t_ctx + t_lat
    l = 0

    rows = -(-(b_lat + 1) // 8) * 8
    c_rows = jnp.zeros((rows, D), F32).at[:b_lat].set(c).at[b_lat].set(c_ctx)
    mod = _ada(c_rows, w_ada[l], b_ada[l][None, :])
    mod_all = mod[:b_lat + 1, None, :]
    mod_lat = mod_all[:b_lat]
    mod_ctx = mod_all[b_lat:]

    w_full = w_in[l]
    w_a = jnp.concatenate([_reorder_q_heads(w_full[:, :AW], 1), w_full[:, AW:A_WIDTH]], axis=1).astype(BF16)
    w_mg = w_full[:, A_WIDTH:].astype(BF16)
    gain = jnp.concatenate([jnp.tile(q_norm[l], N_Q_HEADS), jnp.tile(k_norm[l], N_KV_HEADS)])[None, :]
    head_of = np.arange(AW + KVW) // HD
    ind = jnp.asarray((head_of[:, None] == np.arange(LANES)[None, :]) / HD, BF16)
    w_ba = _reorder_q_heads(w_br_attn[l], 0).astype(BF16)
    w_bg = w_br_gla[l].astype(BF16)
    w_o = w_out[l].astype(BF16)
    w_r = jnp.zeros((D, LANES), F32).at[:, :N_GROUPS].set(router_group_w[l])
    w_r = w_r.at[:, N_GROUPS:N_GROUPS + N_EXP].set(router_expert_w[l]).astype(BF16)
    b_r = jnp.zeros((1, LANES), F32).at[0, :N_GROUPS].set(router_group_b[l])
    b_r = b_r.at[0, N_GROUPS:N_GROUPS + N_EXP].set(router_expert_b[l])
    wg = jnp.zeros((2, 2 * GATE_RANK, GKW), F32)
    wg = wg.at[0, :GATE_RANK].set(gla_w_gate[l, 0]).at[1, GATE_RANK:].set(gla_w_gate[l, 1])
    bg = gla_b_gate[l][:, None, :]
    gn = gla_norm[l][None, :]

    xc = x_prompt.reshape(t_ctx, D)
    xl = x_sample.reshape(t_lat, D)

    q_c, k_c, v_c, qg_c, kg_c, vg_c, rs_c, lr_c, kf_c, vf_c = _inproj(
        xc, mod_ctx, w_a, gain, ind, None, seq_ctx, latent=False)
    attn_c = _attention(q_c, k_c, v_c, None, seq_ctx)
    zero_state = jnp.zeros((1, 2, 2, LANES, LANES), F32)
    gla_c, sfin_c = _gla(qg_c, kg_c, vg_c, lr_c, rs_c, zero_state, wg, bg, gn, seq_ctx)

    q_l, k_l, v_l, qg_l, kg_l, vg_l, rs_l, lr_l = _inproj(
        xl, mod_lat, w_a, gain, ind, _rope_tables(seq_lat), seq_lat, latent=True)
    past = cache_k.shape[2]
    kc = cache_k[:, l].reshape(b_lat, past, KVW).astype(BF16)
    vc = cache_v[:, l].reshape(b_lat, past, KVW).astype(BF16)
    attn_l = _attention(q_l, k_l, v_l, (kc, vc), seq_lat)
    s0 = jnp.stack([_pair_states(state_gla_fwd[:, l]), _pair_states(state_gla_bwd[:, l])], axis=1)
    gla_l, _ = _gla(qg_l, kg_l, vg_l, lr_l, rs_l, s0, wg, bg, gn, seq_lat)

    x1, h2p, rt, ert, counts = _post(xc, xl, attn_c, attn_l, gla_c, gla_l, mod_all, seq_lat,
                                w_mg, w_ba, w_bg, w_o, ln1_g[l][None, :], ln1_b[l][None, :], w_r, b_r)

    pos0, pos1, starts, n_win = _route_tables(ert, counts)
    xs = _scatter_rows(h2p, pos0, pos1)
    ys = _experts(xs, starts, n_win, exp_w_gate[l], exp_w_up[l], exp_w_down[l])
    y_ctx, y_lat = _final(x1, rt, pos0, pos1, mod_all, ln2_g[l][None, :], ln2_b[l][None, :], ys,
                          t_ctx, seq_lat)

    untranspose = lambda a: a.reshape(b_ctx, 1, N_KV_HEADS, HD, seq_ctx).transpose(0, 1, 4, 2, 3)
    new_k, new_v = untranspose(kf_c), untranspose(vf_c)
    new_sf = _unpair_states(sfin_c[:, 0])[:, None]
    new_sb = _unpair_states(sfin_c[:, 1])[:, None]
    return (y_ctx.reshape(b_ctx, seq_ctx, D), y_lat.reshape(b_lat, seq_lat, D),
            new_k, new_v, new_sf, new_sb)
```

```python
import functools

import numpy as np
import jax
import jax.numpy as jnp
from jax import lax
from jax.experimental import pallas as pl
from jax.experimental.pallas import tpu as pltpu

F32 = jnp.float32
BF16 = jnp.bfloat16

D = 1024
GRID_W = 64
HD = 64
N_Q_HEADS = 8
N_KV_HEADS = 2
AW = N_Q_HEADS * HD
KVW = N_KV_HEADS * HD
ROPE_THETA = 10000.0
GLA_H = 4
GLA_DK = 64
GLA_DV = 128
GKW = GLA_H * GLA_DK
GVW = GLA_H * GLA_DV
GATE_RANK = 16
GLA_TAU = 16.0
CHUNK = 64
N_GROUPS = 4
EPG = 8
N_EXP = N_GROUPS * EPG
D_EXP = 256
DEPTH = 1
ALPHA = (2.0 * DEPTH) ** 0.25
LN_EPS = 1e-6
RMS_EPS = 1e-6

LANES = 128
A_WIDTH = AW + 2 * KVW + 2 * GKW + 2 * GVW + 2 * GATE_RANK
MG_WIDTH = 2 * D
TM_TOK = 512
ROW_GROUP = 128
LOG2E = 1.4426950408889634
ONES_ROWS = 16
TQ_LAT = 256
TM_EXP = 256
READ_AHEAD = 3
TS_ROWS = 2048
VMEM_LIMIT = 56 * 1024 * 1024


def _cparams(sem):
    return pltpu.CompilerParams(dimension_semantics=sem, vmem_limit_bytes=VMEM_LIMIT)


def _dot(a, b):
    return jnp.dot(a, b, preferred_element_type=F32)


def _dot_nt(a, b):
    return lax.dot_general(a, b, (((1,), (1,)), ((), ())), preferred_element_type=F32)


def _dot_tn(a, b):
    return lax.dot_general(a, b, (((0,), (0,)), ((), ())), preferred_element_type=F32)


def _ln(x):
    mu = jnp.mean(x, axis=-1, keepdims=True)
    xc = x - mu
    var = jnp.mean(xc * xc, axis=-1, keepdims=True)
    return xc * lax.rsqrt(var + LN_EPS)


def _silu(x):
    return x * jax.nn.sigmoid(x)


def _split_bf16(x):
    hi = x.astype(BF16)
    lo = (x - hi.astype(F32)).astype(BF16)
    return hi, lo


def _ada_kernel(c_ref, w_ref, b_ref, o_ref):
    rows = c_ref.shape[0]
    s_hi, s_mid, s_lo = _split3_bf16(_silu(c_ref[...]))
    w_hi, w_lo = _split_bf16(w_ref[...])
    a = _dot(jnp.concatenate([s_hi, s_mid, s_lo], axis=0), w_hi)
    b = _dot(jnp.concatenate([s_hi, s_mid], axis=0), w_lo)
    o_ref[...] = (a[0:rows] + a[rows:2 * rows] + a[2 * rows:3 * rows]
                  + b[0:rows] + b[rows:2 * rows] + b_ref[...])


def _ada(c_rows, w_ada, b_ada):
    rows = c_rows.shape[0]
    n = w_ada.shape[1]
    bn = 1024
    return pl.pallas_call(
        _ada_kernel,
        grid=(n // bn,),
        in_specs=[pl.BlockSpec((rows, D), lambda j: (0, 0)),
                  pl.BlockSpec((D, bn), lambda j: (0, j)),
                  pl.BlockSpec((1, bn), lambda j: (0, j))],
        out_specs=pl.BlockSpec((rows, bn), lambda j: (0, j)),
        out_shape=jax.ShapeDtypeStruct((rows, n), F32),
        compiler_params=_cparams(("arbitrary",)),
        name="ada",
    )(c_rows, w_ada, b_ada)


def _inproj_kernel(*refs, latent, seq):
    if latent:
        (x_ref, mod_ref, w_ref, gain_ref, ind_ref, cos_ref, sin_ref,
         q_ref, k_ref, v_ref, qg_ref, kg_ref, vg_ref, rs_ref, lr_ref) = refs
    else:
        (x_ref, mod_ref, w_ref, gain_ref, ind_ref,
         q_ref, k_ref, v_ref, qg_ref, kg_ref, vg_ref, rs_ref, lr_ref, kf_ref, vf_ref) = refs
    tm = x_ref.shape[0]
    sub = ROW_GROUP
    n_groups = tm // sub
    sh1 = mod_ref[0, :, 0:D]
    sc1 = mod_ref[0, :, D:2 * D]
    lane = lax.broadcasted_iota(jnp.int32, (sub, LANES), 1)
    low_half = lane < HD
    first = (lane % 32) < 16

    def project(g):
        rows = slice(g * sub, (g + 1) * sub)
        h = (_ln(x_ref[rows, :]) * (1.0 + sc1) + sh1).astype(BF16)
        return _dot(h, w_ref[...])

    def finish(g, res):
        rows = slice(g * sub, (g + 1) * sub)
        qk = res[:, 0:AW + KVW]
        hi, lo = _split_bf16(qk * qk)
        ms = _dot(hi, ind_ref[...]) + _dot(lo, ind_ref[...])
        r = lax.rsqrt(ms + RMS_EPS)
        for s in range(5):
            rb = jnp.where(low_half, r[:, 2 * s:2 * s + 1], r[:, 2 * s + 1:2 * s + 2])
            y = res[:, LANES * s:LANES * (s + 1)] * rb * gain_ref[:, LANES * s:LANES * (s + 1)]
            if s == 4 and not latent:
                kf_ref[(g * sub) // seq, :, (g * sub) % seq:(g * sub) % seq + sub] = y.T
            if latent:
                partner = jnp.where(first, pltpu.roll(y, LANES - 16, 1), pltpu.roll(y, 16, 1))
                y = y * cos_ref[rows, :] + partner * sin_ref[rows, :]
            if s < 4:
                q_ref[rows, LANES * s:LANES * (s + 1)] = (y * (HD ** -0.5 * LOG2E)).astype(BF16)
            else:
                k_ref[rows, :] = y.astype(BF16)
        o = AW + KVW
        v = res[:, o:o + KVW]
        v_ref[rows, :] = v.astype(BF16)
        if not latent:
            vf_ref[(g * sub) // seq, :, (g * sub) % seq:(g * sub) % seq + sub] = v.T
        o += KVW
        qg_ref[rows, :] = res[:, o:o + GKW] * (GLA_DK ** -0.5)
        o += GKW
        kg_ref[rows, :] = res[:, o:o + GKW]
        o += GKW
        vg_ref[rows, :] = res[:, o:o + GVW].astype(BF16)
        o += GVW
        rs_ref[rows, :] = _silu(res[:, o:o + GVW]).astype(BF16)
        o += GVW
        lr_ref[rows, :] = res[:, o:o + 2 * GATE_RANK]

    pending = {0: project(0)}
    for g in range(n_groups):
        if g + 1 < n_groups:
            pending[g + 1] = project(g + 1)
        finish(g, pending.pop(g))


def _inproj(x2, mod, w_a, gain, ind, rope, seq, latent):
    t = x2.shape[0]
    tm = TM_TOK
    per_seq = max(seq // tm, 1)
    per_tile = max(tm // seq, 1)
    row = lambda i: (i, 0)
    const = lambda i: (0, 0)
    in_specs = [pl.BlockSpec((tm, D), row),
                pl.BlockSpec((1, 1, 6 * D), (lambda i: (i // per_seq, 0, 0)) if latent else (lambda i: (0, 0, 0))),
                pl.BlockSpec((D, A_WIDTH), const),
                pl.BlockSpec((1, AW + KVW), const),
                pl.BlockSpec((AW + KVW, LANES), const)]
    args = [x2, mod, w_a, gain, ind]
    if latent:
        in_specs += [pl.BlockSpec((tm, LANES), lambda i: (i % per_seq, 0))] * 2
        args += list(rope)
    widths = [(AW, BF16), (KVW, BF16), (KVW, BF16), (GKW, F32), (GKW, F32), (GVW, BF16), (GVW, BF16),
              (2 * GATE_RANK, F32)]
    out_specs = [pl.BlockSpec((tm, w), row) for w, _ in widths]
    out_shape = [jax.ShapeDtypeStruct((t, w), dt) for w, dt in widths]
    if not latent:
        cache_spec = pl.BlockSpec((per_tile, KVW, seq), lambda i: (i, 0, 0))
        out_specs += [cache_spec] * 2
        out_shape += [jax.ShapeDtypeStruct((t // seq, KVW, seq), F32)] * 2
    return pl.pallas_call(
        functools.partial(_inproj_kernel, latent=latent, seq=seq),
        grid=(t // tm,),
        in_specs=in_specs,
        out_specs=out_specs,
        out_shape=out_shape,
        compiler_params=_cparams(("parallel",)),
        name="inproj_lat" if latent else "inproj_ctx",
    )(*args)


def _attn_kernel(*refs, has_cache):
    def transposed_with_ones(dst, src):
        dst[0:KVW, :] = src[...].astype(F32).T.astype(BF16)
        dst[KVW:, :] = jnp.ones((ONES_ROWS, dst.shape[1]), BF16)

    if has_cache:
        q_ref, k_ref, v_ref, kc_ref, vc_ref, o_ref, vt_s, vct_s = refs

        @pl.when(pl.program_id(1) == 0)
        def _():
            transposed_with_ones(vt_s, v_ref)
            transposed_with_ones(vct_s, vc_ref)
    else:
        q_ref, k_ref, v_ref, o_ref, vt_s = refs
        transposed_with_ones(vt_s, v_ref)
    tq = q_ref.shape[0]

    lane = lax.broadcasted_iota(jnp.int32, (tq, LANES), 1)
    low_half = lane < HD
    k = k_ref[...]
    scores = []
    for j in range(N_KV_HEADS):
        keep = low_half if j == 0 else jnp.logical_not(low_half)
        zero = jnp.zeros((tq, LANES), BF16)
        for pair in range(2):
            qs = jnp.concatenate([jnp.where(keep, q_ref[:, LANES * s:LANES * (s + 1)], zero)
                                  for s in (2 * pair, 2 * pair + 1)], axis=0)
            s1 = _dot_nt(k, qs)
            s2 = _dot_nt(kc_ref[...], qs) if has_cache else None
            scores.append((s1, s2))
    outs = []
    for s1, s2 in scores:
        m = jnp.max(s1, axis=0, keepdims=True)
        if has_cache:
            m = jnp.maximum(m, jnp.max(s2, axis=0, keepdims=True))
        acc = _dot(vt_s[...], jnp.exp2(s1 - m).astype(BF16))
        if has_cache:
            acc = acc + _dot(vct_s[...], jnp.exp2(s2 - m).astype(BF16))
        outs.append(acc[0:KVW] / acc[KVW:KVW + 1])
    head0 = jnp.concatenate(outs[0:2], axis=1)
    head1 = jnp.concatenate(outs[2:4], axis=1)
    row = lax.broadcasted_iota(jnp.int32, (LANES, 4 * tq), 0)
    out = jnp.where(row < HD, head0, head1).T
    for s in range(4):
        o_ref[:, LANES * s:LANES * (s + 1)] = out[s * tq:(s + 1) * tq].astype(BF16)


def _attention(q, k, v, cache, seq):
    t = q.shape[0]
    if cache is None:
        tq = seq
        grid = (t // seq,)
        qmap = lambda b: (b, 0)
        in_specs = [pl.BlockSpec((tq, AW), qmap), pl.BlockSpec((seq, KVW), qmap),
                    pl.BlockSpec((seq, KVW), qmap)]
        args = [q, k, v]
        scratch = [pltpu.VMEM((KVW + ONES_ROWS, seq), BF16)]
        sem = ("parallel",)
        name = "attn_ctx"
    else:
        tq = TQ_LAT
        nq = seq // tq
        kc, vc = cache
        past = kc.shape[1]
        grid = (t // seq, nq)
        qmap = lambda b, i: (b * nq + i, 0)
        kmap = lambda b, i: (b, 0)
        cmap = lambda b, i: (b, 0, 0)
        in_specs = [pl.BlockSpec((tq, AW), qmap), pl.BlockSpec((seq, KVW), kmap),
                    pl.BlockSpec((seq, KVW), kmap),
                    pl.BlockSpec((None, past, KVW), cmap), pl.BlockSpec((None, past, KVW), cmap)]
        args = [q, k, v, kc, vc]
        scratch = [pltpu.VMEM((KVW + ONES_ROWS, seq), BF16), pltpu.VMEM((KVW + ONES_ROWS, past), BF16)]
        sem = ("parallel", "arbitrary")
        name = "attn_lat"
    return pl.pallas_call(
        functools.partial(_attn_kernel, has_cache=cache is not None),
        grid=grid,
        in_specs=in_specs,
        out_specs=pl.BlockSpec((tq, AW), qmap),
        out_shape=jax.ShapeDtypeStruct((t, AW), BF16),
        scratch_shapes=scratch,
        compiler_params=_cparams(sem),
        name=name,
    )(*args)


GLA_BLK = 256
GLA_UNROLL = 8


def _split3_bf16(x):
    hi = x.astype(BF16)
    r1 = x - hi.astype(F32)
    mid = r1.astype(BF16)
    lo = (r1 - mid.astype(F32)).astype(BF16)
    return hi, mid, lo


def _gla_kernel(qg_ref, kg_ref, vg_ref, lr_ref, rs_ref, s0_ref, wg_ref, bg_ref, gn_ref,
                o_ref, sfin_ref, cum_s, kv_s, dec_s):
    n = qg_ref.shape[0]
    nc = n // CHUNK
    unroll = min(GLA_UNROLL, nc)
    lane = lax.broadcasted_iota(jnp.int32, (CHUNK, LANES), 1)
    low_half = lane < GLA_DK
    lane_sq = lax.broadcasted_iota(jnp.int32, (LANES, LANES), 1)
    low_half_sq = lane_sq < GLA_DK
    ri = lax.broadcasted_iota(jnp.int32, (2 * CHUNK, 2 * CHUNK), 0)
    ci = lax.broadcasted_iota(jnp.int32, (2 * CHUNK, 2 * CHUNK), 1)
    diag = (ri >> 6) == (ci >> 6)
    keep2 = (diag & (ri >= ci), diag & (ci >= ri))

    rb = lax.broadcasted_iota(jnp.int32, (GLA_BLK, GLA_BLK), 0)
    cb = lax.broadcasted_iota(jnp.int32, (GLA_BLK, GLA_BLK), 1)
    same = (rb >> 6) == (cb >> 6)
    tri = ((same & (rb >= cb)).astype(BF16), (same & (cb >= rb)).astype(BF16))
    for d in range(2):
        w_hi, w_lo = _split_bf16(wg_ref[d])
        for blk in range(n // GLA_BLK):
            rows = slice(blk * GLA_BLK, (blk + 1) * GLA_BLK)
            l_hi, l_lo = _split_bf16(lr_ref[rows, :])
            z = _dot(l_hi, w_hi) + _dot(l_lo, w_hi) + _dot(l_hi, w_lo) + bg_ref[d]
            logg = (jnp.minimum(z, 0.0) - jnp.log(1.0 + jnp.exp(-jnp.abs(z)))) * (1.0 / GLA_TAU)
            pieces = _split3_bf16(logg)
            cum_s[d, rows, :] = _dot(tri[d], pieces[0]) + _dot(tri[d], pieces[1]) + _dot(tri[d], pieces[2])

    def pass1(i, carry):
        cs = [i * unroll + u for u in range(unroll)]
        rws = [pl.ds(pl.multiple_of(c * CHUNK, CHUNK), CHUNK) for c in cs]
        prods = []
        for c, rows in zip(cs, rws):
            kc = kg_ref[rows, :]
            kdec = []
            for d in range(2):
                cum = cum_s[d, rows, :]
                last = cum[CHUNK - 1:CHUNK, :] if d == 0 else cum[0:1, :]
                dec_s[d, c] = jnp.exp(last)
                kdec.append((kc * jnp.exp(last - cum)).astype(BF16))
            for p in range(2):
                vpair = vg_ref[rows, GLA_DV * 2 * p:GLA_DV * 2 * (p + 1)]
                kpair = jnp.concatenate([kdec[0][:, LANES * p:LANES * (p + 1)],
                                         kdec[1][:, LANES * p:LANES * (p + 1)]], axis=1)
                prods.append(_dot_tn(vpair, kpair))
        for j, c in enumerate(cs):
            for p in range(2):
                res = prods[2 * j + p]
                for d in range(2):
                    cols = slice(LANES * d, LANES * (d + 1))
                    kv_s[d, c, p] = jnp.where(low_half_sq, res[0:GLA_DV, cols], res[GLA_DV:2 * GLA_DV, cols])
        return carry

    lax.fori_loop(0, nc // unroll, pass1, 0)

    for d in range(2):
        def scan(i, st):
            c = i if d == 0 else nc - 1 - i
            dec = dec_s[d, c]
            new = []
            for p in range(2):
                kv = kv_s[d, c, p]
                kv_s[d, c, p] = st[p]
                new.append(st[p] * dec[:, LANES * p:LANES * (p + 1)] + kv)
            return tuple(new)

        fin = lax.fori_loop(0, nc, scan, (s0_ref[d, 0], s0_ref[d, 1]))
        sfin_ref[d, 0] = fin[0]
        sfin_ref[d, 1] = fin[1]

    def pass3(i, carry):
        cs = [i * unroll + u for u in range(unroll)]
        rws = [pl.ds(pl.multiple_of(c * CHUNK, CHUNK), CHUNK) for c in cs]
        first = []
        for c, rows in zip(cs, rws):
            q = qg_ref[rows, :]
            k = kg_ref[rows, :]
            for d in range(2):
                cum = cum_s[d, rows, :]
                qt = q * jnp.exp(cum)
                kt = (k * jnp.exp(-cum)).astype(BF16)
                for p in range(2):
                    qs = qt[:, LANES * p:LANES * (p + 1)]
                    lhs = jnp.concatenate([jnp.where(low_half, qs, 0.0), jnp.where(low_half, 0.0, qs)],
                                          axis=0).astype(BF16)
                    kts = kt[:, LANES * p:LANES * (p + 1)]
                    rhs = jnp.concatenate([kts, kts, kv_s[d, c, p].astype(BF16)], axis=0)
                    first.append(_dot_nt(lhs, rhs))
        second = []
        for j, rows in enumerate(rws):
            for d in range(2):
                for p in range(2):
                    res = first[4 * j + 2 * d + p]
                    vp = jnp.concatenate([vg_ref[rows, GLA_DV * (2 * p):GLA_DV * (2 * p + 1)],
                                          vg_ref[rows, GLA_DV * (2 * p + 1):GLA_DV * (2 * p + 2)]], axis=0)
                    a = jnp.where(keep2[d], res[:, 0:2 * CHUNK], 0.0).astype(BF16)
                    second.append(_dot(a, vp) + res[:, 2 * CHUNK:])
        for j, rows in enumerate(rws):
            for p in range(2):
                tot = second[4 * j + p] + second[4 * j + 2 + p]
                y = tot * lax.rsqrt(jnp.mean(tot * tot, axis=-1, keepdims=True) + RMS_EPS) * gn_ref[...]
                for hh in range(2):
                    cols = slice(GLA_DV * (2 * p + hh), GLA_DV * (2 * p + hh + 1))
                    o_ref[rows, cols] = (y[CHUNK * hh:CHUNK * (hh + 1)]
                                         * rs_ref[rows, cols].astype(F32)).astype(BF16)
        return carry

    lax.fori_loop(0, nc // unroll, pass3, 0)


def _gla(qg, kg, vg, lr, rs, s0, wg, bg, gn, seq):
    t = qg.shape[0]
    nb = t // seq
    nc = seq // CHUNK
    row = lambda b: (b, 0)
    c3 = lambda b: (0, 0, 0)
    st = lambda b: (b, 0, 0, 0, 0)
    if s0.shape[0] == 1:
        s0map = lambda b: (0, 0, 0, 0, 0)
    else:
        s0map = st
    return pl.pallas_call(
        _gla_kernel,
        grid=(nb,),
        in_specs=[pl.BlockSpec((seq, GKW), row), pl.BlockSpec((seq, GKW), row),
                  pl.BlockSpec((seq, GVW), row), pl.BlockSpec((seq, 2 * GATE_RANK), row),
                  pl.BlockSpec((seq, GVW), row),
                  pl.BlockSpec((None, 2, 2, LANES, LANES), s0map),
                  pl.BlockSpec((2, 2 * GATE_RANK, GKW), c3), pl.BlockSpec((2, 1, GKW), c3),
                  pl.BlockSpec((1, GLA_DV), lambda b: (0, 0))],
        out_specs=[pl.BlockSpec((seq, GVW), row),
                   pl.BlockSpec((None, 2, 2, LANES, LANES), st)],
        out_shape=[jax.ShapeDtypeStruct((t, GVW), BF16),
                   jax.ShapeDtypeStruct((nb, 2, 2, LANES, LANES), F32)],
        scratch_shapes=[pltpu.VMEM((2, seq, GKW), F32),
                        pltpu.VMEM((2, nc, 2, LANES, LANES), F32),
                        pltpu.VMEM((2, nc, 1, GKW), F32)],
        compiler_params=_cparams(("parallel",)),
        name="gla_lat" if seq > 256 else "gla_ctx",
    )(qg, kg, vg, lr, rs, s0, wg, bg, gn)


ROW_SUB = D // LANES


def _store_row_slabs(ref, x, tmp):
    m = x.shape[0]
    for c in range(ROW_SUB):
        tmp[pl.ds(c, m, stride=ROW_SUB), :] = x[:, LANES * c:LANES * (c + 1)]
    ref[...] = tmp[...].reshape(m, ROW_SUB, LANES).astype(BF16)


def _load_row_slabs(ref, tmp):
    m = ref.shape[0]
    tmp[...] = ref[...].astype(F32).reshape(m * ROW_SUB, LANES)
    return jnp.concatenate([tmp[pl.ds(c, m, stride=ROW_SUB), :] for c in range(ROW_SUB)], axis=1)


def _row_slab(ref, row):
    return ref.at[pl.ds(row, 1)]


def _post_kernel(xc_ref, xl_ref, ac_ref, al_ref, gc_ref, gl_ref, mod_ref,
                 wmg_ref, wba_ref, wbg_ref, wo_ref, l1g_ref, l1b_ref, wr_ref, br_ref,
                 x1_ref, h2_ref, rt_ref, ert_ref, cnt_ref, run_s, slab_s, logit_s, *, n_ctx_tiles):
    i = pl.program_id(0)
    tm = xc_ref.shape[0]
    sub = ROW_GROUP
    n_groups = tm // sub
    is_ctx = i < n_ctx_tiles
    parts = [slice(g * sub, (g + 1) * sub) for g in range(n_groups)]

    @pl.when(i == 0)
    def _():
        run_s[...] = jnp.zeros_like(run_s)
        logit_s[...] = jnp.zeros_like(logit_s)

    sh1 = mod_ref[0, :, 0:D]
    sc1 = mod_ref[0, :, D:2 * D]
    g1 = mod_ref[0, :, 2 * D:3 * D]
    sh2 = mod_ref[0, :, 3 * D:4 * D]
    sc2 = mod_ref[0, :, 4 * D:5 * D]

    xs = [jnp.where(is_ctx, xc_ref[r, :], xl_ref[r, :]) for r in parts]
    hs = [(_ln(x) * (1.0 + sc1) + sh1).astype(BF16) for x in xs]
    gates = [jax.nn.sigmoid(_dot(h, wmg_ref[...])) for h in hs]
    ba = [_dot(jnp.where(is_ctx, ac_ref[r, :], al_ref[r, :]), wba_ref[...]) for r in parts]
    bg = [_dot(jnp.where(is_ctx, gc_ref[r, :], gl_ref[r, :]), wbg_ref[...]) for r in parts]

    ri = lax.broadcasted_iota(jnp.int32, (sub, sub), 0)
    ci = lax.broadcasted_iota(jnp.int32, (sub, sub), 1)
    earlier = (ri > ci).astype(BF16)
    counted = (i > 0).astype(F32)
    run = run_s[0:1, :]
    for r in parts:
        run = _route(logit_s[r, :], earlier, run, counted, rt_ref.at[r, :], ert_ref.at[:, r])
    run_s[0:1, :] = run
    cnt_ref[...] = jnp.broadcast_to(run, cnt_ref.shape)

    merged = [(g[:, :D] * a + g[:, D:] * b).astype(BF16) for g, a, b in zip(gates, ba, bg)]
    mix = [_dot(m, wo_ref[...]) for m in merged]
    x1s = [_ln(ALPHA * x + g1 * m) * l1g_ref[...] + l1b_ref[...] for x, m in zip(xs, mix)]
    h2s = [_ln(x1) * (1.0 + sc2) + sh2 for x1 in x1s]
    logits = [_dot(h2.astype(BF16), wr_ref[...]) + br_ref[...] for h2 in h2s]
    for g, r in enumerate(parts):
        x1_ref[r, :] = x1s[g]
        _store_row_slabs(h2_ref.at[pl.ds(g * sub, sub)], h2s[g], slab_s)
        logit_s[r, :] = logits[g]


def _route(logit, earlier, run, counted, rt_ref, ert_ref):
    tm = logit.shape[0]
    lane_i = lax.broadcasted_iota(jnp.int32, (tm, LANES), 1)
    lane = lane_i.astype(F32)
    lane_grp = ((lane_i - N_GROUPS) >> 3).astype(F32)
    neg = jnp.float32(-jnp.inf)
    far = jnp.float32(LANES)
    is_g = lane_i < N_GROUPS
    lg = jnp.where(is_g, logit, neg)
    mg = jnp.max(lg, axis=-1, keepdims=True)
    pg_top = 1.0 / jnp.sum(jnp.where(is_g, jnp.exp(logit - mg), 0.0), axis=-1, keepdims=True)
    g_idx = jnp.min(jnp.where(lg == mg, lane, far), axis=-1, keepdims=True)
    in_grp = (lane_i >= N_GROUPS) & (lane_i < N_GROUPS + N_EXP) & (lane_grp == g_idx)
    le = jnp.where(in_grp, logit, neg)
    v1 = jnp.max(le, axis=-1, keepdims=True)
    i1 = jnp.min(jnp.where(le == v1, lane, far), axis=-1, keepdims=True)
    le2 = jnp.where(lane == i1, neg, le)
    v2 = jnp.max(le2, axis=-1, keepdims=True)
    i2 = jnp.min(jnp.where(le2 == v2, lane, far), axis=-1, keepdims=True)
    e1 = i1 - N_GROUPS
    e2 = i2 - N_GROUPS
    tt = jnp.exp(v2 - v1)
    w1 = pg_top / (1.0 + tt)
    w2 = pg_top * tt / (1.0 + tt)

    hot = ((lane == e1) | (lane == e2)).astype(F32)
    before = _dot(earlier, hot.astype(BF16)) + run
    r1 = jnp.sum(jnp.where(lane == e1, before, 0.0), axis=-1, keepdims=True)
    r2 = jnp.sum(jnp.where(lane == e2, before, 0.0), axis=-1, keepdims=True)

    rt = jnp.where(lane_i == 0, e1, 0.0)
    rt = jnp.where(lane_i == 1, e2, rt)
    rt = jnp.where(lane_i == 2, w1, rt)
    rt = jnp.where(lane_i == 3, w2, rt)
    rt = jnp.where(lane_i == 4, r1, rt)
    rt = jnp.where(lane_i == 5, r2, rt)
    rt_ref[...] = rt
    ert_ref[...] = rt.T[0:8, :]
    return run + counted * jnp.sum(hot, axis=0, keepdims=True)


def _post(x_ctx, x_lat, a_ctx, a_lat, g_ctx, g_lat, mod_all, seq_lat,
          w_mg, w_ba, w_bg, w_o, l1g, l1b, w_r, b_r):
    t_ctx, t_lat = x_ctx.shape[0], x_lat.shape[0]
    tm = TM_TOK
    nct, nlt = t_ctx // tm, t_lat // tm
    per_seq = seq_lat // tm
    nb_lat = t_lat // seq_lat
    t = t_ctx + t_lat
    n = nct + nlt
    lat = lambda i: jnp.clip(i - nct, 0, nlt - 1)
    cmap = lambda i: (jnp.minimum(i, nct - 1), 0)
    lmap = lambda i: (lat(i), 0)
    mmap = lambda i: (jnp.where(i < nct, nb_lat, lat(i) // per_seq), 0, 0)
    row = lambda i: (jnp.minimum(i, n - 1), 0)
    prev = lambda i: jnp.maximum(i - 1, 0)
    const = lambda i: (0, 0)
    return pl.pallas_call(
        functools.partial(_post_kernel, n_ctx_tiles=nct),
        grid=(n + 1,),
        in_specs=[pl.BlockSpec((tm, D), cmap), pl.BlockSpec((tm, D), lmap),
                  pl.BlockSpec((tm, AW), cmap), pl.BlockSpec((tm, AW), lmap),
                  pl.BlockSpec((tm, GVW), cmap), pl.BlockSpec((tm, GVW), lmap),
                  pl.BlockSpec((1, 1, 6 * D), mmap),
                  pl.BlockSpec((D, MG_WIDTH), const), pl.BlockSpec((AW, D), const),
                  pl.BlockSpec((GVW, D), const), pl.BlockSpec((D, D), const),
                  pl.BlockSpec((1, D), const), pl.BlockSpec((1, D), const),
                  pl.BlockSpec((D, LANES), const), pl.BlockSpec((1, LANES), const)],
        out_specs=[pl.BlockSpec((tm, D), row),
                   pl.BlockSpec((tm, ROW_SUB, LANES), lambda i: (jnp.minimum(i, n - 1), 0, 0)),
                   pl.BlockSpec((tm, LANES), lambda i: (prev(i), 0)),
                   pl.BlockSpec((8, tm), lambda i: (0, prev(i))),
                   pl.BlockSpec((8, LANES), const)],
        out_shape=[jax.ShapeDtypeStruct((t, D), F32), jax.ShapeDtypeStruct((t, ROW_SUB, LANES), BF16),
                   jax.ShapeDtypeStruct((t, LANES), F32), jax.ShapeDtypeStruct((8, t), F32),
                   jax.ShapeDtypeStruct((8, LANES), F32)],
        scratch_shapes=[pltpu.VMEM((8, LANES), F32), pltpu.VMEM((ROW_GROUP * ROW_SUB, LANES), F32),
                        pltpu.VMEM((tm, LANES), F32)],
        compiler_params=_cparams(("arbitrary",)),
        name="post",
    )(x_ctx, x_lat, a_ctx, a_lat, g_ctx, g_lat, mod_all, w_mg, w_ba, w_bg, w_o, l1g, l1b, w_r, b_r)


ROW_UNROLL = 8


def _row_copy(src_ref, dst_ref, sem):
    return pltpu.make_async_copy(src_ref, dst_ref, sem)


def _scatter_kernel(pos0_ref, pos1_ref, h_ref, xs_ref, sem, zero_s, zero_sem):
    ts = h_ref.shape[0]
    n_rows = xs_ref.shape[0] - TM_EXP

    @pl.when(pl.program_id(0) == 0)
    def _():
        zero_s[...] = jnp.zeros_like(zero_s)
        pad = _row_copy(zero_s, xs_ref.at[pl.ds(n_rows, TM_EXP)], zero_sem)
        pad.start()
        pad.wait()

    def issue(g, carry):
        r0 = pl.multiple_of(g * ROW_UNROLL, ROW_UNROLL)
        for k in range(ROW_UNROLL):
            src = _row_slab(h_ref, r0 + k)
            _row_copy(src, _row_slab(xs_ref, pos0_ref[0, r0 + k]), sem).start(priority=0)
            _row_copy(src, _row_slab(xs_ref, pos1_ref[0, r0 + k]), sem).start(priority=1)
        return carry

    lax.fori_loop(0, ts // ROW_UNROLL, issue, 0)
    for _ in range(2):
        _row_copy(h_ref, xs_ref.at[pl.ds(0, ts)], sem).wait()


def _scatter_rows(h2p, pos0, pos1):
    t = h2p.shape[0]
    ts = TS_ROWS
    smem = lambda: pl.BlockSpec((None, 1, ts), lambda i: (i, 0, 0), memory_space=pltpu.SMEM)
    return pl.pallas_call(
        _scatter_kernel,
        grid=(t // ts,),
        in_specs=[smem(), smem(), pl.BlockSpec((ts, ROW_SUB, LANES), lambda i: (i, 0, 0))],
        out_specs=pl.BlockSpec(memory_space=pl.ANY),
        out_shape=jax.ShapeDtypeStruct((2 * t + TM_EXP, ROW_SUB, LANES), BF16),
        scratch_shapes=[pltpu.SemaphoreType.DMA(()), pltpu.VMEM((TM_EXP, ROW_SUB, LANES), BF16),
                        pltpu.SemaphoreType.DMA(())],
        compiler_params=_cparams(("arbitrary",)),
        name="scatter",
    )(pos0.reshape(t // ts, 1, ts), pos1.reshape(t // ts, 1, ts), h2p)


def _expert_kernel(start_ref, nwin_ref, xs_ref, wg_ref, wu_ref, wd_ref, ys_ref,
                   wgu_s, wd_s, in_buf, out_buf, slab_s, in_sem, out_sem):
    e = pl.program_id(0)
    tm = TM_EXP
    wgu_s[:, 0:D_EXP] = wg_ref[...].astype(BF16)
    wgu_s[:, D_EXP:2 * D_EXP] = wu_ref[...].astype(BF16)
    wd_s[...] = wd_ref[...].astype(BF16)
    base = start_ref[e]
    n = nwin_ref[e]

    @pl.when(e == 0)
    def _():
        out_buf[0] = jnp.zeros(out_buf.shape[1:], BF16)
        pad = pltpu.make_async_copy(out_buf.at[0], ys_ref.at[pl.ds(ys_ref.shape[0] - tm, tm)], out_sem.at[0])
        pad.start()
        pad.wait()

    def read(w, slot, first_row=None):
        first_row = base if first_row is None else first_row
        return pltpu.make_async_copy(xs_ref.at[pl.ds(first_row + w * tm, tm)], in_buf.at[slot], in_sem.at[slot])

    def write(w, slot):
        return pltpu.make_async_copy(out_buf.at[slot], ys_ref.at[pl.ds(base + w * tm, tm)], out_sem.at[slot])

    def start_first_reads(expert):
        for a in range(READ_AHEAD):
            @pl.when(nwin_ref[expert] > a)
            def _(a=a):
                read(a, a, start_ref[expert]).start()

    @pl.when(e == 0)
    def _():
        start_first_reads(0)

    def body(w, carry):
        slot = w % 2
        rslot = w % (READ_AHEAD + 1)

        @pl.when(w + READ_AHEAD < n)
        def _():
            read(w + READ_AHEAD, (w + READ_AHEAD) % (READ_AHEAD + 1)).start()

        read(w, rslot).wait()

        @pl.when(w >= 2)
        def _():
            write(w - 2, slot).wait()

        gu = _dot(_load_row_slabs(in_buf.at[rslot], slab_s).astype(BF16), wgu_s[...])
        hid = _silu(gu[:, 0:D_EXP]) * gu[:, D_EXP:2 * D_EXP]
        _store_row_slabs(out_buf.at[slot], _dot(hid.astype(BF16), wd_s[...]), slab_s)
        write(w, slot).start()
        return carry

    lax.fori_loop(0, n, body, 0)

    @pl.when(e + 1 < pl.num_programs(0))
    def _():
        start_first_reads(e + 1)

    @pl.when(n >= 1)
    def _():
        write(n - 1, (n - 1) % 2).wait()

    @pl.when(n >= 2)
    def _():
        write(n - 2, n % 2).wait()


def _experts(xs, starts, n_win, w_gate, w_up, w_down):
    tm = TM_EXP
    wmap = lambda e, st, nw: (e, 0, 0)
    slab = (tm, ROW_SUB, LANES)
    return pl.pallas_call(
        _expert_kernel,
        grid_spec=pltpu.PrefetchScalarGridSpec(
            num_scalar_prefetch=2,
            grid=(N_EXP,),
            in_specs=[pl.BlockSpec(memory_space=pl.ANY),
                      pl.BlockSpec((None, D, D_EXP), wmap), pl.BlockSpec((None, D, D_EXP), wmap),
                      pl.BlockSpec((None, D_EXP, D), wmap)],
            out_specs=pl.BlockSpec(memory_space=pl.ANY),
            scratch_shapes=[pltpu.VMEM((D, 2 * D_EXP), BF16), pltpu.VMEM((D_EXP, D), BF16),
                            pltpu.VMEM((READ_AHEAD + 1,) + slab, BF16), pltpu.VMEM((2,) + slab, BF16),
                            pltpu.VMEM((tm * ROW_SUB, LANES), F32),
                            pltpu.SemaphoreType.DMA((READ_AHEAD + 1,)), pltpu.SemaphoreType.DMA((2,))]),
        out_shape=jax.ShapeDtypeStruct(xs.shape, BF16),
        compiler_params=_cparams(("arbitrary",)),
        name="experts",
    )(starts, n_win, xs, w_gate, w_up, w_down)


def _final_kernel(p0c_ref, p1c_ref, p0n_ref, p1n_ref, x1_ref, rt_ref, mod_ref, l2g_ref, l2b_ref, ys_ref,
                  oc_ref, ol_ref, buf, sem, slab_s, *, n_ctx_tiles):
    i = pl.program_id(0)
    n = pl.num_programs(0)
    tm = x1_ref.shape[0]

    def gather(p0_ref, p1_ref, slot):
        def issue(g, carry):
            r0 = pl.multiple_of(g * ROW_UNROLL, ROW_UNROLL)
            for k in range(ROW_UNROLL):
                _row_copy(_row_slab(ys_ref, p0_ref[0, r0 + k]),
                          _row_slab(buf.at[slot, 0], r0 + k), sem.at[slot]).start(priority=0)
                _row_copy(_row_slab(ys_ref, p1_ref[0, r0 + k]),
                          _row_slab(buf.at[slot, 1], r0 + k), sem.at[slot]).start(priority=1)
            return carry

        lax.fori_loop(0, tm // ROW_UNROLL, issue, 0)

    cur = i % 2

    @pl.when(i == 0)
    def _():
        gather(p0c_ref, p1c_ref, 0)

    @pl.when(i + 1 < n)
    def _():
        gather(p0n_ref, p1n_ref, 1 - cur)

    for k in range(2):
        _row_copy(ys_ref.at[pl.ds(0, tm)], buf.at[cur, k], sem.at[cur]).wait()

    g2 = mod_ref[0, :, 5 * D:6 * D]
    w1 = rt_ref[:, 2:3]
    w2 = rt_ref[:, 3:4]
    moe = (w1 * _load_row_slabs(buf.at[cur, 0], slab_s.at[0])
           + w2 * _load_row_slabs(buf.at[cur, 1], slab_s.at[1]))
    out = _ln(ALPHA * x1_ref[...] + g2 * moe) * l2g_ref[...] + l2b_ref[...]

    @pl.when(i < n_ctx_tiles)
    def _():
        oc_ref[...] = out

    @pl.when(i >= n_ctx_tiles)
    def _():
        ol_ref[...] = out


def _final(x1, rt, pos0, pos1, mod_all, l2g, l2b, ys, t_ctx, seq_lat):
    t = x1.shape[0]
    tm = TM_TOK
    nt = t // tm
    nct = t_ctx // tm
    t_lat = t - t_ctx
    per_seq = seq_lat // tm
    nb_lat = t_lat // seq_lat
    p0 = pos0.reshape(nt, 1, tm)
    p1 = pos1.reshape(nt, 1, tm)
    row = lambda i: (i, 0)
    const = lambda i: (0, 0)
    mmap = lambda i: (jnp.where(i < nct, nb_lat, jnp.maximum(i - nct, 0) // per_seq), 0, 0)
    smem_cur = lambda: pl.BlockSpec((None, 1, tm), lambda i: (i, 0, 0), memory_space=pltpu.SMEM)
    smem_nxt = lambda: pl.BlockSpec((None, 1, tm), lambda i: (jnp.minimum(i + 1, nt - 1), 0, 0),
                                    memory_space=pltpu.SMEM)
    return pl.pallas_call(
        functools.partial(_final_kernel, n_ctx_tiles=nct),
        grid=(nt,),
        in_specs=[smem_cur(), smem_cur(), smem_nxt(), smem_nxt(),
                  pl.BlockSpec((tm, D), row), pl.BlockSpec((tm, LANES), row),
                  pl.BlockSpec((1, 1, 6 * D), mmap),
                  pl.BlockSpec((1, D), const), pl.BlockSpec((1, D), const),
                  pl.BlockSpec(memory_space=pl.ANY)],
        out_specs=[pl.BlockSpec((tm, D), lambda i: (jnp.minimum(i, nct - 1), 0)),
                   pl.BlockSpec((tm, D), lambda i: (jnp.maximum(i - nct, 0), 0))],
        out_shape=[jax.ShapeDtypeStruct((t_ctx, D), F32), jax.ShapeDtypeStruct((t_lat, D), F32)],
        scratch_shapes=[pltpu.VMEM((2, 2, tm, ROW_SUB, LANES), BF16), pltpu.SemaphoreType.DMA((2,)),
                        pltpu.VMEM((2, tm * ROW_SUB, LANES), F32)],
        compiler_params=_cparams(("arbitrary",)),
        name="final",
    )(p0, p1, p0, p1, x1, rt, mod_all, l2g, l2b, ys)


def _reorder_q_heads(w, axis):
    shape = w.shape
    split = shape[:axis] + (N_KV_HEADS, N_Q_HEADS // N_KV_HEADS, HD) + shape[axis + 1:]
    return jnp.swapaxes(w.reshape(split), axis, axis + 1).reshape(shape)


def _rope_tables(seq):
    t = np.arange(seq)
    half = HD // 4
    inv = (ROPE_THETA ** (-np.arange(half, dtype=np.float64) / half)).astype(np.float32)
    d64 = np.arange(LANES) % HD
    pos = np.where((d64 < HD // 2)[None, :], (t // GRID_W)[:, None], (t % GRID_W)[:, None])
    ang = (pos.astype(np.float32) * inv[d64 % half][None, :]).astype(np.float64)
    sign = np.where((d64 % 32) < 16, -1.0, 1.0)
    return (jnp.asarray(np.cos(ang), F32), jnp.asarray(np.sin(ang) * sign[None, :], F32))


def _pair_states(s):
    b = s.shape[0]
    s = s.reshape(b, 2, 2, GLA_DK, GLA_DV)
    return s.transpose(0, 1, 4, 2, 3).reshape(b, 2, GLA_DV, 2 * GLA_DK)


def _unpair_states(s):
    b = s.shape[0]
    s = s.reshape(b, 2, GLA_DV, 2, GLA_DK)
    return s.transpose(0, 1, 3, 4, 2).reshape(b, GLA_H, GLA_DK, GLA_DV)


def _route_tables(ert, counts):
    i32 = jnp.int32
    cnt = counts[0, :N_EXP].astype(i32)
    starts = jnp.cumsum(cnt) - cnt
    table = lambda e: jnp.sum(jnp.where(e[None, :] == jnp.arange(N_EXP, dtype=i32)[:, None],
                                        starts[:, None], 0), axis=0)
    pos0 = table(ert[0].astype(i32)) + ert[4].astype(i32)
    pos1 = table(ert[1].astype(i32)) + ert[5].astype(i32)
    return pos0, pos1, starts, (cnt + (TM_EXP - 1)) // TM_EXP


def kernel(x_prompt, x_sample, cache_k, cache_v, state_gla_fwd, state_gla_bwd, c, c_ctx, w_ada, b_ada, w_in, q_norm, k_norm, gla_w_gate, gla_b_gate, gla_norm, w_br_attn, w_br_gla, w_out, ln1_g, ln1_b, router_group_w, router_group_b, router_expert_w, router_expert_b, exp_w_gate, exp_w_up, exp_w_down, ln2_g, ln2_b):
    b_ctx, seq_ctx, _ = x_prompt.shape
    b_lat, seq_lat, _ = x_sample.shape
    t_ctx, t_lat = b_ctx * seq_ctx, b_lat * seq_lat
    t = t_ctx + t_lat
    l = 0

    rows = -(-(b_lat + 1) // 8) * 8
    c_rows = jnp.zeros((rows, D), F32).at[:b_lat].set(c).at[b_lat].set(c_ctx)
    mod = _ada(c_rows, w_ada[l], b_ada[l][None, :])
    mod_all = mod[:b_lat + 1, None, :]
    mod_lat = mod_all[:b_lat]
    mod_ctx = mod_all[b_lat:]

    w_full = w_in[l]
    w_a = jnp.concatenate([_reorder_q_heads(w_full[:, :AW], 1), w_full[:, AW:A_WIDTH]], axis=1).astype(BF16)
    w_mg = w_full[:, A_WIDTH:].astype(BF16)
    gain = jnp.concatenate([jnp.tile(q_norm[l], N_Q_HEADS), jnp.tile(k_norm[l], N_KV_HEADS)])[None, :]
    head_of = np.arange(AW + KVW) // HD
    ind = jnp.asarray((head_of[:, None] == np.arange(LANES)[None, :]) / HD, BF16)
    w_ba = _reorder_q_heads(w_br_attn[l], 0).astype(BF16)
    w_bg = w_br_gla[l].astype(BF16)
    w_o = w_out[l].astype(BF16)
    w_r = jnp.zeros((D, LANES), F32).at[:, :N_GROUPS].set(router_group_w[l])
    w_r = w_r.at[:, N_GROUPS:N_GROUPS + N_EXP].set(router_expert_w[l]).astype(BF16)
    b_r = jnp.zeros((1, LANES), F32).at[0, :N_GROUPS].set(router_group_b[l])
    b_r = b_r.at[0, N_GROUPS:N_GROUPS + N_EXP].set(router_expert_b[l])
    wg = jnp.zeros((2, 2 * GATE_RANK, GKW), F32)
    wg = wg.at[0, :GATE_RANK].set(gla_w_gate[l, 0]).at[1, GATE_RANK:].set(gla_w_gate[l, 1])
    bg = gla_b_gate[l][:, None, :]
    gn = gla_norm[l][None, :]

    xc = x_prompt.reshape(t_ctx, D)
    xl = x_sample.reshape(t_lat, D)

    q_c, k_c, v_c, qg_c, kg_c, vg_c, rs_c, lr_c, kf_c, vf_c = _inproj(
        xc, mod_ctx, w_a, gain, ind, None, seq_ctx, latent=False)
    attn_c = _attention(q_c, k_c, v_c, None, seq_ctx)
    zero_state = jnp.zeros((1, 2, 2, LANES, LANES), F32)
    gla_c, sfin_c = _gla(qg_c, kg_c, vg_c, lr_c, rs_c, zero_state, wg, bg, gn, seq_ctx)

    q_l, k_l, v_l, qg_l, kg_l, vg_l, rs_l, lr_l = _inproj(
        xl, mod_lat, w_a, gain, ind, _rope_tables(seq_lat), seq_lat, latent=True)
    past = cache_k.shape[2]
    kc = cache_k[:, l].reshape(b_lat, past, KVW).astype(BF16)
    vc = cache_v[:, l].reshape(b_lat, past, KVW).astype(BF16)
    attn_l = _attention(q_l, k_l, v_l, (kc, vc), seq_lat)
    s0 = jnp.stack([_pair_states(state_gla_fwd[:, l]), _pair_states(state_gla_bwd[:, l])], axis=1)
    gla_l, _ = _gla(qg_l, kg_l, vg_l, lr_l, rs_l, s0, wg, bg, gn, seq_lat)

    x1, h2p, rt, ert, counts = _post(xc, xl, attn_c, attn_l, gla_c, gla_l, mod_all, seq_lat,
                                w_mg, w_ba, w_bg, w_o, ln1_g[l][None, :], ln1_b[l][None, :], w_r, b_r)

    pos0, pos1, starts, n_win = _route_tables(ert, counts)
    xs = _scatter_rows(h2p, pos0, pos1)
    ys = _experts(xs, starts, n_win, exp_w_gate[l], exp_w_up[l], exp_w_down[l])
    y_ctx, y_lat = _final(x1, rt, pos0, pos1, mod_all, ln2_g[l][None, :], ln2_b[l][None, :], ys,
                          t_ctx, seq_lat)

    untranspose = lambda a: a.reshape(b_ctx, 1, N_KV_HEADS, HD, seq_ctx).transpose(0, 1, 4, 2, 3)
    new_k, new_v = untranspose(kf_c), untranspose(vf_c)
    new_sf = _unpair_states(sfin_c[:, 0])[:, None]
    new_sb = _unpair_states(sfin_c[:, 1])[:, None]
    return (y_ctx.reshape(b_ctx, seq_ctx, D), y_lat.reshape(b_lat, seq_lat, D),
            new_k, new_v, new_sf, new_sb)
```

```python
import functools

import numpy as np
import jax
import jax.numpy as jnp
from jax import lax
from jax.experimental import pallas as pl
from jax.experimental.pallas import tpu as pltpu

F32 = jnp.float32
BF16 = jnp.bfloat16

D = 1024
GRID_W = 64
HD = 64
N_Q_HEADS = 8
N_KV_HEADS = 2
AW = N_Q_HEADS * HD
KVW = N_KV_HEADS * HD
ROPE_THETA = 10000.0
GLA_H = 4
GLA_DK = 64
GLA_DV = 128
GKW = GLA_H * GLA_DK
GVW = GLA_H * GLA_DV
GATE_RANK = 16
GLA_TAU = 16.0
CHUNK = 64
N_GROUPS = 4
EPG = 8
N_EXP = N_GROUPS * EPG
D_EXP = 256
DEPTH = 1
ALPHA = (2.0 * DEPTH) ** 0.25
LN_EPS = 1e-6
RMS_EPS = 1e-6

LANES = 128
A_WIDTH = AW + 2 * KVW + 2 * GKW + 2 * GVW + 2 * GATE_RANK
MG_WIDTH = 2 * D
TM_TOK = 512
ROW_GROUP = 128
LOG2E = 1.4426950408889634
ONES_ROWS = 16
TQ_LAT = 256
TM_EXP = 256
READ_AHEAD = 3
TS_ROWS = 2048
VMEM_LIMIT = 56 * 1024 * 1024


def _cparams(sem):
    return pltpu.CompilerParams(dimension_semantics=sem, vmem_limit_bytes=VMEM_LIMIT)


def _dot(a, b):
    return jnp.dot(a, b, preferred_element_type=F32)


def _dot_nt(a, b):
    return lax.dot_general(a, b, (((1,), (1,)), ((), ())), preferred_element_type=F32)


def _dot_tn(a, b):
    return lax.dot_general(a, b, (((0,), (0,)), ((), ())), preferred_element_type=F32)


def _ln(x):
    mu = jnp.mean(x, axis=-1, keepdims=True)
    xc = x - mu
    var = jnp.mean(xc * xc, axis=-1, keepdims=True)
    return xc * lax.rsqrt(var + LN_EPS)


def _silu(x):
    return x * jax.nn.sigmoid(x)


def _split_bf16(x):
    hi = x.astype(BF16)
    lo = (x - hi.astype(F32)).astype(BF16)
    return hi, lo


def _ada_kernel(c_ref, w_ref, b_ref, o_ref):
    rows = c_ref.shape[0]
    s_hi, s_mid, s_lo = _split3_bf16(_silu(c_ref[...]))
    w_hi, w_lo = _split_bf16(w_ref[...])
    a = _dot(jnp.concatenate([s_hi, s_mid, s_lo], axis=0), w_hi)
    b = _dot(jnp.concatenate([s_hi, s_mid], axis=0), w_lo)
    o_ref[...] = (a[0:rows] + a[rows:2 * rows] + a[2 * rows:3 * rows]
                  + b[0:rows] + b[rows:2 * rows] + b_ref[...])


def _ada(c_rows, w_ada, b_ada):
    rows = c_rows.shape[0]
    n = w_ada.shape[1]
    bn = 1024
    return pl.pallas_call(
        _ada_kernel,
        grid=(n // bn,),
        in_specs=[pl.BlockSpec((rows, D), lambda j: (0, 0)),
                  pl.BlockSpec((D, bn), lambda j: (0, j)),
                  pl.BlockSpec((1, bn), lambda j: (0, j))],
        out_specs=pl.BlockSpec((rows, bn), lambda j: (0, j)),
        out_shape=jax.ShapeDtypeStruct((rows, n), F32),
        compiler_params=_cparams(("arbitrary",)),
        name="ada",
    )(c_rows, w_ada, b_ada)


def _inproj_kernel(*refs, latent, seq):
    if latent:
        (x_ref, mod_ref, w_ref, gain_ref, ind_ref, cos_ref, sin_ref,
         q_ref, k_ref, v_ref, qg_ref, kg_ref, vg_ref, rs_ref, lr_ref) = refs
    else:
        (x_ref, mod_ref, w_ref, gain_ref, ind_ref,
         q_ref, k_ref, v_ref, qg_ref, kg_ref, vg_ref, rs_ref, lr_ref, kf_ref, vf_ref) = refs
    tm = x_ref.shape[0]
    sub = ROW_GROUP
    n_groups = tm // sub
    sh1 = mod_ref[0, :, 0:D]
    sc1 = mod_ref[0, :, D:2 * D]
    lane = lax.broadcasted_iota(jnp.int32, (sub, LANES), 1)
    low_half = lane < HD
    first = (lane % 32) < 16

    def project(g):
        rows = slice(g * sub, (g + 1) * sub)
        h = (_ln(x_ref[rows, :]) * (1.0 + sc1) + sh1).astype(BF16)
        return _dot(h, w_ref[...])

    def finish(g, res):
        rows = slice(g * sub, (g + 1) * sub)
        qk = res[:, 0:AW + KVW]
        hi, lo = _split_bf16(qk * qk)
        ms = _dot(hi, ind_ref[...]) + _dot(lo, ind_ref[...])
        r = lax.rsqrt(ms + RMS_EPS)
        for s in range(5):
            rb = jnp.where(low_half, r[:, 2 * s:2 * s + 1], r[:, 2 * s + 1:2 * s + 2])
            y = res[:, LANES * s:LANES * (s + 1)] * rb * gain_ref[:, LANES * s:LANES * (s + 1)]
            if s == 4 and not latent:
                kf_ref[(g * sub) // seq, :, (g * sub) % seq:(g * sub) % seq + sub] = y.T
            if latent:
                partner = jnp.where(first, pltpu.roll(y, LANES - 16, 1), pltpu.roll(y, 16, 1))
                y = y * cos_ref[rows, :] + partner * sin_ref[rows, :]
            if s < 4:
                q_ref[rows, LANES * s:LANES * (s + 1)] = (y * (HD ** -0.5 * LOG2E)).astype(BF16)
            else:
                k_ref[rows, :] = y.astype(BF16)
        o = AW + KVW
        v = res[:, o:o + KVW]
        v_ref[rows, :] = v.astype(BF16)
        if not latent:
            vf_ref[(g * sub) // seq, :, (g * sub) % seq:(g * sub) % seq + sub] = v.T
        o += KVW
        qg_ref[rows, :] = res[:, o:o + GKW] * (GLA_DK ** -0.5)
        o += GKW
        kg_ref[rows, :] = res[:, o:o + GKW]
        o += GKW
        vg_ref[rows, :] = res[:, o:o + GVW].astype(BF16)
        o += GVW
        rs_ref[rows, :] = _silu(res[:, o:o + GVW]).astype(BF16)
        o += GVW
        lr_ref[rows, :] = res[:, o:o + 2 * GATE_RANK]

    pending = {0: project(0)}
    for g in range(n_groups):
        if g + 1 < n_groups:
            pending[g + 1] = project(g + 1)
        finish(g, pending.pop(g))


def _inproj(x2, mod, w_a, gain, ind, rope, seq, latent):
    t = x2.shape[0]
    tm = TM_TOK
    per_seq = max(seq // tm, 1)
    per_tile = max(tm // seq, 1)
    row = lambda i: (i, 0)
    const = lambda i: (0, 0)
    in_specs = [pl.BlockSpec((tm, D), row),
                pl.BlockSpec((1, 1, 6 * D), (lambda i: (i // per_seq, 0, 0)) if latent else (lambda i: (0, 0, 0))),
                pl.BlockSpec((D, A_WIDTH), const),
                pl.BlockSpec((1, AW + KVW), const),
                pl.BlockSpec((AW + KVW, LANES), const)]
    args = [x2, mod, w_a, gain, ind]
    if latent:
        in_specs += [pl.BlockSpec((tm, LANES), lambda i: (i % per_seq, 0))] * 2
        args += list(rope)
    widths = [(AW, BF16), (KVW, BF16), (KVW, BF16), (GKW, F32), (GKW, F32), (GVW, BF16), (GVW, BF16),
              (2 * GATE_RANK, F32)]
    out_specs = [pl.BlockSpec((tm, w), row) for w, _ in widths]
    out_shape = [jax.ShapeDtypeStruct((t, w), dt) for w, dt in widths]
    if not latent:
        cache_spec = pl.BlockSpec((per_tile, KVW, seq), lambda i: (i, 0, 0))
        out_specs += [cache_spec] * 2
        out_shape += [jax.ShapeDtypeStruct((t // seq, KVW, seq), F32)] * 2
    return pl.pallas_call(
        functools.partial(_inproj_kernel, latent=latent, seq=seq),
        grid=(t // tm,),
        in_specs=in_specs,
        out_specs=out_specs,
        out_shape=out_shape,
        compiler_params=_cparams(("parallel",)),
        name="inproj_lat" if latent else "inproj_ctx",
    )(*args)


def _attn_kernel(*refs, has_cache):
    def transposed_with_ones(dst, src):
        dst[0:KVW, :] = src[...].astype(F32).T.astype(BF16)
        dst[KVW:, :] = jnp.ones((ONES_ROWS, dst.shape[1]), BF16)

    if has_cache:
        q_ref, k_ref, v_ref, kc_ref, vc_ref, o_ref, vt_s, vct_s = refs

        @pl.when(pl.program_id(1) == 0)
        def _():
            transposed_with_ones(vt_s, v_ref)
            transposed_with_ones(vct_s, vc_ref)
    else:
        q_ref, k_ref, v_ref, o_ref, vt_s = refs
        transposed_with_ones(vt_s, v_ref)
    tq = q_ref.shape[0]

    lane = lax.broadcasted_iota(jnp.int32, (tq, LANES), 1)
    low_half = lane < HD
    k = k_ref[...]
    scores = []
    for j in range(N_KV_HEADS):
        keep = low_half if j == 0 else jnp.logical_not(low_half)
        zero = jnp.zeros((tq, LANES), BF16)
        for pair in range(2):
            qs = jnp.concatenate([jnp.where(keep, q_ref[:, LANES * s:LANES * (s + 1)], zero)
                                  for s in (2 * pair, 2 * pair + 1)], axis=0)
            s1 = _dot_nt(k, qs)
            s2 = _dot_nt(kc_ref[...], qs) if has_cache else None
            scores.append((s1, s2))
    outs = []
    for s1, s2 in scores:
        m = jnp.max(s1, axis=0, keepdims=True)
        if has_cache:
            m = jnp.maximum(m, jnp.max(s2, axis=0, keepdims=True))
        acc = _dot(vt_s[...], jnp.exp2(s1 - m).astype(BF16))
        if has_cache:
            acc = acc + _dot(vct_s[...], jnp.exp2(s2 - m).astype(BF16))
        outs.append(acc[0:KVW] / acc[KVW:KVW + 1])
    head0 = jnp.concatenate(outs[0:2], axis=1)
    head1 = jnp.concatenate(outs[2:4], axis=1)
    row = lax.broadcasted_iota(jnp.int32, (LANES, 4 * tq), 0)
    out = jnp.where(row < HD, head0, head1).T
    for s in range(4):
        o_ref[:, LANES * s:LANES * (s + 1)] = out[s * tq:(s + 1) * tq].astype(BF16)


def _attention(q, k, v, cache, seq):
    t = q.shape[0]
    if cache is None:
        tq = seq
        grid = (t // seq,)
        qmap = lambda b: (b, 0)
        in_specs = [pl.BlockSpec((tq, AW), qmap), pl.BlockSpec((seq, KVW), qmap),
                    pl.BlockSpec((seq, KVW), qmap)]
        args = [q, k, v]
        scratch = [pltpu.VMEM((KVW + ONES_ROWS, seq), BF16)]
        sem = ("parallel",)
        name = "attn_ctx"
    else:
        tq = TQ_LAT
        nq = seq // tq
        kc, vc = cache
        past = kc.shape[1]
        grid = (t // seq, nq)
        qmap = lambda b, i: (b * nq + i, 0)
        kmap = lambda b, i: (b, 0)
        cmap = lambda b, i: (b, 0, 0)
        in_specs = [pl.BlockSpec((tq, AW), qmap), pl.BlockSpec((seq, KVW), kmap),
                    pl.BlockSpec((seq, KVW), kmap),
                    pl.BlockSpec((None, past, KVW), cmap), pl.BlockSpec((None, past, KVW), cmap)]
        args = [q, k, v, kc, vc]
        scratch = [pltpu.VMEM((KVW + ONES_ROWS, seq), BF16), pltpu.VMEM((KVW + ONES_ROWS, past), BF16)]
        sem = ("parallel", "arbitrary")
        name = "attn_lat"
    return pl.pallas_call(
        functools.partial(_attn_kernel, has_cache=cache is not None),
        grid=grid,
        in_specs=in_specs,
        out_specs=pl.BlockSpec((tq, AW), qmap),
        out_shape=jax.ShapeDtypeStruct((t, AW), BF16),
        scratch_shapes=scratch,
        compiler_params=_cparams(sem),
        name=name,
    )(*args)


GLA_BLK = 256
GLA_UNROLL = 8


def _split3_bf16(x):
    hi = x.astype(BF16)
    r1 = x - hi.astype(F32)
    mid = r1.astype(BF16)
    lo = (r1 - mid.astype(F32)).astype(BF16)
    return hi, mid, lo


def _gla_kernel(qg_ref, kg_ref, vg_ref, lr_ref, rs_ref, s0_ref, wg_ref, bg_ref, gn_ref,
                o_ref, sfin_ref, cum_s, kv_s, dec_s):
    n = qg_ref.shape[0]
    nc = n // CHUNK
    unroll = min(GLA_UNROLL, nc)
    lane = lax.broadcasted_iota(jnp.int32, (CHUNK, LANES), 1)
    low_half = lane < GLA_DK
    lane_sq = lax.broadcasted_iota(jnp.int32, (LANES, LANES), 1)
    low_half_sq = lane_sq < GLA_DK
    ri = lax.broadcasted_iota(jnp.int32, (2 * CHUNK, 2 * CHUNK), 0)
    ci = lax.broadcasted_iota(jnp.int32, (2 * CHUNK, 2 * CHUNK), 1)
    diag = (ri >> 6) == (ci >> 6)
    keep2 = (diag & (ri >= ci), diag & (ci >= ri))

    rb = lax.broadcasted_iota(jnp.int32, (GLA_BLK, GLA_BLK), 0)
    cb = lax.broadcasted_iota(jnp.int32, (GLA_BLK, GLA_BLK), 1)
    same = (rb >> 6) == (cb >> 6)
    tri = ((same & (rb >= cb)).astype(BF16), (same & (cb >= rb)).astype(BF16))
    for d in range(2):
        w_hi, w_lo = _split_bf16(wg_ref[d])
        for blk in range(n // GLA_BLK):
            rows = slice(blk * GLA_BLK, (blk + 1) * GLA_BLK)
            l_hi, l_lo = _split_bf16(lr_ref[rows, :])
            z = _dot(l_hi, w_hi) + _dot(l_lo, w_hi) + _dot(l_hi, w_lo) + bg_ref[d]
            logg = (jnp.minimum(z, 0.0) - jnp.log(1.0 + jnp.exp(-jnp.abs(z)))) * (1.0 / GLA_TAU)
            pieces = _split3_bf16(logg)
            cum_s[d, rows, :] = _dot(tri[d], pieces[0]) + _dot(tri[d], pieces[1]) + _dot(tri[d], pieces[2])

    def pass1(i, carry):
        cs = [i * unroll + u for u in range(unroll)]
        rws = [pl.ds(pl.multiple_of(c * CHUNK, CHUNK), CHUNK) for c in cs]
        prods = []
        for c, rows in zip(cs, rws):
            kc = kg_ref[rows, :]
            kdec = []
            for d in range(2):
                cum = cum_s[d, rows, :]
                last = cum[CHUNK - 1:CHUNK, :] if d == 0 else cum[0:1, :]
                dec_s[d, c] = jnp.exp(last)
                kdec.append((kc * jnp.exp(last - cum)).astype(BF16))
            for p in range(2):
                vpair = vg_ref[rows, GLA_DV * 2 * p:GLA_DV * 2 * (p + 1)]
                kpair = jnp.concatenate([kdec[0][:, LANES * p:LANES * (p + 1)],
                                         kdec[1][:, LANES * p:LANES * (p + 1)]], axis=1)
                prods.append(_dot_tn(vpair, kpair))
        for j, c in enumerate(cs):
            for p in range(2):
                res = prods[2 * j + p]
                for d in range(2):
                    cols = slice(LANES * d, LANES * (d + 1))
                    kv_s[d, c, p] = jnp.where(low_half_sq, res[0:GLA_DV, cols], res[GLA_DV:2 * GLA_DV, cols])
        return carry

    lax.fori_loop(0, nc // unroll, pass1, 0)

    for d in range(2):
        def scan(i, st):
            c = i if d == 0 else nc - 1 - i
            dec = dec_s[d, c]
            new = []
            for p in range(2):
                kv = kv_s[d, c, p]
                kv_s[d, c, p] = st[p]
                new.append(st[p] * dec[:, LANES * p:LANES * (p + 1)] + kv)
            return tuple(new)

        fin = lax.fori_loop(0, nc, scan, (s0_ref[d, 0], s0_ref[d, 1]))
        sfin_ref[d, 0] = fin[0]
        sfin_ref[d, 1] = fin[1]

    def pass3(i, carry):
        cs = [i * unroll + u for u in range(unroll)]
        rws = [pl.ds(pl.multiple_of(c * CHUNK, CHUNK), CHUNK) for c in cs]
        first = []
        for c, rows in zip(cs, rws):
            q = qg_ref[rows, :]
            k = kg_ref[rows, :]
            for d in range(2):
                cum = cum_s[d, rows, :]
                qt = q * jnp.exp(cum)
                kt = (k * jnp.exp(-cum)).astype(BF16)
                for p in range(2):
                    qs = qt[:, LANES * p:LANES * (p + 1)]
                    lhs = jnp.concatenate([jnp.where(low_half, qs, 0.0), jnp.where(low_half, 0.0, qs)],
                                          axis=0).astype(BF16)
                    kts = kt[:, LANES * p:LANES * (p + 1)]
                    rhs = jnp.concatenate([kts, kts, kv_s[d, c, p].astype(BF16)], axis=0)
                    first.append(_dot_nt(lhs, rhs))
        second = []
        for j, rows in enumerate(rws):
            for d in range(2):
                for p in range(2):
                    res = first[4 * j + 2 * d + p]
                    vp = jnp.concatenate([vg_ref[rows, GLA_DV * (2 * p):GLA_DV * (2 * p + 1)],
                                          vg_ref[rows, GLA_DV * (2 * p + 1):GLA_DV * (2 * p + 2)]], axis=0)
                    a = jnp.where(keep2[d], res[:, 0:2 * CHUNK], 0.0).astype(BF16)
                    second.append(_dot(a, vp) + res[:, 2 * CHUNK:])
        for j, rows in enumerate(rws):
            for p in range(2):
                tot = second[4 * j + p] + second[4 * j + 2 + p]
                y = tot * lax.rsqrt(jnp.mean(tot * tot, axis=-1, keepdims=True) + RMS_EPS) * gn_ref[...]
                for hh in range(2):
                    cols = slice(GLA_DV * (2 * p + hh), GLA_DV * (2 * p + hh + 1))
                    o_ref[rows, cols] = (y[CHUNK * hh:CHUNK * (hh + 1)]
                                         * rs_ref[rows, cols].astype(F32)).astype(BF16)
        return carry

    lax.fori_loop(0, nc // unroll, pass3, 0)


def _gla(qg, kg, vg, lr, rs, s0, wg, bg, gn, seq):
    t = qg.shape[0]
    nb = t // seq
    nc = seq // CHUNK
    row = lambda b: (b, 0)
    c3 = lambda b: (0, 0, 0)
    st = lambda b: (b, 0, 0, 0, 0)
    if s0.shape[0] == 1:
        s0map = lambda b: (0, 0, 0, 0, 0)
    else:
        s0map = st
    return pl.pallas_call(
        _gla_kernel,
        grid=(nb,),
        in_specs=[pl.BlockSpec((seq, GKW), row), pl.BlockSpec((seq, GKW), row),
                  pl.BlockSpec((seq, GVW), row), pl.BlockSpec((seq, 2 * GATE_RANK), row),
                  pl.BlockSpec((seq, GVW), row),
                  pl.BlockSpec((None, 2, 2, LANES, LANES), s0map),
                  pl.BlockSpec((2, 2 * GATE_RANK, GKW), c3), pl.BlockSpec((2, 1, GKW), c3),
                  pl.BlockSpec((1, GLA_DV), lambda b: (0, 0))],
        out_specs=[pl.BlockSpec((seq, GVW), row),
                   pl.BlockSpec((None, 2, 2, LANES, LANES), st)],
        out_shape=[jax.ShapeDtypeStruct((t, GVW), BF16),
                   jax.ShapeDtypeStruct((nb, 2, 2, LANES, LANES), F32)],
        scratch_shapes=[pltpu.VMEM((2, seq, GKW), F32),
                        pltpu.VMEM((2, nc, 2, LANES, LANES), F32),
                        pltpu.VMEM((2, nc, 1, GKW), F32)],
        compiler_params=_cparams(("parallel",)),
        name="gla_lat" if seq > 256 else "gla_ctx",
    )(qg, kg, vg, lr, rs, s0, wg, bg, gn)


ROW_SUB = D // LANES
ROW_DTYPE = F32


def _store_row_slabs(ref, x, tmp):
    m = x.shape[0]
    for c in range(ROW_SUB):
        tmp[pl.ds(c, m, stride=ROW_SUB), :] = x[:, LANES * c:LANES * (c + 1)]
    ref[...] = tmp[...].reshape(m, ROW_SUB, LANES).astype(ROW_DTYPE)


def _load_row_slabs(ref, tmp):
    m = ref.shape[0]
    tmp[...] = ref[...].astype(F32).reshape(m * ROW_SUB, LANES)
    return jnp.concatenate([tmp[pl.ds(c, m, stride=ROW_SUB), :] for c in range(ROW_SUB)], axis=1)


def _row_slab(ref, row):
    return ref.at[pl.ds(row, 1)]


def _post_kernel(xc_ref, xl_ref, ac_ref, al_ref, gc_ref, gl_ref, mod_ref,
                 wmg_ref, wba_ref, wbg_ref, wo_ref, l1g_ref, l1b_ref, wr_ref, br_ref,
                 x1_ref, h2_ref, rt_ref, ert_ref, cnt_ref, run_s, slab_s, logit_s, *, n_ctx_tiles):
    i = pl.program_id(0)
    tm = xc_ref.shape[0]
    sub = ROW_GROUP
    n_groups = tm // sub
    is_ctx = i < n_ctx_tiles
    parts = [slice(g * sub, (g + 1) * sub) for g in range(n_groups)]

    @pl.when(i == 0)
    def _():
        run_s[...] = jnp.zeros_like(run_s)
        logit_s[...] = jnp.zeros_like(logit_s)

    sh1 = mod_ref[0, :, 0:D]
    sc1 = mod_ref[0, :, D:2 * D]
    g1 = mod_ref[0, :, 2 * D:3 * D]
    sh2 = mod_ref[0, :, 3 * D:4 * D]
    sc2 = mod_ref[0, :, 4 * D:5 * D]

    xs = [jnp.where(is_ctx, xc_ref[r, :], xl_ref[r, :]) for r in parts]
    hs = [(_ln(x) * (1.0 + sc1) + sh1).astype(BF16) for x in xs]
    gates = [jax.nn.sigmoid(_dot(h, wmg_ref[...])) for h in hs]
    ba = [_dot(jnp.where(is_ctx, ac_ref[r, :], al_ref[r, :]), wba_ref[...]) for r in parts]
    bg = [_dot(jnp.where(is_ctx, gc_ref[r, :], gl_ref[r, :]), wbg_ref[...]) for r in parts]

    ri = lax.broadcasted_iota(jnp.int32, (sub, sub), 0)
    ci = lax.broadcasted_iota(jnp.int32, (sub, sub), 1)
    earlier = (ri > ci).astype(BF16)
    counted = (i > 0).astype(F32)
    run = run_s[0:1, :]
    for r in parts:
        run = _route(logit_s[r, :], earlier, run, counted, rt_ref.at[r, :], ert_ref.at[:, r])
    run_s[0:1, :] = run
    cnt_ref[...] = jnp.broadcast_to(run, cnt_ref.shape)

    merged = [(g[:, :D] * a + g[:, D:] * b).astype(BF16) for g, a, b in zip(gates, ba, bg)]
    mix = [_dot(m, wo_ref[...]) for m in merged]
    x1s = [_ln(ALPHA * x + g1 * m) * l1g_ref[...] + l1b_ref[...] for x, m in zip(xs, mix)]
    h2s = [_ln(x1) * (1.0 + sc2) + sh2 for x1 in x1s]
    logits = [_dot(h2.astype(BF16), wr_ref[...]) + br_ref[...] for h2 in h2s]
    for g, r in enumerate(parts):
        x1_ref[r, :] = x1s[g]
        _store_row_slabs(h2_ref.at[pl.ds(g * sub, sub)], h2s[g], slab_s)
        logit_s[r, :] = logits[g]


def _route(logit, earlier, run, counted, rt_ref, ert_ref):
    tm = logit.shape[0]
    lane_i = lax.broadcasted_iota(jnp.int32, (tm, LANES), 1)
    lane = lane_i.astype(F32)
    lane_grp = ((lane_i - N_GROUPS) >> 3).astype(F32)
    neg = jnp.float32(-jnp.inf)
    far = jnp.float32(LANES)
    is_g = lane_i < N_GROUPS
    lg = jnp.where(is_g, logit, neg)
    mg = jnp.max(lg, axis=-1, keepdims=True)
    pg_top = 1.0 / jnp.sum(jnp.where(is_g, jnp.exp(logit - mg), 0.0), axis=-1, keepdims=True)
    g_idx = jnp.min(jnp.where(lg == mg, lane, far), axis=-1, keepdims=True)
    in_grp = (lane_i >= N_GROUPS) & (lane_i < N_GROUPS + N_EXP) & (lane_grp == g_idx)
    le = jnp.where(in_grp, logit, neg)
    v1 = jnp.max(le, axis=-1, keepdims=True)
    i1 = jnp.min(jnp.where(le == v1, lane, far), axis=-1, keepdims=True)
    le2 = jnp.where(lane == i1, neg, le)
    v2 = jnp.max(le2, axis=-1, keepdims=True)
    i2 = jnp.min(jnp.where(le2 == v2, lane, far), axis=-1, keepdims=True)
    e1 = i1 - N_GROUPS
    e2 = i2 - N_GROUPS
    tt = jnp.exp(v2 - v1)
    w1 = pg_top / (1.0 + tt)
    w2 = pg_top * tt / (1.0 + tt)

    hot = ((lane == e1) | (lane == e2)).astype(F32)
    before = _dot(earlier, hot.astype(BF16)) + run
    r1 = jnp.sum(jnp.where(lane == e1, before, 0.0), axis=-1, keepdims=True)
    r2 = jnp.sum(jnp.where(lane == e2, before, 0.0), axis=-1, keepdims=True)

    rt = jnp.where(lane_i == 0, e1, 0.0)
    rt = jnp.where(lane_i == 1, e2, rt)
    rt = jnp.where(lane_i == 2, w1, rt)
    rt = jnp.where(lane_i == 3, w2, rt)
    rt = jnp.where(lane_i == 4, r1, rt)
    rt = jnp.where(lane_i == 5, r2, rt)
    rt_ref[...] = rt
    ert_ref[...] = rt.T[0:8, :]
    return run + counted * jnp.sum(hot, axis=0, keepdims=True)


def _post(x_ctx, x_lat, a_ctx, a_lat, g_ctx, g_lat, mod_all, seq_lat,
          w_mg, w_ba, w_bg, w_o, l1g, l1b, w_r, b_r):
    t_ctx, t_lat = x_ctx.shape[0], x_lat.shape[0]
    tm = TM_TOK
    nct, nlt = t_ctx // tm, t_lat // tm
    per_seq = seq_lat // tm
    nb_lat = t_lat // seq_lat
    t = t_ctx + t_lat
    n = nct + nlt
    lat = lambda i: jnp.clip(i - nct, 0, nlt - 1)
    cmap = lambda i: (jnp.minimum(i, nct - 1), 0)
    lmap = lambda i: (lat(i), 0)
    mmap = lambda i: (jnp.where(i < nct, nb_lat, lat(i) // per_seq), 0, 0)
    row = lambda i: (jnp.minimum(i, n - 1), 0)
    prev = lambda i: jnp.maximum(i - 1, 0)
    const = lambda i: (0, 0)
    return pl.pallas_call(
        functools.partial(_post_kernel, n_ctx_tiles=nct),
        grid=(n + 1,),
        in_specs=[pl.BlockSpec((tm, D), cmap), pl.BlockSpec((tm, D), lmap),
                  pl.BlockSpec((tm, AW), cmap), pl.BlockSpec((tm, AW), lmap),
                  pl.BlockSpec((tm, GVW), cmap), pl.BlockSpec((tm, GVW), lmap),
                  pl.BlockSpec((1, 1, 6 * D), mmap),
                  pl.BlockSpec((D, MG_WIDTH), const), pl.BlockSpec((AW, D), const),
                  pl.BlockSpec((GVW, D), const), pl.BlockSpec((D, D), const),
                  pl.BlockSpec((1, D), const), pl.BlockSpec((1, D), const),
                  pl.BlockSpec((D, LANES), const), pl.BlockSpec((1, LANES), const)],
        out_specs=[pl.BlockSpec((tm, D), row),
                   pl.BlockSpec((tm, ROW_SUB, LANES), lambda i: (jnp.minimum(i, n - 1), 0, 0)),
                   pl.BlockSpec((tm, LANES), lambda i: (prev(i), 0)),
                   pl.BlockSpec((8, tm), lambda i: (0, prev(i))),
                   pl.BlockSpec((8, LANES), const)],
        out_shape=[jax.ShapeDtypeStruct((t, D), F32), jax.ShapeDtypeStruct((t, ROW_SUB, LANES), ROW_DTYPE),
                   jax.ShapeDtypeStruct((t, LANES), F32), jax.ShapeDtypeStruct((8, t), F32),
                   jax.ShapeDtypeStruct((8, LANES), F32)],
        scratch_shapes=[pltpu.VMEM((8, LANES), F32), pltpu.VMEM((ROW_GROUP * ROW_SUB, LANES), F32),
                        pltpu.VMEM((tm, LANES), F32)],
        compiler_params=_cparams(("arbitrary",)),
        name="post",
    )(x_ctx, x_lat, a_ctx, a_lat, g_ctx, g_lat, mod_all, w_mg, w_ba, w_bg, w_o, l1g, l1b, w_r, b_r)


ROW_UNROLL = 8


def _row_copy(src_ref, dst_ref, sem):
    return pltpu.make_async_copy(src_ref, dst_ref, sem)


def _scatter_kernel(pos0_ref, pos1_ref, h_ref, xs_ref, sem, zero_s, zero_sem):
    ts = h_ref.shape[0]
    n_rows = xs_ref.shape[0] - TM_EXP

    @pl.when(pl.program_id(0) == 0)
    def _():
        zero_s[...] = jnp.zeros_like(zero_s)
        pad = _row_copy(zero_s, xs_ref.at[pl.ds(n_rows, TM_EXP)], zero_sem)
        pad.start()
        pad.wait()

    def issue(g, carry):
        r0 = pl.multiple_of(g * ROW_UNROLL, ROW_UNROLL)
        for k in range(ROW_UNROLL):
            src = _row_slab(h_ref, r0 + k)
            _row_copy(src, _row_slab(xs_ref, pos0_ref[0, r0 + k]), sem).start(priority=0)
            _row_copy(src, _row_slab(xs_ref, pos1_ref[0, r0 + k]), sem).start(priority=1)
        return carry

    lax.fori_loop(0, ts // ROW_UNROLL, issue, 0)
    for _ in range(2):
        _row_copy(h_ref, xs_ref.at[pl.ds(0, ts)], sem).wait()


def _scatter_rows(h2p, pos0, pos1):
    t = h2p.shape[0]
    ts = TS_ROWS
    smem = lambda: pl.BlockSpec((None, 1, ts), lambda i: (i, 0, 0), memory_space=pltpu.SMEM)
    return pl.pallas_call(
        _scatter_kernel,
        grid=(t // ts,),
        in_specs=[smem(), smem(), pl.BlockSpec((ts, ROW_SUB, LANES), lambda i: (i, 0, 0))],
        out_specs=pl.BlockSpec(memory_space=pl.ANY),
        out_shape=jax.ShapeDtypeStruct((2 * t + TM_EXP, ROW_SUB, LANES), ROW_DTYPE),
        scratch_shapes=[pltpu.SemaphoreType.DMA(()), pltpu.VMEM((TM_EXP, ROW_SUB, LANES), ROW_DTYPE),
                        pltpu.SemaphoreType.DMA(())],
        compiler_params=_cparams(("arbitrary",)),
        name="scatter",
    )(pos0.reshape(t // ts, 1, ts), pos1.reshape(t // ts, 1, ts), h2p)


def _expert_kernel(start_ref, nwin_ref, xs_ref, wg_ref, wu_ref, wd_ref, ys_ref,
                   wgu_s, wd_s, in_buf, out_buf, slab_s, in_sem, out_sem):
    e = pl.program_id(0)
    tm = TM_EXP
    wgu_s[:, 0:D_EXP] = wg_ref[...].astype(BF16)
    wgu_s[:, D_EXP:2 * D_EXP] = wu_ref[...].astype(BF16)
    wd_s[...] = wd_ref[...].astype(BF16)
    base = start_ref[e]
    n = nwin_ref[e]

    @pl.when(e == 0)
    def _():
        out_buf[0] = jnp.zeros(out_buf.shape[1:], ROW_DTYPE)
        pad = pltpu.make_async_copy(out_buf.at[0], ys_ref.at[pl.ds(ys_ref.shape[0] - tm, tm)], out_sem.at[0])
        pad.start()
        pad.wait()

    def read(w, slot, first_row=None):
        first_row = base if first_row is None else first_row
        return pltpu.make_async_copy(xs_ref.at[pl.ds(first_row + w * tm, tm)], in_buf.at[slot], in_sem.at[slot])

    def write(w, slot):
        return pltpu.make_async_copy(out_buf.at[slot], ys_ref.at[pl.ds(base + w * tm, tm)], out_sem.at[slot])

    def start_first_reads(expert):
        for a in range(READ_AHEAD):
            @pl.when(nwin_ref[expert] > a)
            def _(a=a):
                read(a, a, start_ref[expert]).start()

    @pl.when(e == 0)
    def _():
        start_first_reads(0)

    def body(w, carry):
        slot = w % 2
        rslot = w % (READ_AHEAD + 1)

        @pl.when(w + READ_AHEAD < n)
        def _():
            read(w + READ_AHEAD, (w + READ_AHEAD) % (READ_AHEAD + 1)).start()

        read(w, rslot).wait()

        @pl.when(w >= 2)
        def _():
            write(w - 2, slot).wait()

        gu = _dot(_load_row_slabs(in_buf.at[rslot], slab_s).astype(BF16), wgu_s[...])
        hid = _silu(gu[:, 0:D_EXP]) * gu[:, D_EXP:2 * D_EXP]
        _store_row_slabs(out_buf.at[slot], _dot(hid.astype(BF16), wd_s[...]), slab_s)
        write(w, slot).start()
        return carry

    lax.fori_loop(0, n, body, 0)

    @pl.when(e + 1 < pl.num_programs(0))
    def _():
        start_first_reads(e + 1)

    @pl.when(n >= 1)
    def _():
        write(n - 1, (n - 1) % 2).wait()

    @pl.when(n >= 2)
    def _():
        write(n - 2, n % 2).wait()


def _experts(xs, starts, n_win, w_gate, w_up, w_down):
    tm = TM_EXP
    wmap = lambda e, st, nw: (e, 0, 0)
    slab = (tm, ROW_SUB, LANES)
    return pl.pallas_call(
        _expert_kernel,
        grid_spec=pltpu.PrefetchScalarGridSpec(
            num_scalar_prefetch=2,
            grid=(N_EXP,),
            in_specs=[pl.BlockSpec(memory_space=pl.ANY),
                      pl.BlockSpec((None, D, D_EXP), wmap), pl.BlockSpec((None, D, D_EXP), wmap),
                      pl.BlockSpec((None, D_EXP, D), wmap)],
            out_specs=pl.BlockSpec(memory_space=pl.ANY),
            scratch_shapes=[pltpu.VMEM((D, 2 * D_EXP), BF16), pltpu.VMEM((D_EXP, D), BF16),
                            pltpu.VMEM((READ_AHEAD + 1,) + slab, ROW_DTYPE), pltpu.VMEM((2,) + slab, ROW_DTYPE),
                            pltpu.VMEM((tm * ROW_SUB, LANES), F32),
                            pltpu.SemaphoreType.DMA((READ_AHEAD + 1,)), pltpu.SemaphoreType.DMA((2,))]),
        out_shape=jax.ShapeDtypeStruct(xs.shape, ROW_DTYPE),
        compiler_params=_cparams(("arbitrary",)),
        name="experts",
    )(starts, n_win, xs, w_gate, w_up, w_down)


def _final_kernel(p0c_ref, p1c_ref, p0n_ref, p1n_ref, x1_ref, rt_ref, mod_ref, l2g_ref, l2b_ref, ys_ref,
                  oc_ref, ol_ref, buf, sem, slab_s, *, n_ctx_tiles):
    i = pl.program_id(0)
    n = pl.num_programs(0)
    tm = x1_ref.shape[0]

    def gather(p0_ref, p1_ref, slot):
        def issue(g, carry):
            r0 = pl.multiple_of(g * ROW_UNROLL, ROW_UNROLL)
            for k in range(ROW_UNROLL):
                _row_copy(_row_slab(ys_ref, p0_ref[0, r0 + k]),
                          _row_slab(buf.at[slot, 0], r0 + k), sem.at[slot]).start(priority=0)
                _row_copy(_row_slab(ys_ref, p1_ref[0, r0 + k]),
                          _row_slab(buf.at[slot, 1], r0 + k), sem.at[slot]).start(priority=1)
            return carry

        lax.fori_loop(0, tm // ROW_UNROLL, issue, 0)

    cur = i % 2

    @pl.when(i == 0)
    def _():
        gather(p0c_ref, p1c_ref, 0)

    @pl.when(i + 1 < n)
    def _():
        gather(p0n_ref, p1n_ref, 1 - cur)

    for k in range(2):
        _row_copy(ys_ref.at[pl.ds(0, tm)], buf.at[cur, k], sem.at[cur]).wait()

    g2 = mod_ref[0, :, 5 * D:6 * D]
    w1 = rt_ref[:, 2:3]
    w2 = rt_ref[:, 3:4]
    moe = (w1 * _load_row_slabs(buf.at[cur, 0], slab_s.at[0])
           + w2 * _load_row_slabs(buf.at[cur, 1], slab_s.at[1]))
    out = _ln(ALPHA * x1_ref[...] + g2 * moe) * l2g_ref[...] + l2b_ref[...]

    @pl.when(i < n_ctx_tiles)
    def _():
        oc_ref[...] = out

    @pl.when(i >= n_ctx_tiles)
    def _():
        ol_ref[...] = out


def _final(x1, rt, pos0, pos1, mod_all, l2g, l2b, ys, t_ctx, seq_lat):
    t = x1.shape[0]
    tm = TM_TOK
    nt = t // tm
    nct = t_ctx // tm
    t_lat = t - t_ctx
    per_seq = seq_lat // tm
    nb_lat = t_lat // seq_lat
    p0 = pos0.reshape(nt, 1, tm)
    p1 = pos1.reshape(nt, 1, tm)
    row = lambda i: (i, 0)
    const = lambda i: (0, 0)
    mmap = lambda i: (jnp.where(i < nct, nb_lat, jnp.maximum(i - nct, 0) // per_seq), 0, 0)
    smem_cur = lambda: pl.BlockSpec((None, 1, tm), lambda i: (i, 0, 0), memory_space=pltpu.SMEM)
    smem_nxt = lambda: pl.BlockSpec((None, 1, tm), lambda i: (jnp.minimum(i + 1, nt - 1), 0, 0),
                                    memory_space=pltpu.SMEM)
    return pl.pallas_call(
        functools.partial(_final_kernel, n_ctx_tiles=nct),
        grid=(nt,),
        in_specs=[smem_cur(), smem_cur(), smem_nxt(), smem_nxt(),
                  pl.BlockSpec((tm, D), row), pl.BlockSpec((tm, LANES), row),
                  pl.BlockSpec((1, 1, 6 * D), mmap),
                  pl.BlockSpec((1, D), const), pl.BlockSpec((1, D), const),
                  pl.BlockSpec(memory_space=pl.ANY)],
        out_specs=[pl.BlockSpec((tm, D), lambda i: (jnp.minimum(i, nct - 1), 0)),
                   pl.BlockSpec((tm, D), lambda i: (jnp.maximum(i - nct, 0), 0))],
        out_shape=[jax.ShapeDtypeStruct((t_ctx, D), F32), jax.ShapeDtypeStruct((t_lat, D), F32)],
        scratch_shapes=[pltpu.VMEM((2, 2, tm, ROW_SUB, LANES), ROW_DTYPE), pltpu.SemaphoreType.DMA((2,)),
                        pltpu.VMEM((2, tm * ROW_SUB, LANES), F32)],
        compiler_params=_cparams(("arbitrary",)),
        name="final",
    )(p0, p1, p0, p1, x1, rt, mod_all, l2g, l2b, ys)


def _reorder_q_heads(w, axis):
    shape = w.shape
    split = shape[:axis] + (N_KV_HEADS, N_Q_HEADS // N_KV_HEADS, HD) + shape[axis + 1:]
    return jnp.swapaxes(w.reshape(split), axis, axis + 1).reshape(shape)


def _rope_tables(seq):
    t = np.arange(seq)
    half = HD // 4
    inv = (ROPE_THETA ** (-np.arange(half, dtype=np.float64) / half)).astype(np.float32)
    d64 = np.arange(LANES) % HD
    pos = np.where((d64 < HD // 2)[None, :], (t // GRID_W)[:, None], (t % GRID_W)[:, None])
    ang = (pos.astype(np.float32) * inv[d64 % half][None, :]).astype(np.float64)
    sign = np.where((d64 % 32) < 16, -1.0, 1.0)
    return (jnp.asarray(np.cos(ang), F32), jnp.asarray(np.sin(ang) * sign[None, :], F32))


def _pair_states(s):
    b = s.shape[0]
    s = s.reshape(b, 2, 2, GLA_DK, GLA_DV)
    return s.transpose(0, 1, 4, 2, 3).reshape(b, 2, GLA_DV, 2 * GLA_DK)


def _unpair_states(s):
    b = s.shape[0]
    s = s.reshape(b, 2, GLA_DV, 2, GLA_DK)
    return s.transpose(0, 1, 3, 4, 2).reshape(b, GLA_H, GLA_DK, GLA_DV)


def _route_tables(ert, counts):
    i32 = jnp.int32
    cnt = counts[0, :N_EXP].astype(i32)
    starts = jnp.cumsum(cnt) - cnt
    table = lambda e: jnp.sum(jnp.where(e[None, :] == jnp.arange(N_EXP, dtype=i32)[:, None],
                                        starts[:, None], 0), axis=0)
    pos0 = table(ert[0].astype(i32)) + ert[4].astype(i32)
    pos1 = table(ert[1].astype(i32)) + ert[5].astype(i32)
    return pos0, pos1, starts, (cnt + (TM_EXP - 1)) // TM_EXP


def kernel(x_prompt, x_sample, cache_k, cache_v, state_gla_fwd, state_gla_bwd, c, c_ctx, w_ada, b_ada, w_in, q_norm, k_norm, gla_w_gate, gla_b_gate, gla_norm, w_br_attn, w_br_gla, w_out, ln1_g, ln1_b, router_group_w, router_group_b, router_expert_w, router_expert_b, exp_w_gate, exp_w_up, exp_w_down, ln2_g, ln2_b):
    b_ctx, seq_ctx, _ = x_prompt.shape
    b_lat, seq_lat, _ = x_sample.shape
    t_ctx, t_lat = b_ctx * seq_ctx, b_lat * seq_lat
    t = t_ctx + t_lat
    l = 0

    rows = -(-(b_lat + 1) // 8) * 8
    c_rows = jnp.zeros((rows, D), F32).at[:b_lat].set(c).at[b_lat].set(c_ctx)
    mod = _ada(c_rows, w_ada[l], b_ada[l][None, :])
    mod_all = mod[:b_lat + 1, None, :]
    mod_lat = mod_all[:b_lat]
    mod_ctx = mod_all[b_lat:]

    w_full = w_in[l]
    w_a = jnp.concatenate([_reorder_q_heads(w_full[:, :AW], 1), w_full[:, AW:A_WIDTH]], axis=1).astype(BF16)
    w_mg = w_full[:, A_WIDTH:].astype(BF16)
    gain = jnp.concatenate([jnp.tile(q_norm[l], N_Q_HEADS), jnp.tile(k_norm[l], N_KV_HEADS)])[None, :]
    head_of = np.arange(AW + KVW) // HD
    ind = jnp.asarray((head_of[:, None] == np.arange(LANES)[None, :]) / HD, BF16)
    w_ba = _reorder_q_heads(w_br_attn[l], 0).astype(BF16)
    w_bg = w_br_gla[l].astype(BF16)
    w_o = w_out[l].astype(BF16)
    w_r = jnp.zeros((D, LANES), F32).at[:, :N_GROUPS].set(router_group_w[l])
    w_r = w_r.at[:, N_GROUPS:N_GROUPS + N_EXP].set(router_expert_w[l]).astype(BF16)
    b_r = jnp.zeros((1, LANES), F32).at[0, :N_GROUPS].set(router_group_b[l])
    b_r = b_r.at[0, N_GROUPS:N_GROUPS + N_EXP].set(router_expert_b[l])
    wg = jnp.zeros((2, 2 * GATE_RANK, GKW), F32)
    wg = wg.at[0, :GATE_RANK].set(gla_w_gate[l, 0]).at[1, GATE_RANK:].set(gla_w_gate[l, 1])
    bg = gla_b_gate[l][:, None, :]
    gn = gla_norm[l][None, :]

    xc = x_prompt.reshape(t_ctx, D)
    xl = x_sample.reshape(t_lat, D)

    q_c, k_c, v_c, qg_c, kg_c, vg_c, rs_c, lr_c, kf_c, vf_c = _inproj(
        xc, mod_ctx, w_a, gain, ind, None, seq_ctx, latent=False)
    attn_c = _attention(q_c, k_c, v_c, None, seq_ctx)
    zero_state = jnp.zeros((1, 2, 2, LANES, LANES), F32)
    gla_c, sfin_c = _gla(qg_c, kg_c, vg_c, lr_c, rs_c, zero_state, wg, bg, gn, seq_ctx)

    q_l, k_l, v_l, qg_l, kg_l, vg_l, rs_l, lr_l = _inproj(
        xl, mod_lat, w_a, gain, ind, _rope_tables(seq_lat), seq_lat, latent=True)
    past = cache_k.shape[2]
    kc = cache_k[:, l].reshape(b_lat, past, KVW).astype(BF16)
    vc = cache_v[:, l].reshape(b_lat, past, KVW).astype(BF16)
    attn_l = _attention(q_l, k_l, v_l, (kc, vc), seq_lat)
    s0 = jnp.stack([_pair_states(state_gla_fwd[:, l]), _pair_states(state_gla_bwd[:, l])], axis=1)
    gla_l, _ = _gla(qg_l, kg_l, vg_l, lr_l, rs_l, s0, wg, bg, gn, seq_lat)

    x1, h2p, rt, ert, counts = _post(xc, xl, attn_c, attn_l, gla_c, gla_l, mod_all, seq_lat,
                                w_mg, w_ba, w_bg, w_o, ln1_g[l][None, :], ln1_b[l][None, :], w_r, b_r)

    pos0, pos1, starts, n_win = _route_tables(ert, counts)
    xs = _scatter_rows(h2p, pos0, pos1)
    ys = _experts(xs, starts, n_win, exp_w_gate[l], exp_w_up[l], exp_w_down[l])
    y_ctx, y_lat = _final(x1, rt, pos0, pos1, mod_all, ln2_g[l][None, :], ln2_b[l][None, :], ys,
                          t_ctx, seq_lat)

    untranspose = lambda a: a.reshape(b_ctx, 1, N_KV_HEADS, HD, seq_ctx).transpose(0, 1, 4, 2, 3)
    new_k, new_v = untranspose(kf_c), untranspose(vf_c)
    new_sf = _unpair_states(sfin_c[:, 0])[:, None]
    new_sb = _unpair_states(sfin_c[:, 1])[:, None]
    return (y_ctx.reshape(b_ctx, seq_ctx, D), y_lat.reshape(b_lat, seq_lat, D),
            new_k, new_v, new_sf, new_sb)
```

```python
import functools

import numpy as np
import jax
import jax.numpy as jnp
from jax import lax
from jax.experimental import pallas as pl
from jax.experimental.pallas import tpu as pltpu

F32 = jnp.float32
BF16 = jnp.bfloat16

D = 1024
GRID_W = 64
HD = 64
N_Q_HEADS = 8
N_KV_HEADS = 2
AW = N_Q_HEADS * HD
KVW = N_KV_HEADS * HD
ROPE_THETA = 10000.0
GLA_H = 4
GLA_DK = 64
GLA_DV = 128
GKW = GLA_H * GLA_DK
GVW = GLA_H * GLA_DV
GATE_RANK = 16
GLA_TAU = 16.0
CHUNK = 64
N_GROUPS = 4
EPG = 8
N_EXP = N_GROUPS * EPG
D_EXP = 256
DEPTH = 1
ALPHA = (2.0 * DEPTH) ** 0.25
LN_EPS = 1e-6
RMS_EPS = 1e-6

LANES = 128
A_WIDTH = AW + 2 * KVW + 2 * GKW + 2 * GVW + 2 * GATE_RANK
MG_WIDTH = 2 * D
TM_TOK = 512
ROW_GROUP = 128
LOG2E = 1.4426950408889634
ONES_ROWS = 16
TQ_LAT = 256
TM_EXP = 256
READ_AHEAD = 3
TS_ROWS = 2048
VMEM_LIMIT = 56 * 1024 * 1024


def _cparams(sem):
    return pltpu.CompilerParams(dimension_semantics=sem, vmem_limit_bytes=VMEM_LIMIT)


def _dot(a, b):
    return jnp.dot(a, b, preferred_element_type=F32)


def _dot_nt(a, b):
    return lax.dot_general(a, b, (((1,), (1,)), ((), ())), preferred_element_type=F32)


def _dot_tn(a, b):
    return lax.dot_general(a, b, (((0,), (0,)), ((), ())), preferred_element_type=F32)


def _ln(x):
    mu = jnp.mean(x, axis=-1, keepdims=True)
    xc = x - mu
    var = jnp.mean(xc * xc, axis=-1, keepdims=True)
    return xc * lax.rsqrt(var + LN_EPS)


def _silu(x):
    return x * jax.nn.sigmoid(x)


def _split_bf16(x):
    hi = x.astype(BF16)
    lo = (x - hi.astype(F32)).astype(BF16)
    return hi, lo


def _ada_kernel(c_ref, w_ref, b_ref, o_ref):
    rows = c_ref.shape[0]
    s_hi, s_mid, s_lo = _split3_bf16(_silu(c_ref[...]))
    w_hi, w_lo = _split_bf16(w_ref[...])
    a = _dot(jnp.concatenate([s_hi, s_mid, s_lo], axis=0), w_hi)
    b = _dot(jnp.concatenate([s_hi, s_mid], axis=0), w_lo)
    o_ref[...] = (a[0:rows] + a[rows:2 * rows] + a[2 * rows:3 * rows]
                  + b[0:rows] + b[rows:2 * rows] + b_ref[...])


def _ada(c_rows, w_ada, b_ada):
    rows = c_rows.shape[0]
    n = w_ada.shape[1]
    bn = 1024
    return pl.pallas_call(
        _ada_kernel,
        grid=(n // bn,),
        in_specs=[pl.BlockSpec((rows, D), lambda j: (0, 0)),
                  pl.BlockSpec((D, bn), lambda j: (0, j)),
                  pl.BlockSpec((1, bn), lambda j: (0, j))],
        out_specs=pl.BlockSpec((rows, bn), lambda j: (0, j)),
        out_shape=jax.ShapeDtypeStruct((rows, n), F32),
        compiler_params=_cparams(("arbitrary",)),
        name="ada",
    )(c_rows, w_ada, b_ada)


def _inproj_kernel(*refs, latent, seq):
    if latent:
        (x_ref, mod_ref, w_ref, gain_ref, ind_ref, cos_ref, sin_ref,
         q_ref, k_ref, v_ref, qg_ref, kg_ref, vg_ref, rs_ref, lr_ref) = refs
    else:
        (x_ref, mod_ref, w_ref, gain_ref, ind_ref,
         q_ref, k_ref, v_ref, qg_ref, kg_ref, vg_ref, rs_ref, lr_ref, kf_ref, vf_ref) = refs
    tm = x_ref.shape[0]
    sub = ROW_GROUP
    n_groups = tm // sub
    sh1 = mod_ref[0, :, 0:D]
    sc1 = mod_ref[0, :, D:2 * D]
    lane = lax.broadcasted_iota(jnp.int32, (sub, LANES), 1)
    low_half = lane < HD
    first = (lane % 32) < 16

    def project(g):
        rows = slice(g * sub, (g + 1) * sub)
        h = (_ln(x_ref[rows, :]) * (1.0 + sc1) + sh1).astype(BF16)
        return _dot(h, w_ref[...])

    def finish(g, res):
        rows = slice(g * sub, (g + 1) * sub)
        qk = res[:, 0:AW + KVW]
        hi, lo = _split_bf16(qk * qk)
        ms = _dot(hi, ind_ref[...]) + _dot(lo, ind_ref[...])
        r = lax.rsqrt(ms + RMS_EPS)
        for s in range(5):
            rb = jnp.where(low_half, r[:, 2 * s:2 * s + 1], r[:, 2 * s + 1:2 * s + 2])
            y = res[:, LANES * s:LANES * (s + 1)] * rb * gain_ref[:, LANES * s:LANES * (s + 1)]
            if s == 4 and not latent:
                kf_ref[(g * sub) // seq, :, (g * sub) % seq:(g * sub) % seq + sub] = y.T
            if latent:
                partner = jnp.where(first, pltpu.roll(y, LANES - 16, 1), pltpu.roll(y, 16, 1))
                y = y * cos_ref[rows, :] + partner * sin_ref[rows, :]
            if s < 4:
                q_ref[rows, LANES * s:LANES * (s + 1)] = (y * (HD ** -0.5 * LOG2E)).astype(BF16)
            else:
                k_ref[rows, :] = y.astype(BF16)
        o = AW + KVW
        v = res[:, o:o + KVW]
        v_ref[rows, :] = v.astype(BF16)
        if not latent:
            vf_ref[(g * sub) // seq, :, (g * sub) % seq:(g * sub) % seq + sub] = v.T
        o += KVW
        qg_ref[rows, :] = res[:, o:o + GKW] * (GLA_DK ** -0.5)
        o += GKW
        kg_ref[rows, :] = res[:, o:o + GKW]
        o += GKW
        vg_ref[rows, :] = res[:, o:o + GVW].astype(BF16)
        o += GVW
        rs_ref[rows, :] = _silu(res[:, o:o + GVW]).astype(BF16)
        o += GVW
        lr_ref[rows, :] = res[:, o:o + 2 * GATE_RANK]

    pending = {0: project(0)}
    for g in range(n_groups):
        if g + 1 < n_groups:
            pending[g + 1] = project(g + 1)
        finish(g, pending.pop(g))


def _inproj(x2, mod, w_a, gain, ind, rope, seq, latent):
    t = x2.shape[0]
    tm = TM_TOK
    per_seq = max(seq // tm, 1)
    per_tile = max(tm // seq, 1)
    row = lambda i: (i, 0)
    const = lambda i: (0, 0)
    in_specs = [pl.BlockSpec((tm, D), row),
                pl.BlockSpec((1, 1, 6 * D), (lambda i: (i // per_seq, 0, 0)) if latent else (lambda i: (0, 0, 0))),
                pl.BlockSpec((D, A_WIDTH), const),
                pl.BlockSpec((1, AW + KVW), const),
                pl.BlockSpec((AW + KVW, LANES), const)]
    args = [x2, mod, w_a, gain, ind]
    if latent:
        in_specs += [pl.BlockSpec((tm, LANES), lambda i: (i % per_seq, 0))] * 2
        args += list(rope)
    widths = [(AW, BF16), (KVW, BF16), (KVW, BF16), (GKW, F32), (GKW, F32), (GVW, BF16), (GVW, BF16),
              (2 * GATE_RANK, F32)]
    out_specs = [pl.BlockSpec((tm, w), row) for w, _ in widths]
    out_shape = [jax.ShapeDtypeStruct((t, w), dt) for w, dt in widths]
    if not latent:
        cache_spec = pl.BlockSpec((per_tile, KVW, seq), lambda i: (i, 0, 0))
        out_specs += [cache_spec] * 2
        out_shape += [jax.ShapeDtypeStruct((t // seq, KVW, seq), F32)] * 2
    return pl.pallas_call(
        functools.partial(_inproj_kernel, latent=latent, seq=seq),
        grid=(t // tm,),
        in_specs=in_specs,
        out_specs=out_specs,
        out_shape=out_shape,
        compiler_params=_cparams(("parallel",)),
        name="inproj_lat" if latent else "inproj_ctx",
    )(*args)


def _attn_kernel(*refs, has_cache):
    def transposed_with_ones(dst, src):
        dst[0:KVW, :] = src[...].astype(F32).T.astype(BF16)
        dst[KVW:, :] = jnp.ones((ONES_ROWS, dst.shape[1]), BF16)

    if has_cache:
        q_ref, k_ref, v_ref, kc_ref, vc_ref, o_ref, vt_s, vct_s = refs

        @pl.when(pl.program_id(1) == 0)
        def _():
            transposed_with_ones(vt_s, v_ref)
            transposed_with_ones(vct_s, vc_ref)
    else:
        q_ref, k_ref, v_ref, o_ref, vt_s = refs
        transposed_with_ones(vt_s, v_ref)
    tq = q_ref.shape[0]

    lane = lax.broadcasted_iota(jnp.int32, (tq, LANES), 1)
    low_half = lane < HD
    k = k_ref[...]
    scores = []
    for j in range(N_KV_HEADS):
        keep = low_half if j == 0 else jnp.logical_not(low_half)
        zero = jnp.zeros((tq, LANES), BF16)
        for pair in range(2):
            qs = jnp.concatenate([jnp.where(keep, q_ref[:, LANES * s:LANES * (s + 1)], zero)
                                  for s in (2 * pair, 2 * pair + 1)], axis=0)
            s1 = _dot_nt(k, qs)
            s2 = _dot_nt(kc_ref[...], qs) if has_cache else None
            scores.append((s1, s2))
    outs = []
    for s1, s2 in scores:
        m = jnp.max(s1, axis=0, keepdims=True)
        if has_cache:
            m = jnp.maximum(m, jnp.max(s2, axis=0, keepdims=True))
        acc = _dot(vt_s[...], jnp.exp2(s1 - m).astype(BF16))
        if has_cache:
            acc = acc + _dot(vct_s[...], jnp.exp2(s2 - m).astype(BF16))
        outs.append(acc[0:KVW] / acc[KVW:KVW + 1])
    head0 = jnp.concatenate(outs[0:2], axis=1)
    head1 = jnp.concatenate(outs[2:4], axis=1)
    row = lax.broadcasted_iota(jnp.int32, (LANES, 4 * tq), 0)
    out = jnp.where(row < HD, head0, head1).T
    for s in range(4):
        o_ref[:, LANES * s:LANES * (s + 1)] = out[s * tq:(s + 1) * tq].astype(BF16)


def _attention(q, k, v, cache, seq):
    t = q.shape[0]
    if cache is None:
        tq = seq
        grid = (t // seq,)
        qmap = lambda b: (b, 0)
        in_specs = [pl.BlockSpec((tq, AW), qmap), pl.BlockSpec((seq, KVW), qmap),
                    pl.BlockSpec((seq, KVW), qmap)]
        args = [q, k, v]
        scratch = [pltpu.VMEM((KVW + ONES_ROWS, seq), BF16)]
        sem = ("parallel",)
        name = "attn_ctx"
    else:
        tq = TQ_LAT
        nq = seq // tq
        kc, vc = cache
        past = kc.shape[1]
        grid = (t // seq, nq)
        qmap = lambda b, i: (b * nq + i, 0)
        kmap = lambda b, i: (b, 0)
        cmap = lambda b, i: (b, 0, 0)
        in_specs = [pl.BlockSpec((tq, AW), qmap), pl.BlockSpec((seq, KVW), kmap),
                    pl.BlockSpec((seq, KVW), kmap),
                    pl.BlockSpec((None, past, KVW), cmap), pl.BlockSpec((None, past, KVW), cmap)]
        args = [q, k, v, kc, vc]
        scratch = [pltpu.VMEM((KVW + ONES_ROWS, seq), BF16), pltpu.VMEM((KVW + ONES_ROWS, past), BF16)]
        sem = ("parallel", "arbitrary")
        name = "attn_lat"
    return pl.pallas_call(
        functools.partial(_attn_kernel, has_cache=cache is not None),
        grid=grid,
        in_specs=in_specs,
        out_specs=pl.BlockSpec((tq, AW), qmap),
        out_shape=jax.ShapeDtypeStruct((t, AW), BF16),
        scratch_shapes=scratch,
        compiler_params=_cparams(sem),
        name=name,
    )(*args)


GLA_BLK = 256
GLA_ROWS = 1024
GLA_UNROLL = 8


def _split3_bf16(x):
    hi = x.astype(BF16)
    r1 = x - hi.astype(F32)
    mid = r1.astype(BF16)
    lo = (r1 - mid.astype(F32)).astype(BF16)
    return hi, mid, lo


def _gla_kernel(qg_ref, kg_ref, vg_ref, lr_ref, rs_ref, s0_ref, wg_ref, bg_ref, gn_ref,
                o_ref, sfin_ref, cum_s, kv_s, dec_s, *, seqs):
    n = qg_ref.shape[0]
    nc = n // CHUNK
    nc_seq = nc // seqs
    unroll = min(GLA_UNROLL, nc)
    lane = lax.broadcasted_iota(jnp.int32, (CHUNK, LANES), 1)
    low_half = lane < GLA_DK
    lane_sq = lax.broadcasted_iota(jnp.int32, (LANES, LANES), 1)
    low_half_sq = lane_sq < GLA_DK
    ri = lax.broadcasted_iota(jnp.int32, (2 * CHUNK, 2 * CHUNK), 0)
    ci = lax.broadcasted_iota(jnp.int32, (2 * CHUNK, 2 * CHUNK), 1)
    diag = (ri >> 6) == (ci >> 6)
    keep2 = (diag & (ri >= ci), diag & (ci >= ri))

    rb = lax.broadcasted_iota(jnp.int32, (GLA_BLK, GLA_BLK), 0)
    cb = lax.broadcasted_iota(jnp.int32, (GLA_BLK, GLA_BLK), 1)
    same = (rb >> 6) == (cb >> 6)
    tri = ((same & (rb >= cb)).astype(BF16), (same & (cb >= rb)).astype(BF16))
    for d in range(2):
        w_hi, w_lo = _split_bf16(wg_ref[d])
        for blk in range(n // GLA_BLK):
            rows = slice(blk * GLA_BLK, (blk + 1) * GLA_BLK)
            l_hi, l_lo = _split_bf16(lr_ref[rows, :])
            z = _dot(l_hi, w_hi) + _dot(l_lo, w_hi) + _dot(l_hi, w_lo) + bg_ref[d]
            logg = (jnp.minimum(z, 0.0) - jnp.log(1.0 + jnp.exp(-jnp.abs(z)))) * (1.0 / GLA_TAU)
            pieces = _split3_bf16(logg)
            cum_s[d, rows, :] = _dot(tri[d], pieces[0]) + _dot(tri[d], pieces[1]) + _dot(tri[d], pieces[2])

    def pass1(i, carry):
        cs = [i * unroll + u for u in range(unroll)]
        rws = [pl.ds(pl.multiple_of(c * CHUNK, CHUNK), CHUNK) for c in cs]
        prods = []
        for c, rows in zip(cs, rws):
            kc = kg_ref[rows, :]
            kdec = []
            for d in range(2):
                cum = cum_s[d, rows, :]
                last = cum[CHUNK - 1:CHUNK, :] if d == 0 else cum[0:1, :]
                dec_s[d, c] = jnp.exp(last)
                kdec.append((kc * jnp.exp(last - cum)).astype(BF16))
            for p in range(2):
                vpair = vg_ref[rows, GLA_DV * 2 * p:GLA_DV * 2 * (p + 1)]
                kpair = jnp.concatenate([kdec[0][:, LANES * p:LANES * (p + 1)],
                                         kdec[1][:, LANES * p:LANES * (p + 1)]], axis=1)
                prods.append(_dot_tn(vpair, kpair))
        for j, c in enumerate(cs):
            for p in range(2):
                res = prods[2 * j + p]
                for d in range(2):
                    cols = slice(LANES * d, LANES * (d + 1))
                    kv_s[d, c, p] = jnp.where(low_half_sq, res[0:GLA_DV, cols], res[GLA_DV:2 * GLA_DV, cols])
        return carry

    lax.fori_loop(0, nc // unroll, pass1, 0)

    for s in range(seqs):
        for d in range(2):
            def scan(i, st, s=s, d=d):
                c = s * nc_seq + (i if d == 0 else nc_seq - 1 - i)
                dec = dec_s[d, c]
                new = []
                for p in range(2):
                    kv = kv_s[d, c, p]
                    kv_s[d, c, p] = st[p]
                    new.append(st[p] * dec[:, LANES * p:LANES * (p + 1)] + kv)
                return tuple(new)

            s_in = s if s0_ref.shape[0] > 1 else 0
            fin = lax.fori_loop(0, nc_seq, scan, (s0_ref[s_in, d, 0], s0_ref[s_in, d, 1]))
            sfin_ref[s, d, 0] = fin[0]
            sfin_ref[s, d, 1] = fin[1]

    def pass3(i, carry):
        cs = [i * unroll + u for u in range(unroll)]
        rws = [pl.ds(pl.multiple_of(c * CHUNK, CHUNK), CHUNK) for c in cs]
        first = []
        for c, rows in zip(cs, rws):
            q = qg_ref[rows, :]
            k = kg_ref[rows, :]
            for d in range(2):
                cum = cum_s[d, rows, :]
                qt = q * jnp.exp(cum)
                kt = (k * jnp.exp(-cum)).astype(BF16)
                for p in range(2):
                    qs = qt[:, LANES * p:LANES * (p + 1)]
                    lhs = jnp.concatenate([jnp.where(low_half, qs, 0.0), jnp.where(low_half, 0.0, qs)],
                                          axis=0).astype(BF16)
                    kts = kt[:, LANES * p:LANES * (p + 1)]
                    rhs = jnp.concatenate([kts, kts, kv_s[d, c, p].astype(BF16)], axis=0)
                    first.append(_dot_nt(lhs, rhs))
        second = []
        for j, rows in enumerate(rws):
            for d in range(2):
                for p in range(2):
                    res = first[4 * j + 2 * d + p]
                    vp = jnp.concatenate([vg_ref[rows, GLA_DV * (2 * p):GLA_DV * (2 * p + 1)],
                                          vg_ref[rows, GLA_DV * (2 * p + 1):GLA_DV * (2 * p + 2)]], axis=0)
                    a = jnp.where(keep2[d], res[:, 0:2 * CHUNK], 0.0).astype(BF16)
                    second.append(_dot(a, vp) + res[:, 2 * CHUNK:])
        for j, rows in enumerate(rws):
            for p in range(2):
                tot = second[4 * j + p] + second[4 * j + 2 + p]
                y = tot * lax.rsqrt(jnp.mean(tot * tot, axis=-1, keepdims=True) + RMS_EPS) * gn_ref[...]
                for hh in range(2):
                    cols = slice(GLA_DV * (2 * p + hh), GLA_DV * (2 * p + hh + 1))
                    o_ref[rows, cols] = (y[CHUNK * hh:CHUNK * (hh + 1)]
                                         * rs_ref[rows, cols].astype(F32)).astype(BF16)
        return carry

    lax.fori_loop(0, nc // unroll, pass3, 0)


def _gla(qg, kg, vg, lr, rs, s0, wg, bg, gn, seq):
    t = qg.shape[0]
    nb = t // seq
    seqs = max(GLA_ROWS // seq, 1)
    rows = seqs * seq
    nc = rows // CHUNK
    row = lambda b: (b, 0)
    c3 = lambda b: (0, 0, 0)
    st = lambda b: (b, 0, 0, 0, 0)
    if s0.shape[0] == 1:
        s0_spec = pl.BlockSpec((1, 2, 2, LANES, LANES), lambda b: (0, 0, 0, 0, 0))
    else:
        s0_spec = pl.BlockSpec((seqs, 2, 2, LANES, LANES), st)
    return pl.pallas_call(
        functools.partial(_gla_kernel, seqs=seqs),
        grid=(nb // seqs,),
        in_specs=[pl.BlockSpec((rows, GKW), row), pl.BlockSpec((rows, GKW), row),
                  pl.BlockSpec((rows, GVW), row), pl.BlockSpec((rows, 2 * GATE_RANK), row),
                  pl.BlockSpec((rows, GVW), row),
                  s0_spec,
                  pl.BlockSpec((2, 2 * GATE_RANK, GKW), c3), pl.BlockSpec((2, 1, GKW), c3),
                  pl.BlockSpec((1, GLA_DV), lambda b: (0, 0))],
        out_specs=[pl.BlockSpec((rows, GVW), row),
                   pl.BlockSpec((seqs, 2, 2, LANES, LANES), st)],
        out_shape=[jax.ShapeDtypeStruct((t, GVW), BF16),
                   jax.ShapeDtypeStruct((nb, 2, 2, LANES, LANES), F32)],
        scratch_shapes=[pltpu.VMEM((2, rows, GKW), F32),
                        pltpu.VMEM((2, nc, 2, LANES, LANES), F32),
                        pltpu.VMEM((2, nc, 1, GKW), F32)],
        compiler_params=_cparams(("parallel",)),
        name="gla_lat" if seq > 256 else "gla_ctx",
    )(qg, kg, vg, lr, rs, s0, wg, bg, gn)


ROW_SUB = D // LANES
ROW_DTYPE = BF16


def _store_row_slabs(ref, x, tmp):
    m = x.shape[0]
    for c in range(ROW_SUB):
        tmp[pl.ds(c, m, stride=ROW_SUB), :] = x[:, LANES * c:LANES * (c + 1)]
    ref[...] = tmp[...].reshape(m, ROW_SUB, LANES).astype(ROW_DTYPE)


def _load_row_slabs(ref, tmp):
    m = ref.shape[0]
    tmp[...] = ref[...].astype(F32).reshape(m * ROW_SUB, LANES)
    return jnp.concatenate([tmp[pl.ds(c, m, stride=ROW_SUB), :] for c in range(ROW_SUB)], axis=1)


def _row_slab(ref, row):
    return ref.at[pl.ds(row, 1)]


def _post_kernel(xc_ref, xl_ref, ac_ref, al_ref, gc_ref, gl_ref, mod_ref,
                 wmg_ref, wba_ref, wbg_ref, wo_ref, l1g_ref, l1b_ref, wr_ref, br_ref,
                 x1_ref, h2_ref, rt_ref, ert_ref, cnt_ref, run_s, slab_s, logit_s, *, n_ctx_tiles):
    i = pl.program_id(0)
    tm = xc_ref.shape[0]
    sub = ROW_GROUP
    n_groups = tm // sub
    is_ctx = i < n_ctx_tiles
    parts = [slice(g * sub, (g + 1) * sub) for g in range(n_groups)]

    @pl.when(i == 0)
    def _():
        run_s[...] = jnp.zeros_like(run_s)
        logit_s[...] = jnp.zeros_like(logit_s)

    sh1 = mod_ref[0, :, 0:D]
    sc1 = mod_ref[0, :, D:2 * D]
    g1 = mod_ref[0, :, 2 * D:3 * D]
    sh2 = mod_ref[0, :, 3 * D:4 * D]
    sc2 = mod_ref[0, :, 4 * D:5 * D]

    xs = [jnp.where(is_ctx, xc_ref[r, :], xl_ref[r, :]) for r in parts]
    hs = [(_ln(x) * (1.0 + sc1) + sh1).astype(BF16) for x in xs]
    gates = [jax.nn.sigmoid(_dot(h, wmg_ref[...])) for h in hs]
    ba = [_dot(jnp.where(is_ctx, ac_ref[r, :], al_ref[r, :]), wba_ref[...]) for r in parts]
    bg = [_dot(jnp.where(is_ctx, gc_ref[r, :], gl_ref[r, :]), wbg_ref[...]) for r in parts]

    ri = lax.broadcasted_iota(jnp.int32, (sub, sub), 0)
    ci = lax.broadcasted_iota(jnp.int32, (sub, sub), 1)
    earlier = (ri > ci).astype(BF16)
    counted = (i > 0).astype(F32)
    run = run_s[0:1, :]
    for r in parts:
        run = _route(logit_s[r, :], earlier, run, counted, rt_ref.at[r, :], ert_ref.at[:, r])
    run_s[0:1, :] = run
    cnt_ref[...] = jnp.broadcast_to(run, cnt_ref.shape)

    merged = [(g[:, :D] * a + g[:, D:] * b).astype(BF16) for g, a, b in zip(gates, ba, bg)]
    mix = [_dot(m, wo_ref[...]) for m in merged]
    x1s = [_ln(ALPHA * x + g1 * m) * l1g_ref[...] + l1b_ref[...] for x, m in zip(xs, mix)]
    h2s = [_ln(x1) * (1.0 + sc2) + sh2 for x1 in x1s]
    logits = [_dot(h2.astype(BF16), wr_ref[...]) + br_ref[...] for h2 in h2s]
    for g, r in enumerate(parts):
        x1_ref[r, :] = x1s[g]
        _store_row_slabs(h2_ref.at[pl.ds(g * sub, sub)], h2s[g], slab_s)
        logit_s[r, :] = logits[g]


def _route(logit, earlier, run, counted, rt_ref, ert_ref):
    tm = logit.shape[0]
    lane_i = lax.broadcasted_iota(jnp.int32, (tm, LANES), 1)
    lane = lane_i.astype(F32)
    lane_grp = ((lane_i - N_GROUPS) >> 3).astype(F32)
    neg = jnp.float32(-jnp.inf)
    far = jnp.float32(LANES)
    is_g = lane_i < N_GROUPS
    lg = jnp.where(is_g, logit, neg)
    mg = jnp.max(lg, axis=-1, keepdims=True)
    pg_top = 1.0 / jnp.sum(jnp.where(is_g, jnp.exp(logit - mg), 0.0), axis=-1, keepdims=True)
    g_idx = jnp.min(jnp.where(lg == mg, lane, far), axis=-1, keepdims=True)
    in_grp = (lane_i >= N_GROUPS) & (lane_i < N_GROUPS + N_EXP) & (lane_grp == g_idx)
    le = jnp.where(in_grp, logit, neg)
    v1 = jnp.max(le, axis=-1, keepdims=True)
    i1 = jnp.min(jnp.where(le == v1, lane, far), axis=-1, keepdims=True)
    le2 = jnp.where(lane == i1, neg, le)
    v2 = jnp.max(le2, axis=-1, keepdims=True)
    i2 = jnp.min(jnp.where(le2 == v2, lane, far), axis=-1, keepdims=True)
    e1 = i1 - N_GROUPS
    e2 = i2 - N_GROUPS
    tt = jnp.exp(v2 - v1)
    w1 = pg_top / (1.0 + tt)
    w2 = pg_top * tt / (1.0 + tt)

    hot = ((lane == e1) | (lane == e2)).astype(F32)
    before = _dot(earlier, hot.astype(BF16)) + run
    r1 = jnp.sum(jnp.where(lane == e1, before, 0.0), axis=-1, keepdims=True)
    r2 = jnp.sum(jnp.where(lane == e2, before, 0.0), axis=-1, keepdims=True)

    rt = jnp.where(lane_i == 0, e1, 0.0)
    rt = jnp.where(lane_i == 1, e2, rt)
    rt = jnp.where(lane_i == 2, w1, rt)
    rt = jnp.where(lane_i == 3, w2, rt)
    rt = jnp.where(lane_i == 4, r1, rt)
    rt = jnp.where(lane_i == 5, r2, rt)
    rt_ref[...] = rt
    ert_ref[...] = rt.T[0:8, :]
    return run + counted * jnp.sum(hot, axis=0, keepdims=True)


def _post(x_ctx, x_lat, a_ctx, a_lat, g_ctx, g_lat, mod_all, seq_lat,
          w_mg, w_ba, w_bg, w_o, l1g, l1b, w_r, b_r):
    t_ctx, t_lat = x_ctx.shape[0], x_lat.shape[0]
    tm = TM_TOK
    nct, nlt = t_ctx // tm, t_lat // tm
    per_seq = seq_lat // tm
    nb_lat = t_lat // seq_lat
    t = t_ctx + t_lat
    n = nct + nlt
    lat = lambda i: jnp.clip(i - nct, 0, nlt - 1)
    cmap = lambda i: (jnp.minimum(i, nct - 1), 0)
    lmap = lambda i: (lat(i), 0)
    mmap = lambda i: (jnp.where(i < nct, nb_lat, lat(i) // per_seq), 0, 0)
    row = lambda i: (jnp.minimum(i, n - 1), 0)
    prev = lambda i: jnp.maximum(i - 1, 0)
    const = lambda i: (0, 0)
    return pl.pallas_call(
        functools.partial(_post_kernel, n_ctx_tiles=nct),
        grid=(n + 1,),
        in_specs=[pl.BlockSpec((tm, D), cmap), pl.BlockSpec((tm, D), lmap),
                  pl.BlockSpec((tm, AW), cmap), pl.BlockSpec((tm, AW), lmap),
                  pl.BlockSpec((tm, GVW), cmap), pl.BlockSpec((tm, GVW), lmap),
                  pl.BlockSpec((1, 1, 6 * D), mmap),
                  pl.BlockSpec((D, MG_WIDTH), const), pl.BlockSpec((AW, D), const),
                  pl.BlockSpec((GVW, D), const), pl.BlockSpec((D, D), const),
                  pl.BlockSpec((1, D), const), pl.BlockSpec((1, D), const),
                  pl.BlockSpec((D, LANES), const), pl.BlockSpec((1, LANES), const)],
        out_specs=[pl.BlockSpec((tm, D), row),
                   pl.BlockSpec((tm, ROW_SUB, LANES), lambda i: (jnp.minimum(i, n - 1), 0, 0)),
                   pl.BlockSpec((tm, LANES), lambda i: (prev(i), 0)),
                   pl.BlockSpec((8, tm), lambda i: (0, prev(i))),
                   pl.BlockSpec((8, LANES), const)],
        out_shape=[jax.ShapeDtypeStruct((t, D), F32), jax.ShapeDtypeStruct((t, ROW_SUB, LANES), ROW_DTYPE),
                   jax.ShapeDtypeStruct((t, LANES), F32), jax.ShapeDtypeStruct((8, t), F32),
                   jax.ShapeDtypeStruct((8, LANES), F32)],
        scratch_shapes=[pltpu.VMEM((8, LANES), F32), pltpu.VMEM((ROW_GROUP * ROW_SUB, LANES), F32),
                        pltpu.VMEM((tm, LANES), F32)],
        compiler_params=_cparams(("arbitrary",)),
        name="post",
    )(x_ctx, x_lat, a_ctx, a_lat, g_ctx, g_lat, mod_all, w_mg, w_ba, w_bg, w_o, l1g, l1b, w_r, b_r)


ROW_UNROLL = 8


def _row_copy(src_ref, dst_ref, sem):
    return pltpu.make_async_copy(src_ref, dst_ref, sem)


def _scatter_kernel(pos0_ref, pos1_ref, h_ref, xs_ref, sem, zero_s, zero_sem):
    ts = h_ref.shape[0]
    n_rows = xs_ref.shape[0] - TM_EXP

    @pl.when(pl.program_id(0) == 0)
    def _():
        zero_s[...] = jnp.zeros_like(zero_s)
        pad = _row_copy(zero_s, xs_ref.at[pl.ds(n_rows, TM_EXP)], zero_sem)
        pad.start()
        pad.wait()

    def issue(g, carry):
        r0 = pl.multiple_of(g * ROW_UNROLL, ROW_UNROLL)
        for k in range(ROW_UNROLL):
            src = _row_slab(h_ref, r0 + k)
            _row_copy(src, _row_slab(xs_ref, pos0_ref[0, r0 + k]), sem).start(priority=0)
            _row_copy(src, _row_slab(xs_ref, pos1_ref[0, r0 + k]), sem).start(priority=1)
        return carry

    lax.fori_loop(0, ts // ROW_UNROLL, issue, 0)
    for _ in range(2):
        _row_copy(h_ref, xs_ref.at[pl.ds(0, ts)], sem).wait()


def _scatter_rows(h2p, pos0, pos1):
    t = h2p.shape[0]
    ts = TS_ROWS
    smem = lambda: pl.BlockSpec((None, 1, ts), lambda i: (i, 0, 0), memory_space=pltpu.SMEM)
    return pl.pallas_call(
        _scatter_kernel,
        grid=(t // ts,),
        in_specs=[smem(), smem(), pl.BlockSpec((ts, ROW_SUB, LANES), lambda i: (i, 0, 0))],
        out_specs=pl.BlockSpec(memory_space=pl.ANY),
        out_shape=jax.ShapeDtypeStruct((2 * t + TM_EXP, ROW_SUB, LANES), ROW_DTYPE),
        scratch_shapes=[pltpu.SemaphoreType.DMA(()), pltpu.VMEM((TM_EXP, ROW_SUB, LANES), ROW_DTYPE),
                        pltpu.SemaphoreType.DMA(())],
        compiler_params=_cparams(("arbitrary",)),
        name="scatter",
    )(pos0.reshape(t // ts, 1, ts), pos1.reshape(t // ts, 1, ts), h2p)


def _expert_kernel(start_ref, nwin_ref, xs_ref, wg_ref, wu_ref, wd_ref, ys_ref,
                   wgu_s, wd_s, in_buf, out_buf, slab_s, in_sem, out_sem):
    e = pl.program_id(0)
    tm = TM_EXP
    wgu_s[:, 0:D_EXP] = wg_ref[...].astype(BF16)
    wgu_s[:, D_EXP:2 * D_EXP] = wu_ref[...].astype(BF16)
    wd_s[...] = wd_ref[...].astype(BF16)
    base = start_ref[e]
    n = nwin_ref[e]

    @pl.when(e == 0)
    def _():
        out_buf[0] = jnp.zeros(out_buf.shape[1:], ROW_DTYPE)
        pad = pltpu.make_async_copy(out_buf.at[0], ys_ref.at[pl.ds(ys_ref.shape[0] - tm, tm)], out_sem.at[0])
        pad.start()
        pad.wait()

    def read(w, slot, first_row=None):
        first_row = base if first_row is None else first_row
        return pltpu.make_async_copy(xs_ref.at[pl.ds(first_row + w * tm, tm)], in_buf.at[slot], in_sem.at[slot])

    def write(w, slot):
        return pltpu.make_async_copy(out_buf.at[slot], ys_ref.at[pl.ds(base + w * tm, tm)], out_sem.at[slot])

    def start_first_reads(expert):
        for a in range(READ_AHEAD):
            @pl.when(nwin_ref[expert] > a)
            def _(a=a):
                read(a, a, start_ref[expert]).start()

    @pl.when(e == 0)
    def _():
        start_first_reads(0)

    def body(w, carry):
        slot = w % 2
        rslot = w % (READ_AHEAD + 1)

        @pl.when(w + READ_AHEAD < n)
        def _():
            read(w + READ_AHEAD, (w + READ_AHEAD) % (READ_AHEAD + 1)).start()

        read(w, rslot).wait()

        @pl.when(w >= 2)
        def _():
            write(w - 2, slot).wait()

        gu = _dot(_load_row_slabs(in_buf.at[rslot], slab_s).astype(BF16), wgu_s[...])
        hid = _silu(gu[:, 0:D_EXP]) * gu[:, D_EXP:2 * D_EXP]
        _store_row_slabs(out_buf.at[slot], _dot(hid.astype(BF16), wd_s[...]), slab_s)
        write(w, slot).start()
        return carry

    lax.fori_loop(0, n, body, 0)

    @pl.when(e + 1 < pl.num_programs(0))
    def _():
        start_first_reads(e + 1)

    @pl.when(n >= 1)
    def _():
        write(n - 1, (n - 1) % 2).wait()

    @pl.when(n >= 2)
    def _():
        write(n - 2, n % 2).wait()


def _experts(xs, starts, n_win, w_gate, w_up, w_down):
    tm = TM_EXP
    wmap = lambda e, st, nw: (e, 0, 0)
    slab = (tm, ROW_SUB, LANES)
    return pl.pallas_call(
        _expert_kernel,
        grid_spec=pltpu.PrefetchScalarGridSpec(
            num_scalar_prefetch=2,
            grid=(N_EXP,),
            in_specs=[pl.BlockSpec(memory_space=pl.ANY),
                      pl.BlockSpec((None, D, D_EXP), wmap), pl.BlockSpec((None, D, D_EXP), wmap),
                      pl.BlockSpec((None, D_EXP, D), wmap)],
            out_specs=pl.BlockSpec(memory_space=pl.ANY),
            scratch_shapes=[pltpu.VMEM((D, 2 * D_EXP), BF16), pltpu.VMEM((D_EXP, D), BF16),
                            pltpu.VMEM((READ_AHEAD + 1,) + slab, ROW_DTYPE), pltpu.VMEM((2,) + slab, ROW_DTYPE),
                            pltpu.VMEM((tm * ROW_SUB, LANES), F32),
                            pltpu.SemaphoreType.DMA((READ_AHEAD + 1,)), pltpu.SemaphoreType.DMA((2,))]),
        out_shape=jax.ShapeDtypeStruct(xs.shape, ROW_DTYPE),
        compiler_params=_cparams(("arbitrary",)),
        name="experts",
    )(starts, n_win, xs, w_gate, w_up, w_down)


def _final_kernel(p0c_ref, p1c_ref, p0n_ref, p1n_ref, x1_ref, rt_ref, mod_ref, l2g_ref, l2b_ref, ys_ref,
                  oc_ref, ol_ref, buf, sem, slab_s, *, n_ctx_tiles):
    i = pl.program_id(0)
    n = pl.num_programs(0)
    tm = x1_ref.shape[0]

    def gather(p0_ref, p1_ref, slot):
        def issue(g, carry):
            r0 = pl.multiple_of(g * ROW_UNROLL, ROW_UNROLL)
            for k in range(ROW_UNROLL):
                _row_copy(_row_slab(ys_ref, p0_ref[0, r0 + k]),
                          _row_slab(buf.at[slot, 0], r0 + k), sem.at[slot]).start(priority=0)
                _row_copy(_row_slab(ys_ref, p1_ref[0, r0 + k]),
                          _row_slab(buf.at[slot, 1], r0 + k), sem.at[slot]).start(priority=1)
            return carry

        lax.fori_loop(0, tm // ROW_UNROLL, issue, 0)

    cur = i % 2

    @pl.when(i == 0)
    def _():
        gather(p0c_ref, p1c_ref, 0)

    @pl.when(i + 1 < n)
    def _():
        gather(p0n_ref, p1n_ref, 1 - cur)

    for k in range(2):
        _row_copy(ys_ref.at[pl.ds(0, tm)], buf.at[cur, k], sem.at[cur]).wait()

    g2 = mod_ref[0, :, 5 * D:6 * D]
    w1 = rt_ref[:, 2:3]
    w2 = rt_ref[:, 3:4]
    moe = (w1 * _load_row_slabs(buf.at[cur, 0], slab_s.at[0])
           + w2 * _load_row_slabs(buf.at[cur, 1], slab_s.at[1]))
    out = _ln(ALPHA * x1_ref[...] + g2 * moe) * l2g_ref[...] + l2b_ref[...]

    @pl.when(i < n_ctx_tiles)
    def _():
        oc_ref[...] = out

    @pl.when(i >= n_ctx_tiles)
    def _():
        ol_ref[...] = out


def _final(x1, rt, pos0, pos1, mod_all, l2g, l2b, ys, t_ctx, seq_lat):
    t = x1.shape[0]
    tm = TM_TOK
    nt = t // tm
    nct = t_ctx // tm
    t_lat = t - t_ctx
    per_seq = seq_lat // tm
    nb_lat = t_lat // seq_lat
    p0 = pos0.reshape(nt, 1, tm)
    p1 = pos1.reshape(nt, 1, tm)
    row = lambda i: (i, 0)
    const = lambda i: (0, 0)
    mmap = lambda i: (jnp.where(i < nct, nb_lat, jnp.maximum(i - nct, 0) // per_seq), 0, 0)
    smem_cur = lambda: pl.BlockSpec((None, 1, tm), lambda i: (i, 0, 0), memory_space=pltpu.SMEM)
    smem_nxt = lambda: pl.BlockSpec((None, 1, tm), lambda i: (jnp.minimum(i + 1, nt - 1), 0, 0),
                                    memory_space=pltpu.SMEM)
    return pl.pallas_call(
        functools.partial(_final_kernel, n_ctx_tiles=nct),
        grid=(nt,),
        in_specs=[smem_cur(), smem_cur(), smem_nxt(), smem_nxt(),
                  pl.BlockSpec((tm, D), row), pl.BlockSpec((tm, LANES), row),
                  pl.BlockSpec((1, 1, 6 * D), mmap),
                  pl.BlockSpec((1, D), const), pl.BlockSpec((1, D), const),
                  pl.BlockSpec(memory_space=pl.ANY)],
        out_specs=[pl.BlockSpec((tm, D), lambda i: (jnp.minimum(i, nct - 1), 0)),
                   pl.BlockSpec((tm, D), lambda i: (jnp.maximum(i - nct, 0), 0))],
        out_shape=[jax.ShapeDtypeStruct((t_ctx, D), F32), jax.ShapeDtypeStruct((t_lat, D), F32)],
        scratch_shapes=[pltpu.VMEM((2, 2, tm, ROW_SUB, LANES), ROW_DTYPE), pltpu.SemaphoreType.DMA((2,)),
                        pltpu.VMEM((2, tm * ROW_SUB, LANES), F32)],
        compiler_params=_cparams(("arbitrary",)),
        name="final",
    )(p0, p1, p0, p1, x1, rt, mod_all, l2g, l2b, ys)


def _reorder_q_heads(w, axis):
    shape = w.shape
    split = shape[:axis] + (N_KV_HEADS, N_Q_HEADS // N_KV_HEADS, HD) + shape[axis + 1:]
    return jnp.swapaxes(w.reshape(split), axis, axis + 1).reshape(shape)


def _rope_tables(seq):
    t = np.arange(seq)
    half = HD // 4
    inv = (ROPE_THETA ** (-np.arange(half, dtype=np.float64) / half)).astype(np.float32)
    d64 = np.arange(LANES) % HD
    pos = np.where((d64 < HD // 2)[None, :], (t // GRID_W)[:, None], (t % GRID_W)[:, None])
    ang = (pos.astype(np.float32) * inv[d64 % half][None, :]).astype(np.float64)
    sign = np.where((d64 % 32) < 16, -1.0, 1.0)
    return (jnp.asarray(np.cos(ang), F32), jnp.asarray(np.sin(ang) * sign[None, :], F32))


def _pair_states(s):
    b = s.shape[0]
    s = s.reshape(b, 2, 2, GLA_DK, GLA_DV)
    return s.transpose(0, 1, 4, 2, 3).reshape(b, 2, GLA_DV, 2 * GLA_DK)


def _unpair_states(s):
    b = s.shape[0]
    s = s.reshape(b, 2, GLA_DV, 2, GLA_DK)
    return s.transpose(0, 1, 3, 4, 2).reshape(b, GLA_H, GLA_DK, GLA_DV)


def _route_tables(ert, counts):
    i32 = jnp.int32
    cnt = counts[0, :N_EXP].astype(i32)
    starts = jnp.cumsum(cnt) - cnt
    table = lambda e: jnp.sum(jnp.where(e[None, :] == jnp.arange(N_EXP, dtype=i32)[:, None],
                                        starts[:, None], 0), axis=0)
    pos0 = table(ert[0].astype(i32)) + ert[4].astype(i32)
    pos1 = table(ert[1].astype(i32)) + ert[5].astype(i32)
    return pos0, pos1, starts, (cnt + (TM_EXP - 1)) // TM_EXP


def kernel(x_prompt, x_sample, cache_k, cache_v, state_gla_fwd, state_gla_bwd, c, c_ctx, w_ada, b_ada, w_in, q_norm, k_norm, gla_w_gate, gla_b_gate, gla_norm, w_br_attn, w_br_gla, w_out, ln1_g, ln1_b, router_group_w, router_group_b, router_expert_w, router_expert_b, exp_w_gate, exp_w_up, exp_w_down, ln2_g, ln2_b):
    b_ctx, seq_ctx, _ = x_prompt.shape
    b_lat, seq_lat, _ = x_sample.shape
    t_ctx, t_lat = b_ctx * seq_ctx, b_lat * seq_lat
    t = t_ctx + t_lat
    l = 0

    rows = -(-(b_lat + 1) // 8) * 8
    c_rows = jnp.zeros((rows, D), F32).at[:b_lat].set(c).at[b_lat].set(c_ctx)
    mod = _ada(c_rows, w_ada[l], b_ada[l][None, :])
    mod_all = mod[:b_lat + 1, None, :]
    mod_lat = mod_all[:b_lat]
    mod_ctx = mod_all[b_lat:]

    w_full = w_in[l]
    w_a = jnp.concatenate([_reorder_q_heads(w_full[:, :AW], 1), w_full[:, AW:A_WIDTH]], axis=1).astype(BF16)
    w_mg = w_full[:, A_WIDTH:].astype(BF16)
    gain = jnp.concatenate([jnp.tile(q_norm[l], N_Q_HEADS), jnp.tile(k_norm[l], N_KV_HEADS)])[None, :]
    head_of = np.arange(AW + KVW) // HD
    ind = jnp.asarray((head_of[:, None] == np.arange(LANES)[None, :]) / HD, BF16)
    w_ba = _reorder_q_heads(w_br_attn[l], 0).astype(BF16)
    w_bg = w_br_gla[l].astype(BF16)
    w_o = w_out[l].astype(BF16)
    w_r = jnp.zeros((D, LANES), F32).at[:, :N_GROUPS].set(router_group_w[l])
    w_r = w_r.at[:, N_GROUPS:N_GROUPS + N_EXP].set(router_expert_w[l]).astype(BF16)
    b_r = jnp.zeros((1, LANES), F32).at[0, :N_GROUPS].set(router_group_b[l])
    b_r = b_r.at[0, N_GROUPS:N_GROUPS + N_EXP].set(router_expert_b[l])
    wg = jnp.zeros((2, 2 * GATE_RANK, GKW), F32)
    wg = wg.at[0, :GATE_RANK].set(gla_w_gate[l, 0]).at[1, GATE_RANK:].set(gla_w_gate[l, 1])
    bg = gla_b_gate[l][:, None, :]
    gn = gla_norm[l][None, :]

    xc = x_prompt.reshape(t_ctx, D)
    xl = x_sample.reshape(t_lat, D)

    q_c, k_c, v_c, qg_c, kg_c, vg_c, rs_c, lr_c, kf_c, vf_c = _inproj(
        xc, mod_ctx, w_a, gain, ind, None, seq_ctx, latent=False)
    attn_c = _attention(q_c, k_c, v_c, None, seq_ctx)
    zero_state = jnp.zeros((1, 2, 2, LANES, LANES), F32)
    gla_c, sfin_c = _gla(qg_c, kg_c, vg_c, lr_c, rs_c, zero_state, wg, bg, gn, seq_ctx)

    q_l, k_l, v_l, qg_l, kg_l, vg_l, rs_l, lr_l = _inproj(
        xl, mod_lat, w_a, gain, ind, _rope_tables(seq_lat), seq_lat, latent=True)
    past = cache_k.shape[2]
    kc = cache_k[:, l].reshape(b_lat, past, KVW).astype(BF16)
    vc = cache_v[:, l].reshape(b_lat, past, KVW).astype(BF16)
    attn_l = _attention(q_l, k_l, v_l, (kc, vc), seq_lat)
    s0 = jnp.stack([_pair_states(state_gla_fwd[:, l]), _pair_states(state_gla_bwd[:, l])], axis=1)
    gla_l, _ = _gla(qg_l, kg_l, vg_l, lr_l, rs_l, s0, wg, bg, gn, seq_lat)

    x1, h2p, rt, ert, counts = _post(xc, xl, attn_c, attn_l, gla_c, gla_l, mod_all, seq_lat,
                                w_mg, w_ba, w_bg, w_o, ln1_g[l][None, :], ln1_b[l][None, :], w_r, b_r)

    pos0, pos1, starts, n_win = _route_tables(ert, counts)
    xs = _scatter_rows(h2p, pos0, pos1)
    ys = _experts(xs, starts, n_win, exp_w_gate[l], exp_w_up[l], exp_w_down[l])
    y_ctx, y_lat = _final(x1, rt, pos0, pos1, mod_all, ln2_g[l][None, :], ln2_b[l][None, :], ys,
                          t_ctx, seq_lat)

    untranspose = lambda a: a.reshape(b_ctx, 1, N_KV_HEADS, HD, seq_ctx).transpose(0, 1, 4, 2, 3)
    new_k, new_v = untranspose(kf_c), untranspose(vf_c)
    new_sf = _unpair_states(sfin_c[:, 0])[:, None]
    new_sb = _unpair_states(sfin_c[:, 1])[:, None]
    return (y_ctx.reshape(b_ctx, seq_ctx, D), y_lat.reshape(b_lat, seq_lat, D),
            new_k, new_v, new_sf, new_sb)
```

```python
import functools

import numpy as np
import jax
import jax.numpy as jnp
from jax import lax
from jax.experimental import pallas as pl
from jax.experimental.pallas import tpu as pltpu

F32 = jnp.float32
BF16 = jnp.bfloat16

D = 1024
GRID_W = 64
HD = 64
N_Q_HEADS = 8
N_KV_HEADS = 2
AW = N_Q_HEADS * HD
KVW = N_KV_HEADS * HD
ROPE_THETA = 10000.0
GLA_H = 4
GLA_DK = 64
GLA_DV = 128
GKW = GLA_H * GLA_DK
GVW = GLA_H * GLA_DV
GATE_RANK = 16
GLA_TAU = 16.0
CHUNK = 64
N_GROUPS = 4
EPG = 8
N_EXP = N_GROUPS * EPG
D_EXP = 256
DEPTH = 1
ALPHA = (2.0 * DEPTH) ** 0.25
LN_EPS = 1e-6
RMS_EPS = 1e-6

LANES = 128
A_WIDTH = AW + 2 * KVW + 2 * GKW + 2 * GVW + 2 * GATE_RANK
MG_WIDTH = 2 * D
TM_TOK = 512
ROW_GROUP = 128
LOG2E = 1.4426950408889634
ONES_ROWS = 16
TQ_LAT = 256
SCORE_AHEAD = 2
TM_EXP = 256
READ_AHEAD = 3
TS_ROWS = 2048
VMEM_LIMIT = 56 * 1024 * 1024


def _cparams(sem):
    return pltpu.CompilerParams(dimension_semantics=sem, vmem_limit_bytes=VMEM_LIMIT)


def _dot(a, b):
    return jnp.dot(a, b, preferred_element_type=F32)


def _dot_nt(a, b):
    return lax.dot_general(a, b, (((1,), (1,)), ((), ())), preferred_element_type=F32)


def _dot_tn(a, b):
    return lax.dot_general(a, b, (((0,), (0,)), ((), ())), preferred_element_type=F32)


def _ln(x):
    mu = jnp.mean(x, axis=-1, keepdims=True)
    xc = x - mu
    var = jnp.mean(xc * xc, axis=-1, keepdims=True)
    return xc * lax.rsqrt(var + LN_EPS)


def _silu(x):
    return x * jax.nn.sigmoid(x)


def _split_bf16(x):
    hi = x.astype(BF16)
    lo = (x - hi.astype(F32)).astype(BF16)
    return hi, lo


def _ada_kernel(c_ref, w_ref, b_ref, o_ref):
    rows = c_ref.shape[0]
    s_hi, s_mid, s_lo = _split3_bf16(_silu(c_ref[...]))
    w_hi, w_lo = _split_bf16(w_ref[...])
    a = _dot(jnp.concatenate([s_hi, s_mid, s_lo], axis=0), w_hi)
    b = _dot(jnp.concatenate([s_hi, s_mid], axis=0), w_lo)
    o_ref[...] = (a[0:rows] + a[rows:2 * rows] + a[2 * rows:3 * rows]
                  + b[0:rows] + b[rows:2 * rows] + b_ref[...])


def _ada(c_rows, w_ada, b_ada):
    rows = c_rows.shape[0]
    n = w_ada.shape[1]
    bn = 1024
    return pl.pallas_call(
        _ada_kernel,
        grid=(n // bn,),
        in_specs=[pl.BlockSpec((rows, D), lambda j: (0, 0)),
                  pl.BlockSpec((D, bn), lambda j: (0, j)),
                  pl.BlockSpec((1, bn), lambda j: (0, j))],
        out_specs=pl.BlockSpec((rows, bn), lambda j: (0, j)),
        out_shape=jax.ShapeDtypeStruct((rows, n), F32),
        compiler_params=_cparams(("arbitrary",)),
        name="ada",
    )(c_rows, w_ada, b_ada)


def _inproj_kernel(*refs, latent, seq):
    if latent:
        (x_ref, mod_ref, w_ref, gain_ref, ind_ref, cos_ref, sin_ref,
         q_ref, k_ref, v_ref, qg_ref, kg_ref, vg_ref, rs_ref, lr_ref) = refs
    else:
        (x_ref, mod_ref, w_ref, gain_ref, ind_ref,
         q_ref, k_ref, v_ref, qg_ref, kg_ref, vg_ref, rs_ref, lr_ref, kf_ref, vf_ref) = refs
    tm = x_ref.shape[0]
    sub = ROW_GROUP
    n_groups = tm // sub
    sh1 = mod_ref[0, :, 0:D]
    sc1 = mod_ref[0, :, D:2 * D]
    lane = lax.broadcasted_iota(jnp.int32, (sub, LANES), 1)
    low_half = lane < HD
    first = (lane % 32) < 16

    def project(g):
        rows = slice(g * sub, (g + 1) * sub)
        h = (_ln(x_ref[rows, :]) * (1.0 + sc1) + sh1).astype(BF16)
        return _dot(h, w_ref[...])

    def finish(g, res):
        rows = slice(g * sub, (g + 1) * sub)
        qk = res[:, 0:AW + KVW]
        hi, lo = _split_bf16(qk * qk)
        ms = _dot(hi, ind_ref[...]) + _dot(lo, ind_ref[...])
        r = lax.rsqrt(ms + RMS_EPS)
        for s in range(5):
            rb = jnp.where(low_half, r[:, 2 * s:2 * s + 1], r[:, 2 * s + 1:2 * s + 2])
            y = res[:, LANES * s:LANES * (s + 1)] * rb * gain_ref[:, LANES * s:LANES * (s + 1)]
            if s == 4 and not latent:
                kf_ref[(g * sub) // seq, :, (g * sub) % seq:(g * sub) % seq + sub] = y.T
            if latent:
                partner = jnp.where(first, pltpu.roll(y, LANES - 16, 1), pltpu.roll(y, 16, 1))
                y = y * cos_ref[rows, :] + partner * sin_ref[rows, :]
            if s < 4:
                q_ref[rows, LANES * s:LANES * (s + 1)] = (y * (HD ** -0.5 * LOG2E)).astype(BF16)
            else:
                k_ref[rows, :] = y.astype(BF16)
        o = AW + KVW
        v = res[:, o:o + KVW]
        v_ref[rows, :] = v.astype(BF16)
        if not latent:
            vf_ref[(g * sub) // seq, :, (g * sub) % seq:(g * sub) % seq + sub] = v.T
        o += KVW
        qg_ref[rows, :] = res[:, o:o + GKW] * (GLA_DK ** -0.5)
        o += GKW
        kg_ref[rows, :] = res[:, o:o + GKW]
        o += GKW
        vg_ref[rows, :] = res[:, o:o + GVW].astype(BF16)
        o += GVW
        rs_ref[rows, :] = _silu(res[:, o:o + GVW]).astype(BF16)
        o += GVW
        lr_ref[rows, :] = res[:, o:o + 2 * GATE_RANK]

    pending = {0: project(0)}
    for g in range(n_groups):
        if g + 1 < n_groups:
            pending[g + 1] = project(g + 1)
        finish(g, pending.pop(g))


def _inproj(x2, mod, w_a, gain, ind, rope, seq, latent):
    t = x2.shape[0]
    tm = TM_TOK
    per_seq = max(seq // tm, 1)
    per_tile = max(tm // seq, 1)
    row = lambda i: (i, 0)
    const = lambda i: (0, 0)
    in_specs = [pl.BlockSpec((tm, D), row),
                pl.BlockSpec((1, 1, 6 * D), (lambda i: (i // per_seq, 0, 0)) if latent else (lambda i: (0, 0, 0))),
                pl.BlockSpec((D, A_WIDTH), const),
                pl.BlockSpec((1, AW + KVW), const),
                pl.BlockSpec((AW + KVW, LANES), const)]
    args = [x2, mod, w_a, gain, ind]
    if latent:
        in_specs += [pl.BlockSpec((tm, LANES), lambda i: (i % per_seq, 0))] * 2
        args += list(rope)
    widths = [(AW, BF16), (KVW, BF16), (KVW, BF16), (GKW, F32), (GKW, F32), (GVW, BF16), (GVW, BF16),
              (2 * GATE_RANK, F32)]
    out_specs = [pl.BlockSpec((tm, w), row) for w, _ in widths]
    out_shape = [jax.ShapeDtypeStruct((t, w), dt) for w, dt in widths]
    if not latent:
        cache_spec = pl.BlockSpec((per_tile, KVW, seq), lambda i: (i, 0, 0))
        out_specs += [cache_spec] * 2
        out_shape += [jax.ShapeDtypeStruct((t // seq, KVW, seq), F32)] * 2
    return pl.pallas_call(
        functools.partial(_inproj_kernel, latent=latent, seq=seq),
        grid=(t // tm,),
        in_specs=in_specs,
        out_specs=out_specs,
        out_shape=out_shape,
        compiler_params=_cparams(("parallel",)),
        name="inproj_lat" if latent else "inproj_ctx",
    )(*args)


def _attn_kernel(*refs, has_cache):
    def transposed_with_ones(dst, src):
        dst[0:KVW, :] = src[...].astype(F32).T.astype(BF16)
        dst[KVW:, :] = jnp.ones((ONES_ROWS, dst.shape[1]), BF16)

    if has_cache:
        q_ref, k_ref, v_ref, kc_ref, vc_ref, o_ref, vt_s, vct_s = refs

        @pl.when(pl.program_id(1) == 0)
        def _():
            transposed_with_ones(vt_s, v_ref)
            transposed_with_ones(vct_s, vc_ref)
    else:
        q_ref, k_ref, v_ref, o_ref, vt_s = refs
        transposed_with_ones(vt_s, v_ref)
    tq = q_ref.shape[0]

    lane = lax.broadcasted_iota(jnp.int32, (tq, LANES), 1)
    low_half = lane < HD
    k = k_ref[...]
    def score(c):
        j, pair = divmod(c, 2)
        keep = low_half if j == 0 else jnp.logical_not(low_half)
        zero = jnp.zeros((tq, LANES), BF16)
        qs = jnp.concatenate([jnp.where(keep, q_ref[:, LANES * s:LANES * (s + 1)], zero)
                              for s in (2 * pair, 2 * pair + 1)], axis=0)
        return _dot_nt(k, qs), (_dot_nt(kc_ref[...], qs) if has_cache else None)

    n_chains = 2 * N_KV_HEADS
    ahead = SCORE_AHEAD if has_cache else n_chains
    scores = {c: score(c) for c in range(ahead)}
    outs = []
    for c in range(n_chains):
        if c + ahead < n_chains:
            scores[c + ahead] = score(c + ahead)
        s1, s2 = scores.pop(c)
        m = jnp.max(s1, axis=0, keepdims=True)
        if has_cache:
            m = jnp.maximum(m, jnp.max(s2, axis=0, keepdims=True))
        acc = _dot(vt_s[...], jnp.exp2(s1 - m).astype(BF16))
        if has_cache:
            acc = acc + _dot(vct_s[...], jnp.exp2(s2 - m).astype(BF16))
        outs.append(acc[0:KVW] / acc[KVW:KVW + 1])
    head0 = jnp.concatenate(outs[0:2], axis=1)
    head1 = jnp.concatenate(outs[2:4], axis=1)
    row = lax.broadcasted_iota(jnp.int32, (LANES, 4 * tq), 0)
    out = jnp.where(row < HD, head0, head1).T
    for s in range(4):
        o_ref[:, LANES * s:LANES * (s + 1)] = out[s * tq:(s + 1) * tq].astype(BF16)


def _attention(q, k, v, cache, seq):
    t = q.shape[0]
    if cache is None:
        tq = seq
        grid = (t // seq,)
        qmap = lambda b: (b, 0)
        in_specs = [pl.BlockSpec((tq, AW), qmap), pl.BlockSpec((seq, KVW), qmap),
                    pl.BlockSpec((seq, KVW), qmap)]
        args = [q, k, v]
        scratch = [pltpu.VMEM((KVW + ONES_ROWS, seq), BF16)]
        sem = ("parallel",)
        name = "attn_ctx"
    else:
        tq = TQ_LAT
        nq = seq // tq
        kc, vc = cache
        past = kc.shape[1]
        grid = (t // seq, nq)
        qmap = lambda b, i: (b * nq + i, 0)
        kmap = lambda b, i: (b, 0)
        cmap = lambda b, i: (b, 0, 0)
        in_specs = [pl.BlockSpec((tq, AW), qmap), pl.BlockSpec((seq, KVW), kmap),
                    pl.BlockSpec((seq, KVW), kmap),
                    pl.BlockSpec((None, past, KVW), cmap), pl.BlockSpec((None, past, KVW), cmap)]
        args = [q, k, v, kc, vc]
        scratch = [pltpu.VMEM((KVW + ONES_ROWS, seq), BF16), pltpu.VMEM((KVW + ONES_ROWS, past), BF16)]
        sem = ("parallel", "arbitrary")
        name = "attn_lat"
    return pl.pallas_call(
        functools.partial(_attn_kernel, has_cache=cache is not None),
        grid=grid,
        in_specs=in_specs,
        out_specs=pl.BlockSpec((tq, AW), qmap),
        out_shape=jax.ShapeDtypeStruct((t, AW), BF16),
        scratch_shapes=scratch,
        compiler_params=_cparams(sem),
        name=name,
    )(*args)


GLA_BLK = 256
GLA_ROWS = 1024
GLA_UNROLL = 8


def _split3_bf16(x):
    hi = x.astype(BF16)
    r1 = x - hi.astype(F32)
    mid = r1.astype(BF16)
    lo = (r1 - mid.astype(F32)).astype(BF16)
    return hi, mid, lo


def _gla_kernel(qg_ref, kg_ref, vg_ref, lr_ref, rs_ref, s0_ref, wg_ref, bg_ref, gn_ref,
                o_ref, sfin_ref, cum_s, kv_s, dec_s, *, seqs):
    n = qg_ref.shape[0]
    nc = n // CHUNK
    nc_seq = nc // seqs
    unroll = min(GLA_UNROLL, nc)
    lane = lax.broadcasted_iota(jnp.int32, (CHUNK, LANES), 1)
    low_half = lane < GLA_DK
    lane_sq = lax.broadcasted_iota(jnp.int32, (LANES, LANES), 1)
    low_half_sq = lane_sq < GLA_DK
    ri = lax.broadcasted_iota(jnp.int32, (2 * CHUNK, 2 * CHUNK), 0)
    ci = lax.broadcasted_iota(jnp.int32, (2 * CHUNK, 2 * CHUNK), 1)
    diag = (ri >> 6) == (ci >> 6)
    keep2 = (diag & (ri >= ci), diag & (ci >= ri))

    rb = lax.broadcasted_iota(jnp.int32, (GLA_BLK, GLA_BLK), 0)
    cb = lax.broadcasted_iota(jnp.int32, (GLA_BLK, GLA_BLK), 1)
    same = (rb >> 6) == (cb >> 6)
    tri = ((same & (rb >= cb)).astype(BF16), (same & (cb >= rb)).astype(BF16))
    for d in range(2):
        w_hi, w_lo = _split_bf16(wg_ref[d])
        for blk in range(n // GLA_BLK):
            rows = slice(blk * GLA_BLK, (blk + 1) * GLA_BLK)
            l_hi, l_lo = _split_bf16(lr_ref[rows, :])
            z = _dot(l_hi, w_hi) + _dot(l_lo, w_hi) + _dot(l_hi, w_lo) + bg_ref[d]
            logg = (jnp.minimum(z, 0.0) - jnp.log(1.0 + jnp.exp(-jnp.abs(z)))) * (1.0 / GLA_TAU)
            pieces = _split3_bf16(logg)
            cum_s[d, rows, :] = _dot(tri[d], pieces[0]) + _dot(tri[d], pieces[1]) + _dot(tri[d], pieces[2])

    def pass1(i, carry):
        cs = [i * unroll + u for u in range(unroll)]
        rws = [pl.ds(pl.multiple_of(c * CHUNK, CHUNK), CHUNK) for c in cs]
        prods = []
        for c, rows in zip(cs, rws):
            kc = kg_ref[rows, :]
            kdec = []
            for d in range(2):
                cum = cum_s[d, rows, :]
                last = cum[CHUNK - 1:CHUNK, :] if d == 0 else cum[0:1, :]
                dec_s[d, c] = jnp.exp(last)
                kdec.append((kc * jnp.exp(last - cum)).astype(BF16))
            for p in range(2):
                vpair = vg_ref[rows, GLA_DV * 2 * p:GLA_DV * 2 * (p + 1)]
                kpair = jnp.concatenate([kdec[0][:, LANES * p:LANES * (p + 1)],
                                         kdec[1][:, LANES * p:LANES * (p + 1)]], axis=1)
                prods.append(_dot_tn(vpair, kpair))
        for j, c in enumerate(cs):
            for p in range(2):
                res = prods[2 * j + p]
                for d in range(2):
                    cols = slice(LANES * d, LANES * (d + 1))
                    kv_s[d, c, p] = jnp.where(low_half_sq, res[0:GLA_DV, cols], res[GLA_DV:2 * GLA_DV, cols])
        return carry

    lax.fori_loop(0, nc // unroll, pass1, 0)

    for s in range(seqs):
        for d in range(2):
            def scan(i, st, s=s, d=d):
                c = s * nc_seq + (i if d == 0 else nc_seq - 1 - i)
                dec = dec_s[d, c]
                new = []
                for p in range(2):
                    kv = kv_s[d, c, p]
                    kv_s[d, c, p] = st[p]
                    new.append(st[p] * dec[:, LANES * p:LANES * (p + 1)] + kv)
                return tuple(new)

            s_in = s if s0_ref.shape[0] > 1 else 0
            fin = lax.fori_loop(0, nc_seq, scan, (s0_ref[s_in, d, 0], s0_ref[s_in, d, 1]))
            sfin_ref[s, d, 0] = fin[0]
            sfin_ref[s, d, 1] = fin[1]

    def pass3(i, carry):
        cs = [i * unroll + u for u in range(unroll)]
        rws = [pl.ds(pl.multiple_of(c * CHUNK, CHUNK), CHUNK) for c in cs]
        first = []
        for c, rows in zip(cs, rws):
            q = qg_ref[rows, :]
            k = kg_ref[rows, :]
            for d in range(2):
                cum = cum_s[d, rows, :]
                qt = q * jnp.exp(cum)
                kt = (k * jnp.exp(-cum)).astype(BF16)
                for p in range(2):
                    qs = qt[:, LANES * p:LANES * (p + 1)]
                    lhs = jnp.concatenate([jnp.where(low_half, qs, 0.0), jnp.where(low_half, 0.0, qs)],
                                          axis=0).astype(BF16)
                    kts = kt[:, LANES * p:LANES * (p + 1)]
                    rhs = jnp.concatenate([kts, kts, kv_s[d, c, p].astype(BF16)], axis=0)
                    first.append(_dot_nt(lhs, rhs))
        second = []
        for j, rows in enumerate(rws):
            for d in range(2):
                for p in range(2):
                    res = first[4 * j + 2 * d + p]
                    vp = jnp.concatenate([vg_ref[rows, GLA_DV * (2 * p):GLA_DV * (2 * p + 1)],
                                          vg_ref[rows, GLA_DV * (2 * p + 1):GLA_DV * (2 * p + 2)]], axis=0)
                    a = jnp.where(keep2[d], res[:, 0:2 * CHUNK], 0.0).astype(BF16)
                    second.append(_dot(a, vp) + res[:, 2 * CHUNK:])
        for j, rows in enumerate(rws):
            for p in range(2):
                tot = second[4 * j + p] + second[4 * j + 2 + p]
                y = tot * lax.rsqrt(jnp.mean(tot * tot, axis=-1, keepdims=True) + RMS_EPS) * gn_ref[...]
                for hh in range(2):
                    cols = slice(GLA_DV * (2 * p + hh), GLA_DV * (2 * p + hh + 1))
                    o_ref[rows, cols] = (y[CHUNK * hh:CHUNK * (hh + 1)]
                                         * rs_ref[rows, cols].astype(F32)).astype(BF16)
        return carry

    lax.fori_loop(0, nc // unroll, pass3, 0)


def _gla(qg, kg, vg, lr, rs, s0, wg, bg, gn, seq):
    t = qg.shape[0]
    nb = t // seq
    seqs = max(GLA_ROWS // seq, 1)
    rows = seqs * seq
    nc = rows // CHUNK
    row = lambda b: (b, 0)
    c3 = lambda b: (0, 0, 0)
    st = lambda b: (b, 0, 0, 0, 0)
    if s0.shape[0] == 1:
        s0_spec = pl.BlockSpec((1, 2, 2, LANES, LANES), lambda b: (0, 0, 0, 0, 0))
    else:
        s0_spec = pl.BlockSpec((seqs, 2, 2, LANES, LANES), st)
    return pl.pallas_call(
        functools.partial(_gla_kernel, seqs=seqs),
        grid=(nb // seqs,),
        in_specs=[pl.BlockSpec((rows, GKW), row), pl.BlockSpec((rows, GKW), row),
                  pl.BlockSpec((rows, GVW), row), pl.BlockSpec((rows, 2 * GATE_RANK), row),
                  pl.BlockSpec((rows, GVW), row),
                  s0_spec,
                  pl.BlockSpec((2, 2 * GATE_RANK, GKW), c3), pl.BlockSpec((2, 1, GKW), c3),
                  pl.BlockSpec((1, GLA_DV), lambda b: (0, 0))],
        out_specs=[pl.BlockSpec((rows, GVW), row),
                   pl.BlockSpec((seqs, 2, 2, LANES, LANES), st)],
        out_shape=[jax.ShapeDtypeStruct((t, GVW), BF16),
                   jax.ShapeDtypeStruct((nb, 2, 2, LANES, LANES), F32)],
        scratch_shapes=[pltpu.VMEM((2, rows, GKW), F32),
                        pltpu.VMEM((2, nc, 2, LANES, LANES), F32),
                        pltpu.VMEM((2, nc, 1, GKW), F32)],
        compiler_params=_cparams(("parallel",)),
        name="gla_lat" if seq > 256 else "gla_ctx",
    )(qg, kg, vg, lr, rs, s0, wg, bg, gn)


ROW_SUB = D // LANES
ROW_DTYPE = BF16


def _store_row_slabs(ref, x, tmp):
    m = x.shape[0]
    for c in range(ROW_SUB):
        tmp[pl.ds(c, m, stride=ROW_SUB), :] = x[:, LANES * c:LANES * (c + 1)]
    ref[...] = tmp[...].reshape(m, ROW_SUB, LANES).astype(ROW_DTYPE)


def _load_row_slabs(ref, tmp):
    m = ref.shape[0]
    tmp[...] = ref[...].astype(F32).reshape(m * ROW_SUB, LANES)
    return jnp.concatenate([tmp[pl.ds(c, m, stride=ROW_SUB), :] for c in range(ROW_SUB)], axis=1)


def _row_slab(ref, row):
    return ref.at[pl.ds(row, 1)]


def _post_kernel(xc_ref, xl_ref, ac_ref, al_ref, gc_ref, gl_ref, mod_ref,
                 wmg_ref, wba_ref, wbg_ref, wo_ref, l1g_ref, l1b_ref, wr_ref, br_ref,
                 x1_ref, h2_ref, rt_ref, ert_ref, cnt_ref, run_s, slab_s, logit_s, *, n_ctx_tiles):
    i = pl.program_id(0)
    tm = xc_ref.shape[0]
    sub = ROW_GROUP
    n_groups = tm // sub
    is_ctx = i < n_ctx_tiles
    parts = [slice(g * sub, (g + 1) * sub) for g in range(n_groups)]

    @pl.when(i == 0)
    def _():
        run_s[...] = jnp.zeros_like(run_s)
        logit_s[...] = jnp.zeros_like(logit_s)

    sh1 = mod_ref[0, :, 0:D]
    sc1 = mod_ref[0, :, D:2 * D]
    g1 = mod_ref[0, :, 2 * D:3 * D]
    sh2 = mod_ref[0, :, 3 * D:4 * D]
    sc2 = mod_ref[0, :, 4 * D:5 * D]

    xs = [jnp.where(is_ctx, xc_ref[r, :], xl_ref[r, :]) for r in parts]
    hs = [(_ln(x) * (1.0 + sc1) + sh1).astype(BF16) for x in xs]
    gates = [jax.nn.sigmoid(_dot(h, wmg_ref[...])) for h in hs]
    ba = [_dot(jnp.where(is_ctx, ac_ref[r, :], al_ref[r, :]), wba_ref[...]) for r in parts]
    bg = [_dot(jnp.where(is_ctx, gc_ref[r, :], gl_ref[r, :]), wbg_ref[...]) for r in parts]

    ri = lax.broadcasted_iota(jnp.int32, (sub, sub), 0)
    ci = lax.broadcasted_iota(jnp.int32, (sub, sub), 1)
    earlier = (ri > ci).astype(BF16)
    counted = (i > 0).astype(F32)
    run = run_s[0:1, :]
    for r in parts:
        run = _route(logit_s[r, :], earlier, run, counted, rt_ref.at[r, :], ert_ref.at[:, r])
    run_s[0:1, :] = run
    cnt_ref[...] = jnp.broadcast_to(run, cnt_ref.shape)

    merged = [(g[:, :D] * a + g[:, D:] * b).astype(BF16) for g, a, b in zip(gates, ba, bg)]
    mix = [_dot(m, wo_ref[...]) for m in merged]
    x1s = [_ln(ALPHA * x + g1 * m) * l1g_ref[...] + l1b_ref[...] for x, m in zip(xs, mix)]
    h2s = [_ln(x1) * (1.0 + sc2) + sh2 for x1 in x1s]
    logits = [_dot(h2.astype(BF16), wr_ref[...]) + br_ref[...] for h2 in h2s]
    for g, r in enumerate(parts):
        x1_ref[r, :] = x1s[g]
        _store_row_slabs(h2_ref.at[pl.ds(g * sub, sub)], h2s[g], slab_s)
        logit_s[r, :] = logits[g]


def _route(logit, earlier, run, counted, rt_ref, ert_ref):
    tm = logit.shape[0]
    lane_i = lax.broadcasted_iota(jnp.int32, (tm, LANES), 1)
    lane = lane_i.astype(F32)
    lane_grp = ((lane_i - N_GROUPS) >> 3).astype(F32)
    neg = jnp.float32(-jnp.inf)
    far = jnp.float32(LANES)
    is_g = lane_i < N_GROUPS
    lg = jnp.where(is_g, logit, neg)
    mg = jnp.max(lg, axis=-1, keepdims=True)
    pg_top = 1.0 / jnp.sum(jnp.where(is_g, jnp.exp(logit - mg), 0.0), axis=-1, keepdims=True)
    g_idx = jnp.min(jnp.where(lg == mg, lane, far), axis=-1, keepdims=True)
    in_grp = (lane_i >= N_GROUPS) & (lane_i < N_GROUPS + N_EXP) & (lane_grp == g_idx)
    le = jnp.where(in_grp, logit, neg)
    v1 = jnp.max(le, axis=-1, keepdims=True)
    i1 = jnp.min(jnp.where(le == v1, lane, far), axis=-1, keepdims=True)
    le2 = jnp.where(lane == i1, neg, le)
    v2 = jnp.max(le2, axis=-1, keepdims=True)
    i2 = jnp.min(jnp.where(le2 == v2, lane, far), axis=-1, keepdims=True)
    e1 = i1 - N_GROUPS
    e2 = i2 - N_GROUPS
    tt = jnp.exp(v2 - v1)
    w1 = pg_top / (1.0 + tt)
    w2 = pg_top * tt / (1.0 + tt)

    hot = ((lane == e1) | (lane == e2)).astype(F32)
    before = _dot(earlier, hot.astype(BF16)) + run
    r1 = jnp.sum(jnp.where(lane == e1, before, 0.0), axis=-1, keepdims=True)
    r2 = jnp.sum(jnp.where(lane == e2, before, 0.0), axis=-1, keepdims=True)

    rt = jnp.where(lane_i == 0, e1, 0.0)
    rt = jnp.where(lane_i == 1, e2, rt)
    rt = jnp.where(lane_i == 2, w1, rt)
    rt = jnp.where(lane_i == 3, w2, rt)
    rt = jnp.where(lane_i == 4, r1, rt)
    rt = jnp.where(lane_i == 5, r2, rt)
    rt_ref[...] = rt
    ert_ref[...] = rt.T[0:8, :]
    return run + counted * jnp.sum(hot, axis=0, keepdims=True)


def _post(x_ctx, x_lat, a_ctx, a_lat, g_ctx, g_lat, mod_all, seq_lat,
          w_mg, w_ba, w_bg, w_o, l1g, l1b, w_r, b_r):
    t_ctx, t_lat = x_ctx.shape[0], x_lat.shape[0]
    tm = TM_TOK
    nct, nlt = t_ctx // tm, t_lat // tm
    per_seq = seq_lat // tm
    nb_lat = t_lat // seq_lat
    t = t_ctx + t_lat
    n = nct + nlt
    lat = lambda i: jnp.clip(i - nct, 0, nlt - 1)
    cmap = lambda i: (jnp.minimum(i, nct - 1), 0)
    lmap = lambda i: (lat(i), 0)
    mmap = lambda i: (jnp.where(i < nct, nb_lat, lat(i) // per_seq), 0, 0)
    row = lambda i: (jnp.minimum(i, n - 1), 0)
    prev = lambda i: jnp.maximum(i - 1, 0)
    const = lambda i: (0, 0)
    return pl.pallas_call(
        functools.partial(_post_kernel, n_ctx_tiles=nct),
        grid=(n + 1,),
        in_specs=[pl.BlockSpec((tm, D), cmap), pl.BlockSpec((tm, D), lmap),
                  pl.BlockSpec((tm, AW), cmap), pl.BlockSpec((tm, AW), lmap),
                  pl.BlockSpec((tm, GVW), cmap), pl.BlockSpec((tm, GVW), lmap),
                  pl.BlockSpec((1, 1, 6 * D), mmap),
                  pl.BlockSpec((D, MG_WIDTH), const), pl.BlockSpec((AW, D), const),
                  pl.BlockSpec((GVW, D), const), pl.BlockSpec((D, D), const),
                  pl.BlockSpec((1, D), const), pl.BlockSpec((1, D), const),
                  pl.BlockSpec((D, LANES), const), pl.BlockSpec((1, LANES), const)],
        out_specs=[pl.BlockSpec((tm, D), row),
                   pl.BlockSpec((tm, ROW_SUB, LANES), lambda i: (jnp.minimum(i, n - 1), 0, 0)),
                   pl.BlockSpec((tm, LANES), lambda i: (prev(i), 0)),
                   pl.BlockSpec((8, tm), lambda i: (0, prev(i))),
                   pl.BlockSpec((8, LANES), const)],
        out_shape=[jax.ShapeDtypeStruct((t, D), F32), jax.ShapeDtypeStruct((t, ROW_SUB, LANES), ROW_DTYPE),
                   jax.ShapeDtypeStruct((t, LANES), F32), jax.ShapeDtypeStruct((8, t), F32),
                   jax.ShapeDtypeStruct((8, LANES), F32)],
        scratch_shapes=[pltpu.VMEM((8, LANES), F32), pltpu.VMEM((ROW_GROUP * ROW_SUB, LANES), F32),
                        pltpu.VMEM((tm, LANES), F32)],
        compiler_params=_cparams(("arbitrary",)),
        name="post",
    )(x_ctx, x_lat, a_ctx, a_lat, g_ctx, g_lat, mod_all, w_mg, w_ba, w_bg, w_o, l1g, l1b, w_r, b_r)


ROW_UNROLL = 8


def _row_copy(src_ref, dst_ref, sem):
    return pltpu.make_async_copy(src_ref, dst_ref, sem)


def _scatter_kernel(pos0_ref, pos1_ref, h_ref, xs_ref, sem, zero_s, zero_sem):
    ts = h_ref.shape[0]
    n_rows = xs_ref.shape[0] - TM_EXP

    @pl.when(pl.program_id(0) == 0)
    def _():
        zero_s[...] = jnp.zeros_like(zero_s)
        pad = _row_copy(zero_s, xs_ref.at[pl.ds(n_rows, TM_EXP)], zero_sem)
        pad.start()
        pad.wait()

    def issue(g, carry):
        r0 = pl.multiple_of(g * ROW_UNROLL, ROW_UNROLL)
        for k in range(ROW_UNROLL):
            src = _row_slab(h_ref, r0 + k)
            _row_copy(src, _row_slab(xs_ref, pos0_ref[0, r0 + k]), sem).start(priority=0)
            _row_copy(src, _row_slab(xs_ref, pos1_ref[0, r0 + k]), sem).start(priority=1)
        return carry

    lax.fori_loop(0, ts // ROW_UNROLL, issue, 0)
    for _ in range(2):
        _row_copy(h_ref, xs_ref.at[pl.ds(0, ts)], sem).wait()


def _scatter_rows(h2p, pos0, pos1):
    t = h2p.shape[0]
    ts = TS_ROWS
    smem = lambda: pl.BlockSpec((None, 1, ts), lambda i: (i, 0, 0), memory_space=pltpu.SMEM)
    return pl.pallas_call(
        _scatter_kernel,
        grid=(t // ts,),
        in_specs=[smem(), smem(), pl.BlockSpec((ts, ROW_SUB, LANES), lambda i: (i, 0, 0))],
        out_specs=pl.BlockSpec(memory_space=pl.ANY),
        out_shape=jax.ShapeDtypeStruct((2 * t + TM_EXP, ROW_SUB, LANES), ROW_DTYPE),
        scratch_shapes=[pltpu.SemaphoreType.DMA(()), pltpu.VMEM((TM_EXP, ROW_SUB, LANES), ROW_DTYPE),
                        pltpu.SemaphoreType.DMA(())],
        compiler_params=_cparams(("arbitrary",)),
        name="scatter",
    )(pos0.reshape(t // ts, 1, ts), pos1.reshape(t // ts, 1, ts), h2p)


def _expert_kernel(start_ref, nwin_ref, xs_ref, wg_ref, wu_ref, wd_ref, ys_ref,
                   wgu_s, wd_s, in_buf, out_buf, slab_s, in_sem, out_sem):
    e = pl.program_id(0)
    tm = TM_EXP
    wgu_s[:, 0:D_EXP] = wg_ref[...].astype(BF16)
    wgu_s[:, D_EXP:2 * D_EXP] = wu_ref[...].astype(BF16)
    wd_s[...] = wd_ref[...].astype(BF16)
    base = start_ref[e]
    n = nwin_ref[e]

    @pl.when(e == 0)
    def _():
        out_buf[0] = jnp.zeros(out_buf.shape[1:], ROW_DTYPE)
        pad = pltpu.make_async_copy(out_buf.at[0], ys_ref.at[pl.ds(ys_ref.shape[0] - tm, tm)], out_sem.at[0])
        pad.start()
        pad.wait()

    def read(w, slot, first_row=None):
        first_row = base if first_row is None else first_row
        return pltpu.make_async_copy(xs_ref.at[pl.ds(first_row + w * tm, tm)], in_buf.at[slot], in_sem.at[slot])

    def write(w, slot):
        return pltpu.make_async_copy(out_buf.at[slot], ys_ref.at[pl.ds(base + w * tm, tm)], out_sem.at[slot])

    def start_first_reads(expert):
        for a in range(READ_AHEAD):
            @pl.when(nwin_ref[expert] > a)
            def _(a=a):
                read(a, a, start_ref[expert]).start()

    @pl.when(e == 0)
    def _():
        start_first_reads(0)

    def body(w, carry):
        slot = w % 2
        rslot = w % (READ_AHEAD + 1)

        @pl.when(w + READ_AHEAD < n)
        def _():
            read(w + READ_AHEAD, (w + READ_AHEAD) % (READ_AHEAD + 1)).start()

        read(w, rslot).wait()

        @pl.when(w >= 2)
        def _():
            write(w - 2, slot).wait()

        gu = _dot(_load_row_slabs(in_buf.at[rslot], slab_s).astype(BF16), wgu_s[...])
        hid = _silu(gu[:, 0:D_EXP]) * gu[:, D_EXP:2 * D_EXP]
        _store_row_slabs(out_buf.at[slot], _dot(hid.astype(BF16), wd_s[...]), slab_s)
        write(w, slot).start()
        return carry

    lax.fori_loop(0, n, body, 0)

    @pl.when(e + 1 < pl.num_programs(0))
    def _():
        start_first_reads(e + 1)

    @pl.when(n >= 1)
    def _():
        write(n - 1, (n - 1) % 2).wait()

    @pl.when(n >= 2)
    def _():
        write(n - 2, n % 2).wait()


def _experts(xs, starts, n_win, w_gate, w_up, w_down):
    tm = TM_EXP
    wmap = lambda e, st, nw: (e, 0, 0)
    slab = (tm, ROW_SUB, LANES)
    return pl.pallas_call(
        _expert_kernel,
        grid_spec=pltpu.PrefetchScalarGridSpec(
            num_scalar_prefetch=2,
            grid=(N_EXP,),
            in_specs=[pl.BlockSpec(memory_space=pl.ANY),
                      pl.BlockSpec((None, D, D_EXP), wmap), pl.BlockSpec((None, D, D_EXP), wmap),
                      pl.BlockSpec((None, D_EXP, D), wmap)],
            out_specs=pl.BlockSpec(memory_space=pl.ANY),
            scratch_shapes=[pltpu.VMEM((D, 2 * D_EXP), BF16), pltpu.VMEM((D_EXP, D), BF16),
                            pltpu.VMEM((READ_AHEAD + 1,) + slab, ROW_DTYPE), pltpu.VMEM((2,) + slab, ROW_DTYPE),
                            pltpu.VMEM((tm * ROW_SUB, LANES), F32),
                            pltpu.SemaphoreType.DMA((READ_AHEAD + 1,)), pltpu.SemaphoreType.DMA((2,))]),
        out_shape=jax.ShapeDtypeStruct(xs.shape, ROW_DTYPE),
        compiler_params=_cparams(("arbitrary",)),
        name="experts",
    )(starts, n_win, xs, w_gate, w_up, w_down)


def _final_kernel(p0c_ref, p1c_ref, p0n_ref, p1n_ref, x1_ref, rt_ref, mod_ref, l2g_ref, l2b_ref, ys_ref,
                  oc_ref, ol_ref, buf, sem, slab_s, *, n_ctx_tiles):
    i = pl.program_id(0)
    n = pl.num_programs(0)
    tm = x1_ref.shape[0]

    def gather(p0_ref, p1_ref, slot):
        def issue(g, carry):
            r0 = pl.multiple_of(g * ROW_UNROLL, ROW_UNROLL)
            for k in range(ROW_UNROLL):
                _row_copy(_row_slab(ys_ref, p0_ref[0, r0 + k]),
                          _row_slab(buf.at[slot, 0], r0 + k), sem.at[slot]).start(priority=0)
                _row_copy(_row_slab(ys_ref, p1_ref[0, r0 + k]),
                          _row_slab(buf.at[slot, 1], r0 + k), sem.at[slot]).start(priority=1)
            return carry

        lax.fori_loop(0, tm // ROW_UNROLL, issue, 0)

    cur = i % 2

    @pl.when(i == 0)
    def _():
        gather(p0c_ref, p1c_ref, 0)

    @pl.when(i + 1 < n)
    def _():
        gather(p0n_ref, p1n_ref, 1 - cur)

    for k in range(2):
        _row_copy(ys_ref.at[pl.ds(0, tm)], buf.at[cur, k], sem.at[cur]).wait()

    g2 = mod_ref[0, :, 5 * D:6 * D]
    w1 = rt_ref[:, 2:3]
    w2 = rt_ref[:, 3:4]
    moe = (w1 * _load_row_slabs(buf.at[cur, 0], slab_s.at[0])
           + w2 * _load_row_slabs(buf.at[cur, 1], slab_s.at[1]))
    out = _ln(ALPHA * x1_ref[...] + g2 * moe) * l2g_ref[...] + l2b_ref[...]

    @pl.when(i < n_ctx_tiles)
    def _():
        oc_ref[...] = out

    @pl.when(i >= n_ctx_tiles)
    def _():
        ol_ref[...] = out


def _final(x1, rt, pos0, pos1, mod_all, l2g, l2b, ys, t_ctx, seq_lat):
    t = x1.shape[0]
    tm = TM_TOK
    nt = t // tm
    nct = t_ctx // tm
    t_lat = t - t_ctx
    per_seq = seq_lat // tm
    nb_lat = t_lat // seq_lat
    p0 = pos0.reshape(nt, 1, tm)
    p1 = pos1.reshape(nt, 1, tm)
    row = lambda i: (i, 0)
    const = lambda i: (0, 0)
    mmap = lambda i: (jnp.where(i < nct, nb_lat, jnp.maximum(i - nct, 0) // per_seq), 0, 0)
    smem_cur = lambda: pl.BlockSpec((None, 1, tm), lambda i: (i, 0, 0), memory_space=pltpu.SMEM)
    smem_nxt = lambda: pl.BlockSpec((None, 1, tm), lambda i: (jnp.minimum(i + 1, nt - 1), 0, 0),
                                    memory_space=pltpu.SMEM)
    return pl.pallas_call(
        functools.partial(_final_kernel, n_ctx_tiles=nct),
        grid=(nt,),
        in_specs=[smem_cur(), smem_cur(), smem_nxt(), smem_nxt(),
                  pl.BlockSpec((tm, D), row), pl.BlockSpec((tm, LANES), row),
                  pl.BlockSpec((1, 1, 6 * D), mmap),
                  pl.BlockSpec((1, D), const), pl.BlockSpec((1, D), const),
                  pl.BlockSpec(memory_space=pl.ANY)],
        out_specs=[pl.BlockSpec((tm, D), lambda i: (jnp.minimum(i, nct - 1), 0)),
                   pl.BlockSpec((tm, D), lambda i: (jnp.maximum(i - nct, 0), 0))],
        out_shape=[jax.ShapeDtypeStruct((t_ctx, D), F32), jax.ShapeDtypeStruct((t_lat, D), F32)],
        scratch_shapes=[pltpu.VMEM((2, 2, tm, ROW_SUB, LANES), ROW_DTYPE), pltpu.SemaphoreType.DMA((2,)),
                        pltpu.VMEM((2, tm * ROW_SUB, LANES), F32)],
        compiler_params=_cparams(("arbitrary",)),
        name="final",
    )(p0, p1, p0, p1, x1, rt, mod_all, l2g, l2b, ys)


def _reorder_q_heads(w, axis):
    shape = w.shape
    split = shape[:axis] + (N_KV_HEADS, N_Q_HEADS // N_KV_HEADS, HD) + shape[axis + 1:]
    return jnp.swapaxes(w.reshape(split), axis, axis + 1).reshape(shape)


def _rope_tables(seq):
    t = np.arange(seq)
    half = HD // 4
    inv = (ROPE_THETA ** (-np.arange(half, dtype=np.float64) / half)).astype(np.float32)
    d64 = np.arange(LANES) % HD
    pos = np.where((d64 < HD // 2)[None, :], (t // GRID_W)[:, None], (t % GRID_W)[:, None])
    ang = (pos.astype(np.float32) * inv[d64 % half][None, :]).astype(np.float64)
    sign = np.where((d64 % 32) < 16, -1.0, 1.0)
    return (jnp.asarray(np.cos(ang), F32), jnp.asarray(np.sin(ang) * sign[None, :], F32))


def _pair_states(s):
    b = s.shape[0]
    s = s.reshape(b, 2, 2, GLA_DK, GLA_DV)
    return s.transpose(0, 1, 4, 2, 3).reshape(b, 2, GLA_DV, 2 * GLA_DK)


def _unpair_states(s):
    b = s.shape[0]
    s = s.reshape(b, 2, GLA_DV, 2, GLA_DK)
    return s.transpose(0, 1, 3, 4, 2).reshape(b, GLA_H, GLA_DK, GLA_DV)


def _route_tables(ert, counts):
    i32 = jnp.int32
    cnt = counts[0, :N_EXP].astype(i32)
    starts = jnp.cumsum(cnt) - cnt
    table = lambda e: jnp.sum(jnp.where(e[None, :] == jnp.arange(N_EXP, dtype=i32)[:, None],
                                        starts[:, None], 0), axis=0)
    pos0 = table(ert[0].astype(i32)) + ert[4].astype(i32)
    pos1 = table(ert[1].astype(i32)) + ert[5].astype(i32)
    return pos0, pos1, starts, (cnt + (TM_EXP - 1)) // TM_EXP


def kernel(x_prompt, x_sample, cache_k, cache_v, state_gla_fwd, state_gla_bwd, c, c_ctx, w_ada, b_ada, w_in, q_norm, k_norm, gla_w_gate, gla_b_gate, gla_norm, w_br_attn, w_br_gla, w_out, ln1_g, ln1_b, router_group_w, router_group_b, router_expert_w, router_expert_b, exp_w_gate, exp_w_up, exp_w_down, ln2_g, ln2_b):
    b_ctx, seq_ctx, _ = x_prompt.shape
    b_lat, seq_lat, _ = x_sample.shape
    t_ctx, t_lat = b_ctx * seq_ctx, b_lat * seq_lat
    t = t_ctx + t_lat
    l = 0

    rows = -(-(b_lat + 1) // 8) * 8
    c_rows = jnp.zeros((rows, D), F32).at[:b_lat].set(c).at[b_lat].set(c_ctx)
    mod = _ada(c_rows, w_ada[l], b_ada[l][None, :])
    mod_all = mod[:b_lat + 1, None, :]
    mod_lat = mod_all[:b_lat]
    mod_ctx = mod_all[b_lat:]

    w_full = w_in[l]
    w_a = jnp.concatenate([_reorder_q_heads(w_full[:, :AW], 1), w_full[:, AW:A_WIDTH]], axis=1).astype(BF16)
    w_mg = w_full[:, A_WIDTH:].astype(BF16)
    gain = jnp.concatenate([jnp.tile(q_norm[l], N_Q_HEADS), jnp.tile(k_norm[l], N_KV_HEADS)])[None, :]
    head_of = np.arange(AW + KVW) // HD
    ind = jnp.asarray((head_of[:, None] == np.arange(LANES)[None, :]) / HD, BF16)
    w_ba = _reorder_q_heads(w_br_attn[l], 0).astype(BF16)
    w_bg = w_br_gla[l].astype(BF16)
    w_o = w_out[l].astype(BF16)
    w_r = jnp.zeros((D, LANES), F32).at[:, :N_GROUPS].set(router_group_w[l])
    w_r = w_r.at[:, N_GROUPS:N_GROUPS + N_EXP].set(router_expert_w[l]).astype(BF16)
    b_r = jnp.zeros((1, LANES), F32).at[0, :N_GROUPS].set(router_group_b[l])
    b_r = b_r.at[0, N_GROUPS:N_GROUPS + N_EXP].set(router_expert_b[l])
    wg = jnp.zeros((2, 2 * GATE_RANK, GKW), F32)
    wg = wg.at[0, :GATE_RANK].set(gla_w_gate[l, 0]).at[1, GATE_RANK:].set(gla_w_gate[l, 1])
    bg = gla_b_gate[l][:, None, :]
    gn = gla_norm[l][None, :]

    xc = x_prompt.reshape(t_ctx, D)
    xl = x_sample.reshape(t_lat, D)

    q_c, k_c, v_c, qg_c, kg_c, vg_c, rs_c, lr_c, kf_c, vf_c = _inproj(
        xc, mod_ctx, w_a, gain, ind, None, seq_ctx, latent=False)
    attn_c = _attention(q_c, k_c, v_c, None, seq_ctx)
    zero_state = jnp.zeros((1, 2, 2, LANES, LANES), F32)
    gla_c, sfin_c = _gla(qg_c, kg_c, vg_c, lr_c, rs_c, zero_state, wg, bg, gn, seq_ctx)

    q_l, k_l, v_l, qg_l, kg_l, vg_l, rs_l, lr_l = _inproj(
        xl, mod_lat, w_a, gain, ind, _rope_tables(seq_lat), seq_lat, latent=True)
    past = cache_k.shape[2]
    kc = cache_k[:, l].reshape(b_lat, past, KVW).astype(BF16)
    vc = cache_v[:, l].reshape(b_lat, past, KVW).astype(BF16)
    attn_l = _attention(q_l, k_l, v_l, (kc, vc), seq_lat)
    s0 = jnp.stack([_pair_states(state_gla_fwd[:, l]), _pair_states(state_gla_bwd[:, l])], axis=1)
    gla_l, _ = _gla(qg_l, kg_l, vg_l, lr_l, rs_l, s0, wg, bg, gn, seq_lat)

    x1, h2p, rt, ert, counts = _post(xc, xl, attn_c, attn_l, gla_c, gla_l, mod_all, seq_lat,
                                w_mg, w_ba, w_bg, w_o, ln1_g[l][None, :], ln1_b[l][None, :], w_r, b_r)

    pos0, pos1, starts, n_win = _route_tables(ert, counts)
    xs = _scatter_rows(h2p, pos0, pos1)
    ys = _experts(xs, starts, n_win, exp_w_gate[l], exp_w_up[l], exp_w_down[l])
    y_ctx, y_lat = _final(x1, rt, pos0, pos1, mod_all, ln2_g[l][None, :], ln2_b[l][None, :], ys,
                          t_ctx, seq_lat)

    untranspose = lambda a: a.reshape(b_ctx, 1, N_KV_HEADS, HD, seq_ctx).transpose(0, 1, 4, 2, 3)
    new_k, new_v = untranspose(kf_c), untranspose(vf_c)
    new_sf = _unpair_states(sfin_c[:, 0])[:, None]
    new_sb = _unpair_states(sfin_c[:, 1])[:, None]
    return (y_ctx.reshape(b_ctx, seq_ctx, D), y_lat.reshape(b_lat, seq_lat, D),
            new_k, new_v, new_sf, new_sb)
```

```python
import functools

import numpy as np
import jax
import jax.numpy as jnp
from jax import lax
from jax.experimental import pallas as pl
from jax.experimental.pallas import tpu as pltpu

F32 = jnp.float32
BF16 = jnp.bfloat16

D = 1024
GRID_W = 64
HD = 64
N_Q_HEADS = 8
N_KV_HEADS = 2
AW = N_Q_HEADS * HD
KVW = N_KV_HEADS * HD
ROPE_THETA = 10000.0
GLA_H = 4
GLA_DK = 64
GLA_DV = 128
GKW = GLA_H * GLA_DK
GVW = GLA_H * GLA_DV
GATE_RANK = 16
GLA_TAU = 16.0
CHUNK = 64
N_GROUPS = 4
EPG = 8
N_EXP = N_GROUPS * EPG
D_EXP = 256
DEPTH = 1
ALPHA = (2.0 * DEPTH) ** 0.25
LN_EPS = 1e-6
RMS_EPS = 1e-6

LANES = 128
A_WIDTH = AW + 2 * KVW + 2 * GKW + 2 * GVW + 2 * GATE_RANK
MG_WIDTH = 2 * D
TM_TOK = 512
ROW_GROUP = 128
PROJ_AHEAD = 1
TM_PROJ = 1024
TM_POST = 512
LOG2E = 1.4426950408889634
ONES_ROWS = 16
TQ_LAT = 256
SCORE_AHEAD = 2
TM_EXP = 256
READ_AHEAD = 3
TS_ROWS = 2048
VMEM_LIMIT = 56 * 1024 * 1024


def _cparams(sem):
    return pltpu.CompilerParams(dimension_semantics=sem, vmem_limit_bytes=VMEM_LIMIT)


def _dot(a, b):
    return jnp.dot(a, b, preferred_element_type=F32)


def _dot_nt(a, b):
    return lax.dot_general(a, b, (((1,), (1,)), ((), ())), preferred_element_type=F32)


def _dot_tn(a, b):
    return lax.dot_general(a, b, (((0,), (0,)), ((), ())), preferred_element_type=F32)


def _ln(x):
    mu = jnp.mean(x, axis=-1, keepdims=True)
    xc = x - mu
    var = jnp.mean(xc * xc, axis=-1, keepdims=True)
    return xc * lax.rsqrt(var + LN_EPS)


def _silu(x):
    return x * jax.nn.sigmoid(x)


def _split_bf16(x):
    hi = x.astype(BF16)
    lo = (x - hi.astype(F32)).astype(BF16)
    return hi, lo


def _ada_kernel(c_ref, w_ref, b_ref, o_ref):
    rows = c_ref.shape[0]
    s_hi, s_mid, s_lo = _split3_bf16(_silu(c_ref[...]))
    w_hi, w_lo = _split_bf16(w_ref[...])
    a = _dot(jnp.concatenate([s_hi, s_mid, s_lo], axis=0), w_hi)
    b = _dot(jnp.concatenate([s_hi, s_mid], axis=0), w_lo)
    o_ref[...] = (a[0:rows] + a[rows:2 * rows] + a[2 * rows:3 * rows]
                  + b[0:rows] + b[rows:2 * rows] + b_ref[...])


def _ada(c_rows, w_ada, b_ada):
    rows = c_rows.shape[0]
    n = w_ada.shape[1]
    bn = 1024
    return pl.pallas_call(
        _ada_kernel,
        grid=(n // bn,),
        in_specs=[pl.BlockSpec((rows, D), lambda j: (0, 0)),
                  pl.BlockSpec((D, bn), lambda j: (0, j)),
                  pl.BlockSpec((1, bn), lambda j: (0, j))],
        out_specs=pl.BlockSpec((rows, bn), lambda j: (0, j)),
        out_shape=jax.ShapeDtypeStruct((rows, n), F32),
        compiler_params=_cparams(("arbitrary",)),
        name="ada",
    )(c_rows, w_ada, b_ada)


def _inproj_kernel(*refs, latent, seq):
    if latent:
        (x_ref, mod_ref, w_ref, gain_ref, ind_ref, cos_ref, sin_ref,
         q_ref, k_ref, v_ref, qg_ref, kg_ref, vg_ref, rs_ref, lr_ref) = refs
    else:
        (x_ref, mod_ref, w_ref, gain_ref, ind_ref,
         q_ref, k_ref, v_ref, qg_ref, kg_ref, vg_ref, rs_ref, lr_ref, kf_ref, vf_ref) = refs
    tm = x_ref.shape[0]
    sub = ROW_GROUP
    n_groups = tm // sub
    sh1 = mod_ref[0, :, 0:D]
    sc1 = mod_ref[0, :, D:2 * D]
    lane = lax.broadcasted_iota(jnp.int32, (sub, LANES), 1)
    low_half = lane < HD
    first = (lane % 32) < 16

    def project(g):
        rows = slice(g * sub, (g + 1) * sub)
        h = (_ln(x_ref[rows, :]) * (1.0 + sc1) + sh1).astype(BF16)
        return _dot(h, w_ref[...])

    def finish(g, res):
        rows = slice(g * sub, (g + 1) * sub)
        qk = res[:, 0:AW + KVW]
        hi, lo = _split_bf16(qk * qk)
        ms = _dot(hi, ind_ref[...]) + _dot(lo, ind_ref[...])
        r = lax.rsqrt(ms + RMS_EPS)
        for s in range(5):
            rb = jnp.where(low_half, r[:, 2 * s:2 * s + 1], r[:, 2 * s + 1:2 * s + 2])
            y = res[:, LANES * s:LANES * (s + 1)] * rb * gain_ref[:, LANES * s:LANES * (s + 1)]
            if s == 4 and not latent:
                kf_ref[(g * sub) // seq, :, (g * sub) % seq:(g * sub) % seq + sub] = y.T
            if latent:
                partner = jnp.where(first, pltpu.roll(y, LANES - 16, 1), pltpu.roll(y, 16, 1))
                y = y * cos_ref[rows, :] + partner * sin_ref[rows, :]
            if s < 4:
                q_ref[rows, LANES * s:LANES * (s + 1)] = (y * (HD ** -0.5 * LOG2E)).astype(BF16)
            else:
                k_ref[rows, :] = y.astype(BF16)
        o = AW + KVW
        v = res[:, o:o + KVW]
        v_ref[rows, :] = v.astype(BF16)
        if not latent:
            vf_ref[(g * sub) // seq, :, (g * sub) % seq:(g * sub) % seq + sub] = v.T
        o += KVW
        qg_ref[rows, :] = res[:, o:o + GKW] * (GLA_DK ** -0.5)
        o += GKW
        kg_ref[rows, :] = res[:, o:o + GKW]
        o += GKW
        vg_ref[rows, :] = res[:, o:o + GVW].astype(BF16)
        o += GVW
        rs_ref[rows, :] = _silu(res[:, o:o + GVW]).astype(BF16)
        o += GVW
        lr_ref[rows, :] = res[:, o:o + 2 * GATE_RANK]

    ahead = min(PROJ_AHEAD, n_groups)
    pending = {g: project(g) for g in range(ahead)}
    for g in range(n_groups):
        if g + ahead < n_groups:
            pending[g + ahead] = project(g + ahead)
        finish(g, pending.pop(g))


def _inproj(x2, mod, w_a, gain, ind, rope, seq, latent):
    t = x2.shape[0]
    tm = TM_PROJ
    per_seq = max(seq // tm, 1)
    per_tile = max(tm // seq, 1)
    row = lambda i: (i, 0)
    const = lambda i: (0, 0)
    in_specs = [pl.BlockSpec((tm, D), row),
                pl.BlockSpec((1, 1, 6 * D), (lambda i: (i // per_seq, 0, 0)) if latent else (lambda i: (0, 0, 0))),
                pl.BlockSpec((D, A_WIDTH), const),
                pl.BlockSpec((1, AW + KVW), const),
                pl.BlockSpec((AW + KVW, LANES), const)]
    args = [x2, mod, w_a, gain, ind]
    if latent:
        in_specs += [pl.BlockSpec((tm, LANES), lambda i: (i % per_seq, 0))] * 2
        args += list(rope)
    widths = [(AW, BF16), (KVW, BF16), (KVW, BF16), (GKW, F32), (GKW, F32), (GVW, BF16), (GVW, BF16),
              (2 * GATE_RANK, F32)]
    out_specs = [pl.BlockSpec((tm, w), row) for w, _ in widths]
    out_shape = [jax.ShapeDtypeStruct((t, w), dt) for w, dt in widths]
    if not latent:
        cache_spec = pl.BlockSpec((per_tile, KVW, seq), lambda i: (i, 0, 0))
        out_specs += [cache_spec] * 2
        out_shape += [jax.ShapeDtypeStruct((t // seq, KVW, seq), F32)] * 2
    return pl.pallas_call(
        functools.partial(_inproj_kernel, latent=latent, seq=seq),
        grid=(t // tm,),
        in_specs=in_specs,
        out_specs=out_specs,
        out_shape=out_shape,
        compiler_params=_cparams(("parallel",)),
        name="inproj_lat" if latent else "inproj_ctx",
    )(*args)


def _attn_kernel(*refs, has_cache):
    def transposed_with_ones(dst, src):
        dst[0:KVW, :] = src[...].astype(F32).T.astype(BF16)
        dst[KVW:, :] = jnp.ones((ONES_ROWS, dst.shape[1]), BF16)

    if has_cache:
        q_ref, k_ref, v_ref, kc_ref, vc_ref, o_ref, vt_s, vct_s = refs

        @pl.when(pl.program_id(1) == 0)
        def _():
            transposed_with_ones(vt_s, v_ref)
            transposed_with_ones(vct_s, vc_ref)
    else:
        q_ref, k_ref, v_ref, o_ref, vt_s = refs
        transposed_with_ones(vt_s, v_ref)
    tq = q_ref.shape[0]

    lane = lax.broadcasted_iota(jnp.int32, (tq, LANES), 1)
    low_half = lane < HD
    k = k_ref[...]
    def score(c):
        j, pair = divmod(c, 2)
        keep = low_half if j == 0 else jnp.logical_not(low_half)
        zero = jnp.zeros((tq, LANES), BF16)
        qs = jnp.concatenate([jnp.where(keep, q_ref[:, LANES * s:LANES * (s + 1)], zero)
                              for s in (2 * pair, 2 * pair + 1)], axis=0)
        return _dot_nt(k, qs), (_dot_nt(kc_ref[...], qs) if has_cache else None)

    n_chains = 2 * N_KV_HEADS
    ahead = SCORE_AHEAD if has_cache else n_chains
    scores = {c: score(c) for c in range(ahead)}
    outs = []
    for c in range(n_chains):
        if c + ahead < n_chains:
            scores[c + ahead] = score(c + ahead)
        s1, s2 = scores.pop(c)
        m = jnp.max(s1, axis=0, keepdims=True)
        if has_cache:
            m = jnp.maximum(m, jnp.max(s2, axis=0, keepdims=True))
        acc = _dot(vt_s[...], jnp.exp2(s1 - m).astype(BF16))
        if has_cache:
            acc = acc + _dot(vct_s[...], jnp.exp2(s2 - m).astype(BF16))
        outs.append(acc[0:KVW] / acc[KVW:KVW + 1])
    head0 = jnp.concatenate(outs[0:2], axis=1)
    head1 = jnp.concatenate(outs[2:4], axis=1)
    row = lax.broadcasted_iota(jnp.int32, (LANES, 4 * tq), 0)
    out = jnp.where(row < HD, head0, head1).T
    for s in range(4):
        o_ref[:, LANES * s:LANES * (s + 1)] = out[s * tq:(s + 1) * tq].astype(BF16)


def _attention(q, k, v, cache, seq):
    t = q.shape[0]
    if cache is None:
        tq = seq
        grid = (t // seq,)
        qmap = lambda b: (b, 0)
        in_specs = [pl.BlockSpec((tq, AW), qmap), pl.BlockSpec((seq, KVW), qmap),
                    pl.BlockSpec((seq, KVW), qmap)]
        args = [q, k, v]
        scratch = [pltpu.VMEM((KVW + ONES_ROWS, seq), BF16)]
        sem = ("parallel",)
        name = "attn_ctx"
    else:
        tq = TQ_LAT
        nq = seq // tq
        kc, vc = cache
        past = kc.shape[1]
        grid = (t // seq, nq)
        qmap = lambda b, i: (b * nq + i, 0)
        kmap = lambda b, i: (b, 0)
        cmap = lambda b, i: (b, 0, 0)
        in_specs = [pl.BlockSpec((tq, AW), qmap), pl.BlockSpec((seq, KVW), kmap),
                    pl.BlockSpec((seq, KVW), kmap),
                    pl.BlockSpec((None, past, KVW), cmap), pl.BlockSpec((None, past, KVW), cmap)]
        args = [q, k, v, kc, vc]
        scratch = [pltpu.VMEM((KVW + ONES_ROWS, seq), BF16), pltpu.VMEM((KVW + ONES_ROWS, past), BF16)]
        sem = ("parallel", "arbitrary")
        name = "attn_lat"
    return pl.pallas_call(
        functools.partial(_attn_kernel, has_cache=cache is not None),
        grid=grid,
        in_specs=in_specs,
        out_specs=pl.BlockSpec((tq, AW), qmap),
        out_shape=jax.ShapeDtypeStruct((t, AW), BF16),
        scratch_shapes=scratch,
        compiler_params=_cparams(sem),
        name=name,
    )(*args)


GLA_BLK = 256
GLA_ROWS = 1024
GLA_UNROLL = 8


def _split3_bf16(x):
    hi = x.astype(BF16)
    r1 = x - hi.astype(F32)
    mid = r1.astype(BF16)
    lo = (r1 - mid.astype(F32)).astype(BF16)
    return hi, mid, lo


def _gla_kernel(qg_ref, kg_ref, vg_ref, lr_ref, rs_ref, s0_ref, wg_ref, bg_ref, gn_ref,
                o_ref, sfin_ref, cum_s, kv_s, dec_s, *, seqs):
    n = qg_ref.shape[0]
    nc = n // CHUNK
    nc_seq = nc // seqs
    unroll = min(GLA_UNROLL, nc)
    lane = lax.broadcasted_iota(jnp.int32, (CHUNK, LANES), 1)
    low_half = lane < GLA_DK
    lane_sq = lax.broadcasted_iota(jnp.int32, (LANES, LANES), 1)
    low_half_sq = lane_sq < GLA_DK
    ri = lax.broadcasted_iota(jnp.int32, (2 * CHUNK, 2 * CHUNK), 0)
    ci = lax.broadcasted_iota(jnp.int32, (2 * CHUNK, 2 * CHUNK), 1)
    diag = (ri >> 6) == (ci >> 6)
    keep2 = (diag & (ri >= ci), diag & (ci >= ri))

    rb = lax.broadcasted_iota(jnp.int32, (GLA_BLK, GLA_BLK), 0)
    cb = lax.broadcasted_iota(jnp.int32, (GLA_BLK, GLA_BLK), 1)
    same = (rb >> 6) == (cb >> 6)
    tri = ((same & (rb >= cb)).astype(BF16), (same & (cb >= rb)).astype(BF16))
    for d in range(2):
        w_hi, w_lo = _split_bf16(wg_ref[d])
        for blk in range(n // GLA_BLK):
            rows = slice(blk * GLA_BLK, (blk + 1) * GLA_BLK)
            l_hi, l_lo = _split_bf16(lr_ref[rows, :])
            z = _dot(l_hi, w_hi) + _dot(l_lo, w_hi) + _dot(l_hi, w_lo) + bg_ref[d]
            logg = (jnp.minimum(z, 0.0) - jnp.log(1.0 + jnp.exp(-jnp.abs(z)))) * (1.0 / GLA_TAU)
            pieces = _split3_bf16(logg)
            cum_s[d, rows, :] = _dot(tri[d], pieces[0]) + _dot(tri[d], pieces[1]) + _dot(tri[d], pieces[2])

    def pass1(i, carry):
        cs = [i * unroll + u for u in range(unroll)]
        rws = [pl.ds(pl.multiple_of(c * CHUNK, CHUNK), CHUNK) for c in cs]
        prods = []
        for c, rows in zip(cs, rws):
            kc = kg_ref[rows, :]
            kdec = []
            for d in range(2):
                cum = cum_s[d, rows, :]
                last = cum[CHUNK - 1:CHUNK, :] if d == 0 else cum[0:1, :]
                dec_s[d, c] = jnp.exp(last)
                kdec.append((kc * jnp.exp(last - cum)).astype(BF16))
            for p in range(2):
                vpair = vg_ref[rows, GLA_DV * 2 * p:GLA_DV * 2 * (p + 1)]
                kpair = jnp.concatenate([kdec[0][:, LANES * p:LANES * (p + 1)],
                                         kdec[1][:, LANES * p:LANES * (p + 1)]], axis=1)
                prods.append(_dot_tn(vpair, kpair))
        for j, c in enumerate(cs):
            for p in range(2):
                res = prods[2 * j + p]
                for d in range(2):
                    cols = slice(LANES * d, LANES * (d + 1))
                    kv_s[d, c, p] = jnp.where(low_half_sq, res[0:GLA_DV, cols], res[GLA_DV:2 * GLA_DV, cols])
        return carry

    lax.fori_loop(0, nc // unroll, pass1, 0)

    for s in range(seqs):
        for d in range(2):
            def scan(i, st, s=s, d=d):
                c = s * nc_seq + (i if d == 0 else nc_seq - 1 - i)
                dec = dec_s[d, c]
                new = []
                for p in range(2):
                    kv = kv_s[d, c, p]
                    kv_s[d, c, p] = st[p]
                    new.append(st[p] * dec[:, LANES * p:LANES * (p + 1)] + kv)
                return tuple(new)

            s_in = s if s0_ref.shape[0] > 1 else 0
            fin = lax.fori_loop(0, nc_seq, scan, (s0_ref[s_in, d, 0], s0_ref[s_in, d, 1]))
            sfin_ref[s, d, 0] = fin[0]
            sfin_ref[s, d, 1] = fin[1]

    def pass3(i, carry):
        cs = [i * unroll + u for u in range(unroll)]
        rws = [pl.ds(pl.multiple_of(c * CHUNK, CHUNK), CHUNK) for c in cs]
        first = []
        for c, rows in zip(cs, rws):
            q = qg_ref[rows, :]
            k = kg_ref[rows, :]
            for d in range(2):
                cum = cum_s[d, rows, :]
                qt = q * jnp.exp(cum)
                kt = (k * jnp.exp(-cum)).astype(BF16)
                for p in range(2):
                    qs = qt[:, LANES * p:LANES * (p + 1)]
                    lhs = jnp.concatenate([jnp.where(low_half, qs, 0.0), jnp.where(low_half, 0.0, qs)],
                                          axis=0).astype(BF16)
                    kts = kt[:, LANES * p:LANES * (p + 1)]
                    rhs = jnp.concatenate([kts, kts, kv_s[d, c, p].astype(BF16)], axis=0)
                    first.append(_dot_nt(lhs, rhs))
        second = []
        for j, rows in enumerate(rws):
            for d in range(2):
                for p in range(2):
                    res = first[4 * j + 2 * d + p]
                    vp = jnp.concatenate([vg_ref[rows, GLA_DV * (2 * p):GLA_DV * (2 * p + 1)],
                                          vg_ref[rows, GLA_DV * (2 * p + 1):GLA_DV * (2 * p + 2)]], axis=0)
                    a = jnp.where(keep2[d], res[:, 0:2 * CHUNK], 0.0).astype(BF16)
                    second.append(_dot(a, vp) + res[:, 2 * CHUNK:])
        for j, rows in enumerate(rws):
            for p in range(2):
                tot = second[4 * j + p] + second[4 * j + 2 + p]
                y = tot * lax.rsqrt(jnp.mean(tot * tot, axis=-1, keepdims=True) + RMS_EPS) * gn_ref[...]
                for hh in range(2):
                    cols = slice(GLA_DV * (2 * p + hh), GLA_DV * (2 * p + hh + 1))
                    o_ref[rows, cols] = (y[CHUNK * hh:CHUNK * (hh + 1)]
                                         * rs_ref[rows, cols].astype(F32)).astype(BF16)
        return carry

    lax.fori_loop(0, nc // unroll, pass3, 0)


def _gla(qg, kg, vg, lr, rs, s0, wg, bg, gn, seq):
    t = qg.shape[0]
    nb = t // seq
    seqs = max(GLA_ROWS // seq, 1)
    rows = seqs * seq
    nc = rows // CHUNK
    row = lambda b: (b, 0)
    c3 = lambda b: (0, 0, 0)
    st = lambda b: (b, 0, 0, 0, 0)
    if s0.shape[0] == 1:
        s0_spec = pl.BlockSpec((1, 2, 2, LANES, LANES), lambda b: (0, 0, 0, 0, 0))
    else:
        s0_spec = pl.BlockSpec((seqs, 2, 2, LANES, LANES), st)
    return pl.pallas_call(
        functools.partial(_gla_kernel, seqs=seqs),
        grid=(nb // seqs,),
        in_specs=[pl.BlockSpec((rows, GKW), row), pl.BlockSpec((rows, GKW), row),
                  pl.BlockSpec((rows, GVW), row), pl.BlockSpec((rows, 2 * GATE_RANK), row),
                  pl.BlockSpec((rows, GVW), row),
                  s0_spec,
                  pl.BlockSpec((2, 2 * GATE_RANK, GKW), c3), pl.BlockSpec((2, 1, GKW), c3),
                  pl.BlockSpec((1, GLA_DV), lambda b: (0, 0))],
        out_specs=[pl.BlockSpec((rows, GVW), row),
                   pl.BlockSpec((seqs, 2, 2, LANES, LANES), st)],
        out_shape=[jax.ShapeDtypeStruct((t, GVW), BF16),
                   jax.ShapeDtypeStruct((nb, 2, 2, LANES, LANES), F32)],
        scratch_shapes=[pltpu.VMEM((2, rows, GKW), F32),
                        pltpu.VMEM((2, nc, 2, LANES, LANES), F32),
                        pltpu.VMEM((2, nc, 1, GKW), F32)],
        compiler_params=_cparams(("parallel",)),
        name="gla_lat" if seq > 256 else "gla_ctx",
    )(qg, kg, vg, lr, rs, s0, wg, bg, gn)


ROW_SUB = D // LANES
ROW_DTYPE = BF16


def _store_row_slabs(ref, x, tmp):
    m = x.shape[0]
    for c in range(ROW_SUB):
        tmp[pl.ds(c, m, stride=ROW_SUB), :] = x[:, LANES * c:LANES * (c + 1)]
    ref[...] = tmp[...].reshape(m, ROW_SUB, LANES).astype(ROW_DTYPE)


def _load_row_slabs(ref, tmp):
    m = ref.shape[0]
    tmp[...] = ref[...].astype(F32).reshape(m * ROW_SUB, LANES)
    return jnp.concatenate([tmp[pl.ds(c, m, stride=ROW_SUB), :] for c in range(ROW_SUB)], axis=1)


def _row_slab(ref, row):
    return ref.at[pl.ds(row, 1)]


def _post_kernel(xc_ref, xl_ref, ac_ref, al_ref, gc_ref, gl_ref, mod_ref,
                 wmg_ref, wba_ref, wbg_ref, wo_ref, l1g_ref, l1b_ref, wr_ref, br_ref,
                 x1_ref, h2_ref, rt_ref, ert_ref, cnt_ref, run_s, slab_s, logit_s, *, n_ctx_tiles):
    i = pl.program_id(0)
    tm = xc_ref.shape[0]
    sub = ROW_GROUP
    n_groups = tm // sub
    is_ctx = i < n_ctx_tiles
    parts = [slice(g * sub, (g + 1) * sub) for g in range(n_groups)]

    @pl.when(i == 0)
    def _():
        run_s[...] = jnp.zeros_like(run_s)
        logit_s[...] = jnp.zeros_like(logit_s)

    sh1 = mod_ref[0, :, 0:D]
    sc1 = mod_ref[0, :, D:2 * D]
    g1 = mod_ref[0, :, 2 * D:3 * D]
    sh2 = mod_ref[0, :, 3 * D:4 * D]
    sc2 = mod_ref[0, :, 4 * D:5 * D]

    xs = [jnp.where(is_ctx, xc_ref[r, :], xl_ref[r, :]) for r in parts]
    hs = [(_ln(x) * (1.0 + sc1) + sh1).astype(BF16) for x in xs]
    gates = [jax.nn.sigmoid(_dot(h, wmg_ref[...])) for h in hs]
    ba = [_dot(jnp.where(is_ctx, ac_ref[r, :], al_ref[r, :]), wba_ref[...]) for r in parts]
    bg = [_dot(jnp.where(is_ctx, gc_ref[r, :], gl_ref[r, :]), wbg_ref[...]) for r in parts]

    ri = lax.broadcasted_iota(jnp.int32, (sub, sub), 0)
    ci = lax.broadcasted_iota(jnp.int32, (sub, sub), 1)
    earlier = (ri > ci).astype(BF16)
    counted = (i > 0).astype(F32)
    run = run_s[0:1, :]
    for r in parts:
        run = _route(logit_s[r, :], earlier, run, counted, rt_ref.at[r, :], ert_ref.at[:, r])
    run_s[0:1, :] = run
    cnt_ref[...] = jnp.broadcast_to(run, cnt_ref.shape)

    merged = [(g[:, :D] * a + g[:, D:] * b).astype(BF16) for g, a, b in zip(gates, ba, bg)]
    mix = [_dot(m, wo_ref[...]) for m in merged]
    x1s = [_ln(ALPHA * x + g1 * m) * l1g_ref[...] + l1b_ref[...] for x, m in zip(xs, mix)]
    h2s = [_ln(x1) * (1.0 + sc2) + sh2 for x1 in x1s]
    logits = [_dot(h2.astype(BF16), wr_ref[...]) + br_ref[...] for h2 in h2s]
    for g, r in enumerate(parts):
        x1_ref[r, :] = x1s[g]
        _store_row_slabs(h2_ref.at[pl.ds(g * sub, sub)], h2s[g], slab_s)
        logit_s[r, :] = logits[g]


def _route(logit, earlier, run, counted, rt_ref, ert_ref):
    tm = logit.shape[0]
    lane_i = lax.broadcasted_iota(jnp.int32, (tm, LANES), 1)
    lane = lane_i.astype(F32)
    lane_grp = ((lane_i - N_GROUPS) >> 3).astype(F32)
    neg = jnp.float32(-jnp.inf)
    far = jnp.float32(LANES)
    is_g = lane_i < N_GROUPS
    lg = jnp.where(is_g, logit, neg)
    mg = jnp.max(lg, axis=-1, keepdims=True)
    pg_top = 1.0 / jnp.sum(jnp.where(is_g, jnp.exp(logit - mg), 0.0), axis=-1, keepdims=True)
    g_idx = jnp.min(jnp.where(lg == mg, lane, far), axis=-1, keepdims=True)
    in_grp = (lane_i >= N_GROUPS) & (lane_i < N_GROUPS + N_EXP) & (lane_grp == g_idx)
    le = jnp.where(in_grp, logit, neg)
    v1 = jnp.max(le, axis=-1, keepdims=True)
    i1 = jnp.min(jnp.where(le == v1, lane, far), axis=-1, keepdims=True)
    le2 = jnp.where(lane == i1, neg, le)
    v2 = jnp.max(le2, axis=-1, keepdims=True)
    i2 = jnp.min(jnp.where(le2 == v2, lane, far), axis=-1, keepdims=True)
    e1 = i1 - N_GROUPS
    e2 = i2 - N_GROUPS
    tt = jnp.exp(v2 - v1)
    w1 = pg_top / (1.0 + tt)
    w2 = pg_top * tt / (1.0 + tt)

    hot = ((lane == e1) | (lane == e2)).astype(F32)
    before = _dot(earlier, hot.astype(BF16)) + run
    r1 = jnp.sum(jnp.where(lane == e1, before, 0.0), axis=-1, keepdims=True)
    r2 = jnp.sum(jnp.where(lane == e2, before, 0.0), axis=-1, keepdims=True)

    rt = jnp.where(lane_i == 0, e1, 0.0)
    rt = jnp.where(lane_i == 1, e2, rt)
    rt = jnp.where(lane_i == 2, w1, rt)
    rt = jnp.where(lane_i == 3, w2, rt)
    rt = jnp.where(lane_i == 4, r1, rt)
    rt = jnp.where(lane_i == 5, r2, rt)
    rt_ref[...] = rt
    ert_ref[...] = rt.T[0:8, :]
    return run + counted * jnp.sum(hot, axis=0, keepdims=True)


def _post(x_ctx, x_lat, a_ctx, a_lat, g_ctx, g_lat, mod_all, seq_lat,
          w_mg, w_ba, w_bg, w_o, l1g, l1b, w_r, b_r):
    t_ctx, t_lat = x_ctx.shape[0], x_lat.shape[0]
    tm = TM_POST
    nct, nlt = t_ctx // tm, t_lat // tm
    per_seq = seq_lat // tm
    nb_lat = t_lat // seq_lat
    t = t_ctx + t_lat
    n = nct + nlt
    lat = lambda i: jnp.clip(i - nct, 0, nlt - 1)
    cmap = lambda i: (jnp.minimum(i, nct - 1), 0)
    lmap = lambda i: (lat(i), 0)
    mmap = lambda i: (jnp.where(i < nct, nb_lat, lat(i) // per_seq), 0, 0)
    row = lambda i: (jnp.minimum(i, n - 1), 0)
    prev = lambda i: jnp.maximum(i - 1, 0)
    const = lambda i: (0, 0)
    return pl.pallas_call(
        functools.partial(_post_kernel, n_ctx_tiles=nct),
        grid=(n + 1,),
        in_specs=[pl.BlockSpec((tm, D), cmap), pl.BlockSpec((tm, D), lmap),
                  pl.BlockSpec((tm, AW), cmap), pl.BlockSpec((tm, AW), lmap),
                  pl.BlockSpec((tm, GVW), cmap), pl.BlockSpec((tm, GVW), lmap),
                  pl.BlockSpec((1, 1, 6 * D), mmap),
                  pl.BlockSpec((D, MG_WIDTH), const), pl.BlockSpec((AW, D), const),
                  pl.BlockSpec((GVW, D), const), pl.BlockSpec((D, D), const),
                  pl.BlockSpec((1, D), const), pl.BlockSpec((1, D), const),
                  pl.BlockSpec((D, LANES), const), pl.BlockSpec((1, LANES), const)],
        out_specs=[pl.BlockSpec((tm, D), row),
                   pl.BlockSpec((tm, ROW_SUB, LANES), lambda i: (jnp.minimum(i, n - 1), 0, 0)),
                   pl.BlockSpec((tm, LANES), lambda i: (prev(i), 0)),
                   pl.BlockSpec((8, tm), lambda i: (0, prev(i))),
                   pl.BlockSpec((8, LANES), const)],
        out_shape=[jax.ShapeDtypeStruct((t, D), F32), jax.ShapeDtypeStruct((t, ROW_SUB, LANES), ROW_DTYPE),
                   jax.ShapeDtypeStruct((t, LANES), F32), jax.ShapeDtypeStruct((8, t), F32),
                   jax.ShapeDtypeStruct((8, LANES), F32)],
        scratch_shapes=[pltpu.VMEM((8, LANES), F32), pltpu.VMEM((ROW_GROUP * ROW_SUB, LANES), F32),
                        pltpu.VMEM((tm, LANES), F32)],
        compiler_params=_cparams(("arbitrary",)),
        name="post",
    )(x_ctx, x_lat, a_ctx, a_lat, g_ctx, g_lat, mod_all, w_mg, w_ba, w_bg, w_o, l1g, l1b, w_r, b_r)


ROW_UNROLL = 8


def _row_copy(src_ref, dst_ref, sem):
    return pltpu.make_async_copy(src_ref, dst_ref, sem)


def _scatter_kernel(pos0_ref, pos1_ref, h_ref, xs_ref, sem, zero_s, zero_sem):
    ts = h_ref.shape[0]
    n_rows = xs_ref.shape[0] - TM_EXP

    @pl.when(pl.program_id(0) == 0)
    def _():
        zero_s[...] = jnp.zeros_like(zero_s)
        pad = _row_copy(zero_s, xs_ref.at[pl.ds(n_rows, TM_EXP)], zero_sem)
        pad.start()
        pad.wait()

    def issue(g, carry):
        r0 = pl.multiple_of(g * ROW_UNROLL, ROW_UNROLL)
        for k in range(ROW_UNROLL):
            src = _row_slab(h_ref, r0 + k)
            _row_copy(src, _row_slab(xs_ref, pos0_ref[0, r0 + k]), sem).start(priority=0)
            _row_copy(src, _row_slab(xs_ref, pos1_ref[0, r0 + k]), sem).start(priority=1)
        return carry

    lax.fori_loop(0, ts // ROW_UNROLL, issue, 0)
    for _ in range(2):
        _row_copy(h_ref, xs_ref.at[pl.ds(0, ts)], sem).wait()


def _scatter_rows(h2p, pos0, pos1):
    t = h2p.shape[0]
    ts = TS_ROWS
    smem = lambda: pl.BlockSpec((None, 1, ts), lambda i: (i, 0, 0), memory_space=pltpu.SMEM)
    return pl.pallas_call(
        _scatter_kernel,
        grid=(t // ts,),
        in_specs=[smem(), smem(), pl.BlockSpec((ts, ROW_SUB, LANES), lambda i: (i, 0, 0))],
        out_specs=pl.BlockSpec(memory_space=pl.ANY),
        out_shape=jax.ShapeDtypeStruct((2 * t + TM_EXP, ROW_SUB, LANES), ROW_DTYPE),
        scratch_shapes=[pltpu.SemaphoreType.DMA(()), pltpu.VMEM((TM_EXP, ROW_SUB, LANES), ROW_DTYPE),
                        pltpu.SemaphoreType.DMA(())],
        compiler_params=_cparams(("arbitrary",)),
        name="scatter",
    )(pos0.reshape(t // ts, 1, ts), pos1.reshape(t // ts, 1, ts), h2p)


def _expert_kernel(start_ref, nwin_ref, xs_ref, wg_ref, wu_ref, wd_ref, ys_ref,
                   wgu_s, wd_s, in_buf, out_buf, slab_s, in_sem, out_sem):
    e = pl.program_id(0)
    tm = TM_EXP
    wgu_s[:, 0:D_EXP] = wg_ref[...].astype(BF16)
    wgu_s[:, D_EXP:2 * D_EXP] = wu_ref[...].astype(BF16)
    wd_s[...] = wd_ref[...].astype(BF16)
    base = start_ref[e]
    n = nwin_ref[e]

    @pl.when(e == 0)
    def _():
        out_buf[0] = jnp.zeros(out_buf.shape[1:], ROW_DTYPE)
        pad = pltpu.make_async_copy(out_buf.at[0], ys_ref.at[pl.ds(ys_ref.shape[0] - tm, tm)], out_sem.at[0])
        pad.start()
        pad.wait()

    def read(w, slot, first_row=None):
        first_row = base if first_row is None else first_row
        return pltpu.make_async_copy(xs_ref.at[pl.ds(first_row + w * tm, tm)], in_buf.at[slot], in_sem.at[slot])

    def write(w, slot):
        return pltpu.make_async_copy(out_buf.at[slot], ys_ref.at[pl.ds(base + w * tm, tm)], out_sem.at[slot])

    def start_first_reads(expert):
        for a in range(READ_AHEAD):
            @pl.when(nwin_ref[expert] > a)
            def _(a=a):
                read(a, a, start_ref[expert]).start()

    @pl.when(e == 0)
    def _():
        start_first_reads(0)

    def body(w, carry):
        slot = w % 2
        rslot = w % (READ_AHEAD + 1)

        @pl.when(w + READ_AHEAD < n)
        def _():
            read(w + READ_AHEAD, (w + READ_AHEAD) % (READ_AHEAD + 1)).start()

        read(w, rslot).wait()

        @pl.when(w >= 2)
        def _():
            write(w - 2, slot).wait()

        gu = _dot(_load_row_slabs(in_buf.at[rslot], slab_s).astype(BF16), wgu_s[...])
        hid = _silu(gu[:, 0:D_EXP]) * gu[:, D_EXP:2 * D_EXP]
        _store_row_slabs(out_buf.at[slot], _dot(hid.astype(BF16), wd_s[...]), slab_s)
        write(w, slot).start()
        return carry

    lax.fori_loop(0, n, body, 0)

    @pl.when(e + 1 < pl.num_programs(0))
    def _():
        start_first_reads(e + 1)

    @pl.when(n >= 1)
    def _():
        write(n - 1, (n - 1) % 2).wait()

    @pl.when(n >= 2)
    def _():
        write(n - 2, n % 2).wait()


def _experts(xs, starts, n_win, w_gate, w_up, w_down):
    tm = TM_EXP
    wmap = lambda e, st, nw: (e, 0, 0)
    slab = (tm, ROW_SUB, LANES)
    return pl.pallas_call(
        _expert_kernel,
        grid_spec=pltpu.PrefetchScalarGridSpec(
            num_scalar_prefetch=2,
            grid=(N_EXP,),
            in_specs=[pl.BlockSpec(memory_space=pl.ANY),
                      pl.BlockSpec((None, D, D_EXP), wmap), pl.BlockSpec((None, D, D_EXP), wmap),
                      pl.BlockSpec((None, D_EXP, D), wmap)],
            out_specs=pl.BlockSpec(memory_space=pl.ANY),
            scratch_shapes=[pltpu.VMEM((D, 2 * D_EXP), BF16), pltpu.VMEM((D_EXP, D), BF16),
                            pltpu.VMEM((READ_AHEAD + 1,) + slab, ROW_DTYPE), pltpu.VMEM((2,) + slab, ROW_DTYPE),
                            pltpu.VMEM((tm * ROW_SUB, LANES), F32),
                            pltpu.SemaphoreType.DMA((READ_AHEAD + 1,)), pltpu.SemaphoreType.DMA((2,))]),
        out_shape=jax.ShapeDtypeStruct(xs.shape, ROW_DTYPE),
        compiler_params=_cparams(("arbitrary",)),
        name="experts",
    )(starts, n_win, xs, w_gate, w_up, w_down)


def _final_kernel(p0c_ref, p1c_ref, p0n_ref, p1n_ref, x1_ref, rt_ref, mod_ref, l2g_ref, l2b_ref, ys_ref,
                  oc_ref, ol_ref, buf, sem, slab_s, *, n_ctx_tiles):
    i = pl.program_id(0)
    n = pl.num_programs(0)
    tm = x1_ref.shape[0]

    def gather(p0_ref, p1_ref, slot):
        def issue(g, carry):
            r0 = pl.multiple_of(g * ROW_UNROLL, ROW_UNROLL)
            for k in range(ROW_UNROLL):
                _row_copy(_row_slab(ys_ref, p0_ref[0, r0 + k]),
                          _row_slab(buf.at[slot, 0], r0 + k), sem.at[slot]).start(priority=0)
                _row_copy(_row_slab(ys_ref, p1_ref[0, r0 + k]),
                          _row_slab(buf.at[slot, 1], r0 + k), sem.at[slot]).start(priority=1)
            return carry

        lax.fori_loop(0, tm // ROW_UNROLL, issue, 0)

    cur = i % 2

    @pl.when(i == 0)
    def _():
        gather(p0c_ref, p1c_ref, 0)

    @pl.when(i + 1 < n)
    def _():
        gather(p0n_ref, p1n_ref, 1 - cur)

    for k in range(2):
        _row_copy(ys_ref.at[pl.ds(0, tm)], buf.at[cur, k], sem.at[cur]).wait()

    g2 = mod_ref[0, :, 5 * D:6 * D]
    w1 = rt_ref[:, 2:3]
    w2 = rt_ref[:, 3:4]
    moe = (w1 * _load_row_slabs(buf.at[cur, 0], slab_s.at[0])
           + w2 * _load_row_slabs(buf.at[cur, 1], slab_s.at[1]))
    out = _ln(ALPHA * x1_ref[...] + g2 * moe) * l2g_ref[...] + l2b_ref[...]

    @pl.when(i < n_ctx_tiles)
    def _():
        oc_ref[...] = out

    @pl.when(i >= n_ctx_tiles)
    def _():
        ol_ref[...] = out


def _final(x1, rt, pos0, pos1, mod_all, l2g, l2b, ys, t_ctx, seq_lat):
    t = x1.shape[0]
    tm = TM_TOK
    nt = t // tm
    nct = t_ctx // tm
    t_lat = t - t_ctx
    per_seq = seq_lat // tm
    nb_lat = t_lat // seq_lat
    p0 = pos0.reshape(nt, 1, tm)
    p1 = pos1.reshape(nt, 1, tm)
    row = lambda i: (i, 0)
    const = lambda i: (0, 0)
    mmap = lambda i: (jnp.where(i < nct, nb_lat, jnp.maximum(i - nct, 0) // per_seq), 0, 0)
    smem_cur = lambda: pl.BlockSpec((None, 1, tm), lambda i: (i, 0, 0), memory_space=pltpu.SMEM)
    smem_nxt = lambda: pl.BlockSpec((None, 1, tm), lambda i: (jnp.minimum(i + 1, nt - 1), 0, 0),
                                    memory_space=pltpu.SMEM)
    return pl.pallas_call(
        functools.partial(_final_kernel, n_ctx_tiles=nct),
        grid=(nt,),
        in_specs=[smem_cur(), smem_cur(), smem_nxt(), smem_nxt(),
                  pl.BlockSpec((tm, D), row), pl.BlockSpec((tm, LANES), row),
                  pl.BlockSpec((1, 1, 6 * D), mmap),
                  pl.BlockSpec((1, D), const), pl.BlockSpec((1, D), const),
                  pl.BlockSpec(memory_space=pl.ANY)],
        out_specs=[pl.BlockSpec((tm, D), lambda i: (jnp.minimum(i, nct - 1), 0)),
                   pl.BlockSpec((tm, D), lambda i: (jnp.maximum(i - nct, 0), 0))],
        out_shape=[jax.ShapeDtypeStruct((t_ctx, D), F32), jax.ShapeDtypeStruct((t_lat, D), F32)],
        scratch_shapes=[pltpu.VMEM((2, 2, tm, ROW_SUB, LANES), ROW_DTYPE), pltpu.SemaphoreType.DMA((2,)),
                        pltpu.VMEM((2, tm * ROW_SUB, LANES), F32)],
        compiler_params=_cparams(("arbitrary",)),
        name="final",
    )(p0, p1, p0, p1, x1, rt, mod_all, l2g, l2b, ys)


def _reorder_q_heads(w, axis):
    shape = w.shape
    split = shape[:axis] + (N_KV_HEADS, N_Q_HEADS // N_KV_HEADS, HD) + shape[axis + 1:]
    return jnp.swapaxes(w.reshape(split), axis, axis + 1).reshape(shape)


def _rope_tables(seq):
    t = np.arange(seq)
    half = HD // 4
    inv = (ROPE_THETA ** (-np.arange(half, dtype=np.float64) / half)).astype(np.float32)
    d64 = np.arange(LANES) % HD
    pos = np.where((d64 < HD // 2)[None, :], (t // GRID_W)[:, None], (t % GRID_W)[:, None])
    ang = (pos.astype(np.float32) * inv[d64 % half][None, :]).astype(np.float64)
    sign = np.where((d64 % 32) < 16, -1.0, 1.0)
    return (jnp.asarray(np.cos(ang), F32), jnp.asarray(np.sin(ang) * sign[None, :], F32))


def _pair_states(s):
    b = s.shape[0]
    s = s.reshape(b, 2, 2, GLA_DK, GLA_DV)
    return s.transpose(0, 1, 4, 2, 3).reshape(b, 2, GLA_DV, 2 * GLA_DK)


def _unpair_states(s):
    b = s.shape[0]
    s = s.reshape(b, 2, GLA_DV, 2, GLA_DK)
    return s.transpose(0, 1, 3, 4, 2).reshape(b, GLA_H, GLA_DK, GLA_DV)


def _route_tables(ert, counts):
    i32 = jnp.int32
    cnt = counts[0, :N_EXP].astype(i32)
    starts = jnp.cumsum(cnt) - cnt
    table = lambda e: jnp.sum(jnp.where(e[None, :] == jnp.arange(N_EXP, dtype=i32)[:, None],
                                        starts[:, None], 0), axis=0)
    pos0 = table(ert[0].astype(i32)) + ert[4].astype(i32)
    pos1 = table(ert[1].astype(i32)) + ert[5].astype(i32)
    return pos0, pos1, starts, (cnt + (TM_EXP - 1)) // TM_EXP


def kernel(x_prompt, x_sample, cache_k, cache_v, state_gla_fwd, state_gla_bwd, c, c_ctx, w_ada, b_ada, w_in, q_norm, k_norm, gla_w_gate, gla_b_gate, gla_norm, w_br_attn, w_br_gla, w_out, ln1_g, ln1_b, router_group_w, router_group_b, router_expert_w, router_expert_b, exp_w_gate, exp_w_up, exp_w_down, ln2_g, ln2_b):
    b_ctx, seq_ctx, _ = x_prompt.shape
    b_lat, seq_lat, _ = x_sample.shape
    t_ctx, t_lat = b_ctx * seq_ctx, b_lat * seq_lat
    t = t_ctx + t_lat
    l = 0

    rows = -(-(b_lat + 1) // 8) * 8
    c_rows = jnp.zeros((rows, D), F32).at[:b_lat].set(c).at[b_lat].set(c_ctx)
    mod = _ada(c_rows, w_ada[l], b_ada[l][None, :])
    mod_all = mod[:b_lat + 1, None, :]
    mod_lat = mod_all[:b_lat]
    mod_ctx = mod_all[b_lat:]

    w_full = w_in[l]
    w_a = jnp.concatenate([_reorder_q_heads(w_full[:, :AW], 1), w_full[:, AW:A_WIDTH]], axis=1).astype(BF16)
    w_mg = w_full[:, A_WIDTH:].astype(BF16)
    gain = jnp.concatenate([jnp.tile(q_norm[l], N_Q_HEADS), jnp.tile(k_norm[l], N_KV_HEADS)])[None, :]
    head_of = np.arange(AW + KVW) // HD
    ind = jnp.asarray((head_of[:, None] == np.arange(LANES)[None, :]) / HD, BF16)
    w_ba = _reorder_q_heads(w_br_attn[l], 0).astype(BF16)
    w_bg = w_br_gla[l].astype(BF16)
    w_o = w_out[l].astype(BF16)
    w_r = jnp.zeros((D, LANES), F32).at[:, :N_GROUPS].set(router_group_w[l])
    w_r = w_r.at[:, N_GROUPS:N_GROUPS + N_EXP].set(router_expert_w[l]).astype(BF16)
    b_r = jnp.zeros((1, LANES), F32).at[0, :N_GROUPS].set(router_group_b[l])
    b_r = b_r.at[0, N_GROUPS:N_GROUPS + N_EXP].set(router_expert_b[l])
    wg = jnp.zeros((2, 2 * GATE_RANK, GKW), F32)
    wg = wg.at[0, :GATE_RANK].set(gla_w_gate[l, 0]).at[1, GATE_RANK:].set(gla_w_gate[l, 1])
    bg = gla_b_gate[l][:, None, :]
    gn = gla_norm[l][None, :]

    xc = x_prompt.reshape(t_ctx, D)
    xl = x_sample.reshape(t_lat, D)

    q_c, k_c, v_c, qg_c, kg_c, vg_c, rs_c, lr_c, kf_c, vf_c = _inproj(
        xc, mod_ctx, w_a, gain, ind, None, seq_ctx, latent=False)
    attn_c = _attention(q_c, k_c, v_c, None, seq_ctx)
    zero_state = jnp.zeros((1, 2, 2, LANES, LANES), F32)
    gla_c, sfin_c = _gla(qg_c, kg_c, vg_c, lr_c, rs_c, zero_state, wg, bg, gn, seq_ctx)

    q_l, k_l, v_l, qg_l, kg_l, vg_l, rs_l, lr_l = _inproj(
        xl, mod_lat, w_a, gain, ind, _rope_tables(seq_lat), seq_lat, latent=True)
    past = cache_k.shape[2]
    kc = cache_k[:, l].reshape(b_lat, past, KVW).astype(BF16)
    vc = cache_v[:, l].reshape(b_lat, past, KVW).astype(BF16)
    attn_l = _attention(q_l, k_l, v_l, (kc, vc), seq_lat)
    s0 = jnp.stack([_pair_states(state_gla_fwd[:, l]), _pair_states(state_gla_bwd[:, l])], axis=1)
    gla_l, _ = _gla(qg_l, kg_l, vg_l, lr_l, rs_l, s0, wg, bg, gn, seq_lat)

    x1, h2p, rt, ert, counts = _post(xc, xl, attn_c, attn_l, gla_c, gla_l, mod_all, seq_lat,
                                w_mg, w_ba, w_bg, w_o, ln1_g[l][None, :], ln1_b[l][None, :], w_r, b_r)

    pos0, pos1, starts, n_win = _route_tables(ert, counts)
    xs = _scatter_rows(h2p, pos0, pos1)
    ys = _experts(xs, starts, n_win, exp_w_gate[l], exp_w_up[l], exp_w_down[l])
    y_ctx, y_lat = _final(x1, rt, pos0, pos1, mod_all, ln2_g[l][None, :], ln2_b[l][None, :], ys,
                          t_ctx, seq_lat)

    untranspose = lambda a: a.reshape(b_ctx, 1, N_KV_HEADS, HD, seq_ctx).transpose(0, 1, 4, 2, 3)
    new_k, new_v = untranspose(kf_c), untranspose(vf_c)
    new_sf = _unpair_states(sfin_c[:, 0])[:, None]
    new_sb = _unpair_states(sfin_c[:, 1])[:, None]
    return (y_ctx.reshape(b_ctx, seq_ctx, D), y_lat.reshape(b_lat, seq_lat, D),
            new_k, new_v, new_sf, new_sb)
```

```python
import functools

import numpy as np
import jax
import jax.numpy as jnp
from jax import lax
from jax.experimental import pallas as pl
from jax.experimental.pallas import tpu as pltpu

F32 = jnp.float32
BF16 = jnp.bfloat16

D = 1024
GRID_W = 64
HD = 64
N_Q_HEADS = 8
N_KV_HEADS = 2
AW = N_Q_HEADS * HD
KVW = N_KV_HEADS * HD
ROPE_THETA = 10000.0
GLA_H = 4
GLA_DK = 64
GLA_DV = 128
GKW = GLA_H * GLA_DK
GVW = GLA_H * GLA_DV
GATE_RANK = 16
GLA_TAU = 16.0
CHUNK = 64
N_GROUPS = 4
EPG = 8
N_EXP = N_GROUPS * EPG
D_EXP = 256
DEPTH = 1
ALPHA = (2.0 * DEPTH) ** 0.25
LN_EPS = 1e-6
RMS_EPS = 1e-6

LANES = 128
A_WIDTH = AW + 2 * KVW + 2 * GKW + 2 * GVW + 2 * GATE_RANK
MG_WIDTH = 2 * D
TM_TOK = 512
ROW_GROUP = 128
PROJ_AHEAD = 1
TM_PROJ = 1024
TM_POST = 512
LOG2E = 1.4426950408889634
ONES_ROWS = 16
TQ_LAT = 256
SCORE_AHEAD = 2
TM_EXP = 512
READ_AHEAD = 3
TS_ROWS = 2048
VMEM_LIMIT = 56 * 1024 * 1024


def _cparams(sem):
    return pltpu.CompilerParams(dimension_semantics=sem, vmem_limit_bytes=VMEM_LIMIT)


def _dot(a, b):
    return jnp.dot(a, b, preferred_element_type=F32)


def _dot_nt(a, b):
    return lax.dot_general(a, b, (((1,), (1,)), ((), ())), preferred_element_type=F32)


def _dot_tn(a, b):
    return lax.dot_general(a, b, (((0,), (0,)), ((), ())), preferred_element_type=F32)


def _ln(x):
    mu = jnp.mean(x, axis=-1, keepdims=True)
    xc = x - mu
    var = jnp.mean(xc * xc, axis=-1, keepdims=True)
    return xc * lax.rsqrt(var + LN_EPS)


def _silu(x):
    return x * jax.nn.sigmoid(x)


def _split_bf16(x):
    hi = x.astype(BF16)
    lo = (x - hi.astype(F32)).astype(BF16)
    return hi, lo


def _split3_bf16(x):
    hi = x.astype(BF16)
    r1 = x - hi.astype(F32)
    mid = r1.astype(BF16)
    lo = (r1 - mid.astype(F32)).astype(BF16)
    return hi, mid, lo


def _ada_kernel(c_ref, w_ref, b_ref, o_ref):
    rows = c_ref.shape[0]
    s_hi, s_mid, s_lo = _split3_bf16(_silu(c_ref[...]))
    w_hi, w_lo = _split_bf16(w_ref[...])
    a = _dot(jnp.concatenate([s_hi, s_mid, s_lo], axis=0), w_hi)
    b = _dot(jnp.concatenate([s_hi, s_mid], axis=0), w_lo)
    o_ref[...] = (a[0:rows] + a[rows:2 * rows] + a[2 * rows:3 * rows]
                  + b[0:rows] + b[rows:2 * rows] + b_ref[...])


def _ada(c_rows, w_ada, b_ada):
    rows = c_rows.shape[0]
    n = w_ada.shape[1]
    bn = 1024
    return pl.pallas_call(
        _ada_kernel,
        grid=(n // bn,),
        in_specs=[pl.BlockSpec((rows, D), lambda j: (0, 0)),
                  pl.BlockSpec((D, bn), lambda j: (0, j)),
                  pl.BlockSpec((1, bn), lambda j: (0, j))],
        out_specs=pl.BlockSpec((rows, bn), lambda j: (0, j)),
        out_shape=jax.ShapeDtypeStruct((rows, n), F32),
        compiler_params=_cparams(("arbitrary",)),
        name="ada",
    )(c_rows, w_ada, b_ada)


def _inproj_kernel(*refs, latent, seq):
    if latent:
        (x_ref, mod_ref, w_ref, gain_ref, ind_ref, cos_ref, sin_ref,
         q_ref, k_ref, v_ref, qg_ref, kg_ref, vg_ref, rs_ref, lr_ref) = refs
    else:
        (x_ref, mod_ref, w_ref, gain_ref, ind_ref,
         q_ref, k_ref, v_ref, qg_ref, kg_ref, vg_ref, rs_ref, lr_ref, kf_ref, vf_ref) = refs
    tm = x_ref.shape[0]
    sub = ROW_GROUP
    n_groups = tm // sub
    sh1 = mod_ref[0, :, 0:D]
    sc1 = mod_ref[0, :, D:2 * D]
    lane = lax.broadcasted_iota(jnp.int32, (sub, LANES), 1)
    low_half = lane < HD
    first = (lane % 32) < 16

    def project(g):
        rows = slice(g * sub, (g + 1) * sub)
        h = (_ln(x_ref[rows, :]) * (1.0 + sc1) + sh1).astype(BF16)
        return _dot(h, w_ref[...])

    def finish(g, res):
        rows = slice(g * sub, (g + 1) * sub)
        qk = res[:, 0:AW + KVW]
        hi, lo = _split_bf16(qk * qk)
        ms = _dot(hi, ind_ref[...]) + _dot(lo, ind_ref[...])
        r = lax.rsqrt(ms + RMS_EPS)
        for s in range(5):
            rb = jnp.where(low_half, r[:, 2 * s:2 * s + 1], r[:, 2 * s + 1:2 * s + 2])
            y = res[:, LANES * s:LANES * (s + 1)] * rb * gain_ref[:, LANES * s:LANES * (s + 1)]
            if s == 4 and not latent:
                kf_ref[(g * sub) // seq, :, (g * sub) % seq:(g * sub) % seq + sub] = y.T
            if latent:
                partner = jnp.where(first, pltpu.roll(y, LANES - 16, 1), pltpu.roll(y, 16, 1))
                y = y * cos_ref[rows, :] + partner * sin_ref[rows, :]
            if s < 4:
                q_ref[rows, LANES * s:LANES * (s + 1)] = (y * (HD ** -0.5 * LOG2E)).astype(BF16)
            else:
                k_ref[rows, :] = y.astype(BF16)
        o = AW + KVW
        v = res[:, o:o + KVW]
        v_ref[rows, :] = v.astype(BF16)
        if not latent:
            vf_ref[(g * sub) // seq, :, (g * sub) % seq:(g * sub) % seq + sub] = v.T
        o += KVW
        qg_ref[rows, :] = res[:, o:o + GKW] * (GLA_DK ** -0.5)
        o += GKW
        kg_ref[rows, :] = res[:, o:o + GKW]
        o += GKW
        vg_ref[rows, :] = res[:, o:o + GVW].astype(BF16)
        o += GVW
        rs_ref[rows, :] = _silu(res[:, o:o + GVW]).astype(BF16)
        o += GVW
        lr_ref[rows, :] = res[:, o:o + 2 * GATE_RANK]

    ahead = min(PROJ_AHEAD, n_groups)
    pending = {g: project(g) for g in range(ahead)}
    for g in range(n_groups):
        if g + ahead < n_groups:
            pending[g + ahead] = project(g + ahead)
        finish(g, pending.pop(g))


def _inproj(x2, mod, w_a, gain, ind, rope, seq, latent):
    t = x2.shape[0]
    tm = TM_PROJ
    per_seq = max(seq // tm, 1)
    per_tile = max(tm // seq, 1)
    row = lambda i: (i, 0)
    const = lambda i: (0, 0)
    in_specs = [pl.BlockSpec((tm, D), row),
                pl.BlockSpec((1, 1, 6 * D), (lambda i: (i // per_seq, 0, 0)) if latent else (lambda i: (0, 0, 0))),
                pl.BlockSpec((D, A_WIDTH), const),
                pl.BlockSpec((1, AW + KVW), const),
                pl.BlockSpec((AW + KVW, LANES), const)]
    args = [x2, mod, w_a, gain, ind]
    if latent:
        in_specs += [pl.BlockSpec((tm, LANES), lambda i: (i % per_seq, 0))] * 2
        args += list(rope)
    widths = [(AW, BF16), (KVW, BF16), (KVW, BF16), (GKW, F32), (GKW, F32), (GVW, BF16), (GVW, BF16),
              (2 * GATE_RANK, F32)]
    out_specs = [pl.BlockSpec((tm, w), row) for w, _ in widths]
    out_shape = [jax.ShapeDtypeStruct((t, w), dt) for w, dt in widths]
    if not latent:
        cache_spec = pl.BlockSpec((per_tile, KVW, seq), lambda i: (i, 0, 0))
        out_specs += [cache_spec] * 2
        out_shape += [jax.ShapeDtypeStruct((t // seq, KVW, seq), F32)] * 2
    return pl.pallas_call(
        functools.partial(_inproj_kernel, latent=latent, seq=seq),
        grid=(t // tm,),
        in_specs=in_specs,
        out_specs=out_specs,
        out_shape=out_shape,
        compiler_params=_cparams(("parallel",)),
        name="inproj_lat" if latent else "inproj_ctx",
    )(*args)


def _attn_kernel(*refs, has_cache):
    def transposed_with_ones(dst, src):
        dst[0:KVW, :] = src[...].astype(F32).T.astype(BF16)
        dst[KVW:, :] = jnp.ones((ONES_ROWS, dst.shape[1]), BF16)

    if has_cache:
        q_ref, k_ref, v_ref, kc_ref, vc_ref, o_ref, vt_s, vct_s = refs

        @pl.when(pl.program_id(1) == 0)
        def _():
            transposed_with_ones(vt_s, v_ref)
            transposed_with_ones(vct_s, vc_ref)
    else:
        q_ref, k_ref, v_ref, o_ref, vt_s = refs
        transposed_with_ones(vt_s, v_ref)
    tq = q_ref.shape[0]

    lane = lax.broadcasted_iota(jnp.int32, (tq, LANES), 1)
    low_half = lane < HD
    k = k_ref[...]
    def score(c):
        j, pair = divmod(c, 2)
        keep = low_half if j == 0 else jnp.logical_not(low_half)
        zero = jnp.zeros((tq, LANES), BF16)
        qs = jnp.concatenate([jnp.where(keep, q_ref[:, LANES * s:LANES * (s + 1)], zero)
                              for s in (2 * pair, 2 * pair + 1)], axis=0)
        return _dot_nt(k, qs), (_dot_nt(kc_ref[...], qs) if has_cache else None)

    n_chains = 2 * N_KV_HEADS
    ahead = SCORE_AHEAD if has_cache else n_chains
    scores = {c: score(c) for c in range(ahead)}
    outs = []
    for c in range(n_chains):
        if c + ahead < n_chains:
            scores[c + ahead] = score(c + ahead)
        s1, s2 = scores.pop(c)
        m = jnp.max(s1, axis=0, keepdims=True)
        if has_cache:
            m = jnp.maximum(m, jnp.max(s2, axis=0, keepdims=True))
        acc = _dot(vt_s[...], jnp.exp2(s1 - m).astype(BF16))
        if has_cache:
            acc = acc + _dot(vct_s[...], jnp.exp2(s2 - m).astype(BF16))
        outs.append(acc[0:KVW] / acc[KVW:KVW + 1])
    head0 = jnp.concatenate(outs[0:2], axis=1)
    head1 = jnp.concatenate(outs[2:4], axis=1)
    row = lax.broadcasted_iota(jnp.int32, (LANES, 4 * tq), 0)
    out = jnp.where(row < HD, head0, head1).T
    for s in range(4):
        o_ref[:, LANES * s:LANES * (s + 1)] = out[s * tq:(s + 1) * tq].astype(BF16)


def _attention(q, k, v, cache, seq):
    t = q.shape[0]
    if cache is None:
        tq = seq
        grid = (t // seq,)
        qmap = lambda b: (b, 0)
        in_specs = [pl.BlockSpec((tq, AW), qmap), pl.BlockSpec((seq, KVW), qmap),
                    pl.BlockSpec((seq, KVW), qmap)]
        args = [q, k, v]
        scratch = [pltpu.VMEM((KVW + ONES_ROWS, seq), BF16)]
        sem = ("parallel",)
        name = "attn_ctx"
    else:
        tq = TQ_LAT
        nq = seq // tq
        kc, vc = cache
        past = kc.shape[1]
        grid = (t // seq, nq)
        qmap = lambda b, i: (b * nq + i, 0)
        kmap = lambda b, i: (b, 0)
        cmap = lambda b, i: (b, 0, 0)
        in_specs = [pl.BlockSpec((tq, AW), qmap), pl.BlockSpec((seq, KVW), kmap),
                    pl.BlockSpec((seq, KVW), kmap),
                    pl.BlockSpec((None, past, KVW), cmap), pl.BlockSpec((None, past, KVW), cmap)]
        args = [q, k, v, kc, vc]
        scratch = [pltpu.VMEM((KVW + ONES_ROWS, seq), BF16), pltpu.VMEM((KVW + ONES_ROWS, past), BF16)]
        sem = ("parallel", "arbitrary")
        name = "attn_lat"
    return pl.pallas_call(
        functools.partial(_attn_kernel, has_cache=cache is not None),
        grid=grid,
        in_specs=in_specs,
        out_specs=pl.BlockSpec((tq, AW), qmap),
        out_shape=jax.ShapeDtypeStruct((t, AW), BF16),
        scratch_shapes=scratch,
        compiler_params=_cparams(sem),
        name=name,
    )(*args)


GLA_BLK = 256
GLA_ROWS = 1024
GLA_UNROLL = 8


def _gla_kernel(qg_ref, kg_ref, vg_ref, lr_ref, rs_ref, s0_ref, wg_ref, bg_ref, gn_ref,
                o_ref, sfin_ref, cum_s, kv_s, dec_s, *, seqs):
    n = qg_ref.shape[0]
    nc = n // CHUNK
    nc_seq = nc // seqs
    unroll = min(GLA_UNROLL, nc)
    lane = lax.broadcasted_iota(jnp.int32, (CHUNK, LANES), 1)
    low_half = lane < GLA_DK
    lane_sq = lax.broadcasted_iota(jnp.int32, (LANES, LANES), 1)
    low_half_sq = lane_sq < GLA_DK
    ri = lax.broadcasted_iota(jnp.int32, (2 * CHUNK, 2 * CHUNK), 0)
    ci = lax.broadcasted_iota(jnp.int32, (2 * CHUNK, 2 * CHUNK), 1)
    diag = (ri >> 6) == (ci >> 6)
    keep2 = (diag & (ri >= ci), diag & (ci >= ri))

    rb = lax.broadcasted_iota(jnp.int32, (GLA_BLK, GLA_BLK), 0)
    cb = lax.broadcasted_iota(jnp.int32, (GLA_BLK, GLA_BLK), 1)
    same = (rb >> 6) == (cb >> 6)
    tri = ((same & (rb >= cb)).astype(BF16), (same & (cb >= rb)).astype(BF16))
    for d in range(2):
        w_hi, w_lo = _split_bf16(wg_ref[d])
        for blk in range(n // GLA_BLK):
            rows = slice(blk * GLA_BLK, (blk + 1) * GLA_BLK)
            l_hi, l_lo = _split_bf16(lr_ref[rows, :])
            z = _dot(l_hi, w_hi) + _dot(l_lo, w_hi) + _dot(l_hi, w_lo) + bg_ref[d]
            logg = (jnp.minimum(z, 0.0) - jnp.log(1.0 + jnp.exp(-jnp.abs(z)))) * (1.0 / GLA_TAU)
            pieces = _split3_bf16(logg)
            cum_s[d, rows, :] = _dot(tri[d], pieces[0]) + _dot(tri[d], pieces[1]) + _dot(tri[d], pieces[2])

    def pass1(i, carry):
        cs = [i * unroll + u for u in range(unroll)]
        rws = [pl.ds(pl.multiple_of(c * CHUNK, CHUNK), CHUNK) for c in cs]
        prods = []
        for c, rows in zip(cs, rws):
            kc = kg_ref[rows, :]
            kdec = []
            for d in range(2):
                cum = cum_s[d, rows, :]
                last = cum[CHUNK - 1:CHUNK, :] if d == 0 else cum[0:1, :]
                dec_s[d, c] = jnp.exp(last)
                kdec.append((kc * jnp.exp(last - cum)).astype(BF16))
            for p in range(2):
                vpair = vg_ref[rows, GLA_DV * 2 * p:GLA_DV * 2 * (p + 1)]
                kpair = jnp.concatenate([kdec[0][:, LANES * p:LANES * (p + 1)],
                                         kdec[1][:, LANES * p:LANES * (p + 1)]], axis=1)
                prods.append(_dot_tn(vpair, kpair))
        for j, c in enumerate(cs):
            for p in range(2):
                res = prods[2 * j + p]
                for d in range(2):
                    cols = slice(LANES * d, LANES * (d + 1))
                    kv_s[d, c, p] = jnp.where(low_half_sq, res[0:GLA_DV, cols], res[GLA_DV:2 * GLA_DV, cols])
        return carry

    lax.fori_loop(0, nc // unroll, pass1, 0)

    for s in range(seqs):
        for d in range(2):
            def scan(i, st, s=s, d=d):
                c = s * nc_seq + (i if d == 0 else nc_seq - 1 - i)
                dec = dec_s[d, c]
                new = []
                for p in range(2):
                    kv = kv_s[d, c, p]
                    kv_s[d, c, p] = st[p]
                    new.append(st[p] * dec[:, LANES * p:LANES * (p + 1)] + kv)
                return tuple(new)

            s_in = s if s0_ref.shape[0] > 1 else 0
            fin = lax.fori_loop(0, nc_seq, scan, (s0_ref[s_in, d, 0], s0_ref[s_in, d, 1]))
            sfin_ref[s, d, 0] = fin[0]
            sfin_ref[s, d, 1] = fin[1]

    def pass3(i, carry):
        cs = [i * unroll + u for u in range(unroll)]
        rws = [pl.ds(pl.multiple_of(c * CHUNK, CHUNK), CHUNK) for c in cs]
        first = []
        for c, rows in zip(cs, rws):
            q = qg_ref[rows, :]
            k = kg_ref[rows, :]
            for d in range(2):
                cum = cum_s[d, rows, :]
                qt = q * jnp.exp(cum)
                kt = (k * jnp.exp(-cum)).astype(BF16)
                for p in range(2):
                    qs = qt[:, LANES * p:LANES * (p + 1)]
                    lhs = jnp.concatenate([jnp.where(low_half, qs, 0.0), jnp.where(low_half, 0.0, qs)],
                                          axis=0).astype(BF16)
                    kts = kt[:, LANES * p:LANES * (p + 1)]
                    rhs = jnp.concatenate([kts, kts, kv_s[d, c, p].astype(BF16)], axis=0)
                    first.append(_dot_nt(lhs, rhs))
        second = []
        for j, rows in enumerate(rws):
            for d in range(2):
                for p in range(2):
                    res = first[4 * j + 2 * d + p]
                    vp = jnp.concatenate([vg_ref[rows, GLA_DV * (2 * p):GLA_DV * (2 * p + 1)],
                                          vg_ref[rows, GLA_DV * (2 * p + 1):GLA_DV * (2 * p + 2)]], axis=0)
                    a = jnp.where(keep2[d], res[:, 0:2 * CHUNK], 0.0).astype(BF16)
                    second.append(_dot(a, vp) + res[:, 2 * CHUNK:])
        for j, rows in enumerate(rws):
            for p in range(2):
                tot = second[4 * j + p] + second[4 * j + 2 + p]
                y = tot * lax.rsqrt(jnp.mean(tot * tot, axis=-1, keepdims=True) + RMS_EPS) * gn_ref[...]
                for hh in range(2):
                    cols = slice(GLA_DV * (2 * p + hh), GLA_DV * (2 * p + hh + 1))
                    o_ref[rows, cols] = (y[CHUNK * hh:CHUNK * (hh + 1)]
                                         * rs_ref[rows, cols].astype(F32)).astype(BF16)
        return carry

    lax.fori_loop(0, nc // unroll, pass3, 0)


def _gla(qg, kg, vg, lr, rs, s0, wg, bg, gn, seq):
    t = qg.shape[0]
    nb = t // seq
    seqs = max(GLA_ROWS // seq, 1)
    rows = seqs * seq
    nc = rows // CHUNK
    row = lambda b: (b, 0)
    c3 = lambda b: (0, 0, 0)
    st = lambda b: (b, 0, 0, 0, 0)
    if s0.shape[0] == 1:
        s0_spec = pl.BlockSpec((1, 2, 2, LANES, LANES), lambda b: (0, 0, 0, 0, 0))
    else:
        s0_spec = pl.BlockSpec((seqs, 2, 2, LANES, LANES), st)
    return pl.pallas_call(
        functools.partial(_gla_kernel, seqs=seqs),
        grid=(nb // seqs,),
        in_specs=[pl.BlockSpec((rows, GKW), row), pl.BlockSpec((rows, GKW), row),
                  pl.BlockSpec((rows, GVW), row), pl.BlockSpec((rows, 2 * GATE_RANK), row),
                  pl.BlockSpec((rows, GVW), row),
                  s0_spec,
                  pl.BlockSpec((2, 2 * GATE_RANK, GKW), c3), pl.BlockSpec((2, 1, GKW), c3),
                  pl.BlockSpec((1, GLA_DV), lambda b: (0, 0))],
        out_specs=[pl.BlockSpec((rows, GVW), row),
                   pl.BlockSpec((seqs, 2, 2, LANES, LANES), st)],
        out_shape=[jax.ShapeDtypeStruct((t, GVW), BF16),
                   jax.ShapeDtypeStruct((nb, 2, 2, LANES, LANES), F32)],
        scratch_shapes=[pltpu.VMEM((2, rows, GKW), F32),
                        pltpu.VMEM((2, nc, 2, LANES, LANES), F32),
                        pltpu.VMEM((2, nc, 1, GKW), F32)],
        compiler_params=_cparams(("parallel",)),
        name="gla_lat" if seq > 256 else "gla_ctx",
    )(qg, kg, vg, lr, rs, s0, wg, bg, gn)


ROW_SUB = D // LANES
ROW_DTYPE = BF16


def _store_row_slabs(ref, x, tmp):
    m = x.shape[0]
    for c in range(ROW_SUB):
        tmp[pl.ds(c, m, stride=ROW_SUB), :] = x[:, LANES * c:LANES * (c + 1)]
    ref[...] = tmp[...].reshape(m, ROW_SUB, LANES).astype(ROW_DTYPE)


def _load_row_slabs(ref, tmp):
    m = ref.shape[0]
    tmp[...] = ref[...].astype(F32).reshape(m * ROW_SUB, LANES)
    return jnp.concatenate([tmp[pl.ds(c, m, stride=ROW_SUB), :] for c in range(ROW_SUB)], axis=1)


def _row_slab(ref, row):
    return ref.at[pl.ds(row, 1)]


def _post_kernel(xc_ref, xl_ref, ac_ref, al_ref, gc_ref, gl_ref, mod_ref,
                 wmg_ref, wba_ref, wbg_ref, wo_ref, l1g_ref, l1b_ref, wr_ref, br_ref,
                 x1_ref, h2_ref, rt_ref, ert_ref, cnt_ref, run_s, slab_s, logit_s, *, n_ctx_tiles):
    i = pl.program_id(0)
    tm = xc_ref.shape[0]
    sub = ROW_GROUP
    n_groups = tm // sub
    is_ctx = i < n_ctx_tiles
    parts = [slice(g * sub, (g + 1) * sub) for g in range(n_groups)]

    @pl.when(i == 0)
    def _():
        run_s[...] = jnp.zeros_like(run_s)
        logit_s[...] = jnp.zeros_like(logit_s)

    sh1 = mod_ref[0, :, 0:D]
    sc1 = mod_ref[0, :, D:2 * D]
    g1 = mod_ref[0, :, 2 * D:3 * D]
    sh2 = mod_ref[0, :, 3 * D:4 * D]
    sc2 = mod_ref[0, :, 4 * D:5 * D]

    xs = [jnp.where(is_ctx, xc_ref[r, :], xl_ref[r, :]) for r in parts]
    hs = [(_ln(x) * (1.0 + sc1) + sh1).astype(BF16) for x in xs]
    gates = [jax.nn.sigmoid(_dot(h, wmg_ref[...])) for h in hs]
    ba = [_dot(jnp.where(is_ctx, ac_ref[r, :], al_ref[r, :]), wba_ref[...]) for r in parts]
    bg = [_dot(jnp.where(is_ctx, gc_ref[r, :], gl_ref[r, :]), wbg_ref[...]) for r in parts]

    ri = lax.broadcasted_iota(jnp.int32, (sub, sub), 0)
    ci = lax.broadcasted_iota(jnp.int32, (sub, sub), 1)
    earlier = (ri > ci).astype(BF16)
    counted = (i > 0).astype(F32)
    run = run_s[0:1, :]
    for r in parts:
        run = _route(logit_s[r, :], earlier, run, counted, rt_ref.at[r, :], ert_ref.at[:, r])
    run_s[0:1, :] = run
    cnt_ref[...] = jnp.broadcast_to(run, cnt_ref.shape)

    merged = [(g[:, :D] * a + g[:, D:] * b).astype(BF16) for g, a, b in zip(gates, ba, bg)]
    mix = [_dot(m, wo_ref[...]) for m in merged]
    x1s = [_ln(ALPHA * x + g1 * m) * l1g_ref[...] + l1b_ref[...] for x, m in zip(xs, mix)]
    h2s = [_ln(x1) * (1.0 + sc2) + sh2 for x1 in x1s]
    logits = [_dot(h2.astype(BF16), wr_ref[...]) + br_ref[...] for h2 in h2s]
    for g, r in enumerate(parts):
        x1_ref[r, :] = x1s[g]
        _store_row_slabs(h2_ref.at[pl.ds(g * sub, sub)], h2s[g], slab_s)
        logit_s[r, :] = logits[g]


def _route(logit, earlier, run, counted, rt_ref, ert_ref):
    tm = logit.shape[0]
    lane_i = lax.broadcasted_iota(jnp.int32, (tm, LANES), 1)
    lane = lane_i.astype(F32)
    lane_grp = ((lane_i - N_GROUPS) >> 3).astype(F32)
    neg = jnp.float32(-jnp.inf)
    far = jnp.float32(LANES)
    is_g = lane_i < N_GROUPS
    lg = jnp.where(is_g, logit, neg)
    mg = jnp.max(lg, axis=-1, keepdims=True)
    pg_top = 1.0 / jnp.sum(jnp.where(is_g, jnp.exp(logit - mg), 0.0), axis=-1, keepdims=True)
    g_idx = jnp.min(jnp.where(lg == mg, lane, far), axis=-1, keepdims=True)
    in_grp = (lane_i >= N_GROUPS) & (lane_i < N_GROUPS + N_EXP) & (lane_grp == g_idx)
    le = jnp.where(in_grp, logit, neg)
    v1 = jnp.max(le, axis=-1, keepdims=True)
    i1 = jnp.min(jnp.where(le == v1, lane, far), axis=-1, keepdims=True)
    le2 = jnp.where(lane == i1, neg, le)
    v2 = jnp.max(le2, axis=-1, keepdims=True)
    i2 = jnp.min(jnp.where(le2 == v2, lane, far), axis=-1, keepdims=True)
    e1 = i1 - N_GROUPS
    e2 = i2 - N_GROUPS
    tt = jnp.exp(v2 - v1)
    w1 = pg_top / (1.0 + tt)
    w2 = pg_top * tt / (1.0 + tt)

    hot = ((lane == e1) | (lane == e2)).astype(F32)
    before = _dot(earlier, hot.astype(BF16)) + run
    r1 = jnp.sum(jnp.where(lane == e1, before, 0.0), axis=-1, keepdims=True)
    r2 = jnp.sum(jnp.where(lane == e2, before, 0.0), axis=-1, keepdims=True)

    rt = jnp.where(lane_i == 0, e1, 0.0)
    rt = jnp.where(lane_i == 1, e2, rt)
    rt = jnp.where(lane_i == 2, w1, rt)
    rt = jnp.where(lane_i == 3, w2, rt)
    rt = jnp.where(lane_i == 4, r1, rt)
    rt = jnp.where(lane_i == 5, r2, rt)
    rt_ref[...] = rt
    ert_ref[...] = rt.T[0:8, :]
    return run + counted * jnp.sum(hot, axis=0, keepdims=True)


def _post(x_ctx, x_lat, a_ctx, a_lat, g_ctx, g_lat, mod_all, seq_lat,
          w_mg, w_ba, w_bg, w_o, l1g, l1b, w_r, b_r):
    t_ctx, t_lat = x_ctx.shape[0], x_lat.shape[0]
    tm = TM_POST
    nct, nlt = t_ctx // tm, t_lat // tm
    per_seq = seq_lat // tm
    nb_lat = t_lat // seq_lat
    t = t_ctx + t_lat
    n = nct + nlt
    lat = lambda i: jnp.clip(i - nct, 0, nlt - 1)
    cmap = lambda i: (jnp.minimum(i, nct - 1), 0)
    lmap = lambda i: (lat(i), 0)
    mmap = lambda i: (jnp.where(i < nct, nb_lat, lat(i) // per_seq), 0, 0)
    row = lambda i: (jnp.minimum(i, n - 1), 0)
    prev = lambda i: jnp.maximum(i - 1, 0)
    const = lambda i: (0, 0)
    return pl.pallas_call(
        functools.partial(_post_kernel, n_ctx_tiles=nct),
        grid=(n + 1,),
        in_specs=[pl.BlockSpec((tm, D), cmap), pl.BlockSpec((tm, D), lmap),
                  pl.BlockSpec((tm, AW), cmap), pl.BlockSpec((tm, AW), lmap),
                  pl.BlockSpec((tm, GVW), cmap), pl.BlockSpec((tm, GVW), lmap),
                  pl.BlockSpec((1, 1, 6 * D), mmap),
                  pl.BlockSpec((D, MG_WIDTH), const), pl.BlockSpec((AW, D), const),
                  pl.BlockSpec((GVW, D), const), pl.BlockSpec((D, D), const),
                  pl.BlockSpec((1, D), const), pl.BlockSpec((1, D), const),
                  pl.BlockSpec((D, LANES), const), pl.BlockSpec((1, LANES), const)],
        out_specs=[pl.BlockSpec((tm, D), row),
                   pl.BlockSpec((tm, ROW_SUB, LANES), lambda i: (jnp.minimum(i, n - 1), 0, 0)),
                   pl.BlockSpec((tm, LANES), lambda i: (prev(i), 0)),
                   pl.BlockSpec((8, tm), lambda i: (0, prev(i))),
                   pl.BlockSpec((8, LANES), const)],
        out_shape=[jax.ShapeDtypeStruct((t, D), F32), jax.ShapeDtypeStruct((t, ROW_SUB, LANES), ROW_DTYPE),
                   jax.ShapeDtypeStruct((t, LANES), F32), jax.ShapeDtypeStruct((8, t), F32),
                   jax.ShapeDtypeStruct((8, LANES), F32)],
        scratch_shapes=[pltpu.VMEM((8, LANES), F32), pltpu.VMEM((ROW_GROUP * ROW_SUB, LANES), F32),
                        pltpu.VMEM((tm, LANES), F32)],
        compiler_params=_cparams(("arbitrary",)),
        name="post",
    )(x_ctx, x_lat, a_ctx, a_lat, g_ctx, g_lat, mod_all, w_mg, w_ba, w_bg, w_o, l1g, l1b, w_r, b_r)


ROW_UNROLL = 8


def _row_copy(src_ref, dst_ref, sem):
    return pltpu.make_async_copy(src_ref, dst_ref, sem)


def _scatter_kernel(pos0_ref, pos1_ref, h_ref, xs_ref, sem, zero_s, zero_sem):
    ts = h_ref.shape[0]
    n_rows = xs_ref.shape[0] - TM_EXP

    @pl.when(pl.program_id(0) == 0)
    def _():
        zero_s[...] = jnp.zeros_like(zero_s)
        pad = _row_copy(zero_s, xs_ref.at[pl.ds(n_rows, TM_EXP)], zero_sem)
        pad.start()
        pad.wait()

    def issue(g, carry):
        r0 = pl.multiple_of(g * ROW_UNROLL, ROW_UNROLL)
        for k in range(ROW_UNROLL):
            src = _row_slab(h_ref, r0 + k)
            _row_copy(src, _row_slab(xs_ref, pos0_ref[0, r0 + k]), sem).start(priority=0)
            _row_copy(src, _row_slab(xs_ref, pos1_ref[0, r0 + k]), sem).start(priority=1)
        return carry

    lax.fori_loop(0, ts // ROW_UNROLL, issue, 0)
    for _ in range(2):
        _row_copy(h_ref, xs_ref.at[pl.ds(0, ts)], sem).wait()


def _scatter_rows(h2p, pos0, pos1):
    t = h2p.shape[0]
    ts = TS_ROWS
    smem = lambda: pl.BlockSpec((None, 1, ts), lambda i: (i, 0, 0), memory_space=pltpu.SMEM)
    return pl.pallas_call(
        _scatter_kernel,
        grid=(t // ts,),
        in_specs=[smem(), smem(), pl.BlockSpec((ts, ROW_SUB, LANES), lambda i: (i, 0, 0))],
        out_specs=pl.BlockSpec(memory_space=pl.ANY),
        out_shape=jax.ShapeDtypeStruct((2 * t + TM_EXP, ROW_SUB, LANES), ROW_DTYPE),
        scratch_shapes=[pltpu.SemaphoreType.DMA(()), pltpu.VMEM((TM_EXP, ROW_SUB, LANES), ROW_DTYPE),
                        pltpu.SemaphoreType.DMA(())],
        compiler_params=_cparams(("arbitrary",)),
        name="scatter",
    )(pos0.reshape(t // ts, 1, ts), pos1.reshape(t // ts, 1, ts), h2p)


def _expert_kernel(start_ref, nwin_ref, xs_ref, wg_ref, wu_ref, wd_ref, ys_ref,
                   wgu_s, wd_s, in_buf, out_buf, slab_s, in_sem, out_sem):
    e = pl.program_id(0)
    tm = TM_EXP
    wgu_s[:, 0:D_EXP] = wg_ref[...].astype(BF16)
    wgu_s[:, D_EXP:2 * D_EXP] = wu_ref[...].astype(BF16)
    wd_s[...] = wd_ref[...].astype(BF16)
    base = start_ref[e]
    n = nwin_ref[e]

    @pl.when(e == 0)
    def _():
        out_buf[0] = jnp.zeros(out_buf.shape[1:], ROW_DTYPE)
        pad = pltpu.make_async_copy(out_buf.at[0], ys_ref.at[pl.ds(ys_ref.shape[0] - tm, tm)], out_sem.at[0])
        pad.start()
        pad.wait()

    def read(w, slot, first_row=None):
        first_row = base if first_row is None else first_row
        return pltpu.make_async_copy(xs_ref.at[pl.ds(first_row + w * tm, tm)], in_buf.at[slot], in_sem.at[slot])

    def write(w, slot):
        return pltpu.make_async_copy(out_buf.at[slot], ys_ref.at[pl.ds(base + w * tm, tm)], out_sem.at[slot])

    def start_first_reads(expert):
        for a in range(READ_AHEAD):
            @pl.when(nwin_ref[expert] > a)
            def _(a=a):
                read(a, a, start_ref[expert]).start()

    @pl.when(e == 0)
    def _():
        start_first_reads(0)

    def body(w, carry):
        slot = w % 2
        rslot = w % (READ_AHEAD + 1)

        @pl.when(w + READ_AHEAD < n)
        def _():
            read(w + READ_AHEAD, (w + READ_AHEAD) % (READ_AHEAD + 1)).start()

        read(w, rslot).wait()

        @pl.when(w >= 2)
        def _():
            write(w - 2, slot).wait()

        gu = _dot(_load_row_slabs(in_buf.at[rslot], slab_s).astype(BF16), wgu_s[...])
        hid = _silu(gu[:, 0:D_EXP]) * gu[:, D_EXP:2 * D_EXP]
        _store_row_slabs(out_buf.at[slot], _dot(hid.astype(BF16), wd_s[...]), slab_s)
        write(w, slot).start()
        return carry

    lax.fori_loop(0, n, body, 0)

    @pl.when(e + 1 < pl.num_programs(0))
    def _():
        start_first_reads(e + 1)

    @pl.when(n >= 1)
    def _():
        write(n - 1, (n - 1) % 2).wait()

    @pl.when(n >= 2)
    def _():
        write(n - 2, n % 2).wait()


def _experts(xs, starts, n_win, w_gate, w_up, w_down):
    tm = TM_EXP
    wmap = lambda e, st, nw: (e, 0, 0)
    slab = (tm, ROW_SUB, LANES)
    return pl.pallas_call(
        _expert_kernel,
        grid_spec=pltpu.PrefetchScalarGridSpec(
            num_scalar_prefetch=2,
            grid=(N_EXP,),
            in_specs=[pl.BlockSpec(memory_space=pl.ANY),
                      pl.BlockSpec((None, D, D_EXP), wmap), pl.BlockSpec((None, D, D_EXP), wmap),
                      pl.BlockSpec((None, D_EXP, D), wmap)],
            out_specs=pl.BlockSpec(memory_space=pl.ANY),
            scratch_shapes=[pltpu.VMEM((D, 2 * D_EXP), BF16), pltpu.VMEM((D_EXP, D), BF16),
                            pltpu.VMEM((READ_AHEAD + 1,) + slab, ROW_DTYPE), pltpu.VMEM((2,) + slab, ROW_DTYPE),
                            pltpu.VMEM((tm * ROW_SUB, LANES), F32),
                            pltpu.SemaphoreType.DMA((READ_AHEAD + 1,)), pltpu.SemaphoreType.DMA((2,))]),
        out_shape=jax.ShapeDtypeStruct(xs.shape, ROW_DTYPE),
        compiler_params=_cparams(("arbitrary",)),
        name="experts",
    )(starts, n_win, xs, w_gate, w_up, w_down)


def _final_kernel(p0c_ref, p1c_ref, p0n_ref, p1n_ref, x1_ref, rt_ref, mod_ref, l2g_ref, l2b_ref, ys_ref,
                  oc_ref, ol_ref, buf, sem, slab_s, *, n_ctx_tiles):
    i = pl.program_id(0)
    n = pl.num_programs(0)
    tm = x1_ref.shape[0]

    def gather(p0_ref, p1_ref, slot):
        def issue(g, carry):
            r0 = pl.multiple_of(g * ROW_UNROLL, ROW_UNROLL)
            for k in range(ROW_UNROLL):
                _row_copy(_row_slab(ys_ref, p0_ref[0, r0 + k]),
                          _row_slab(buf.at[slot, 0], r0 + k), sem.at[slot]).start(priority=0)
                _row_copy(_row_slab(ys_ref, p1_ref[0, r0 + k]),
                          _row_slab(buf.at[slot, 1], r0 + k), sem.at[slot]).start(priority=1)
            return carry

        lax.fori_loop(0, tm // ROW_UNROLL, issue, 0)

    cur = i % 2

    @pl.when(i == 0)
    def _():
        gather(p0c_ref, p1c_ref, 0)

    @pl.when(i + 1 < n)
    def _():
        gather(p0n_ref, p1n_ref, 1 - cur)

    for k in range(2):
        _row_copy(ys_ref.at[pl.ds(0, tm)], buf.at[cur, k], sem.at[cur]).wait()

    g2 = mod_ref[0, :, 5 * D:6 * D]
    w1 = rt_ref[:, 2:3]
    w2 = rt_ref[:, 3:4]
    moe = (w1 * _load_row_slabs(buf.at[cur, 0], slab_s.at[0])
           + w2 * _load_row_slabs(buf.at[cur, 1], slab_s.at[1]))
    out = _ln(ALPHA * x1_ref[...] + g2 * moe) * l2g_ref[...] + l2b_ref[...]

    @pl.when(i < n_ctx_tiles)
    def _():
        oc_ref[...] = out

    @pl.when(i >= n_ctx_tiles)
    def _():
        ol_ref[...] = out


def _final(x1, rt, pos0, pos1, mod_all, l2g, l2b, ys, t_ctx, seq_lat):
    t = x1.shape[0]
    tm = TM_TOK
    nt = t // tm
    nct = t_ctx // tm
    t_lat = t - t_ctx
    per_seq = seq_lat // tm
    nb_lat = t_lat // seq_lat
    p0 = pos0.reshape(nt, 1, tm)
    p1 = pos1.reshape(nt, 1, tm)
    row = lambda i: (i, 0)
    const = lambda i: (0, 0)
    mmap = lambda i: (jnp.where(i < nct, nb_lat, jnp.maximum(i - nct, 0) // per_seq), 0, 0)
    smem_cur = lambda: pl.BlockSpec((None, 1, tm), lambda i: (i, 0, 0), memory_space=pltpu.SMEM)
    smem_nxt = lambda: pl.BlockSpec((None, 1, tm), lambda i: (jnp.minimum(i + 1, nt - 1), 0, 0),
                                    memory_space=pltpu.SMEM)
    return pl.pallas_call(
        functools.partial(_final_kernel, n_ctx_tiles=nct),
        grid=(nt,),
        in_specs=[smem_cur(), smem_cur(), smem_nxt(), smem_nxt(),
                  pl.BlockSpec((tm, D), row), pl.BlockSpec((tm, LANES), row),
                  pl.BlockSpec((1, 1, 6 * D), mmap),
                  pl.BlockSpec((1, D), const), pl.BlockSpec((1, D), const),
                  pl.BlockSpec(memory_space=pl.ANY)],
        out_specs=[pl.BlockSpec((tm, D), lambda i: (jnp.minimum(i, nct - 1), 0)),
                   pl.BlockSpec((tm, D), lambda i: (jnp.maximum(i - nct, 0), 0))],
        out_shape=[jax.ShapeDtypeStruct((t_ctx, D), F32), jax.ShapeDtypeStruct((t_lat, D), F32)],
        scratch_shapes=[pltpu.VMEM((2, 2, tm, ROW_SUB, LANES), ROW_DTYPE), pltpu.SemaphoreType.DMA((2,)),
                        pltpu.VMEM((2, tm * ROW_SUB, LANES), F32)],
        compiler_params=_cparams(("arbitrary",)),
        name="final",
    )(p0, p1, p0, p1, x1, rt, mod_all, l2g, l2b, ys)


def _reorder_q_heads(w, axis):
    shape = w.shape
    split = shape[:axis] + (N_KV_HEADS, N_Q_HEADS // N_KV_HEADS, HD) + shape[axis + 1:]
    return jnp.swapaxes(w.reshape(split), axis, axis + 1).reshape(shape)


def _rope_tables(seq):
    t = np.arange(seq)
    half = HD // 4
    inv = (ROPE_THETA ** (-np.arange(half, dtype=np.float64) / half)).astype(np.float32)
    d64 = np.arange(LANES) % HD
    pos = np.where((d64 < HD // 2)[None, :], (t // GRID_W)[:, None], (t % GRID_W)[:, None])
    ang = (pos.astype(np.float32) * inv[d64 % half][None, :]).astype(np.float64)
    sign = np.where((d64 % 32) < 16, -1.0, 1.0)
    return (jnp.asarray(np.cos(ang), F32), jnp.asarray(np.sin(ang) * sign[None, :], F32))


def _pair_states(s):
    b = s.shape[0]
    s = s.reshape(b, 2, 2, GLA_DK, GLA_DV)
    return s.transpose(0, 1, 4, 2, 3).reshape(b, 2, GLA_DV, 2 * GLA_DK)


def _unpair_states(s):
    b = s.shape[0]
    s = s.reshape(b, 2, GLA_DV, 2, GLA_DK)
    return s.transpose(0, 1, 3, 4, 2).reshape(b, GLA_H, GLA_DK, GLA_DV)


def _route_tables(ert, counts):
    i32 = jnp.int32
    cnt = counts[0, :N_EXP].astype(i32)
    starts = jnp.cumsum(cnt) - cnt
    table = lambda e: jnp.sum(jnp.where(e[None, :] == jnp.arange(N_EXP, dtype=i32)[:, None],
                                        starts[:, None], 0), axis=0)
    pos0 = table(ert[0].astype(i32)) + ert[4].astype(i32)
    pos1 = table(ert[1].astype(i32)) + ert[5].astype(i32)
    return pos0, pos1, starts, (cnt + (TM_EXP - 1)) // TM_EXP


def kernel(x_prompt, x_sample, cache_k, cache_v, state_gla_fwd, state_gla_bwd, c, c_ctx, w_ada, b_ada, w_in, q_norm, k_norm, gla_w_gate, gla_b_gate, gla_norm, w_br_attn, w_br_gla, w_out, ln1_g, ln1_b, router_group_w, router_group_b, router_expert_w, router_expert_b, exp_w_gate, exp_w_up, exp_w_down, ln2_g, ln2_b):
    b_ctx, seq_ctx, _ = x_prompt.shape
    b_lat, seq_lat, _ = x_sample.shape
    t_ctx, t_lat = b_ctx * seq_ctx, b_lat * seq_lat
    t = t_ctx + t_lat
    l = 0

    rows = -(-(b_lat + 1) // 8) * 8
    c_rows = jnp.zeros((rows, D), F32).at[:b_lat].set(c).at[b_lat].set(c_ctx)
    mod = _ada(c_rows, w_ada[l], b_ada[l][None, :])
    mod_all = mod[:b_lat + 1, None, :]
    mod_lat = mod_all[:b_lat]
    mod_ctx = mod_all[b_lat:]

    w_full = w_in[l]
    w_a = jnp.concatenate([_reorder_q_heads(w_full[:, :AW], 1), w_full[:, AW:A_WIDTH]], axis=1).astype(BF16)
    w_mg = w_full[:, A_WIDTH:].astype(BF16)
    gain = jnp.concatenate([jnp.tile(q_norm[l], N_Q_HEADS), jnp.tile(k_norm[l], N_KV_HEADS)])[None, :]
    head_of = np.arange(AW + KVW) // HD
    ind = jnp.asarray((head_of[:, None] == np.arange(LANES)[None, :]) / HD, BF16)
    w_ba = _reorder_q_heads(w_br_attn[l], 0).astype(BF16)
    w_bg = w_br_gla[l].astype(BF16)
    w_o = w_out[l].astype(BF16)
    w_r = jnp.zeros((D, LANES), F32).at[:, :N_GROUPS].set(router_group_w[l])
    w_r = w_r.at[:, N_GROUPS:N_GROUPS + N_EXP].set(router_expert_w[l]).astype(BF16)
    b_r = jnp.zeros((1, LANES), F32).at[0, :N_GROUPS].set(router_group_b[l])
    b_r = b_r.at[0, N_GROUPS:N_GROUPS + N_EXP].set(router_expert_b[l])
    wg = jnp.zeros((2, 2 * GATE_RANK, GKW), F32)
    wg = wg.at[0, :GATE_RANK].set(gla_w_gate[l, 0]).at[1, GATE_RANK:].set(gla_w_gate[l, 1])
    bg = gla_b_gate[l][:, None, :]
    gn = gla_norm[l][None, :]

    xc = x_prompt.reshape(t_ctx, D)
    xl = x_sample.reshape(t_lat, D)

    q_c, k_c, v_c, qg_c, kg_c, vg_c, rs_c, lr_c, kf_c, vf_c = _inproj(
        xc, mod_ctx, w_a, gain, ind, None, seq_ctx, latent=False)
    attn_c = _attention(q_c, k_c, v_c, None, seq_ctx)
    zero_state = jnp.zeros((1, 2, 2, LANES, LANES), F32)
    gla_c, sfin_c = _gla(qg_c, kg_c, vg_c, lr_c, rs_c, zero_state, wg, bg, gn, seq_ctx)

    q_l, k_l, v_l, qg_l, kg_l, vg_l, rs_l, lr_l = _inproj(
        xl, mod_lat, w_a, gain, ind, _rope_tables(seq_lat), seq_lat, latent=True)
    past = cache_k.shape[2]
    kc = cache_k[:, l].reshape(b_lat, past, KVW).astype(BF16)
    vc = cache_v[:, l].reshape(b_lat, past, KVW).astype(BF16)
    attn_l = _attention(q_l, k_l, v_l, (kc, vc), seq_lat)
    s0 = jnp.stack([_pair_states(state_gla_fwd[:, l]), _pair_states(state_gla_bwd[:, l])], axis=1)
    gla_l, _ = _gla(qg_l, kg_l, vg_l, lr_l, rs_l, s0, wg, bg, gn, seq_lat)

    x1, h2p, rt, ert, counts = _post(xc, xl, attn_c, attn_l, gla_c, gla_l, mod_all, seq_lat,
                                w_mg, w_ba, w_bg, w_o, ln1_g[l][None, :], ln1_b[l][None, :], w_r, b_r)

    pos0, pos1, starts, n_win = _route_tables(ert, counts)
    xs = _scatter_rows(h2p, pos0, pos1)
    ys = _experts(xs, starts, n_win, exp_w_gate[l], exp_w_up[l], exp_w_down[l])
    y_ctx, y_lat = _final(x1, rt, pos0, pos1, mod_all, ln2_g[l][None, :], ln2_b[l][None, :], ys,
                          t_ctx, seq_lat)

    untranspose = lambda a: a.reshape(b_ctx, 1, N_KV_HEADS, HD, seq_ctx).transpose(0, 1, 4, 2, 3)
    new_k, new_v = untranspose(kf_c), untranspose(vf_c)
    new_sf = _unpair_states(sfin_c[:, 0])[:, None]
    new_sb = _unpair_states(sfin_c[:, 1])[:, None]
    return (y_ctx.reshape(b_ctx, seq_ctx, D), y_lat.reshape(b_lat, seq_lat, D),
            new_k, new_v, new_sf, new_sb)
```

```python
import functools

import numpy as np
import jax
import jax.numpy as jnp
from jax import lax
from jax.experimental import pallas as pl
from jax.experimental.pallas import tpu as pltpu

F32 = jnp.float32
BF16 = jnp.bfloat16

D = 1024
GRID_W = 64
HD = 64
N_Q_HEADS = 8
N_KV_HEADS = 2
AW = N_Q_HEADS * HD
KVW = N_KV_HEADS * HD
ROPE_THETA = 10000.0
GLA_H = 4
GLA_DK = 64
GLA_DV = 128
GKW = GLA_H * GLA_DK
GVW = GLA_H * GLA_DV
GATE_RANK = 16
GLA_TAU = 16.0
CHUNK = 64
N_GROUPS = 4
EPG = 8
N_EXP = N_GROUPS * EPG
D_EXP = 256
DEPTH = 1
ALPHA = (2.0 * DEPTH) ** 0.25
LN_EPS = 1e-6
RMS_EPS = 1e-6

LANES = 128
A_WIDTH = AW + 2 * KVW + 2 * GKW + 2 * GVW + 2 * GATE_RANK
MG_WIDTH = 2 * D
TM_TOK = 512
ROW_GROUP = 128
PROJ_AHEAD = 1
TM_PROJ = 1024
TM_POST = 512
LOG2E = 1.4426950408889634
ONES_ROWS = 16
TQ_LAT = 256
SCORE_AHEAD = 2
TM_EXP = 512
READ_AHEAD = 3
TS_ROWS = 2048
VMEM_LIMIT = 56 * 1024 * 1024


def _cparams(sem):
    return pltpu.CompilerParams(dimension_semantics=sem, vmem_limit_bytes=VMEM_LIMIT)


def _dot(a, b):
    return jnp.dot(a, b, preferred_element_type=F32)


def _dot_nt(a, b):
    return lax.dot_general(a, b, (((1,), (1,)), ((), ())), preferred_element_type=F32)


def _dot_tn(a, b):
    return lax.dot_general(a, b, (((0,), (0,)), ((), ())), preferred_element_type=F32)


def _ln(x):
    mu = jnp.mean(x, axis=-1, keepdims=True)
    xc = x - mu
    var = jnp.mean(xc * xc, axis=-1, keepdims=True)
    return xc * lax.rsqrt(var + LN_EPS)


def _silu(x):
    return x * jax.nn.sigmoid(x)


def _split_bf16(x):
    hi = x.astype(BF16)
    lo = (x - hi.astype(F32)).astype(BF16)
    return hi, lo


def _split3_bf16(x):
    hi = x.astype(BF16)
    r1 = x - hi.astype(F32)
    mid = r1.astype(BF16)
    lo = (r1 - mid.astype(F32)).astype(BF16)
    return hi, mid, lo


def _ada_kernel(c_ref, w_ref, b_ref, o_ref):
    rows = c_ref.shape[0]
    s_hi, s_mid, s_lo = _split3_bf16(_silu(c_ref[...]))
    w_hi, w_lo = _split_bf16(w_ref[...])
    a = _dot(jnp.concatenate([s_hi, s_mid, s_lo], axis=0), w_hi)
    b = _dot(jnp.concatenate([s_hi, s_mid], axis=0), w_lo)
    o_ref[...] = (a[0:rows] + a[rows:2 * rows] + a[2 * rows:3 * rows]
                  + b[0:rows] + b[rows:2 * rows] + b_ref[...])


def _ada(c_rows, w_ada, b_ada):
    rows = c_rows.shape[0]
    n = w_ada.shape[1]
    bn = 1024
    return pl.pallas_call(
        _ada_kernel,
        grid=(n // bn,),
        in_specs=[pl.BlockSpec((rows, D), lambda j: (0, 0)),
                  pl.BlockSpec((D, bn), lambda j: (0, j)),
                  pl.BlockSpec((1, bn), lambda j: (0, j))],
        out_specs=pl.BlockSpec((rows, bn), lambda j: (0, j)),
        out_shape=jax.ShapeDtypeStruct((rows, n), F32),
        compiler_params=_cparams(("arbitrary",)),
        name="ada",
    )(c_rows, w_ada, b_ada)


def _inproj_kernel(*refs, latent, seq):
    if latent:
        (x_ref, mod_ref, w_ref, gain_ref, ind_ref, cos_ref, sin_ref,
         q_ref, k_ref, v_ref, qg_ref, kg_ref, vg_ref, rs_ref, lr_ref) = refs
    else:
        (x_ref, mod_ref, w_ref, gain_ref, ind_ref,
         q_ref, k_ref, v_ref, qg_ref, kg_ref, vg_ref, rs_ref, lr_ref, kf_ref, vf_ref) = refs
    tm = x_ref.shape[0]
    sub = ROW_GROUP
    n_groups = tm // sub
    sh1 = mod_ref[0, :, 0:D]
    sc1 = mod_ref[0, :, D:2 * D]
    lane = lax.broadcasted_iota(jnp.int32, (sub, LANES), 1)
    low_half = lane < HD
    first = (lane % 32) < 16

    def project(g):
        rows = slice(g * sub, (g + 1) * sub)
        h = (_ln(x_ref[rows, :]) * (1.0 + sc1) + sh1).astype(BF16)
        return _dot(h, w_ref[...])

    def finish(g, res):
        rows = slice(g * sub, (g + 1) * sub)
        qk = res[:, 0:AW + KVW]
        hi, lo = _split_bf16(qk * qk)
        ms = _dot(hi, ind_ref[...]) + _dot(lo, ind_ref[...])
        r = lax.rsqrt(ms + RMS_EPS)
        for s in range(5):
            rb = jnp.where(low_half, r[:, 2 * s:2 * s + 1], r[:, 2 * s + 1:2 * s + 2])
            y = res[:, LANES * s:LANES * (s + 1)] * rb * gain_ref[:, LANES * s:LANES * (s + 1)]
            if s == 4 and not latent:
                kf_ref[(g * sub) // seq, :, (g * sub) % seq:(g * sub) % seq + sub] = y.T
            if latent:
                partner = jnp.where(first, pltpu.roll(y, LANES - 16, 1), pltpu.roll(y, 16, 1))
                y = y * cos_ref[rows, :] + partner * sin_ref[rows, :]
            if s < 4:
                q_ref[rows, LANES * s:LANES * (s + 1)] = (y * (HD ** -0.5 * LOG2E)).astype(BF16)
            else:
                k_ref[rows, :] = y.astype(BF16)
        o = AW + KVW
        v = res[:, o:o + KVW]
        v_ref[rows, :] = v.astype(BF16)
        if not latent:
            vf_ref[(g * sub) // seq, :, (g * sub) % seq:(g * sub) % seq + sub] = v.T
        o += KVW
        qg_ref[rows, :] = res[:, o:o + GKW] * (GLA_DK ** -0.5)
        o += GKW
        kg_ref[rows, :] = res[:, o:o + GKW]
        o += GKW
        vg_ref[rows, :] = res[:, o:o + GVW].astype(BF16)
        o += GVW
        rs_ref[rows, :] = _silu(res[:, o:o + GVW]).astype(BF16)
        o += GVW
        lr_ref[rows, :] = res[:, o:o + 2 * GATE_RANK]

    ahead = min(PROJ_AHEAD, n_groups)
    pending = {g: project(g) for g in range(ahead)}
    for g in range(n_groups):
        if g + ahead < n_groups:
            pending[g + ahead] = project(g + ahead)
        finish(g, pending.pop(g))


def _inproj(x2, mod, w_a, gain, ind, rope, seq, latent):
    t = x2.shape[0]
    tm = TM_PROJ
    per_seq = max(seq // tm, 1)
    per_tile = max(tm // seq, 1)
    row = lambda i: (i, 0)
    const = lambda i: (0, 0)
    in_specs = [pl.BlockSpec((tm, D), row),
                pl.BlockSpec((1, 1, 6 * D), (lambda i: (i // per_seq, 0, 0)) if latent else (lambda i: (0, 0, 0))),
                pl.BlockSpec((D, A_WIDTH), const),
                pl.BlockSpec((1, AW + KVW), const),
                pl.BlockSpec((AW + KVW, LANES), const)]
    args = [x2, mod, w_a, gain, ind]
    if latent:
        in_specs += [pl.BlockSpec((tm, LANES), lambda i: (i % per_seq, 0))] * 2
        args += list(rope)
    widths = [(AW, BF16), (KVW, BF16), (KVW, BF16), (GKW, F32), (GKW, F32), (GVW, BF16), (GVW, BF16),
              (2 * GATE_RANK, F32)]
    out_specs = [pl.BlockSpec((tm, w), row) for w, _ in widths]
    out_shape = [jax.ShapeDtypeStruct((t, w), dt) for w, dt in widths]
    if not latent:
        cache_spec = pl.BlockSpec((per_tile, KVW, seq), lambda i: (i, 0, 0))
        out_specs += [cache_spec] * 2
        out_shape += [jax.ShapeDtypeStruct((t // seq, KVW, seq), F32)] * 2
    return pl.pallas_call(
        functools.partial(_inproj_kernel, latent=latent, seq=seq),
        grid=(t // tm,),
        in_specs=in_specs,
        out_specs=out_specs,
        out_shape=out_shape,
        compiler_params=_cparams(("parallel",)),
        name="inproj_lat" if latent else "inproj_ctx",
    )(*args)


def _attn_kernel(*refs, has_cache):
    def transposed_with_ones(dst, src):
        dst[0:KVW, :] = src[...].astype(F32).T.astype(BF16)
        dst[KVW:, :] = jnp.ones((ONES_ROWS, dst.shape[1]), BF16)

    if has_cache:
        q_ref, k_ref, v_ref, kc_ref, vc_ref, o_ref, vt_s, vct_s = refs

        @pl.when(pl.program_id(1) == 0)
        def _():
            transposed_with_ones(vt_s, v_ref)
            transposed_with_ones(vct_s, vc_ref)
    else:
        q_ref, k_ref, v_ref, o_ref, vt_s = refs
        transposed_with_ones(vt_s, v_ref)
    tq = q_ref.shape[0]

    lane = lax.broadcasted_iota(jnp.int32, (tq, LANES), 1)
    low_half = lane < HD
    k = k_ref[...]
    def score(c):
        j, pair = divmod(c, 2)
        keep = low_half if j == 0 else jnp.logical_not(low_half)
        zero = jnp.zeros((tq, LANES), BF16)
        qs = jnp.concatenate([jnp.where(keep, q_ref[:, LANES * s:LANES * (s + 1)], zero)
                              for s in (2 * pair, 2 * pair + 1)], axis=0)
        return _dot_nt(k, qs), (_dot_nt(kc_ref[...], qs) if has_cache else None)

    n_chains = 2 * N_KV_HEADS
    ahead = SCORE_AHEAD if has_cache else n_chains
    scores = {c: score(c) for c in range(ahead)}
    outs = []
    for c in range(n_chains):
        if c + ahead < n_chains:
            scores[c + ahead] = score(c + ahead)
        s1, s2 = scores.pop(c)
        m = jnp.max(s1, axis=0, keepdims=True)
        if has_cache:
            m = jnp.maximum(m, jnp.max(s2, axis=0, keepdims=True))
        acc = _dot(vt_s[...], jnp.exp2(s1 - m).astype(BF16))
        if has_cache:
            acc = acc + _dot(vct_s[...], jnp.exp2(s2 - m).astype(BF16))
        outs.append(acc[0:KVW] / acc[KVW:KVW + 1])
    head0 = jnp.concatenate(outs[0:2], axis=1)
    head1 = jnp.concatenate(outs[2:4], axis=1)
    row = lax.broadcasted_iota(jnp.int32, (LANES, 4 * tq), 0)
    out = jnp.where(row < HD, head0, head1).T
    for s in range(4):
        o_ref[:, LANES * s:LANES * (s + 1)] = out[s * tq:(s + 1) * tq].astype(BF16)


def _attention(q, k, v, cache, seq):
    t = q.shape[0]
    if cache is None:
        tq = seq
        grid = (t // seq,)
        qmap = lambda b: (b, 0)
        in_specs = [pl.BlockSpec((tq, AW), qmap), pl.BlockSpec((seq, KVW), qmap),
                    pl.BlockSpec((seq, KVW), qmap)]
        args = [q, k, v]
        scratch = [pltpu.VMEM((KVW + ONES_ROWS, seq), BF16)]
        sem = ("parallel",)
        name = "attn_ctx"
    else:
        tq = TQ_LAT
        nq = seq // tq
        kc, vc = cache
        past = kc.shape[1]
        grid = (t // seq, nq)
        qmap = lambda b, i: (b * nq + i, 0)
        kmap = lambda b, i: (b, 0)
        cmap = lambda b, i: (b, 0, 0)
        in_specs = [pl.BlockSpec((tq, AW), qmap), pl.BlockSpec((seq, KVW), kmap),
                    pl.BlockSpec((seq, KVW), kmap),
                    pl.BlockSpec((None, past, KVW), cmap), pl.BlockSpec((None, past, KVW), cmap)]
        args = [q, k, v, kc, vc]
        scratch = [pltpu.VMEM((KVW + ONES_ROWS, seq), BF16), pltpu.VMEM((KVW + ONES_ROWS, past), BF16)]
        sem = ("parallel", "arbitrary")
        name = "attn_lat"
    return pl.pallas_call(
        functools.partial(_attn_kernel, has_cache=cache is not None),
        grid=grid,
        in_specs=in_specs,
        out_specs=pl.BlockSpec((tq, AW), qmap),
        out_shape=jax.ShapeDtypeStruct((t, AW), BF16),
        scratch_shapes=scratch,
        compiler_params=_cparams(sem),
        name=name,
    )(*args)


GLA_BLK = 256
GLA_ROWS = 2048
GLA_UNROLL = 8


def _gla_kernel(qg_ref, kg_ref, vg_ref, lr_ref, rs_ref, s0_ref, wg_ref, bg_ref, gn_ref,
                o_ref, sfin_ref, cum_s, kv_s, dec_s, *, seqs):
    n = qg_ref.shape[0]
    nc = n // CHUNK
    nc_seq = nc // seqs
    unroll = min(GLA_UNROLL, nc)
    lane = lax.broadcasted_iota(jnp.int32, (CHUNK, LANES), 1)
    low_half = lane < GLA_DK
    lane_sq = lax.broadcasted_iota(jnp.int32, (LANES, LANES), 1)
    low_half_sq = lane_sq < GLA_DK
    ri = lax.broadcasted_iota(jnp.int32, (2 * CHUNK, 2 * CHUNK), 0)
    ci = lax.broadcasted_iota(jnp.int32, (2 * CHUNK, 2 * CHUNK), 1)
    diag = (ri >> 6) == (ci >> 6)
    keep2 = (diag & (ri >= ci), diag & (ci >= ri))

    rb = lax.broadcasted_iota(jnp.int32, (GLA_BLK, GLA_BLK), 0)
    cb = lax.broadcasted_iota(jnp.int32, (GLA_BLK, GLA_BLK), 1)
    same = (rb >> 6) == (cb >> 6)
    tri = ((same & (rb >= cb)).astype(BF16), (same & (cb >= rb)).astype(BF16))
    for d in range(2):
        w_hi, w_lo = _split_bf16(wg_ref[d])
        for blk in range(n // GLA_BLK):
            rows = slice(blk * GLA_BLK, (blk + 1) * GLA_BLK)
            l_hi, l_lo = _split_bf16(lr_ref[rows, :])
            z = _dot(l_hi, w_hi) + _dot(l_lo, w_hi) + _dot(l_hi, w_lo) + bg_ref[d]
            logg = (jnp.minimum(z, 0.0) - jnp.log(1.0 + jnp.exp(-jnp.abs(z)))) * (1.0 / GLA_TAU)
            pieces = _split3_bf16(logg)
            cum_s[d, rows, :] = _dot(tri[d], pieces[0]) + _dot(tri[d], pieces[1]) + _dot(tri[d], pieces[2])

    def pass1(i, carry):
        cs = [i * unroll + u for u in range(unroll)]
        rws = [pl.ds(pl.multiple_of(c * CHUNK, CHUNK), CHUNK) for c in cs]
        prods = []
        for c, rows in zip(cs, rws):
            kc = kg_ref[rows, :]
            kdec = []
            for d in range(2):
                cum = cum_s[d, rows, :]
                last = cum[CHUNK - 1:CHUNK, :] if d == 0 else cum[0:1, :]
                dec_s[d, c] = jnp.exp(last)
                kdec.append((kc * jnp.exp(last - cum)).astype(BF16))
            for p in range(2):
                vpair = vg_ref[rows, GLA_DV * 2 * p:GLA_DV * 2 * (p + 1)]
                kpair = jnp.concatenate([kdec[0][:, LANES * p:LANES * (p + 1)],
                                         kdec[1][:, LANES * p:LANES * (p + 1)]], axis=1)
                prods.append(_dot_tn(vpair, kpair))
        for j, c in enumerate(cs):
            for p in range(2):
                res = prods[2 * j + p]
                for d in range(2):
                    cols = slice(LANES * d, LANES * (d + 1))
                    kv_s[d, c, p] = jnp.where(low_half_sq, res[0:GLA_DV, cols], res[GLA_DV:2 * GLA_DV, cols])
        return carry

    lax.fori_loop(0, nc // unroll, pass1, 0)

    for s in range(seqs):
        for d in range(2):
            def scan(i, st, s=s, d=d):
                c = s * nc_seq + (i if d == 0 else nc_seq - 1 - i)
                dec = dec_s[d, c]
                new = []
                for p in range(2):
                    kv = kv_s[d, c, p]
                    kv_s[d, c, p] = st[p]
                    new.append(st[p] * dec[:, LANES * p:LANES * (p + 1)] + kv)
                return tuple(new)

            s_in = s if s0_ref.shape[0] > 1 else 0
            fin = lax.fori_loop(0, nc_seq, scan, (s0_ref[s_in, d, 0], s0_ref[s_in, d, 1]))
            sfin_ref[s, d, 0] = fin[0]
            sfin_ref[s, d, 1] = fin[1]

    def pass3(i, carry):
        cs = [i * unroll + u for u in range(unroll)]
        rws = [pl.ds(pl.multiple_of(c * CHUNK, CHUNK), CHUNK) for c in cs]
        first = []
        for c, rows in zip(cs, rws):
            q = qg_ref[rows, :]
            k = kg_ref[rows, :]
            for d in range(2):
                cum = cum_s[d, rows, :]
                qt = q * jnp.exp(cum)
                kt = (k * jnp.exp(-cum)).astype(BF16)
                for p in range(2):
                    qs = qt[:, LANES * p:LANES * (p + 1)]
                    lhs = jnp.concatenate([jnp.where(low_half, qs, 0.0), jnp.where(low_half, 0.0, qs)],
                                          axis=0).astype(BF16)
                    kts = kt[:, LANES * p:LANES * (p + 1)]
                    rhs = jnp.concatenate([kts, kts, kv_s[d, c, p].astype(BF16)], axis=0)
                    first.append(_dot_nt(lhs, rhs))
        second = []
        for j, rows in enumerate(rws):
            for d in range(2):
                for p in range(2):
                    res = first[4 * j + 2 * d + p]
                    vp = jnp.concatenate([vg_ref[rows, GLA_DV * (2 * p):GLA_DV * (2 * p + 1)],
                                          vg_ref[rows, GLA_DV * (2 * p + 1):GLA_DV * (2 * p + 2)]], axis=0)
                    a = jnp.where(keep2[d], res[:, 0:2 * CHUNK], 0.0).astype(BF16)
                    second.append(_dot(a, vp) + res[:, 2 * CHUNK:])
        for j, rows in enumerate(rws):
            for p in range(2):
                tot = second[4 * j + p] + second[4 * j + 2 + p]
                y = tot * lax.rsqrt(jnp.mean(tot * tot, axis=-1, keepdims=True) + RMS_EPS) * gn_ref[...]
                for hh in range(2):
                    cols = slice(GLA_DV * (2 * p + hh), GLA_DV * (2 * p + hh + 1))
                    o_ref[rows, cols] = (y[CHUNK * hh:CHUNK * (hh + 1)]
                                         * rs_ref[rows, cols].astype(F32)).astype(BF16)
        return carry

    lax.fori_loop(0, nc // unroll, pass3, 0)


def _gla(qg, kg, vg, lr, rs, s0, wg, bg, gn, seq):
    t = qg.shape[0]
    nb = t // seq
    seqs = max(GLA_ROWS // seq, 1)
    rows = seqs * seq
    nc = rows // CHUNK
    row = lambda b: (b, 0)
    c3 = lambda b: (0, 0, 0)
    st = lambda b: (b, 0, 0, 0, 0)
    if s0.shape[0] == 1:
        s0_spec = pl.BlockSpec((1, 2, 2, LANES, LANES), lambda b: (0, 0, 0, 0, 0))
    else:
        s0_spec = pl.BlockSpec((seqs, 2, 2, LANES, LANES), st)
    return pl.pallas_call(
        functools.partial(_gla_kernel, seqs=seqs),
        grid=(nb // seqs,),
        in_specs=[pl.BlockSpec((rows, GKW), row), pl.BlockSpec((rows, GKW), row),
                  pl.BlockSpec((rows, GVW), row), pl.BlockSpec((rows, 2 * GATE_RANK), row),
                  pl.BlockSpec((rows, GVW), row),
                  s0_spec,
                  pl.BlockSpec((2, 2 * GATE_RANK, GKW), c3), pl.BlockSpec((2, 1, GKW), c3),
                  pl.BlockSpec((1, GLA_DV), lambda b: (0, 0))],
        out_specs=[pl.BlockSpec((rows, GVW), row),
                   pl.BlockSpec((seqs, 2, 2, LANES, LANES), st)],
        out_shape=[jax.ShapeDtypeStruct((t, GVW), BF16),
                   jax.ShapeDtypeStruct((nb, 2, 2, LANES, LANES), F32)],
        scratch_shapes=[pltpu.VMEM((2, rows, GKW), F32),
                        pltpu.VMEM((2, nc, 2, LANES, LANES), F32),
                        pltpu.VMEM((2, nc, 1, GKW), F32)],
        compiler_params=_cparams(("parallel",)),
        name="gla_lat" if seq > 256 else "gla_ctx",
    )(qg, kg, vg, lr, rs, s0, wg, bg, gn)


ROW_SUB = D // LANES
ROW_DTYPE = BF16


def _store_row_slabs(ref, x, tmp):
    m = x.shape[0]
    for c in range(ROW_SUB):
        tmp[pl.ds(c, m, stride=ROW_SUB), :] = x[:, LANES * c:LANES * (c + 1)]
    ref[...] = tmp[...].reshape(m, ROW_SUB, LANES).astype(ROW_DTYPE)


def _load_row_slabs(ref, tmp):
    m = ref.shape[0]
    tmp[...] = ref[...].astype(F32).reshape(m * ROW_SUB, LANES)
    return jnp.concatenate([tmp[pl.ds(c, m, stride=ROW_SUB), :] for c in range(ROW_SUB)], axis=1)


def _row_slab(ref, row):
    return ref.at[pl.ds(row, 1)]


def _post_kernel(xc_ref, xl_ref, ac_ref, al_ref, gc_ref, gl_ref, mod_ref,
                 wmg_ref, wba_ref, wbg_ref, wo_ref, l1g_ref, l1b_ref, wr_ref, br_ref,
                 x1_ref, h2_ref, rt_ref, ert_ref, cnt_ref, run_s, slab_s, logit_s, *, n_ctx_tiles):
    i = pl.program_id(0)
    tm = xc_ref.shape[0]
    sub = ROW_GROUP
    n_groups = tm // sub
    is_ctx = i < n_ctx_tiles
    parts = [slice(g * sub, (g + 1) * sub) for g in range(n_groups)]

    @pl.when(i == 0)
    def _():
        run_s[...] = jnp.zeros_like(run_s)
        logit_s[...] = jnp.zeros_like(logit_s)

    sh1 = mod_ref[0, :, 0:D]
    sc1 = mod_ref[0, :, D:2 * D]
    g1 = mod_ref[0, :, 2 * D:3 * D]
    sh2 = mod_ref[0, :, 3 * D:4 * D]
    sc2 = mod_ref[0, :, 4 * D:5 * D]

    xs = [jnp.where(is_ctx, xc_ref[r, :], xl_ref[r, :]) for r in parts]
    hs = [(_ln(x) * (1.0 + sc1) + sh1).astype(BF16) for x in xs]
    gates = [jax.nn.sigmoid(_dot(h, wmg_ref[...])) for h in hs]
    ba = [_dot(jnp.where(is_ctx, ac_ref[r, :], al_ref[r, :]), wba_ref[...]) for r in parts]
    bg = [_dot(jnp.where(is_ctx, gc_ref[r, :], gl_ref[r, :]), wbg_ref[...]) for r in parts]

    ri = lax.broadcasted_iota(jnp.int32, (sub, sub), 0)
    ci = lax.broadcasted_iota(jnp.int32, (sub, sub), 1)
    earlier = (ri > ci).astype(BF16)
    counted = (i > 0).astype(F32)
    run = run_s[0:1, :]
    for r in parts:
        run = _route(logit_s[r, :], earlier, run, counted, rt_ref.at[r, :], ert_ref.at[:, r])
    run_s[0:1, :] = run
    cnt_ref[...] = jnp.broadcast_to(run, cnt_ref.shape)

    merged = [(g[:, :D] * a + g[:, D:] * b).astype(BF16) for g, a, b in zip(gates, ba, bg)]
    mix = [_dot(m, wo_ref[...]) for m in merged]
    x1s = [_ln(ALPHA * x + g1 * m) * l1g_ref[...] + l1b_ref[...] for x, m in zip(xs, mix)]
    h2s = [_ln(x1) * (1.0 + sc2) + sh2 for x1 in x1s]
    logits = [_dot(h2.astype(BF16), wr_ref[...]) + br_ref[...] for h2 in h2s]
    for g, r in enumerate(parts):
        x1_ref[r, :] = x1s[g]
        _store_row_slabs(h2_ref.at[pl.ds(g * sub, sub)], h2s[g], slab_s)
        logit_s[r, :] = logits[g]


def _route(logit, earlier, run, counted, rt_ref, ert_ref):
    tm = logit.shape[0]
    lane_i = lax.broadcasted_iota(jnp.int32, (tm, LANES), 1)
    lane = lane_i.astype(F32)
    lane_grp = ((lane_i - N_GROUPS) >> 3).astype(F32)
    neg = jnp.float32(-jnp.inf)
    far = jnp.float32(LANES)
    is_g = lane_i < N_GROUPS
    lg = jnp.where(is_g, logit, neg)
    mg = jnp.max(lg, axis=-1, keepdims=True)
    pg_top = 1.0 / jnp.sum(jnp.where(is_g, jnp.exp(logit - mg), 0.0), axis=-1, keepdims=True)
    g_idx = jnp.min(jnp.where(lg == mg, lane, far), axis=-1, keepdims=True)
    in_grp = (lane_i >= N_GROUPS) & (lane_i < N_GROUPS + N_EXP) & (lane_grp == g_idx)
    le = jnp.where(in_grp, logit, neg)
    v1 = jnp.max(le, axis=-1, keepdims=True)
    i1 = jnp.min(jnp.where(le == v1, lane, far), axis=-1, keepdims=True)
    le2 = jnp.where(lane == i1, neg, le)
    v2 = jnp.max(le2, axis=-1, keepdims=True)
    i2 = jnp.min(jnp.where(le2 == v2, lane, far), axis=-1, keepdims=True)
    e1 = i1 - N_GROUPS
    e2 = i2 - N_GROUPS
    tt = jnp.exp(v2 - v1)
    w1 = pg_top / (1.0 + tt)
    w2 = pg_top * tt / (1.0 + tt)

    hot = ((lane == e1) | (lane == e2)).astype(F32)
    before = _dot(earlier, hot.astype(BF16)) + run
    r1 = jnp.sum(jnp.where(lane == e1, before, 0.0), axis=-1, keepdims=True)
    r2 = jnp.sum(jnp.where(lane == e2, before, 0.0), axis=-1, keepdims=True)

    rt = jnp.where(lane_i == 0, e1, 0.0)
    rt = jnp.where(lane_i == 1, e2, rt)
    rt = jnp.where(lane_i == 2, w1, rt)
    rt = jnp.where(lane_i == 3, w2, rt)
    rt = jnp.where(lane_i == 4, r1, rt)
    rt = jnp.where(lane_i == 5, r2, rt)
    rt_ref[...] = rt
    ert_ref[...] = rt.T[0:8, :]
    return run + counted * jnp.sum(hot, axis=0, keepdims=True)


def _post(x_ctx, x_lat, a_ctx, a_lat, g_ctx, g_lat, mod_all, seq_lat,
          w_mg, w_ba, w_bg, w_o, l1g, l1b, w_r, b_r):
    t_ctx, t_lat = x_ctx.shape[0], x_lat.shape[0]
    tm = TM_POST
    nct, nlt = t_ctx // tm, t_lat // tm
    per_seq = seq_lat // tm
    nb_lat = t_lat // seq_lat
    t = t_ctx + t_lat
    n = nct + nlt
    lat = lambda i: jnp.clip(i - nct, 0, nlt - 1)
    cmap = lambda i: (jnp.minimum(i, nct - 1), 0)
    lmap = lambda i: (lat(i), 0)
    mmap = lambda i: (jnp.where(i < nct, nb_lat, lat(i) // per_seq), 0, 0)
    row = lambda i: (jnp.minimum(i, n - 1), 0)
    prev = lambda i: jnp.maximum(i - 1, 0)
    const = lambda i: (0, 0)
    return pl.pallas_call(
        functools.partial(_post_kernel, n_ctx_tiles=nct),
        grid=(n + 1,),
        in_specs=[pl.BlockSpec((tm, D), cmap), pl.BlockSpec((tm, D), lmap),
                  pl.BlockSpec((tm, AW), cmap), pl.BlockSpec((tm, AW), lmap),
                  pl.BlockSpec((tm, GVW), cmap), pl.BlockSpec((tm, GVW), lmap),
                  pl.BlockSpec((1, 1, 6 * D), mmap),
                  pl.BlockSpec((D, MG_WIDTH), const), pl.BlockSpec((AW, D), const),
                  pl.BlockSpec((GVW, D), const), pl.BlockSpec((D, D), const),
                  pl.BlockSpec((1, D), const), pl.BlockSpec((1, D), const),
                  pl.BlockSpec((D, LANES), const), pl.BlockSpec((1, LANES), const)],
        out_specs=[pl.BlockSpec((tm, D), row),
                   pl.BlockSpec((tm, ROW_SUB, LANES), lambda i: (jnp.minimum(i, n - 1), 0, 0)),
                   pl.BlockSpec((tm, LANES), lambda i: (prev(i), 0)),
                   pl.BlockSpec((8, tm), lambda i: (0, prev(i))),
                   pl.BlockSpec((8, LANES), const)],
        out_shape=[jax.ShapeDtypeStruct((t, D), F32), jax.ShapeDtypeStruct((t, ROW_SUB, LANES), ROW_DTYPE),
                   jax.ShapeDtypeStruct((t, LANES), F32), jax.ShapeDtypeStruct((8, t), F32),
                   jax.ShapeDtypeStruct((8, LANES), F32)],
        scratch_shapes=[pltpu.VMEM((8, LANES), F32), pltpu.VMEM((ROW_GROUP * ROW_SUB, LANES), F32),
                        pltpu.VMEM((tm, LANES), F32)],
        compiler_params=_cparams(("arbitrary",)),
        name="post",
    )(x_ctx, x_lat, a_ctx, a_lat, g_ctx, g_lat, mod_all, w_mg, w_ba, w_bg, w_o, l1g, l1b, w_r, b_r)


ROW_UNROLL = 8


def _row_copy(src_ref, dst_ref, sem):
    return pltpu.make_async_copy(src_ref, dst_ref, sem)


def _scatter_kernel(pos0_ref, pos1_ref, h_ref, xs_ref, sem, zero_s, zero_sem):
    ts = h_ref.shape[0]
    n_rows = xs_ref.shape[0] - TM_EXP

    @pl.when(pl.program_id(0) == 0)
    def _():
        zero_s[...] = jnp.zeros_like(zero_s)
        pad = _row_copy(zero_s, xs_ref.at[pl.ds(n_rows, TM_EXP)], zero_sem)
        pad.start()
        pad.wait()

    def issue(g, carry):
        r0 = pl.multiple_of(g * ROW_UNROLL, ROW_UNROLL)
        for k in range(ROW_UNROLL):
            src = _row_slab(h_ref, r0 + k)
            _row_copy(src, _row_slab(xs_ref, pos0_ref[0, r0 + k]), sem).start(priority=0)
            _row_copy(src, _row_slab(xs_ref, pos1_ref[0, r0 + k]), sem).start(priority=1)
        return carry

    lax.fori_loop(0, ts // ROW_UNROLL, issue, 0)
    for _ in range(2):
        _row_copy(h_ref, xs_ref.at[pl.ds(0, ts)], sem).wait()


def _scatter_rows(h2p, pos0, pos1):
    t = h2p.shape[0]
    ts = TS_ROWS
    smem = lambda: pl.BlockSpec((None, 1, ts), lambda i: (i, 0, 0), memory_space=pltpu.SMEM)
    return pl.pallas_call(
        _scatter_kernel,
        grid=(t // ts,),
        in_specs=[smem(), smem(), pl.BlockSpec((ts, ROW_SUB, LANES), lambda i: (i, 0, 0))],
        out_specs=pl.BlockSpec(memory_space=pl.ANY),
        out_shape=jax.ShapeDtypeStruct((2 * t + TM_EXP, ROW_SUB, LANES), ROW_DTYPE),
        scratch_shapes=[pltpu.SemaphoreType.DMA(()), pltpu.VMEM((TM_EXP, ROW_SUB, LANES), ROW_DTYPE),
                        pltpu.SemaphoreType.DMA(())],
        compiler_params=_cparams(("arbitrary",)),
        name="scatter",
    )(pos0.reshape(t // ts, 1, ts), pos1.reshape(t // ts, 1, ts), h2p)


def _expert_kernel(start_ref, nwin_ref, xs_ref, wg_ref, wu_ref, wd_ref, ys_ref,
                   wgu_s, wd_s, in_buf, out_buf, slab_s, in_sem, out_sem):
    e = pl.program_id(0)
    tm = TM_EXP
    wgu_s[:, 0:D_EXP] = wg_ref[...].astype(BF16)
    wgu_s[:, D_EXP:2 * D_EXP] = wu_ref[...].astype(BF16)
    wd_s[...] = wd_ref[...].astype(BF16)
    base = start_ref[e]
    n = nwin_ref[e]

    @pl.when(e == 0)
    def _():
        out_buf[0] = jnp.zeros(out_buf.shape[1:], ROW_DTYPE)
        pad = pltpu.make_async_copy(out_buf.at[0], ys_ref.at[pl.ds(ys_ref.shape[0] - tm, tm)], out_sem.at[0])
        pad.start()
        pad.wait()

    def read(w, slot, first_row=None):
        first_row = base if first_row is None else first_row
        return pltpu.make_async_copy(xs_ref.at[pl.ds(first_row + w * tm, tm)], in_buf.at[slot], in_sem.at[slot])

    def write(w, slot):
        return pltpu.make_async_copy(out_buf.at[slot], ys_ref.at[pl.ds(base + w * tm, tm)], out_sem.at[slot])

    def start_first_reads(expert):
        for a in range(READ_AHEAD):
            @pl.when(nwin_ref[expert] > a)
            def _(a=a):
                read(a, a, start_ref[expert]).start()

    @pl.when(e == 0)
    def _():
        start_first_reads(0)

    def body(w, carry):
        slot = w % 2
        rslot = w % (READ_AHEAD + 1)

        @pl.when(w + READ_AHEAD < n)
        def _():
            read(w + READ_AHEAD, (w + READ_AHEAD) % (READ_AHEAD + 1)).start()

        read(w, rslot).wait()

        @pl.when(w >= 2)
        def _():
            write(w - 2, slot).wait()

        gu = _dot(_load_row_slabs(in_buf.at[rslot], slab_s).astype(BF16), wgu_s[...])
        hid = _silu(gu[:, 0:D_EXP]) * gu[:, D_EXP:2 * D_EXP]
        _store_row_slabs(out_buf.at[slot], _dot(hid.astype(BF16), wd_s[...]), slab_s)
        write(w, slot).start()
        return carry

    lax.fori_loop(0, n, body, 0)

    @pl.when(e + 1 < pl.num_programs(0))
    def _():
        start_first_reads(e + 1)

    @pl.when(n >= 1)
    def _():
        write(n - 1, (n - 1) % 2).wait()

    @pl.when(n >= 2)
    def _():
        write(n - 2, n % 2).wait()


def _experts(xs, starts, n_win, w_gate, w_up, w_down):
    tm = TM_EXP
    wmap = lambda e, st, nw: (e, 0, 0)
    slab = (tm, ROW_SUB, LANES)
    return pl.pallas_call(
        _expert_kernel,
        grid_spec=pltpu.PrefetchScalarGridSpec(
            num_scalar_prefetch=2,
            grid=(N_EXP,),
            in_specs=[pl.BlockSpec(memory_space=pl.ANY),
                      pl.BlockSpec((None, D, D_EXP), wmap), pl.BlockSpec((None, D, D_EXP), wmap),
                      pl.BlockSpec((None, D_EXP, D), wmap)],
            out_specs=pl.BlockSpec(memory_space=pl.ANY),
            scratch_shapes=[pltpu.VMEM((D, 2 * D_EXP), BF16), pltpu.VMEM((D_EXP, D), BF16),
                            pltpu.VMEM((READ_AHEAD + 1,) + slab, ROW_DTYPE), pltpu.VMEM((2,) + slab, ROW_DTYPE),
                            pltpu.VMEM((tm * ROW_SUB, LANES), F32),
                            pltpu.SemaphoreType.DMA((READ_AHEAD + 1,)), pltpu.SemaphoreType.DMA((2,))]),
        out_shape=jax.ShapeDtypeStruct(xs.shape, ROW_DTYPE),
        compiler_params=_cparams(("arbitrary",)),
        name="experts",
    )(starts, n_win, xs, w_gate, w_up, w_down)


def _final_kernel(p0c_ref, p1c_ref, p0n_ref, p1n_ref, x1_ref, rt_ref, mod_ref, l2g_ref, l2b_ref, ys_ref,
                  oc_ref, ol_ref, buf, sem, slab_s, *, n_ctx_tiles):
    i = pl.program_id(0)
    n = pl.num_programs(0)
    tm = x1_ref.shape[0]

    def gather(p0_ref, p1_ref, slot):
        def issue(g, carry):
            r0 = pl.multiple_of(g * ROW_UNROLL, ROW_UNROLL)
            for k in range(ROW_UNROLL):
                _row_copy(_row_slab(ys_ref, p0_ref[0, r0 + k]),
                          _row_slab(buf.at[slot, 0], r0 + k), sem.at[slot]).start(priority=0)
                _row_copy(_row_slab(ys_ref, p1_ref[0, r0 + k]),
                          _row_slab(buf.at[slot, 1], r0 + k), sem.at[slot]).start(priority=1)
            return carry

        lax.fori_loop(0, tm // ROW_UNROLL, issue, 0)

    cur = i % 2

    @pl.when(i == 0)
    def _():
        gather(p0c_ref, p1c_ref, 0)

    @pl.when(i + 1 < n)
    def _():
        gather(p0n_ref, p1n_ref, 1 - cur)

    for k in range(2):
        _row_copy(ys_ref.at[pl.ds(0, tm)], buf.at[cur, k], sem.at[cur]).wait()

    g2 = mod_ref[0, :, 5 * D:6 * D]
    w1 = rt_ref[:, 2:3]
    w2 = rt_ref[:, 3:4]
    moe = (w1 * _load_row_slabs(buf.at[cur, 0], slab_s.at[0])
           + w2 * _load_row_slabs(buf.at[cur, 1], slab_s.at[1]))
    out = _ln(ALPHA * x1_ref[...] + g2 * moe) * l2g_ref[...] + l2b_ref[...]

    @pl.when(i < n_ctx_tiles)
    def _():
        oc_ref[...] = out

    @pl.when(i >= n_ctx_tiles)
    def _():
        ol_ref[...] = out


def _final(x1, rt, pos0, pos1, mod_all, l2g, l2b, ys, t_ctx, seq_lat):
    t = x1.shape[0]
    tm = TM_TOK
    nt = t // tm
    nct = t_ctx // tm
    t_lat = t - t_ctx
    per_seq = seq_lat // tm
    nb_lat = t_lat // seq_lat
    p0 = pos0.reshape(nt, 1, tm)
    p1 = pos1.reshape(nt, 1, tm)
    row = lambda i: (i, 0)
    const = lambda i: (0, 0)
    mmap = lambda i: (jnp.where(i < nct, nb_lat, jnp.maximum(i - nct, 0) // per_seq), 0, 0)
    smem_cur = lambda: pl.BlockSpec((None, 1, tm), lambda i: (i, 0, 0), memory_space=pltpu.SMEM)
    smem_nxt = lambda: pl.BlockSpec((None, 1, tm), lambda i: (jnp.minimum(i + 1, nt - 1), 0, 0),
                                    memory_space=pltpu.SMEM)
    return pl.pallas_call(
        functools.partial(_final_kernel, n_ctx_tiles=nct),
        grid=(nt,),
        in_specs=[smem_cur(), smem_cur(), smem_nxt(), smem_nxt(),
                  pl.BlockSpec((tm, D), row), pl.BlockSpec((tm, LANES), row),
                  pl.BlockSpec((1, 1, 6 * D), mmap),
                  pl.BlockSpec((1, D), const), pl.BlockSpec((1, D), const),
                  pl.BlockSpec(memory_space=pl.ANY)],
        out_specs=[pl.BlockSpec((tm, D), lambda i: (jnp.minimum(i, nct - 1), 0)),
                   pl.BlockSpec((tm, D), lambda i: (jnp.maximum(i - nct, 0), 0))],
        out_shape=[jax.ShapeDtypeStruct((t_ctx, D), F32), jax.ShapeDtypeStruct((t_lat, D), F32)],
        scratch_shapes=[pltpu.VMEM((2, 2, tm, ROW_SUB, LANES), ROW_DTYPE), pltpu.SemaphoreType.DMA((2,)),
                        pltpu.VMEM((2, tm * ROW_SUB, LANES), F32)],
        compiler_params=_cparams(("arbitrary",)),
        name="final",
    )(p0, p1, p0, p1, x1, rt, mod_all, l2g, l2b, ys)


def _reorder_q_heads(w, axis):
    shape = w.shape
    split = shape[:axis] + (N_KV_HEADS, N_Q_HEADS // N_KV_HEADS, HD) + shape[axis + 1:]
    return jnp.swapaxes(w.reshape(split), axis, axis + 1).reshape(shape)


def _rope_tables(seq):
    t = np.arange(seq)
    half = HD // 4
    inv = (ROPE_THETA ** (-np.arange(half, dtype=np.float64) / half)).astype(np.float32)
    d64 = np.arange(LANES) % HD
    pos = np.where((d64 < HD // 2)[None, :], (t // GRID_W)[:, None], (t % GRID_W)[:, None])
    ang = (pos.astype(np.float32) * inv[d64 % half][None, :]).astype(np.float64)
    sign = np.where((d64 % 32) < 16, -1.0, 1.0)
    return (jnp.asarray(np.cos(ang), F32), jnp.asarray(np.sin(ang) * sign[None, :], F32))


def _pair_states(s):
    b = s.shape[0]
    s = s.reshape(b, 2, 2, GLA_DK, GLA_DV)
    return s.transpose(0, 1, 4, 2, 3).reshape(b, 2, GLA_DV, 2 * GLA_DK)


def _unpair_states(s):
    b = s.shape[0]
    s = s.reshape(b, 2, GLA_DV, 2, GLA_DK)
    return s.transpose(0, 1, 3, 4, 2).reshape(b, GLA_H, GLA_DK, GLA_DV)


def _route_tables(ert, counts):
    i32 = jnp.int32
    cnt = counts[0, :N_EXP].astype(i32)
    starts = jnp.cumsum(cnt) - cnt
    table = lambda e: jnp.sum(jnp.where(e[None, :] == jnp.arange(N_EXP, dtype=i32)[:, None],
                                        starts[:, None], 0), axis=0)
    pos0 = table(ert[0].astype(i32)) + ert[4].astype(i32)
    pos1 = table(ert[1].astype(i32)) + ert[5].astype(i32)
    return pos0, pos1, starts, (cnt + (TM_EXP - 1)) // TM_EXP


def kernel(x_prompt, x_sample, cache_k, cache_v, state_gla_fwd, state_gla_bwd, c, c_ctx, w_ada, b_ada, w_in, q_norm, k_norm, gla_w_gate, gla_b_gate, gla_norm, w_br_attn, w_br_gla, w_out, ln1_g, ln1_b, router_group_w, router_group_b, router_expert_w, router_expert_b, exp_w_gate, exp_w_up, exp_w_down, ln2_g, ln2_b):
    b_ctx, seq_ctx, _ = x_prompt.shape
    b_lat, seq_lat, _ = x_sample.shape
    t_ctx, t_lat = b_ctx * seq_ctx, b_lat * seq_lat
    t = t_ctx + t_lat
    l = 0

    rows = -(-(b_lat + 1) // 8) * 8
    c_rows = jnp.zeros((rows, D), F32).at[:b_lat].set(c).at[b_lat].set(c_ctx)
    mod = _ada(c_rows, w_ada[l], b_ada[l][None, :])
    mod_all = mod[:b_lat + 1, None, :]
    mod_lat = mod_all[:b_lat]
    mod_ctx = mod_all[b_lat:]

    w_full = w_in[l]
    w_a = jnp.concatenate([_reorder_q_heads(w_full[:, :AW], 1), w_full[:, AW:A_WIDTH]], axis=1).astype(BF16)
    w_mg = w_full[:, A_WIDTH:].astype(BF16)
    gain = jnp.concatenate([jnp.tile(q_norm[l], N_Q_HEADS), jnp.tile(k_norm[l], N_KV_HEADS)])[None, :]
    head_of = np.arange(AW + KVW) // HD
    ind = jnp.asarray((head_of[:, None] == np.arange(LANES)[None, :]) / HD, BF16)
    w_ba = _reorder_q_heads(w_br_attn[l], 0).astype(BF16)
    w_bg = w_br_gla[l].astype(BF16)
    w_o = w_out[l].astype(BF16)
    w_r = jnp.zeros((D, LANES), F32).at[:, :N_GROUPS].set(router_group_w[l])
    w_r = w_r.at[:, N_GROUPS:N_GROUPS + N_EXP].set(router_expert_w[l]).astype(BF16)
    b_r = jnp.zeros((1, LANES), F32).at[0, :N_GROUPS].set(router_group_b[l])
    b_r = b_r.at[0, N_GROUPS:N_GROUPS + N_EXP].set(router_expert_b[l])
    wg = jnp.zeros((2, 2 * GATE_RANK, GKW), F32)
    wg = wg.at[0, :GATE_RANK].set(gla_w_gate[l, 0]).at[1, GATE_RANK:].set(gla_w_gate[l, 1])
    bg = gla_b_gate[l][:, None, :]
    gn = gla_norm[l][None, :]

    xc = x_prompt.reshape(t_ctx, D)
    xl = x_sample.reshape(t_lat, D)

    q_c, k_c, v_c, qg_c, kg_c, vg_c, rs_c, lr_c, kf_c, vf_c = _inproj(
        xc, mod_ctx, w_a, gain, ind, None, seq_ctx, latent=False)
    attn_c = _attention(q_c, k_c, v_c, None, seq_ctx)
    zero_state = jnp.zeros((1, 2, 2, LANES, LANES), F32)
    gla_c, sfin_c = _gla(qg_c, kg_c, vg_c, lr_c, rs_c, zero_state, wg, bg, gn, seq_ctx)

    q_l, k_l, v_l, qg_l, kg_l, vg_l, rs_l, lr_l = _inproj(
        xl, mod_lat, w_a, gain, ind, _rope_tables(seq_lat), seq_lat, latent=True)
    past = cache_k.shape[2]
    kc = cache_k[:, l].reshape(b_lat, past, KVW).astype(BF16)
    vc = cache_v[:, l].reshape(b_lat, past, KVW).astype(BF16)
    attn_l = _attention(q_l, k_l, v_l, (kc, vc), seq_lat)
    s0 = jnp.stack([_pair_states(state_gla_fwd[:, l]), _pair_states(state_gla_bwd[:, l])], axis=1)
    gla_l, _ = _gla(qg_l, kg_l, vg_l, lr_l, rs_l, s0, wg, bg, gn, seq_lat)

    x1, h2p, rt, ert, counts = _post(xc, xl, attn_c, attn_l, gla_c, gla_l, mod_all, seq_lat,
                                w_mg, w_ba, w_bg, w_o, ln1_g[l][None, :], ln1_b[l][None, :], w_r, b_r)

    pos0, pos1, starts, n_win = _route_tables(ert, counts)
    xs = _scatter_rows(h2p, pos0, pos1)
    ys = _experts(xs, starts, n_win, exp_w_gate[l], exp_w_up[l], exp_w_down[l])
    y_ctx, y_lat = _final(x1, rt, pos0, pos1, mod_all, ln2_g[l][None, :], ln2_b[l][None, :], ys,
                          t_ctx, seq_lat)

    untranspose = lambda a: a.reshape(b_ctx, 1, N_KV_HEADS, HD, seq_ctx).transpose(0, 1, 4, 2, 3)
    new_k, new_v = untranspose(kf_c), untranspose(vf_c)
    new_sf = _unpair_states(sfin_c[:, 0])[:, None]
    new_sb = _unpair_states(sfin_c[:, 1])[:, None]
    return (y_ctx.reshape(b_ctx, seq_ctx, D), y_lat.reshape(b_lat, seq_lat, D),
            new_k, new_v, new_sf, new_sb)
```

```python
import functools

import numpy as np
import jax
import jax.numpy as jnp
from jax import lax
from jax.experimental import pallas as pl
from jax.experimental.pallas import tpu as pltpu

F32 = jnp.float32
BF16 = jnp.bfloat16

D = 1024
GRID_W = 64
HD = 64
N_Q_HEADS = 8
N_KV_HEADS = 2
AW = N_Q_HEADS * HD
KVW = N_KV_HEADS * HD
ROPE_THETA = 10000.0
GLA_H = 4
GLA_DK = 64
GLA_DV = 128
GKW = GLA_H * GLA_DK
GVW = GLA_H * GLA_DV
GATE_RANK = 16
GLA_TAU = 16.0
CHUNK = 64
N_GROUPS = 4
EPG = 8
N_EXP = N_GROUPS * EPG
D_EXP = 256
DEPTH = 1
ALPHA = (2.0 * DEPTH) ** 0.25
LN_EPS = 1e-6
RMS_EPS = 1e-6

LANES = 128
A_WIDTH = AW + 2 * KVW + 2 * GKW + 2 * GVW + 2 * GATE_RANK
MG_WIDTH = 2 * D
TM_TOK = 1024
ROW_GROUP = 128
PROJ_AHEAD = 1
TM_PROJ = 1024
TM_POST = 512
LOG2E = 1.4426950408889634
ONES_ROWS = 16
TQ_LAT = 256
SCORE_AHEAD = 2
TM_EXP = 512
READ_AHEAD = 3
TS_ROWS = 2048
VMEM_LIMIT = 56 * 1024 * 1024


def _cparams(sem):
    return pltpu.CompilerParams(dimension_semantics=sem, vmem_limit_bytes=VMEM_LIMIT)


def _dot(a, b):
    return jnp.dot(a, b, preferred_element_type=F32)


def _dot_nt(a, b):
    return lax.dot_general(a, b, (((1,), (1,)), ((), ())), preferred_element_type=F32)


def _dot_tn(a, b):
    return lax.dot_general(a, b, (((0,), (0,)), ((), ())), preferred_element_type=F32)


def _ln(x):
    mu = jnp.mean(x, axis=-1, keepdims=True)
    xc = x - mu
    var = jnp.mean(xc * xc, axis=-1, keepdims=True)
    return xc * lax.rsqrt(var + LN_EPS)


def _silu(x):
    return x * jax.nn.sigmoid(x)


def _split_bf16(x):
    hi = x.astype(BF16)
    lo = (x - hi.astype(F32)).astype(BF16)
    return hi, lo


def _split3_bf16(x):
    hi = x.astype(BF16)
    r1 = x - hi.astype(F32)
    mid = r1.astype(BF16)
    lo = (r1 - mid.astype(F32)).astype(BF16)
    return hi, mid, lo


def _ada_kernel(c_ref, w_ref, b_ref, o_ref):
    rows = c_ref.shape[0]
    s_hi, s_mid, s_lo = _split3_bf16(_silu(c_ref[...]))
    w_hi, w_lo = _split_bf16(w_ref[...])
    a = _dot(jnp.concatenate([s_hi, s_mid, s_lo], axis=0), w_hi)
    b = _dot(jnp.concatenate([s_hi, s_mid], axis=0), w_lo)
    o_ref[...] = (a[0:rows] + a[rows:2 * rows] + a[2 * rows:3 * rows]
                  + b[0:rows] + b[rows:2 * rows] + b_ref[...])


def _ada(c_rows, w_ada, b_ada):
    rows = c_rows.shape[0]
    n = w_ada.shape[1]
    bn = 1024
    return pl.pallas_call(
        _ada_kernel,
        grid=(n // bn,),
        in_specs=[pl.BlockSpec((rows, D), lambda j: (0, 0)),
                  pl.BlockSpec((D, bn), lambda j: (0, j)),
                  pl.BlockSpec((1, bn), lambda j: (0, j))],
        out_specs=pl.BlockSpec((rows, bn), lambda j: (0, j)),
        out_shape=jax.ShapeDtypeStruct((rows, n), F32),
        compiler_params=_cparams(("arbitrary",)),
        name="ada",
    )(c_rows, w_ada, b_ada)


def _inproj_kernel(*refs, latent, seq):
    if latent:
        (x_ref, mod_ref, w_ref, gain_ref, ind_ref, cos_ref, sin_ref,
         q_ref, k_ref, v_ref, qg_ref, kg_ref, vg_ref, rs_ref, lr_ref) = refs
    else:
        (x_ref, mod_ref, w_ref, gain_ref, ind_ref,
         q_ref, k_ref, v_ref, qg_ref, kg_ref, vg_ref, rs_ref, lr_ref, kf_ref, vf_ref) = refs
    tm = x_ref.shape[0]
    sub = ROW_GROUP
    n_groups = tm // sub
    sh1 = mod_ref[0, :, 0:D]
    sc1 = mod_ref[0, :, D:2 * D]
    lane = lax.broadcasted_iota(jnp.int32, (sub, LANES), 1)
    low_half = lane < HD
    first = (lane % 32) < 16

    def project(g):
        rows = slice(g * sub, (g + 1) * sub)
        h = (_ln(x_ref[rows, :]) * (1.0 + sc1) + sh1).astype(BF16)
        return _dot(h, w_ref[...])

    def finish(g, res):
        rows = slice(g * sub, (g + 1) * sub)
        qk = res[:, 0:AW + KVW]
        hi, lo = _split_bf16(qk * qk)
        ms = _dot(hi, ind_ref[...]) + _dot(lo, ind_ref[...])
        r = lax.rsqrt(ms + RMS_EPS)
        for s in range(5):
            rb = jnp.where(low_half, r[:, 2 * s:2 * s + 1], r[:, 2 * s + 1:2 * s + 2])
            y = res[:, LANES * s:LANES * (s + 1)] * rb * gain_ref[:, LANES * s:LANES * (s + 1)]
            if s == 4 and not latent:
                kf_ref[(g * sub) // seq, :, (g * sub) % seq:(g * sub) % seq + sub] = y.T
            if latent:
                partner = jnp.where(first, pltpu.roll(y, LANES - 16, 1), pltpu.roll(y, 16, 1))
                y = y * cos_ref[rows, :] + partner * sin_ref[rows, :]
            if s < 4:
                q_ref[rows, LANES * s:LANES * (s + 1)] = (y * (HD ** -0.5 * LOG2E)).astype(BF16)
            else:
                k_ref[rows, :] = y.astype(BF16)
        o = AW + KVW
        v = res[:, o:o + KVW]
        v_ref[rows, :] = v.astype(BF16)
        if not latent:
            vf_ref[(g * sub) // seq, :, (g * sub) % seq:(g * sub) % seq + sub] = v.T
        o += KVW
        qg_ref[rows, :] = res[:, o:o + GKW] * (GLA_DK ** -0.5)
        o += GKW
        kg_ref[rows, :] = res[:, o:o + GKW]
        o += GKW
        vg_ref[rows, :] = res[:, o:o + GVW].astype(BF16)
        o += GVW
        rs_ref[rows, :] = _silu(res[:, o:o + GVW]).astype(BF16)
        o += GVW
        lr_ref[rows, :] = res[:, o:o + 2 * GATE_RANK]

    ahead = min(PROJ_AHEAD, n_groups)
    pending = {g: project(g) for g in range(ahead)}
    for g in range(n_groups):
        if g + ahead < n_groups:
            pending[g + ahead] = project(g + ahead)
        finish(g, pending.pop(g))


def _inproj(x2, mod, w_a, gain, ind, rope, seq, latent):
    t = x2.shape[0]
    tm = TM_PROJ
    per_seq = max(seq // tm, 1)
    per_tile = max(tm // seq, 1)
    row = lambda i: (i, 0)
    const = lambda i: (0, 0)
    in_specs = [pl.BlockSpec((tm, D), row),
                pl.BlockSpec((1, 1, 6 * D), (lambda i: (i // per_seq, 0, 0)) if latent else (lambda i: (0, 0, 0))),
                pl.BlockSpec((D, A_WIDTH), const),
                pl.BlockSpec((1, AW + KVW), const),
                pl.BlockSpec((AW + KVW, LANES), const)]
    args = [x2, mod, w_a, gain, ind]
    if latent:
        in_specs += [pl.BlockSpec((tm, LANES), lambda i: (i % per_seq, 0))] * 2
        args += list(rope)
    widths = [(AW, BF16), (KVW, BF16), (KVW, BF16), (GKW, F32), (GKW, F32), (GVW, BF16), (GVW, BF16),
              (2 * GATE_RANK, F32)]
    out_specs = [pl.BlockSpec((tm, w), row) for w, _ in widths]
    out_shape = [jax.ShapeDtypeStruct((t, w), dt) for w, dt in widths]
    if not latent:
        cache_spec = pl.BlockSpec((per_tile, KVW, seq), lambda i: (i, 0, 0))
        out_specs += [cache_spec] * 2
        out_shape += [jax.ShapeDtypeStruct((t // seq, KVW, seq), F32)] * 2
    return pl.pallas_call(
        functools.partial(_inproj_kernel, latent=latent, seq=seq),
        grid=(t // tm,),
        in_specs=in_specs,
        out_specs=out_specs,
        out_shape=out_shape,
        compiler_params=_cparams(("parallel",)),
        name="inproj_lat" if latent else "inproj_ctx",
    )(*args)


def _attn_kernel(*refs, has_cache):
    def transposed_with_ones(dst, src):
        dst[0:KVW, :] = src[...].astype(F32).T.astype(BF16)
        dst[KVW:, :] = jnp.ones((ONES_ROWS, dst.shape[1]), BF16)

    if has_cache:
        q_ref, k_ref, v_ref, kc_ref, vc_ref, o_ref, vt_s, vct_s = refs

        @pl.when(pl.program_id(1) == 0)
        def _():
            transposed_with_ones(vt_s, v_ref)
            transposed_with_ones(vct_s, vc_ref)
    else:
        q_ref, k_ref, v_ref, o_ref, vt_s = refs
        transposed_with_ones(vt_s, v_ref)
    tq = q_ref.shape[0]

    lane = lax.broadcasted_iota(jnp.int32, (tq, LANES), 1)
    low_half = lane < HD
    k = k_ref[...]
    def score(c):
        j, pair = divmod(c, 2)
        keep = low_half if j == 0 else jnp.logical_not(low_half)
        zero = jnp.zeros((tq, LANES), BF16)
        qs = jnp.concatenate([jnp.where(keep, q_ref[:, LANES * s:LANES * (s + 1)], zero)
                              for s in (2 * pair, 2 * pair + 1)], axis=0)
        return _dot_nt(k, qs), (_dot_nt(kc_ref[...], qs) if has_cache else None)

    n_chains = 2 * N_KV_HEADS
    ahead = SCORE_AHEAD if has_cache else n_chains
    scores = {c: score(c) for c in range(ahead)}
    outs = []
    for c in range(n_chains):
        if c + ahead < n_chains:
            scores[c + ahead] = score(c + ahead)
        s1, s2 = scores.pop(c)
        m = jnp.max(s1, axis=0, keepdims=True)
        if has_cache:
            m = jnp.maximum(m, jnp.max(s2, axis=0, keepdims=True))
        acc = _dot(vt_s[...], jnp.exp2(s1 - m).astype(BF16))
        if has_cache:
            acc = acc + _dot(vct_s[...], jnp.exp2(s2 - m).astype(BF16))
        outs.append(acc[0:KVW] / acc[KVW:KVW + 1])
    head0 = jnp.concatenate(outs[0:2], axis=1)
    head1 = jnp.concatenate(outs[2:4], axis=1)
    row = lax.broadcasted_iota(jnp.int32, (LANES, 4 * tq), 0)
    out = jnp.where(row < HD, head0, head1).T
    for s in range(4):
        o_ref[:, LANES * s:LANES * (s + 1)] = out[s * tq:(s + 1) * tq].astype(BF16)


def _attention(q, k, v, cache, seq):
    t = q.shape[0]
    if cache is None:
        tq = seq
        grid = (t // seq,)
        qmap = lambda b: (b, 0)
        in_specs = [pl.BlockSpec((tq, AW), qmap), pl.BlockSpec((seq, KVW), qmap),
                    pl.BlockSpec((seq, KVW), qmap)]
        args = [q, k, v]
        scratch = [pltpu.VMEM((KVW + ONES_ROWS, seq), BF16)]
        sem = ("parallel",)
        name = "attn_ctx"
    else:
        tq = TQ_LAT
        nq = seq // tq
        kc, vc = cache
        past = kc.shape[1]
        grid = (t // seq, nq)
        qmap = lambda b, i: (b * nq + i, 0)
        kmap = lambda b, i: (b, 0)
        cmap = lambda b, i: (b, 0, 0)
        in_specs = [pl.BlockSpec((tq, AW), qmap), pl.BlockSpec((seq, KVW), kmap),
                    pl.BlockSpec((seq, KVW), kmap),
                    pl.BlockSpec((None, past, KVW), cmap), pl.BlockSpec((None, past, KVW), cmap)]
        args = [q, k, v, kc, vc]
        scratch = [pltpu.VMEM((KVW + ONES_ROWS, seq), BF16), pltpu.VMEM((KVW + ONES_ROWS, past), BF16)]
        sem = ("parallel", "arbitrary")
        name = "attn_lat"
    return pl.pallas_call(
        functools.partial(_attn_kernel, has_cache=cache is not None),
        grid=grid,
        in_specs=in_specs,
        out_specs=pl.BlockSpec((tq, AW), qmap),
        out_shape=jax.ShapeDtypeStruct((t, AW), BF16),
        scratch_shapes=scratch,
        compiler_params=_cparams(sem),
        name=name,
    )(*args)


GLA_BLK = 256
GLA_ROWS = 2048
GLA_UNROLL = 8


def _gla_kernel(qg_ref, kg_ref, vg_ref, lr_ref, rs_ref, s0_ref, wg_ref, bg_ref, gn_ref,
                o_ref, sfin_ref, cum_s, kv_s, dec_s, *, seqs):
    n = qg_ref.shape[0]
    nc = n // CHUNK
    nc_seq = nc // seqs
    unroll = min(GLA_UNROLL, nc)
    lane = lax.broadcasted_iota(jnp.int32, (CHUNK, LANES), 1)
    low_half = lane < GLA_DK
    lane_sq = lax.broadcasted_iota(jnp.int32, (LANES, LANES), 1)
    low_half_sq = lane_sq < GLA_DK
    ri = lax.broadcasted_iota(jnp.int32, (2 * CHUNK, 2 * CHUNK), 0)
    ci = lax.broadcasted_iota(jnp.int32, (2 * CHUNK, 2 * CHUNK), 1)
    diag = (ri >> 6) == (ci >> 6)
    keep2 = (diag & (ri >= ci), diag & (ci >= ri))

    rb = lax.broadcasted_iota(jnp.int32, (GLA_BLK, GLA_BLK), 0)
    cb = lax.broadcasted_iota(jnp.int32, (GLA_BLK, GLA_BLK), 1)
    same = (rb >> 6) == (cb >> 6)
    tri = ((same & (rb >= cb)).astype(BF16), (same & (cb >= rb)).astype(BF16))
    for d in range(2):
        w_hi, w_lo = _split_bf16(wg_ref[d])
        for blk in range(n // GLA_BLK):
            rows = slice(blk * GLA_BLK, (blk + 1) * GLA_BLK)
            l_hi, l_lo = _split_bf16(lr_ref[rows, :])
            z = _dot(l_hi, w_hi) + _dot(l_lo, w_hi) + _dot(l_hi, w_lo) + bg_ref[d]
            logg = (jnp.minimum(z, 0.0) - jnp.log(1.0 + jnp.exp(-jnp.abs(z)))) * (1.0 / GLA_TAU)
            pieces = _split3_bf16(logg)
            cum_s[d, rows, :] = _dot(tri[d], pieces[0]) + _dot(tri[d], pieces[1]) + _dot(tri[d], pieces[2])

    def pass1(i, carry):
        cs = [i * unroll + u for u in range(unroll)]
        rws = [pl.ds(pl.multiple_of(c * CHUNK, CHUNK), CHUNK) for c in cs]
        prods = []
        for c, rows in zip(cs, rws):
            kc = kg_ref[rows, :]
            kdec = []
            for d in range(2):
                cum = cum_s[d, rows, :]
                last = cum[CHUNK - 1:CHUNK, :] if d == 0 else cum[0:1, :]
                dec_s[d, c] = jnp.exp(last)
                kdec.append((kc * jnp.exp(last - cum)).astype(BF16))
            for p in range(2):
                vpair = vg_ref[rows, GLA_DV * 2 * p:GLA_DV * 2 * (p + 1)]
                kpair = jnp.concatenate([kdec[0][:, LANES * p:LANES * (p + 1)],
                                         kdec[1][:, LANES * p:LANES * (p + 1)]], axis=1)
                prods.append(_dot_tn(vpair, kpair))
        for j, c in enumerate(cs):
            for p in range(2):
                res = prods[2 * j + p]
                for d in range(2):
                    cols = slice(LANES * d, LANES * (d + 1))
                    kv_s[d, c, p] = jnp.where(low_half_sq, res[0:GLA_DV, cols], res[GLA_DV:2 * GLA_DV, cols])
        return carry

    lax.fori_loop(0, nc // unroll, pass1, 0)

    for s in range(seqs):
        for d in range(2):
            def scan(i, st, s=s, d=d):
                c = s * nc_seq + (i if d == 0 else nc_seq - 1 - i)
                dec = dec_s[d, c]
                new = []
                for p in range(2):
                    kv = kv_s[d, c, p]
                    kv_s[d, c, p] = st[p]
                    new.append(st[p] * dec[:, LANES * p:LANES * (p + 1)] + kv)
                return tuple(new)

            s_in = s if s0_ref.shape[0] > 1 else 0
            fin = lax.fori_loop(0, nc_seq, scan, (s0_ref[s_in, d, 0], s0_ref[s_in, d, 1]))
            sfin_ref[s, d, 0] = fin[0]
            sfin_ref[s, d, 1] = fin[1]

    def pass3(i, carry):
        cs = [i * unroll + u for u in range(unroll)]
        rws = [pl.ds(pl.multiple_of(c * CHUNK, CHUNK), CHUNK) for c in cs]
        first = []
        for c, rows in zip(cs, rws):
            q = qg_ref[rows, :]
            k = kg_ref[rows, :]
            for d in range(2):
                cum = cum_s[d, rows, :]
                qt = q * jnp.exp(cum)
                kt = (k * jnp.exp(-cum)).astype(BF16)
                for p in range(2):
                    qs = qt[:, LANES * p:LANES * (p + 1)]
                    lhs = jnp.concatenate([jnp.where(low_half, qs, 0.0), jnp.where(low_half, 0.0, qs)],
                                          axis=0).astype(BF16)
                    kts = kt[:, LANES * p:LANES * (p + 1)]
                    rhs = jnp.concatenate([kts, kts, kv_s[d, c, p].astype(BF16)], axis=0)
                    first.append(_dot_nt(lhs, rhs))
        second = []
        for j, rows in enumerate(rws):
            for d in range(2):
                for p in range(2):
                    res = first[4 * j + 2 * d + p]
                    vp = jnp.concatenate([vg_ref[rows, GLA_DV * (2 * p):GLA_DV * (2 * p + 1)],
                                          vg_ref[rows, GLA_DV * (2 * p + 1):GLA_DV * (2 * p + 2)]], axis=0)
                    a = jnp.where(keep2[d], res[:, 0:2 * CHUNK], 0.0).astype(BF16)
                    second.append(_dot(a, vp) + res[:, 2 * CHUNK:])
        for j, rows in enumerate(rws):
            for p in range(2):
                tot = second[4 * j + p] + second[4 * j + 2 + p]
                y = tot * lax.rsqrt(jnp.mean(tot * tot, axis=-1, keepdims=True) + RMS_EPS) * gn_ref[...]
                for hh in range(2):
                    cols = slice(GLA_DV * (2 * p + hh), GLA_DV * (2 * p + hh + 1))
                    o_ref[rows, cols] = (y[CHUNK * hh:CHUNK * (hh + 1)]
                                         * rs_ref[rows, cols].astype(F32)).astype(BF16)
        return carry

    lax.fori_loop(0, nc // unroll, pass3, 0)


def _gla(qg, kg, vg, lr, rs, s0, wg, bg, gn, seq):
    t = qg.shape[0]
    nb = t // seq
    seqs = max(GLA_ROWS // seq, 1)
    rows = seqs * seq
    nc = rows // CHUNK
    row = lambda b: (b, 0)
    c3 = lambda b: (0, 0, 0)
    st = lambda b: (b, 0, 0, 0, 0)
    if s0.shape[0] == 1:
        s0_spec = pl.BlockSpec((1, 2, 2, LANES, LANES), lambda b: (0, 0, 0, 0, 0))
    else:
        s0_spec = pl.BlockSpec((seqs, 2, 2, LANES, LANES), st)
    return pl.pallas_call(
        functools.partial(_gla_kernel, seqs=seqs),
        grid=(nb // seqs,),
        in_specs=[pl.BlockSpec((rows, GKW), row), pl.BlockSpec((rows, GKW), row),
                  pl.BlockSpec((rows, GVW), row), pl.BlockSpec((rows, 2 * GATE_RANK), row),
                  pl.BlockSpec((rows, GVW), row),
                  s0_spec,
                  pl.BlockSpec((2, 2 * GATE_RANK, GKW), c3), pl.BlockSpec((2, 1, GKW), c3),
                  pl.BlockSpec((1, GLA_DV), lambda b: (0, 0))],
        out_specs=[pl.BlockSpec((rows, GVW), row),
                   pl.BlockSpec((seqs, 2, 2, LANES, LANES), st)],
        out_shape=[jax.ShapeDtypeStruct((t, GVW), BF16),
                   jax.ShapeDtypeStruct((nb, 2, 2, LANES, LANES), F32)],
        scratch_shapes=[pltpu.VMEM((2, rows, GKW), F32),
                        pltpu.VMEM((2, nc, 2, LANES, LANES), F32),
                        pltpu.VMEM((2, nc, 1, GKW), F32)],
        compiler_params=_cparams(("parallel",)),
        name="gla_lat" if seq > 256 else "gla_ctx",
    )(qg, kg, vg, lr, rs, s0, wg, bg, gn)


ROW_SUB = D // LANES
ROW_DTYPE = BF16


def _store_row_slabs(ref, x, tmp):
    m = x.shape[0]
    for c in range(ROW_SUB):
        tmp[pl.ds(c, m, stride=ROW_SUB), :] = x[:, LANES * c:LANES * (c + 1)]
    ref[...] = tmp[...].reshape(m, ROW_SUB, LANES).astype(ROW_DTYPE)


def _load_row_slabs(ref, tmp):
    m = ref.shape[0]
    tmp[...] = ref[...].astype(F32).reshape(m * ROW_SUB, LANES)
    return jnp.concatenate([tmp[pl.ds(c, m, stride=ROW_SUB), :] for c in range(ROW_SUB)], axis=1)


def _row_slab(ref, row):
    return ref.at[pl.ds(row, 1)]


def _post_kernel(xc_ref, xl_ref, ac_ref, al_ref, gc_ref, gl_ref, mod_ref,
                 wmg_ref, wba_ref, wbg_ref, wo_ref, l1g_ref, l1b_ref, wr_ref, br_ref,
                 x1_ref, h2_ref, rt_ref, ert_ref, cnt_ref, run_s, slab_s, logit_s, *, n_ctx_tiles):
    i = pl.program_id(0)
    tm = xc_ref.shape[0]
    sub = ROW_GROUP
    n_groups = tm // sub
    is_ctx = i < n_ctx_tiles
    parts = [slice(g * sub, (g + 1) * sub) for g in range(n_groups)]

    @pl.when(i == 0)
    def _():
        run_s[...] = jnp.zeros_like(run_s)
        logit_s[...] = jnp.zeros_like(logit_s)

    sh1 = mod_ref[0, :, 0:D]
    sc1 = mod_ref[0, :, D:2 * D]
    g1 = mod_ref[0, :, 2 * D:3 * D]
    sh2 = mod_ref[0, :, 3 * D:4 * D]
    sc2 = mod_ref[0, :, 4 * D:5 * D]

    xs = [jnp.where(is_ctx, xc_ref[r, :], xl_ref[r, :]) for r in parts]
    hs = [(_ln(x) * (1.0 + sc1) + sh1).astype(BF16) for x in xs]
    gates = [jax.nn.sigmoid(_dot(h, wmg_ref[...])) for h in hs]
    ba = [_dot(jnp.where(is_ctx, ac_ref[r, :], al_ref[r, :]), wba_ref[...]) for r in parts]
    bg = [_dot(jnp.where(is_ctx, gc_ref[r, :], gl_ref[r, :]), wbg_ref[...]) for r in parts]

    ri = lax.broadcasted_iota(jnp.int32, (sub, sub), 0)
    ci = lax.broadcasted_iota(jnp.int32, (sub, sub), 1)
    earlier = (ri > ci).astype(BF16)
    counted = (i > 0).astype(F32)
    run = run_s[0:1, :]
    for r in parts:
        run = _route(logit_s[r, :], earlier, run, counted, rt_ref.at[r, :], ert_ref.at[:, r])
    run_s[0:1, :] = run
    cnt_ref[...] = jnp.broadcast_to(run, cnt_ref.shape)

    merged = [(g[:, :D] * a + g[:, D:] * b).astype(BF16) for g, a, b in zip(gates, ba, bg)]
    mix = [_dot(m, wo_ref[...]) for m in merged]
    x1s = [_ln(ALPHA * x + g1 * m) * l1g_ref[...] + l1b_ref[...] for x, m in zip(xs, mix)]
    h2s = [_ln(x1) * (1.0 + sc2) + sh2 for x1 in x1s]
    logits = [_dot(h2.astype(BF16), wr_ref[...]) + br_ref[...] for h2 in h2s]
    for g, r in enumerate(parts):
        x1_ref[r, :] = x1s[g]
        _store_row_slabs(h2_ref.at[pl.ds(g * sub, sub)], h2s[g], slab_s)
        logit_s[r, :] = logits[g]


def _route(logit, earlier, run, counted, rt_ref, ert_ref):
    tm = logit.shape[0]
    lane_i = lax.broadcasted_iota(jnp.int32, (tm, LANES), 1)
    lane = lane_i.astype(F32)
    lane_grp = ((lane_i - N_GROUPS) >> 3).astype(F32)
    neg = jnp.float32(-jnp.inf)
    far = jnp.float32(LANES)
    is_g = lane_i < N_GROUPS
    lg = jnp.where(is_g, logit, neg)
    mg = jnp.max(lg, axis=-1, keepdims=True)
    pg_top = 1.0 / jnp.sum(jnp.where(is_g, jnp.exp(logit - mg), 0.0), axis=-1, keepdims=True)
    g_idx = jnp.min(jnp.where(lg == mg, lane, far), axis=-1, keepdims=True)
    in_grp = (lane_i >= N_GROUPS) & (lane_i < N_GROUPS + N_EXP) & (lane_grp == g_idx)
    le = jnp.where(in_grp, logit, neg)
    v1 = jnp.max(le, axis=-1, keepdims=True)
    i1 = jnp.min(jnp.where(le == v1, lane, far), axis=-1, keepdims=True)
    le2 = jnp.where(lane == i1, neg, le)
    v2 = jnp.max(le2, axis=-1, keepdims=True)
    i2 = jnp.min(jnp.where(le2 == v2, lane, far), axis=-1, keepdims=True)
    e1 = i1 - N_GROUPS
    e2 = i2 - N_GROUPS
    tt = jnp.exp(v2 - v1)
    w1 = pg_top / (1.0 + tt)
    w2 = pg_top * tt / (1.0 + tt)

    hot = ((lane == e1) | (lane == e2)).astype(F32)
    before = _dot(earlier, hot.astype(BF16)) + run
    r1 = jnp.sum(jnp.where(lane == e1, before, 0.0), axis=-1, keepdims=True)
    r2 = jnp.sum(jnp.where(lane == e2, before, 0.0), axis=-1, keepdims=True)

    rt = jnp.where(lane_i == 0, e1, 0.0)
    rt = jnp.where(lane_i == 1, e2, rt)
    rt = jnp.where(lane_i == 2, w1, rt)
    rt = jnp.where(lane_i == 3, w2, rt)
    rt = jnp.where(lane_i == 4, r1, rt)
    rt = jnp.where(lane_i == 5, r2, rt)
    rt_ref[...] = rt
    ert_ref[...] = rt.T[0:8, :]
    return run + counted * jnp.sum(hot, axis=0, keepdims=True)


def _post(x_ctx, x_lat, a_ctx, a_lat, g_ctx, g_lat, mod_all, seq_lat,
          w_mg, w_ba, w_bg, w_o, l1g, l1b, w_r, b_r):
    t_ctx, t_lat = x_ctx.shape[0], x_lat.shape[0]
    tm = TM_POST
    nct, nlt = t_ctx // tm, t_lat // tm
    per_seq = seq_lat // tm
    nb_lat = t_lat // seq_lat
    t = t_ctx + t_lat
    n = nct + nlt
    lat = lambda i: jnp.clip(i - nct, 0, nlt - 1)
    cmap = lambda i: (jnp.minimum(i, nct - 1), 0)
    lmap = lambda i: (lat(i), 0)
    mmap = lambda i: (jnp.where(i < nct, nb_lat, lat(i) // per_seq), 0, 0)
    row = lambda i: (jnp.minimum(i, n - 1), 0)
    prev = lambda i: jnp.maximum(i - 1, 0)
    const = lambda i: (0, 0)
    return pl.pallas_call(
        functools.partial(_post_kernel, n_ctx_tiles=nct),
        grid=(n + 1,),
        in_specs=[pl.BlockSpec((tm, D), cmap), pl.BlockSpec((tm, D), lmap),
                  pl.BlockSpec((tm, AW), cmap), pl.BlockSpec((tm, AW), lmap),
                  pl.BlockSpec((tm, GVW), cmap), pl.BlockSpec((tm, GVW), lmap),
                  pl.BlockSpec((1, 1, 6 * D), mmap),
                  pl.BlockSpec((D, MG_WIDTH), const), pl.BlockSpec((AW, D), const),
                  pl.BlockSpec((GVW, D), const), pl.BlockSpec((D, D), const),
                  pl.BlockSpec((1, D), const), pl.BlockSpec((1, D), const),
                  pl.BlockSpec((D, LANES), const), pl.BlockSpec((1, LANES), const)],
        out_specs=[pl.BlockSpec((tm, D), row),
                   pl.BlockSpec((tm, ROW_SUB, LANES), lambda i: (jnp.minimum(i, n - 1), 0, 0)),
                   pl.BlockSpec((tm, LANES), lambda i: (prev(i), 0)),
                   pl.BlockSpec((8, tm), lambda i: (0, prev(i))),
                   pl.BlockSpec((8, LANES), const)],
        out_shape=[jax.ShapeDtypeStruct((t, D), F32), jax.ShapeDtypeStruct((t, ROW_SUB, LANES), ROW_DTYPE),
                   jax.ShapeDtypeStruct((t, LANES), F32), jax.ShapeDtypeStruct((8, t), F32),
                   jax.ShapeDtypeStruct((8, LANES), F32)],
        scratch_shapes=[pltpu.VMEM((8, LANES), F32), pltpu.VMEM((ROW_GROUP * ROW_SUB, LANES), F32),
                        pltpu.VMEM((tm, LANES), F32)],
        compiler_params=_cparams(("arbitrary",)),
        name="post",
    )(x_ctx, x_lat, a_ctx, a_lat, g_ctx, g_lat, mod_all, w_mg, w_ba, w_bg, w_o, l1g, l1b, w_r, b_r)


ROW_UNROLL = 8


def _row_copy(src_ref, dst_ref, sem):
    return pltpu.make_async_copy(src_ref, dst_ref, sem)


def _scatter_kernel(pos0_ref, pos1_ref, h_ref, xs_ref, sem, zero_s, zero_sem):
    ts = h_ref.shape[0]
    n_rows = xs_ref.shape[0] - TM_EXP

    @pl.when(pl.program_id(0) == 0)
    def _():
        zero_s[...] = jnp.zeros_like(zero_s)
        pad = _row_copy(zero_s, xs_ref.at[pl.ds(n_rows, TM_EXP)], zero_sem)
        pad.start()
        pad.wait()

    def issue(g, carry):
        r0 = pl.multiple_of(g * ROW_UNROLL, ROW_UNROLL)
        for k in range(ROW_UNROLL):
            src = _row_slab(h_ref, r0 + k)
            _row_copy(src, _row_slab(xs_ref, pos0_ref[0, r0 + k]), sem).start(priority=0)
            _row_copy(src, _row_slab(xs_ref, pos1_ref[0, r0 + k]), sem).start(priority=1)
        return carry

    lax.fori_loop(0, ts // ROW_UNROLL, issue, 0)
    for _ in range(2):
        _row_copy(h_ref, xs_ref.at[pl.ds(0, ts)], sem).wait()


def _scatter_rows(h2p, pos0, pos1):
    t = h2p.shape[0]
    ts = TS_ROWS
    smem = lambda: pl.BlockSpec((None, 1, ts), lambda i: (i, 0, 0), memory_space=pltpu.SMEM)
    return pl.pallas_call(
        _scatter_kernel,
        grid=(t // ts,),
        in_specs=[smem(), smem(), pl.BlockSpec((ts, ROW_SUB, LANES), lambda i: (i, 0, 0))],
        out_specs=pl.BlockSpec(memory_space=pl.ANY),
        out_shape=jax.ShapeDtypeStruct((2 * t + TM_EXP, ROW_SUB, LANES), ROW_DTYPE),
        scratch_shapes=[pltpu.SemaphoreType.DMA(()), pltpu.VMEM((TM_EXP, ROW_SUB, LANES), ROW_DTYPE),
                        pltpu.SemaphoreType.DMA(())],
        compiler_params=_cparams(("arbitrary",)),
        name="scatter",
    )(pos0.reshape(t // ts, 1, ts), pos1.reshape(t // ts, 1, ts), h2p)


def _expert_kernel(start_ref, nwin_ref, xs_ref, wg_ref, wu_ref, wd_ref, ys_ref,
                   wgu_s, wd_s, in_buf, out_buf, slab_s, in_sem, out_sem):
    e = pl.program_id(0)
    tm = TM_EXP
    wgu_s[:, 0:D_EXP] = wg_ref[...].astype(BF16)
    wgu_s[:, D_EXP:2 * D_EXP] = wu_ref[...].astype(BF16)
    wd_s[...] = wd_ref[...].astype(BF16)
    base = start_ref[e]
    n = nwin_ref[e]

    @pl.when(e == 0)
    def _():
        out_buf[0] = jnp.zeros(out_buf.shape[1:], ROW_DTYPE)
        pad = pltpu.make_async_copy(out_buf.at[0], ys_ref.at[pl.ds(ys_ref.shape[0] - tm, tm)], out_sem.at[0])
        pad.start()
        pad.wait()

    def read(w, slot, first_row=None):
        first_row = base if first_row is None else first_row
        return pltpu.make_async_copy(xs_ref.at[pl.ds(first_row + w * tm, tm)], in_buf.at[slot], in_sem.at[slot])

    def write(w, slot):
        return pltpu.make_async_copy(out_buf.at[slot], ys_ref.at[pl.ds(base + w * tm, tm)], out_sem.at[slot])

    def start_first_reads(expert):
        for a in range(READ_AHEAD):
            @pl.when(nwin_ref[expert] > a)
            def _(a=a):
                read(a, a, start_ref[expert]).start()

    @pl.when(e == 0)
    def _():
        start_first_reads(0)

    def body(w, carry):
        slot = w % 2
        rslot = w % (READ_AHEAD + 1)

        @pl.when(w + READ_AHEAD < n)
        def _():
            read(w + READ_AHEAD, (w + READ_AHEAD) % (READ_AHEAD + 1)).start()

        read(w, rslot).wait()

        @pl.when(w >= 2)
        def _():
            write(w - 2, slot).wait()

        gu = _dot(_load_row_slabs(in_buf.at[rslot], slab_s).astype(BF16), wgu_s[...])
        hid = _silu(gu[:, 0:D_EXP]) * gu[:, D_EXP:2 * D_EXP]
        _store_row_slabs(out_buf.at[slot], _dot(hid.astype(BF16), wd_s[...]), slab_s)
        write(w, slot).start()
        return carry

    lax.fori_loop(0, n, body, 0)

    @pl.when(e + 1 < pl.num_programs(0))
    def _():
        start_first_reads(e + 1)

    @pl.when(n >= 1)
    def _():
        write(n - 1, (n - 1) % 2).wait()

    @pl.when(n >= 2)
    def _():
        write(n - 2, n % 2).wait()


def _experts(xs, starts, n_win, w_gate, w_up, w_down):
    tm = TM_EXP
    wmap = lambda e, st, nw: (e, 0, 0)
    slab = (tm, ROW_SUB, LANES)
    return pl.pallas_call(
        _expert_kernel,
        grid_spec=pltpu.PrefetchScalarGridSpec(
            num_scalar_prefetch=2,
            grid=(N_EXP,),
            in_specs=[pl.BlockSpec(memory_space=pl.ANY),
                      pl.BlockSpec((None, D, D_EXP), wmap), pl.BlockSpec((None, D, D_EXP), wmap),
                      pl.BlockSpec((None, D_EXP, D), wmap)],
            out_specs=pl.BlockSpec(memory_space=pl.ANY),
            scratch_shapes=[pltpu.VMEM((D, 2 * D_EXP), BF16), pltpu.VMEM((D_EXP, D), BF16),
                            pltpu.VMEM((READ_AHEAD + 1,) + slab, ROW_DTYPE), pltpu.VMEM((2,) + slab, ROW_DTYPE),
                            pltpu.VMEM((tm * ROW_SUB, LANES), F32),
                            pltpu.SemaphoreType.DMA((READ_AHEAD + 1,)), pltpu.SemaphoreType.DMA((2,))]),
        out_shape=jax.ShapeDtypeStruct(xs.shape, ROW_DTYPE),
        compiler_params=_cparams(("arbitrary",)),
        name="experts",
    )(starts, n_win, xs, w_gate, w_up, w_down)


def _final_kernel(p0c_ref, p1c_ref, p0n_ref, p1n_ref, x1_ref, rt_ref, mod_ref, l2g_ref, l2b_ref, ys_ref,
                  oc_ref, ol_ref, buf, sem, slab_s, *, n_ctx_tiles):
    i = pl.program_id(0)
    n = pl.num_programs(0)
    tm = x1_ref.shape[0]

    def gather(p0_ref, p1_ref, slot):
        def issue(g, carry):
            r0 = pl.multiple_of(g * ROW_UNROLL, ROW_UNROLL)
            for k in range(ROW_UNROLL):
                _row_copy(_row_slab(ys_ref, p0_ref[0, r0 + k]),
                          _row_slab(buf.at[slot, 0], r0 + k), sem.at[slot]).start(priority=0)
                _row_copy(_row_slab(ys_ref, p1_ref[0, r0 + k]),
                          _row_slab(buf.at[slot, 1], r0 + k), sem.at[slot]).start(priority=1)
            return carry

        lax.fori_loop(0, tm // ROW_UNROLL, issue, 0)

    cur = i % 2

    @pl.when(i == 0)
    def _():
        gather(p0c_ref, p1c_ref, 0)

    @pl.when(i + 1 < n)
    def _():
        gather(p0n_ref, p1n_ref, 1 - cur)

    for k in range(2):
        _row_copy(ys_ref.at[pl.ds(0, tm)], buf.at[cur, k], sem.at[cur]).wait()

    g2 = mod_ref[0, :, 5 * D:6 * D]
    w1 = rt_ref[:, 2:3]
    w2 = rt_ref[:, 3:4]
    moe = (w1 * _load_row_slabs(buf.at[cur, 0], slab_s.at[0])
           + w2 * _load_row_slabs(buf.at[cur, 1], slab_s.at[1]))
    out = _ln(ALPHA * x1_ref[...] + g2 * moe) * l2g_ref[...] + l2b_ref[...]

    @pl.when(i < n_ctx_tiles)
    def _():
        oc_ref[...] = out

    @pl.when(i >= n_ctx_tiles)
    def _():
        ol_ref[...] = out


def _final(x1, rt, pos0, pos1, mod_all, l2g, l2b, ys, t_ctx, seq_lat):
    t = x1.shape[0]
    tm = TM_TOK
    nt = t // tm
    nct = t_ctx // tm
    t_lat = t - t_ctx
    per_seq = seq_lat // tm
    nb_lat = t_lat // seq_lat
    p0 = pos0.reshape(nt, 1, tm)
    p1 = pos1.reshape(nt, 1, tm)
    row = lambda i: (i, 0)
    const = lambda i: (0, 0)
    mmap = lambda i: (jnp.where(i < nct, nb_lat, jnp.maximum(i - nct, 0) // per_seq), 0, 0)
    smem_cur = lambda: pl.BlockSpec((None, 1, tm), lambda i: (i, 0, 0), memory_space=pltpu.SMEM)
    smem_nxt = lambda: pl.BlockSpec((None, 1, tm), lambda i: (jnp.minimum(i + 1, nt - 1), 0, 0),
                                    memory_space=pltpu.SMEM)
    return pl.pallas_call(
        functools.partial(_final_kernel, n_ctx_tiles=nct),
        grid=(nt,),
        in_specs=[smem_cur(), smem_cur(), smem_nxt(), smem_nxt(),
                  pl.BlockSpec((tm, D), row), pl.BlockSpec((tm, LANES), row),
                  pl.BlockSpec((1, 1, 6 * D), mmap),
                  pl.BlockSpec((1, D), const), pl.BlockSpec((1, D), const),
                  pl.BlockSpec(memory_space=pl.ANY)],
        out_specs=[pl.BlockSpec((tm, D), lambda i: (jnp.minimum(i, nct - 1), 0)),
                   pl.BlockSpec((tm, D), lambda i: (jnp.maximum(i - nct, 0), 0))],
        out_shape=[jax.ShapeDtypeStruct((t_ctx, D), F32), jax.ShapeDtypeStruct((t_lat, D), F32)],
        scratch_shapes=[pltpu.VMEM((2, 2, tm, ROW_SUB, LANES), ROW_DTYPE), pltpu.SemaphoreType.DMA((2,)),
                        pltpu.VMEM((2, tm * ROW_SUB, LANES), F32)],
        compiler_params=_cparams(("arbitrary",)),
        name="final",
    )(p0, p1, p0, p1, x1, rt, mod_all, l2g, l2b, ys)


def _reorder_q_heads(w, axis):
    shape = w.shape
    split = shape[:axis] + (N_KV_HEADS, N_Q_HEADS // N_KV_HEADS, HD) + shape[axis + 1:]
    return jnp.swapaxes(w.reshape(split), axis, axis + 1).reshape(shape)


def _rope_tables(seq):
    t = np.arange(seq)
    half = HD // 4
    inv = (ROPE_THETA ** (-np.arange(half, dtype=np.float64) / half)).astype(np.float32)
    d64 = np.arange(LANES) % HD
    pos = np.where((d64 < HD // 2)[None, :], (t // GRID_W)[:, None], (t % GRID_W)[:, None])
    ang = (pos.astype(np.float32) * inv[d64 % half][None, :]).astype(np.float64)
    sign = np.where((d64 % 32) < 16, -1.0, 1.0)
    return (jnp.asarray(np.cos(ang), F32), jnp.asarray(np.sin(ang) * sign[None, :], F32))


def _pair_states(s):
    b = s.shape[0]
    s = s.reshape(b, 2, 2, GLA_DK, GLA_DV)
    return s.transpose(0, 1, 4, 2, 3).reshape(b, 2, GLA_DV, 2 * GLA_DK)


def _unpair_states(s):
    b = s.shape[0]
    s = s.reshape(b, 2, GLA_DV, 2, GLA_DK)
    return s.transpose(0, 1, 3, 4, 2).reshape(b, GLA_H, GLA_DK, GLA_DV)


def _route_tables(ert, counts):
    i32 = jnp.int32
    cnt = counts[0, :N_EXP].astype(i32)
    starts = jnp.cumsum(cnt) - cnt
    table = lambda e: jnp.sum(jnp.where(e[None, :] == jnp.arange(N_EXP, dtype=i32)[:, None],
                                        starts[:, None], 0), axis=0)
    pos0 = table(ert[0].astype(i32)) + ert[4].astype(i32)
    pos1 = table(ert[1].astype(i32)) + ert[5].astype(i32)
    return pos0, pos1, starts, (cnt + (TM_EXP - 1)) // TM_EXP


def kernel(x_prompt, x_sample, cache_k, cache_v, state_gla_fwd, state_gla_bwd, c, c_ctx, w_ada, b_ada, w_in, q_norm, k_norm, gla_w_gate, gla_b_gate, gla_norm, w_br_attn, w_br_gla, w_out, ln1_g, ln1_b, router_group_w, router_group_b, router_expert_w, router_expert_b, exp_w_gate, exp_w_up, exp_w_down, ln2_g, ln2_b):
    b_ctx, seq_ctx, _ = x_prompt.shape
    b_lat, seq_lat, _ = x_sample.shape
    t_ctx, t_lat = b_ctx * seq_ctx, b_lat * seq_lat
    t = t_ctx + t_lat
    l = 0

    rows = -(-(b_lat + 1) // 8) * 8
    c_rows = jnp.zeros((rows, D), F32).at[:b_lat].set(c).at[b_lat].set(c_ctx)
    mod = _ada(c_rows, w_ada[l], b_ada[l][None, :])
    mod_all = mod[:b_lat + 1, None, :]
    mod_lat = mod_all[:b_lat]
    mod_ctx = mod_all[b_lat:]

    w_full = w_in[l]
    w_a = jnp.concatenate([_reorder_q_heads(w_full[:, :AW], 1), w_full[:, AW:A_WIDTH]], axis=1).astype(BF16)
    w_mg = w_full[:, A_WIDTH:].astype(BF16)
    gain = jnp.concatenate([jnp.tile(q_norm[l], N_Q_HEADS), jnp.tile(k_norm[l], N_KV_HEADS)])[None, :]
    head_of = np.arange(AW + KVW) // HD
    ind = jnp.asarray((head_of[:, None] == np.arange(LANES)[None, :]) / HD, BF16)
    w_ba = _reorder_q_heads(w_br_attn[l], 0).astype(BF16)
    w_bg = w_br_gla[l].astype(BF16)
    w_o = w_out[l].astype(BF16)
    w_r = jnp.zeros((D, LANES), F32).at[:, :N_GROUPS].set(router_group_w[l])
    w_r = w_r.at[:, N_GROUPS:N_GROUPS + N_EXP].set(router_expert_w[l]).astype(BF16)
    b_r = jnp.zeros((1, LANES), F32).at[0, :N_GROUPS].set(router_group_b[l])
    b_r = b_r.at[0, N_GROUPS:N_GROUPS + N_EXP].set(router_expert_b[l])
    wg = jnp.zeros((2, 2 * GATE_RANK, GKW), F32)
    wg = wg.at[0, :GATE_RANK].set(gla_w_gate[l, 0]).at[1, GATE_RANK:].set(gla_w_gate[l, 1])
    bg = gla_b_gate[l][:, None, :]
    gn = gla_norm[l][None, :]

    xc = x_prompt.reshape(t_ctx, D)
    xl = x_sample.reshape(t_lat, D)

    q_c, k_c, v_c, qg_c, kg_c, vg_c, rs_c, lr_c, kf_c, vf_c = _inproj(
        xc, mod_ctx, w_a, gain, ind, None, seq_ctx, latent=False)
    attn_c = _attention(q_c, k_c, v_c, None, seq_ctx)
    zero_state = jnp.zeros((1, 2, 2, LANES, LANES), F32)
    gla_c, sfin_c = _gla(qg_c, kg_c, vg_c, lr_c, rs_c, zero_state, wg, bg, gn, seq_ctx)

    q_l, k_l, v_l, qg_l, kg_l, vg_l, rs_l, lr_l = _inproj(
        xl, mod_lat, w_a, gain, ind, _rope_tables(seq_lat), seq_lat, latent=True)
    past = cache_k.shape[2]
    kc = cache_k[:, l].reshape(b_lat, past, KVW).astype(BF16)
    vc = cache_v[:, l].reshape(b_lat, past, KVW).astype(BF16)
    attn_l = _attention(q_l, k_l, v_l, (kc, vc), seq_lat)
    s0 = jnp.stack([_pair_states(state_gla_fwd[:, l]), _pair_states(state_gla_bwd[:, l])], axis=1)
    gla_l, _ = _gla(qg_l, kg_l, vg_l, lr_l, rs_l, s0, wg, bg, gn, seq_lat)

    x1, h2p, rt, ert, counts = _post(xc, xl, attn_c, attn_l, gla_c, gla_l, mod_all, seq_lat,
                                w_mg, w_ba, w_bg, w_o, ln1_g[l][None, :], ln1_b[l][None, :], w_r, b_r)

    pos0, pos1, starts, n_win = _route_tables(ert, counts)
    xs = _scatter_rows(h2p, pos0, pos1)
    ys = _experts(xs, starts, n_win, exp_w_gate[l], exp_w_up[l], exp_w_down[l])
    y_ctx, y_lat = _final(x1, rt, pos0, pos1, mod_all, ln2_g[l][None, :], ln2_b[l][None, :], ys,
                          t_ctx, seq_lat)

    untranspose = lambda a: a.reshape(b_ctx, 1, N_KV_HEADS, HD, seq_ctx).transpose(0, 1, 4, 2, 3)
    new_k, new_v = untranspose(kf_c), untranspose(vf_c)
    new_sf = _unpair_states(sfin_c[:, 0])[:, None]
    new_sb = _unpair_states(sfin_c[:, 1])[:, None]
    return (y_ctx.reshape(b_ctx, seq_ctx, D), y_lat.reshape(b_lat, seq_lat, D),
            new_k, new_v, new_sf, new_sb)
```

```python
import functools

import numpy as np
import jax
import jax.numpy as jnp
from jax import lax
from jax.experimental import pallas as pl
from jax.experimental.pallas import tpu as pltpu

F32 = jnp.float32
BF16 = jnp.bfloat16

D = 1024
GRID_W = 64
HD = 64
N_Q_HEADS = 8
N_KV_HEADS = 2
AW = N_Q_HEADS * HD
KVW = N_KV_HEADS * HD
ROPE_THETA = 10000.0
GLA_H = 4
GLA_DK = 64
GLA_DV = 128
GKW = GLA_H * GLA_DK
GVW = GLA_H * GLA_DV
GATE_RANK = 16
GLA_TAU = 16.0
CHUNK = 64
N_GROUPS = 4
EPG = 8
N_EXP = N_GROUPS * EPG
D_EXP = 256
DEPTH = 1
ALPHA = (2.0 * DEPTH) ** 0.25
LN_EPS = 1e-6
RMS_EPS = 1e-6

LANES = 128
A_WIDTH = AW + 2 * KVW + 2 * GKW + 2 * GVW + 2 * GATE_RANK
MG_WIDTH = 2 * D
TM_TOK = 512
ROW_GROUP = 128
PROJ_AHEAD = 1
TM_PROJ = 1024
TM_POST = 512
LOG2E = 1.4426950408889634
ONES_ROWS = 16
TQ_LAT = 256
SCORE_AHEAD = 2
TM_EXP = 512
READ_AHEAD = 3
TS_ROWS = 1024
SCATTER_RING = 4
VMEM_LIMIT = 56 * 1024 * 1024


def _cparams(sem):
    return pltpu.CompilerParams(dimension_semantics=sem, vmem_limit_bytes=VMEM_LIMIT)


def _dot(a, b):
    return jnp.dot(a, b, preferred_element_type=F32)


def _dot_nt(a, b):
    return lax.dot_general(a, b, (((1,), (1,)), ((), ())), preferred_element_type=F32)


def _dot_tn(a, b):
    return lax.dot_general(a, b, (((0,), (0,)), ((), ())), preferred_element_type=F32)


def _ln(x):
    mu = jnp.mean(x, axis=-1, keepdims=True)
    xc = x - mu
    var = jnp.mean(xc * xc, axis=-1, keepdims=True)
    return xc * lax.rsqrt(var + LN_EPS)


def _silu(x):
    return x * jax.nn.sigmoid(x)


def _split_bf16(x):
    hi = x.astype(BF16)
    lo = (x - hi.astype(F32)).astype(BF16)
    return hi, lo


def _split3_bf16(x):
    hi = x.astype(BF16)
    r1 = x - hi.astype(F32)
    mid = r1.astype(BF16)
    lo = (r1 - mid.astype(F32)).astype(BF16)
    return hi, mid, lo


def _ada_kernel(c_ref, w_ref, b_ref, o_ref):
    rows = c_ref.shape[0]
    s_hi, s_mid, s_lo = _split3_bf16(_silu(c_ref[...]))
    w_hi, w_lo = _split_bf16(w_ref[...])
    a = _dot(jnp.concatenate([s_hi, s_mid, s_lo], axis=0), w_hi)
    b = _dot(jnp.concatenate([s_hi, s_mid], axis=0), w_lo)
    o_ref[...] = (a[0:rows] + a[rows:2 * rows] + a[2 * rows:3 * rows]
                  + b[0:rows] + b[rows:2 * rows] + b_ref[...])


def _ada(c_rows, w_ada, b_ada):
    rows = c_rows.shape[0]
    n = w_ada.shape[1]
    bn = 1024
    return pl.pallas_call(
        _ada_kernel,
        grid=(n // bn,),
        in_specs=[pl.BlockSpec((rows, D), lambda j: (0, 0)),
                  pl.BlockSpec((D, bn), lambda j: (0, j)),
                  pl.BlockSpec((1, bn), lambda j: (0, j))],
        out_specs=pl.BlockSpec((rows, bn), lambda j: (0, j)),
        out_shape=jax.ShapeDtypeStruct((rows, n), F32),
        compiler_params=_cparams(("arbitrary",)),
        name="ada",
    )(c_rows, w_ada, b_ada)


def _inproj_kernel(*refs, latent, seq):
    if latent:
        (x_ref, mod_ref, w_ref, gain_ref, ind_ref, cos_ref, sin_ref,
         q_ref, k_ref, v_ref, qg_ref, kg_ref, vg_ref, rs_ref, lr_ref) = refs
    else:
        (x_ref, mod_ref, w_ref, gain_ref, ind_ref,
         q_ref, k_ref, v_ref, qg_ref, kg_ref, vg_ref, rs_ref, lr_ref, kf_ref, vf_ref) = refs
    tm = x_ref.shape[0]
    sub = ROW_GROUP
    n_groups = tm // sub
    sh1 = mod_ref[0, :, 0:D]
    sc1 = mod_ref[0, :, D:2 * D]
    lane = lax.broadcasted_iota(jnp.int32, (sub, LANES), 1)
    low_half = lane < HD
    first = (lane % 32) < 16

    def project(g):
        rows = slice(g * sub, (g + 1) * sub)
        h = (_ln(x_ref[rows, :]) * (1.0 + sc1) + sh1).astype(BF16)
        return _dot(h, w_ref[...])

    def finish(g, res):
        rows = slice(g * sub, (g + 1) * sub)
        qk = res[:, 0:AW + KVW]
        hi, lo = _split_bf16(qk * qk)
        ms = _dot(hi, ind_ref[...]) + _dot(lo, ind_ref[...])
        r = lax.rsqrt(ms + RMS_EPS)
        for s in range(5):
            rb = jnp.where(low_half, r[:, 2 * s:2 * s + 1], r[:, 2 * s + 1:2 * s + 2])
            y = res[:, LANES * s:LANES * (s + 1)] * rb * gain_ref[:, LANES * s:LANES * (s + 1)]
            if s == 4 and not latent:
                kf_ref[(g * sub) // seq, :, (g * sub) % seq:(g * sub) % seq + sub] = y.T
            if latent:
                partner = jnp.where(first, pltpu.roll(y, LANES - 16, 1), pltpu.roll(y, 16, 1))
                y = y * cos_ref[rows, :] + partner * sin_ref[rows, :]
            if s < 4:
                q_ref[rows, LANES * s:LANES * (s + 1)] = (y * (HD ** -0.5 * LOG2E)).astype(BF16)
            else:
                k_ref[rows, :] = y.astype(BF16)
        o = AW + KVW
        v = res[:, o:o + KVW]
        v_ref[rows, :] = v.astype(BF16)
        if not latent:
            vf_ref[(g * sub) // seq, :, (g * sub) % seq:(g * sub) % seq + sub] = v.T
        o += KVW
        qg_ref[rows, :] = res[:, o:o + GKW] * (GLA_DK ** -0.5)
        o += GKW
        kg_ref[rows, :] = res[:, o:o + GKW]
        o += GKW
        vg_ref[rows, :] = res[:, o:o + GVW].astype(BF16)
        o += GVW
        rs_ref[rows, :] = _silu(res[:, o:o + GVW]).astype(BF16)
        o += GVW
        lr_ref[rows, :] = res[:, o:o + 2 * GATE_RANK]

    ahead = min(PROJ_AHEAD, n_groups)
    pending = {g: project(g) for g in range(ahead)}
    for g in range(n_groups):
        if g + ahead < n_groups:
            pending[g + ahead] = project(g + ahead)
        finish(g, pending.pop(g))


def _inproj(x2, mod, w_a, gain, ind, rope, seq, latent):
    t = x2.shape[0]
    tm = TM_PROJ
    per_seq = max(seq // tm, 1)
    per_tile = max(tm // seq, 1)
    row = lambda i: (i, 0)
    const = lambda i: (0, 0)
    in_specs = [pl.BlockSpec((tm, D), row),
                pl.BlockSpec((1, 1, 6 * D), (lambda i: (i // per_seq, 0, 0)) if latent else (lambda i: (0, 0, 0))),
                pl.BlockSpec((D, A_WIDTH), const),
                pl.BlockSpec((1, AW + KVW), const),
                pl.BlockSpec((AW + KVW, LANES), const)]
    args = [x2, mod, w_a, gain, ind]
    if latent:
        in_specs += [pl.BlockSpec((tm, LANES), lambda i: (i % per_seq, 0))] * 2
        args += list(rope)
    widths = [(AW, BF16), (KVW, BF16), (KVW, BF16), (GKW, F32), (GKW, F32), (GVW, BF16), (GVW, BF16),
              (2 * GATE_RANK, F32)]
    out_specs = [pl.BlockSpec((tm, w), row) for w, _ in widths]
    out_shape = [jax.ShapeDtypeStruct((t, w), dt) for w, dt in widths]
    if not latent:
        cache_spec = pl.BlockSpec((per_tile, KVW, seq), lambda i: (i, 0, 0))
        out_specs += [cache_spec] * 2
        out_shape += [jax.ShapeDtypeStruct((t // seq, KVW, seq), F32)] * 2
    return pl.pallas_call(
        functools.partial(_inproj_kernel, latent=latent, seq=seq),
        grid=(t // tm,),
        in_specs=in_specs,
        out_specs=out_specs,
        out_shape=out_shape,
        compiler_params=_cparams(("parallel",)),
        name="inproj_lat" if latent else "inproj_ctx",
    )(*args)


def _attn_kernel(*refs, has_cache):
    def transposed_with_ones(dst, src):
        dst[0:KVW, :] = src[...].astype(F32).T.astype(BF16)
        dst[KVW:, :] = jnp.ones((ONES_ROWS, dst.shape[1]), BF16)

    if has_cache:
        q_ref, k_ref, v_ref, kc_ref, vc_ref, o_ref, vt_s, vct_s = refs

        @pl.when(pl.program_id(1) == 0)
        def _():
            transposed_with_ones(vt_s, v_ref)
            transposed_with_ones(vct_s, vc_ref)
    else:
        q_ref, k_ref, v_ref, o_ref, vt_s = refs
        transposed_with_ones(vt_s, v_ref)
    tq = q_ref.shape[0]

    lane = lax.broadcasted_iota(jnp.int32, (tq, LANES), 1)
    low_half = lane < HD
    k = k_ref[...]
    def score(c):
        j, pair = divmod(c, 2)
        keep = low_half if j == 0 else jnp.logical_not(low_half)
        zero = jnp.zeros((tq, LANES), BF16)
        qs = jnp.concatenate([jnp.where(keep, q_ref[:, LANES * s:LANES * (s + 1)], zero)
                              for s in (2 * pair, 2 * pair + 1)], axis=0)
        return _dot_nt(k, qs), (_dot_nt(kc_ref[...], qs) if has_cache else None)

    n_chains = 2 * N_KV_HEADS
    ahead = SCORE_AHEAD if has_cache else n_chains
    scores = {c: score(c) for c in range(ahead)}
    outs = []
    for c in range(n_chains):
        if c + ahead < n_chains:
            scores[c + ahead] = score(c + ahead)
        s1, s2 = scores.pop(c)
        m = jnp.max(s1, axis=0, keepdims=True)
        if has_cache:
            m = jnp.maximum(m, jnp.max(s2, axis=0, keepdims=True))
        acc = _dot(vt_s[...], jnp.exp2(s1 - m).astype(BF16))
        if has_cache:
            acc = acc + _dot(vct_s[...], jnp.exp2(s2 - m).astype(BF16))
        outs.append(acc[0:KVW] / acc[KVW:KVW + 1])
    head0 = jnp.concatenate(outs[0:2], axis=1)
    head1 = jnp.concatenate(outs[2:4], axis=1)
    row = lax.broadcasted_iota(jnp.int32, (LANES, 4 * tq), 0)
    out = jnp.where(row < HD, head0, head1).T
    for s in range(4):
        o_ref[:, LANES * s:LANES * (s + 1)] = out[s * tq:(s + 1) * tq].astype(BF16)


def _attention(q, k, v, cache, seq):
    t = q.shape[0]
    if cache is None:
        tq = seq
        grid = (t // seq,)
        qmap = lambda b: (b, 0)
        in_specs = [pl.BlockSpec((tq, AW), qmap), pl.BlockSpec((seq, KVW), qmap),
                    pl.BlockSpec((seq, KVW), qmap)]
        args = [q, k, v]
        scratch = [pltpu.VMEM((KVW + ONES_ROWS, seq), BF16)]
        sem = ("parallel",)
        name = "attn_ctx"
    else:
        tq = TQ_LAT
        nq = seq // tq
        kc, vc = cache
        past = kc.shape[1]
        grid = (t // seq, nq)
        qmap = lambda b, i: (b * nq + i, 0)
        kmap = lambda b, i: (b, 0)
        cmap = lambda b, i: (b, 0, 0)
        in_specs = [pl.BlockSpec((tq, AW), qmap), pl.BlockSpec((seq, KVW), kmap),
                    pl.BlockSpec((seq, KVW), kmap),
                    pl.BlockSpec((None, past, KVW), cmap), pl.BlockSpec((None, past, KVW), cmap)]
        args = [q, k, v, kc, vc]
        scratch = [pltpu.VMEM((KVW + ONES_ROWS, seq), BF16), pltpu.VMEM((KVW + ONES_ROWS, past), BF16)]
        sem = ("parallel", "arbitrary")
        name = "attn_lat"
    return pl.pallas_call(
        functools.partial(_attn_kernel, has_cache=cache is not None),
        grid=grid,
        in_specs=in_specs,
        out_specs=pl.BlockSpec((tq, AW), qmap),
        out_shape=jax.ShapeDtypeStruct((t, AW), BF16),
        scratch_shapes=scratch,
        compiler_params=_cparams(sem),
        name=name,
    )(*args)


GLA_BLK = 256
GLA_ROWS = 2048
GLA_UNROLL = 8


def _gla_kernel(qg_ref, kg_ref, vg_ref, lr_ref, rs_ref, s0_ref, wg_ref, bg_ref, gn_ref,
                o_ref, sfin_ref, cum_s, kv_s, dec_s, *, seqs):
    n = qg_ref.shape[0]
    nc = n // CHUNK
    nc_seq = nc // seqs
    unroll = min(GLA_UNROLL, nc)
    lane = lax.broadcasted_iota(jnp.int32, (CHUNK, LANES), 1)
    low_half = lane < GLA_DK
    lane_sq = lax.broadcasted_iota(jnp.int32, (LANES, LANES), 1)
    low_half_sq = lane_sq < GLA_DK
    ri = lax.broadcasted_iota(jnp.int32, (2 * CHUNK, 2 * CHUNK), 0)
    ci = lax.broadcasted_iota(jnp.int32, (2 * CHUNK, 2 * CHUNK), 1)
    diag = (ri >> 6) == (ci >> 6)
    keep2 = (diag & (ri >= ci), diag & (ci >= ri))

    rb = lax.broadcasted_iota(jnp.int32, (GLA_BLK, GLA_BLK), 0)
    cb = lax.broadcasted_iota(jnp.int32, (GLA_BLK, GLA_BLK), 1)
    same = (rb >> 6) == (cb >> 6)
    tri = ((same & (rb >= cb)).astype(BF16), (same & (cb >= rb)).astype(BF16))
    for d in range(2):
        w_hi, w_lo = _split_bf16(wg_ref[d])
        for blk in range(n // GLA_BLK):
            rows = slice(blk * GLA_BLK, (blk + 1) * GLA_BLK)
            l_hi, l_lo = _split_bf16(lr_ref[rows, :])
            z = _dot(l_hi, w_hi) + _dot(l_lo, w_hi) + _dot(l_hi, w_lo) + bg_ref[d]
            logg = (jnp.minimum(z, 0.0) - jnp.log(1.0 + jnp.exp(-jnp.abs(z)))) * (1.0 / GLA_TAU)
            pieces = _split3_bf16(logg)
            cum_s[d, rows, :] = _dot(tri[d], pieces[0]) + _dot(tri[d], pieces[1]) + _dot(tri[d], pieces[2])

    def pass1(i, carry):
        cs = [i * unroll + u for u in range(unroll)]
        rws = [pl.ds(pl.multiple_of(c * CHUNK, CHUNK), CHUNK) for c in cs]
        prods = []
        for c, rows in zip(cs, rws):
            kc = kg_ref[rows, :]
            kdec = []
            for d in range(2):
                cum = cum_s[d, rows, :]
                last = cum[CHUNK - 1:CHUNK, :] if d == 0 else cum[0:1, :]
                dec_s[d, c] = jnp.exp(last)
                kdec.append((kc * jnp.exp(last - cum)).astype(BF16))
            for p in range(2):
                vpair = vg_ref[rows, GLA_DV * 2 * p:GLA_DV * 2 * (p + 1)]
                kpair = jnp.concatenate([kdec[0][:, LANES * p:LANES * (p + 1)],
                                         kdec[1][:, LANES * p:LANES * (p + 1)]], axis=1)
                prods.append(_dot_tn(vpair, kpair))
        for j, c in enumerate(cs):
            for p in range(2):
                res = prods[2 * j + p]
                for d in range(2):
                    cols = slice(LANES * d, LANES * (d + 1))
                    kv_s[d, c, p] = jnp.where(low_half_sq, res[0:GLA_DV, cols], res[GLA_DV:2 * GLA_DV, cols])
        return carry

    lax.fori_loop(0, nc // unroll, pass1, 0)

    for s in range(seqs):
        for d in range(2):
            def scan(i, st, s=s, d=d):
                c = s * nc_seq + (i if d == 0 else nc_seq - 1 - i)
                dec = dec_s[d, c]
                new = []
                for p in range(2):
                    kv = kv_s[d, c, p]
                    kv_s[d, c, p] = st[p]
                    new.append(st[p] * dec[:, LANES * p:LANES * (p + 1)] + kv)
                return tuple(new)

            s_in = s if s0_ref.shape[0] > 1 else 0
            fin = lax.fori_loop(0, nc_seq, scan, (s0_ref[s_in, d, 0], s0_ref[s_in, d, 1]))
            sfin_ref[s, d, 0] = fin[0]
            sfin_ref[s, d, 1] = fin[1]

    def pass3(i, carry):
        cs = [i * unroll + u for u in range(unroll)]
        rws = [pl.ds(pl.multiple_of(c * CHUNK, CHUNK), CHUNK) for c in cs]
        first = []
        for c, rows in zip(cs, rws):
            q = qg_ref[rows, :]
            k = kg_ref[rows, :]
            for d in range(2):
                cum = cum_s[d, rows, :]
                qt = q * jnp.exp(cum)
                kt = (k * jnp.exp(-cum)).astype(BF16)
                for p in range(2):
                    qs = qt[:, LANES * p:LANES * (p + 1)]
                    lhs = jnp.concatenate([jnp.where(low_half, qs, 0.0), jnp.where(low_half, 0.0, qs)],
                                          axis=0).astype(BF16)
                    kts = kt[:, LANES * p:LANES * (p + 1)]
                    rhs = jnp.concatenate([kts, kts, kv_s[d, c, p].astype(BF16)], axis=0)
                    first.append(_dot_nt(lhs, rhs))
        second = []
        for j, rows in enumerate(rws):
            for d in range(2):
                for p in range(2):
                    res = first[4 * j + 2 * d + p]
                    vp = jnp.concatenate([vg_ref[rows, GLA_DV * (2 * p):GLA_DV * (2 * p + 1)],
                                          vg_ref[rows, GLA_DV * (2 * p + 1):GLA_DV * (2 * p + 2)]], axis=0)
                    a = jnp.where(keep2[d], res[:, 0:2 * CHUNK], 0.0).astype(BF16)
                    second.append(_dot(a, vp) + res[:, 2 * CHUNK:])
        for j, rows in enumerate(rws):
            for p in range(2):
                tot = second[4 * j + p] + second[4 * j + 2 + p]
                y = tot * lax.rsqrt(jnp.mean(tot * tot, axis=-1, keepdims=True) + RMS_EPS) * gn_ref[...]
                for hh in range(2):
                    cols = slice(GLA_DV * (2 * p + hh), GLA_DV * (2 * p + hh + 1))
                    o_ref[rows, cols] = (y[CHUNK * hh:CHUNK * (hh + 1)]
                                         * rs_ref[rows, cols].astype(F32)).astype(BF16)
        return carry

    lax.fori_loop(0, nc // unroll, pass3, 0)


def _gla(qg, kg, vg, lr, rs, s0, wg, bg, gn, seq):
    t = qg.shape[0]
    nb = t // seq
    seqs = max(GLA_ROWS // seq, 1)
    rows = seqs * seq
    nc = rows // CHUNK
    row = lambda b: (b, 0)
    c3 = lambda b: (0, 0, 0)
    st = lambda b: (b, 0, 0, 0, 0)
    if s0.shape[0] == 1:
        s0_spec = pl.BlockSpec((1, 2, 2, LANES, LANES), lambda b: (0, 0, 0, 0, 0))
    else:
        s0_spec = pl.BlockSpec((seqs, 2, 2, LANES, LANES), st)
    return pl.pallas_call(
        functools.partial(_gla_kernel, seqs=seqs),
        grid=(nb // seqs,),
        in_specs=[pl.BlockSpec((rows, GKW), row), pl.BlockSpec((rows, GKW), row),
                  pl.BlockSpec((rows, GVW), row), pl.BlockSpec((rows, 2 * GATE_RANK), row),
                  pl.BlockSpec((rows, GVW), row),
                  s0_spec,
                  pl.BlockSpec((2, 2 * GATE_RANK, GKW), c3), pl.BlockSpec((2, 1, GKW), c3),
                  pl.BlockSpec((1, GLA_DV), lambda b: (0, 0))],
        out_specs=[pl.BlockSpec((rows, GVW), row),
                   pl.BlockSpec((seqs, 2, 2, LANES, LANES), st)],
        out_shape=[jax.ShapeDtypeStruct((t, GVW), BF16),
                   jax.ShapeDtypeStruct((nb, 2, 2, LANES, LANES), F32)],
        scratch_shapes=[pltpu.VMEM((2, rows, GKW), F32),
                        pltpu.VMEM((2, nc, 2, LANES, LANES), F32),
                        pltpu.VMEM((2, nc, 1, GKW), F32)],
        compiler_params=_cparams(("parallel",)),
        name="gla_lat" if seq > 256 else "gla_ctx",
    )(qg, kg, vg, lr, rs, s0, wg, bg, gn)


ROW_SUB = D // LANES
ROW_DTYPE = BF16


def _store_row_slabs(ref, x, tmp):
    m = x.shape[0]
    for c in range(ROW_SUB):
        tmp[pl.ds(c, m, stride=ROW_SUB), :] = x[:, LANES * c:LANES * (c + 1)]
    ref[...] = tmp[...].reshape(m, ROW_SUB, LANES).astype(ROW_DTYPE)


def _load_row_slabs(ref, tmp):
    m = ref.shape[0]
    tmp[...] = ref[...].astype(F32).reshape(m * ROW_SUB, LANES)
    return jnp.concatenate([tmp[pl.ds(c, m, stride=ROW_SUB), :] for c in range(ROW_SUB)], axis=1)


def _row_slab(ref, row):
    return ref.at[pl.ds(row, 1)]


def _post_kernel(xc_ref, xl_ref, ac_ref, al_ref, gc_ref, gl_ref, mod_ref,
                 wmg_ref, wba_ref, wbg_ref, wo_ref, l1g_ref, l1b_ref, wr_ref, br_ref,
                 x1_ref, h2_ref, rt_ref, ert_ref, cnt_ref, run_s, slab_s, logit_s, *, n_ctx_tiles):
    i = pl.program_id(0)
    tm = xc_ref.shape[0]
    sub = ROW_GROUP
    n_groups = tm // sub
    is_ctx = i < n_ctx_tiles
    parts = [slice(g * sub, (g + 1) * sub) for g in range(n_groups)]

    @pl.when(i == 0)
    def _():
        run_s[...] = jnp.zeros_like(run_s)
        logit_s[...] = jnp.zeros_like(logit_s)

    sh1 = mod_ref[0, :, 0:D]
    sc1 = mod_ref[0, :, D:2 * D]
    g1 = mod_ref[0, :, 2 * D:3 * D]
    sh2 = mod_ref[0, :, 3 * D:4 * D]
    sc2 = mod_ref[0, :, 4 * D:5 * D]

    xs = [jnp.where(is_ctx, xc_ref[r, :], xl_ref[r, :]) for r in parts]
    hs = [(_ln(x) * (1.0 + sc1) + sh1).astype(BF16) for x in xs]
    gates = [jax.nn.sigmoid(_dot(h, wmg_ref[...])) for h in hs]
    ba = [_dot(jnp.where(is_ctx, ac_ref[r, :], al_ref[r, :]), wba_ref[...]) for r in parts]
    bg = [_dot(jnp.where(is_ctx, gc_ref[r, :], gl_ref[r, :]), wbg_ref[...]) for r in parts]

    ri = lax.broadcasted_iota(jnp.int32, (sub, sub), 0)
    ci = lax.broadcasted_iota(jnp.int32, (sub, sub), 1)
    earlier = (ri > ci).astype(BF16)
    counted = (i > 0).astype(F32)
    run = run_s[0:1, :]
    for r in parts:
        run = _route(logit_s[r, :], earlier, run, counted, rt_ref.at[r, :], ert_ref.at[:, r])
    run_s[0:1, :] = run
    cnt_ref[...] = jnp.broadcast_to(run, cnt_ref.shape)

    merged = [(g[:, :D] * a + g[:, D:] * b).astype(BF16) for g, a, b in zip(gates, ba, bg)]
    mix = [_dot(m, wo_ref[...]) for m in merged]
    x1s = [_ln(ALPHA * x + g1 * m) * l1g_ref[...] + l1b_ref[...] for x, m in zip(xs, mix)]
    h2s = [_ln(x1) * (1.0 + sc2) + sh2 for x1 in x1s]
    logits = [_dot(h2.astype(BF16), wr_ref[...]) + br_ref[...] for h2 in h2s]
    for g, r in enumerate(parts):
        x1_ref[r, :] = x1s[g]
        _store_row_slabs(h2_ref.at[pl.ds(g * sub, sub)], h2s[g], slab_s)
        logit_s[r, :] = logits[g]


def _route(logit, earlier, run, counted, rt_ref, ert_ref):
    tm = logit.shape[0]
    lane_i = lax.broadcasted_iota(jnp.int32, (tm, LANES), 1)
    lane = lane_i.astype(F32)
    lane_grp = ((lane_i - N_GROUPS) >> 3).astype(F32)
    neg = jnp.float32(-jnp.inf)
    far = jnp.float32(LANES)
    is_g = lane_i < N_GROUPS
    lg = jnp.where(is_g, logit, neg)
    mg = jnp.max(lg, axis=-1, keepdims=True)
    pg_top = 1.0 / jnp.sum(jnp.where(is_g, jnp.exp(logit - mg), 0.0), axis=-1, keepdims=True)
    g_idx = jnp.min(jnp.where(lg == mg, lane, far), axis=-1, keepdims=True)
    in_grp = (lane_i >= N_GROUPS) & (lane_i < N_GROUPS + N_EXP) & (lane_grp == g_idx)
    le = jnp.where(in_grp, logit, neg)
    v1 = jnp.max(le, axis=-1, keepdims=True)
    i1 = jnp.min(jnp.where(le == v1, lane, far), axis=-1, keepdims=True)
    le2 = jnp.where(lane == i1, neg, le)
    v2 = jnp.max(le2, axis=-1, keepdims=True)
    i2 = jnp.min(jnp.where(le2 == v2, lane, far), axis=-1, keepdims=True)
    e1 = i1 - N_GROUPS
    e2 = i2 - N_GROUPS
    tt = jnp.exp(v2 - v1)
    w1 = pg_top / (1.0 + tt)
    w2 = pg_top * tt / (1.0 + tt)

    hot = ((lane == e1) | (lane == e2)).astype(F32)
    before = _dot(earlier, hot.astype(BF16)) + run
    r1 = jnp.sum(jnp.where(lane == e1, before, 0.0), axis=-1, keepdims=True)
    r2 = jnp.sum(jnp.where(lane == e2, before, 0.0), axis=-1, keepdims=True)

    rt = jnp.where(lane_i == 0, e1, 0.0)
    rt = jnp.where(lane_i == 1, e2, rt)
    rt = jnp.where(lane_i == 2, w1, rt)
    rt = jnp.where(lane_i == 3, w2, rt)
    rt = jnp.where(lane_i == 4, r1, rt)
    rt = jnp.where(lane_i == 5, r2, rt)
    rt_ref[...] = rt
    ert_ref[...] = rt.T[0:8, :]
    return run + counted * jnp.sum(hot, axis=0, keepdims=True)


def _post(x_ctx, x_lat, a_ctx, a_lat, g_ctx, g_lat, mod_all, seq_lat,
          w_mg, w_ba, w_bg, w_o, l1g, l1b, w_r, b_r):
    t_ctx, t_lat = x_ctx.shape[0], x_lat.shape[0]
    tm = TM_POST
    nct, nlt = t_ctx // tm, t_lat // tm
    per_seq = seq_lat // tm
    nb_lat = t_lat // seq_lat
    t = t_ctx + t_lat
    n = nct + nlt
    lat = lambda i: jnp.clip(i - nct, 0, nlt - 1)
    cmap = lambda i: (jnp.minimum(i, nct - 1), 0)
    lmap = lambda i: (lat(i), 0)
    mmap = lambda i: (jnp.where(i < nct, nb_lat, lat(i) // per_seq), 0, 0)
    row = lambda i: (jnp.minimum(i, n - 1), 0)
    prev = lambda i: jnp.maximum(i - 1, 0)
    const = lambda i: (0, 0)
    return pl.pallas_call(
        functools.partial(_post_kernel, n_ctx_tiles=nct),
        grid=(n + 1,),
        in_specs=[pl.BlockSpec((tm, D), cmap), pl.BlockSpec((tm, D), lmap),
                  pl.BlockSpec((tm, AW), cmap), pl.BlockSpec((tm, AW), lmap),
                  pl.BlockSpec((tm, GVW), cmap), pl.BlockSpec((tm, GVW), lmap),
                  pl.BlockSpec((1, 1, 6 * D), mmap),
                  pl.BlockSpec((D, MG_WIDTH), const), pl.BlockSpec((AW, D), const),
                  pl.BlockSpec((GVW, D), const), pl.BlockSpec((D, D), const),
                  pl.BlockSpec((1, D), const), pl.BlockSpec((1, D), const),
                  pl.BlockSpec((D, LANES), const), pl.BlockSpec((1, LANES), const)],
        out_specs=[pl.BlockSpec((tm, D), row),
                   pl.BlockSpec((tm, ROW_SUB, LANES), lambda i: (jnp.minimum(i, n - 1), 0, 0)),
                   pl.BlockSpec((tm, LANES), lambda i: (prev(i), 0)),
                   pl.BlockSpec((8, tm), lambda i: (0, prev(i))),
                   pl.BlockSpec((8, LANES), const)],
        out_shape=[jax.ShapeDtypeStruct((t, D), F32), jax.ShapeDtypeStruct((t, ROW_SUB, LANES), ROW_DTYPE),
                   jax.ShapeDtypeStruct((t, LANES), F32), jax.ShapeDtypeStruct((8, t), F32),
                   jax.ShapeDtypeStruct((8, LANES), F32)],
        scratch_shapes=[pltpu.VMEM((8, LANES), F32), pltpu.VMEM((ROW_GROUP * ROW_SUB, LANES), F32),
                        pltpu.VMEM((tm, LANES), F32)],
        compiler_params=_cparams(("arbitrary",)),
        name="post",
    )(x_ctx, x_lat, a_ctx, a_lat, g_ctx, g_lat, mod_all, w_mg, w_ba, w_bg, w_o, l1g, l1b, w_r, b_r)


ROW_UNROLL = 8


def _row_copy(src_ref, dst_ref, sem):
    return pltpu.make_async_copy(src_ref, dst_ref, sem)


def _scatter_kernel(pos0_ref, pos1_ref, h_ref, xs_ref, ring, in_sem, out_sem, zero_s, zero_sem):
    i = pl.program_id(0)
    n = pl.num_programs(0)
    ts = ring.shape[1]
    n_rows = xs_ref.shape[0] - TM_EXP

    def read(tile, slot):
        return pltpu.make_async_copy(h_ref.at[pl.ds(tile * ts, ts)], ring.at[slot], in_sem.at[slot])

    def drain(parity):
        for _ in range(2):
            _row_copy(ring.at[0], xs_ref.at[pl.ds(0, ts)], out_sem.at[parity]).wait()

    @pl.when(i == 0)
    def _():
        zero_s[...] = jnp.zeros_like(zero_s)
        pad = _row_copy(zero_s, xs_ref.at[pl.ds(n_rows, TM_EXP)], zero_sem)
        pad.start()
        pad.wait()
        read(0, 0).start()

        @pl.when(n > 1)
        def _():
            read(1, 1).start()

    @pl.when(i + 2 < n)
    def _():
        read(i + 2, (i + 2) % SCATTER_RING).start()

    slot = i % SCATTER_RING
    read(i, slot).wait()
    src_tile = ring.at[slot]
    sem = out_sem.at[i % 2]

    def issue(g, carry):
        r0 = pl.multiple_of(g * ROW_UNROLL, ROW_UNROLL)
        for k in range(ROW_UNROLL):
            src = _row_slab(src_tile, r0 + k)
            _row_copy(src, _row_slab(xs_ref, pos0_ref[0, r0 + k]), sem).start(priority=0)
            _row_copy(src, _row_slab(xs_ref, pos1_ref[0, r0 + k]), sem).start(priority=1)
        return carry

    lax.fori_loop(0, ts // ROW_UNROLL, issue, 0)

    @pl.when(i > 0)
    def _():
        drain((i - 1) % 2)

    @pl.when(i == n - 1)
    def _():
        drain(i % 2)


def _scatter_rows(h2p, pos0, pos1):
    t = h2p.shape[0]
    ts = TS_ROWS
    smem = lambda: pl.BlockSpec((None, 1, ts), lambda i: (i, 0, 0), memory_space=pltpu.SMEM)
    return pl.pallas_call(
        _scatter_kernel,
        grid=(t // ts,),
        in_specs=[smem(), smem(), pl.BlockSpec(memory_space=pl.ANY)],
        out_specs=pl.BlockSpec(memory_space=pl.ANY),
        out_shape=jax.ShapeDtypeStruct((2 * t + TM_EXP, ROW_SUB, LANES), ROW_DTYPE),
        scratch_shapes=[pltpu.VMEM((SCATTER_RING, ts, ROW_SUB, LANES), ROW_DTYPE),
                        pltpu.SemaphoreType.DMA((SCATTER_RING,)), pltpu.SemaphoreType.DMA((2,)),
                        pltpu.VMEM((TM_EXP, ROW_SUB, LANES), ROW_DTYPE),
                        pltpu.SemaphoreType.DMA(())],
        compiler_params=_cparams(("arbitrary",)),
        name="scatter",
    )(pos0.reshape(t // ts, 1, ts), pos1.reshape(t // ts, 1, ts), h2p)


def _expert_kernel(start_ref, nwin_ref, xs_ref, wg_ref, wu_ref, wd_ref, ys_ref,
                   wgu_s, wd_s, in_buf, out_buf, slab_s, in_sem, out_sem):
    e = pl.program_id(0)
    tm = TM_EXP
    wgu_s[:, 0:D_EXP] = wg_ref[...].astype(BF16)
    wgu_s[:, D_EXP:2 * D_EXP] = wu_ref[...].astype(BF16)
    wd_s[...] = wd_ref[...].astype(BF16)
    base = start_ref[e]
    n = nwin_ref[e]

    @pl.when(e == 0)
    def _():
        out_buf[0] = jnp.zeros(out_buf.shape[1:], ROW_DTYPE)
        pad = pltpu.make_async_copy(out_buf.at[0], ys_ref.at[pl.ds(ys_ref.shape[0] - tm, tm)], out_sem.at[0])
        pad.start()
        pad.wait()

    def read(w, slot, first_row=None):
        first_row = base if first_row is None else first_row
        return pltpu.make_async_copy(xs_ref.at[pl.ds(first_row + w * tm, tm)], in_buf.at[slot], in_sem.at[slot])

    def write(w, slot):
        return pltpu.make_async_copy(out_buf.at[slot], ys_ref.at[pl.ds(base + w * tm, tm)], out_sem.at[slot])

    def start_first_reads(expert):
        for a in range(READ_AHEAD):
            @pl.when(nwin_ref[expert] > a)
            def _(a=a):
                read(a, a, start_ref[expert]).start()

    @pl.when(e == 0)
    def _():
        start_first_reads(0)

    def body(w, carry):
        slot = w % 2
        rslot = w % (READ_AHEAD + 1)

        @pl.when(w + READ_AHEAD < n)
        def _():
            read(w + READ_AHEAD, (w + READ_AHEAD) % (READ_AHEAD + 1)).start()

        read(w, rslot).wait()

        @pl.when(w >= 2)
        def _():
            write(w - 2, slot).wait()

        gu = _dot(_load_row_slabs(in_buf.at[rslot], slab_s).astype(BF16), wgu_s[...])
        hid = _silu(gu[:, 0:D_EXP]) * gu[:, D_EXP:2 * D_EXP]
        _store_row_slabs(out_buf.at[slot], _dot(hid.astype(BF16), wd_s[...]), slab_s)
        write(w, slot).start()
        return carry

    lax.fori_loop(0, n, body, 0)

    @pl.when(e + 1 < pl.num_programs(0))
    def _():
        start_first_reads(e + 1)

    @pl.when(n >= 1)
    def _():
        write(n - 1, (n - 1) % 2).wait()

    @pl.when(n >= 2)
    def _():
        write(n - 2, n % 2).wait()


def _experts(xs, starts, n_win, w_gate, w_up, w_down):
    tm = TM_EXP
    wmap = lambda e, st, nw: (e, 0, 0)
    slab = (tm, ROW_SUB, LANES)
    return pl.pallas_call(
        _expert_kernel,
        grid_spec=pltpu.PrefetchScalarGridSpec(
            num_scalar_prefetch=2,
            grid=(N_EXP,),
            in_specs=[pl.BlockSpec(memory_space=pl.ANY),
                      pl.BlockSpec((None, D, D_EXP), wmap), pl.BlockSpec((None, D, D_EXP), wmap),
                      pl.BlockSpec((None, D_EXP, D), wmap)],
            out_specs=pl.BlockSpec(memory_space=pl.ANY),
            scratch_shapes=[pltpu.VMEM((D, 2 * D_EXP), BF16), pltpu.VMEM((D_EXP, D), BF16),
                            pltpu.VMEM((READ_AHEAD + 1,) + slab, ROW_DTYPE), pltpu.VMEM((2,) + slab, ROW_DTYPE),
                            pltpu.VMEM((tm * ROW_SUB, LANES), F32),
                            pltpu.SemaphoreType.DMA((READ_AHEAD + 1,)), pltpu.SemaphoreType.DMA((2,))]),
        out_shape=jax.ShapeDtypeStruct(xs.shape, ROW_DTYPE),
        compiler_params=_cparams(("arbitrary",)),
        name="experts",
    )(starts, n_win, xs, w_gate, w_up, w_down)


def _final_kernel(p0c_ref, p1c_ref, p0n_ref, p1n_ref, x1_ref, rt_ref, mod_ref, l2g_ref, l2b_ref, ys_ref,
                  oc_ref, ol_ref, buf, sem, slab_s, *, n_ctx_tiles):
    i = pl.program_id(0)
    n = pl.num_programs(0)
    tm = x1_ref.shape[0]

    def gather(p0_ref, p1_ref, slot):
        def issue(g, carry):
            r0 = pl.multiple_of(g * ROW_UNROLL, ROW_UNROLL)
            for k in range(ROW_UNROLL):
                _row_copy(_row_slab(ys_ref, p0_ref[0, r0 + k]),
                          _row_slab(buf.at[slot, 0], r0 + k), sem.at[slot]).start(priority=0)
                _row_copy(_row_slab(ys_ref, p1_ref[0, r0 + k]),
                          _row_slab(buf.at[slot, 1], r0 + k), sem.at[slot]).start(priority=1)
            return carry

        lax.fori_loop(0, tm // ROW_UNROLL, issue, 0)

    cur = i % 2

    @pl.when(i == 0)
    def _():
        gather(p0c_ref, p1c_ref, 0)

    @pl.when(i + 1 < n)
    def _():
        gather(p0n_ref, p1n_ref, 1 - cur)

    for k in range(2):
        _row_copy(ys_ref.at[pl.ds(0, tm)], buf.at[cur, k], sem.at[cur]).wait()

    g2 = mod_ref[0, :, 5 * D:6 * D]
    w1 = rt_ref[:, 2:3]
    w2 = rt_ref[:, 3:4]
    moe = (w1 * _load_row_slabs(buf.at[cur, 0], slab_s.at[0])
           + w2 * _load_row_slabs(buf.at[cur, 1], slab_s.at[1]))
    out = _ln(ALPHA * x1_ref[...] + g2 * moe) * l2g_ref[...] + l2b_ref[...]

    @pl.when(i < n_ctx_tiles)
    def _():
        oc_ref[...] = out

    @pl.when(i >= n_ctx_tiles)
    def _():
        ol_ref[...] = out


def _final(x1, rt, pos0, pos1, mod_all, l2g, l2b, ys, t_ctx, seq_lat):
    t = x1.shape[0]
    tm = TM_TOK
    nt = t // tm
    nct = t_ctx // tm
    t_lat = t - t_ctx
    per_seq = seq_lat // tm
    nb_lat = t_lat // seq_lat
    p0 = pos0.reshape(nt, 1, tm)
    p1 = pos1.reshape(nt, 1, tm)
    row = lambda i: (i, 0)
    const = lambda i: (0, 0)
    mmap = lambda i: (jnp.where(i < nct, nb_lat, jnp.maximum(i - nct, 0) // per_seq), 0, 0)
    smem_cur = lambda: pl.BlockSpec((None, 1, tm), lambda i: (i, 0, 0), memory_space=pltpu.SMEM)
    smem_nxt = lambda: pl.BlockSpec((None, 1, tm), lambda i: (jnp.minimum(i + 1, nt - 1), 0, 0),
                                    memory_space=pltpu.SMEM)
    return pl.pallas_call(
        functools.partial(_final_kernel, n_ctx_tiles=nct),
        grid=(nt,),
        in_specs=[smem_cur(), smem_cur(), smem_nxt(), smem_nxt(),
                  pl.BlockSpec((tm, D), row), pl.BlockSpec((tm, LANES), row),
                  pl.BlockSpec((1, 1, 6 * D), mmap),
                  pl.BlockSpec((1, D), const), pl.BlockSpec((1, D), const),
                  pl.BlockSpec(memory_space=pl.ANY)],
        out_specs=[pl.BlockSpec((tm, D), lambda i: (jnp.minimum(i, nct - 1), 0)),
                   pl.BlockSpec((tm, D), lambda i: (jnp.maximum(i - nct, 0), 0))],
        out_shape=[jax.ShapeDtypeStruct((t_ctx, D), F32), jax.ShapeDtypeStruct((t_lat, D), F32)],
        scratch_shapes=[pltpu.VMEM((2, 2, tm, ROW_SUB, LANES), ROW_DTYPE), pltpu.SemaphoreType.DMA((2,)),
                        pltpu.VMEM((2, tm * ROW_SUB, LANES), F32)],
        compiler_params=_cparams(("arbitrary",)),
        name="final",
    )(p0, p1, p0, p1, x1, rt, mod_all, l2g, l2b, ys)


def _reorder_q_heads(w, axis):
    shape = w.shape
    split = shape[:axis] + (N_KV_HEADS, N_Q_HEADS // N_KV_HEADS, HD) + shape[axis + 1:]
    return jnp.swapaxes(w.reshape(split), axis, axis + 1).reshape(shape)


def _rope_tables(seq):
    t = np.arange(seq)
    half = HD // 4
    inv = (ROPE_THETA ** (-np.arange(half, dtype=np.float64) / half)).astype(np.float32)
    d64 = np.arange(LANES) % HD
    pos = np.where((d64 < HD // 2)[None, :], (t // GRID_W)[:, None], (t % GRID_W)[:, None])
    ang = (pos.astype(np.float32) * inv[d64 % half][None, :]).astype(np.float64)
    sign = np.where((d64 % 32) < 16, -1.0, 1.0)
    return (jnp.asarray(np.cos(ang), F32), jnp.asarray(np.sin(ang) * sign[None, :], F32))


def _pair_states(s):
    b = s.shape[0]
    s = s.reshape(b, 2, 2, GLA_DK, GLA_DV)
    return s.transpose(0, 1, 4, 2, 3).reshape(b, 2, GLA_DV, 2 * GLA_DK)


def _unpair_states(s):
    b = s.shape[0]
    s = s.reshape(b, 2, GLA_DV, 2, GLA_DK)
    return s.transpose(0, 1, 3, 4, 2).reshape(b, GLA_H, GLA_DK, GLA_DV)


def _route_tables(ert, counts):
    i32 = jnp.int32
    cnt = counts[0, :N_EXP].astype(i32)
    starts = jnp.cumsum(cnt) - cnt
    table = lambda e: jnp.sum(jnp.where(e[None, :] == jnp.arange(N_EXP, dtype=i32)[:, None],
                                        starts[:, None], 0), axis=0)
    pos0 = table(ert[0].astype(i32)) + ert[4].astype(i32)
    pos1 = table(ert[1].astype(i32)) + ert[5].astype(i32)
    return pos0, pos1, starts, (cnt + (TM_EXP - 1)) // TM_EXP


def kernel(x_prompt, x_sample, cache_k, cache_v, state_gla_fwd, state_gla_bwd, c, c_ctx, w_ada, b_ada, w_in, q_norm, k_norm, gla_w_gate, gla_b_gate, gla_norm, w_br_attn, w_br_gla, w_out, ln1_g, ln1_b, router_group_w, router_group_b, router_expert_w, router_expert_b, exp_w_gate, exp_w_up, exp_w_down, ln2_g, ln2_b):
    b_ctx, seq_ctx, _ = x_prompt.shape
    b_lat, seq_lat, _ = x_sample.shape
    t_ctx, t_lat = b_ctx * seq_ctx, b_lat * seq_lat
    t = t_ctx + t_lat
    l = 0

    rows = -(-(b_lat + 1) // 8) * 8
    c_rows = jnp.zeros((rows, D), F32).at[:b_lat].set(c).at[b_lat].set(c_ctx)
    mod = _ada(c_rows, w_ada[l], b_ada[l][None, :])
    mod_all = mod[:b_lat + 1, None, :]
    mod_lat = mod_all[:b_lat]
    mod_ctx = mod_all[b_lat:]

    w_full = w_in[l]
    w_a = jnp.concatenate([_reorder_q_heads(w_full[:, :AW], 1), w_full[:, AW:A_WIDTH]], axis=1).astype(BF16)
    w_mg = w_full[:, A_WIDTH:].astype(BF16)
    gain = jnp.concatenate([jnp.tile(q_norm[l], N_Q_HEADS), jnp.tile(k_norm[l], N_KV_HEADS)])[None, :]
    head_of = np.arange(AW + KVW) // HD
    ind = jnp.asarray((head_of[:, None] == np.arange(LANES)[None, :]) / HD, BF16)
    w_ba = _reorder_q_heads(w_br_attn[l], 0).astype(BF16)
    w_bg = w_br_gla[l].astype(BF16)
    w_o = w_out[l].astype(BF16)
    w_r = jnp.zeros((D, LANES), F32).at[:, :N_GROUPS].set(router_group_w[l])
    w_r = w_r.at[:, N_GROUPS:N_GROUPS + N_EXP].set(router_expert_w[l]).astype(BF16)
    b_r = jnp.zeros((1, LANES), F32).at[0, :N_GROUPS].set(router_group_b[l])
    b_r = b_r.at[0, N_GROUPS:N_GROUPS + N_EXP].set(router_expert_b[l])
    wg = jnp.zeros((2, 2 * GATE_RANK, GKW), F32)
    wg = wg.at[0, :GATE_RANK].set(gla_w_gate[l, 0]).at[1, GATE_RANK:].set(gla_w_gate[l, 1])
    bg = gla_b_gate[l][:, None, :]
    gn = gla_norm[l][None, :]

    xc = x_prompt.reshape(t_ctx, D)
    xl = x_sample.reshape(t_lat, D)

    q_c, k_c, v_c, qg_c, kg_c, vg_c, rs_c, lr_c, kf_c, vf_c = _inproj(
        xc, mod_ctx, w_a, gain, ind, None, seq_ctx, latent=False)
    attn_c = _attention(q_c, k_c, v_c, None, seq_ctx)
    zero_state = jnp.zeros((1, 2, 2, LANES, LANES), F32)
    gla_c, sfin_c = _gla(qg_c, kg_c, vg_c, lr_c, rs_c, zero_state, wg, bg, gn, seq_ctx)

    q_l, k_l, v_l, qg_l, kg_l, vg_l, rs_l, lr_l = _inproj(
        xl, mod_lat, w_a, gain, ind, _rope_tables(seq_lat), seq_lat, latent=True)
    past = cache_k.shape[2]
    kc = cache_k[:, l].reshape(b_lat, past, KVW).astype(BF16)
    vc = cache_v[:, l].reshape(b_lat, past, KVW).astype(BF16)
    attn_l = _attention(q_l, k_l, v_l, (kc, vc), seq_lat)
    s0 = jnp.stack([_pair_states(state_gla_fwd[:, l]), _pair_states(state_gla_bwd[:, l])], axis=1)
    gla_l, _ = _gla(qg_l, kg_l, vg_l, lr_l, rs_l, s0, wg, bg, gn, seq_lat)

    x1, h2p, rt, ert, counts = _post(xc, xl, attn_c, attn_l, gla_c, gla_l, mod_all, seq_lat,
                                w_mg, w_ba, w_bg, w_o, ln1_g[l][None, :], ln1_b[l][None, :], w_r, b_r)

    pos0, pos1, starts, n_win = _route_tables(ert, counts)
    xs = _scatter_rows(h2p, pos0, pos1)
    ys = _experts(xs, starts, n_win, exp_w_gate[l], exp_w_up[l], exp_w_down[l])
    y_ctx, y_lat = _final(x1, rt, pos0, pos1, mod_all, ln2_g[l][None, :], ln2_b[l][None, :], ys,
                          t_ctx, seq_lat)

    untranspose = lambda a: a.reshape(b_ctx, 1, N_KV_HEADS, HD, seq_ctx).transpose(0, 1, 4, 2, 3)
    new_k, new_v = untranspose(kf_c), untranspose(vf_c)
    new_sf = _unpair_states(sfin_c[:, 0])[:, None]
    new_sb = _unpair_states(sfin_c[:, 1])[:, None]
    return (y_ctx.reshape(b_ctx, seq_ctx, D), y_lat.reshape(b_lat, seq_lat, D),
            new_k, new_v, new_sf, new_sb)
```
